```python
import jax, jax.numpy as jnp
from jax import lax
import numpy as np

D_MODEL = 2048
BATCH = 8
SEQ = 4096
DEPTH = 1

D_MIX = D_MODEL
HG_DK = 128
HG_DV = 128
HG_WIDTH = D_MIX // 2
HG_HEADS = HG_WIDTH // HG_DV
HG_QK = HG_HEADS * HG_DK
GDN_DK = 128
GDN_DV = 128
GDN_WIDTH = D_MIX - HG_WIDTH
GDN_HEADS = GDN_WIDTH // GDN_DV
GDN_QK = GDN_HEADS * GDN_DK
CONV_K = 4
CHUNK = 64
D_FF = 4 * D_MODEL
N_MOD = 6
EPS = 1e-6

HG_COLS = 2 * HG_QK + 2 * HG_WIDTH
GDN_CONV_CH = 2 * GDN_QK + GDN_WIDTH
GDN_COLS = GDN_CONV_CH + GDN_WIDTH + 2 * GDN_HEADS
IN_COLS = HG_COLS + GDN_COLS

kernel_name = "hybrid_hgrn2_gdn_parallel_heads_adaln"


def rmsnorm(x, w):
    x32 = x.astype(jnp.float32)
    y = x32 * lax.rsqrt(jnp.mean(x32 * x32, axis=-1, keepdims=True) + EPS)
    return (y * w.astype(jnp.float32)).astype(x.dtype)


def l2norm(x):
    return x * lax.rsqrt(jnp.sum(x * x, axis=-1, keepdims=True) + EPS)


def to_chunks(x):
    B, T, H, D = x.shape
    return x.reshape(B, T // CHUNK, CHUNK, H, D).transpose(1, 0, 3, 2, 4)


def from_chunks(x):
    N, B, H, C, D = x.shape
    return x.transpose(1, 0, 3, 2, 4).reshape(B, N * C, H, D)


def causal_conv(u, w):
    ch = u.shape[-1]
    return lax.conv_general_dilated(
        u, w[:, None, :].astype(u.dtype), window_strides=(1,), padding=[(CONV_K - 1, 0)],
        dimension_numbers=('NWC', 'WIO', 'NWC'), feature_group_count=ch)


def hgrn2_chunked(q, log_f, k, v):
    B, T, H, DK = q.shape
    DV = v.shape[-1]
    causal = jnp.tril(jnp.ones((CHUNK, CHUNK), dtype=bool))

    def step(S, xs):
        qc, lfc, kc, vc = xs
        b = jnp.cumsum(lfc, axis=-2)
        diff = b[:, :, :, None, :] - b[:, :, None, :, :]
        decay = jnp.exp(jnp.where(causal[None, None, :, :, None], diff, -jnp.inf))
        attn = jnp.einsum('bhtk,bhsk,bhtsk->bhts', qc, kc, decay)
        o = (jnp.einsum('bhts,bhsv->bhtv', attn, vc)
             + jnp.einsum('bhtk,bhkv->bhtv', qc * jnp.exp(b), S))
        b_last = b[:, :, -1:, :]
        S_new = (S * jnp.exp(b_last[:, :, 0, :, None])
                 + jnp.einsum('bhsk,bhsv->bhkv', kc * jnp.exp(b_last - b), vc))
        return S_new, o

    S0 = jnp.zeros((B, H, DK, DV), jnp.float32)
    _, o = lax.scan(step, S0, (to_chunks(q), to_chunks(log_f), to_chunks(k), to_chunks(v)))
    return from_chunks(o)


def gated_delta_chunked(q, k, v, log_a, beta):
    B, T, H, DK = q.shape
    DV = v.shape[-1]
    qc, kc, vc = to_chunks(q), to_chunks(k), to_chunks(v)
    g = jnp.cumsum(to_chunks(log_a[..., None])[..., 0], axis=-1)
    bc = to_chunks(beta[..., None])[..., 0]
    incl = jnp.tril(jnp.ones((CHUNK, CHUNK), dtype=bool))
    strict = jnp.tril(jnp.ones((CHUNK, CHUNK), dtype=bool), -1)
    gamma = jnp.exp(jnp.where(incl, g[..., :, None] - g[..., None, :], -jnp.inf))
    kk = jnp.einsum('nbhtk,nbhsk->nbhts', kc, kc)
    m = jnp.where(strict, bc[..., :, None] * kk * gamma, 0.0)
    a_mat = jnp.eye(CHUNK, dtype=jnp.float32) + m
    rhs = jnp.concatenate([vc * bc[..., None], kc * (bc * jnp.exp(g))[..., None]], axis=-1)
    sol = lax.linalg.triangular_solve(a_mat, rhs, left_side=True, lower=True, unit_diagonal=True)
    u, w = sol[..., :DV], sol[..., DV:]
    qk = jnp.einsum('nbhtk,nbhsk->nbhts', qc, kc) * gamma
    q_dec = qc * jnp.exp(g)[..., None]
    k_tail = kc * jnp.exp(g[..., -1:] - g)[..., None]
    tail = jnp.exp(g[..., -1])

    def step(S, xs):
        u_c, w_c, qk_c, qd_c, kt_c, tl_c = xs
        v_new = u_c - jnp.einsum('bhck,bhkv->bhcv', w_c, S)
        o = jnp.einsum('bhck,bhkv->bhcv', qd_c, S) + jnp.einsum('bhts,bhsv->bhtv', qk_c, v_new)
        S = S * tl_c[..., None, None] + jnp.einsum('bhck,bhcv->bhkv', kt_c, v_new)
        return S, o

    S0 = jnp.zeros((B, H, DK, DV), jnp.float32)
    _, o = lax.scan(step, S0, (u, w, qk, q_dec, k_tail, tail))
    return from_chunks(o)


def hgrn2_group(p, lb, norm_w):
    B, T, _ = p.shape
    dt = p.dtype
    p32 = p.astype(jnp.float32)
    q = p32[..., :HG_QK].reshape(B, T, HG_HEADS, HG_DK)
    f_logit = p32[..., HG_QK:2 * HG_QK].reshape(B, T, HG_HEADS, HG_DK)
    i_in = p32[..., 2 * HG_QK:2 * HG_QK + HG_WIDTH].reshape(B, T, HG_HEADS, HG_DV)
    g_out = p[..., 2 * HG_QK + HG_WIDTH:].reshape(B, T, HG_HEADS, HG_DV)
    f = lb + (1.0 - lb) * jax.nn.sigmoid(f_logit)
    o = hgrn2_chunked(q, jnp.log(f), 1.0 - f, i_in).astype(dt)
    o = rmsnorm(o, norm_w) * jax.nn.silu(g_out)
    return o.reshape(B, T, HG_WIDTH)


def gdn_group(p, conv_w, a_log, dt_bias, norm_w):
    B, T, _ = p.shape
    dt = p.dtype
    qkv = jax.nn.silu(causal_conv(p[..., :GDN_CONV_CH], conv_w)).astype(jnp.float32)
    q = l2norm(qkv[..., :GDN_QK].reshape(B, T, GDN_HEADS, GDN_DK)) * (GDN_DK ** -0.5)
    k = l2norm(qkv[..., GDN_QK:2 * GDN_QK].reshape(B, T, GDN_HEADS, GDN_DK))
    v = qkv[..., 2 * GDN_QK:].reshape(B, T, GDN_HEADS, GDN_DV)
    off = GDN_CONV_CH
    g_out = p[..., off:off + GDN_WIDTH].reshape(B, T, GDN_HEADS, GDN_DV)
    a = p[..., off + GDN_WIDTH:off + GDN_WIDTH + GDN_HEADS].astype(jnp.float32)
    b = p[..., off + GDN_WIDTH + GDN_HEADS:].astype(jnp.float32)
    log_a = -jnp.exp(a_log.astype(jnp.float32)) * jax.nn.softplus(a + dt_bias.astype(jnp.float32))
    beta = jax.nn.sigmoid(b)
    o = gated_delta_chunked(q, k, v, log_a, beta).astype(dt)
    o = rmsnorm(o, norm_w) * jax.nn.silu(g_out)
    return o.reshape(B, T, GDN_WIDTH)


def _fwd_setup_inputs(seed: int = 0) -> dict:
    key = jax.random.key(seed)
    ks = jax.random.split(key, 20)
    f32 = jnp.float32
    nrm = lambda k, s, sc: jax.random.normal(k, s, f32) * sc
    gain = lambda k, s: 1.0 + 0.05 * jax.random.normal(k, s, f32)
    dtv = jnp.exp(jax.random.uniform(ks[13], (DEPTH, GDN_HEADS), f32, np.log(1e-3), np.log(1e-1)))
    return {
        "x": nrm(ks[0], (BATCH, SEQ, D_MODEL), 1.0),
        "c": nrm(ks[1], (BATCH, D_MODEL), 1.0),
        "w_ada": nrm(ks[2], (DEPTH, D_MODEL, N_MOD * D_MODEL), 0.5 * D_MODEL ** -0.5),
        "b_ada": nrm(ks[3], (DEPTH, N_MOD * D_MODEL), 0.02),
        "pre_mix_norm": gain(ks[4], (DEPTH, D_MODEL)),
        "post_mix_norm": gain(ks[5], (DEPTH, D_MODEL)),
        "pre_ffn_norm": gain(ks[6], (DEPTH, D_MODEL)),
        "post_ffn_norm": gain(ks[7], (DEPTH, D_MODEL)),
        "w_in": nrm(ks[8], (DEPTH, D_MODEL, IN_COLS), D_MODEL ** -0.5),
        "hg_lb_logits": nrm(ks[9], (DEPTH + 1, HG_HEADS, HG_DK), 0.5),
        "hg_norm": gain(ks[10], (DEPTH, HG_DV)),
        "gdn_conv_w": nrm(ks[11], (DEPTH, CONV_K, GDN_CONV_CH), CONV_K ** -0.5),
        "gdn_a_log": jnp.log(jax.random.uniform(ks[12], (DEPTH, GDN_HEADS), f32, 1.0, 16.0)),
        "gdn_dt_bias": dtv + jnp.log(-jnp.expm1(-dtv)),
        "gdn_norm": gain(ks[14], (DEPTH, GDN_DV)),
        "w_out": nrm(ks[15], (DEPTH, D_MIX, D_MODEL), D_MIX ** -0.5),
        "w_ff1": nrm(ks[16], (DEPTH, D_MODEL, D_FF), D_MODEL ** -0.5),
        "w_ff2": nrm(ks[17], (DEPTH, D_FF, D_MODEL), D_FF ** -0.5),
    }


def _fwd_reference(x, c, w_ada, b_ada, pre_mix_norm, post_mix_norm, pre_ffn_norm, post_ffn_norm,
              w_in, hg_lb_logits, hg_norm, gdn_conv_w, gdn_a_log, gdn_dt_bias, gdn_norm,
              w_out, w_ff1, w_ff2):
    lb_all = jnp.cumsum(jax.nn.softmax(hg_lb_logits.astype(jnp.float32), axis=0), axis=0)
    c_act = jax.nn.silu(c)
    for l in range(DEPTH):
        mod = c_act @ w_ada[l] + b_ada[l]
        sh_m, sc_m, gt_m, sh_f, sc_f, gt_f = jnp.split(mod[:, None, :], N_MOD, axis=-1)
        h = rmsnorm(x, pre_mix_norm[l]) * (1.0 + sc_m) + sh_m
        proj = h @ w_in[l]
        o_hg = hgrn2_group(proj[..., :HG_COLS], lb_all[l], hg_norm[l])
        o_gdn = gdn_group(proj[..., HG_COLS:], gdn_conv_w[l], gdn_a_log[l], gdn_dt_bias[l], gdn_norm[l])
        y = jnp.concatenate([o_hg, o_gdn], axis=-1) @ w_out[l]
        x = x + gt_m * rmsnorm(y, post_mix_norm[l])
        h = rmsnorm(x, pre_ffn_norm[l]) * (1.0 + sc_f) + sh_f
        y = jnp.square(jax.nn.relu(h @ w_ff1[l])) @ w_ff2[l]
        x = x + gt_f * rmsnorm(y, post_ffn_norm[l])
    return x


import jax as _jax
import jax.numpy as _jnp

TWIN_FORMAT = 'train_step'
FWD_PARAMS = ['x', 'c', 'w_ada', 'b_ada', 'pre_mix_norm', 'post_mix_norm', 'pre_ffn_norm', 'post_ffn_norm', 'w_in', 'hg_lb_logits', 'hg_norm', 'gdn_conv_w', 'gdn_a_log', 'gdn_dt_bias', 'gdn_norm', 'w_out', 'w_ff1', 'w_ff2']
TWIN_WEIGHTS = ['w_ada', 'b_ada', 'pre_mix_norm', 'post_mix_norm', 'pre_ffn_norm', 'post_ffn_norm', 'w_in', 'hg_lb_logits', 'hg_norm', 'gdn_conv_w', 'gdn_a_log', 'gdn_dt_bias', 'gdn_norm', 'w_out', 'w_ff1', 'w_ff2']
TWIN_DIFF_INPUT = 'x'
TWIN_INPUTS = ['x', 'c', 'w_ada', 'b_ada', 'pre_mix_norm', 'post_mix_norm', 'pre_ffn_norm', 'post_ffn_norm', 'w_in', 'hg_lb_logits', 'hg_norm', 'gdn_conv_w', 'gdn_a_log', 'gdn_dt_bias', 'gdn_norm', 'w_out', 'w_ff1', 'w_ff2', 'loss_target', 'm_w_ada', 'm_b_ada', 'm_pre_mix_norm', 'm_post_mix_norm', 'm_pre_ffn_norm', 'm_post_ffn_norm', 'm_w_in', 'm_hg_lb_logits', 'm_hg_norm', 'm_gdn_conv_w', 'm_gdn_a_log', 'm_gdn_dt_bias', 'm_gdn_norm', 'm_w_out', 'm_w_ff1', 'm_w_ff2', 'v_w_ada', 'v_b_ada', 'v_pre_mix_norm', 'v_post_mix_norm', 'v_pre_ffn_norm', 'v_post_ffn_norm', 'v_w_in', 'v_hg_lb_logits', 'v_hg_norm', 'v_gdn_conv_w', 'v_gdn_a_log', 'v_gdn_dt_bias', 'v_gdn_norm', 'v_w_out', 'v_w_ff1', 'v_w_ff2']
TWIN_OUTPUTS = ['loss', 'grad_x', 'grad_w_ada', 'grad_b_ada', 'grad_pre_mix_norm', 'grad_post_mix_norm', 'grad_pre_ffn_norm', 'grad_post_ffn_norm', 'grad_w_in', 'grad_hg_lb_logits', 'grad_hg_norm', 'grad_gdn_conv_w', 'grad_gdn_a_log', 'grad_gdn_dt_bias', 'grad_gdn_norm', 'grad_w_out', 'grad_w_ff1', 'grad_w_ff2', 'delta_w_ada', 'delta_b_ada', 'delta_pre_mix_norm', 'delta_post_mix_norm', 'delta_pre_ffn_norm', 'delta_post_ffn_norm', 'delta_w_in', 'delta_hg_lb_logits', 'delta_hg_norm', 'delta_gdn_conv_w', 'delta_gdn_a_log', 'delta_gdn_dt_bias', 'delta_gdn_norm', 'delta_w_out', 'delta_w_ff1', 'delta_w_ff2', 'new_m_w_ada', 'new_m_b_ada', 'new_m_pre_mix_norm', 'new_m_post_mix_norm', 'new_m_pre_ffn_norm', 'new_m_post_ffn_norm', 'new_m_w_in', 'new_m_hg_lb_logits', 'new_m_hg_norm', 'new_m_gdn_conv_w', 'new_m_gdn_a_log', 'new_m_gdn_dt_bias', 'new_m_gdn_norm', 'new_m_w_out', 'new_m_w_ff1', 'new_m_w_ff2', 'new_v_w_ada', 'new_v_b_ada', 'new_v_pre_mix_norm', 'new_v_post_mix_norm', 'new_v_pre_ffn_norm', 'new_v_post_ffn_norm', 'new_v_w_in', 'new_v_hg_lb_logits', 'new_v_hg_norm', 'new_v_gdn_conv_w', 'new_v_gdn_a_log', 'new_v_gdn_dt_bias', 'new_v_gdn_norm', 'new_v_w_out', 'new_v_w_ff1', 'new_v_w_ff2']
TWIN_LEAF_KINDS = {'loss': 'loss', 'grad_x': 'grad_x', 'grad_w_ada': 'grad_w', 'grad_b_ada': 'grad_w', 'grad_pre_mix_norm': 'grad_w', 'grad_post_mix_norm': 'grad_w', 'grad_pre_ffn_norm': 'grad_w', 'grad_post_ffn_norm': 'grad_w', 'grad_w_in': 'grad_w', 'grad_hg_lb_logits': 'grad_w', 'grad_hg_norm': 'grad_w', 'grad_gdn_conv_w': 'grad_w', 'grad_gdn_a_log': 'grad_w', 'grad_gdn_dt_bias': 'grad_w', 'grad_gdn_norm': 'grad_w', 'grad_w_out': 'grad_w', 'grad_w_ff1': 'grad_w', 'grad_w_ff2': 'grad_w', 'delta_w_ada': 'delta_w', 'delta_b_ada': 'delta_w', 'delta_pre_mix_norm': 'delta_w', 'delta_post_mix_norm': 'delta_w', 'delta_pre_ffn_norm': 'delta_w', 'delta_post_ffn_norm': 'delta_w', 'delta_w_in': 'delta_w', 'delta_hg_lb_logits': 'delta_w', 'delta_hg_norm': 'delta_w', 'delta_gdn_conv_w': 'delta_w', 'delta_gdn_a_log': 'delta_w', 'delta_gdn_dt_bias': 'delta_w', 'delta_gdn_norm': 'delta_w', 'delta_w_out': 'delta_w', 'delta_w_ff1': 'delta_w', 'delta_w_ff2': 'delta_w', 'new_m_w_ada': 'new_m', 'new_m_b_ada': 'new_m', 'new_m_pre_mix_norm': 'new_m', 'new_m_post_mix_norm': 'new_m', 'new_m_pre_ffn_norm': 'new_m', 'new_m_post_ffn_norm': 'new_m', 'new_m_w_in': 'new_m', 'new_m_hg_lb_logits': 'new_m', 'new_m_hg_norm': 'new_m', 'new_m_gdn_conv_w': 'new_m', 'new_m_gdn_a_log': 'new_m', 'new_m_gdn_dt_bias': 'new_m', 'new_m_gdn_norm': 'new_m', 'new_m_w_out': 'new_m', 'new_m_w_ff1': 'new_m', 'new_m_w_ff2': 'new_m', 'new_v_w_ada': 'new_v', 'new_v_b_ada': 'new_v', 'new_v_pre_mix_norm': 'new_v', 'new_v_post_mix_norm': 'new_v', 'new_v_pre_ffn_norm': 'new_v', 'new_v_post_ffn_norm': 'new_v', 'new_v_w_in': 'new_v', 'new_v_hg_lb_logits': 'new_v', 'new_v_hg_norm': 'new_v', 'new_v_gdn_conv_w': 'new_v', 'new_v_gdn_a_log': 'new_v', 'new_v_gdn_dt_bias': 'new_v', 'new_v_gdn_norm': 'new_v', 'new_v_w_out': 'new_v', 'new_v_w_ff1': 'new_v', 'new_v_w_ff2': 'new_v'}


def _forward(args):
    return _fwd_reference(*[args[k] for k in FWD_PARAMS])


def _output_shape():
    def fwd():
        inp = _fwd_setup_inputs(0)
        return _fwd_reference(*[inp[k] for k in FWD_PARAMS])
    out = _jax.eval_shape(fwd)
    return out.shape, out.dtype

N_MICROBATCH = 1
ADAM_LR = 0.001
ADAM_B1 = 0.9
ADAM_B2 = 0.999
ADAM_EPS = 1e-08
ADAM_WD = 0.01
ADAM_STEP = 10
PER_EXAMPLE_BATCH_AXIS = {'x': 0, 'c': 0, 'loss_target': 0}
SHARED_INPUTS = []
_WEIGHT_DTYPES = {'w_ada': _jnp.float32, 'b_ada': _jnp.float32, 'pre_mix_norm': _jnp.float32, 'post_mix_norm': _jnp.float32, 'pre_ffn_norm': _jnp.float32, 'post_ffn_norm': _jnp.float32, 'w_in': _jnp.float32, 'hg_lb_logits': _jnp.float32, 'hg_norm': _jnp.float32, 'gdn_conv_w': _jnp.float32, 'gdn_a_log': _jnp.float32, 'gdn_dt_bias': _jnp.float32, 'gdn_norm': _jnp.float32, 'w_out': _jnp.float32, 'w_ff1': _jnp.float32, 'w_ff2': _jnp.float32}
MOMENT_SCALE = {'w_ada': 6.239603e-01, 'b_ada': 1.361854e+00, 'pre_mix_norm': 7.777822e-02, 'post_mix_norm': 1.645563e+00, 'pre_ffn_norm': 5.623270e-02, 'post_ffn_norm': 1.686692e+00, 'w_in': 4.532494e-02, 'hg_lb_logits': 2.302217e-02, 'hg_norm': 1.738965e-01, 'gdn_conv_w': 4.345360e-02, 'gdn_a_log': 2.044061e-01, 'gdn_dt_bias': 1.861958e-01, 'gdn_norm': 2.643777e-01, 'w_out': 6.357285e-02, 'w_ff1': 3.647038e-02, 'w_ff2': 1.433071e-01}


def _to_microbatches(a, axis):
    t = _jnp.moveaxis(a, axis, 0)
    t = t.reshape((N_MICROBATCH, t.shape[0] // N_MICROBATCH) + t.shape[1:])
    return _jnp.moveaxis(t, 1, axis + 1)


def setup_inputs(seed: int = 0) -> dict:
    inp = _fwd_setup_inputs(seed)
    key = _jax.random.fold_in(_jax.random.key(seed), 7919)
    shape, _ = _output_shape()
    out = dict(inp)
    out["loss_target"] = _jax.random.normal(_jax.random.fold_in(key, 0), shape, _jnp.float32)
    for i, name in enumerate(TWIN_WEIGHTS):
        w = inp[name].astype(_jnp.float32)
        if MOMENT_SCALE is None:
            s = _jnp.sqrt(_jnp.mean(_jnp.square(w)) + 1e-30)
        else:
            s = MOMENT_SCALE[name]
        km, kv = _jax.random.split(_jax.random.fold_in(key, i + 1))
        out[name] = w
        out["m_" + name] = s * _jax.random.normal(km, w.shape, _jnp.float32)
        out["v_" + name] = (s * s) * _jax.random.uniform(kv, w.shape, _jnp.float32, 0.5, 1.5)
    if N_MICROBATCH > 1:
        for name, axis in PER_EXAMPLE_BATCH_AXIS.items():
            out[name] = _to_microbatches(out[name], axis)
    return {'x': out['x'], 'c': out['c'], 'w_ada': out['w_ada'], 'b_ada': out['b_ada'], 'pre_mix_norm': out['pre_mix_norm'], 'post_mix_norm': out['post_mix_norm'], 'pre_ffn_norm': out['pre_ffn_norm'], 'post_ffn_norm': out['post_ffn_norm'], 'w_in': out['w_in'], 'hg_lb_logits': out['hg_lb_logits'], 'hg_norm': out['hg_norm'], 'gdn_conv_w': out['gdn_conv_w'], 'gdn_a_log': out['gdn_a_log'], 'gdn_dt_bias': out['gdn_dt_bias'], 'gdn_norm': out['gdn_norm'], 'w_out': out['w_out'], 'w_ff1': out['w_ff1'], 'w_ff2': out['w_ff2'], 'loss_target': out['loss_target'], 'm_w_ada': out['m_w_ada'], 'm_b_ada': out['m_b_ada'], 'm_pre_mix_norm': out['m_pre_mix_norm'], 'm_post_mix_norm': out['m_post_mix_norm'], 'm_pre_ffn_norm': out['m_pre_ffn_norm'], 'm_post_ffn_norm': out['m_post_ffn_norm'], 'm_w_in': out['m_w_in'], 'm_hg_lb_logits': out['m_hg_lb_logits'], 'm_hg_norm': out['m_hg_norm'], 'm_gdn_conv_w': out['m_gdn_conv_w'], 'm_gdn_a_log': out['m_gdn_a_log'], 'm_gdn_dt_bias': out['m_gdn_dt_bias'], 'm_gdn_norm': out['m_gdn_norm'], 'm_w_out': out['m_w_out'], 'm_w_ff1': out['m_w_ff1'], 'm_w_ff2': out['m_w_ff2'], 'v_w_ada': out['v_w_ada'], 'v_b_ada': out['v_b_ada'], 'v_pre_mix_norm': out['v_pre_mix_norm'], 'v_post_mix_norm': out['v_post_mix_norm'], 'v_pre_ffn_norm': out['v_pre_ffn_norm'], 'v_post_ffn_norm': out['v_post_ffn_norm'], 'v_w_in': out['v_w_in'], 'v_hg_lb_logits': out['v_hg_lb_logits'], 'v_hg_norm': out['v_hg_norm'], 'v_gdn_conv_w': out['v_gdn_conv_w'], 'v_gdn_a_log': out['v_gdn_a_log'], 'v_gdn_dt_bias': out['v_gdn_dt_bias'], 'v_gdn_norm': out['v_gdn_norm'], 'v_w_out': out['v_w_out'], 'v_w_ff1': out['v_w_ff1'], 'v_w_ff2': out['v_w_ff2']}


def _loss(weights, diff, rest, loss_target):
    with _jax.named_scope("forward"):
        args = {**rest, TWIN_DIFF_INPUT: diff, **{k: w.astype(_WEIGHT_DTYPES[k]) for k, w in weights.items()}}
        y = _forward(args)
    with _jax.named_scope("loss_head"):
        err = _jnp.square(y.astype(_jnp.float32) - loss_target)
        return 0.5 * _jnp.sum(_jnp.mean(err, axis=-1)) if err.ndim else 0.5 * err


def _adamw(w, g, m, v):
    m = ADAM_B1 * m + (1.0 - ADAM_B1) * g
    v = ADAM_B2 * v + (1.0 - ADAM_B2) * _jnp.square(g)
    m_hat = m / (1.0 - ADAM_B1 ** ADAM_STEP)
    v_hat = v / (1.0 - ADAM_B2 ** ADAM_STEP)
    delta = -ADAM_LR * (m_hat / (_jnp.sqrt(v_hat) + ADAM_EPS) + ADAM_WD * w)
    return delta, m, v


def reference(x, c, w_ada, b_ada, pre_mix_norm, post_mix_norm, pre_ffn_norm, post_ffn_norm, w_in, hg_lb_logits, hg_norm, gdn_conv_w, gdn_a_log, gdn_dt_bias, gdn_norm, w_out, w_ff1, w_ff2, loss_target, m_w_ada, m_b_ada, m_pre_mix_norm, m_post_mix_norm, m_pre_ffn_norm, m_post_ffn_norm, m_w_in, m_hg_lb_logits, m_hg_norm, m_gdn_conv_w, m_gdn_a_log, m_gdn_dt_bias, m_gdn_norm, m_w_out, m_w_ff1, m_w_ff2, v_w_ada, v_b_ada, v_pre_mix_norm, v_post_mix_norm, v_pre_ffn_norm, v_post_ffn_norm, v_w_in, v_hg_lb_logits, v_hg_norm, v_gdn_conv_w, v_gdn_a_log, v_gdn_dt_bias, v_gdn_norm, v_w_out, v_w_ff1, v_w_ff2):
    given = dict(x=x, c=c, w_ada=w_ada, b_ada=b_ada, pre_mix_norm=pre_mix_norm, post_mix_norm=post_mix_norm, pre_ffn_norm=pre_ffn_norm, post_ffn_norm=post_ffn_norm, w_in=w_in, hg_lb_logits=hg_lb_logits, hg_norm=hg_norm, gdn_conv_w=gdn_conv_w, gdn_a_log=gdn_a_log, gdn_dt_bias=gdn_dt_bias, gdn_norm=gdn_norm, w_out=w_out, w_ff1=w_ff1, w_ff2=w_ff2, loss_target=loss_target, m_w_ada=m_w_ada, m_b_ada=m_b_ada, m_pre_mix_norm=m_pre_mix_norm, m_post_mix_norm=m_post_mix_norm, m_pre_ffn_norm=m_pre_ffn_norm, m_post_ffn_norm=m_post_ffn_norm, m_w_in=m_w_in, m_hg_lb_logits=m_hg_lb_logits, m_hg_norm=m_hg_norm, m_gdn_conv_w=m_gdn_conv_w, m_gdn_a_log=m_gdn_a_log, m_gdn_dt_bias=m_gdn_dt_bias, m_gdn_norm=m_gdn_norm, m_w_out=m_w_out, m_w_ff1=m_w_ff1, m_w_ff2=m_w_ff2, v_w_ada=v_w_ada, v_b_ada=v_b_ada, v_pre_mix_norm=v_pre_mix_norm, v_post_mix_norm=v_post_mix_norm, v_pre_ffn_norm=v_pre_ffn_norm, v_post_ffn_norm=v_post_ffn_norm, v_w_in=v_w_in, v_hg_lb_logits=v_hg_lb_logits, v_hg_norm=v_hg_norm, v_gdn_conv_w=v_gdn_conv_w, v_gdn_a_log=v_gdn_a_log, v_gdn_dt_bias=v_gdn_dt_bias, v_gdn_norm=v_gdn_norm, v_w_out=v_w_out, v_w_ff1=v_w_ff1, v_w_ff2=v_w_ff2)
    weights = {n: given[n] for n in TWIN_WEIGHTS}
    shared = {n: given[n] for n in SHARED_INPUTS}
    per_example = {n: given[n] for n in ['x', 'c']}
    grad_fn = _jax.value_and_grad(_loss, argnums=(0, 1))

    def one_microbatch(ex, loss_target):
        ex = dict(ex)
        diff = ex.pop(TWIN_DIFF_INPUT)
        return grad_fn(weights, diff, {**shared, **ex}, loss_target)

    if N_MICROBATCH == 1:
        loss, (grad_w, grad_x) = one_microbatch(per_example, given["loss_target"])
    else:
        def body(carry, xs):
            loss_sum, grad_sum = carry
            l_k, (gw_k, gx_k) = one_microbatch(xs[0], xs[1])
            with _jax.named_scope("update"):
                return (loss_sum + l_k, _jax.tree.map(_jnp.add, grad_sum, gw_k)), gx_k

        init = (_jnp.zeros((), _jnp.float32), _jax.tree.map(_jnp.zeros_like, weights))
        (loss, grad_w), grad_x = _jax.lax.scan(body, init, (per_example, given["loss_target"]))
    with _jax.named_scope("update"):
        delta_w, new_m, new_v = {}, {}, {}
        for n in TWIN_WEIGHTS:
            delta_w[n], new_m[n], new_v[n] = _adamw(weights[n], grad_w[n], given["m_" + n], given["v_" + n])
    return (loss, grad_x, *[grad_w[n] for n in TWIN_WEIGHTS], *[delta_w[n] for n in TWIN_WEIGHTS],
            *[new_m[n] for n in TWIN_WEIGHTS], *[new_v[n] for n in TWIN_WEIGHTS])
```

```python
import functools
import math

import jax
import jax.numpy as jnp
from jax import lax
from jax.experimental import pallas as pl
from jax.experimental.pallas import tpu as pltpu

F32 = jnp.float32
BF16 = jnp.bfloat16
HI = lax.Precision.HIGHEST
MESH = pl.DeviceIdType.MESH

LANES = 128
SUBLANES = 8
VMEM_LIMIT = 48 * 1024 * 1024
EPS = 1e-6
HEAD = 128
CONV_K = 4
GDN_CHUNK = 64
GDN_INV_BLOCK = 16
HG_SUB = 16
HG_BLOCK = 128
N_MOD = 6
N_DEV = 8
N_CHIP = 4

ADAM_LR = 0.001
ADAM_B1 = 0.9
ADAM_B2 = 0.999
ADAM_EPS = 1e-08
ADAM_WD = 0.01
ADAM_STEP = 10

NT_DIMS = (((1,), (1,)), ((), ()))
TN_DIMS = (((0,), (0,)), ((), ()))


def _tile(dim, target, align):
    if dim <= target:
        return dim
    best = dim
    t = align
    while t <= target:
        if dim % t == 0:
            best = t
        t += align
    return best


def _elementwise_tiles(r, c):
    tc = _tile(c, 1024, LANES)
    tr = _tile(r, max(16, (256 * 1024) // tc // 16 * 16), 16)
    return tr, tc


def _params(sem):
    return pltpu.CompilerParams(dimension_semantics=sem, vmem_limit_bytes=VMEM_LIMIT)


def _silu(x):
    return x * jax.nn.sigmoid(x)


def _softplus(x):
    pos = x > 0
    return jnp.where(pos, x, 0.0) + jnp.log(1.0 + jnp.exp(jnp.where(pos, -x, x)))


def _rms_scale(x):
    return lax.rsqrt(jnp.mean(x * x, axis=-1, keepdims=True) + EPS)


def _gather8(x_shard, name):
    m_per, n = x_shard.shape
    assert m_per % SUBLANES == 0 and n % LANES == 0

    def body(x_ref, out_ref, send_sems, recv_sems, local_sem):
        x, y, c = lax.axis_index("x"), lax.axis_index("y"), lax.axis_index("c")
        me, sibling = (x, y, c), (x, y, 1 - c)
        chips = [(1 - x, y), (x, 1 - y), (1 - x, 1 - y)]

        def rows(px, py, pc):
            return out_ref.at[pl.ds((4 * px + 2 * py + pc) * m_per, m_per), :]

        def copy(k, block, to, src=None):
            return pltpu.make_async_remote_copy(
                src_ref=rows(*block) if src is None else src, dst_ref=rows(*block),
                send_sem=send_sems.at[k], recv_sem=recv_sems.at[k], device_id=to, device_id_type=MESH)

        mine = pltpu.make_async_copy(x_ref, rows(*me), local_sem)
        mine.start()
        first = [copy(0, me, sibling, src=x_ref)]
        first += [copy(1 + j, me, (*chip, c), src=x_ref) for j, chip in enumerate(chips)]
        for cp in first:
            cp.start()
        passed = [copy(4 + j, (*chip, c), sibling) for j, chip in enumerate(chips)]
        for j, chip in enumerate(chips):
            copy(1 + j, (*chip, c), me).wait_recv()
            passed[j].start()
        copy(0, sibling, me).wait_recv()
        for j, chip in enumerate(chips):
            copy(4 + j, (*chip, 1 - c), me).wait_recv()
        for cp in first + passed:
            cp.wait_send()
        mine.wait()

    return pl.pallas_call(
        body, name=name,
        out_shape=jax.ShapeDtypeStruct((N_DEV * m_per, n), x_shard.dtype),
        in_specs=[pl.BlockSpec(memory_space=pltpu.VMEM)],
        out_specs=pl.BlockSpec(memory_space=pltpu.VMEM),
        scratch_shapes=[pltpu.SemaphoreType.DMA((7,)), pltpu.SemaphoreType.DMA((7,)), pltpu.SemaphoreType.DMA],
        compiler_params=pltpu.CompilerParams(vmem_limit_bytes=VMEM_LIMIT),
    )(x_shard)


def _chip_exchange(arrs, bcast, name):
    n = len(arrs)
    out_shapes = [jax.ShapeDtypeStruct((N_CHIP,) + (a.shape if bcast else a.shape[1:]), a.dtype) for a in arrs]

    def body(*refs):
        ins, outs = refs[:n], refs[n:2 * n]
        send_sems, recv_sems, local_sems = refs[2 * n:]
        x, y, c = lax.axis_index("x"), lax.axis_index("y"), lax.axis_index("c")
        me = 2 * x + y
        peers = [(1 - x, y), (x, 1 - y), (1 - x, 1 - y)]
        copies = []
        for a in range(n):
            src_own = ins[a] if bcast else ins[a].at[me]
            loc = pltpu.make_async_copy(src_own, outs[a].at[me], local_sems.at[a])
            loc.start()
            copies.append(loc)
        remote = []
        for a in range(n):
            for k, (px, py) in enumerate(peers):
                them = 2 * px + py
                cp = pltpu.make_async_remote_copy(
                    src_ref=ins[a] if bcast else ins[a].at[them], dst_ref=outs[a].at[me],
                    send_sem=send_sems.at[3 * a + k], recv_sem=recv_sems.at[3 * a + k],
                    device_id=(px, py, c), device_id_type=MESH)
                cp.start()
                remote.append((cp, a, k, them))
        for cp, a, k, them in remote:
            pltpu.make_async_remote_copy(
                src_ref=ins[a] if bcast else ins[a].at[them], dst_ref=outs[a].at[them],
                send_sem=send_sems.at[3 * a + k], recv_sem=recv_sems.at[3 * a + k],
                device_id=(x, y, c), device_id_type=MESH).wait_recv()
        for cp, a, k, them in remote:
            cp.wait_send()
        for loc in copies:
            loc.wait()

    hbm = pl.BlockSpec(memory_space=pltpu.HBM)
    return pl.pallas_call(
        body, name=name, out_shape=out_shapes, in_specs=[hbm] * n, out_specs=[hbm] * n,
        scratch_shapes=[pltpu.SemaphoreType.DMA((3 * n,)), pltpu.SemaphoreType.DMA((3 * n,)), pltpu.SemaphoreType.DMA((n,))],
    )(*arrs)


def _sibling_exchange(arrs, name):
    n = len(arrs)

    def body(*refs):
        ins, outs = refs[:n], refs[n:2 * n]
        send_sems, recv_sems = refs[2 * n:]
        sibling = (lax.axis_index("x"), lax.axis_index("y"), 1 - lax.axis_index("c"))
        cps = []
        for a in range(n):
            cp = pltpu.make_async_remote_copy(src_ref=ins[a], dst_ref=outs[a], send_sem=send_sems.at[a],
                                              recv_sem=recv_sems.at[a], device_id=sibling, device_id_type=MESH)
            cp.start()
            cps.append(cp)
        for cp in cps:
            cp.wait_recv()
        for cp in cps:
            cp.wait_send()

    hbm = pl.BlockSpec(memory_space=pltpu.HBM)
    return pl.pallas_call(
        body, name=name, out_shape=[jax.ShapeDtypeStruct(a.shape, a.dtype) for a in arrs],
        in_specs=[hbm] * n, out_specs=[hbm] * n,
        scratch_shapes=[pltpu.SemaphoreType.DMA((n,)), pltpu.SemaphoreType.DMA((n,))],
    )(*arrs)


def _matmul(a, b, mode, out_dtype, name, tm=1024, tn=1024, tk=512):
    if mode == "nn":
        (m, k), (k2, n) = a.shape, b.shape
    elif mode == "nt":
        (m, k), (n, k2) = a.shape, b.shape
    else:
        (k, m), (k2, n) = a.shape, b.shape
    assert k == k2, (a.shape, b.shape, mode)
    tm, tn, tk = _tile(m, tm, LANES), _tile(n, tn, LANES), _tile(k, tk, LANES)
    nk = k // tk

    def body(a_ref, b_ref, o_ref, acc_ref):
        kk = pl.program_id(2)

        @pl.when(kk == 0)
        def _():
            acc_ref[...] = jnp.zeros_like(acc_ref)

        if mode == "nn":
            acc_ref[...] += jnp.dot(a_ref[...], b_ref[...], preferred_element_type=F32)
        elif mode == "nt":
            acc_ref[...] += lax.dot_general(a_ref[...], b_ref[...], NT_DIMS, preferred_element_type=F32)
        else:
            acc_ref[...] += lax.dot_general(a_ref[...], b_ref[...], TN_DIMS, preferred_element_type=F32)

        @pl.when(kk == nk - 1)
        def _():
            o_ref[...] = acc_ref[...].astype(o_ref.dtype)

    if mode == "nn":
        a_spec = pl.BlockSpec((tm, tk), lambda i, j, kk: (i, kk))
        b_spec = pl.BlockSpec((tk, tn), lambda i, j, kk: (kk, j))
    elif mode == "nt":
        a_spec = pl.BlockSpec((tm, tk), lambda i, j, kk: (i, kk))
        b_spec = pl.BlockSpec((tn, tk), lambda i, j, kk: (j, kk))
    else:
        a_spec = pl.BlockSpec((tk, tm), lambda i, j, kk: (kk, i))
        b_spec = pl.BlockSpec((tk, tn), lambda i, j, kk: (kk, j))
    return pl.pallas_call(
        body, name=name, grid=(m // tm, n // tn, nk), in_specs=[a_spec, b_spec],
        out_specs=pl.BlockSpec((tm, tn), lambda i, j, kk: (i, j)),
        out_shape=jax.ShapeDtypeStruct((m, n), out_dtype),
        scratch_shapes=[pltpu.VMEM((tm, tn), F32)],
        compiler_params=_params(("parallel", "parallel", "arbitrary")),
    )(a, b)


def _mod_part(c_all, w_s, b_s, name):
    d, na = w_s.shape
    tn = _tile(na, 512, LANES)

    def body(c_ref, w_ref, b_ref, o_ref):
        ca = _silu(c_ref[...]).astype(BF16)
        o_ref[...] = jnp.dot(ca, w_ref[...].astype(BF16), preferred_element_type=F32) + b_ref[...]

    return pl.pallas_call(
        body, name=name, grid=(na // tn,),
        in_specs=[pl.BlockSpec((N_DEV, d), lambda j: (0, 0)), pl.BlockSpec((d, tn), lambda j: (0, j)),
                  pl.BlockSpec((1, tn), lambda j: (0, j))],
        out_specs=pl.BlockSpec((N_DEV, tn), lambda j: (0, j)),
        out_shape=jax.ShapeDtypeStruct((N_DEV, na), F32), compiler_params=_params(("parallel",)),
    )(c_all, w_s, b_s)


def _wada_grad(c_all, dmod_s, name):
    d = c_all.shape[1]
    na = dmod_s.shape[1]
    td, tn = _tile(d, 512, LANES), _tile(na, 512, LANES)

    def body(c_ref, g_ref, o_ref):
        o_ref[...] = lax.dot_general(_silu(c_ref[...]), g_ref[...], TN_DIMS, precision=HI, preferred_element_type=F32)

    return pl.pallas_call(
        body, name=name, grid=(d // td, na // tn),
        in_specs=[pl.BlockSpec((N_DEV, td), lambda i, j: (0, i)), pl.BlockSpec((N_DEV, tn), lambda i, j: (0, j))],
        out_specs=pl.BlockSpec((td, tn), lambda i, j: (i, j)),
        out_shape=jax.ShapeDtypeStruct((d, na), F32), compiler_params=_params(("parallel", "parallel")),
    )(c_all, dmod_s)


def _row_specs(tb, d, n_full, n_vec):
    full = pl.BlockSpec((tb, d), lambda i: (i, 0))
    vec = pl.BlockSpec((1, d), lambda i: (0, 0))
    return [full] * n_full + [vec] * n_vec


def _norm_mod(x, w, sc, sh, name):
    t, d = x.shape
    tb = _tile(t, 256, SUBLANES)

    def body(x_ref, w_ref, sc_ref, sh_ref, o_ref):
        xv = x_ref[...]
        o_ref[...] = (xv * _rms_scale(xv) * w_ref[...] * (1.0 + sc_ref[...]) + sh_ref[...]).astype(o_ref.dtype)

    return pl.pallas_call(
        body, name=name, grid=(t // tb,), in_specs=_row_specs(tb, d, 1, 3),
        out_specs=pl.BlockSpec((tb, d), lambda i: (i, 0)), out_shape=jax.ShapeDtypeStruct((t, d), BF16),
        compiler_params=_params(("parallel",)),
    )(x, w, sc, sh)


def _norm_mod_bwd(x, w, sc, dh, dres, name):
    t, d = x.shape
    tb = _tile(t, 256, SUBLANES)

    def body(x_ref, w_ref, sc_ref, dh_ref, dres_ref, dx_ref, dw_ref, dsc_ref, dsh_ref):
        @pl.when(pl.program_id(0) == 0)
        def _():
            dw_ref[...] = jnp.zeros_like(dw_ref)
            dsc_ref[...] = jnp.zeros_like(dsc_ref)
            dsh_ref[...] = jnp.zeros_like(dsh_ref)

        xv = x_ref[...]
        r = _rms_scale(xv)
        xn = xv * r
        g = dh_ref[...].astype(F32)
        wv, one_sc = w_ref[...], 1.0 + sc_ref[...]
        gxn = g * xn
        dsh_ref[...] += jnp.sum(g, axis=0, keepdims=True)
        dsc_ref[...] += jnp.sum(gxn, axis=0, keepdims=True) * wv
        dw_ref[...] += jnp.sum(gxn, axis=0, keepdims=True) * one_sc
        dxn = g * (wv * one_sc)
        dx_ref[...] = dres_ref[...] + r * (dxn - xn * jnp.mean(dxn * xn, axis=-1, keepdims=True))

    vec_out = pl.BlockSpec((1, d), lambda i: (0, 0))
    return pl.pallas_call(
        body, name=name, grid=(t // tb,),
        in_specs=[pl.BlockSpec((tb, d), lambda i: (i, 0)), pl.BlockSpec((1, d), lambda i: (0, 0)),
                  pl.BlockSpec((1, d), lambda i: (0, 0)), pl.BlockSpec((tb, d), lambda i: (i, 0)),
                  pl.BlockSpec((tb, d), lambda i: (i, 0))],
        out_specs=[pl.BlockSpec((tb, d), lambda i: (i, 0)), vec_out, vec_out, vec_out],
        out_shape=[jax.ShapeDtypeStruct((t, d), F32)] + [jax.ShapeDtypeStruct((1, d), F32)] * 3,
        compiler_params=_params(("arbitrary",)),
    )(x, w, sc, dh, dres)


def _resid(x, y, w, gt, name):
    t, d = x.shape
    tb = _tile(t, 256, SUBLANES)

    def body(x_ref, y_ref, w_ref, gt_ref, o_ref):
        yv = y_ref[...]
        o_ref[...] = x_ref[...] + gt_ref[...] * (yv * _rms_scale(yv) * w_ref[...])

    return pl.pallas_call(
        body, name=name, grid=(t // tb,), in_specs=_row_specs(tb, d, 2, 2),
        out_specs=pl.BlockSpec((tb, d), lambda i: (i, 0)), out_shape=jax.ShapeDtypeStruct((t, d), F32),
        compiler_params=_params(("parallel",)),
    )(x, y, w, gt)


def _loss_head(x2, y2, w, gt, target, name):
    t, d = x2.shape
    tb = _tile(t, 256, SUBLANES)

    def body(x_ref, y_ref, tg_ref, w_ref, gt_ref, do_ref, loss_ref):
        @pl.when(pl.program_id(0) == 0)
        def _():
            loss_ref[...] = jnp.zeros_like(loss_ref)

        yv = y_ref[...]
        out = x_ref[...] + gt_ref[...] * (yv * _rms_scale(yv) * w_ref[...])
        err = out - tg_ref[...]
        do_ref[...] = err * (1.0 / d)
        per_tok = jnp.mean(err * err, axis=-1, keepdims=True)
        loss_ref[...] += 0.5 * jnp.sum(per_tok, axis=0, keepdims=True)

    return pl.pallas_call(
        body, name=name, grid=(t // tb,), in_specs=_row_specs(tb, d, 3, 2),
        out_specs=[pl.BlockSpec((tb, d), lambda i: (i, 0)), pl.BlockSpec((1, LANES), lambda i: (0, 0))],
        out_shape=[jax.ShapeDtypeStruct((t, d), F32), jax.ShapeDtypeStruct((1, LANES), F32)],
        compiler_params=_params(("arbitrary",)),
    )(x2, y2, target, w, gt)


def _resid_bwd(dout, y, w, gt, name):
    t, d = y.shape
    tb = _tile(t, 256, SUBLANES)

    def body(do_ref, y_ref, w_ref, gt_ref, dy_ref, dgt_ref, dw_ref):
        @pl.when(pl.program_id(0) == 0)
        def _():
            dgt_ref[...] = jnp.zeros_like(dgt_ref)
            dw_ref[...] = jnp.zeros_like(dw_ref)

        yv, g = y_ref[...], do_ref[...]
        r = _rms_scale(yv)
        yn = yv * r
        wv, gtv = w_ref[...], gt_ref[...]
        gyn = jnp.sum(g * yn, axis=0, keepdims=True)
        dgt_ref[...] += gyn * wv
        dw_ref[...] += gyn * gtv
        dyn = g * (gtv * wv)
        dy_ref[...] = (r * (dyn - yn * jnp.mean(dyn * yn, axis=-1, keepdims=True))).astype(dy_ref.dtype)

    vec_out = pl.BlockSpec((1, d), lambda i: (0, 0))
    return pl.pallas_call(
        body, name=name, grid=(t // tb,), in_specs=_row_specs(tb, d, 2, 2),
        out_specs=[pl.BlockSpec((tb, d), lambda i: (i, 0)), vec_out, vec_out],
        out_shape=[jax.ShapeDtypeStruct((t, d), BF16)] + [jax.ShapeDtypeStruct((1, d), F32)] * 2,
        compiler_params=_params(("arbitrary",)),
    )(dout, y, w, gt)


def _relu2(a1, name):
    t, n = a1.shape
    tb, tn = _elementwise_tiles(t, n)

    def body(a_ref, o_ref):
        r = jnp.maximum(a_ref[...], 0.0)
        o_ref[...] = (r * r).astype(o_ref.dtype)

    spec = pl.BlockSpec((tb, tn), lambda i, j: (i, j))
    return pl.pallas_call(body, name=name, grid=(t // tb, n // tn), in_specs=[spec], out_specs=spec,
                          out_shape=jax.ShapeDtypeStruct((t, n), BF16), compiler_params=_params(("parallel", "parallel")))(a1)


def _relu2_bwd(a1, dr, name):
    t, n = a1.shape
    tb, tn = _elementwise_tiles(t, n)

    def body(a_ref, g_ref, o_ref):
        o_ref[...] = (2.0 * jnp.maximum(a_ref[...], 0.0) * g_ref[...]).astype(o_ref.dtype)

    spec = pl.BlockSpec((tb, tn), lambda i, j: (i, j))
    return pl.pallas_call(body, name=name, grid=(t // tb, n // tn), in_specs=[spec, spec], out_specs=spec,
                          out_shape=jax.ShapeDtypeStruct((t, n), BF16), compiler_params=_params(("parallel", "parallel")))(a1, dr)


def _sum_chips(recv, name):
    _, r, c = recv.shape
    tr, tc = _elementwise_tiles(r, c)

    def body(x_ref, o_ref):
        acc = x_ref[0].astype(F32)
        for j in range(1, N_CHIP):
            acc = acc + x_ref[j].astype(F32)
        o_ref[...] = acc

    return pl.pallas_call(
        body, name=name, grid=(r // tr, c // tc), in_specs=[pl.BlockSpec((N_CHIP, tr, tc), lambda i, j: (0, i, j))],
        out_specs=pl.BlockSpec((tr, tc), lambda i, j: (i, j)), out_shape=jax.ShapeDtypeStruct((r, c), F32),
        compiler_params=_params(("parallel", "parallel")),
    )(recv)


def _adamw(w, g_parts, m, v, name):
    r, c = w.shape
    tr, tc = _elementwise_tiles(r, c)
    n_g = len(g_parts)
    c1 = 1.0 / (1.0 - ADAM_B1 ** ADAM_STEP)
    c2 = 1.0 / (1.0 - ADAM_B2 ** ADAM_STEP)

    def body(*refs):
        w_ref, g_refs, m_ref, v_ref = refs[0], refs[1:1 + n_g], refs[1 + n_g], refs[2 + n_g]
        g_out, d_out, m_out, v_out = refs[3 + n_g:]
        g = g_refs[0][...]
        for gr in g_refs[1:]:
            g = g + gr[...]
        mn = ADAM_B1 * m_ref[...] + (1.0 - ADAM_B1) * g
        vn = ADAM_B2 * v_ref[...] + (1.0 - ADAM_B2) * (g * g)
        g_out[...] = g
        m_out[...] = mn
        v_out[...] = vn
        d_out[...] = -ADAM_LR * ((mn * c1) / (jnp.sqrt(vn * c2) + ADAM_EPS) + ADAM_WD * w_ref[...])

    spec = pl.BlockSpec((tr, tc), lambda i, j: (i, j))
    return pl.pallas_call(
        body, name=name, grid=(r // tr, c // tc), in_specs=[spec] * (3 + n_g), out_specs=[spec] * 4,
        out_shape=[jax.ShapeDtypeStruct((r, c), F32)] * 4, compiler_params=_params(("parallel", "parallel")),
    )(w, *g_parts, m, v)


def _conv_taps(u, t):
    rows = lax.broadcasted_iota(jnp.int32, u.shape, 0)
    return [u] + [jnp.where(rows >= dd, pltpu.roll(u, dd, 0), 0.0) for dd in range(1, CONV_K)]


def _conv_fwd(proj, conv_w, col0, name):
    t = proj.shape[0]
    ch = conv_w.shape[1]

    def body(u_ref, w_ref, o_ref):
        taps = _conv_taps(u_ref[...], t)
        wv = w_ref[...]
        y = taps[0] * wv[CONV_K - 1:CONV_K]
        for dd in range(1, CONV_K):
            y = y + taps[dd] * wv[CONV_K - 1 - dd:CONV_K - dd]
        o_ref[...] = _silu(y)

    return pl.pallas_call(
        body, name=name, grid=(ch // LANES,),
        in_specs=[pl.BlockSpec((t, LANES), lambda j: (0, col0 + j)), pl.BlockSpec((CONV_K, LANES), lambda j: (0, j))],
        out_specs=pl.BlockSpec((t, LANES), lambda j: (0, j)), out_shape=jax.ShapeDtypeStruct((t, ch), F32),
        compiler_params=_params(("parallel",)),
    )(proj, conv_w)


def _conv_bwd(proj, conv_w, ds, col0, name):
    t = proj.shape[0]
    ch = conv_w.shape[1]

    def body(u_ref, w_ref, ds_ref, du_ref, dw_ref):
        u = u_ref[...]
        taps = _conv_taps(u, t)
        wv = w_ref[...]
        y = taps[0] * wv[CONV_K - 1:CONV_K]
        for dd in range(1, CONV_K):
            y = y + taps[dd] * wv[CONV_K - 1 - dd:CONV_K - dd]
        sg = jax.nn.sigmoid(y)
        dy = ds_ref[...] * (sg * (1.0 + y * (1.0 - sg)))
        rows = lax.broadcasted_iota(jnp.int32, u.shape, 0)
        du = dy * wv[CONV_K - 1:CONV_K]
        for dd in range(1, CONV_K):
            ahead = jnp.where(rows < t - dd, pltpu.roll(dy, t - dd, 0), 0.0)
            du = du + ahead * wv[CONV_K - 1 - dd:CONV_K - dd]
        du_ref[...] = du.astype(du_ref.dtype)
        dws = [jnp.sum(dy * taps[CONV_K - 1 - j], axis=0, keepdims=True) for j in range(CONV_K)]
        dw_ref[...] = jnp.concatenate(dws, axis=0)

    return pl.pallas_call(
        body, name=name, grid=(ch // LANES,),
        in_specs=[pl.BlockSpec((t, LANES), lambda j: (0, col0 + j)), pl.BlockSpec((CONV_K, LANES), lambda j: (0, j)),
                  pl.BlockSpec((t, LANES), lambda j: (0, j))],
        out_specs=[pl.BlockSpec((t, LANES), lambda j: (0, j)), pl.BlockSpec((CONV_K, LANES), lambda j: (0, j))],
        out_shape=[jax.ShapeDtypeStruct((t, ch), BF16), jax.ShapeDtypeStruct((CONV_K, ch), F32)],
        compiler_params=_params(("parallel",)),
    )(proj, conv_w, ds)


def _hg_block(st, q, fl, vi, g, l0, l1, nw):
    tb = q.shape[0]
    ln = HG_SUB
    lb = jax.nn.sigmoid(l0 - l1)
    rows = lax.broadcasted_iota(jnp.int32, (ln, HEAD), 0)
    tri = (lax.broadcasted_iota(jnp.int32, (ln, ln), 0) >= lax.broadcasted_iota(jnp.int32, (ln, ln), 1)).astype(F32)
    outs = []
    for i in range(tb // ln):
        qs, fs, vs = q[i * ln:(i + 1) * ln], fl[i * ln:(i + 1) * ln], vi[i * ln:(i + 1) * ln]
        f = lb + (1.0 - lb) * jax.nn.sigmoid(fs)
        k = 1.0 - f
        b = jnp.dot(tri, jnp.log(f), precision=HI, preferred_element_type=F32)
        o = lax.dot_general((qs * jnp.exp(b)).astype(BF16), st.astype(BF16), NT_DIMS, preferred_element_type=F32)
        for s in range(ln):
            e = jnp.exp(jnp.where(rows >= s, b - b[s:s + 1], -1e30))
            a = jnp.sum(qs * e * k[s:s + 1], axis=-1, keepdims=True)
            o = o + a * vs[s:s + 1]
        bl = b[ln - 1:ln]
        kt = k * jnp.exp(bl - b)
        st = st * jnp.exp(bl) + lax.dot_general(vs.astype(BF16), kt.astype(BF16), TN_DIMS, preferred_element_type=F32)
        outs.append(o)
    o = jnp.concatenate(outs, axis=0)
    out = o * _rms_scale(o) * nw * _silu(g)
    return st, out


def _hg_in_specs(n_heads, tb, time_index):
    cols = [pl.BlockSpec((tb, HEAD), functools.partial(lambda base, h, j: (time_index(j), base + h), base))
            for base in (0, n_heads, 2 * n_heads, 3 * n_heads)]
    head_row = pl.BlockSpec((None, 1, HEAD), lambda h, j: (h, 0, 0))
    return cols + [head_row, head_row, pl.BlockSpec((1, HEAD), lambda h, j: (0, 0))]


def _hgrn2_fwd(proj, l0, l1, nw, n_heads, name):
    t = proj.shape[0]
    tb = _tile(t, HG_BLOCK, HG_SUB)
    nb = t // tb

    def body(q_ref, f_ref, i_ref, g_ref, l0_ref, l1_ref, nw_ref, o_ref, save_ref, st_ref):
        @pl.when(pl.program_id(1) == 0)
        def _():
            st_ref[...] = jnp.zeros_like(st_ref)

        st = st_ref[...]
        save_ref[...] = st
        st, out = _hg_block(st, q_ref[...], f_ref[...], i_ref[...], g_ref[...], l0_ref[...], l1_ref[...], nw_ref[...])
        st_ref[...] = st
        o_ref[...] = out.astype(o_ref.dtype)

    return pl.pallas_call(
        body, name=name, grid=(n_heads, nb), in_specs=_hg_in_specs(n_heads, tb, lambda j: j),
        out_specs=[pl.BlockSpec((tb, HEAD), lambda h, j: (j, h)),
                   pl.BlockSpec((None, None, HEAD, HEAD), lambda h, j: (h, j, 0, 0))],
        out_shape=[jax.ShapeDtypeStruct((t, n_heads * HEAD), BF16), jax.ShapeDtypeStruct((n_heads, nb, HEAD, HEAD), F32)],
        scratch_shapes=[pltpu.VMEM((HEAD, HEAD), F32)], compiler_params=_params(("arbitrary", "arbitrary")),
    )(proj, proj, proj, proj, l0, l1, nw)


def _hgrn2_bwd(proj, l0, l1, nw, saved, d_ocat, n_heads, name):
    t = proj.shape[0]
    tb = _tile(t, HG_BLOCK, HG_SUB)
    nb = t // tb
    rev = lambda j: nb - 1 - j

    def body(q_ref, f_ref, i_ref, g_ref, l0_ref, l1_ref, nw_ref, save_ref, do_ref,
             dq_ref, df_ref, di_ref, dg_ref, dl0_ref, dl1_ref, dnw_ref, dst_ref):
        h, j = pl.program_id(0), pl.program_id(1)

        @pl.when(j == 0)
        def _():
            dst_ref[...] = jnp.zeros_like(dst_ref)
            dl0_ref[...] = jnp.zeros_like(dl0_ref)
            dl1_ref[...] = jnp.zeros_like(dl1_ref)

        @pl.when((j == 0) & (h == 0))
        def _():
            dnw_ref[...] = jnp.zeros_like(dnw_ref)

        _, vjp = jax.vjp(_hg_block, save_ref[...], q_ref[...], f_ref[...], i_ref[...], g_ref[...],
                         l0_ref[...], l1_ref[...], nw_ref[...])
        dst, dq, df, di, dg, dl0, dl1, dnw = vjp((dst_ref[...], do_ref[...]))
        dst_ref[...] = dst
        dq_ref[...] = dq.astype(dq_ref.dtype)
        df_ref[...] = df.astype(df_ref.dtype)
        di_ref[...] = di.astype(di_ref.dtype)
        dg_ref[...] = dg.astype(dg_ref.dtype)
        dl0_ref[...] += dl0
        dl1_ref[...] += dl1
        dnw_ref[...] += dnw

    col_out = pl.BlockSpec((tb, HEAD), lambda h, j: (rev(j), h))
    head_row = pl.BlockSpec((None, 1, HEAD), lambda h, j: (h, 0, 0))
    col_shape = jax.ShapeDtypeStruct((t, n_heads * HEAD), BF16)
    return pl.pallas_call(
        body, name=name, grid=(n_heads, nb),
        in_specs=_hg_in_specs(n_heads, tb, rev) + [pl.BlockSpec((None, None, HEAD, HEAD), lambda h, j: (h, rev(j), 0, 0)),
                                                   pl.BlockSpec((tb, HEAD), lambda h, j: (rev(j), h))],
        out_specs=[col_out] * 4 + [head_row, head_row, pl.BlockSpec((1, HEAD), lambda h, j: (0, 0))],
        out_shape=[col_shape] * 4 + [jax.ShapeDtypeStruct((n_heads, 1, HEAD), F32)] * 2 + [jax.ShapeDtypeStruct((1, HEAD), F32)],
        scratch_shapes=[pltpu.VMEM((HEAD, HEAD), F32)], compiler_params=_params(("arbitrary", "arbitrary")),
    )(proj, proj, proj, proj, l0, l1, nw, saved, d_ocat)


def _inv_unit_lower_raw(m):
    c = m.shape[0]
    r = lax.broadcasted_iota(jnp.int32, (c, c), 0)
    q = lax.broadcasted_iota(jnp.int32, (c, c), 1)
    eye = (r == q).astype(F32)
    dot = functools.partial(jnp.dot, precision=HI, preferred_element_type=F32)
    md = jnp.where((r // GDN_INV_BLOCK) == (q // GDN_INV_BLOCK), m, 0.0)
    x = -md
    t16 = eye + x
    p = x
    for _ in range(int(math.log2(GDN_INV_BLOCK)) - 1):
        p = dot(p, p)
        t16 = t16 + dot(t16, p)
    y = -dot(t16, m - md)
    t2 = eye + y
    p = y
    for _ in range(int(math.log2(c // GDN_INV_BLOCK)) - 1):
        p = dot(p, p)
        t2 = t2 + dot(t2, p)
    return dot(t2, t16)


@jax.custom_vjp
def _inv_unit_lower(m):
    return _inv_unit_lower_raw(m)


def _inv_fwd(m):
    t = _inv_unit_lower_raw(m)
    return t, t


def _inv_bwd(t, dt):
    inner = lax.dot_general(t, dt, TN_DIMS, precision=HI, preferred_element_type=F32)
    return (-lax.dot_general(inner, t, NT_DIMS, precision=HI, preferred_element_type=F32),)


_inv_unit_lower.defvjp(_inv_fwd, _inv_bwd)


def _gdn_block(inverse, oh_a, oh_b, st, qc, kc, vc, g, ab, alog_row, dtb_row, nw):
    c = qc.shape[0]
    a = jnp.sum(ab * oh_a, axis=-1, keepdims=True)
    bb = jnp.sum(ab * oh_b, axis=-1, keepdims=True)
    alog = jnp.sum(alog_row * oh_a, axis=-1, keepdims=True)
    dtb = jnp.sum(dtb_row * oh_a, axis=-1, keepdims=True)
    la = -jnp.exp(alog) * _softplus(a + dtb)
    beta = jax.nn.sigmoid(bb)
    q = qc * lax.rsqrt(jnp.sum(qc * qc, axis=-1, keepdims=True) + EPS) * (HEAD ** -0.5)
    k = kc * lax.rsqrt(jnp.sum(kc * kc, axis=-1, keepdims=True) + EPS)
    r = lax.broadcasted_iota(jnp.int32, (c, c), 0)
    s = lax.broadcasted_iota(jnp.int32, (c, c), 1)
    tri = (r >= s).astype(F32)
    g_cc = jnp.dot(tri, jnp.broadcast_to(la, (c, c)), precision=HI, preferred_element_type=F32)
    g_cl = jnp.dot(tri, jnp.broadcast_to(la, (c, HEAD)), precision=HI, preferred_element_type=F32)
    gamma = jnp.exp(jnp.where(r >= s, g_cc - g_cc.T, -1e30))
    kb = k.astype(BF16)
    kk = lax.dot_general(kb, kb, NT_DIMS, preferred_element_type=F32)
    m = jnp.where(r > s, beta * kk * gamma, 0.0)
    tm = inverse(m)
    eg = jnp.exp(g_cl)
    rhs = jnp.concatenate([vc * beta, k * (beta * eg)], axis=1)
    sol = jnp.dot(tm, rhs, precision=HI, preferred_element_type=F32)
    u, w = sol[:, :HEAD], sol[:, HEAD:]
    qk = lax.dot_general(q.astype(BF16), kb, NT_DIMS, preferred_element_type=F32) * gamma
    g_last = g_cl[c - 1:c]
    k_tail = k * jnp.exp(g_last - g_cl)
    stb = st.astype(BF16)
    v_new = u - lax.dot_general(w.astype(BF16), stb, NT_DIMS, preferred_element_type=F32)
    o = (lax.dot_general((q * eg).astype(BF16), stb, NT_DIMS, preferred_element_type=F32)
         + jnp.dot(qk.astype(BF16), v_new.astype(BF16), preferred_element_type=F32))
    st = st * jnp.exp(g_last) + lax.dot_general(v_new.astype(BF16), k_tail.astype(BF16), TN_DIMS, preferred_element_type=F32)
    out = o * _rms_scale(o) * nw * _silu(g)
    return st, out


def _head_onehots(n_heads):
    lane = lax.broadcasted_iota(jnp.int32, (1, LANES), 1)
    h = pl.program_id(0)
    return (lane == h).astype(F32), (lane == n_heads + h).astype(F32)


def _gdn_in_specs(n_heads, c, time_index):
    qkv = [pl.BlockSpec((c, HEAD), functools.partial(lambda base, h, j: (time_index(j), base + h), base))
           for base in (0, n_heads, 2 * n_heads)]
    row = pl.BlockSpec((1, LANES), lambda h, j: (0, 0))
    return qkv + [pl.BlockSpec((c, HEAD), lambda h, j: (time_index(j), 7 * n_heads + h)),
                  pl.BlockSpec((c, LANES), lambda h, j: (time_index(j), 8 * n_heads)), row, row, row]


def _gdn_fwd(qkv, proj, alog_row, dtb_row, nw, n_heads, name):
    t = qkv.shape[0]
    c = _tile(t, GDN_CHUNK, GDN_CHUNK)
    nb = t // c

    def body(q_ref, k_ref, v_ref, g_ref, ab_ref, al_ref, dt_ref, nw_ref, o_ref, save_ref, st_ref):
        @pl.when(pl.program_id(1) == 0)
        def _():
            st_ref[...] = jnp.zeros_like(st_ref)

        oh_a, oh_b = _head_onehots(n_heads)
        st = st_ref[...]
        save_ref[...] = st
        st, out = _gdn_block(_inv_unit_lower_raw, oh_a, oh_b, st, q_ref[...], k_ref[...], v_ref[...], g_ref[...],
                             ab_ref[...], al_ref[...], dt_ref[...], nw_ref[...])
        st_ref[...] = st
        o_ref[...] = out.astype(o_ref.dtype)

    return pl.pallas_call(
        body, name=name, grid=(n_heads, nb), in_specs=_gdn_in_specs(n_heads, c, lambda j: j),
        out_specs=[pl.BlockSpec((c, HEAD), lambda h, j: (j, h)),
                   pl.BlockSpec((None, None, HEAD, HEAD), lambda h, j: (h, j, 0, 0))],
        out_shape=[jax.ShapeDtypeStruct((t, n_heads * HEAD), BF16), jax.ShapeDtypeStruct((n_heads, nb, HEAD, HEAD), F32)],
        scratch_shapes=[pltpu.VMEM((HEAD, HEAD), F32)], compiler_params=_params(("arbitrary", "arbitrary")),
    )(qkv, qkv, qkv, proj, proj, alog_row, dtb_row, nw)


def _gdn_bwd(qkv, proj, alog_row, dtb_row, nw, saved, d_ocat, n_heads, name):
    t = qkv.shape[0]
    c = _tile(t, GDN_CHUNK, GDN_CHUNK)
    nb = t // c
    rev = lambda j: nb - 1 - j

    def body(q_ref, k_ref, v_ref, g_ref, ab_ref, al_ref, dt_ref, nw_ref, save_ref, do_ref,
             dq_ref, dk_ref, dv_ref, dg_ref, dab_ref, dal_ref, ddt_ref, dnw_ref, dst_ref):
        h, j = pl.program_id(0), pl.program_id(1)

        @pl.when(j == 0)
        def _():
            dst_ref[...] = jnp.zeros_like(dst_ref)

        @pl.when((j == 0) & (h == 0))
        def _():
            dab_ref[...] = jnp.zeros_like(dab_ref)
            dal_ref[...] = jnp.zeros_like(dal_ref)
            ddt_ref[...] = jnp.zeros_like(ddt_ref)
            dnw_ref[...] = jnp.zeros_like(dnw_ref)

        oh_a, oh_b = _head_onehots(n_heads)
        fn = functools.partial(_gdn_block, _inv_unit_lower, oh_a, oh_b)
        _, vjp = jax.vjp(fn, save_ref[...], q_ref[...], k_ref[...], v_ref[...], g_ref[...], ab_ref[...],
                         al_ref[...], dt_ref[...], nw_ref[...])
        dst, dq, dk, dv, dg, dab, dal, ddt, dnw = vjp((dst_ref[...], do_ref[...]))
        dst_ref[...] = dst
        dq_ref[...] = dq
        dk_ref[...] = dk
        dv_ref[...] = dv
        dg_ref[...] = dg.astype(dg_ref.dtype)
        rows = pl.ds(pl.multiple_of(rev(j) * c, c), c)
        dab_ref[rows, :] += dab
        dal_ref[...] += dal
        ddt_ref[...] += ddt
        dnw_ref[...] += dnw

    col_out = pl.BlockSpec((c, HEAD), lambda h, j: (rev(j), h))
    row = pl.BlockSpec((1, LANES), lambda h, j: (0, 0))
    col_f32 = jax.ShapeDtypeStruct((t, n_heads * HEAD), F32)
    return pl.pallas_call(
        body, name=name, grid=(n_heads, nb),
        in_specs=_gdn_in_specs(n_heads, c, rev) + [pl.BlockSpec((None, None, HEAD, HEAD), lambda h, j: (h, rev(j), 0, 0)),
                                                   pl.BlockSpec((c, HEAD), lambda h, j: (rev(j), n_heads + h))],
        out_specs=[col_out] * 4 + [pl.BlockSpec((t, LANES), lambda h, j: (0, 0)), row, row, row],
        out_shape=[col_f32] * 3 + [jax.ShapeDtypeStruct((t, n_heads * HEAD), BF16), jax.ShapeDtypeStruct((t, LANES), F32)]
        + [jax.ShapeDtypeStruct((1, LANES), F32)] * 3,
        scratch_shapes=[pltpu.VMEM((HEAD, HEAD), F32)], compiler_params=_params(("arbitrary", "arbitrary")),
    )(qkv, qkv, qkv, proj, proj, alog_row, dtb_row, nw, saved, d_ocat)


def _pad_lanes(v, n):
    v = v.reshape(1, -1)
    return jnp.pad(v, ((0, 0), (0, n - v.shape[1])))


def _pack_rows(vecs):
    flat = jnp.concatenate([v.reshape(-1) for v in vecs])
    offs, o = [], 0
    for v in vecs:
        offs.append((o, v.size))
        o += v.size
    per_row = -(-o // (SUBLANES * LANES)) * LANES
    flat = jnp.pad(flat, (0, SUBLANES * per_row - o))
    return flat.reshape(SUBLANES, per_row), offs


def _unpack(gathered, offs):
    per_dev = gathered.reshape(N_DEV, -1)
    return [per_dev[:, o:o + n] for o, n in offs]


def _sum_devices(part):
    acc = part[0]
    for i in range(1, N_DEV):
        acc = acc + part[i]
    return acc


def kernel(x, c, w_ada, b_ada, pre_mix_norm, post_mix_norm, pre_ffn_norm, post_ffn_norm, w_in, hg_lb_logits, hg_norm, gdn_conv_w, gdn_a_log, gdn_dt_bias, gdn_norm, w_out, w_ff1, w_ff2, loss_target, m_w_ada, m_b_ada, m_pre_mix_norm, m_post_mix_norm, m_pre_ffn_norm, m_post_ffn_norm, m_w_in, m_hg_lb_logits, m_hg_norm, m_gdn_conv_w, m_gdn_a_log, m_gdn_dt_bias, m_gdn_norm, m_w_out, m_w_ff1, m_w_ff2, v_w_ada, v_b_ada, v_pre_mix_norm, v_post_mix_norm, v_pre_ffn_norm, v_post_ffn_norm, v_w_in, v_hg_lb_logits, v_hg_norm, v_gdn_conv_w, v_gdn_a_log, v_gdn_dt_bias, v_gdn_norm, v_w_out, v_w_ff1, v_w_ff2):
    assert x.shape[0] == 1 and w_ada.shape[0] == 1 and hg_lb_logits.shape[0] == 2
    t, d = x.shape[1], x.shape[2]
    n_heads = (d // 2) // HEAD
    hw = n_heads * HEAD
    in_cols = 8 * hw + 2 * n_heads
    np_cols = 8 * hw + LANES
    d_ff = w_ff1.shape[2] * N_CHIP
    na = w_ada.shape[2]
    ax, ay, ac = lax.axis_index("x"), lax.axis_index("y"), lax.axis_index("c")
    chip = 2 * ax + ay
    dev = 4 * ax + 2 * ay + ac

    x2d, tgt = x[0], loss_target[0]

    pack1, offs1 = _pack_rows([c[0], gdn_conv_w[0]])
    c_all, convw_all = _unpack(_gather8(pack1, "gather_cond"), offs1)
    conv_sh = gdn_conv_w.shape[2]
    conv_w = jnp.concatenate([convw_all[2 * j].reshape(CONV_K, conv_sh) for j in range(N_CHIP)], axis=1)

    b_s = lax.dynamic_slice(b_ada, (0, chip * na), (1, na))
    mod_part = _mod_part(c_all, w_ada[0], b_s, "mod_part")
    pack2, offs2 = _pack_rows([mod_part])
    (mod_parts,) = _unpack(_gather8(pack2, "gather_mod"), offs2)
    mod_all = jnp.concatenate([mod_parts[2 * j].reshape(N_DEV, na) for j in range(N_CHIP)], axis=1)
    mod = lax.dynamic_slice(mod_all, (dev, 0), (1, N_MOD * d))
    sh_m, sc_m, gt_m, sh_f, sc_f, gt_f = [mod[:, i * d:(i + 1) * d] for i in range(N_MOD)]

    g_in, g_out, g_ff1, g_ff2 = _chip_exchange(
        [w_in[0].astype(BF16), w_out[0].astype(BF16), w_ff1[0].astype(BF16), w_ff2[0].astype(BF16)], True, "gather_weights")
    w_in_f = jnp.pad(jnp.transpose(g_in, (1, 0, 2)).reshape(d, in_cols), ((0, 0), (0, np_cols - in_cols)))
    w_out_f = g_out.reshape(d, d)
    w_ff1_f = jnp.transpose(g_ff1, (1, 0, 2)).reshape(d, d_ff)
    w_ff2_f = g_ff2.reshape(d_ff, d)

    h1 = _norm_mod(x2d, pre_mix_norm, sc_m, sh_m, "norm_mod_mix")
    proj = _matmul(h1, w_in_f, "nn", F32, "mm_in", tn=640)
    l0, l1 = hg_lb_logits[0].reshape(n_heads, 1, HEAD), hg_lb_logits[1].reshape(n_heads, 1, HEAD)
    o_hg, hg_saved = _hgrn2_fwd(proj, l0, l1, hg_norm, n_heads, "hgrn2_fwd")
    qkv = _conv_fwd(proj, conv_w, 4 * n_heads, "conv_fwd")
    alog_row, dtb_row = _pad_lanes(gdn_a_log, LANES), _pad_lanes(gdn_dt_bias, LANES)
    o_gdn, gdn_saved = _gdn_fwd(qkv, proj, alog_row, dtb_row, gdn_norm, n_heads, "gdn_fwd")
    o_cat = jnp.concatenate([o_hg, o_gdn], axis=1)
    y1 = _matmul(o_cat, w_out_f, "nn", F32, "mm_out")
    x_mid = _resid(x2d, y1, post_mix_norm, gt_m, "resid_mix")

    h2 = _norm_mod(x_mid, pre_ffn_norm, sc_f, sh_f, "norm_mod_ffn")
    a1 = _matmul(h2, w_ff1_f, "nn", F32, "mm_ff1")
    r1 = _relu2(a1, "relu2")
    y2 = _matmul(r1, w_ff2_f, "nn", F32, "mm_ff2")
    d_out, loss_row = _loss_head(x_mid, y2, post_ffn_norm, gt_f, tgt, "loss_head")

    dy2, d_gt_f, d_post_ffn = _resid_bwd(d_out, y2, post_ffn_norm, gt_f, "resid_ffn_bwd")
    gw_ff2 = _matmul(r1, dy2, "tn", F32, "mm_ff2_dw")
    dr1 = _matmul(dy2, w_ff2_f, "nt", BF16, "mm_ff2_dx")
    da1 = _relu2_bwd(a1, dr1, "relu2_bwd")
    gw_ff1 = _matmul(h2, da1, "tn", F32, "mm_ff1_dw")
    dh2 = _matmul(da1, w_ff1_f, "nt", BF16, "mm_ff1_dx")
    d_mid, d_pre_ffn, d_sc_f, d_sh_f = _norm_mod_bwd(x_mid, pre_ffn_norm, sc_f, dh2, d_out, "norm_mod_ffn_bwd")

    dy1, d_gt_m, d_post_mix = _resid_bwd(d_mid, y1, post_mix_norm, gt_m, "resid_mix_bwd")
    gw_out = _matmul(o_cat, dy1, "tn", F32, "mm_out_dw")
    d_ocat = _matmul(dy1, w_out_f, "nt", F32, "mm_out_dx")
    dq_h, df_h, di_h, dg_h, dl0, dl1, d_hg_norm = _hgrn2_bwd(proj, l0, l1, hg_norm, hg_saved, d_ocat, n_heads, "hgrn2_bwd")
    dqc, dkc, dvc, dg_g, dab, d_alog, d_dtb, d_gdn_norm = _gdn_bwd(
        qkv, proj, alog_row, dtb_row, gdn_norm, gdn_saved, d_ocat, n_heads, "gdn_bwd")
    du, d_conv_w = _conv_bwd(proj, conv_w, jnp.concatenate([dqc, dkc, dvc], axis=1), 4 * n_heads, "conv_bwd")
    dproj = jnp.concatenate([dq_h, df_h, di_h, dg_h, du, dg_g, dab.astype(BF16)], axis=1)
    gw_in = _matmul(h1, dproj, "tn", F32, "mm_in_dw", tn=640)
    dh1 = _matmul(dproj, w_in_f, "nt", BF16, "mm_in_dx", tk=640)
    grad_x, d_pre_mix, d_sc_m, d_sh_m = _norm_mod_bwd(x2d, pre_mix_norm, sc_m, dh1, d_mid, "norm_mod_mix_bwd")

    d_mod = jnp.concatenate([d_sh_m, d_sc_m, d_gt_m, d_sh_f, d_sc_f, d_gt_f], axis=1)
    d_lb_logits = jnp.stack([dl0.reshape(n_heads, HEAD), dl1.reshape(n_heads, HEAD)])
    pack3, offs3 = _pack_rows([loss_row[0, :1], d_pre_mix, d_post_mix, d_pre_ffn, d_post_ffn, d_lb_logits, d_hg_norm,
                               d_conv_w, d_alog[0, :n_heads], d_dtb[0, :n_heads], d_gdn_norm, d_mod])
    parts = _unpack(_gather8(pack3, "gather_vec_grads"), offs3)
    sums = [_sum_devices(p) for p in parts[:-1]]
    loss = sums[0][0]
    dmod_all = parts[-1]
    g_b_ada = _sum_devices(dmod_all).reshape(1, N_MOD * d)
    g_conv_full = sums[7].reshape(CONV_K, N_CHIP * conv_sh)
    g_conv = lax.dynamic_slice(g_conv_full, (0, chip * conv_sh), (CONV_K, conv_sh))
    gw_ada = _wada_grad(c_all, lax.dynamic_slice(dmod_all, (0, chip * na), (N_DEV, na)), "wada_grad")

    in_sh = in_cols // N_CHIP
    ff_sh = d_ff // N_CHIP
    send = [jnp.transpose(gw_in[:, :in_cols].reshape(d, N_CHIP, in_sh), (1, 0, 2)).astype(BF16),
            gw_out.reshape(N_CHIP, d // N_CHIP, d).astype(BF16),
            jnp.transpose(gw_ff1.reshape(d, N_CHIP, ff_sh), (1, 0, 2)).astype(BF16),
            gw_ff2.reshape(N_CHIP, ff_sh, d).astype(BF16)]
    recv = _chip_exchange(send, False, "scatter_grads")
    mine = [_sum_chips(rv, f"sum_chips_{i}") for i, rv in enumerate(recv)]
    theirs = _sibling_exchange(mine, "sibling_grads")

    big = {}
    for i, (nm, w_, m_, v_) in enumerate([("w_in", w_in, m_w_in, v_w_in), ("w_out", w_out, m_w_out, v_w_out),
                                          ("w_ff1", w_ff1, m_w_ff1, v_w_ff1), ("w_ff2", w_ff2, m_w_ff2, v_w_ff2)]):
        big[nm] = [o[None] for o in _adamw(w_[0], [mine[i], theirs[i]], m_[0], v_[0], f"adamw_{nm}")]
    big["w_ada"] = [o[None] for o in _adamw(w_ada[0], [gw_ada], m_w_ada[0], v_w_ada[0], "adamw_w_ada")]

    small_names = ["b_ada", "pre_mix_norm", "post_mix_norm", "pre_ffn_norm", "post_ffn_norm", "hg_lb_logits", "hg_norm",
                   "gdn_conv_w", "gdn_a_log", "gdn_dt_bias", "gdn_norm"]
    small_w = [b_ada, pre_mix_norm, post_mix_norm, pre_ffn_norm, post_ffn_norm, hg_lb_logits, hg_norm, gdn_conv_w,
               gdn_a_log, gdn_dt_bias, gdn_norm]
    small_m = [m_b_ada, m_pre_mix_norm, m_post_mix_norm, m_pre_ffn_norm, m_post_ffn_norm, m_hg_lb_logits, m_hg_norm,
               m_gdn_conv_w, m_gdn_a_log, m_gdn_dt_bias, m_gdn_norm]
    small_v = [v_b_ada, v_pre_mix_norm, v_post_mix_norm, v_pre_ffn_norm, v_post_ffn_norm, v_hg_lb_logits, v_hg_norm,
               v_gdn_conv_w, v_gdn_a_log, v_gdn_dt_bias, v_gdn_norm]
    small_g = [g_b_ada, sums[1], sums[2], sums[3], sums[4], sums[5], sums[6], g_conv, sums[8], sums[9], sums[10]]
    pw, offs_s = _pack_rows(small_w)
    pg, _ = _pack_rows(small_g)
    pm, _ = _pack_rows(small_m)
    pv, _ = _pack_rows(small_v)
    packed = _adamw(pw, [pg], pm, pv, "adamw_vectors")
    small = {}
    for nm, w_, (o, n) in zip(small_names, small_w, offs_s):
        small[nm] = [p.reshape(-1)[o:o + n].reshape(w_.shape) for p in packed]

    order = ["w_ada", "b_ada", "pre_mix_norm", "post_mix_norm", "pre_ffn_norm", "post_ffn_norm", "w_in", "hg_lb_logits",
             "hg_norm", "gdn_conv_w", "gdn_a_log", "gdn_dt_bias", "gdn_norm", "w_out", "w_ff1", "w_ff2"]
    res = {**big, **small}
    outs = [loss, grad_x[None]]
    for k in range(4):
        outs += [res[nm][k] for nm in order]
    return tuple(outs)
```

```python
import functools
import math

import jax
import jax.numpy as jnp
from jax import lax
from jax.experimental import pallas as pl
from jax.experimental.pallas import tpu as pltpu

F32 = jnp.float32
BF16 = jnp.bfloat16
HI = lax.Precision.HIGHEST
MESH = pl.DeviceIdType.MESH

LANES = 128
SUBLANES = 8
VMEM_LIMIT = 48 * 1024 * 1024
EPS = 1e-6
HEAD = 128
CONV_K = 4
GDN_CHUNK = 64
GDN_INV_BLOCK = 16
HG_SUB = 16
HG_BLOCK = 128
HEAD_GROUP = 8
N_MOD = 6
N_DEV = 8
N_CHIP = 4

ADAM_LR = 0.001
ADAM_B1 = 0.9
ADAM_B2 = 0.999
ADAM_EPS = 1e-08
ADAM_WD = 0.01
ADAM_STEP = 10

NT_DIMS = (((1,), (1,)), ((), ()))
TN_DIMS = (((0,), (0,)), ((), ()))


def _tile(dim, target, align):
    if dim <= target:
        return dim
    best = dim
    t = align
    while t <= target:
        if dim % t == 0:
            best = t
        t += align
    return best


def _elementwise_tiles(r, c):
    tc = _tile(c, 1024, LANES)
    tr = _tile(r, max(16, (256 * 1024) // tc // 16 * 16), 16)
    return tr, tc


def _params(sem):
    return pltpu.CompilerParams(dimension_semantics=sem, vmem_limit_bytes=VMEM_LIMIT)


def _silu(x):
    return x * jax.nn.sigmoid(x)


def _softplus(x):
    pos = x > 0
    return jnp.where(pos, x, 0.0) + jnp.log(1.0 + jnp.exp(jnp.where(pos, -x, x)))


def _rms_scale(x):
    return lax.rsqrt(jnp.mean(x * x, axis=-1, keepdims=True) + EPS)


def _gather8(x_shard, name):
    m_per, n = x_shard.shape
    assert m_per % SUBLANES == 0 and n % LANES == 0

    def body(x_ref, out_ref, send_sems, recv_sems, local_sem):
        x, y, c = lax.axis_index("x"), lax.axis_index("y"), lax.axis_index("c")
        me, sibling = (x, y, c), (x, y, 1 - c)
        chips = [(1 - x, y), (x, 1 - y), (1 - x, 1 - y)]

        def rows(px, py, pc):
            return out_ref.at[pl.ds((4 * px + 2 * py + pc) * m_per, m_per), :]

        def copy(k, block, to, src=None):
            return pltpu.make_async_remote_copy(
                src_ref=rows(*block) if src is None else src, dst_ref=rows(*block),
                send_sem=send_sems.at[k], recv_sem=recv_sems.at[k], device_id=to, device_id_type=MESH)

        mine = pltpu.make_async_copy(x_ref, rows(*me), local_sem)
        mine.start()
        first = [copy(0, me, sibling, src=x_ref)]
        first += [copy(1 + j, me, (*chip, c), src=x_ref) for j, chip in enumerate(chips)]
        for cp in first:
            cp.start()
        passed = [copy(4 + j, (*chip, c), sibling) for j, chip in enumerate(chips)]
        for j, chip in enumerate(chips):
            copy(1 + j, (*chip, c), me).wait_recv()
            passed[j].start()
        copy(0, sibling, me).wait_recv()
        for j, chip in enumerate(chips):
            copy(4 + j, (*chip, 1 - c), me).wait_recv()
        for cp in first + passed:
            cp.wait_send()
        mine.wait()

    return pl.pallas_call(
        body, name=name,
        out_shape=jax.ShapeDtypeStruct((N_DEV * m_per, n), x_shard.dtype),
        in_specs=[pl.BlockSpec(memory_space=pltpu.VMEM)],
        out_specs=pl.BlockSpec(memory_space=pltpu.VMEM),
        scratch_shapes=[pltpu.SemaphoreType.DMA((7,)), pltpu.SemaphoreType.DMA((7,)), pltpu.SemaphoreType.DMA],
        compiler_params=pltpu.CompilerParams(vmem_limit_bytes=VMEM_LIMIT),
    )(x_shard)


def _chip_exchange(arrs, bcast, name):
    n = len(arrs)
    out_shapes = [jax.ShapeDtypeStruct((N_CHIP,) + (a.shape if bcast else a.shape[1:]), a.dtype) for a in arrs]

    def body(*refs):
        ins, outs = refs[:n], refs[n:2 * n]
        send_sems, recv_sems, local_sems = refs[2 * n:]
        x, y, c = lax.axis_index("x"), lax.axis_index("y"), lax.axis_index("c")
        me = 2 * x + y
        peers = [(1 - x, y), (x, 1 - y), (1 - x, 1 - y)]
        copies = []
        for a in range(n):
            src_own = ins[a] if bcast else ins[a].at[me]
            loc = pltpu.make_async_copy(src_own, outs[a].at[me], local_sems.at[a])
            loc.start()
            copies.append(loc)
        remote = []
        for a in range(n):
            for k, (px, py) in enumerate(peers):
                them = 2 * px + py
                cp = pltpu.make_async_remote_copy(
                    src_ref=ins[a] if bcast else ins[a].at[them], dst_ref=outs[a].at[me],
                    send_sem=send_sems.at[3 * a + k], recv_sem=recv_sems.at[3 * a + k],
                    device_id=(px, py, c), device_id_type=MESH)
                cp.start()
                remote.append((cp, a, k, them))
        for cp, a, k, them in remote:
            pltpu.make_async_remote_copy(
                src_ref=ins[a] if bcast else ins[a].at[them], dst_ref=outs[a].at[them],
                send_sem=send_sems.at[3 * a + k], recv_sem=recv_sems.at[3 * a + k],
                device_id=(x, y, c), device_id_type=MESH).wait_recv()
        for cp, a, k, them in remote:
            cp.wait_send()
        for loc in copies:
            loc.wait()

    hbm = pl.BlockSpec(memory_space=pltpu.HBM)
    return pl.pallas_call(
        body, name=name, out_shape=out_shapes, in_specs=[hbm] * n, out_specs=[hbm] * n,
        scratch_shapes=[pltpu.SemaphoreType.DMA((3 * n,)), pltpu.SemaphoreType.DMA((3 * n,)), pltpu.SemaphoreType.DMA((n,))],
    )(*arrs)


def _sibling_exchange(arrs, name):
    n = len(arrs)

    def body(*refs):
        ins, outs = refs[:n], refs[n:2 * n]
        send_sems, recv_sems = refs[2 * n:]
        sibling = (lax.axis_index("x"), lax.axis_index("y"), 1 - lax.axis_index("c"))
        cps = []
        for a in range(n):
            cp = pltpu.make_async_remote_copy(src_ref=ins[a], dst_ref=outs[a], send_sem=send_sems.at[a],
                                              recv_sem=recv_sems.at[a], device_id=sibling, device_id_type=MESH)
            cp.start()
            cps.append(cp)
        for cp in cps:
            cp.wait_recv()
        for cp in cps:
            cp.wait_send()

    hbm = pl.BlockSpec(memory_space=pltpu.HBM)
    return pl.pallas_call(
        body, name=name, out_shape=[jax.ShapeDtypeStruct(a.shape, a.dtype) for a in arrs],
        in_specs=[hbm] * n, out_specs=[hbm] * n,
        scratch_shapes=[pltpu.SemaphoreType.DMA((n,)), pltpu.SemaphoreType.DMA((n,))],
    )(*arrs)


def _matmul(a, b, mode, out_dtype, name, tm=1024, tn=1024, tk=512):
    if mode == "nn":
        (m, k), (k2, n) = a.shape, b.shape
    elif mode == "nt":
        (m, k), (n, k2) = a.shape, b.shape
    else:
        (k, m), (k2, n) = a.shape, b.shape
    assert k == k2, (a.shape, b.shape, mode)
    tm, tn, tk = _tile(m, tm, LANES), _tile(n, tn, LANES), _tile(k, tk, LANES)
    nk = k // tk

    def body(a_ref, b_ref, o_ref, acc_ref):
        kk = pl.program_id(2)

        @pl.when(kk == 0)
        def _():
            acc_ref[...] = jnp.zeros_like(acc_ref)

        if mode == "nn":
            acc_ref[...] += jnp.dot(a_ref[...], b_ref[...], preferred_element_type=F32)
        elif mode == "nt":
            acc_ref[...] += lax.dot_general(a_ref[...], b_ref[...], NT_DIMS, preferred_element_type=F32)
        else:
            acc_ref[...] += lax.dot_general(a_ref[...], b_ref[...], TN_DIMS, preferred_element_type=F32)

        @pl.when(kk == nk - 1)
        def _():
            o_ref[...] = acc_ref[...].astype(o_ref.dtype)

    if mode == "nn":
        a_spec = pl.BlockSpec((tm, tk), lambda i, j, kk: (i, kk))
        b_spec = pl.BlockSpec((tk, tn), lambda i, j, kk: (kk, j))
    elif mode == "nt":
        a_spec = pl.BlockSpec((tm, tk), lambda i, j, kk: (i, kk))
        b_spec = pl.BlockSpec((tn, tk), lambda i, j, kk: (j, kk))
    else:
        a_spec = pl.BlockSpec((tk, tm), lambda i, j, kk: (kk, i))
        b_spec = pl.BlockSpec((tk, tn), lambda i, j, kk: (kk, j))
    return pl.pallas_call(
        body, name=name, grid=(m // tm, n // tn, nk), in_specs=[a_spec, b_spec],
        out_specs=pl.BlockSpec((tm, tn), lambda i, j, kk: (i, j)),
        out_shape=jax.ShapeDtypeStruct((m, n), out_dtype),
        scratch_shapes=[pltpu.VMEM((tm, tn), F32)],
        compiler_params=_params(("parallel", "parallel", "arbitrary")),
    )(a, b)


def _mod_part(c_all, w_s, b_s, name):
    d, na = w_s.shape
    tn = _tile(na, 512, LANES)

    def body(c_ref, w_ref, b_ref, o_ref):
        ca = _silu(c_ref[...]).astype(BF16)
        o_ref[...] = jnp.dot(ca, w_ref[...].astype(BF16), preferred_element_type=F32) + b_ref[...]

    return pl.pallas_call(
        body, name=name, grid=(na // tn,),
        in_specs=[pl.BlockSpec((N_DEV, d), lambda j: (0, 0)), pl.BlockSpec((d, tn), lambda j: (0, j)),
                  pl.BlockSpec((1, tn), lambda j: (0, j))],
        out_specs=pl.BlockSpec((N_DEV, tn), lambda j: (0, j)),
        out_shape=jax.ShapeDtypeStruct((N_DEV, na), F32), compiler_params=_params(("parallel",)),
    )(c_all, w_s, b_s)


def _wada_grad(c_all, dmod_s, name):
    d = c_all.shape[1]
    na = dmod_s.shape[1]
    td, tn = _tile(d, 512, LANES), _tile(na, 512, LANES)

    def body(c_ref, g_ref, o_ref):
        o_ref[...] = lax.dot_general(_silu(c_ref[...]), g_ref[...], TN_DIMS, precision=HI, preferred_element_type=F32)

    return pl.pallas_call(
        body, name=name, grid=(d // td, na // tn),
        in_specs=[pl.BlockSpec((N_DEV, td), lambda i, j: (0, i)), pl.BlockSpec((N_DEV, tn), lambda i, j: (0, j))],
        out_specs=pl.BlockSpec((td, tn), lambda i, j: (i, j)),
        out_shape=jax.ShapeDtypeStruct((d, na), F32), compiler_params=_params(("parallel", "parallel")),
    )(c_all, dmod_s)


def _row_specs(tb, d, n_full, n_vec):
    full = pl.BlockSpec((tb, d), lambda i: (i, 0))
    vec = pl.BlockSpec((1, d), lambda i: (0, 0))
    return [full] * n_full + [vec] * n_vec


def _norm_mod(x, w, sc, sh, name):
    t, d = x.shape
    tb = _tile(t, 256, SUBLANES)

    def body(x_ref, w_ref, sc_ref, sh_ref, o_ref):
        xv = x_ref[...]
        o_ref[...] = (xv * _rms_scale(xv) * w_ref[...] * (1.0 + sc_ref[...]) + sh_ref[...]).astype(o_ref.dtype)

    return pl.pallas_call(
        body, name=name, grid=(t // tb,), in_specs=_row_specs(tb, d, 1, 3),
        out_specs=pl.BlockSpec((tb, d), lambda i: (i, 0)), out_shape=jax.ShapeDtypeStruct((t, d), BF16),
        compiler_params=_params(("parallel",)),
    )(x, w, sc, sh)


def _norm_mod_bwd(x, w, sc, dh, dres, name):
    t, d = x.shape
    tb = _tile(t, 256, SUBLANES)

    def body(x_ref, w_ref, sc_ref, dh_ref, dres_ref, dx_ref, dw_ref, dsc_ref, dsh_ref):
        @pl.when(pl.program_id(0) == 0)
        def _():
            dw_ref[...] = jnp.zeros_like(dw_ref)
            dsc_ref[...] = jnp.zeros_like(dsc_ref)
            dsh_ref[...] = jnp.zeros_like(dsh_ref)

        xv = x_ref[...]
        r = _rms_scale(xv)
        xn = xv * r
        g = dh_ref[...].astype(F32)
        wv, one_sc = w_ref[...], 1.0 + sc_ref[...]
        gxn = g * xn
        dsh_ref[...] += jnp.sum(g, axis=0, keepdims=True)
        dsc_ref[...] += jnp.sum(gxn, axis=0, keepdims=True) * wv
        dw_ref[...] += jnp.sum(gxn, axis=0, keepdims=True) * one_sc
        dxn = g * (wv * one_sc)
        dx_ref[...] = dres_ref[...] + r * (dxn - xn * jnp.mean(dxn * xn, axis=-1, keepdims=True))

    vec_out = pl.BlockSpec((1, d), lambda i: (0, 0))
    return pl.pallas_call(
        body, name=name, grid=(t // tb,),
        in_specs=[pl.BlockSpec((tb, d), lambda i: (i, 0)), pl.BlockSpec((1, d), lambda i: (0, 0)),
                  pl.BlockSpec((1, d), lambda i: (0, 0)), pl.BlockSpec((tb, d), lambda i: (i, 0)),
                  pl.BlockSpec((tb, d), lambda i: (i, 0))],
        out_specs=[pl.BlockSpec((tb, d), lambda i: (i, 0)), vec_out, vec_out, vec_out],
        out_shape=[jax.ShapeDtypeStruct((t, d), F32)] + [jax.ShapeDtypeStruct((1, d), F32)] * 3,
        compiler_params=_params(("arbitrary",)),
    )(x, w, sc, dh, dres)


def _resid(x, y, w, gt, name):
    t, d = x.shape
    tb = _tile(t, 256, SUBLANES)

    def body(x_ref, y_ref, w_ref, gt_ref, o_ref):
        yv = y_ref[...]
        o_ref[...] = x_ref[...] + gt_ref[...] * (yv * _rms_scale(yv) * w_ref[...])

    return pl.pallas_call(
        body, name=name, grid=(t // tb,), in_specs=_row_specs(tb, d, 2, 2),
        out_specs=pl.BlockSpec((tb, d), lambda i: (i, 0)), out_shape=jax.ShapeDtypeStruct((t, d), F32),
        compiler_params=_params(("parallel",)),
    )(x, y, w, gt)


def _loss_head(x2, y2, w, gt, target, name):
    t, d = x2.shape
    tb = _tile(t, 256, SUBLANES)

    def body(x_ref, y_ref, tg_ref, w_ref, gt_ref, do_ref, loss_ref):
        @pl.when(pl.program_id(0) == 0)
        def _():
            loss_ref[...] = jnp.zeros_like(loss_ref)

        yv = y_ref[...]
        out = x_ref[...] + gt_ref[...] * (yv * _rms_scale(yv) * w_ref[...])
        err = out - tg_ref[...]
        do_ref[...] = err * (1.0 / d)
        per_tok = jnp.mean(err * err, axis=-1, keepdims=True)
        loss_ref[...] += 0.5 * jnp.sum(per_tok, axis=0, keepdims=True)

    return pl.pallas_call(
        body, name=name, grid=(t // tb,), in_specs=_row_specs(tb, d, 3, 2),
        out_specs=[pl.BlockSpec((tb, d), lambda i: (i, 0)), pl.BlockSpec((1, LANES), lambda i: (0, 0))],
        out_shape=[jax.ShapeDtypeStruct((t, d), F32), jax.ShapeDtypeStruct((1, LANES), F32)],
        compiler_params=_params(("arbitrary",)),
    )(x2, y2, target, w, gt)


def _resid_bwd(dout, y, w, gt, name):
    t, d = y.shape
    tb = _tile(t, 256, SUBLANES)

    def body(do_ref, y_ref, w_ref, gt_ref, dy_ref, dgt_ref, dw_ref):
        @pl.when(pl.program_id(0) == 0)
        def _():
            dgt_ref[...] = jnp.zeros_like(dgt_ref)
            dw_ref[...] = jnp.zeros_like(dw_ref)

        yv, g = y_ref[...], do_ref[...]
        r = _rms_scale(yv)
        yn = yv * r
        wv, gtv = w_ref[...], gt_ref[...]
        gyn = jnp.sum(g * yn, axis=0, keepdims=True)
        dgt_ref[...] += gyn * wv
        dw_ref[...] += gyn * gtv
        dyn = g * (gtv * wv)
        dy_ref[...] = (r * (dyn - yn * jnp.mean(dyn * yn, axis=-1, keepdims=True))).astype(dy_ref.dtype)

    vec_out = pl.BlockSpec((1, d), lambda i: (0, 0))
    return pl.pallas_call(
        body, name=name, grid=(t // tb,), in_specs=_row_specs(tb, d, 2, 2),
        out_specs=[pl.BlockSpec((tb, d), lambda i: (i, 0)), vec_out, vec_out],
        out_shape=[jax.ShapeDtypeStruct((t, d), BF16)] + [jax.ShapeDtypeStruct((1, d), F32)] * 2,
        compiler_params=_params(("arbitrary",)),
    )(dout, y, w, gt)


def _relu2(a1, name):
    t, n = a1.shape
    tb, tn = _elementwise_tiles(t, n)

    def body(a_ref, o_ref):
        r = jnp.maximum(a_ref[...], 0.0)
        o_ref[...] = (r * r).astype(o_ref.dtype)

    spec = pl.BlockSpec((tb, tn), lambda i, j: (i, j))
    return pl.pallas_call(body, name=name, grid=(t // tb, n // tn), in_specs=[spec], out_specs=spec,
                          out_shape=jax.ShapeDtypeStruct((t, n), BF16), compiler_params=_params(("parallel", "parallel")))(a1)


def _relu2_bwd(a1, dr, name):
    t, n = a1.shape
    tb, tn = _elementwise_tiles(t, n)

    def body(a_ref, g_ref, o_ref):
        o_ref[...] = (2.0 * jnp.maximum(a_ref[...], 0.0) * g_ref[...]).astype(o_ref.dtype)

    spec = pl.BlockSpec((tb, tn), lambda i, j: (i, j))
    return pl.pallas_call(body, name=name, grid=(t // tb, n // tn), in_specs=[spec, spec], out_specs=spec,
                          out_shape=jax.ShapeDtypeStruct((t, n), BF16), compiler_params=_params(("parallel", "parallel")))(a1, dr)


def _sum_chips(recv, name):
    _, r, c = recv.shape
    tr, tc = _elementwise_tiles(r, c)

    def body(x_ref, o_ref):
        acc = x_ref[0].astype(F32)
        for j in range(1, N_CHIP):
            acc = acc + x_ref[j].astype(F32)
        o_ref[...] = acc

    return pl.pallas_call(
        body, name=name, grid=(r // tr, c // tc), in_specs=[pl.BlockSpec((N_CHIP, tr, tc), lambda i, j: (0, i, j))],
        out_specs=pl.BlockSpec((tr, tc), lambda i, j: (i, j)), out_shape=jax.ShapeDtypeStruct((r, c), F32),
        compiler_params=_params(("parallel", "parallel")),
    )(recv)


def _adamw(w, g_parts, m, v, name):
    r, c = w.shape
    tr, tc = _elementwise_tiles(r, c)
    n_g = len(g_parts)
    c1 = 1.0 / (1.0 - ADAM_B1 ** ADAM_STEP)
    c2 = 1.0 / (1.0 - ADAM_B2 ** ADAM_STEP)

    def body(*refs):
        w_ref, g_refs, m_ref, v_ref = refs[0], refs[1:1 + n_g], refs[1 + n_g], refs[2 + n_g]
        g_out, d_out, m_out, v_out = refs[3 + n_g:]
        g = g_refs[0][...]
        for gr in g_refs[1:]:
            g = g + gr[...]
        mn = ADAM_B1 * m_ref[...] + (1.0 - ADAM_B1) * g
        vn = ADAM_B2 * v_ref[...] + (1.0 - ADAM_B2) * (g * g)
        g_out[...] = g
        m_out[...] = mn
        v_out[...] = vn
        d_out[...] = -ADAM_LR * ((mn * c1) / (jnp.sqrt(vn * c2) + ADAM_EPS) + ADAM_WD * w_ref[...])

    spec = pl.BlockSpec((tr, tc), lambda i, j: (i, j))
    return pl.pallas_call(
        body, name=name, grid=(r // tr, c // tc), in_specs=[spec] * (3 + n_g), out_specs=[spec] * 4,
        out_shape=[jax.ShapeDtypeStruct((r, c), F32)] * 4, compiler_params=_params(("parallel", "parallel")),
    )(w, *g_parts, m, v)


def _conv_taps(u, t):
    rows = lax.broadcasted_iota(jnp.int32, u.shape, 0)
    return [u] + [jnp.where(rows >= dd, pltpu.roll(u, dd, 0), 0.0) for dd in range(1, CONV_K)]


def _conv_fwd(proj, conv_w, col0, name):
    t = proj.shape[0]
    ch = conv_w.shape[1]

    def body(u_ref, w_ref, o_ref):
        taps = _conv_taps(u_ref[...], t)
        wv = w_ref[...]
        y = taps[0] * wv[CONV_K - 1:CONV_K]
        for dd in range(1, CONV_K):
            y = y + taps[dd] * wv[CONV_K - 1 - dd:CONV_K - dd]
        o_ref[...] = _silu(y)

    return pl.pallas_call(
        body, name=name, grid=(ch // LANES,),
        in_specs=[pl.BlockSpec((t, LANES), lambda j: (0, col0 + j)), pl.BlockSpec((CONV_K, LANES), lambda j: (0, j))],
        out_specs=pl.BlockSpec((t, LANES), lambda j: (0, j)), out_shape=jax.ShapeDtypeStruct((t, ch), F32),
        compiler_params=_params(("parallel",)),
    )(proj, conv_w)


def _conv_bwd(proj, conv_w, ds, col0, name):
    t = proj.shape[0]
    ch = conv_w.shape[1]

    def body(u_ref, w_ref, ds_ref, du_ref, dw_ref):
        u = u_ref[...]
        taps = _conv_taps(u, t)
        wv = w_ref[...]
        y = taps[0] * wv[CONV_K - 1:CONV_K]
        for dd in range(1, CONV_K):
            y = y + taps[dd] * wv[CONV_K - 1 - dd:CONV_K - dd]
        sg = jax.nn.sigmoid(y)
        dy = ds_ref[...] * (sg * (1.0 + y * (1.0 - sg)))
        rows = lax.broadcasted_iota(jnp.int32, u.shape, 0)
        du = dy * wv[CONV_K - 1:CONV_K]
        for dd in range(1, CONV_K):
            ahead = jnp.where(rows < t - dd, pltpu.roll(dy, t - dd, 0), 0.0)
            du = du + ahead * wv[CONV_K - 1 - dd:CONV_K - dd]
        du_ref[...] = du.astype(du_ref.dtype)
        dws = [jnp.sum(dy * taps[CONV_K - 1 - j], axis=0, keepdims=True) for j in range(CONV_K)]
        dw_ref[...] = jnp.concatenate(dws, axis=0)

    return pl.pallas_call(
        body, name=name, grid=(ch // LANES,),
        in_specs=[pl.BlockSpec((t, LANES), lambda j: (0, col0 + j)), pl.BlockSpec((CONV_K, LANES), lambda j: (0, j)),
                  pl.BlockSpec((t, LANES), lambda j: (0, j))],
        out_specs=[pl.BlockSpec((t, LANES), lambda j: (0, j)), pl.BlockSpec((CONV_K, LANES), lambda j: (0, j))],
        out_shape=[jax.ShapeDtypeStruct((t, ch), BF16), jax.ShapeDtypeStruct((CONV_K, ch), F32)],
        compiler_params=_params(("parallel",)),
    )(proj, conv_w, ds)


def _hg_block(st, q, fl, vi, g, l0, l1, nw):
    hs = range(len(st))
    tb = q[0].shape[0]
    ln = HG_SUB
    lb = [jax.nn.sigmoid(l0[h] - l1[h]) for h in hs]
    rows = lax.broadcasted_iota(jnp.int32, (ln, HEAD), 0)
    tri = (lax.broadcasted_iota(jnp.int32, (ln, ln), 0) >= lax.broadcasted_iota(jnp.int32, (ln, ln), 1)).astype(F32)
    st = list(st)
    outs = [[] for _ in hs]
    for i in range(tb // ln):
        sl = slice(i * ln, (i + 1) * ln)
        qs, vs = [q[h][sl] for h in hs], [vi[h][sl] for h in hs]
        f = [lb[h] + (1.0 - lb[h]) * jax.nn.sigmoid(fl[h][sl]) for h in hs]
        k = [1.0 - f[h] for h in hs]
        b = [jnp.dot(tri, jnp.log(f[h]), precision=HI, preferred_element_type=F32) for h in hs]
        o = [lax.dot_general((qs[h] * jnp.exp(b[h])).astype(BF16), st[h].astype(BF16), NT_DIMS, preferred_element_type=F32)
             for h in hs]
        for s in range(ln):
            e = [jnp.exp(jnp.where(rows >= s, b[h] - b[h][s:s + 1], -1e30)) for h in hs]
            a = [jnp.sum(qs[h] * e[h] * k[h][s:s + 1], axis=-1, keepdims=True) for h in hs]
            o = [o[h] + a[h] * vs[h][s:s + 1] for h in hs]
        kt = [k[h] * jnp.exp(b[h][ln - 1:ln] - b[h]) for h in hs]
        upd = [lax.dot_general(vs[h].astype(BF16), kt[h].astype(BF16), TN_DIMS, preferred_element_type=F32) for h in hs]
        st = [st[h] * jnp.exp(b[h][ln - 1:ln]) + upd[h] for h in hs]
        for h in hs:
            outs[h].append(o[h])
    o = [jnp.concatenate(outs[h], axis=0) for h in hs]
    out = [o[h] * _rms_scale(o[h]) * nw * _silu(g[h]) for h in hs]
    return st, out


def _head_cols(h):
    return slice(h * HEAD, (h + 1) * HEAD)


def _head_groups(n_heads):
    g = min(HEAD_GROUP, n_heads)
    return [list(range(i, min(i + g, n_heads))) for i in range(0, n_heads, g)]


def _hg_in_specs(n_heads, tb, time_index):
    hw = n_heads * HEAD
    cols = [pl.BlockSpec((tb, hw), functools.partial(lambda part, j: (time_index(j), part), part)) for part in range(4)]
    head_rows = pl.BlockSpec((n_heads, 1, HEAD), lambda j: (0, 0, 0))
    return cols + [head_rows, head_rows, pl.BlockSpec((1, HEAD), lambda j: (0, 0))]


def _hgrn2_fwd(proj, l0, l1, nw, n_heads, name):
    t = proj.shape[0]
    hw = n_heads * HEAD
    tb = _tile(t, HG_BLOCK, HG_SUB)
    nb = t // tb

    def body(q_ref, f_ref, i_ref, g_ref, l0_ref, l1_ref, nw_ref, o_ref, save_ref, st_ref):
        @pl.when(pl.program_id(0) == 0)
        def _():
            st_ref[...] = jnp.zeros_like(st_ref)

        for hs in _head_groups(n_heads):
            st = [st_ref[h] for h in hs]
            for h, s in zip(hs, st):
                save_ref[h] = s
            st, out = _hg_block(st, *[[r[:, _head_cols(h)] for h in hs] for r in (q_ref, f_ref, i_ref, g_ref)],
                                [l0_ref[h] for h in hs], [l1_ref[h] for h in hs], nw_ref[...])
            for h, s, o in zip(hs, st, out):
                st_ref[h] = s
                o_ref[:, _head_cols(h)] = o.astype(o_ref.dtype)

    return pl.pallas_call(
        body, name=name, grid=(nb,), in_specs=_hg_in_specs(n_heads, tb, lambda j: j),
        out_specs=[pl.BlockSpec((tb, hw), lambda j: (j, 0)),
                   pl.BlockSpec((None, n_heads, HEAD, HEAD), lambda j: (j, 0, 0, 0))],
        out_shape=[jax.ShapeDtypeStruct((t, hw), BF16), jax.ShapeDtypeStruct((nb, n_heads, HEAD, HEAD), F32)],
        scratch_shapes=[pltpu.VMEM((n_heads, HEAD, HEAD), F32)], compiler_params=_params(("arbitrary",)),
    )(proj, proj, proj, proj, l0, l1, nw)


def _hgrn2_bwd(proj, l0, l1, nw, saved, d_ocat, n_heads, name):
    t = proj.shape[0]
    tb = _tile(t, HG_BLOCK, HG_SUB)
    nb = t // tb
    rev = lambda j: nb - 1 - j

    hw = n_heads * HEAD

    def body(q_ref, f_ref, i_ref, g_ref, l0_ref, l1_ref, nw_ref, save_ref, do_ref,
             dp_ref, dl0_ref, dl1_ref, dnw_ref, dst_ref):
        @pl.when(pl.program_id(0) == 0)
        def _():
            dst_ref[...] = jnp.zeros_like(dst_ref)
            dl0_ref[...] = jnp.zeros_like(dl0_ref)
            dl1_ref[...] = jnp.zeros_like(dl1_ref)
            dnw_ref[...] = jnp.zeros_like(dnw_ref)

        dnw_acc = jnp.zeros((1, HEAD), F32)
        for hs in _head_groups(n_heads):
            _, vjp = jax.vjp(_hg_block, [save_ref[h] for h in hs],
                             *[[r[:, _head_cols(h)] for h in hs] for r in (q_ref, f_ref, i_ref, g_ref)],
                             [l0_ref[h] for h in hs], [l1_ref[h] for h in hs], nw_ref[...])
            dst, dq, df, di, dg, dl0, dl1, dnw = vjp(([dst_ref[h] for h in hs], [do_ref[:, _head_cols(h)] for h in hs]))
            for i, h in enumerate(hs):
                dst_ref[h] = dst[i]
                for part, val in enumerate((dq, df, di, dg)):
                    dp_ref[:, part * hw + h * HEAD:part * hw + (h + 1) * HEAD] = val[i].astype(dp_ref.dtype)
                dl0_ref[h] += dl0[i]
                dl1_ref[h] += dl1[i]
            dnw_acc = dnw_acc + dnw
        dnw_ref[...] += dnw_acc

    head_rows = pl.BlockSpec((n_heads, 1, HEAD), lambda j: (0, 0, 0))
    return pl.pallas_call(
        body, name=name, grid=(nb,),
        in_specs=_hg_in_specs(n_heads, tb, rev) + [pl.BlockSpec((None, n_heads, HEAD, HEAD), lambda j: (rev(j), 0, 0, 0)),
                                                   pl.BlockSpec((tb, hw), lambda j: (rev(j), 0))],
        out_specs=[pl.BlockSpec((tb, 4 * hw), lambda j: (rev(j), 0)), head_rows, head_rows,
                   pl.BlockSpec((1, HEAD), lambda j: (0, 0))],
        out_shape=[jax.ShapeDtypeStruct((t, 4 * hw), BF16)] + [jax.ShapeDtypeStruct((n_heads, 1, HEAD), F32)] * 2
        + [jax.ShapeDtypeStruct((1, HEAD), F32)],
        scratch_shapes=[pltpu.VMEM((n_heads, HEAD, HEAD), F32)], compiler_params=_params(("arbitrary",)),
    )(proj, proj, proj, proj, l0, l1, nw, saved, d_ocat)


def _dot_hi(a, b, dims=(((1,), (0,)), ((), ()))):
    return lax.dot_general(a, b, dims, precision=HI, preferred_element_type=F32)


def _dot_bf16(a, b, dims=(((1,), (0,)), ((), ()))):
    return lax.dot_general(a.astype(BF16), b.astype(BF16), dims, preferred_element_type=F32)


def _inv_unit_lower_raw(ms):
    hs = range(len(ms))
    c = ms[0].shape[0]
    r = lax.broadcasted_iota(jnp.int32, (c, c), 0)
    q = lax.broadcasted_iota(jnp.int32, (c, c), 1)
    eye = (r == q).astype(F32)
    md = [jnp.where((r // GDN_INV_BLOCK) == (q // GDN_INV_BLOCK), ms[h], 0.0) for h in hs]
    p = [-md[h] for h in hs]
    t16 = [eye + p[h] for h in hs]
    for _ in range(int(math.log2(GDN_INV_BLOCK)) - 1):
        p = [_dot_hi(p[h], p[h]) for h in hs]
        t16 = [t16[h] + _dot_hi(t16[h], p[h]) for h in hs]
    p = [-_dot_hi(t16[h], ms[h] - md[h]) for h in hs]
    t2 = [eye + p[h] for h in hs]
    for _ in range(int(math.log2(c // GDN_INV_BLOCK)) - 1):
        p = [_dot_hi(p[h], p[h]) for h in hs]
        t2 = [t2[h] + _dot_hi(t2[h], p[h]) for h in hs]
    return [_dot_hi(t2[h], t16[h]) for h in hs]


@jax.custom_vjp
def _inv_unit_lower(ms):
    return _inv_unit_lower_raw(ms)


def _inv_fwd(ms):
    ts = _inv_unit_lower_raw(ms)
    return ts, ts


def _inv_bwd(ts, dts):
    hs = range(len(ts))
    inner = [_dot_hi(ts[h], dts[h], TN_DIMS) for h in hs]
    return ([-_dot_hi(inner[h], ts[h], NT_DIMS) for h in hs],)


_inv_unit_lower.defvjp(_inv_fwd, _inv_bwd)


def _gdn_block(inverse, onehots, st, qc, kc, vc, g, ab, alog_row, dtb_row, nw):
    hs = range(len(st))
    c = qc[0].shape[0]
    lane_sum = lambda v: jnp.sum(v, axis=-1, keepdims=True)
    a = [lane_sum(ab * onehots[h][0]) for h in hs]
    bb = [lane_sum(ab * onehots[h][1]) for h in hs]
    alog = [lane_sum(alog_row * onehots[h][0]) for h in hs]
    dtb = [lane_sum(dtb_row * onehots[h][0]) for h in hs]
    la = [-jnp.exp(alog[h]) * _softplus(a[h] + dtb[h]) for h in hs]
    beta = [jax.nn.sigmoid(bb[h]) for h in hs]
    q = [qc[h] * lax.rsqrt(lane_sum(qc[h] * qc[h]) + EPS) * (HEAD ** -0.5) for h in hs]
    k = [kc[h] * lax.rsqrt(lane_sum(kc[h] * kc[h]) + EPS) for h in hs]
    r = lax.broadcasted_iota(jnp.int32, (c, c), 0)
    s = lax.broadcasted_iota(jnp.int32, (c, c), 1)
    tri = (r >= s).astype(F32)
    g_cc = [_dot_hi(tri, jnp.broadcast_to(la[h], (c, c))) for h in hs]
    g_cl = [_dot_hi(tri, jnp.broadcast_to(la[h], (c, HEAD))) for h in hs]
    gamma = [jnp.exp(jnp.where(r >= s, g_cc[h] - g_cc[h].T, -1e30)) for h in hs]
    kk = [_dot_bf16(k[h], k[h], NT_DIMS) for h in hs]
    m = [jnp.where(r > s, beta[h] * kk[h] * gamma[h], 0.0) for h in hs]
    tm = inverse(m)
    eg = [jnp.exp(g_cl[h]) for h in hs]
    rhs = [jnp.concatenate([vc[h] * beta[h], k[h] * (beta[h] * eg[h])], axis=1) for h in hs]
    sol = [_dot_hi(tm[h], rhs[h]) for h in hs]
    qk = [_dot_bf16(q[h], k[h], NT_DIMS) * gamma[h] for h in hs]
    g_last = [g_cl[h][c - 1:c] for h in hs]
    k_tail = [k[h] * jnp.exp(g_last[h] - g_cl[h]) for h in hs]
    v_new = [sol[h][:, :HEAD] - _dot_bf16(sol[h][:, HEAD:], st[h], NT_DIMS) for h in hs]
    o_st = [_dot_bf16(q[h] * eg[h], st[h], NT_DIMS) for h in hs]
    o = [o_st[h] + _dot_bf16(qk[h], v_new[h]) for h in hs]
    upd = [_dot_bf16(v_new[h], k_tail[h], TN_DIMS) for h in hs]
    st = [st[h] * jnp.exp(g_last[h]) + upd[h] for h in hs]
    out = [o[h] * _rms_scale(o[h]) * nw * _silu(g[h]) for h in hs]
    return st, out


def _head_onehots(n_heads, h):
    lane = lax.broadcasted_iota(jnp.int32, (1, LANES), 1)
    return (lane == h).astype(F32), (lane == n_heads + h).astype(F32)


def _gdn_in_specs(n_heads, c, time_index):
    hw = n_heads * HEAD
    qkv = [pl.BlockSpec((c, hw), functools.partial(lambda part, j: (time_index(j), part), part)) for part in range(3)]
    row = pl.BlockSpec((1, LANES), lambda j: (0, 0))
    return qkv + [pl.BlockSpec((c, hw), lambda j: (time_index(j), 7)),
                  pl.BlockSpec((c, LANES), lambda j: (time_index(j), 8 * n_heads)), row, row, row]


def _gdn_fwd(qkv, proj, alog_row, dtb_row, nw, n_heads, name):
    t = qkv.shape[0]
    hw = n_heads * HEAD
    c = _tile(t, GDN_CHUNK, GDN_CHUNK)
    nb = t // c

    def body(q_ref, k_ref, v_ref, g_ref, ab_ref, al_ref, dt_ref, nw_ref, o_ref, save_ref, st_ref):
        @pl.when(pl.program_id(0) == 0)
        def _():
            st_ref[...] = jnp.zeros_like(st_ref)

        for hs in _head_groups(n_heads):
            st = [st_ref[h] for h in hs]
            for h, s in zip(hs, st):
                save_ref[h] = s
            st, out = _gdn_block(_inv_unit_lower_raw, [_head_onehots(n_heads, h) for h in hs], st,
                                 *[[r[:, _head_cols(h)] for h in hs] for r in (q_ref, k_ref, v_ref, g_ref)],
                                 ab_ref[...], al_ref[...], dt_ref[...], nw_ref[...])
            for h, s, o in zip(hs, st, out):
                st_ref[h] = s
                o_ref[:, _head_cols(h)] = o.astype(o_ref.dtype)

    return pl.pallas_call(
        body, name=name, grid=(nb,), in_specs=_gdn_in_specs(n_heads, c, lambda j: j),
        out_specs=[pl.BlockSpec((c, hw), lambda j: (j, 0)),
                   pl.BlockSpec((None, n_heads, HEAD, HEAD), lambda j: (j, 0, 0, 0))],
        out_shape=[jax.ShapeDtypeStruct((t, hw), BF16), jax.ShapeDtypeStruct((nb, n_heads, HEAD, HEAD), F32)],
        scratch_shapes=[pltpu.VMEM((n_heads, HEAD, HEAD), F32)], compiler_params=_params(("arbitrary",)),
    )(qkv, qkv, qkv, proj, proj, alog_row, dtb_row, nw)


def _gdn_bwd(qkv, proj, alog_row, dtb_row, nw, saved, d_ocat, n_heads, name):
    t = qkv.shape[0]
    c = _tile(t, GDN_CHUNK, GDN_CHUNK)
    nb = t // c
    rev = lambda j: nb - 1 - j

    hw = n_heads * HEAD

    def body(q_ref, k_ref, v_ref, g_ref, ab_ref, al_ref, dt_ref, nw_ref, save_ref, do_ref,
             dqkv_ref, dg_ref, dab_ref, dal_ref, ddt_ref, dnw_ref, dst_ref):
        @pl.when(pl.program_id(0) == 0)
        def _():
            dst_ref[...] = jnp.zeros_like(dst_ref)
            dal_ref[...] = jnp.zeros_like(dal_ref)
            ddt_ref[...] = jnp.zeros_like(ddt_ref)
            dnw_ref[...] = jnp.zeros_like(dnw_ref)

        dab_acc = jnp.zeros((c, LANES), F32)
        row_acc = [jnp.zeros((1, LANES), F32)] * 3
        for hs in _head_groups(n_heads):
            fn = functools.partial(_gdn_block, _inv_unit_lower, [_head_onehots(n_heads, h) for h in hs])
            _, vjp = jax.vjp(fn, [save_ref[h] for h in hs],
                             *[[r[:, _head_cols(h)] for h in hs] for r in (q_ref, k_ref, v_ref, g_ref)],
                             ab_ref[...], al_ref[...], dt_ref[...], nw_ref[...])
            dst, dq, dk, dv, dg, dab, dal, ddt, dnw = vjp(([dst_ref[h] for h in hs], [do_ref[:, _head_cols(h)] for h in hs]))
            for i, h in enumerate(hs):
                dst_ref[h] = dst[i]
                for part, val in enumerate((dq, dk, dv)):
                    dqkv_ref[:, part * hw + h * HEAD:part * hw + (h + 1) * HEAD] = val[i]
                dg_ref[:, _head_cols(h)] = dg[i].astype(dg_ref.dtype)
            dab_acc = dab_acc + dab
            row_acc = [acc + val for acc, val in zip(row_acc, (dal, ddt, dnw))]
        dab_ref[...] = dab_acc
        dal_ref[...] += row_acc[0]
        ddt_ref[...] += row_acc[1]
        dnw_ref[...] += row_acc[2]

    row = pl.BlockSpec((1, LANES), lambda j: (0, 0))
    return pl.pallas_call(
        body, name=name, grid=(nb,),
        in_specs=_gdn_in_specs(n_heads, c, rev) + [pl.BlockSpec((None, n_heads, HEAD, HEAD), lambda j: (rev(j), 0, 0, 0)),
                                                   pl.BlockSpec((c, hw), lambda j: (rev(j), 1))],
        out_specs=[pl.BlockSpec((c, 3 * hw), lambda j: (rev(j), 0)), pl.BlockSpec((c, hw), lambda j: (rev(j), 0)),
                   pl.BlockSpec((c, LANES), lambda j: (rev(j), 0)), row, row, row],
        out_shape=[jax.ShapeDtypeStruct((t, 3 * hw), F32), jax.ShapeDtypeStruct((t, hw), BF16),
                   jax.ShapeDtypeStruct((t, LANES), F32)] + [jax.ShapeDtypeStruct((1, LANES), F32)] * 3,
        scratch_shapes=[pltpu.VMEM((n_heads, HEAD, HEAD), F32)], compiler_params=_params(("arbitrary",)),
    )(qkv, qkv, qkv, proj, proj, alog_row, dtb_row, nw, saved, d_ocat)


def _pad_lanes(v, n):
    v = v.reshape(1, -1)
    return jnp.pad(v, ((0, 0), (0, n - v.shape[1])))


def _pack_rows(vecs):
    flat = jnp.concatenate([v.reshape(-1) for v in vecs])
    offs, o = [], 0
    for v in vecs:
        offs.append((o, v.size))
        o += v.size
    per_row = -(-o // (SUBLANES * LANES)) * LANES
    flat = jnp.pad(flat, (0, SUBLANES * per_row - o))
    return flat.reshape(SUBLANES, per_row), offs


def _unpack(gathered, offs):
    per_dev = gathered.reshape(N_DEV, -1)
    return [per_dev[:, o:o + n] for o, n in offs]


def _sum_devices(part):
    acc = part[0]
    for i in range(1, N_DEV):
        acc = acc + part[i]
    return acc


def kernel(x, c, w_ada, b_ada, pre_mix_norm, post_mix_norm, pre_ffn_norm, post_ffn_norm, w_in, hg_lb_logits, hg_norm, gdn_conv_w, gdn_a_log, gdn_dt_bias, gdn_norm, w_out, w_ff1, w_ff2, loss_target, m_w_ada, m_b_ada, m_pre_mix_norm, m_post_mix_norm, m_pre_ffn_norm, m_post_ffn_norm, m_w_in, m_hg_lb_logits, m_hg_norm, m_gdn_conv_w, m_gdn_a_log, m_gdn_dt_bias, m_gdn_norm, m_w_out, m_w_ff1, m_w_ff2, v_w_ada, v_b_ada, v_pre_mix_norm, v_post_mix_norm, v_pre_ffn_norm, v_post_ffn_norm, v_w_in, v_hg_lb_logits, v_hg_norm, v_gdn_conv_w, v_gdn_a_log, v_gdn_dt_bias, v_gdn_norm, v_w_out, v_w_ff1, v_w_ff2):
    assert x.shape[0] == 1 and w_ada.shape[0] == 1 and hg_lb_logits.shape[0] == 2
    t, d = x.shape[1], x.shape[2]
    n_heads = (d // 2) // HEAD
    hw = n_heads * HEAD
    in_cols = 8 * hw + 2 * n_heads
    np_cols = 8 * hw + LANES
    d_ff = w_ff1.shape[2] * N_CHIP
    na = w_ada.shape[2]
    ax, ay, ac = lax.axis_index("x"), lax.axis_index("y"), lax.axis_index("c")
    chip = 2 * ax + ay
    dev = 4 * ax + 2 * ay + ac

    x2d, tgt = x[0], loss_target[0]

    pack1, offs1 = _pack_rows([c[0], gdn_conv_w[0]])
    c_all, convw_all = _unpack(_gather8(pack1, "gather_cond"), offs1)
    conv_sh = gdn_conv_w.shape[2]
    conv_w = jnp.concatenate([convw_all[2 * j].reshape(CONV_K, conv_sh) for j in range(N_CHIP)], axis=1)

    b_s = lax.dynamic_slice(b_ada, (0, chip * na), (1, na))
    mod_part = _mod_part(c_all, w_ada[0], b_s, "mod_part")
    pack2, offs2 = _pack_rows([mod_part])
    (mod_parts,) = _unpack(_gather8(pack2, "gather_mod"), offs2)
    mod_all = jnp.concatenate([mod_parts[2 * j].reshape(N_DEV, na) for j in range(N_CHIP)], axis=1)
    mod = lax.dynamic_slice(mod_all, (dev, 0), (1, N_MOD * d))
    sh_m, sc_m, gt_m, sh_f, sc_f, gt_f = [mod[:, i * d:(i + 1) * d] for i in range(N_MOD)]

    g_in, g_out, g_ff1, g_ff2 = _chip_exchange(
        [w_in[0].astype(BF16), w_out[0].astype(BF16), w_ff1[0].astype(BF16), w_ff2[0].astype(BF16)], True, "gather_weights")
    w_in_f = jnp.pad(jnp.transpose(g_in, (1, 0, 2)).reshape(d, in_cols), ((0, 0), (0, np_cols - in_cols)))
    w_out_f = g_out.reshape(d, d)
    w_ff1_f = jnp.transpose(g_ff1, (1, 0, 2)).reshape(d, d_ff)
    w_ff2_f = g_ff2.reshape(d_ff, d)

    h1 = _norm_mod(x2d, pre_mix_norm, sc_m, sh_m, "norm_mod_mix")
    proj = _matmul(h1, w_in_f, "nn", F32, "mm_in", tn=640)
    l0, l1 = hg_lb_logits[0].reshape(n_heads, 1, HEAD), hg_lb_logits[1].reshape(n_heads, 1, HEAD)
    o_hg, hg_saved = _hgrn2_fwd(proj, l0, l1, hg_norm, n_heads, "hgrn2_fwd")
    qkv = _conv_fwd(proj, conv_w, 4 * n_heads, "conv_fwd")
    alog_row, dtb_row = _pad_lanes(gdn_a_log, LANES), _pad_lanes(gdn_dt_bias, LANES)
    o_gdn, gdn_saved = _gdn_fwd(qkv, proj, alog_row, dtb_row, gdn_norm, n_heads, "gdn_fwd")
    o_cat = jnp.concatenate([o_hg, o_gdn], axis=1)
    y1 = _matmul(o_cat, w_out_f, "nn", F32, "mm_out")
    x_mid = _resid(x2d, y1, post_mix_norm, gt_m, "resid_mix")

    h2 = _norm_mod(x_mid, pre_ffn_norm, sc_f, sh_f, "norm_mod_ffn")
    a1 = _matmul(h2, w_ff1_f, "nn", F32, "mm_ff1")
    r1 = _relu2(a1, "relu2")
    y2 = _matmul(r1, w_ff2_f, "nn", F32, "mm_ff2")
    d_out, loss_row = _loss_head(x_mid, y2, post_ffn_norm, gt_f, tgt, "loss_head")

    dy2, d_gt_f, d_post_ffn = _resid_bwd(d_out, y2, post_ffn_norm, gt_f, "resid_ffn_bwd")
    gw_ff2 = _matmul(r1, dy2, "tn", F32, "mm_ff2_dw")
    dr1 = _matmul(dy2, w_ff2_f, "nt", BF16, "mm_ff2_dx")
    da1 = _relu2_bwd(a1, dr1, "relu2_bwd")
    gw_ff1 = _matmul(h2, da1, "tn", F32, "mm_ff1_dw")
    dh2 = _matmul(da1, w_ff1_f, "nt", BF16, "mm_ff1_dx")
    d_mid, d_pre_ffn, d_sc_f, d_sh_f = _norm_mod_bwd(x_mid, pre_ffn_norm, sc_f, dh2, d_out, "norm_mod_ffn_bwd")

    dy1, d_gt_m, d_post_mix = _resid_bwd(d_mid, y1, post_mix_norm, gt_m, "resid_mix_bwd")
    gw_out = _matmul(o_cat, dy1, "tn", F32, "mm_out_dw")
    d_ocat = _matmul(dy1, w_out_f, "nt", F32, "mm_out_dx")
    dp_hg, dl0, dl1, d_hg_norm = _hgrn2_bwd(proj, l0, l1, hg_norm, hg_saved, d_ocat, n_heads, "hgrn2_bwd")
    dqkv, dg_g, dab, d_alog, d_dtb, d_gdn_norm = _gdn_bwd(
        qkv, proj, alog_row, dtb_row, gdn_norm, gdn_saved, d_ocat, n_heads, "gdn_bwd")
    du, d_conv_w = _conv_bwd(proj, conv_w, dqkv, 4 * n_heads, "conv_bwd")
    dproj = jnp.concatenate([dp_hg, du, dg_g, dab.astype(BF16)], axis=1)
    gw_in = _matmul(h1, dproj, "tn", F32, "mm_in_dw", tn=640)
    dh1 = _matmul(dproj, w_in_f, "nt", BF16, "mm_in_dx", tk=640)
    grad_x, d_pre_mix, d_sc_m, d_sh_m = _norm_mod_bwd(x2d, pre_mix_norm, sc_m, dh1, d_mid, "norm_mod_mix_bwd")

    d_mod = jnp.concatenate([d_sh_m, d_sc_m, d_gt_m, d_sh_f, d_sc_f, d_gt_f], axis=1)
    d_lb_logits = jnp.stack([dl0.reshape(n_heads, HEAD), dl1.reshape(n_heads, HEAD)])
    pack3, offs3 = _pack_rows([loss_row[0, :1], d_pre_mix, d_post_mix, d_pre_ffn, d_post_ffn, d_lb_logits, d_hg_norm,
                               d_conv_w, d_alog[0, :n_heads], d_dtb[0, :n_heads], d_gdn_norm, d_mod])
    parts = _unpack(_gather8(pack3, "gather_vec_grads"), offs3)
    sums = [_sum_devices(p) for p in parts[:-1]]
    loss = sums[0][0]
    dmod_all = parts[-1]
    g_b_ada = _sum_devices(dmod_all).reshape(1, N_MOD * d)
    g_conv_full = sums[7].reshape(CONV_K, N_CHIP * conv_sh)
    g_conv = lax.dynamic_slice(g_conv_full, (0, chip * conv_sh), (CONV_K, conv_sh))
    gw_ada = _wada_grad(c_all, lax.dynamic_slice(dmod_all, (0, chip * na), (N_DEV, na)), "wada_grad")

    in_sh = in_cols // N_CHIP
    ff_sh = d_ff // N_CHIP
    send = [jnp.transpose(gw_in[:, :in_cols].reshape(d, N_CHIP, in_sh), (1, 0, 2)).astype(BF16),
            gw_out.reshape(N_CHIP, d // N_CHIP, d).astype(BF16),
            jnp.transpose(gw_ff1.reshape(d, N_CHIP, ff_sh), (1, 0, 2)).astype(BF16),
            gw_ff2.reshape(N_CHIP, ff_sh, d).astype(BF16)]
    recv = _chip_exchange(send, False, "scatter_grads")
    mine = [_sum_chips(rv, f"sum_chips_{i}") for i, rv in enumerate(recv)]
    theirs = _sibling_exchange(mine, "sibling_grads")

    big = {}
    for i, (nm, w_, m_, v_) in enumerate([("w_in", w_in, m_w_in, v_w_in), ("w_out", w_out, m_w_out, v_w_out),
                                          ("w_ff1", w_ff1, m_w_ff1, v_w_ff1), ("w_ff2", w_ff2, m_w_ff2, v_w_ff2)]):
        big[nm] = [o[None] for o in _adamw(w_[0], [mine[i], theirs[i]], m_[0], v_[0], f"adamw_{nm}")]
    big["w_ada"] = [o[None] for o in _adamw(w_ada[0], [gw_ada], m_w_ada[0], v_w_ada[0], "adamw_w_ada")]

    small_names = ["b_ada", "pre_mix_norm", "post_mix_norm", "pre_ffn_norm", "post_ffn_norm", "hg_lb_logits", "hg_norm",
                   "gdn_conv_w", "gdn_a_log", "gdn_dt_bias", "gdn_norm"]
    small_w = [b_ada, pre_mix_norm, post_mix_norm, pre_ffn_norm, post_ffn_norm, hg_lb_logits, hg_norm, gdn_conv_w,
               gdn_a_log, gdn_dt_bias, gdn_norm]
    small_m = [m_b_ada, m_pre_mix_norm, m_post_mix_norm, m_pre_ffn_norm, m_post_ffn_norm, m_hg_lb_logits, m_hg_norm,
               m_gdn_conv_w, m_gdn_a_log, m_gdn_dt_bias, m_gdn_norm]
    small_v = [v_b_ada, v_pre_mix_norm, v_post_mix_norm, v_pre_ffn_norm, v_post_ffn_norm, v_hg_lb_logits, v_hg_norm,
               v_gdn_conv_w, v_gdn_a_log, v_gdn_dt_bias, v_gdn_norm]
    small_g = [g_b_ada, sums[1], sums[2], sums[3], sums[4], sums[5], sums[6], g_conv, sums[8], sums[9], sums[10]]
    pw, offs_s = _pack_rows(small_w)
    pg, _ = _pack_rows(small_g)
    pm, _ = _pack_rows(small_m)
    pv, _ = _pack_rows(small_v)
    packed = _adamw(pw, [pg], pm, pv, "adamw_vectors")
    small = {}
    for nm, w_, (o, n) in zip(small_names, small_w, offs_s):
        small[nm] = [p.reshape(-1)[o:o + n].reshape(w_.shape) for p in packed]

    order = ["w_ada", "b_ada", "pre_mix_norm", "post_mix_norm", "pre_ffn_norm", "post_ffn_norm", "w_in", "hg_lb_logits",
             "hg_norm", "gdn_conv_w", "gdn_a_log", "gdn_dt_bias", "gdn_norm", "w_out", "w_ff1", "w_ff2"]
    res = {**big, **small}
    outs = [loss, grad_x[None]]
    for k in range(4):
        outs += [res[nm][k] for nm in order]
    return tuple(outs)
```

```python
import functools
import math

import jax
import jax.numpy as jnp
from jax import lax
from jax.experimental import pallas as pl
from jax.experimental.pallas import tpu as pltpu

F32 = jnp.float32
BF16 = jnp.bfloat16
HI = lax.Precision.HIGHEST
MESH = pl.DeviceIdType.MESH

LANES = 128
SUBLANES = 8
VMEM_LIMIT = 48 * 1024 * 1024
EPS = 1e-6
HEAD = 128
CONV_K = 4
GDN_CHUNK = 64
GDN_INV_BLOCK = 16
HG_SUB = 16
HG_BLOCK = 128
HEAD_GROUP = 8
N_MOD = 6
N_DEV = 8
N_CHIP = 4

ADAM_LR = 0.001
ADAM_B1 = 0.9
ADAM_B2 = 0.999
ADAM_EPS = 1e-08
ADAM_WD = 0.01
ADAM_STEP = 10

NT_DIMS = (((1,), (1,)), ((), ()))
TN_DIMS = (((0,), (0,)), ((), ()))


def _tile(dim, target, align):
    if dim <= target:
        return dim
    best = dim
    t = align
    while t <= target:
        if dim % t == 0:
            best = t
        t += align
    return best


def _elementwise_tiles(r, c):
    tc = _tile(c, 1024, LANES)
    tr = _tile(r, max(16, (256 * 1024) // tc // 16 * 16), 16)
    return tr, tc


def _params(sem):
    return pltpu.CompilerParams(dimension_semantics=sem, vmem_limit_bytes=VMEM_LIMIT)


def _silu(x):
    return x * jax.nn.sigmoid(x)


def _softplus(x):
    pos = x > 0
    return jnp.where(pos, x, 0.0) + jnp.log(1.0 + jnp.exp(jnp.where(pos, -x, x)))


def _rms_scale(x):
    return lax.rsqrt(jnp.mean(x * x, axis=-1, keepdims=True) + EPS)


def _gather8(x_shard, name):
    m_per, n = x_shard.shape
    assert m_per % SUBLANES == 0 and n % LANES == 0

    def body(x_ref, out_ref, send_sems, recv_sems, local_sem):
        x, y, c = lax.axis_index("x"), lax.axis_index("y"), lax.axis_index("c")
        me, sibling = (x, y, c), (x, y, 1 - c)
        chips = [(1 - x, y), (x, 1 - y), (1 - x, 1 - y)]

        def rows(px, py, pc):
            return out_ref.at[pl.ds((4 * px + 2 * py + pc) * m_per, m_per), :]

        def copy(k, block, to, src=None):
            return pltpu.make_async_remote_copy(
                src_ref=rows(*block) if src is None else src, dst_ref=rows(*block),
                send_sem=send_sems.at[k], recv_sem=recv_sems.at[k], device_id=to, device_id_type=MESH)

        mine = pltpu.make_async_copy(x_ref, rows(*me), local_sem)
        mine.start()
        first = [copy(0, me, sibling, src=x_ref)]
        first += [copy(1 + j, me, (*chip, c), src=x_ref) for j, chip in enumerate(chips)]
        for cp in first:
            cp.start()
        passed = [copy(4 + j, (*chip, c), sibling) for j, chip in enumerate(chips)]
        for j, chip in enumerate(chips):
            copy(1 + j, (*chip, c), me).wait_recv()
            passed[j].start()
        copy(0, sibling, me).wait_recv()
        for j, chip in enumerate(chips):
            copy(4 + j, (*chip, 1 - c), me).wait_recv()
        for cp in first + passed:
            cp.wait_send()
        mine.wait()

    return pl.pallas_call(
        body, name=name,
        out_shape=jax.ShapeDtypeStruct((N_DEV * m_per, n), x_shard.dtype),
        in_specs=[pl.BlockSpec(memory_space=pltpu.VMEM)],
        out_specs=pl.BlockSpec(memory_space=pltpu.VMEM),
        scratch_shapes=[pltpu.SemaphoreType.DMA((7,)), pltpu.SemaphoreType.DMA((7,)), pltpu.SemaphoreType.DMA],
        compiler_params=pltpu.CompilerParams(vmem_limit_bytes=VMEM_LIMIT),
    )(x_shard)


def _chip_exchange(arrs, bcast, name):
    n = len(arrs)
    out_shapes = [jax.ShapeDtypeStruct((N_CHIP,) + (a.shape if bcast else a.shape[1:]), a.dtype) for a in arrs]

    def body(*refs):
        ins, outs = refs[:n], refs[n:2 * n]
        send_sems, recv_sems, local_sems = refs[2 * n:]
        x, y, c = lax.axis_index("x"), lax.axis_index("y"), lax.axis_index("c")
        me = 2 * x + y
        peers = [(1 - x, y), (x, 1 - y), (1 - x, 1 - y)]
        copies = []
        for a in range(n):
            src_own = ins[a] if bcast else ins[a].at[me]
            loc = pltpu.make_async_copy(src_own, outs[a].at[me], local_sems.at[a])
            loc.start()
            copies.append(loc)
        remote = []
        for a in range(n):
            for k, (px, py) in enumerate(peers):
                them = 2 * px + py
                cp = pltpu.make_async_remote_copy(
                    src_ref=ins[a] if bcast else ins[a].at[them], dst_ref=outs[a].at[me],
                    send_sem=send_sems.at[3 * a + k], recv_sem=recv_sems.at[3 * a + k],
                    device_id=(px, py, c), device_id_type=MESH)
                cp.start()
                remote.append((cp, a, k, them))
        for cp, a, k, them in remote:
            pltpu.make_async_remote_copy(
                src_ref=ins[a] if bcast else ins[a].at[them], dst_ref=outs[a].at[them],
                send_sem=send_sems.at[3 * a + k], recv_sem=recv_sems.at[3 * a + k],
                device_id=(x, y, c), device_id_type=MESH).wait_recv()
        for cp, a, k, them in remote:
            cp.wait_send()
        for loc in copies:
            loc.wait()

    hbm = pl.BlockSpec(memory_space=pltpu.HBM)
    return pl.pallas_call(
        body, name=name, out_shape=out_shapes, in_specs=[hbm] * n, out_specs=[hbm] * n,
        scratch_shapes=[pltpu.SemaphoreType.DMA((3 * n,)), pltpu.SemaphoreType.DMA((3 * n,)), pltpu.SemaphoreType.DMA((n,))],
    )(*arrs)


def _gather_weights(arrs, name):
    n = len(arrs)
    out_shapes = [jax.ShapeDtypeStruct((N_CHIP,) + a.shape, a.dtype) for a in arrs]

    def body(*refs):
        ins, outs = refs[:n], refs[n:2 * n]
        ici_send, ici_recv, d2d_send, d2d_recv, local_sems = refs[2 * n:]
        x, y, c = lax.axis_index("x"), lax.axis_index("y"), lax.axis_index("c")
        me = 2 * x + y
        sibling = (x, y, 1 - c)
        peers = [(1 - x, y), (x, 1 - y), (1 - x, 1 - y)]

        def half(a, cc):
            rh = arrs[a].shape[0] // 2
            return pl.ds(pl.multiple_of(cc * rh, 16), rh)

        local = []
        for a in range(n):
            loc = pltpu.make_async_copy(ins[a], outs[a].at[me], local_sems.at[a])
            loc.start()
            local.append(loc)
        sent = []
        for a in range(n):
            for k, (px, py) in enumerate(peers):
                cp = pltpu.make_async_remote_copy(
                    src_ref=ins[a].at[half(a, c)], dst_ref=outs[a].at[me, half(a, c)],
                    send_sem=ici_send.at[3 * a + k], recv_sem=ici_recv.at[3 * a + k],
                    device_id=(px, py, c), device_id_type=MESH)
                cp.start()
                sent.append(cp)
        for a in range(n):
            for k, (px, py) in enumerate(peers):
                landed = outs[a].at[2 * px + py, half(a, c)]
                pltpu.make_async_remote_copy(
                    src_ref=landed, dst_ref=landed, send_sem=ici_send.at[3 * a + k], recv_sem=ici_recv.at[3 * a + k],
                    device_id=(px, py, c), device_id_type=MESH).wait_recv()
                fwd = pltpu.make_async_remote_copy(
                    src_ref=landed, dst_ref=landed, send_sem=d2d_send.at[3 * a + k], recv_sem=d2d_recv.at[3 * a + k],
                    device_id=sibling, device_id_type=MESH)
                fwd.start()
                sent.append(fwd)
        for a in range(n):
            for k, (px, py) in enumerate(peers):
                passed = outs[a].at[2 * px + py, half(a, 1 - c)]
                pltpu.make_async_remote_copy(
                    src_ref=passed, dst_ref=passed, send_sem=d2d_send.at[3 * a + k], recv_sem=d2d_recv.at[3 * a + k],
                    device_id=sibling, device_id_type=MESH).wait_recv()
        for cp in sent:
            cp.wait_send()
        for loc in local:
            loc.wait()

    hbm = pl.BlockSpec(memory_space=pltpu.HBM)
    return pl.pallas_call(
        body, name=name, out_shape=out_shapes, in_specs=[hbm] * n, out_specs=[hbm] * n,
        scratch_shapes=[pltpu.SemaphoreType.DMA((3 * n,))] * 4 + [pltpu.SemaphoreType.DMA((n,))],
    )(*arrs)


def _sibling_merge(halves, name):
    n = len(halves)

    def body(*refs):
        ins, outs = refs[:n], refs[n:2 * n]
        send_sems, recv_sems, local_sems = refs[2 * n:]
        x, y, c = lax.axis_index("x"), lax.axis_index("y"), lax.axis_index("c")
        sibling = (x, y, 1 - c)
        cps = []
        for a in range(n):
            rh = halves[a].shape[0]
            mine = outs[a].at[pl.ds(pl.multiple_of(c * rh, 8), rh)]
            loc = pltpu.make_async_copy(ins[a], mine, local_sems.at[a])
            loc.start()
            cp = pltpu.make_async_remote_copy(src_ref=ins[a], dst_ref=mine, send_sem=send_sems.at[a],
                                              recv_sem=recv_sems.at[a], device_id=sibling, device_id_type=MESH)
            cp.start()
            cps.append((loc, cp))
        for a in range(n):
            rh = halves[a].shape[0]
            theirs = outs[a].at[pl.ds(pl.multiple_of((1 - c) * rh, 8), rh)]
            pltpu.make_async_remote_copy(src_ref=ins[a], dst_ref=theirs, send_sem=send_sems.at[a],
                                         recv_sem=recv_sems.at[a], device_id=sibling, device_id_type=MESH).wait_recv()
        for loc, cp in cps:
            cp.wait_send()
            loc.wait()

    hbm = pl.BlockSpec(memory_space=pltpu.HBM)
    return pl.pallas_call(
        body, name=name, out_shape=[jax.ShapeDtypeStruct((2 * h.shape[0],) + h.shape[1:], h.dtype) for h in halves],
        in_specs=[hbm] * n, out_specs=[hbm] * n,
        scratch_shapes=[pltpu.SemaphoreType.DMA((n,)), pltpu.SemaphoreType.DMA((n,)), pltpu.SemaphoreType.DMA((n,))],
    )(*halves)


def _sibling_exchange(arrs, name):
    n = len(arrs)

    def body(*refs):
        ins, outs = refs[:n], refs[n:2 * n]
        send_sems, recv_sems = refs[2 * n:]
        sibling = (lax.axis_index("x"), lax.axis_index("y"), 1 - lax.axis_index("c"))
        cps = []
        for a in range(n):
            cp = pltpu.make_async_remote_copy(src_ref=ins[a], dst_ref=outs[a], send_sem=send_sems.at[a],
                                              recv_sem=recv_sems.at[a], device_id=sibling, device_id_type=MESH)
            cp.start()
            cps.append(cp)
        for cp in cps:
            cp.wait_recv()
        for cp in cps:
            cp.wait_send()

    hbm = pl.BlockSpec(memory_space=pltpu.HBM)
    return pl.pallas_call(
        body, name=name, out_shape=[jax.ShapeDtypeStruct(a.shape, a.dtype) for a in arrs],
        in_specs=[hbm] * n, out_specs=[hbm] * n,
        scratch_shapes=[pltpu.SemaphoreType.DMA((n,)), pltpu.SemaphoreType.DMA((n,))],
    )(*arrs)


def _matmul(a, b, mode, out_dtype, name, tm=1024, tn=1024, tk=512):
    if mode == "nn":
        (m, k), (k2, n) = a.shape, b.shape
    elif mode == "nt":
        (m, k), (n, k2) = a.shape, b.shape
    else:
        (k, m), (k2, n) = a.shape, b.shape
    assert k == k2, (a.shape, b.shape, mode)
    tm, tn, tk = _tile(m, tm, LANES), _tile(n, tn, LANES), _tile(k, tk, LANES)
    nk = k // tk

    def body(a_ref, b_ref, o_ref, acc_ref):
        kk = pl.program_id(2)

        @pl.when(kk == 0)
        def _():
            acc_ref[...] = jnp.zeros_like(acc_ref)

        if mode == "nn":
            acc_ref[...] += jnp.dot(a_ref[...], b_ref[...], preferred_element_type=F32)
        elif mode == "nt":
            acc_ref[...] += lax.dot_general(a_ref[...], b_ref[...], NT_DIMS, preferred_element_type=F32)
        else:
            acc_ref[...] += lax.dot_general(a_ref[...], b_ref[...], TN_DIMS, preferred_element_type=F32)

        @pl.when(kk == nk - 1)
        def _():
            o_ref[...] = acc_ref[...].astype(o_ref.dtype)

    if mode == "nn":
        a_spec = pl.BlockSpec((tm, tk), lambda i, j, kk: (i, kk))
        b_spec = pl.BlockSpec((tk, tn), lambda i, j, kk: (kk, j))
    elif mode == "nt":
        a_spec = pl.BlockSpec((tm, tk), lambda i, j, kk: (i, kk))
        b_spec = pl.BlockSpec((tn, tk), lambda i, j, kk: (j, kk))
    else:
        a_spec = pl.BlockSpec((tk, tm), lambda i, j, kk: (kk, i))
        b_spec = pl.BlockSpec((tk, tn), lambda i, j, kk: (kk, j))
    return pl.pallas_call(
        body, name=name, grid=(m // tm, n // tn, nk), in_specs=[a_spec, b_spec],
        out_specs=pl.BlockSpec((tm, tn), lambda i, j, kk: (i, j)),
        out_shape=jax.ShapeDtypeStruct((m, n), out_dtype),
        scratch_shapes=[pltpu.VMEM((tm, tn), F32)],
        compiler_params=_params(("parallel", "parallel", "arbitrary")),
    )(a, b)


def _mod_part(c_all, w_s, b_s, name):
    d, na = w_s.shape
    tn = _tile(na, 512, LANES)

    def body(c_ref, w_ref, b_ref, o_ref):
        ca = _silu(c_ref[...]).astype(BF16)
        o_ref[...] = jnp.dot(ca, w_ref[...].astype(BF16), preferred_element_type=F32) + b_ref[...]

    return pl.pallas_call(
        body, name=name, grid=(na // tn,),
        in_specs=[pl.BlockSpec((N_DEV, d), lambda j: (0, 0)), pl.BlockSpec((d, tn), lambda j: (0, j)),
                  pl.BlockSpec((1, tn), lambda j: (0, j))],
        out_specs=pl.BlockSpec((N_DEV, tn), lambda j: (0, j)),
        out_shape=jax.ShapeDtypeStruct((N_DEV, na), F32), compiler_params=_params(("parallel",)),
    )(c_all, w_s, b_s)


def _wada_grad(c_all, dmod_s, name):
    d = c_all.shape[1]
    na = dmod_s.shape[1]
    td, tn = _tile(d, 512, LANES), _tile(na, 512, LANES)

    def body(c_ref, g_ref, o_ref):
        o_ref[...] = lax.dot_general(_silu(c_ref[...]), g_ref[...], TN_DIMS, precision=HI, preferred_element_type=F32)

    return pl.pallas_call(
        body, name=name, grid=(d // td, na // tn),
        in_specs=[pl.BlockSpec((N_DEV, td), lambda i, j: (0, i)), pl.BlockSpec((N_DEV, tn), lambda i, j: (0, j))],
        out_specs=pl.BlockSpec((td, tn), lambda i, j: (i, j)),
        out_shape=jax.ShapeDtypeStruct((d, na), F32), compiler_params=_params(("parallel", "parallel")),
    )(c_all, dmod_s)


def _row_specs(tb, d, n_full, n_vec):
    full = pl.BlockSpec((tb, d), lambda i: (i, 0))
    vec = pl.BlockSpec((1, d), lambda i: (0, 0))
    return [full] * n_full + [vec] * n_vec


def _norm_mod(x, w, sc, sh, name):
    t, d = x.shape
    tb = _tile(t, 256, SUBLANES)

    def body(x_ref, w_ref, sc_ref, sh_ref, o_ref):
        xv = x_ref[...]
        o_ref[...] = (xv * _rms_scale(xv) * w_ref[...] * (1.0 + sc_ref[...]) + sh_ref[...]).astype(o_ref.dtype)

    return pl.pallas_call(
        body, name=name, grid=(t // tb,), in_specs=_row_specs(tb, d, 1, 3),
        out_specs=pl.BlockSpec((tb, d), lambda i: (i, 0)), out_shape=jax.ShapeDtypeStruct((t, d), BF16),
        compiler_params=_params(("parallel",)),
    )(x, w, sc, sh)


def _norm_mod_bwd(x, w, sc, dh, dres, name):
    t, d = x.shape
    tb = _tile(t, 256, SUBLANES)

    def body(x_ref, w_ref, sc_ref, dh_ref, dres_ref, dx_ref, dw_ref, dsc_ref, dsh_ref):
        @pl.when(pl.program_id(0) == 0)
        def _():
            dw_ref[...] = jnp.zeros_like(dw_ref)
            dsc_ref[...] = jnp.zeros_like(dsc_ref)
            dsh_ref[...] = jnp.zeros_like(dsh_ref)

        xv = x_ref[...]
        r = _rms_scale(xv)
        xn = xv * r
        g = dh_ref[...].astype(F32)
        wv, one_sc = w_ref[...], 1.0 + sc_ref[...]
        gxn = g * xn
        dsh_ref[...] += jnp.sum(g, axis=0, keepdims=True)
        dsc_ref[...] += jnp.sum(gxn, axis=0, keepdims=True) * wv
        dw_ref[...] += jnp.sum(gxn, axis=0, keepdims=True) * one_sc
        dxn = g * (wv * one_sc)
        dx_ref[...] = dres_ref[...] + r * (dxn - xn * jnp.mean(dxn * xn, axis=-1, keepdims=True))

    vec_out = pl.BlockSpec((1, d), lambda i: (0, 0))
    return pl.pallas_call(
        body, name=name, grid=(t // tb,),
        in_specs=[pl.BlockSpec((tb, d), lambda i: (i, 0)), pl.BlockSpec((1, d), lambda i: (0, 0)),
                  pl.BlockSpec((1, d), lambda i: (0, 0)), pl.BlockSpec((tb, d), lambda i: (i, 0)),
                  pl.BlockSpec((tb, d), lambda i: (i, 0))],
        out_specs=[pl.BlockSpec((tb, d), lambda i: (i, 0)), vec_out, vec_out, vec_out],
        out_shape=[jax.ShapeDtypeStruct((t, d), F32)] + [jax.ShapeDtypeStruct((1, d), F32)] * 3,
        compiler_params=_params(("arbitrary",)),
    )(x, w, sc, dh, dres)


def _resid(x, y, w, gt, name):
    t, d = x.shape
    tb = _tile(t, 256, SUBLANES)

    def body(x_ref, y_ref, w_ref, gt_ref, o_ref):
        yv = y_ref[...]
        o_ref[...] = x_ref[...] + gt_ref[...] * (yv * _rms_scale(yv) * w_ref[...])

    return pl.pallas_call(
        body, name=name, grid=(t // tb,), in_specs=_row_specs(tb, d, 2, 2),
        out_specs=pl.BlockSpec((tb, d), lambda i: (i, 0)), out_shape=jax.ShapeDtypeStruct((t, d), F32),
        compiler_params=_params(("parallel",)),
    )(x, y, w, gt)


def _loss_head(x2, y2, w, gt, target, name):
    t, d = x2.shape
    tb = _tile(t, 256, SUBLANES)

    def body(x_ref, y_ref, tg_ref, w_ref, gt_ref, do_ref, loss_ref):
        @pl.when(pl.program_id(0) == 0)
        def _():
            loss_ref[...] = jnp.zeros_like(loss_ref)

        yv = y_ref[...]
        out = x_ref[...] + gt_ref[...] * (yv * _rms_scale(yv) * w_ref[...])
        err = out - tg_ref[...]
        do_ref[...] = err * (1.0 / d)
        per_tok = jnp.mean(err * err, axis=-1, keepdims=True)
        loss_ref[...] += 0.5 * jnp.sum(per_tok, axis=0, keepdims=True)

    return pl.pallas_call(
        body, name=name, grid=(t // tb,), in_specs=_row_specs(tb, d, 3, 2),
        out_specs=[pl.BlockSpec((tb, d), lambda i: (i, 0)), pl.BlockSpec((1, LANES), lambda i: (0, 0))],
        out_shape=[jax.ShapeDtypeStruct((t, d), F32), jax.ShapeDtypeStruct((1, LANES), F32)],
        compiler_params=_params(("arbitrary",)),
    )(x2, y2, target, w, gt)


def _resid_bwd(dout, y, w, gt, name):
    t, d = y.shape
    tb = _tile(t, 256, SUBLANES)

    def body(do_ref, y_ref, w_ref, gt_ref, dy_ref, dgt_ref, dw_ref):
        @pl.when(pl.program_id(0) == 0)
        def _():
            dgt_ref[...] = jnp.zeros_like(dgt_ref)
            dw_ref[...] = jnp.zeros_like(dw_ref)

        yv, g = y_ref[...], do_ref[...]
        r = _rms_scale(yv)
        yn = yv * r
        wv, gtv = w_ref[...], gt_ref[...]
        gyn = jnp.sum(g * yn, axis=0, keepdims=True)
        dgt_ref[...] += gyn * wv
        dw_ref[...] += gyn * gtv
        dyn = g * (gtv * wv)
        dy_ref[...] = (r * (dyn - yn * jnp.mean(dyn * yn, axis=-1, keepdims=True))).astype(dy_ref.dtype)

    vec_out = pl.BlockSpec((1, d), lambda i: (0, 0))
    return pl.pallas_call(
        body, name=name, grid=(t // tb,), in_specs=_row_specs(tb, d, 2, 2),
        out_specs=[pl.BlockSpec((tb, d), lambda i: (i, 0)), vec_out, vec_out],
        out_shape=[jax.ShapeDtypeStruct((t, d), BF16)] + [jax.ShapeDtypeStruct((1, d), F32)] * 2,
        compiler_params=_params(("arbitrary",)),
    )(dout, y, w, gt)


def _relu2(a1, name):
    t, n = a1.shape
    tb, tn = _elementwise_tiles(t, n)

    def body(a_ref, o_ref):
        r = jnp.maximum(a_ref[...], 0.0)
        o_ref[...] = (r * r).astype(o_ref.dtype)

    spec = pl.BlockSpec((tb, tn), lambda i, j: (i, j))
    return pl.pallas_call(body, name=name, grid=(t // tb, n // tn), in_specs=[spec], out_specs=spec,
                          out_shape=jax.ShapeDtypeStruct((t, n), BF16), compiler_params=_params(("parallel", "parallel")))(a1)


def _relu2_bwd(a1, dr, name):
    t, n = a1.shape
    tb, tn = _elementwise_tiles(t, n)

    def body(a_ref, g_ref, o_ref):
        o_ref[...] = (2.0 * jnp.maximum(a_ref[...], 0.0) * g_ref[...]).astype(o_ref.dtype)

    spec = pl.BlockSpec((tb, tn), lambda i, j: (i, j))
    return pl.pallas_call(body, name=name, grid=(t // tb, n // tn), in_specs=[spec, spec], out_specs=spec,
                          out_shape=jax.ShapeDtypeStruct((t, n), BF16), compiler_params=_params(("parallel", "parallel")))(a1, dr)


def _add_halves(own, sib, name):
    _, r, c = own.shape
    tr, tc = _elementwise_tiles(r, c)

    def body(a_ref, b_ref, o_ref):
        o_ref[...] = (a_ref[...] + b_ref[...].astype(F32)).astype(o_ref.dtype)

    spec = pl.BlockSpec((1, tr, tc), lambda j, i, k: (j, i, k))
    return pl.pallas_call(
        body, name=name, grid=(N_CHIP, r // tr, c // tc), in_specs=[spec, spec], out_specs=spec,
        out_shape=jax.ShapeDtypeStruct(own.shape, BF16), compiler_params=_params(("parallel", "parallel", "parallel")),
    )(own, sib)


def _sum_chips(recv, name):
    _, r, c = recv.shape
    tr, tc = _elementwise_tiles(r, c)

    def body(x_ref, o_ref):
        acc = x_ref[0].astype(F32)
        for j in range(1, N_CHIP):
            acc = acc + x_ref[j].astype(F32)
        o_ref[...] = acc

    return pl.pallas_call(
        body, name=name, grid=(r // tr, c // tc), in_specs=[pl.BlockSpec((N_CHIP, tr, tc), lambda i, j: (0, i, j))],
        out_specs=pl.BlockSpec((tr, tc), lambda i, j: (i, j)), out_shape=jax.ShapeDtypeStruct((r, c), F32),
        compiler_params=_params(("parallel", "parallel")),
    )(recv)


def _adamw(w, g_parts, m, v, name):
    r, c = w.shape
    tr, tc = _elementwise_tiles(r, c)
    n_g = len(g_parts)
    c1 = 1.0 / (1.0 - ADAM_B1 ** ADAM_STEP)
    c2 = 1.0 / (1.0 - ADAM_B2 ** ADAM_STEP)

    def body(*refs):
        w_ref, g_refs, m_ref, v_ref = refs[0], refs[1:1 + n_g], refs[1 + n_g], refs[2 + n_g]
        g_out, d_out, m_out, v_out = refs[3 + n_g:]
        g = g_refs[0][...]
        for gr in g_refs[1:]:
            g = g + gr[...]
        mn = ADAM_B1 * m_ref[...] + (1.0 - ADAM_B1) * g
        vn = ADAM_B2 * v_ref[...] + (1.0 - ADAM_B2) * (g * g)
        g_out[...] = g
        m_out[...] = mn
        v_out[...] = vn
        d_out[...] = -ADAM_LR * ((mn * c1) / (jnp.sqrt(vn * c2) + ADAM_EPS) + ADAM_WD * w_ref[...])

    spec = pl.BlockSpec((tr, tc), lambda i, j: (i, j))
    return pl.pallas_call(
        body, name=name, grid=(r // tr, c // tc), in_specs=[spec] * (3 + n_g), out_specs=[spec] * 4,
        out_shape=[jax.ShapeDtypeStruct((r, c), F32)] * 4, compiler_params=_params(("parallel", "parallel")),
    )(w, *g_parts, m, v)


def _conv_taps(u, t):
    rows = lax.broadcasted_iota(jnp.int32, u.shape, 0)
    return [u] + [jnp.where(rows >= dd, pltpu.roll(u, dd, 0), 0.0) for dd in range(1, CONV_K)]


def _conv_fwd(proj, conv_w, col0, name):
    t = proj.shape[0]
    ch = conv_w.shape[1]

    def body(u_ref, w_ref, o_ref):
        taps = _conv_taps(u_ref[...], t)
        wv = w_ref[...]
        y = taps[0] * wv[CONV_K - 1:CONV_K]
        for dd in range(1, CONV_K):
            y = y + taps[dd] * wv[CONV_K - 1 - dd:CONV_K - dd]
        o_ref[...] = _silu(y)

    return pl.pallas_call(
        body, name=name, grid=(ch // LANES,),
        in_specs=[pl.BlockSpec((t, LANES), lambda j: (0, col0 + j)), pl.BlockSpec((CONV_K, LANES), lambda j: (0, j))],
        out_specs=pl.BlockSpec((t, LANES), lambda j: (0, j)), out_shape=jax.ShapeDtypeStruct((t, ch), F32),
        compiler_params=_params(("parallel",)),
    )(proj, conv_w)


def _conv_bwd(proj, conv_w, ds, col0, name):
    t = proj.shape[0]
    ch = conv_w.shape[1]

    def body(u_ref, w_ref, ds_ref, du_ref, dw_ref):
        u = u_ref[...]
        taps = _conv_taps(u, t)
        wv = w_ref[...]
        y = taps[0] * wv[CONV_K - 1:CONV_K]
        for dd in range(1, CONV_K):
            y = y + taps[dd] * wv[CONV_K - 1 - dd:CONV_K - dd]
        sg = jax.nn.sigmoid(y)
        dy = ds_ref[...] * (sg * (1.0 + y * (1.0 - sg)))
        rows = lax.broadcasted_iota(jnp.int32, u.shape, 0)
        du = dy * wv[CONV_K - 1:CONV_K]
        for dd in range(1, CONV_K):
            ahead = jnp.where(rows < t - dd, pltpu.roll(dy, t - dd, 0), 0.0)
            du = du + ahead * wv[CONV_K - 1 - dd:CONV_K - dd]
        du_ref[...] = du.astype(du_ref.dtype)
        dws = [jnp.sum(dy * taps[CONV_K - 1 - j], axis=0, keepdims=True) for j in range(CONV_K)]
        dw_ref[...] = jnp.concatenate(dws, axis=0)

    return pl.pallas_call(
        body, name=name, grid=(ch // LANES,),
        in_specs=[pl.BlockSpec((t, LANES), lambda j: (0, col0 + j)), pl.BlockSpec((CONV_K, LANES), lambda j: (0, j)),
                  pl.BlockSpec((t, LANES), lambda j: (0, j))],
        out_specs=[pl.BlockSpec((t, LANES), lambda j: (0, j)), pl.BlockSpec((CONV_K, LANES), lambda j: (0, j))],
        out_shape=[jax.ShapeDtypeStruct((t, ch), BF16), jax.ShapeDtypeStruct((CONV_K, ch), F32)],
        compiler_params=_params(("parallel",)),
    )(proj, conv_w, ds)


def _hg_block(st, q, fl, vi, g, l0, l1, nw):
    hs = range(len(st))
    tb = q[0].shape[0]
    ln = HG_SUB
    lb = [jax.nn.sigmoid(l0[h] - l1[h]) for h in hs]
    rows = lax.broadcasted_iota(jnp.int32, (ln, HEAD), 0)
    tri = (lax.broadcasted_iota(jnp.int32, (ln, ln), 0) >= lax.broadcasted_iota(jnp.int32, (ln, ln), 1)).astype(F32)
    st = list(st)
    outs = [[] for _ in hs]
    for i in range(tb // ln):
        sl = slice(i * ln, (i + 1) * ln)
        qs, vs = [q[h][sl] for h in hs], [vi[h][sl] for h in hs]
        f = [lb[h] + (1.0 - lb[h]) * jax.nn.sigmoid(fl[h][sl]) for h in hs]
        k = [1.0 - f[h] for h in hs]
        b = [jnp.dot(tri, jnp.log(f[h]), precision=HI, preferred_element_type=F32) for h in hs]
        o = [lax.dot_general((qs[h] * jnp.exp(b[h])).astype(BF16), st[h].astype(BF16), NT_DIMS, preferred_element_type=F32)
             for h in hs]
        for s in range(ln):
            e = [jnp.exp(jnp.where(rows >= s, b[h] - b[h][s:s + 1], -1e30)) for h in hs]
            a = [jnp.sum(qs[h] * e[h] * k[h][s:s + 1], axis=-1, keepdims=True) for h in hs]
            o = [o[h] + a[h] * vs[h][s:s + 1] for h in hs]
        kt = [k[h] * jnp.exp(b[h][ln - 1:ln] - b[h]) for h in hs]
        upd = [lax.dot_general(vs[h].astype(BF16), kt[h].astype(BF16), TN_DIMS, preferred_element_type=F32) for h in hs]
        st = [st[h] * jnp.exp(b[h][ln - 1:ln]) + upd[h] for h in hs]
        for h in hs:
            outs[h].append(o[h])
    o = [jnp.concatenate(outs[h], axis=0) for h in hs]
    out = [o[h] * _rms_scale(o[h]) * nw * _silu(g[h]) for h in hs]
    return st, out


def _head_cols(h):
    return slice(h * HEAD, (h + 1) * HEAD)


def _head_groups(n_heads):
    g = min(HEAD_GROUP, n_heads)
    return [list(range(i, min(i + g, n_heads))) for i in range(0, n_heads, g)]


def _hg_in_specs(n_heads, tb, time_index):
    hw = n_heads * HEAD
    cols = [pl.BlockSpec((tb, hw), functools.partial(lambda part, j: (time_index(j), part), part)) for part in range(4)]
    head_rows = pl.BlockSpec((n_heads, 1, HEAD), lambda j: (0, 0, 0))
    return cols + [head_rows, head_rows, pl.BlockSpec((1, HEAD), lambda j: (0, 0))]


def _hgrn2_fwd(proj, l0, l1, nw, n_heads, name):
    t = proj.shape[0]
    hw = n_heads * HEAD
    tb = _tile(t, HG_BLOCK, HG_SUB)
    nb = t // tb

    def body(q_ref, f_ref, i_ref, g_ref, l0_ref, l1_ref, nw_ref, o_ref, save_ref, st_ref):
        @pl.when(pl.program_id(0) == 0)
        def _():
            st_ref[...] = jnp.zeros_like(st_ref)

        for hs in _head_groups(n_heads):
            st = [st_ref[h] for h in hs]
            for h, s in zip(hs, st):
                save_ref[h] = s
            st, out = _hg_block(st, *[[r[:, _head_cols(h)] for h in hs] for r in (q_ref, f_ref, i_ref, g_ref)],
                                [l0_ref[h] for h in hs], [l1_ref[h] for h in hs], nw_ref[...])
            for h, s, o in zip(hs, st, out):
                st_ref[h] = s
                o_ref[:, _head_cols(h)] = o.astype(o_ref.dtype)

    return pl.pallas_call(
        body, name=name, grid=(nb,), in_specs=_hg_in_specs(n_heads, tb, lambda j: j),
        out_specs=[pl.BlockSpec((tb, hw), lambda j: (j, 0)),
                   pl.BlockSpec((None, n_heads, HEAD, HEAD), lambda j: (j, 0, 0, 0))],
        out_shape=[jax.ShapeDtypeStruct((t, hw), BF16), jax.ShapeDtypeStruct((nb, n_heads, HEAD, HEAD), F32)],
        scratch_shapes=[pltpu.VMEM((n_heads, HEAD, HEAD), F32)], compiler_params=_params(("arbitrary",)),
    )(proj, proj, proj, proj, l0, l1, nw)


def _hgrn2_bwd(proj, l0, l1, nw, saved, d_ocat, n_heads, name):
    t = proj.shape[0]
    tb = _tile(t, HG_BLOCK, HG_SUB)
    nb = t // tb
    rev = lambda j: nb - 1 - j

    hw = n_heads * HEAD

    def body(q_ref, f_ref, i_ref, g_ref, l0_ref, l1_ref, nw_ref, save_ref, do_ref,
             dp_ref, dl0_ref, dl1_ref, dnw_ref, dst_ref):
        @pl.when(pl.program_id(0) == 0)
        def _():
            dst_ref[...] = jnp.zeros_like(dst_ref)
            dl0_ref[...] = jnp.zeros_like(dl0_ref)
            dl1_ref[...] = jnp.zeros_like(dl1_ref)
            dnw_ref[...] = jnp.zeros_like(dnw_ref)

        dnw_acc = jnp.zeros((1, HEAD), F32)
        for hs in _head_groups(n_heads):
            _, vjp = jax.vjp(_hg_block, [save_ref[h] for h in hs],
                             *[[r[:, _head_cols(h)] for h in hs] for r in (q_ref, f_ref, i_ref, g_ref)],
                             [l0_ref[h] for h in hs], [l1_ref[h] for h in hs], nw_ref[...])
            dst, dq, df, di, dg, dl0, dl1, dnw = vjp(([dst_ref[h] for h in hs], [do_ref[:, _head_cols(h)] for h in hs]))
            for i, h in enumerate(hs):
                dst_ref[h] = dst[i]
                for part, val in enumerate((dq, df, di, dg)):
                    dp_ref[:, part * hw + h * HEAD:part * hw + (h + 1) * HEAD] = val[i].astype(dp_ref.dtype)
                dl0_ref[h] += dl0[i]
                dl1_ref[h] += dl1[i]
            dnw_acc = dnw_acc + dnw
        dnw_ref[...] += dnw_acc

    head_rows = pl.BlockSpec((n_heads, 1, HEAD), lambda j: (0, 0, 0))
    return pl.pallas_call(
        body, name=name, grid=(nb,),
        in_specs=_hg_in_specs(n_heads, tb, rev) + [pl.BlockSpec((None, n_heads, HEAD, HEAD), lambda j: (rev(j), 0, 0, 0)),
                                                   pl.BlockSpec((tb, hw), lambda j: (rev(j), 0))],
        out_specs=[pl.BlockSpec((tb, 4 * hw), lambda j: (rev(j), 0)), head_rows, head_rows,
                   pl.BlockSpec((1, HEAD), lambda j: (0, 0))],
        out_shape=[jax.ShapeDtypeStruct((t, 4 * hw), BF16)] + [jax.ShapeDtypeStruct((n_heads, 1, HEAD), F32)] * 2
        + [jax.ShapeDtypeStruct((1, HEAD), F32)],
        scratch_shapes=[pltpu.VMEM((n_heads, HEAD, HEAD), F32)], compiler_params=_params(("arbitrary",)),
    )(proj, proj, proj, proj, l0, l1, nw, saved, d_ocat)


def _dot_hi(a, b, dims=(((1,), (0,)), ((), ()))):
    return lax.dot_general(a, b, dims, precision=HI, preferred_element_type=F32)


def _dot_bf16(a, b, dims=(((1,), (0,)), ((), ()))):
    return lax.dot_general(a.astype(BF16), b.astype(BF16), dims, preferred_element_type=F32)


def _inv_unit_lower_raw(ms):
    hs = range(len(ms))
    c = ms[0].shape[0]
    r = lax.broadcasted_iota(jnp.int32, (c, c), 0)
    q = lax.broadcasted_iota(jnp.int32, (c, c), 1)
    eye = (r == q).astype(F32)
    md = [jnp.where((r // GDN_INV_BLOCK) == (q // GDN_INV_BLOCK), ms[h], 0.0) for h in hs]
    p = [-md[h] for h in hs]
    t16 = [eye + p[h] for h in hs]
    for _ in range(int(math.log2(GDN_INV_BLOCK)) - 1):
        p = [_dot_hi(p[h], p[h]) for h in hs]
        t16 = [t16[h] + _dot_hi(t16[h], p[h]) for h in hs]
    p = [-_dot_hi(t16[h], ms[h] - md[h]) for h in hs]
    t2 = [eye + p[h] for h in hs]
    for _ in range(int(math.log2(c // GDN_INV_BLOCK)) - 1):
        p = [_dot_hi(p[h], p[h]) for h in hs]
        t2 = [t2[h] + _dot_hi(t2[h], p[h]) for h in hs]
    return [_dot_hi(t2[h], t16[h]) for h in hs]


@jax.custom_vjp
def _inv_unit_lower(ms):
    return _inv_unit_lower_raw(ms)


def _inv_fwd(ms):
    ts = _inv_unit_lower_raw(ms)
    return ts, ts


def _inv_bwd(ts, dts):
    hs = range(len(ts))
    inner = [_dot_hi(ts[h], dts[h], TN_DIMS) for h in hs]
    return ([-_dot_hi(inner[h], ts[h], NT_DIMS) for h in hs],)


_inv_unit_lower.defvjp(_inv_fwd, _inv_bwd)


def _gdn_block(inverse, onehots, st, qc, kc, vc, g, ab, alog_row, dtb_row, nw):
    hs = range(len(st))
    c = qc[0].shape[0]
    lane_sum = lambda v: jnp.sum(v, axis=-1, keepdims=True)
    a = [lane_sum(ab * onehots[h][0]) for h in hs]
    bb = [lane_sum(ab * onehots[h][1]) for h in hs]
    alog = [lane_sum(alog_row * onehots[h][0]) for h in hs]
    dtb = [lane_sum(dtb_row * onehots[h][0]) for h in hs]
    la = [-jnp.exp(alog[h]) * _softplus(a[h] + dtb[h]) for h in hs]
    beta = [jax.nn.sigmoid(bb[h]) for h in hs]
    q = [qc[h] * lax.rsqrt(lane_sum(qc[h] * qc[h]) + EPS) * (HEAD ** -0.5) for h in hs]
    k = [kc[h] * lax.rsqrt(lane_sum(kc[h] * kc[h]) + EPS) for h in hs]
    r = lax.broadcasted_iota(jnp.int32, (c, c), 0)
    s = lax.broadcasted_iota(jnp.int32, (c, c), 1)
    tri = (r >= s).astype(F32)
    g_cc = [_dot_hi(tri, jnp.broadcast_to(la[h], (c, c))) for h in hs]
    g_cl = [_dot_hi(tri, jnp.broadcast_to(la[h], (c, HEAD))) for h in hs]
    gamma = [jnp.exp(jnp.where(r >= s, g_cc[h] - g_cc[h].T, -1e30)) for h in hs]
    kk = [_dot_bf16(k[h], k[h], NT_DIMS) for h in hs]
    m = [jnp.where(r > s, beta[h] * kk[h] * gamma[h], 0.0) for h in hs]
    tm = inverse(m)
    eg = [jnp.exp(g_cl[h]) for h in hs]
    rhs = [jnp.concatenate([vc[h] * beta[h], k[h] * (beta[h] * eg[h])], axis=1) for h in hs]
    sol = [_dot_hi(tm[h], rhs[h]) for h in hs]
    qk = [_dot_bf16(q[h], k[h], NT_DIMS) * gamma[h] for h in hs]
    g_last = [g_cl[h][c - 1:c] for h in hs]
    k_tail = [k[h] * jnp.exp(g_last[h] - g_cl[h]) for h in hs]
    v_new = [sol[h][:, :HEAD] - _dot_bf16(sol[h][:, HEAD:], st[h], NT_DIMS) for h in hs]
    o_st = [_dot_bf16(q[h] * eg[h], st[h], NT_DIMS) for h in hs]
    o = [o_st[h] + _dot_bf16(qk[h], v_new[h]) for h in hs]
    upd = [_dot_bf16(v_new[h], k_tail[h], TN_DIMS) for h in hs]
    st = [st[h] * jnp.exp(g_last[h]) + upd[h] for h in hs]
    out = [o[h] * _rms_scale(o[h]) * nw * _silu(g[h]) for h in hs]
    return st, out


def _head_onehots(n_heads, h):
    lane = lax.broadcasted_iota(jnp.int32, (1, LANES), 1)
    return (lane == h).astype(F32), (lane == n_heads + h).astype(F32)


def _gdn_in_specs(n_heads, c, time_index):
    hw = n_heads * HEAD
    qkv = [pl.BlockSpec((c, hw), functools.partial(lambda part, j: (time_index(j), part), part)) for part in range(3)]
    row = pl.BlockSpec((1, LANES), lambda j: (0, 0))
    return qkv + [pl.BlockSpec((c, hw), lambda j: (time_index(j), 7)),
                  pl.BlockSpec((c, LANES), lambda j: (time_index(j), 8 * n_heads)), row, row, row]


def _gdn_fwd(qkv, proj, alog_row, dtb_row, nw, n_heads, name):
    t = qkv.shape[0]
    hw = n_heads * HEAD
    c = _tile(t, GDN_CHUNK, GDN_CHUNK)
    nb = t // c

    def body(q_ref, k_ref, v_ref, g_ref, ab_ref, al_ref, dt_ref, nw_ref, o_ref, save_ref, st_ref):
        @pl.when(pl.program_id(0) == 0)
        def _():
            st_ref[...] = jnp.zeros_like(st_ref)

        for hs in _head_groups(n_heads):
            st = [st_ref[h] for h in hs]
            for h, s in zip(hs, st):
                save_ref[h] = s
            st, out = _gdn_block(_inv_unit_lower_raw, [_head_onehots(n_heads, h) for h in hs], st,
                                 *[[r[:, _head_cols(h)] for h in hs] for r in (q_ref, k_ref, v_ref, g_ref)],
                                 ab_ref[...], al_ref[...], dt_ref[...], nw_ref[...])
            for h, s, o in zip(hs, st, out):
                st_ref[h] = s
                o_ref[:, _head_cols(h)] = o.astype(o_ref.dtype)

    return pl.pallas_call(
        body, name=name, grid=(nb,), in_specs=_gdn_in_specs(n_heads, c, lambda j: j),
        out_specs=[pl.BlockSpec((c, hw), lambda j: (j, 0)),
                   pl.BlockSpec((None, n_heads, HEAD, HEAD), lambda j: (j, 0, 0, 0))],
        out_shape=[jax.ShapeDtypeStruct((t, hw), BF16), jax.ShapeDtypeStruct((nb, n_heads, HEAD, HEAD), F32)],
        scratch_shapes=[pltpu.VMEM((n_heads, HEAD, HEAD), F32)], compiler_params=_params(("arbitrary",)),
    )(qkv, qkv, qkv, proj, proj, alog_row, dtb_row, nw)


def _gdn_bwd(qkv, proj, alog_row, dtb_row, nw, saved, d_ocat, n_heads, name):
    t = qkv.shape[0]
    c = _tile(t, GDN_CHUNK, GDN_CHUNK)
    nb = t // c
    rev = lambda j: nb - 1 - j

    hw = n_heads * HEAD

    def body(q_ref, k_ref, v_ref, g_ref, ab_ref, al_ref, dt_ref, nw_ref, save_ref, do_ref,
             dqkv_ref, dg_ref, dab_ref, dal_ref, ddt_ref, dnw_ref, dst_ref):
        @pl.when(pl.program_id(0) == 0)
        def _():
            dst_ref[...] = jnp.zeros_like(dst_ref)
            dal_ref[...] = jnp.zeros_like(dal_ref)
            ddt_ref[...] = jnp.zeros_like(ddt_ref)
            dnw_ref[...] = jnp.zeros_like(dnw_ref)

        dab_acc = jnp.zeros((c, LANES), F32)
        row_acc = [jnp.zeros((1, LANES), F32)] * 3
        for hs in _head_groups(n_heads):
            fn = functools.partial(_gdn_block, _inv_unit_lower, [_head_onehots(n_heads, h) for h in hs])
            _, vjp = jax.vjp(fn, [save_ref[h] for h in hs],
                             *[[r[:, _head_cols(h)] for h in hs] for r in (q_ref, k_ref, v_ref, g_ref)],
                             ab_ref[...], al_ref[...], dt_ref[...], nw_ref[...])
            dst, dq, dk, dv, dg, dab, dal, ddt, dnw = vjp(([dst_ref[h] for h in hs], [do_ref[:, _head_cols(h)] for h in hs]))
            for i, h in enumerate(hs):
                dst_ref[h] = dst[i]
                for part, val in enumerate((dq, dk, dv)):
                    dqkv_ref[:, part * hw + h * HEAD:part * hw + (h + 1) * HEAD] = val[i]
                dg_ref[:, _head_cols(h)] = dg[i].astype(dg_ref.dtype)
            dab_acc = dab_acc + dab
            row_acc = [acc + val for acc, val in zip(row_acc, (dal, ddt, dnw))]
        dab_ref[...] = dab_acc
        dal_ref[...] += row_acc[0]
        ddt_ref[...] += row_acc[1]
        dnw_ref[...] += row_acc[2]

    row = pl.BlockSpec((1, LANES), lambda j: (0, 0))
    return pl.pallas_call(
        body, name=name, grid=(nb,),
        in_specs=_gdn_in_specs(n_heads, c, rev) + [pl.BlockSpec((None, n_heads, HEAD, HEAD), lambda j: (rev(j), 0, 0, 0)),
                                                   pl.BlockSpec((c, hw), lambda j: (rev(j), 1))],
        out_specs=[pl.BlockSpec((c, 3 * hw), lambda j: (rev(j), 0)), pl.BlockSpec((c, hw), lambda j: (rev(j), 0)),
                   pl.BlockSpec((c, LANES), lambda j: (rev(j), 0)), row, row, row],
        out_shape=[jax.ShapeDtypeStruct((t, 3 * hw), F32), jax.ShapeDtypeStruct((t, hw), BF16),
                   jax.ShapeDtypeStruct((t, LANES), F32)] + [jax.ShapeDtypeStruct((1, LANES), F32)] * 3,
        scratch_shapes=[pltpu.VMEM((n_heads, HEAD, HEAD), F32)], compiler_params=_params(("arbitrary",)),
    )(qkv, qkv, qkv, proj, proj, alog_row, dtb_row, nw, saved, d_ocat)


def _pad_lanes(v, n):
    v = v.reshape(1, -1)
    return jnp.pad(v, ((0, 0), (0, n - v.shape[1])))


def _pack_rows(vecs):
    flat = jnp.concatenate([v.reshape(-1) for v in vecs])
    offs, o = [], 0
    for v in vecs:
        offs.append((o, v.size))
        o += v.size
    per_row = -(-o // (SUBLANES * LANES)) * LANES
    flat = jnp.pad(flat, (0, SUBLANES * per_row - o))
    return flat.reshape(SUBLANES, per_row), offs


def _unpack(gathered, offs):
    per_dev = gathered.reshape(N_DEV, -1)
    return [per_dev[:, o:o + n] for o, n in offs]


def _sum_devices(part):
    acc = part[0]
    for i in range(1, N_DEV):
        acc = acc + part[i]
    return acc


def kernel(x, c, w_ada, b_ada, pre_mix_norm, post_mix_norm, pre_ffn_norm, post_ffn_norm, w_in, hg_lb_logits, hg_norm, gdn_conv_w, gdn_a_log, gdn_dt_bias, gdn_norm, w_out, w_ff1, w_ff2, loss_target, m_w_ada, m_b_ada, m_pre_mix_norm, m_post_mix_norm, m_pre_ffn_norm, m_post_ffn_norm, m_w_in, m_hg_lb_logits, m_hg_norm, m_gdn_conv_w, m_gdn_a_log, m_gdn_dt_bias, m_gdn_norm, m_w_out, m_w_ff1, m_w_ff2, v_w_ada, v_b_ada, v_pre_mix_norm, v_post_mix_norm, v_pre_ffn_norm, v_post_ffn_norm, v_w_in, v_hg_lb_logits, v_hg_norm, v_gdn_conv_w, v_gdn_a_log, v_gdn_dt_bias, v_gdn_norm, v_w_out, v_w_ff1, v_w_ff2):
    assert x.shape[0] == 1 and w_ada.shape[0] == 1 and hg_lb_logits.shape[0] == 2
    t, d = x.shape[1], x.shape[2]
    n_heads = (d // 2) // HEAD
    hw = n_heads * HEAD
    in_cols = 8 * hw + 2 * n_heads
    np_cols = 8 * hw + LANES
    d_ff = w_ff1.shape[2] * N_CHIP
    na = w_ada.shape[2]
    ax, ay, ac = lax.axis_index("x"), lax.axis_index("y"), lax.axis_index("c")
    chip = 2 * ax + ay
    dev = 4 * ax + 2 * ay + ac

    x2d, tgt = x[0], loss_target[0]

    pack1, offs1 = _pack_rows([c[0], gdn_conv_w[0]])
    c_all, convw_all = _unpack(_gather8(pack1, "gather_cond"), offs1)
    conv_sh = gdn_conv_w.shape[2]
    conv_w = jnp.concatenate([convw_all[2 * j].reshape(CONV_K, conv_sh) for j in range(N_CHIP)], axis=1)

    b_s = lax.dynamic_slice(b_ada, (0, chip * na), (1, na))
    mod_part = _mod_part(c_all, w_ada[0], b_s, "mod_part")
    pack2, offs2 = _pack_rows([mod_part])
    (mod_parts,) = _unpack(_gather8(pack2, "gather_mod"), offs2)
    mod_all = jnp.concatenate([mod_parts[2 * j].reshape(N_DEV, na) for j in range(N_CHIP)], axis=1)
    mod = lax.dynamic_slice(mod_all, (dev, 0), (1, N_MOD * d))
    sh_m, sc_m, gt_m, sh_f, sc_f, gt_f = [mod[:, i * d:(i + 1) * d] for i in range(N_MOD)]

    g_in, g_out, g_ff1, g_ff2 = _gather_weights(
        [w_in[0].astype(BF16), w_out[0].astype(BF16), w_ff1[0].astype(BF16), w_ff2[0].astype(BF16)], "gather_weights")
    w_in_f = jnp.pad(jnp.transpose(g_in, (1, 0, 2)).reshape(d, in_cols), ((0, 0), (0, np_cols - in_cols)))
    w_out_f = g_out.reshape(d, d)
    w_ff1_f = jnp.transpose(g_ff1, (1, 0, 2)).reshape(d, d_ff)
    w_ff2_f = g_ff2.reshape(d_ff, d)

    h1 = _norm_mod(x2d, pre_mix_norm, sc_m, sh_m, "norm_mod_mix")
    proj = _matmul(h1, w_in_f, "nn", F32, "mm_in", tn=640)
    l0, l1 = hg_lb_logits[0].reshape(n_heads, 1, HEAD), hg_lb_logits[1].reshape(n_heads, 1, HEAD)
    o_hg, hg_saved = _hgrn2_fwd(proj, l0, l1, hg_norm, n_heads, "hgrn2_fwd")
    qkv = _conv_fwd(proj, conv_w, 4 * n_heads, "conv_fwd")
    alog_row, dtb_row = _pad_lanes(gdn_a_log, LANES), _pad_lanes(gdn_dt_bias, LANES)
    o_gdn, gdn_saved = _gdn_fwd(qkv, proj, alog_row, dtb_row, gdn_norm, n_heads, "gdn_fwd")
    o_cat = jnp.concatenate([o_hg, o_gdn], axis=1)
    y1 = _matmul(o_cat, w_out_f, "nn", F32, "mm_out")
    x_mid = _resid(x2d, y1, post_mix_norm, gt_m, "resid_mix")

    h2 = _norm_mod(x_mid, pre_ffn_norm, sc_f, sh_f, "norm_mod_ffn")
    a1 = _matmul(h2, w_ff1_f, "nn", F32, "mm_ff1")
    r1 = _relu2(a1, "relu2")
    y2 = _matmul(r1, w_ff2_f, "nn", F32, "mm_ff2")
    d_out, loss_row = _loss_head(x_mid, y2, post_ffn_norm, gt_f, tgt, "loss_head")

    dy2, d_gt_f, d_post_ffn = _resid_bwd(d_out, y2, post_ffn_norm, gt_f, "resid_ffn_bwd")
    gw_ff2 = _matmul(r1, dy2, "tn", F32, "mm_ff2_dw")
    dr1 = _matmul(dy2, w_ff2_f, "nt", BF16, "mm_ff2_dx")
    da1 = _relu2_bwd(a1, dr1, "relu2_bwd")
    gw_ff1 = _matmul(h2, da1, "tn", F32, "mm_ff1_dw")
    dh2 = _matmul(da1, w_ff1_f, "nt", BF16, "mm_ff1_dx")
    d_mid, d_pre_ffn, d_sc_f, d_sh_f = _norm_mod_bwd(x_mid, pre_ffn_norm, sc_f, dh2, d_out, "norm_mod_ffn_bwd")

    dy1, d_gt_m, d_post_mix = _resid_bwd(d_mid, y1, post_mix_norm, gt_m, "resid_mix_bwd")
    gw_out = _matmul(o_cat, dy1, "tn", F32, "mm_out_dw")
    d_ocat = _matmul(dy1, w_out_f, "nt", F32, "mm_out_dx")
    dp_hg, dl0, dl1, d_hg_norm = _hgrn2_bwd(proj, l0, l1, hg_norm, hg_saved, d_ocat, n_heads, "hgrn2_bwd")
    dqkv, dg_g, dab, d_alog, d_dtb, d_gdn_norm = _gdn_bwd(
        qkv, proj, alog_row, dtb_row, gdn_norm, gdn_saved, d_ocat, n_heads, "gdn_bwd")
    du, d_conv_w = _conv_bwd(proj, conv_w, dqkv, 4 * n_heads, "conv_bwd")
    dproj = jnp.concatenate([dp_hg, du, dg_g, dab.astype(BF16)], axis=1)
    gw_in = _matmul(h1, dproj, "tn", F32, "mm_in_dw", tn=640)
    dh1 = _matmul(dproj, w_in_f, "nt", BF16, "mm_in_dx", tk=640)
    grad_x, d_pre_mix, d_sc_m, d_sh_m = _norm_mod_bwd(x2d, pre_mix_norm, sc_m, dh1, d_mid, "norm_mod_mix_bwd")

    d_mod = jnp.concatenate([d_sh_m, d_sc_m, d_gt_m, d_sh_f, d_sc_f, d_gt_f], axis=1)
    d_lb_logits = jnp.stack([dl0.reshape(n_heads, HEAD), dl1.reshape(n_heads, HEAD)])
    pack3, offs3 = _pack_rows([loss_row[0, :1], d_pre_mix, d_post_mix, d_pre_ffn, d_post_ffn, d_lb_logits, d_hg_norm,
                               d_conv_w, d_alog[0, :n_heads], d_dtb[0, :n_heads], d_gdn_norm, d_mod])
    parts = _unpack(_gather8(pack3, "gather_vec_grads"), offs3)
    sums = [_sum_devices(p) for p in parts[:-1]]
    loss = sums[0][0]
    dmod_all = parts[-1]
    g_b_ada = _sum_devices(dmod_all).reshape(1, N_MOD * d)
    g_conv_full = sums[7].reshape(CONV_K, N_CHIP * conv_sh)
    g_conv = lax.dynamic_slice(g_conv_full, (0, chip * conv_sh), (CONV_K, conv_sh))
    gw_ada = _wada_grad(c_all, lax.dynamic_slice(dmod_all, (0, chip * na), (N_DEV, na)), "wada_grad")

    in_sh = in_cols // N_CHIP
    ff_sh = d_ff // N_CHIP
    by_chip = [jnp.transpose(gw_in[:, :in_cols].reshape(d, N_CHIP, in_sh), (1, 0, 2)),
               gw_out.reshape(N_CHIP, d // N_CHIP, d),
               jnp.transpose(gw_ff1.reshape(d, N_CHIP, ff_sh), (1, 0, 2)),
               gw_ff2.reshape(N_CHIP, ff_sh, d)]

    def row_half(a, cc):
        rh = a.shape[1] // 2
        return lax.dynamic_slice(a, (0, cc * rh, 0), (N_CHIP, rh, a.shape[2]))

    from_sib = _sibling_exchange([row_half(a, 1 - ac).astype(BF16) for a in by_chip], "sibling_partials")
    chip_part = [_add_halves(row_half(a, ac), s, f"add_halves_{i}") for i, (a, s) in enumerate(zip(by_chip, from_sib))]
    recv = _chip_exchange(chip_part, False, "scatter_grads")
    totals = _sibling_merge([_sum_chips(rv, f"sum_chips_{i}") for i, rv in enumerate(recv)], "sibling_grads")

    big = {}
    for i, (nm, w_, m_, v_) in enumerate([("w_in", w_in, m_w_in, v_w_in), ("w_out", w_out, m_w_out, v_w_out),
                                          ("w_ff1", w_ff1, m_w_ff1, v_w_ff1), ("w_ff2", w_ff2, m_w_ff2, v_w_ff2)]):
        big[nm] = [o[None] for o in _adamw(w_[0], [totals[i]], m_[0], v_[0], f"adamw_{nm}")]
    big["w_ada"] = [o[None] for o in _adamw(w_ada[0], [gw_ada], m_w_ada[0], v_w_ada[0], "adamw_w_ada")]

    small_names = ["b_ada", "pre_mix_norm", "post_mix_norm", "pre_ffn_norm", "post_ffn_norm", "hg_lb_logits", "hg_norm",
                   "gdn_conv_w", "gdn_a_log", "gdn_dt_bias", "gdn_norm"]
    small_w = [b_ada, pre_mix_norm, post_mix_norm, pre_ffn_norm, post_ffn_norm, hg_lb_logits, hg_norm, gdn_conv_w,
               gdn_a_log, gdn_dt_bias, gdn_norm]
    small_m = [m_b_ada, m_pre_mix_norm, m_post_mix_norm, m_pre_ffn_norm, m_post_ffn_norm, m_hg_lb_logits, m_hg_norm,
               m_gdn_conv_w, m_gdn_a_log, m_gdn_dt_bias, m_gdn_norm]
    small_v = [v_b_ada, v_pre_mix_norm, v_post_mix_norm, v_pre_ffn_norm, v_post_ffn_norm, v_hg_lb_logits, v_hg_norm,
               v_gdn_conv_w, v_gdn_a_log, v_gdn_dt_bias, v_gdn_norm]
    small_g = [g_b_ada, sums[1], sums[2], sums[3], sums[4], sums[5], sums[6], g_conv, sums[8], sums[9], sums[10]]
    pw, offs_s = _pack_rows(small_w)
    pg, _ = _pack_rows(small_g)
    pm, _ = _pack_rows(small_m)
    pv, _ = _pack_rows(small_v)
    packed = _adamw(pw, [pg], pm, pv, "adamw_vectors")
    small = {}
    for nm, w_, (o, n) in zip(small_names, small_w, offs_s):
        small[nm] = [p.reshape(-1)[o:o + n].reshape(w_.shape) for p in packed]

    order = ["w_ada", "b_ada", "pre_mix_norm", "post_mix_norm", "pre_ffn_norm", "post_ffn_norm", "w_in", "hg_lb_logits",
             "hg_norm", "gdn_conv_w", "gdn_a_log", "gdn_dt_bias", "gdn_norm", "w_out", "w_ff1", "w_ff2"]
    res = {**big, **small}
    outs = [loss, grad_x[None]]
    for k in range(4):
        outs += [res[nm][k] for nm in order]
    return tuple(outs)
```

```python
import functools
import math

import jax
import jax.numpy as jnp
from jax import lax
from jax.experimental import pallas as pl
from jax.experimental.pallas import tpu as pltpu

F32 = jnp.float32
BF16 = jnp.bfloat16
HI = lax.Precision.HIGHEST
MESH = pl.DeviceIdType.MESH

LANES = 128
SUBLANES = 8
VMEM_LIMIT = 48 * 1024 * 1024
EPS = 1e-6
HEAD = 128
CONV_K = 4
GDN_CHUNK = 64
GDN_INV_BLOCK = 16
HG_SUB = 16
HG_BLOCK = 128
HEAD_GROUP = 8
N_MOD = 6
N_DEV = 8
N_CHIP = 4

ADAM_LR = 0.001
ADAM_B1 = 0.9
ADAM_B2 = 0.999
ADAM_EPS = 1e-08
ADAM_WD = 0.01
ADAM_STEP = 10

NT_DIMS = (((1,), (1,)), ((), ()))
TN_DIMS = (((0,), (0,)), ((), ()))


def _tile(dim, target, align):
    if dim <= target:
        return dim
    best = dim
    t = align
    while t <= target:
        if dim % t == 0:
            best = t
        t += align
    return best


def _elementwise_tiles(r, c):
    tc = _tile(c, 1024, LANES)
    tr = _tile(r, max(16, (256 * 1024) // tc // 16 * 16), 16)
    return tr, tc


def _params(sem):
    return pltpu.CompilerParams(dimension_semantics=sem, vmem_limit_bytes=VMEM_LIMIT)


def _silu(x):
    return x * jax.nn.sigmoid(x)


def _softplus(x):
    pos = x > 0
    return jnp.where(pos, x, 0.0) + jnp.log(1.0 + jnp.exp(jnp.where(pos, -x, x)))


def _rms_scale(x):
    return lax.rsqrt(jnp.mean(x * x, axis=-1, keepdims=True) + EPS)


def _gather8(x_shard, name):
    m_per, n = x_shard.shape
    assert m_per % SUBLANES == 0 and n % LANES == 0

    def body(x_ref, out_ref, send_sems, recv_sems, local_sem):
        x, y, c = lax.axis_index("x"), lax.axis_index("y"), lax.axis_index("c")
        me, sibling = (x, y, c), (x, y, 1 - c)
        chips = [(1 - x, y), (x, 1 - y), (1 - x, 1 - y)]

        def rows(px, py, pc):
            return out_ref.at[pl.ds((4 * px + 2 * py + pc) * m_per, m_per), :]

        def copy(k, block, to, src=None):
            return pltpu.make_async_remote_copy(
                src_ref=rows(*block) if src is None else src, dst_ref=rows(*block),
                send_sem=send_sems.at[k], recv_sem=recv_sems.at[k], device_id=to, device_id_type=MESH)

        mine = pltpu.make_async_copy(x_ref, rows(*me), local_sem)
        mine.start()
        first = [copy(0, me, sibling, src=x_ref)]
        first += [copy(1 + j, me, (*chip, c), src=x_ref) for j, chip in enumerate(chips)]
        for cp in first:
            cp.start()
        passed = [copy(4 + j, (*chip, c), sibling) for j, chip in enumerate(chips)]
        for j, chip in enumerate(chips):
            copy(1 + j, (*chip, c), me).wait_recv()
            passed[j].start()
        copy(0, sibling, me).wait_recv()
        for j, chip in enumerate(chips):
            copy(4 + j, (*chip, 1 - c), me).wait_recv()
        for cp in first + passed:
            cp.wait_send()
        mine.wait()

    return pl.pallas_call(
        body, name=name,
        out_shape=jax.ShapeDtypeStruct((N_DEV * m_per, n), x_shard.dtype),
        in_specs=[pl.BlockSpec(memory_space=pltpu.VMEM)],
        out_specs=pl.BlockSpec(memory_space=pltpu.VMEM),
        scratch_shapes=[pltpu.SemaphoreType.DMA((7,)), pltpu.SemaphoreType.DMA((7,)), pltpu.SemaphoreType.DMA],
        compiler_params=pltpu.CompilerParams(vmem_limit_bytes=VMEM_LIMIT),
    )(x_shard)


def _chip_exchange(arrs, bcast, name):
    n = len(arrs)
    out_shapes = [jax.ShapeDtypeStruct((N_CHIP,) + (a.shape if bcast else a.shape[1:]), a.dtype) for a in arrs]

    def body(*refs):
        ins, outs = refs[:n], refs[n:2 * n]
        send_sems, recv_sems, local_sems = refs[2 * n:]
        x, y, c = lax.axis_index("x"), lax.axis_index("y"), lax.axis_index("c")
        me = 2 * x + y
        peers = [(1 - x, y), (x, 1 - y), (1 - x, 1 - y)]
        copies = []
        for a in range(n):
            src_own = ins[a] if bcast else ins[a].at[me]
            loc = pltpu.make_async_copy(src_own, outs[a].at[me], local_sems.at[a])
            loc.start()
            copies.append(loc)
        remote = []
        for a in range(n):
            for k, (px, py) in enumerate(peers):
                them = 2 * px + py
                cp = pltpu.make_async_remote_copy(
                    src_ref=ins[a] if bcast else ins[a].at[them], dst_ref=outs[a].at[me],
                    send_sem=send_sems.at[3 * a + k], recv_sem=recv_sems.at[3 * a + k],
                    device_id=(px, py, c), device_id_type=MESH)
                cp.start()
                remote.append((cp, a, k, them))
        for cp, a, k, them in remote:
            pltpu.make_async_remote_copy(
                src_ref=ins[a] if bcast else ins[a].at[them], dst_ref=outs[a].at[them],
                send_sem=send_sems.at[3 * a + k], recv_sem=recv_sems.at[3 * a + k],
                device_id=(x, y, c), device_id_type=MESH).wait_recv()
        for cp, a, k, them in remote:
            cp.wait_send()
        for loc in copies:
            loc.wait()

    hbm = pl.BlockSpec(memory_space=pltpu.HBM)
    return pl.pallas_call(
        body, name=name, out_shape=out_shapes, in_specs=[hbm] * n, out_specs=[hbm] * n,
        scratch_shapes=[pltpu.SemaphoreType.DMA((3 * n,)), pltpu.SemaphoreType.DMA((3 * n,)), pltpu.SemaphoreType.DMA((n,))],
    )(*arrs)


def _gather_weights(arrs, name):
    n = len(arrs)
    out_shapes = [jax.ShapeDtypeStruct((N_CHIP,) + a.shape, a.dtype) for a in arrs]

    def body(*refs):
        ins, outs = refs[:n], refs[n:2 * n]
        ici_send, ici_recv, d2d_send, d2d_recv = refs[2 * n:]
        x, y, c = lax.axis_index("x"), lax.axis_index("y"), lax.axis_index("c")
        me = 2 * x + y
        sibling = (x, y, 1 - c)
        peers = [(1 - x, y), (x, 1 - y), (1 - x, 1 - y)]

        def half(a, cc):
            rh = arrs[a].shape[0] // 2
            return pl.ds(pl.multiple_of(cc * rh, 16), rh)

        sent = []
        for a in range(n):
            for k, (px, py) in enumerate(peers):
                cp = pltpu.make_async_remote_copy(
                    src_ref=ins[a].at[half(a, c)], dst_ref=outs[a].at[me, half(a, c)],
                    send_sem=ici_send.at[3 * a + k], recv_sem=ici_recv.at[3 * a + k],
                    device_id=(px, py, c), device_id_type=MESH)
                cp.start()
                sent.append(cp)
        for a in range(n):
            for k, (px, py) in enumerate(peers):
                landed = outs[a].at[2 * px + py, half(a, c)]
                pltpu.make_async_remote_copy(
                    src_ref=landed, dst_ref=landed, send_sem=ici_send.at[3 * a + k], recv_sem=ici_recv.at[3 * a + k],
                    device_id=(px, py, c), device_id_type=MESH).wait_recv()
                fwd = pltpu.make_async_remote_copy(
                    src_ref=landed, dst_ref=landed, send_sem=d2d_send.at[3 * a + k], recv_sem=d2d_recv.at[3 * a + k],
                    device_id=sibling, device_id_type=MESH)
                fwd.start()
                sent.append(fwd)
        for a in range(n):
            for k, (px, py) in enumerate(peers):
                passed = outs[a].at[2 * px + py, half(a, 1 - c)]
                pltpu.make_async_remote_copy(
                    src_ref=passed, dst_ref=passed, send_sem=d2d_send.at[3 * a + k], recv_sem=d2d_recv.at[3 * a + k],
                    device_id=sibling, device_id_type=MESH).wait_recv()
        for cp in sent:
            cp.wait_send()

    hbm = pl.BlockSpec(memory_space=pltpu.HBM)
    gathered = pl.pallas_call(
        body, name=name, out_shape=out_shapes, in_specs=[hbm] * n, out_specs=[hbm] * n,
        scratch_shapes=[pltpu.SemaphoreType.DMA((3 * n,))] * 4,
    )(*arrs)
    chip = 2 * lax.axis_index("x") + lax.axis_index("y")
    return [lax.dynamic_update_slice(g, a[None], (chip, 0, 0)) for g, a in zip(gathered, arrs)]


def _sibling_exchange(arrs, name):
    n = len(arrs)

    def body(*refs):
        ins, outs = refs[:n], refs[n:2 * n]
        send_sems, recv_sems = refs[2 * n:]
        sibling = (lax.axis_index("x"), lax.axis_index("y"), 1 - lax.axis_index("c"))
        cps = []
        for a in range(n):
            cp = pltpu.make_async_remote_copy(src_ref=ins[a], dst_ref=outs[a], send_sem=send_sems.at[a],
                                              recv_sem=recv_sems.at[a], device_id=sibling, device_id_type=MESH)
            cp.start()
            cps.append(cp)
        for cp in cps:
            cp.wait_recv()
        for cp in cps:
            cp.wait_send()

    hbm = pl.BlockSpec(memory_space=pltpu.HBM)
    return pl.pallas_call(
        body, name=name, out_shape=[jax.ShapeDtypeStruct(a.shape, a.dtype) for a in arrs],
        in_specs=[hbm] * n, out_specs=[hbm] * n,
        scratch_shapes=[pltpu.SemaphoreType.DMA((n,)), pltpu.SemaphoreType.DMA((n,))],
    )(*arrs)


def _matmul(a, b, mode, out_dtype, name, tm=1024, tn=1024, tk=512, relu2=False, times=None):
    if mode == "nn":
        (m, k), (k2, n) = a.shape, b.shape
    elif mode == "nt":
        (m, k), (n, k2) = a.shape, b.shape
    else:
        (k, m), (k2, n) = a.shape, b.shape
    assert k == k2, (a.shape, b.shape, mode)
    tm, tn, tk = _tile(m, tm, LANES), _tile(n, tn, LANES), _tile(k, tk, LANES)
    nk = k // tk

    n_in = 2 if times is None else 3
    n_out = 2 if relu2 else 1

    def body(*refs):
        a_ref, b_ref = refs[:2]
        o_refs, acc_ref = refs[n_in:n_in + n_out], refs[n_in + n_out]
        kk = pl.program_id(2)

        @pl.when(kk == 0)
        def _():
            acc_ref[...] = jnp.zeros_like(acc_ref)

        if mode == "nn":
            acc_ref[...] += jnp.dot(a_ref[...], b_ref[...], preferred_element_type=F32)
        elif mode == "nt":
            acc_ref[...] += lax.dot_general(a_ref[...], b_ref[...], NT_DIMS, preferred_element_type=F32)
        else:
            acc_ref[...] += lax.dot_general(a_ref[...], b_ref[...], TN_DIMS, preferred_element_type=F32)

        @pl.when(kk == nk - 1)
        def _():
            p = acc_ref[...]
            if relu2:
                p = jnp.maximum(p, 0.0)
                o_refs[0][...] = p.astype(o_refs[0].dtype)
                o_refs[1][...] = (p * p).astype(o_refs[1].dtype)
            elif times is not None:
                o_refs[0][...] = (2.0 * refs[2][...].astype(F32) * p).astype(o_refs[0].dtype)
            else:
                o_refs[0][...] = p.astype(o_refs[0].dtype)

    if mode == "nn":
        a_spec = pl.BlockSpec((tm, tk), lambda i, j, kk: (i, kk))
        b_spec = pl.BlockSpec((tk, tn), lambda i, j, kk: (kk, j))
    elif mode == "nt":
        a_spec = pl.BlockSpec((tm, tk), lambda i, j, kk: (i, kk))
        b_spec = pl.BlockSpec((tn, tk), lambda i, j, kk: (j, kk))
    else:
        a_spec = pl.BlockSpec((tk, tm), lambda i, j, kk: (kk, i))
        b_spec = pl.BlockSpec((tk, tn), lambda i, j, kk: (kk, j))
    o_spec = pl.BlockSpec((tm, tn), lambda i, j, kk: (i, j))
    out = pl.pallas_call(
        body, name=name, grid=(m // tm, n // tn, nk), in_specs=[a_spec, b_spec] + [o_spec] * (n_in - 2),
        out_specs=[o_spec] * n_out, out_shape=[jax.ShapeDtypeStruct((m, n), out_dtype)] * n_out,
        scratch_shapes=[pltpu.VMEM((tm, tn), F32)],
        compiler_params=_params(("parallel", "parallel", "arbitrary")),
    )(*((a, b) if times is None else (a, b, times)))
    return out if relu2 else out[0]


def _mod_part(c_all, w_s, b_s, name):
    d, na = w_s.shape
    tn = _tile(na, 512, LANES)

    def body(c_ref, w_ref, b_ref, o_ref):
        ca = _silu(c_ref[...]).astype(BF16)
        o_ref[...] = jnp.dot(ca, w_ref[...].astype(BF16), preferred_element_type=F32) + b_ref[...]

    return pl.pallas_call(
        body, name=name, grid=(na // tn,),
        in_specs=[pl.BlockSpec((N_DEV, d), lambda j: (0, 0)), pl.BlockSpec((d, tn), lambda j: (0, j)),
                  pl.BlockSpec((1, tn), lambda j: (0, j))],
        out_specs=pl.BlockSpec((N_DEV, tn), lambda j: (0, j)),
        out_shape=jax.ShapeDtypeStruct((N_DEV, na), F32), compiler_params=_params(("parallel",)),
    )(c_all, w_s, b_s)


def _wada_grad(c_all, dmod_s, name):
    d = c_all.shape[1]
    na = dmod_s.shape[1]
    td, tn = _tile(d, 512, LANES), _tile(na, 512, LANES)

    def body(c_ref, g_ref, o_ref):
        o_ref[...] = lax.dot_general(_silu(c_ref[...]), g_ref[...], TN_DIMS, precision=HI, preferred_element_type=F32)

    return pl.pallas_call(
        body, name=name, grid=(d // td, na // tn),
        in_specs=[pl.BlockSpec((N_DEV, td), lambda i, j: (0, i)), pl.BlockSpec((N_DEV, tn), lambda i, j: (0, j))],
        out_specs=pl.BlockSpec((td, tn), lambda i, j: (i, j)),
        out_shape=jax.ShapeDtypeStruct((d, na), F32), compiler_params=_params(("parallel", "parallel")),
    )(c_all, dmod_s)


def _row_specs(tb, d, n_full, n_vec):
    full = pl.BlockSpec((tb, d), lambda i: (i, 0))
    vec = pl.BlockSpec((1, d), lambda i: (0, 0))
    return [full] * n_full + [vec] * n_vec


def _norm_mod(x, w, sc, sh, name):
    t, d = x.shape
    tb = _tile(t, 256, SUBLANES)

    def body(x_ref, w_ref, sc_ref, sh_ref, o_ref):
        xv = x_ref[...]
        o_ref[...] = (xv * _rms_scale(xv) * w_ref[...] * (1.0 + sc_ref[...]) + sh_ref[...]).astype(o_ref.dtype)

    return pl.pallas_call(
        body, name=name, grid=(t // tb,), in_specs=_row_specs(tb, d, 1, 3),
        out_specs=pl.BlockSpec((tb, d), lambda i: (i, 0)), out_shape=jax.ShapeDtypeStruct((t, d), BF16),
        compiler_params=_params(("parallel",)),
    )(x, w, sc, sh)


def _norm_mod_bwd(x, w, sc, dh, dres, name):
    t, d = x.shape
    tb = _tile(t, 256, SUBLANES)

    def body(x_ref, w_ref, sc_ref, dh_ref, dres_ref, dx_ref, dw_ref, dsc_ref, dsh_ref):
        @pl.when(pl.program_id(0) == 0)
        def _():
            dw_ref[...] = jnp.zeros_like(dw_ref)
            dsc_ref[...] = jnp.zeros_like(dsc_ref)
            dsh_ref[...] = jnp.zeros_like(dsh_ref)

        xv = x_ref[...]
        r = _rms_scale(xv)
        xn = xv * r
        g = dh_ref[...].astype(F32)
        wv, one_sc = w_ref[...], 1.0 + sc_ref[...]
        gxn = g * xn
        dsh_ref[...] += jnp.sum(g, axis=0, keepdims=True)
        dsc_ref[...] += jnp.sum(gxn, axis=0, keepdims=True) * wv
        dw_ref[...] += jnp.sum(gxn, axis=0, keepdims=True) * one_sc
        dxn = g * (wv * one_sc)
        dx_ref[...] = dres_ref[...] + r * (dxn - xn * jnp.mean(dxn * xn, axis=-1, keepdims=True))

    vec_out = pl.BlockSpec((1, d), lambda i: (0, 0))
    return pl.pallas_call(
        body, name=name, grid=(t // tb,),
        in_specs=[pl.BlockSpec((tb, d), lambda i: (i, 0)), pl.BlockSpec((1, d), lambda i: (0, 0)),
                  pl.BlockSpec((1, d), lambda i: (0, 0)), pl.BlockSpec((tb, d), lambda i: (i, 0)),
                  pl.BlockSpec((tb, d), lambda i: (i, 0))],
        out_specs=[pl.BlockSpec((tb, d), lambda i: (i, 0)), vec_out, vec_out, vec_out],
        out_shape=[jax.ShapeDtypeStruct((t, d), F32)] + [jax.ShapeDtypeStruct((1, d), F32)] * 3,
        compiler_params=_params(("arbitrary",)),
    )(x, w, sc, dh, dres)


def _resid(x, y, w, gt, name):
    t, d = x.shape
    tb = _tile(t, 256, SUBLANES)

    def body(x_ref, y_ref, w_ref, gt_ref, o_ref):
        yv = y_ref[...]
        o_ref[...] = x_ref[...] + gt_ref[...] * (yv * _rms_scale(yv) * w_ref[...])

    return pl.pallas_call(
        body, name=name, grid=(t // tb,), in_specs=_row_specs(tb, d, 2, 2),
        out_specs=pl.BlockSpec((tb, d), lambda i: (i, 0)), out_shape=jax.ShapeDtypeStruct((t, d), F32),
        compiler_params=_params(("parallel",)),
    )(x, y, w, gt)


def _loss_head(x2, y2, w, gt, target, name):
    t, d = x2.shape
    tb = _tile(t, 256, SUBLANES)

    def body(x_ref, y_ref, tg_ref, w_ref, gt_ref, do_ref, loss_ref):
        @pl.when(pl.program_id(0) == 0)
        def _():
            loss_ref[...] = jnp.zeros_like(loss_ref)

        yv = y_ref[...]
        out = x_ref[...] + gt_ref[...] * (yv * _rms_scale(yv) * w_ref[...])
        err = out - tg_ref[...]
        do_ref[...] = err * (1.0 / d)
        per_tok = jnp.mean(err * err, axis=-1, keepdims=True)
        loss_ref[...] += 0.5 * jnp.sum(per_tok, axis=0, keepdims=True)

    return pl.pallas_call(
        body, name=name, grid=(t // tb,), in_specs=_row_specs(tb, d, 3, 2),
        out_specs=[pl.BlockSpec((tb, d), lambda i: (i, 0)), pl.BlockSpec((1, LANES), lambda i: (0, 0))],
        out_shape=[jax.ShapeDtypeStruct((t, d), F32), jax.ShapeDtypeStruct((1, LANES), F32)],
        compiler_params=_params(("arbitrary",)),
    )(x2, y2, target, w, gt)


def _resid_bwd(dout, y, w, gt, name):
    t, d = y.shape
    tb = _tile(t, 256, SUBLANES)

    def body(do_ref, y_ref, w_ref, gt_ref, dy_ref, dgt_ref, dw_ref):
        @pl.when(pl.program_id(0) == 0)
        def _():
            dgt_ref[...] = jnp.zeros_like(dgt_ref)
            dw_ref[...] = jnp.zeros_like(dw_ref)

        yv, g = y_ref[...], do_ref[...]
        r = _rms_scale(yv)
        yn = yv * r
        wv, gtv = w_ref[...], gt_ref[...]
        gyn = jnp.sum(g * yn, axis=0, keepdims=True)
        dgt_ref[...] += gyn * wv
        dw_ref[...] += gyn * gtv
        dyn = g * (gtv * wv)
        dy_ref[...] = (r * (dyn - yn * jnp.mean(dyn * yn, axis=-1, keepdims=True))).astype(dy_ref.dtype)

    vec_out = pl.BlockSpec((1, d), lambda i: (0, 0))
    return pl.pallas_call(
        body, name=name, grid=(t // tb,), in_specs=_row_specs(tb, d, 2, 2),
        out_specs=[pl.BlockSpec((tb, d), lambda i: (i, 0)), vec_out, vec_out],
        out_shape=[jax.ShapeDtypeStruct((t, d), BF16)] + [jax.ShapeDtypeStruct((1, d), F32)] * 2,
        compiler_params=_params(("arbitrary",)),
    )(dout, y, w, gt)


def _add_halves(own, sib, name):
    _, r, c = own.shape
    tr, tc = _elementwise_tiles(r, c)

    def body(a_ref, b_ref, o_ref):
        o_ref[...] = (a_ref[...] + b_ref[...].astype(F32)).astype(o_ref.dtype)

    spec = pl.BlockSpec((1, tr, tc), lambda j, i, k: (j, i, k))
    return pl.pallas_call(
        body, name=name, grid=(N_CHIP, r // tr, c // tc), in_specs=[spec, spec], out_specs=spec,
        out_shape=jax.ShapeDtypeStruct(own.shape, BF16), compiler_params=_params(("parallel", "parallel", "parallel")),
    )(own, sib)


def _sum_chips(recv, name):
    _, r, c = recv.shape
    tr, tc = _elementwise_tiles(r, c)

    def body(x_ref, o_ref):
        acc = x_ref[0].astype(F32)
        for j in range(1, N_CHIP):
            acc = acc + x_ref[j].astype(F32)
        o_ref[...] = acc

    return pl.pallas_call(
        body, name=name, grid=(r // tr, c // tc), in_specs=[pl.BlockSpec((N_CHIP, tr, tc), lambda i, j: (0, i, j))],
        out_specs=pl.BlockSpec((tr, tc), lambda i, j: (i, j)), out_shape=jax.ShapeDtypeStruct((r, c), F32),
        compiler_params=_params(("parallel", "parallel")),
    )(recv)


def _adamw(w, g_parts, m, v, name, by_core=False):
    r, c = w.shape
    tr, tc = _elementwise_tiles(r // 2 if by_core else r, c)
    n_g = len(g_parts)
    nbh = (r // 2) // tr
    c1 = 1.0 / (1.0 - ADAM_B1 ** ADAM_STEP)
    c2 = 1.0 / (1.0 - ADAM_B2 ** ADAM_STEP)

    def body(*refs):
        w_ref, g_refs, m_ref, v_ref = refs[0], refs[1:1 + n_g], refs[1 + n_g], refs[2 + n_g]
        g_out, d_out, m_out, v_out = refs[3 + n_g:]
        if by_core:
            in_my_half = (pl.program_id(0) // nbh) == lax.axis_index("c")
            g = jnp.where(in_my_half, g_refs[0][...], g_refs[1][...])
        else:
            g = g_refs[0][...]
        mn = ADAM_B1 * m_ref[...] + (1.0 - ADAM_B1) * g
        vn = ADAM_B2 * v_ref[...] + (1.0 - ADAM_B2) * (g * g)
        g_out[...] = g
        m_out[...] = mn
        v_out[...] = vn
        d_out[...] = -ADAM_LR * ((mn * c1) / (jnp.sqrt(vn * c2) + ADAM_EPS) + ADAM_WD * w_ref[...])

    spec = pl.BlockSpec((tr, tc), lambda i, j: (i, j))
    g_spec = pl.BlockSpec((tr, tc), lambda i, j: (i % nbh, j)) if by_core else spec
    return pl.pallas_call(
        body, name=name, grid=(r // tr, c // tc), in_specs=[spec] + [g_spec] * n_g + [spec] * 2, out_specs=[spec] * 4,
        out_shape=[jax.ShapeDtypeStruct((r, c), F32)] * 4, compiler_params=_params(("parallel", "parallel")),
    )(w, *g_parts, m, v)


def _conv_taps(u, t):
    rows = lax.broadcasted_iota(jnp.int32, u.shape, 0)
    return [u] + [jnp.where(rows >= dd, pltpu.roll(u, dd, 0), 0.0) for dd in range(1, CONV_K)]


def _conv_fwd(proj, conv_w, col0, name):
    t = proj.shape[0]
    ch = conv_w.shape[1]

    def body(u_ref, w_ref, o_ref):
        taps = _conv_taps(u_ref[...], t)
        wv = w_ref[...]
        y = taps[0] * wv[CONV_K - 1:CONV_K]
        for dd in range(1, CONV_K):
            y = y + taps[dd] * wv[CONV_K - 1 - dd:CONV_K - dd]
        o_ref[...] = _silu(y)

    return pl.pallas_call(
        body, name=name, grid=(ch // LANES,),
        in_specs=[pl.BlockSpec((t, LANES), lambda j: (0, col0 + j)), pl.BlockSpec((CONV_K, LANES), lambda j: (0, j))],
        out_specs=pl.BlockSpec((t, LANES), lambda j: (0, j)), out_shape=jax.ShapeDtypeStruct((t, ch), F32),
        compiler_params=_params(("parallel",)),
    )(proj, conv_w)


def _conv_bwd(proj, conv_w, ds, col0, name):
    t = proj.shape[0]
    ch = conv_w.shape[1]

    def body(u_ref, w_ref, ds_ref, du_ref, dw_ref):
        u = u_ref[...]
        taps = _conv_taps(u, t)
        wv = w_ref[...]
        y = taps[0] * wv[CONV_K - 1:CONV_K]
        for dd in range(1, CONV_K):
            y = y + taps[dd] * wv[CONV_K - 1 - dd:CONV_K - dd]
        sg = jax.nn.sigmoid(y)
        dy = ds_ref[...] * (sg * (1.0 + y * (1.0 - sg)))
        rows = lax.broadcasted_iota(jnp.int32, u.shape, 0)
        du = dy * wv[CONV_K - 1:CONV_K]
        for dd in range(1, CONV_K):
            ahead = jnp.where(rows < t - dd, pltpu.roll(dy, t - dd, 0), 0.0)
            du = du + ahead * wv[CONV_K - 1 - dd:CONV_K - dd]
        du_ref[...] = du.astype(du_ref.dtype)
        dws = [jnp.sum(dy * taps[CONV_K - 1 - j], axis=0, keepdims=True) for j in range(CONV_K)]
        dw_ref[...] = jnp.concatenate(dws, axis=0)

    return pl.pallas_call(
        body, name=name, grid=(ch // LANES,),
        in_specs=[pl.BlockSpec((t, LANES), lambda j: (0, col0 + j)), pl.BlockSpec((CONV_K, LANES), lambda j: (0, j)),
                  pl.BlockSpec((t, LANES), lambda j: (0, j))],
        out_specs=[pl.BlockSpec((t, LANES), lambda j: (0, j)), pl.BlockSpec((CONV_K, LANES), lambda j: (0, j))],
        out_shape=[jax.ShapeDtypeStruct((t, ch), BF16), jax.ShapeDtypeStruct((CONV_K, ch), F32)],
        compiler_params=_params(("parallel",)),
    )(proj, conv_w, ds)


def _hg_block(st, q, fl, vi, g, l0, l1, nw):
    hs = range(len(st))
    tb = q[0].shape[0]
    ln = HG_SUB
    lb = [jax.nn.sigmoid(l0[h] - l1[h]) for h in hs]
    rows = lax.broadcasted_iota(jnp.int32, (ln, HEAD), 0)
    tri = (lax.broadcasted_iota(jnp.int32, (ln, ln), 0) >= lax.broadcasted_iota(jnp.int32, (ln, ln), 1)).astype(F32)
    st = list(st)
    outs = [[] for _ in hs]
    for i in range(tb // ln):
        sl = slice(i * ln, (i + 1) * ln)
        qs, vs = [q[h][sl] for h in hs], [vi[h][sl] for h in hs]
        f = [lb[h] + (1.0 - lb[h]) * jax.nn.sigmoid(fl[h][sl]) for h in hs]
        k = [1.0 - f[h] for h in hs]
        b = [jnp.dot(tri, jnp.log(f[h]), precision=HI, preferred_element_type=F32) for h in hs]
        o = [lax.dot_general((qs[h] * jnp.exp(b[h])).astype(BF16), st[h].astype(BF16), NT_DIMS, preferred_element_type=F32)
             for h in hs]
        for s in range(ln):
            e = [jnp.exp(jnp.where(rows >= s, b[h] - b[h][s:s + 1], -1e30)) for h in hs]
            a = [jnp.sum(qs[h] * e[h] * k[h][s:s + 1], axis=-1, keepdims=True) for h in hs]
            o = [o[h] + a[h] * vs[h][s:s + 1] for h in hs]
        kt = [k[h] * jnp.exp(b[h][ln - 1:ln] - b[h]) for h in hs]
        upd = [lax.dot_general(vs[h].astype(BF16), kt[h].astype(BF16), TN_DIMS, preferred_element_type=F32) for h in hs]
        st = [st[h] * jnp.exp(b[h][ln - 1:ln]) + upd[h] for h in hs]
        for h in hs:
            outs[h].append(o[h])
    o = [jnp.concatenate(outs[h], axis=0) for h in hs]
    out = [o[h] * _rms_scale(o[h]) * nw * _silu(g[h]) for h in hs]
    return st, out


def _head_cols(h):
    return slice(h * HEAD, (h + 1) * HEAD)


def _head_groups(n_heads):
    g = min(HEAD_GROUP, n_heads)
    return [list(range(i, min(i + g, n_heads))) for i in range(0, n_heads, g)]


def _hg_in_specs(n_heads, tb, time_index):
    hw = n_heads * HEAD
    cols = [pl.BlockSpec((tb, hw), functools.partial(lambda part, j: (time_index(j), part), part)) for part in range(4)]
    head_rows = pl.BlockSpec((n_heads, 1, HEAD), lambda j: (0, 0, 0))
    return cols + [head_rows, head_rows, pl.BlockSpec((1, HEAD), lambda j: (0, 0))]


def _hgrn2_fwd(proj, l0, l1, nw, n_heads, name):
    t = proj.shape[0]
    hw = n_heads * HEAD
    tb = _tile(t, HG_BLOCK, HG_SUB)
    nb = t // tb

    def body(q_ref, f_ref, i_ref, g_ref, l0_ref, l1_ref, nw_ref, o_ref, save_ref, st_ref):
        @pl.when(pl.program_id(0) == 0)
        def _():
            st_ref[...] = jnp.zeros_like(st_ref)

        for hs in _head_groups(n_heads):
            st = [st_ref[h] for h in hs]
            for h, s in zip(hs, st):
                save_ref[h] = s
            st, out = _hg_block(st, *[[r[:, _head_cols(h)] for h in hs] for r in (q_ref, f_ref, i_ref, g_ref)],
                                [l0_ref[h] for h in hs], [l1_ref[h] for h in hs], nw_ref[...])
            for h, s, o in zip(hs, st, out):
                st_ref[h] = s
                o_ref[:, _head_cols(h)] = o.astype(o_ref.dtype)

    return pl.pallas_call(
        body, name=name, grid=(nb,), in_specs=_hg_in_specs(n_heads, tb, lambda j: j),
        out_specs=[pl.BlockSpec((tb, hw), lambda j: (j, 0)),
                   pl.BlockSpec((None, n_heads, HEAD, HEAD), lambda j: (j, 0, 0, 0))],
        out_shape=[jax.ShapeDtypeStruct((t, hw), BF16), jax.ShapeDtypeStruct((nb, n_heads, HEAD, HEAD), F32)],
        scratch_shapes=[pltpu.VMEM((n_heads, HEAD, HEAD), F32)], compiler_params=_params(("arbitrary",)),
    )(proj, proj, proj, proj, l0, l1, nw)


def _hgrn2_bwd(proj, l0, l1, nw, saved, d_ocat, n_heads, name):
    t = proj.shape[0]
    tb = _tile(t, HG_BLOCK, HG_SUB)
    nb = t // tb
    rev = lambda j: nb - 1 - j

    hw = n_heads * HEAD

    def body(q_ref, f_ref, i_ref, g_ref, l0_ref, l1_ref, nw_ref, save_ref, do_ref,
             dp_ref, dl0_ref, dl1_ref, dnw_ref, dst_ref):
        @pl.when(pl.program_id(0) == 0)
        def _():
            dst_ref[...] = jnp.zeros_like(dst_ref)
            dl0_ref[...] = jnp.zeros_like(dl0_ref)
            dl1_ref[...] = jnp.zeros_like(dl1_ref)
            dnw_ref[...] = jnp.zeros_like(dnw_ref)

        dnw_acc = jnp.zeros((1, HEAD), F32)
        for hs in _head_groups(n_heads):
            _, vjp = jax.vjp(_hg_block, [save_ref[h] for h in hs],
                             *[[r[:, _head_cols(h)] for h in hs] for r in (q_ref, f_ref, i_ref, g_ref)],
                             [l0_ref[h] for h in hs], [l1_ref[h] for h in hs], nw_ref[...])
            dst, dq, df, di, dg, dl0, dl1, dnw = vjp(([dst_ref[h] for h in hs], [do_ref[:, _head_cols(h)] for h in hs]))
            for i, h in enumerate(hs):
                dst_ref[h] = dst[i]
                for part, val in enumerate((dq, df, di, dg)):
                    dp_ref[:, part * hw + h * HEAD:part * hw + (h + 1) * HEAD] = val[i].astype(dp_ref.dtype)
                dl0_ref[h] += dl0[i]
                dl1_ref[h] += dl1[i]
            dnw_acc = dnw_acc + dnw
        dnw_ref[...] += dnw_acc

    head_rows = pl.BlockSpec((n_heads, 1, HEAD), lambda j: (0, 0, 0))
    return pl.pallas_call(
        body, name=name, grid=(nb,),
        in_specs=_hg_in_specs(n_heads, tb, rev) + [pl.BlockSpec((None, n_heads, HEAD, HEAD), lambda j: (rev(j), 0, 0, 0)),
                                                   pl.BlockSpec((tb, hw), lambda j: (rev(j), 0))],
        out_specs=[pl.BlockSpec((tb, 4 * hw), lambda j: (rev(j), 0)), head_rows, head_rows,
                   pl.BlockSpec((1, HEAD), lambda j: (0, 0))],
        out_shape=[jax.ShapeDtypeStruct((t, 4 * hw), BF16)] + [jax.ShapeDtypeStruct((n_heads, 1, HEAD), F32)] * 2
        + [jax.ShapeDtypeStruct((1, HEAD), F32)],
        scratch_shapes=[pltpu.VMEM((n_heads, HEAD, HEAD), F32)], compiler_params=_params(("arbitrary",)),
    )(proj, proj, proj, proj, l0, l1, nw, saved, d_ocat)


def _dot_hi(a, b, dims=(((1,), (0,)), ((), ()))):
    return lax.dot_general(a, b, dims, precision=HI, preferred_element_type=F32)


def _dot_bf16(a, b, dims=(((1,), (0,)), ((), ()))):
    return lax.dot_general(a.astype(BF16), b.astype(BF16), dims, preferred_element_type=F32)


def _inv_unit_lower_raw(ms):
    hs = range(len(ms))
    c = ms[0].shape[0]
    r = lax.broadcasted_iota(jnp.int32, (c, c), 0)
    q = lax.broadcasted_iota(jnp.int32, (c, c), 1)
    eye = (r == q).astype(F32)
    md = [jnp.where((r // GDN_INV_BLOCK) == (q // GDN_INV_BLOCK), ms[h], 0.0) for h in hs]
    p = [-md[h] for h in hs]
    t16 = [eye + p[h] for h in hs]
    for _ in range(int(math.log2(GDN_INV_BLOCK)) - 1):
        p = [_dot_hi(p[h], p[h]) for h in hs]
        t16 = [t16[h] + _dot_hi(t16[h], p[h]) for h in hs]
    p = [-_dot_hi(t16[h], ms[h] - md[h]) for h in hs]
    t2 = [eye + p[h] for h in hs]
    for _ in range(int(math.log2(c // GDN_INV_BLOCK)) - 1):
        p = [_dot_hi(p[h], p[h]) for h in hs]
        t2 = [t2[h] + _dot_hi(t2[h], p[h]) for h in hs]
    return [_dot_hi(t2[h], t16[h]) for h in hs]


@jax.custom_vjp
def _inv_unit_lower(ms):
    return _inv_unit_lower_raw(ms)


def _inv_fwd(ms):
    ts = _inv_unit_lower_raw(ms)
    return ts, ts


def _inv_bwd(ts, dts):
    hs = range(len(ts))
    inner = [_dot_hi(ts[h], dts[h], TN_DIMS) for h in hs]
    return ([-_dot_hi(inner[h], ts[h], NT_DIMS) for h in hs],)


_inv_unit_lower.defvjp(_inv_fwd, _inv_bwd)


def _gdn_block(inverse, onehots, st, qc, kc, vc, g, ab, alog_row, dtb_row, nw):
    hs = range(len(st))
    c = qc[0].shape[0]
    lane_sum = lambda v: jnp.sum(v, axis=-1, keepdims=True)
    a = [lane_sum(ab * onehots[h][0]) for h in hs]
    bb = [lane_sum(ab * onehots[h][1]) for h in hs]
    alog = [lane_sum(alog_row * onehots[h][0]) for h in hs]
    dtb = [lane_sum(dtb_row * onehots[h][0]) for h in hs]
    la = [-jnp.exp(alog[h]) * _softplus(a[h] + dtb[h]) for h in hs]
    beta = [jax.nn.sigmoid(bb[h]) for h in hs]
    q = [qc[h] * lax.rsqrt(lane_sum(qc[h] * qc[h]) + EPS) * (HEAD ** -0.5) for h in hs]
    k = [kc[h] * lax.rsqrt(lane_sum(kc[h] * kc[h]) + EPS) for h in hs]
    r = lax.broadcasted_iota(jnp.int32, (c, c), 0)
    s = lax.broadcasted_iota(jnp.int32, (c, c), 1)
    tri = (r >= s).astype(F32)
    g_cc = [_dot_hi(tri, jnp.broadcast_to(la[h], (c, c))) for h in hs]
    g_cl = [_dot_hi(tri, jnp.broadcast_to(la[h], (c, HEAD))) for h in hs]
    gamma = [jnp.exp(jnp.where(r >= s, g_cc[h] - g_cc[h].T, -1e30)) for h in hs]
    kk = [_dot_bf16(k[h], k[h], NT_DIMS) for h in hs]
    m = [jnp.where(r > s, beta[h] * kk[h] * gamma[h], 0.0) for h in hs]
    tm = inverse(m)
    eg = [jnp.exp(g_cl[h]) for h in hs]
    rhs = [jnp.concatenate([vc[h] * beta[h], k[h] * (beta[h] * eg[h])], axis=1) for h in hs]
    sol = [_dot_hi(tm[h], rhs[h]) for h in hs]
    qk = [_dot_bf16(q[h], k[h], NT_DIMS) * gamma[h] for h in hs]
    g_last = [g_cl[h][c - 1:c] for h in hs]
    k_tail = [k[h] * jnp.exp(g_last[h] - g_cl[h]) for h in hs]
    v_new = [sol[h][:, :HEAD] - _dot_bf16(sol[h][:, HEAD:], st[h], NT_DIMS) for h in hs]
    o_st = [_dot_bf16(q[h] * eg[h], st[h], NT_DIMS) for h in hs]
    o = [o_st[h] + _dot_bf16(qk[h], v_new[h]) for h in hs]
    upd = [_dot_bf16(v_new[h], k_tail[h], TN_DIMS) for h in hs]
    st = [st[h] * jnp.exp(g_last[h]) + upd[h] for h in hs]
    out = [o[h] * _rms_scale(o[h]) * nw * _silu(g[h]) for h in hs]
    return st, out


def _head_onehots(n_heads, h):
    lane = lax.broadcasted_iota(jnp.int32, (1, LANES), 1)
    return (lane == h).astype(F32), (lane == n_heads + h).astype(F32)


def _gdn_in_specs(n_heads, c, time_index):
    hw = n_heads * HEAD
    qkv = [pl.BlockSpec((c, hw), functools.partial(lambda part, j: (time_index(j), part), part)) for part in range(3)]
    row = pl.BlockSpec((1, LANES), lambda j: (0, 0))
    return qkv + [pl.BlockSpec((c, hw), lambda j: (time_index(j), 7)),
                  pl.BlockSpec((c, LANES), lambda j: (time_index(j), 8 * n_heads)), row, row, row]


def _gdn_fwd(qkv, proj, alog_row, dtb_row, nw, n_heads, name):
    t = qkv.shape[0]
    hw = n_heads * HEAD
    c = _tile(t, GDN_CHUNK, GDN_CHUNK)
    nb = t // c

    def body(q_ref, k_ref, v_ref, g_ref, ab_ref, al_ref, dt_ref, nw_ref, o_ref, save_ref, st_ref):
        @pl.when(pl.program_id(0) == 0)
        def _():
            st_ref[...] = jnp.zeros_like(st_ref)

        for hs in _head_groups(n_heads):
            st = [st_ref[h] for h in hs]
            for h, s in zip(hs, st):
                save_ref[h] = s
            st, out = _gdn_block(_inv_unit_lower_raw, [_head_onehots(n_heads, h) for h in hs], st,
                                 *[[r[:, _head_cols(h)] for h in hs] for r in (q_ref, k_ref, v_ref, g_ref)],
                                 ab_ref[...], al_ref[...], dt_ref[...], nw_ref[...])
            for h, s, o in zip(hs, st, out):
                st_ref[h] = s
                o_ref[:, _head_cols(h)] = o.astype(o_ref.dtype)

    return pl.pallas_call(
        body, name=name, grid=(nb,), in_specs=_gdn_in_specs(n_heads, c, lambda j: j),
        out_specs=[pl.BlockSpec((c, hw), lambda j: (j, 0)),
                   pl.BlockSpec((None, n_heads, HEAD, HEAD), lambda j: (j, 0, 0, 0))],
        out_shape=[jax.ShapeDtypeStruct((t, hw), BF16), jax.ShapeDtypeStruct((nb, n_heads, HEAD, HEAD), F32)],
        scratch_shapes=[pltpu.VMEM((n_heads, HEAD, HEAD), F32)], compiler_params=_params(("arbitrary",)),
    )(qkv, qkv, qkv, proj, proj, alog_row, dtb_row, nw)


def _gdn_bwd(qkv, proj, alog_row, dtb_row, nw, saved, d_ocat, n_heads, name):
    t = qkv.shape[0]
    c = _tile(t, GDN_CHUNK, GDN_CHUNK)
    nb = t // c
    rev = lambda j: nb - 1 - j

    hw = n_heads * HEAD

    def body(q_ref, k_ref, v_ref, g_ref, ab_ref, al_ref, dt_ref, nw_ref, save_ref, do_ref,
             dqkv_ref, dg_ref, dab_ref, dal_ref, ddt_ref, dnw_ref, dst_ref):
        @pl.when(pl.program_id(0) == 0)
        def _():
            dst_ref[...] = jnp.zeros_like(dst_ref)
            dal_ref[...] = jnp.zeros_like(dal_ref)
            ddt_ref[...] = jnp.zeros_like(ddt_ref)
            dnw_ref[...] = jnp.zeros_like(dnw_ref)

        dab_acc = jnp.zeros((c, LANES), F32)
        row_acc = [jnp.zeros((1, LANES), F32)] * 3
        for hs in _head_groups(n_heads):
            fn = functools.partial(_gdn_block, _inv_unit_lower, [_head_onehots(n_heads, h) for h in hs])
            _, vjp = jax.vjp(fn, [save_ref[h] for h in hs],
                             *[[r[:, _head_cols(h)] for h in hs] for r in (q_ref, k_ref, v_ref, g_ref)],
                             ab_ref[...], al_ref[...], dt_ref[...], nw_ref[...])
            dst, dq, dk, dv, dg, dab, dal, ddt, dnw = vjp(([dst_ref[h] for h in hs], [do_ref[:, _head_cols(h)] for h in hs]))
            for i, h in enumerate(hs):
                dst_ref[h] = dst[i]
                for part, val in enumerate((dq, dk, dv)):
                    dqkv_ref[:, part * hw + h * HEAD:part * hw + (h + 1) * HEAD] = val[i]
                dg_ref[:, _head_cols(h)] = dg[i].astype(dg_ref.dtype)
            dab_acc = dab_acc + dab
            row_acc = [acc + val for acc, val in zip(row_acc, (dal, ddt, dnw))]
        dab_ref[...] = dab_acc
        dal_ref[...] += row_acc[0]
        ddt_ref[...] += row_acc[1]
        dnw_ref[...] += row_acc[2]

    row = pl.BlockSpec((1, LANES), lambda j: (0, 0))
    return pl.pallas_call(
        body, name=name, grid=(nb,),
        in_specs=_gdn_in_specs(n_heads, c, rev) + [pl.BlockSpec((None, n_heads, HEAD, HEAD), lambda j: (rev(j), 0, 0, 0)),
                                                   pl.BlockSpec((c, hw), lambda j: (rev(j), 1))],
        out_specs=[pl.BlockSpec((c, 3 * hw), lambda j: (rev(j), 0)), pl.BlockSpec((c, hw), lambda j: (rev(j), 0)),
                   pl.BlockSpec((c, LANES), lambda j: (rev(j), 0)), row, row, row],
        out_shape=[jax.ShapeDtypeStruct((t, 3 * hw), F32), jax.ShapeDtypeStruct((t, hw), BF16),
                   jax.ShapeDtypeStruct((t, LANES), F32)] + [jax.ShapeDtypeStruct((1, LANES), F32)] * 3,
        scratch_shapes=[pltpu.VMEM((n_heads, HEAD, HEAD), F32)], compiler_params=_params(("arbitrary",)),
    )(qkv, qkv, qkv, proj, proj, alog_row, dtb_row, nw, saved, d_ocat)


def _pad_lanes(v, n):
    v = v.reshape(1, -1)
    return jnp.pad(v, ((0, 0), (0, n - v.shape[1])))


def _pack_rows(vecs):
    flat = jnp.concatenate([v.reshape(-1) for v in vecs])
    offs, o = [], 0
    for v in vecs:
        offs.append((o, v.size))
        o += v.size
    per_row = -(-o // (SUBLANES * LANES)) * LANES
    flat = jnp.pad(flat, (0, SUBLANES * per_row - o))
    return flat.reshape(SUBLANES, per_row), offs


def _unpack(gathered, offs):
    per_dev = gathered.reshape(N_DEV, -1)
    return [per_dev[:, o:o + n] for o, n in offs]


def _sum_devices(part):
    acc = part[0]
    for i in range(1, N_DEV):
        acc = acc + part[i]
    return acc


def kernel(x, c, w_ada, b_ada, pre_mix_norm, post_mix_norm, pre_ffn_norm, post_ffn_norm, w_in, hg_lb_logits, hg_norm, gdn_conv_w, gdn_a_log, gdn_dt_bias, gdn_norm, w_out, w_ff1, w_ff2, loss_target, m_w_ada, m_b_ada, m_pre_mix_norm, m_post_mix_norm, m_pre_ffn_norm, m_post_ffn_norm, m_w_in, m_hg_lb_logits, m_hg_norm, m_gdn_conv_w, m_gdn_a_log, m_gdn_dt_bias, m_gdn_norm, m_w_out, m_w_ff1, m_w_ff2, v_w_ada, v_b_ada, v_pre_mix_norm, v_post_mix_norm, v_pre_ffn_norm, v_post_ffn_norm, v_w_in, v_hg_lb_logits, v_hg_norm, v_gdn_conv_w, v_gdn_a_log, v_gdn_dt_bias, v_gdn_norm, v_w_out, v_w_ff1, v_w_ff2):
    assert x.shape[0] == 1 and w_ada.shape[0] == 1 and hg_lb_logits.shape[0] == 2
    t, d = x.shape[1], x.shape[2]
    n_heads = (d // 2) // HEAD
    hw = n_heads * HEAD
    in_cols = 8 * hw + 2 * n_heads
    np_cols = 8 * hw + LANES
    d_ff = w_ff1.shape[2] * N_CHIP
    na = w_ada.shape[2]
    ax, ay, ac = lax.axis_index("x"), lax.axis_index("y"), lax.axis_index("c")
    chip = 2 * ax + ay
    dev = 4 * ax + 2 * ay + ac

    x2d, tgt = x[0], loss_target[0]

    pack1, offs1 = _pack_rows([c[0], gdn_conv_w[0]])
    c_all, convw_all = _unpack(_gather8(pack1, "gather_cond"), offs1)
    conv_sh = gdn_conv_w.shape[2]
    conv_w = jnp.concatenate([convw_all[2 * j].reshape(CONV_K, conv_sh) for j in range(N_CHIP)], axis=1)

    b_s = lax.dynamic_slice(b_ada, (0, chip * na), (1, na))
    mod_part = _mod_part(c_all, w_ada[0], b_s, "mod_part")
    pack2, offs2 = _pack_rows([mod_part])
    (mod_parts,) = _unpack(_gather8(pack2, "gather_mod"), offs2)
    mod_all = jnp.concatenate([mod_parts[2 * j].reshape(N_DEV, na) for j in range(N_CHIP)], axis=1)
    mod = lax.dynamic_slice(mod_all, (dev, 0), (1, N_MOD * d))
    sh_m, sc_m, gt_m, sh_f, sc_f, gt_f = [mod[:, i * d:(i + 1) * d] for i in range(N_MOD)]

    g_in, g_out, g_ff1, g_ff2 = _gather_weights(
        [w_in[0].astype(BF16), w_out[0].astype(BF16), w_ff1[0].astype(BF16), w_ff2[0].astype(BF16)], "gather_weights")
    w_in_f = jnp.pad(jnp.transpose(g_in, (1, 0, 2)).reshape(d, in_cols), ((0, 0), (0, np_cols - in_cols)))
    w_out_f = g_out.reshape(d, d)
    w_ff1_f = jnp.transpose(g_ff1, (1, 0, 2)).reshape(d, d_ff)
    w_ff2_f = g_ff2.reshape(d_ff, d)

    h1 = _norm_mod(x2d, pre_mix_norm, sc_m, sh_m, "norm_mod_mix")
    proj = _matmul(h1, w_in_f, "nn", F32, "mm_in", tn=640)
    l0, l1 = hg_lb_logits[0].reshape(n_heads, 1, HEAD), hg_lb_logits[1].reshape(n_heads, 1, HEAD)
    o_hg, hg_saved = _hgrn2_fwd(proj, l0, l1, hg_norm, n_heads, "hgrn2_fwd")
    qkv = _conv_fwd(proj, conv_w, 4 * n_heads, "conv_fwd")
    alog_row, dtb_row = _pad_lanes(gdn_a_log, LANES), _pad_lanes(gdn_dt_bias, LANES)
    o_gdn, gdn_saved = _gdn_fwd(qkv, proj, alog_row, dtb_row, gdn_norm, n_heads, "gdn_fwd")
    o_cat = jnp.concatenate([o_hg, o_gdn], axis=1)
    y1 = _matmul(o_cat, w_out_f, "nn", F32, "mm_out")
    x_mid = _resid(x2d, y1, post_mix_norm, gt_m, "resid_mix")

    h2 = _norm_mod(x_mid, pre_ffn_norm, sc_f, sh_f, "norm_mod_ffn")
    relu_a1, r1 = _matmul(h2, w_ff1_f, "nn", BF16, "mm_ff1", relu2=True)
    y2 = _matmul(r1, w_ff2_f, "nn", F32, "mm_ff2")
    d_out, loss_row = _loss_head(x_mid, y2, post_ffn_norm, gt_f, tgt, "loss_head")

    dy2, d_gt_f, d_post_ffn = _resid_bwd(d_out, y2, post_ffn_norm, gt_f, "resid_ffn_bwd")
    gw_ff2 = _matmul(r1, dy2, "tn", F32, "mm_ff2_dw")
    da1 = _matmul(dy2, w_ff2_f, "nt", BF16, "mm_ff2_dx", times=relu_a1)
    gw_ff1 = _matmul(h2, da1, "tn", F32, "mm_ff1_dw")
    dh2 = _matmul(da1, w_ff1_f, "nt", BF16, "mm_ff1_dx")
    d_mid, d_pre_ffn, d_sc_f, d_sh_f = _norm_mod_bwd(x_mid, pre_ffn_norm, sc_f, dh2, d_out, "norm_mod_ffn_bwd")

    dy1, d_gt_m, d_post_mix = _resid_bwd(d_mid, y1, post_mix_norm, gt_m, "resid_mix_bwd")
    gw_out = _matmul(o_cat, dy1, "tn", F32, "mm_out_dw")
    d_ocat = _matmul(dy1, w_out_f, "nt", F32, "mm_out_dx")
    dp_hg, dl0, dl1, d_hg_norm = _hgrn2_bwd(proj, l0, l1, hg_norm, hg_saved, d_ocat, n_heads, "hgrn2_bwd")
    dqkv, dg_g, dab, d_alog, d_dtb, d_gdn_norm = _gdn_bwd(
        qkv, proj, alog_row, dtb_row, gdn_norm, gdn_saved, d_ocat, n_heads, "gdn_bwd")
    du, d_conv_w = _conv_bwd(proj, conv_w, dqkv, 4 * n_heads, "conv_bwd")
    dproj = jnp.concatenate([dp_hg, du, dg_g, dab.astype(BF16)], axis=1)
    gw_in = _matmul(h1, dproj, "tn", F32, "mm_in_dw", tn=640)
    dh1 = _matmul(dproj, w_in_f, "nt", BF16, "mm_in_dx", tk=640)
    grad_x, d_pre_mix, d_sc_m, d_sh_m = _norm_mod_bwd(x2d, pre_mix_norm, sc_m, dh1, d_mid, "norm_mod_mix_bwd")

    d_mod = jnp.concatenate([d_sh_m, d_sc_m, d_gt_m, d_sh_f, d_sc_f, d_gt_f], axis=1)
    d_lb_logits = jnp.stack([dl0.reshape(n_heads, HEAD), dl1.reshape(n_heads, HEAD)])
    pack3, offs3 = _pack_rows([loss_row[0, :1], d_pre_mix, d_post_mix, d_pre_ffn, d_post_ffn, d_lb_logits, d_hg_norm,
                               d_conv_w, d_alog[0, :n_heads], d_dtb[0, :n_heads], d_gdn_norm, d_mod])
    parts = _unpack(_gather8(pack3, "gather_vec_grads"), offs3)
    sums = [_sum_devices(p) for p in parts[:-1]]
    loss = sums[0][0]
    dmod_all = parts[-1]
    g_b_ada = _sum_devices(dmod_all).reshape(1, N_MOD * d)
    g_conv_full = sums[7].reshape(CONV_K, N_CHIP * conv_sh)
    g_conv = lax.dynamic_slice(g_conv_full, (0, chip * conv_sh), (CONV_K, conv_sh))
    gw_ada = _wada_grad(c_all, lax.dynamic_slice(dmod_all, (0, chip * na), (N_DEV, na)), "wada_grad")

    in_sh = in_cols // N_CHIP
    ff_sh = d_ff // N_CHIP
    by_chip = [jnp.transpose(gw_in[:, :in_cols].reshape(d, N_CHIP, in_sh), (1, 0, 2)),
               gw_out.reshape(N_CHIP, d // N_CHIP, d),
               jnp.transpose(gw_ff1.reshape(d, N_CHIP, ff_sh), (1, 0, 2)),
               gw_ff2.reshape(N_CHIP, ff_sh, d)]

    def row_half(a, cc):
        rh = a.shape[1] // 2
        return lax.dynamic_slice(a, (0, cc * rh, 0), (N_CHIP, rh, a.shape[2]))

    from_sib = _sibling_exchange([row_half(a, 1 - ac).astype(BF16) for a in by_chip], "sibling_partials")
    chip_part = [_add_halves(row_half(a, ac), s, f"add_halves_{i}") for i, (a, s) in enumerate(zip(by_chip, from_sib))]
    recv = _chip_exchange(chip_part, False, "scatter_grads")
    mine = [_sum_chips(rv, f"sum_chips_{i}") for i, rv in enumerate(recv)]
    theirs = _sibling_exchange(mine, "sibling_grads")

    big = {}
    for i, (nm, w_, m_, v_) in enumerate([("w_in", w_in, m_w_in, v_w_in), ("w_out", w_out, m_w_out, v_w_out),
                                          ("w_ff1", w_ff1, m_w_ff1, v_w_ff1), ("w_ff2", w_ff2, m_w_ff2, v_w_ff2)]):
        big[nm] = [o[None] for o in _adamw(w_[0], [mine[i], theirs[i]], m_[0], v_[0], f"adamw_{nm}", by_core=True)]
    big["w_ada"] = [o[None] for o in _adamw(w_ada[0], [gw_ada], m_w_ada[0], v_w_ada[0], "adamw_w_ada")]

    small_names = ["b_ada", "pre_mix_norm", "post_mix_norm", "pre_ffn_norm", "post_ffn_norm", "hg_lb_logits", "hg_norm",
                   "gdn_conv_w", "gdn_a_log", "gdn_dt_bias", "gdn_norm"]
    small_w = [b_ada, pre_mix_norm, post_mix_norm, pre_ffn_norm, post_ffn_norm, hg_lb_logits, hg_norm, gdn_conv_w,
               gdn_a_log, gdn_dt_bias, gdn_norm]
    small_m = [m_b_ada, m_pre_mix_norm, m_post_mix_norm, m_pre_ffn_norm, m_post_ffn_norm, m_hg_lb_logits, m_hg_norm,
               m_gdn_conv_w, m_gdn_a_log, m_gdn_dt_bias, m_gdn_norm]
    small_v = [v_b_ada, v_pre_mix_norm, v_post_mix_norm, v_pre_ffn_norm, v_post_ffn_norm, v_hg_lb_logits, v_hg_norm,
               v_gdn_conv_w, v_gdn_a_log, v_gdn_dt_bias, v_gdn_norm]
    small_g = [g_b_ada, sums[1], sums[2], sums[3], sums[4], sums[5], sums[6], g_conv, sums[8], sums[9], sums[10]]
    pw, offs_s = _pack_rows(small_w)
    pg, _ = _pack_rows(small_g)
    pm, _ = _pack_rows(small_m)
    pv, _ = _pack_rows(small_v)
    packed = _adamw(pw, [pg], pm, pv, "adamw_vectors")
    small = {}
    for nm, w_, (o, n) in zip(small_names, small_w, offs_s):
        small[nm] = [p.reshape(-1)[o:o + n].reshape(w_.shape) for p in packed]

    order = ["w_ada", "b_ada", "pre_mix_norm", "post_mix_norm", "pre_ffn_norm", "post_ffn_norm", "w_in", "hg_lb_logits",
             "hg_norm", "gdn_conv_w", "gdn_a_log", "gdn_dt_bias", "gdn_norm", "w_out", "w_ff1", "w_ff2"]
    res = {**big, **small}
    outs = [loss, grad_x[None]]
    for k in range(4):
        outs += [res[nm][k] for nm in order]
    return tuple(outs)
```

```python
import functools
import math

import jax
import jax.numpy as jnp
from jax import lax
from jax.experimental import pallas as pl
from jax.experimental.pallas import tpu as pltpu

F32 = jnp.float32
BF16 = jnp.bfloat16
HI = lax.Precision.HIGHEST
MESH = pl.DeviceIdType.MESH

LANES = 128
SUBLANES = 8
VMEM_LIMIT = 48 * 1024 * 1024
EPS = 1e-6
HEAD = 128
CONV_K = 4
GDN_CHUNK = 64
GDN_INV_BLOCK = 16
HG_SUB = 16
HG_BLOCK = 128
HEAD_GROUP = 8
N_MOD = 6
N_DEV = 8
N_CHIP = 4

ADAM_LR = 0.001
ADAM_B1 = 0.9
ADAM_B2 = 0.999
ADAM_EPS = 1e-08
ADAM_WD = 0.01
ADAM_STEP = 10

NT_DIMS = (((1,), (1,)), ((), ()))
TN_DIMS = (((0,), (0,)), ((), ()))


def _tile(dim, target, align):
    if dim <= target:
        return dim
    best = dim
    t = align
    while t <= target:
        if dim % t == 0:
            best = t
        t += align
    return best


def _elementwise_tiles(r, c):
    tc = _tile(c, 1024, LANES)
    tr = _tile(r, max(16, (256 * 1024) // tc // 16 * 16), 16)
    return tr, tc


def _params(sem):
    return pltpu.CompilerParams(dimension_semantics=sem, vmem_limit_bytes=VMEM_LIMIT)


def _silu(x):
    return x * jax.nn.sigmoid(x)


def _softplus(x):
    pos = x > 0
    return jnp.where(pos, x, 0.0) + jnp.log(1.0 + jnp.exp(jnp.where(pos, -x, x)))


def _rms_scale(x):
    return lax.rsqrt(jnp.mean(x * x, axis=-1, keepdims=True) + EPS)


def _gather8(x_shard, name):
    m_per, n = x_shard.shape
    assert m_per % SUBLANES == 0 and n % LANES == 0

    def body(x_ref, out_ref, send_sems, recv_sems, local_sem):
        x, y, c = lax.axis_index("x"), lax.axis_index("y"), lax.axis_index("c")
        me, sibling = (x, y, c), (x, y, 1 - c)
        chips = [(1 - x, y), (x, 1 - y), (1 - x, 1 - y)]

        def rows(px, py, pc):
            return out_ref.at[pl.ds((4 * px + 2 * py + pc) * m_per, m_per), :]

        def copy(k, block, to, src=None):
            return pltpu.make_async_remote_copy(
                src_ref=rows(*block) if src is None else src, dst_ref=rows(*block),
                send_sem=send_sems.at[k], recv_sem=recv_sems.at[k], device_id=to, device_id_type=MESH)

        mine = pltpu.make_async_copy(x_ref, rows(*me), local_sem)
        mine.start()
        first = [copy(0, me, sibling, src=x_ref)]
        first += [copy(1 + j, me, (*chip, c), src=x_ref) for j, chip in enumerate(chips)]
        for cp in first:
            cp.start()
        passed = [copy(4 + j, (*chip, c), sibling) for j, chip in enumerate(chips)]
        for j, chip in enumerate(chips):
            copy(1 + j, (*chip, c), me).wait_recv()
            passed[j].start()
        copy(0, sibling, me).wait_recv()
        for j, chip in enumerate(chips):
            copy(4 + j, (*chip, 1 - c), me).wait_recv()
        for cp in first + passed:
            cp.wait_send()
        mine.wait()

    return pl.pallas_call(
        body, name=name,
        out_shape=jax.ShapeDtypeStruct((N_DEV * m_per, n), x_shard.dtype),
        in_specs=[pl.BlockSpec(memory_space=pltpu.VMEM)],
        out_specs=pl.BlockSpec(memory_space=pltpu.VMEM),
        scratch_shapes=[pltpu.SemaphoreType.DMA((7,)), pltpu.SemaphoreType.DMA((7,)), pltpu.SemaphoreType.DMA],
        compiler_params=pltpu.CompilerParams(vmem_limit_bytes=VMEM_LIMIT),
    )(x_shard)


def _chip_exchange(arrs, bcast, name):
    n = len(arrs)
    out_shapes = [jax.ShapeDtypeStruct((N_CHIP,) + (a.shape if bcast else a.shape[1:]), a.dtype) for a in arrs]

    def body(*refs):
        ins, outs = refs[:n], refs[n:2 * n]
        send_sems, recv_sems, local_sems = refs[2 * n:]
        x, y, c = lax.axis_index("x"), lax.axis_index("y"), lax.axis_index("c")
        me = 2 * x + y
        peers = [(1 - x, y), (x, 1 - y), (1 - x, 1 - y)]
        copies = []
        for a in range(n):
            src_own = ins[a] if bcast else ins[a].at[me]
            loc = pltpu.make_async_copy(src_own, outs[a].at[me], local_sems.at[a])
            loc.start()
            copies.append(loc)
        remote = []
        for a in range(n):
            for k, (px, py) in enumerate(peers):
                them = 2 * px + py
                cp = pltpu.make_async_remote_copy(
                    src_ref=ins[a] if bcast else ins[a].at[them], dst_ref=outs[a].at[me],
                    send_sem=send_sems.at[3 * a + k], recv_sem=recv_sems.at[3 * a + k],
                    device_id=(px, py, c), device_id_type=MESH)
                cp.start()
                remote.append((cp, a, k, them))
        for cp, a, k, them in remote:
            pltpu.make_async_remote_copy(
                src_ref=ins[a] if bcast else ins[a].at[them], dst_ref=outs[a].at[them],
                send_sem=send_sems.at[3 * a + k], recv_sem=recv_sems.at[3 * a + k],
                device_id=(x, y, c), device_id_type=MESH).wait_recv()
        for cp, a, k, them in remote:
            cp.wait_send()
        for loc in copies:
            loc.wait()

    hbm = pl.BlockSpec(memory_space=pltpu.HBM)
    return pl.pallas_call(
        body, name=name, out_shape=out_shapes, in_specs=[hbm] * n, out_specs=[hbm] * n,
        scratch_shapes=[pltpu.SemaphoreType.DMA((3 * n,)), pltpu.SemaphoreType.DMA((3 * n,)), pltpu.SemaphoreType.DMA((n,))],
    )(*arrs)


def _gather_weights(arrs, name):
    n = len(arrs)
    out_shapes = [jax.ShapeDtypeStruct((N_CHIP,) + a.shape, a.dtype) for a in arrs]

    def body(*refs):
        ins, outs = refs[:n], refs[n:2 * n]
        ici_send, ici_recv, d2d_send, d2d_recv = refs[2 * n:]
        x, y, c = lax.axis_index("x"), lax.axis_index("y"), lax.axis_index("c")
        me = 2 * x + y
        sibling = (x, y, 1 - c)
        peers = [(1 - x, y), (x, 1 - y), (1 - x, 1 - y)]

        def half(a, cc):
            rh = arrs[a].shape[0] // 2
            return pl.ds(pl.multiple_of(cc * rh, 16), rh)

        sent = []
        for a in range(n):
            for k, (px, py) in enumerate(peers):
                cp = pltpu.make_async_remote_copy(
                    src_ref=ins[a].at[half(a, c)], dst_ref=outs[a].at[me, half(a, c)],
                    send_sem=ici_send.at[3 * a + k], recv_sem=ici_recv.at[3 * a + k],
                    device_id=(px, py, c), device_id_type=MESH)
                cp.start()
                sent.append(cp)
        for a in range(n):
            for k, (px, py) in enumerate(peers):
                landed = outs[a].at[2 * px + py, half(a, c)]
                pltpu.make_async_remote_copy(
                    src_ref=landed, dst_ref=landed, send_sem=ici_send.at[3 * a + k], recv_sem=ici_recv.at[3 * a + k],
                    device_id=(px, py, c), device_id_type=MESH).wait_recv()
                fwd = pltpu.make_async_remote_copy(
                    src_ref=landed, dst_ref=landed, send_sem=d2d_send.at[3 * a + k], recv_sem=d2d_recv.at[3 * a + k],
                    device_id=sibling, device_id_type=MESH)
                fwd.start()
                sent.append(fwd)
        for a in range(n):
            for k, (px, py) in enumerate(peers):
                passed = outs[a].at[2 * px + py, half(a, 1 - c)]
                pltpu.make_async_remote_copy(
                    src_ref=passed, dst_ref=passed, send_sem=d2d_send.at[3 * a + k], recv_sem=d2d_recv.at[3 * a + k],
                    device_id=sibling, device_id_type=MESH).wait_recv()
        for cp in sent:
            cp.wait_send()

    hbm = pl.BlockSpec(memory_space=pltpu.HBM)
    gathered = pl.pallas_call(
        body, name=name, out_shape=out_shapes, in_specs=[hbm] * n, out_specs=[hbm] * n,
        scratch_shapes=[pltpu.SemaphoreType.DMA((3 * n,))] * 4,
    )(*arrs)
    chip = 2 * lax.axis_index("x") + lax.axis_index("y")
    return [lax.dynamic_update_slice(g, a[None], (chip, 0, 0)) for g, a in zip(gathered, arrs)]


def _sibling_exchange(arrs, name):
    n = len(arrs)

    def body(*refs):
        ins, outs = refs[:n], refs[n:2 * n]
        send_sems, recv_sems = refs[2 * n:]
        sibling = (lax.axis_index("x"), lax.axis_index("y"), 1 - lax.axis_index("c"))
        cps = []
        for a in range(n):
            cp = pltpu.make_async_remote_copy(src_ref=ins[a], dst_ref=outs[a], send_sem=send_sems.at[a],
                                              recv_sem=recv_sems.at[a], device_id=sibling, device_id_type=MESH)
            cp.start()
            cps.append(cp)
        for cp in cps:
            cp.wait_recv()
        for cp in cps:
            cp.wait_send()

    hbm = pl.BlockSpec(memory_space=pltpu.HBM)
    return pl.pallas_call(
        body, name=name, out_shape=[jax.ShapeDtypeStruct(a.shape, a.dtype) for a in arrs],
        in_specs=[hbm] * n, out_specs=[hbm] * n,
        scratch_shapes=[pltpu.SemaphoreType.DMA((n,)), pltpu.SemaphoreType.DMA((n,))],
    )(*arrs)


def _matmul(a, b, mode, out_dtype, name, tm=1024, tn=1024, tk=2048, relu2=False, times=None, b_split=False,
            out_split=False):
    b_shape = (b.shape[1], b.shape[2] * N_CHIP) if b_split else b.shape
    if mode == "nn":
        (m, k), (k2, n) = a.shape, b_shape
    elif mode == "nt":
        (m, k), (n, k2) = a.shape, b_shape
    else:
        (k, m), (k2, n) = a.shape, b_shape
    assert k == k2, (a.shape, b.shape, mode)
    n_cut = n // N_CHIP if (out_split or (b_split and mode != "nt")) else n
    k_cut = k // N_CHIP if (b_split and mode == "nt") else k
    tm, tn, tk = _tile(m, tm, LANES), _tile(n_cut, tn, LANES), _tile(k_cut, tk, LANES)
    assert n_cut % tn == 0 and k_cut % tk == 0 and m % tm == 0, (name, m, n, k, tm, tn, tk)
    nk = k // tk
    nbc, nkc = n_cut // tn, k_cut // tk
    n_in = 2 if times is None else 3
    n_out = 2 if relu2 else 1

    def product(a_ref, b_ref):
        if mode == "nn":
            return jnp.dot(a_ref[...], b_ref[...], preferred_element_type=F32)
        return lax.dot_general(a_ref[...], b_ref[...], NT_DIMS if mode == "nt" else TN_DIMS, preferred_element_type=F32)

    def finish(p, refs, o_refs):
        if relu2:
            p = jnp.maximum(p, 0.0)
            o_refs[0][...] = p.astype(o_refs[0].dtype)
            o_refs[1][...] = (p * p).astype(o_refs[1].dtype)
        elif times is not None:
            o_refs[0][...] = (2.0 * refs[2][...].astype(F32) * p).astype(o_refs[0].dtype)
        else:
            o_refs[0][...] = p.astype(o_refs[0].dtype)

    def body(*refs):
        o_refs = refs[n_in:n_in + n_out]
        if nk == 1:
            finish(product(refs[0], refs[1]), refs, o_refs)
            return
        acc_ref = refs[n_in + n_out]
        kk = pl.program_id(2)

        @pl.when(kk == 0)
        def _():
            acc_ref[...] = product(refs[0], refs[1])

        @pl.when((kk > 0) & (kk < nk - 1))
        def _():
            acc_ref[...] += product(refs[0], refs[1])

        @pl.when(kk == nk - 1)
        def _():
            finish(acc_ref[...] + product(refs[0], refs[1]), refs, o_refs)

    if mode == "tn":
        a_spec = pl.BlockSpec((tk, tm), lambda i, j, kk: (kk, i))
    else:
        a_spec = pl.BlockSpec((tm, tk), lambda i, j, kk: (i, kk))
    if mode == "nt":
        b_spec = (pl.BlockSpec((None, tn, tk), lambda i, j, kk: (kk // nkc, j, kk % nkc)) if b_split
                  else pl.BlockSpec((tn, tk), lambda i, j, kk: (j, kk)))
    else:
        b_spec = (pl.BlockSpec((None, tk, tn), lambda i, j, kk: (j // nbc, kk, j % nbc)) if b_split
                  else pl.BlockSpec((tk, tn), lambda i, j, kk: (kk, j)))
    mn_spec = pl.BlockSpec((tm, tn), lambda i, j, kk: (i, j))
    if out_split:
        o_spec = pl.BlockSpec((None, tm, tn), lambda i, j, kk: (j // nbc, i, j % nbc))
        o_shape = jax.ShapeDtypeStruct((N_CHIP, m, n_cut), out_dtype)
    else:
        o_spec, o_shape = mn_spec, jax.ShapeDtypeStruct((m, n), out_dtype)
    out = pl.pallas_call(
        body, name=name, grid=(m // tm, n // tn, nk), in_specs=[a_spec, b_spec] + [mn_spec] * (n_in - 2),
        out_specs=[o_spec] * n_out, out_shape=[o_shape] * n_out,
        scratch_shapes=[] if nk == 1 else [pltpu.VMEM((tm, tn), F32)],
        compiler_params=_params(("parallel", "parallel", "arbitrary")),
    )(*((a, b) if times is None else (a, b, times)))
    return out if relu2 else out[0]


def _mod_part(c_all, w_s, b_s, name):
    d, na = w_s.shape
    tn = _tile(na, 512, LANES)

    def body(c_ref, w_ref, b_ref, o_ref):
        ca = _silu(c_ref[...]).astype(BF16)
        o_ref[...] = jnp.dot(ca, w_ref[...].astype(BF16), preferred_element_type=F32) + b_ref[...]

    return pl.pallas_call(
        body, name=name, grid=(na // tn,),
        in_specs=[pl.BlockSpec((N_DEV, d), lambda j: (0, 0)), pl.BlockSpec((d, tn), lambda j: (0, j)),
                  pl.BlockSpec((1, tn), lambda j: (0, j))],
        out_specs=pl.BlockSpec((N_DEV, tn), lambda j: (0, j)),
        out_shape=jax.ShapeDtypeStruct((N_DEV, na), F32), compiler_params=_params(("parallel",)),
    )(c_all, w_s, b_s)


def _wada_grad(c_all, dmod_s, name):
    d = c_all.shape[1]
    na = dmod_s.shape[1]
    td, tn = _tile(d, 512, LANES), _tile(na, 512, LANES)

    def body(c_ref, g_ref, o_ref):
        o_ref[...] = lax.dot_general(_silu(c_ref[...]), g_ref[...], TN_DIMS, precision=HI, preferred_element_type=F32)

    return pl.pallas_call(
        body, name=name, grid=(d // td, na // tn),
        in_specs=[pl.BlockSpec((N_DEV, td), lambda i, j: (0, i)), pl.BlockSpec((N_DEV, tn), lambda i, j: (0, j))],
        out_specs=pl.BlockSpec((td, tn), lambda i, j: (i, j)),
        out_shape=jax.ShapeDtypeStruct((d, na), F32), compiler_params=_params(("parallel", "parallel")),
    )(c_all, dmod_s)


def _row_specs(tb, d, n_full, n_vec):
    full = pl.BlockSpec((tb, d), lambda i: (i, 0))
    vec = pl.BlockSpec((1, d), lambda i: (0, 0))
    return [full] * n_full + [vec] * n_vec


def _norm_mod(x, w, sc, sh, name):
    t, d = x.shape
    tb = _tile(t, 256, SUBLANES)

    def body(x_ref, w_ref, sc_ref, sh_ref, o_ref):
        xv = x_ref[...]
        o_ref[...] = (xv * _rms_scale(xv) * w_ref[...] * (1.0 + sc_ref[...]) + sh_ref[...]).astype(o_ref.dtype)

    return pl.pallas_call(
        body, name=name, grid=(t // tb,), in_specs=_row_specs(tb, d, 1, 3),
        out_specs=pl.BlockSpec((tb, d), lambda i: (i, 0)), out_shape=jax.ShapeDtypeStruct((t, d), BF16),
        compiler_params=_params(("parallel",)),
    )(x, w, sc, sh)


def _norm_mod_bwd(x, w, sc, dh, dres, name):
    t, d = x.shape
    tb = _tile(t, 256, SUBLANES)

    def body(x_ref, w_ref, sc_ref, dh_ref, dres_ref, dx_ref, dw_ref, dsc_ref, dsh_ref):
        @pl.when(pl.program_id(0) == 0)
        def _():
            dw_ref[...] = jnp.zeros_like(dw_ref)
            dsc_ref[...] = jnp.zeros_like(dsc_ref)
            dsh_ref[...] = jnp.zeros_like(dsh_ref)

        xv = x_ref[...]
        r = _rms_scale(xv)
        xn = xv * r
        g = dh_ref[...].astype(F32)
        wv, one_sc = w_ref[...], 1.0 + sc_ref[...]
        gxn = g * xn
        dsh_ref[...] += jnp.sum(g, axis=0, keepdims=True)
        dsc_ref[...] += jnp.sum(gxn, axis=0, keepdims=True) * wv
        dw_ref[...] += jnp.sum(gxn, axis=0, keepdims=True) * one_sc
        dxn = g * (wv * one_sc)
        dx_ref[...] = dres_ref[...] + r * (dxn - xn * jnp.mean(dxn * xn, axis=-1, keepdims=True))

    vec_out = pl.BlockSpec((1, d), lambda i: (0, 0))
    return pl.pallas_call(
        body, name=name, grid=(t // tb,),
        in_specs=[pl.BlockSpec((tb, d), lambda i: (i, 0)), pl.BlockSpec((1, d), lambda i: (0, 0)),
                  pl.BlockSpec((1, d), lambda i: (0, 0)), pl.BlockSpec((tb, d), lambda i: (i, 0)),
                  pl.BlockSpec((tb, d), lambda i: (i, 0))],
        out_specs=[pl.BlockSpec((tb, d), lambda i: (i, 0)), vec_out, vec_out, vec_out],
        out_shape=[jax.ShapeDtypeStruct((t, d), F32)] + [jax.ShapeDtypeStruct((1, d), F32)] * 3,
        compiler_params=_params(("arbitrary",)),
    )(x, w, sc, dh, dres)


def _resid(x, y, w, gt, name):
    t, d = x.shape
    tb = _tile(t, 256, SUBLANES)

    def body(x_ref, y_ref, w_ref, gt_ref, o_ref):
        yv = y_ref[...]
        o_ref[...] = x_ref[...] + gt_ref[...] * (yv * _rms_scale(yv) * w_ref[...])

    return pl.pallas_call(
        body, name=name, grid=(t // tb,), in_specs=_row_specs(tb, d, 2, 2),
        out_specs=pl.BlockSpec((tb, d), lambda i: (i, 0)), out_shape=jax.ShapeDtypeStruct((t, d), F32),
        compiler_params=_params(("parallel",)),
    )(x, y, w, gt)


def _loss_head(x2, y2, w, gt, target, name):
    t, d = x2.shape
    tb = _tile(t, 256, SUBLANES)

    def body(x_ref, y_ref, tg_ref, w_ref, gt_ref, do_ref, loss_ref):
        @pl.when(pl.program_id(0) == 0)
        def _():
            loss_ref[...] = jnp.zeros_like(loss_ref)

        yv = y_ref[...]
        out = x_ref[...] + gt_ref[...] * (yv * _rms_scale(yv) * w_ref[...])
        err = out - tg_ref[...]
        do_ref[...] = err * (1.0 / d)
        per_tok = jnp.mean(err * err, axis=-1, keepdims=True)
        loss_ref[...] += 0.5 * jnp.sum(per_tok, axis=0, keepdims=True)

    return pl.pallas_call(
        body, name=name, grid=(t // tb,), in_specs=_row_specs(tb, d, 3, 2),
        out_specs=[pl.BlockSpec((tb, d), lambda i: (i, 0)), pl.BlockSpec((1, LANES), lambda i: (0, 0))],
        out_shape=[jax.ShapeDtypeStruct((t, d), F32), jax.ShapeDtypeStruct((1, LANES), F32)],
        compiler_params=_params(("arbitrary",)),
    )(x2, y2, target, w, gt)


def _resid_bwd(dout, y, w, gt, name):
    t, d = y.shape
    tb = _tile(t, 256, SUBLANES)

    def body(do_ref, y_ref, w_ref, gt_ref, dy_ref, dgt_ref, dw_ref):
        @pl.when(pl.program_id(0) == 0)
        def _():
            dgt_ref[...] = jnp.zeros_like(dgt_ref)
            dw_ref[...] = jnp.zeros_like(dw_ref)

        yv, g = y_ref[...], do_ref[...]
        r = _rms_scale(yv)
        yn = yv * r
        wv, gtv = w_ref[...], gt_ref[...]
        gyn = jnp.sum(g * yn, axis=0, keepdims=True)
        dgt_ref[...] += gyn * wv
        dw_ref[...] += gyn * gtv
        dyn = g * (gtv * wv)
        dy_ref[...] = (r * (dyn - yn * jnp.mean(dyn * yn, axis=-1, keepdims=True))).astype(dy_ref.dtype)

    vec_out = pl.BlockSpec((1, d), lambda i: (0, 0))
    return pl.pallas_call(
        body, name=name, grid=(t // tb,), in_specs=_row_specs(tb, d, 2, 2),
        out_specs=[pl.BlockSpec((tb, d), lambda i: (i, 0)), vec_out, vec_out],
        out_shape=[jax.ShapeDtypeStruct((t, d), BF16)] + [jax.ShapeDtypeStruct((1, d), F32)] * 2,
        compiler_params=_params(("arbitrary",)),
    )(dout, y, w, gt)


def _row_half_to_bf16(full, which, sib, name):
    n, r2, c = full.shape
    r = r2 // 2
    tr, tc = _elementwise_tiles(r, c)
    nbh = r // tr

    def body(which_ref, a_ref, *rest):
        if sib is None:
            rest[0][...] = a_ref[...].astype(BF16)
        else:
            rest[1][...] = (a_ref[...] + rest[0][...].astype(F32)).astype(BF16)

    half_spec = pl.BlockSpec((1, tr, tc), lambda j, i, k, which_ref: (j, which_ref[0] * nbh + i, k))
    spec = pl.BlockSpec((1, tr, tc), lambda j, i, k, which_ref: (j, i, k))
    grid_spec = pltpu.PrefetchScalarGridSpec(
        num_scalar_prefetch=1, grid=(n, nbh, c // tc), in_specs=[half_spec] + ([] if sib is None else [spec]), out_specs=spec)
    return pl.pallas_call(
        body, name=name, grid_spec=grid_spec, out_shape=jax.ShapeDtypeStruct((n, r, c), BF16),
        compiler_params=_params(("parallel", "parallel", "parallel")),
    )(which, full, *([] if sib is None else [sib]))


def _sum_chips(recv, name):
    _, r, c = recv.shape
    tr, tc = _elementwise_tiles(r, c)

    def body(x_ref, o_ref):
        acc = x_ref[0].astype(F32)
        for j in range(1, N_CHIP):
            acc = acc + x_ref[j].astype(F32)
        o_ref[...] = acc

    return pl.pallas_call(
        body, name=name, grid=(r // tr, c // tc), in_specs=[pl.BlockSpec((N_CHIP, tr, tc), lambda i, j: (0, i, j))],
        out_specs=pl.BlockSpec((tr, tc), lambda i, j: (i, j)), out_shape=jax.ShapeDtypeStruct((r, c), F32),
        compiler_params=_params(("parallel", "parallel")),
    )(recv)


def _adamw(w, g_parts, m, v, name, by_core=False):
    r, c = w.shape
    tr, tc = _elementwise_tiles(r // 2 if by_core else r, c)
    n_g = len(g_parts)
    nbh = (r // 2) // tr
    c1 = 1.0 / (1.0 - ADAM_B1 ** ADAM_STEP)
    c2 = 1.0 / (1.0 - ADAM_B2 ** ADAM_STEP)

    def body(*refs):
        w_ref, g_refs, m_ref, v_ref = refs[0], refs[1:1 + n_g], refs[1 + n_g], refs[2 + n_g]
        g_out, d_out, m_out, v_out = refs[3 + n_g:]
        if by_core:
            in_my_half = (pl.program_id(0) // nbh) == lax.axis_index("c")
            g = jnp.where(in_my_half, g_refs[0][...], g_refs[1][...])
        else:
            g = g_refs[0][...]
        mn = ADAM_B1 * m_ref[...] + (1.0 - ADAM_B1) * g
        vn = ADAM_B2 * v_ref[...] + (1.0 - ADAM_B2) * (g * g)
        g_out[...] = g
        m_out[...] = mn
        v_out[...] = vn
        d_out[...] = -ADAM_LR * ((mn * c1) / (jnp.sqrt(vn * c2) + ADAM_EPS) + ADAM_WD * w_ref[...])

    spec = pl.BlockSpec((tr, tc), lambda i, j: (i, j))
    g_spec = pl.BlockSpec((tr, tc), lambda i, j: (i % nbh, j)) if by_core else spec
    return pl.pallas_call(
        body, name=name, grid=(r // tr, c // tc), in_specs=[spec] + [g_spec] * n_g + [spec] * 2, out_specs=[spec] * 4,
        out_shape=[jax.ShapeDtypeStruct((r, c), F32)] * 4, compiler_params=_params(("parallel", "parallel")),
    )(w, *g_parts, m, v)


def _conv_taps(u, t):
    rows = lax.broadcasted_iota(jnp.int32, u.shape, 0)
    return [u] + [jnp.where(rows >= dd, pltpu.roll(u, dd, 0), 0.0) for dd in range(1, CONV_K)]


def _conv_fwd(proj, conv_w, col0, name):
    t = proj.shape[0]
    ch = conv_w.shape[1]

    def body(u_ref, w_ref, o_ref):
        taps = _conv_taps(u_ref[...], t)
        wv = w_ref[...]
        y = taps[0] * wv[CONV_K - 1:CONV_K]
        for dd in range(1, CONV_K):
            y = y + taps[dd] * wv[CONV_K - 1 - dd:CONV_K - dd]
        o_ref[...] = _silu(y)

    return pl.pallas_call(
        body, name=name, grid=(ch // LANES,),
        in_specs=[pl.BlockSpec((t, LANES), lambda j: (0, col0 + j)), pl.BlockSpec((CONV_K, LANES), lambda j: (0, j))],
        out_specs=pl.BlockSpec((t, LANES), lambda j: (0, j)), out_shape=jax.ShapeDtypeStruct((t, ch), F32),
        compiler_params=_params(("parallel",)),
    )(proj, conv_w)


def _conv_bwd(proj, conv_w, ds, col0, name):
    t = proj.shape[0]
    ch = conv_w.shape[1]

    def body(u_ref, w_ref, ds_ref, du_ref, dw_ref):
        u = u_ref[...]
        taps = _conv_taps(u, t)
        wv = w_ref[...]
        y = taps[0] * wv[CONV_K - 1:CONV_K]
        for dd in range(1, CONV_K):
            y = y + taps[dd] * wv[CONV_K - 1 - dd:CONV_K - dd]
        sg = jax.nn.sigmoid(y)
        dy = ds_ref[...] * (sg * (1.0 + y * (1.0 - sg)))
        rows = lax.broadcasted_iota(jnp.int32, u.shape, 0)
        du = dy * wv[CONV_K - 1:CONV_K]
        for dd in range(1, CONV_K):
            ahead = jnp.where(rows < t - dd, pltpu.roll(dy, t - dd, 0), 0.0)
            du = du + ahead * wv[CONV_K - 1 - dd:CONV_K - dd]
        du_ref[...] = du.astype(du_ref.dtype)
        dws = [jnp.sum(dy * taps[CONV_K - 1 - j], axis=0, keepdims=True) for j in range(CONV_K)]
        dw_ref[...] = jnp.concatenate(dws, axis=0)

    return pl.pallas_call(
        body, name=name, grid=(ch // LANES,),
        in_specs=[pl.BlockSpec((t, LANES), lambda j: (0, col0 + j)), pl.BlockSpec((CONV_K, LANES), lambda j: (0, j)),
                  pl.BlockSpec((t, LANES), lambda j: (0, j))],
        out_specs=[pl.BlockSpec((t, LANES), lambda j: (0, j)), pl.BlockSpec((CONV_K, LANES), lambda j: (0, j))],
        out_shape=[jax.ShapeDtypeStruct((t, ch), BF16), jax.ShapeDtypeStruct((CONV_K, ch), F32)],
        compiler_params=_params(("parallel",)),
    )(proj, conv_w, ds)


def _hg_block(st, q, fl, vi, g, l0, l1, nw):
    hs = range(len(st))
    tb = q[0].shape[0]
    ln = HG_SUB
    lb = [jax.nn.sigmoid(l0[h] - l1[h]) for h in hs]
    rows = lax.broadcasted_iota(jnp.int32, (ln, HEAD), 0)
    tri = (lax.broadcasted_iota(jnp.int32, (ln, ln), 0) >= lax.broadcasted_iota(jnp.int32, (ln, ln), 1)).astype(F32)
    st = list(st)
    outs = [[] for _ in hs]
    for i in range(tb // ln):
        sl = slice(i * ln, (i + 1) * ln)
        qs, vs = [q[h][sl] for h in hs], [vi[h][sl] for h in hs]
        f = [lb[h] + (1.0 - lb[h]) * jax.nn.sigmoid(fl[h][sl]) for h in hs]
        k = [1.0 - f[h] for h in hs]
        b = [jnp.dot(tri, jnp.log(f[h]), precision=HI, preferred_element_type=F32) for h in hs]
        o = [lax.dot_general((qs[h] * jnp.exp(b[h])).astype(BF16), st[h].astype(BF16), NT_DIMS, preferred_element_type=F32)
             for h in hs]
        for s in range(ln):
            e = [jnp.exp(jnp.where(rows >= s, b[h] - b[h][s:s + 1], -1e30)) for h in hs]
            a = [jnp.sum(qs[h] * e[h] * k[h][s:s + 1], axis=-1, keepdims=True) for h in hs]
            o = [o[h] + a[h] * vs[h][s:s + 1] for h in hs]
        kt = [k[h] * jnp.exp(b[h][ln - 1:ln] - b[h]) for h in hs]
        upd = [lax.dot_general(vs[h].astype(BF16), kt[h].astype(BF16), TN_DIMS, preferred_element_type=F32) for h in hs]
        st = [st[h] * jnp.exp(b[h][ln - 1:ln]) + upd[h] for h in hs]
        for h in hs:
            outs[h].append(o[h])
    o = [jnp.concatenate(outs[h], axis=0) for h in hs]
    out = [o[h] * _rms_scale(o[h]) * nw * _silu(g[h]) for h in hs]
    return st, out


def _head_cols(h):
    return slice(h * HEAD, (h + 1) * HEAD)


def _head_groups(n_heads):
    g = min(HEAD_GROUP, n_heads)
    return [list(range(i, min(i + g, n_heads))) for i in range(0, n_heads, g)]


def _hg_in_specs(n_heads, tb, time_index):
    hw = n_heads * HEAD
    cols = [pl.BlockSpec((tb, hw), functools.partial(lambda part, j: (time_index(j), part), part)) for part in range(4)]
    head_rows = pl.BlockSpec((n_heads, 1, HEAD), lambda j: (0, 0, 0))
    return cols + [head_rows, head_rows, pl.BlockSpec((1, HEAD), lambda j: (0, 0))]


def _hgrn2_fwd(proj, l0, l1, nw, n_heads, name):
    t = proj.shape[0]
    hw = n_heads * HEAD
    tb = _tile(t, HG_BLOCK, HG_SUB)
    nb = t // tb

    def body(q_ref, f_ref, i_ref, g_ref, l0_ref, l1_ref, nw_ref, o_ref, save_ref, st_ref):
        @pl.when(pl.program_id(0) == 0)
        def _():
            st_ref[...] = jnp.zeros_like(st_ref)

        for hs in _head_groups(n_heads):
            st = [st_ref[h] for h in hs]
            for h, s in zip(hs, st):
                save_ref[h] = s
            st, out = _hg_block(st, *[[r[:, _head_cols(h)] for h in hs] for r in (q_ref, f_ref, i_ref, g_ref)],
                                [l0_ref[h] for h in hs], [l1_ref[h] for h in hs], nw_ref[...])
            for h, s, o in zip(hs, st, out):
                st_ref[h] = s
                o_ref[:, _head_cols(h)] = o.astype(o_ref.dtype)

    return pl.pallas_call(
        body, name=name, grid=(nb,), in_specs=_hg_in_specs(n_heads, tb, lambda j: j),
        out_specs=[pl.BlockSpec((tb, hw), lambda j: (j, 0)),
                   pl.BlockSpec((None, n_heads, HEAD, HEAD), lambda j: (j, 0, 0, 0))],
        out_shape=[jax.ShapeDtypeStruct((t, hw), BF16), jax.ShapeDtypeStruct((nb, n_heads, HEAD, HEAD), F32)],
        scratch_shapes=[pltpu.VMEM((n_heads, HEAD, HEAD), F32)], compiler_params=_params(("arbitrary",)),
    )(proj, proj, proj, proj, l0, l1, nw)


def _hgrn2_bwd(proj, l0, l1, nw, saved, d_ocat, n_heads, name):
    t = proj.shape[0]
    tb = _tile(t, HG_BLOCK, HG_SUB)
    nb = t // tb
    rev = lambda j: nb - 1 - j

    hw = n_heads * HEAD

    def body(q_ref, f_ref, i_ref, g_ref, l0_ref, l1_ref, nw_ref, save_ref, do_ref,
             dp_ref, dl0_ref, dl1_ref, dnw_ref, dst_ref):
        @pl.when(pl.program_id(0) == 0)
        def _():
            dst_ref[...] = jnp.zeros_like(dst_ref)
            dl0_ref[...] = jnp.zeros_like(dl0_ref)
            dl1_ref[...] = jnp.zeros_like(dl1_ref)
            dnw_ref[...] = jnp.zeros_like(dnw_ref)

        dnw_acc = jnp.zeros((1, HEAD), F32)
        for hs in _head_groups(n_heads):
            _, vjp = jax.vjp(_hg_block, [save_ref[h] for h in hs],
                             *[[r[:, _head_cols(h)] for h in hs] for r in (q_ref, f_ref, i_ref, g_ref)],
                             [l0_ref[h] for h in hs], [l1_ref[h] for h in hs], nw_ref[...])
            dst, dq, df, di, dg, dl0, dl1, dnw = vjp(([dst_ref[h] for h in hs], [do_ref[:, _head_cols(h)] for h in hs]))
            for i, h in enumerate(hs):
                dst_ref[h] = dst[i]
                for part, val in enumerate((dq, df, di, dg)):
                    dp_ref[:, part * hw + h * HEAD:part * hw + (h + 1) * HEAD] = val[i].astype(dp_ref.dtype)
                dl0_ref[h] += dl0[i]
                dl1_ref[h] += dl1[i]
            dnw_acc = dnw_acc + dnw
        dnw_ref[...] += dnw_acc

    head_rows = pl.BlockSpec((n_heads, 1, HEAD), lambda j: (0, 0, 0))
    return pl.pallas_call(
        body, name=name, grid=(nb,),
        in_specs=_hg_in_specs(n_heads, tb, rev) + [pl.BlockSpec((None, n_heads, HEAD, HEAD), lambda j: (rev(j), 0, 0, 0)),
                                                   pl.BlockSpec((tb, hw), lambda j: (rev(j), 0))],
        out_specs=[pl.BlockSpec((tb, 4 * hw), lambda j: (rev(j), 0)), head_rows, head_rows,
                   pl.BlockSpec((1, HEAD), lambda j: (0, 0))],
        out_shape=[jax.ShapeDtypeStruct((t, 4 * hw), BF16)] + [jax.ShapeDtypeStruct((n_heads, 1, HEAD), F32)] * 2
        + [jax.ShapeDtypeStruct((1, HEAD), F32)],
        scratch_shapes=[pltpu.VMEM((n_heads, HEAD, HEAD), F32)], compiler_params=_params(("arbitrary",)),
    )(proj, proj, proj, proj, l0, l1, nw, saved, d_ocat)


def _dot_hi(a, b, dims=(((1,), (0,)), ((), ()))):
    return lax.dot_general(a, b, dims, precision=HI, preferred_element_type=F32)


def _dot_bf16(a, b, dims=(((1,), (0,)), ((), ()))):
    return lax.dot_general(a.astype(BF16), b.astype(BF16), dims, preferred_element_type=F32)


def _inv_unit_lower_raw(ms):
    hs = range(len(ms))
    c = ms[0].shape[0]
    r = lax.broadcasted_iota(jnp.int32, (c, c), 0)
    q = lax.broadcasted_iota(jnp.int32, (c, c), 1)
    eye = (r == q).astype(F32)
    md = [jnp.where((r // GDN_INV_BLOCK) == (q // GDN_INV_BLOCK), ms[h], 0.0) for h in hs]
    p = [-md[h] for h in hs]
    t16 = [eye + p[h] for h in hs]
    for _ in range(int(math.log2(GDN_INV_BLOCK)) - 1):
        p = [_dot_hi(p[h], p[h]) for h in hs]
        t16 = [t16[h] + _dot_hi(t16[h], p[h]) for h in hs]
    p = [-_dot_hi(t16[h], ms[h] - md[h]) for h in hs]
    t2 = [eye + p[h] for h in hs]
    for _ in range(int(math.log2(c // GDN_INV_BLOCK)) - 1):
        p = [_dot_hi(p[h], p[h]) for h in hs]
        t2 = [t2[h] + _dot_hi(t2[h], p[h]) for h in hs]
    return [_dot_hi(t2[h], t16[h]) for h in hs]


@jax.custom_vjp
def _inv_unit_lower(ms):
    return _inv_unit_lower_raw(ms)


def _inv_fwd(ms):
    ts = _inv_unit_lower_raw(ms)
    return ts, ts


def _inv_bwd(ts, dts):
    hs = range(len(ts))
    inner = [_dot_hi(ts[h], dts[h], TN_DIMS) for h in hs]
    return ([-_dot_hi(inner[h], ts[h], NT_DIMS) for h in hs],)


_inv_unit_lower.defvjp(_inv_fwd, _inv_bwd)


def _gdn_block(inverse, onehots, st, qc, kc, vc, g, ab, alog_row, dtb_row, nw):
    hs = range(len(st))
    c = qc[0].shape[0]
    lane_sum = lambda v: jnp.sum(v, axis=-1, keepdims=True)
    a = [lane_sum(ab * onehots[h][0]) for h in hs]
    bb = [lane_sum(ab * onehots[h][1]) for h in hs]
    alog = [lane_sum(alog_row * onehots[h][0]) for h in hs]
    dtb = [lane_sum(dtb_row * onehots[h][0]) for h in hs]
    la = [-jnp.exp(alog[h]) * _softplus(a[h] + dtb[h]) for h in hs]
    beta = [jax.nn.sigmoid(bb[h]) for h in hs]
    q = [qc[h] * lax.rsqrt(lane_sum(qc[h] * qc[h]) + EPS) * (HEAD ** -0.5) for h in hs]
    k = [kc[h] * lax.rsqrt(lane_sum(kc[h] * kc[h]) + EPS) for h in hs]
    r = lax.broadcasted_iota(jnp.int32, (c, c), 0)
    s = lax.broadcasted_iota(jnp.int32, (c, c), 1)
    tri = (r >= s).astype(F32)
    g_cc = [_dot_hi(tri, jnp.broadcast_to(la[h], (c, c))) for h in hs]
    g_cl = [_dot_hi(tri, jnp.broadcast_to(la[h], (c, HEAD))) for h in hs]
    gamma = [jnp.exp(jnp.where(r >= s, g_cc[h] - g_cc[h].T, -1e30)) for h in hs]
    kk = [_dot_bf16(k[h], k[h], NT_DIMS) for h in hs]
    m = [jnp.where(r > s, beta[h] * kk[h] * gamma[h], 0.0) for h in hs]
    tm = inverse(m)
    eg = [jnp.exp(g_cl[h]) for h in hs]
    rhs = [jnp.concatenate([vc[h] * beta[h], k[h] * (beta[h] * eg[h])], axis=1) for h in hs]
    sol = [_dot_hi(tm[h], rhs[h]) for h in hs]
    qk = [_dot_bf16(q[h], k[h], NT_DIMS) * gamma[h] for h in hs]
    g_last = [g_cl[h][c - 1:c] for h in hs]
    k_tail = [k[h] * jnp.exp(g_last[h] - g_cl[h]) for h in hs]
    v_new = [sol[h][:, :HEAD] - _dot_bf16(sol[h][:, HEAD:], st[h], NT_DIMS) for h in hs]
    o_st = [_dot_bf16(q[h] * eg[h], st[h], NT_DIMS) for h in hs]
    o = [o_st[h] + _dot_bf16(qk[h], v_new[h]) for h in hs]
    upd = [_dot_bf16(v_new[h], k_tail[h], TN_DIMS) for h in hs]
    st = [st[h] * jnp.exp(g_last[h]) + upd[h] for h in hs]
    out = [o[h] * _rms_scale(o[h]) * nw * _silu(g[h]) for h in hs]
    return st, out


def _head_onehots(n_heads, h):
    lane = lax.broadcasted_iota(jnp.int32, (1, LANES), 1)
    return (lane == h).astype(F32), (lane == n_heads + h).astype(F32)


def _gdn_in_specs(n_heads, c, time_index):
    hw = n_heads * HEAD
    qkv = [pl.BlockSpec((c, hw), functools.partial(lambda part, j: (time_index(j), part), part)) for part in range(3)]
    row = pl.BlockSpec((1, LANES), lambda j: (0, 0))
    return qkv + [pl.BlockSpec((c, hw), lambda j: (time_index(j), 7)),
                  pl.BlockSpec((c, LANES), lambda j: (time_index(j), 8 * n_heads)), row, row, row]


def _gdn_fwd(qkv, proj, alog_row, dtb_row, nw, n_heads, name):
    t = qkv.shape[0]
    hw = n_heads * HEAD
    c = _tile(t, GDN_CHUNK, GDN_CHUNK)
    nb = t // c

    def body(q_ref, k_ref, v_ref, g_ref, ab_ref, al_ref, dt_ref, nw_ref, o_ref, save_ref, st_ref):
        @pl.when(pl.program_id(0) == 0)
        def _():
            st_ref[...] = jnp.zeros_like(st_ref)

        for hs in _head_groups(n_heads):
            st = [st_ref[h] for h in hs]
            for h, s in zip(hs, st):
                save_ref[h] = s
            st, out = _gdn_block(_inv_unit_lower_raw, [_head_onehots(n_heads, h) for h in hs], st,
                                 *[[r[:, _head_cols(h)] for h in hs] for r in (q_ref, k_ref, v_ref, g_ref)],
                                 ab_ref[...], al_ref[...], dt_ref[...], nw_ref[...])
            for h, s, o in zip(hs, st, out):
                st_ref[h] = s
                o_ref[:, _head_cols(h)] = o.astype(o_ref.dtype)

    return pl.pallas_call(
        body, name=name, grid=(nb,), in_specs=_gdn_in_specs(n_heads, c, lambda j: j),
        out_specs=[pl.BlockSpec((c, hw), lambda j: (j, 0)),
                   pl.BlockSpec((None, n_heads, HEAD, HEAD), lambda j: (j, 0, 0, 0))],
        out_shape=[jax.ShapeDtypeStruct((t, hw), BF16), jax.ShapeDtypeStruct((nb, n_heads, HEAD, HEAD), F32)],
        scratch_shapes=[pltpu.VMEM((n_heads, HEAD, HEAD), F32)], compiler_params=_params(("arbitrary",)),
    )(qkv, qkv, qkv, proj, proj, alog_row, dtb_row, nw)


def _gdn_bwd(qkv, proj, alog_row, dtb_row, nw, saved, d_ocat, n_heads, name):
    t = qkv.shape[0]
    c = _tile(t, GDN_CHUNK, GDN_CHUNK)
    nb = t // c
    rev = lambda j: nb - 1 - j

    hw = n_heads * HEAD

    def body(q_ref, k_ref, v_ref, g_ref, ab_ref, al_ref, dt_ref, nw_ref, save_ref, do_ref,
             dqkv_ref, dg_ref, dab_ref, dal_ref, ddt_ref, dnw_ref, dst_ref):
        @pl.when(pl.program_id(0) == 0)
        def _():
            dst_ref[...] = jnp.zeros_like(dst_ref)
            dal_ref[...] = jnp.zeros_like(dal_ref)
            ddt_ref[...] = jnp.zeros_like(ddt_ref)
            dnw_ref[...] = jnp.zeros_like(dnw_ref)

        dab_acc = jnp.zeros((c, LANES), F32)
        row_acc = [jnp.zeros((1, LANES), F32)] * 3
        for hs in _head_groups(n_heads):
            fn = functools.partial(_gdn_block, _inv_unit_lower, [_head_onehots(n_heads, h) for h in hs])
            _, vjp = jax.vjp(fn, [save_ref[h] for h in hs],
                             *[[r[:, _head_cols(h)] for h in hs] for r in (q_ref, k_ref, v_ref, g_ref)],
                             ab_ref[...], al_ref[...], dt_ref[...], nw_ref[...])
            dst, dq, dk, dv, dg, dab, dal, ddt, dnw = vjp(([dst_ref[h] for h in hs], [do_ref[:, _head_cols(h)] for h in hs]))
            for i, h in enumerate(hs):
                dst_ref[h] = dst[i]
                for part, val in enumerate((dq, dk, dv)):
                    dqkv_ref[:, part * hw + h * HEAD:part * hw + (h + 1) * HEAD] = val[i]
                dg_ref[:, _head_cols(h)] = dg[i].astype(dg_ref.dtype)
            dab_acc = dab_acc + dab
            row_acc = [acc + val for acc, val in zip(row_acc, (dal, ddt, dnw))]
        dab_ref[...] = dab_acc
        dal_ref[...] += row_acc[0]
        ddt_ref[...] += row_acc[1]
        dnw_ref[...] += row_acc[2]

    row = pl.BlockSpec((1, LANES), lambda j: (0, 0))
    return pl.pallas_call(
        body, name=name, grid=(nb,),
        in_specs=_gdn_in_specs(n_heads, c, rev) + [pl.BlockSpec((None, n_heads, HEAD, HEAD), lambda j: (rev(j), 0, 0, 0)),
                                                   pl.BlockSpec((c, hw), lambda j: (rev(j), 1))],
        out_specs=[pl.BlockSpec((c, 3 * hw), lambda j: (rev(j), 0)), pl.BlockSpec((c, hw), lambda j: (rev(j), 0)),
                   pl.BlockSpec((c, LANES), lambda j: (rev(j), 0)), row, row, row],
        out_shape=[jax.ShapeDtypeStruct((t, 3 * hw), F32), jax.ShapeDtypeStruct((t, hw), BF16),
                   jax.ShapeDtypeStruct((t, LANES), F32)] + [jax.ShapeDtypeStruct((1, LANES), F32)] * 3,
        scratch_shapes=[pltpu.VMEM((n_heads, HEAD, HEAD), F32)], compiler_params=_params(("arbitrary",)),
    )(qkv, qkv, qkv, proj, proj, alog_row, dtb_row, nw, saved, d_ocat)


def _pad_lanes(v, n):
    v = v.reshape(1, -1)
    return jnp.pad(v, ((0, 0), (0, n - v.shape[1])))


def _pack_rows(vecs):
    flat = jnp.concatenate([v.reshape(-1) for v in vecs])
    offs, o = [], 0
    for v in vecs:
        offs.append((o, v.size))
        o += v.size
    per_row = -(-o // (SUBLANES * LANES)) * LANES
    flat = jnp.pad(flat, (0, SUBLANES * per_row - o))
    return flat.reshape(SUBLANES, per_row), offs


def _unpack(gathered, offs):
    per_dev = gathered.reshape(N_DEV, -1)
    return [per_dev[:, o:o + n] for o, n in offs]


def _sum_devices(part):
    acc = part[0]
    for i in range(1, N_DEV):
        acc = acc + part[i]
    return acc


def kernel(x, c, w_ada, b_ada, pre_mix_norm, post_mix_norm, pre_ffn_norm, post_ffn_norm, w_in, hg_lb_logits, hg_norm, gdn_conv_w, gdn_a_log, gdn_dt_bias, gdn_norm, w_out, w_ff1, w_ff2, loss_target, m_w_ada, m_b_ada, m_pre_mix_norm, m_post_mix_norm, m_pre_ffn_norm, m_post_ffn_norm, m_w_in, m_hg_lb_logits, m_hg_norm, m_gdn_conv_w, m_gdn_a_log, m_gdn_dt_bias, m_gdn_norm, m_w_out, m_w_ff1, m_w_ff2, v_w_ada, v_b_ada, v_pre_mix_norm, v_post_mix_norm, v_pre_ffn_norm, v_post_ffn_norm, v_w_in, v_hg_lb_logits, v_hg_norm, v_gdn_conv_w, v_gdn_a_log, v_gdn_dt_bias, v_gdn_norm, v_w_out, v_w_ff1, v_w_ff2):
    assert x.shape[0] == 1 and w_ada.shape[0] == 1 and hg_lb_logits.shape[0] == 2
    t, d = x.shape[1], x.shape[2]
    n_heads = (d // 2) // HEAD
    hw = n_heads * HEAD
    in_cols = 8 * hw + 2 * n_heads
    np_cols = 8 * hw + LANES
    d_ff = w_ff1.shape[2] * N_CHIP
    na = w_ada.shape[2]
    ax, ay, ac = lax.axis_index("x"), lax.axis_index("y"), lax.axis_index("c")
    chip = 2 * ax + ay
    dev = 4 * ax + 2 * ay + ac

    x2d, tgt = x[0], loss_target[0]

    pack1, offs1 = _pack_rows([c[0], gdn_conv_w[0]])
    c_all, convw_all = _unpack(_gather8(pack1, "gather_cond"), offs1)
    conv_sh = gdn_conv_w.shape[2]
    conv_w = jnp.concatenate([convw_all[2 * j].reshape(CONV_K, conv_sh) for j in range(N_CHIP)], axis=1)

    b_s = lax.dynamic_slice(b_ada, (0, chip * na), (1, na))
    mod_part = _mod_part(c_all, w_ada[0], b_s, "mod_part")
    pack2, offs2 = _pack_rows([mod_part])
    (mod_parts,) = _unpack(_gather8(pack2, "gather_mod"), offs2)
    mod_all = jnp.concatenate([mod_parts[2 * j].reshape(N_DEV, na) for j in range(N_CHIP)], axis=1)
    mod = lax.dynamic_slice(mod_all, (dev, 0), (1, N_MOD * d))
    sh_m, sc_m, gt_m, sh_f, sc_f, gt_f = [mod[:, i * d:(i + 1) * d] for i in range(N_MOD)]

    g_in, g_out, g_ff1, g_ff2 = _gather_weights(
        [w_in[0].astype(BF16), w_out[0].astype(BF16), w_ff1[0].astype(BF16), w_ff2[0].astype(BF16)], "gather_weights")
    w_in_f = jnp.pad(jnp.transpose(g_in, (1, 0, 2)).reshape(d, in_cols), ((0, 0), (0, np_cols - in_cols)))
    w_out_f = g_out.reshape(d, d)
    w_ff2_f = g_ff2.reshape(d_ff, d)

    h1 = _norm_mod(x2d, pre_mix_norm, sc_m, sh_m, "norm_mod_mix")
    proj = _matmul(h1, w_in_f, "nn", F32, "mm_in", tn=640)
    l0, l1 = hg_lb_logits[0].reshape(n_heads, 1, HEAD), hg_lb_logits[1].reshape(n_heads, 1, HEAD)
    o_hg, hg_saved = _hgrn2_fwd(proj, l0, l1, hg_norm, n_heads, "hgrn2_fwd")
    qkv = _conv_fwd(proj, conv_w, 4 * n_heads, "conv_fwd")
    alog_row, dtb_row = _pad_lanes(gdn_a_log, LANES), _pad_lanes(gdn_dt_bias, LANES)
    o_gdn, gdn_saved = _gdn_fwd(qkv, proj, alog_row, dtb_row, gdn_norm, n_heads, "gdn_fwd")
    o_cat = jnp.concatenate([o_hg, o_gdn], axis=1)
    y1 = _matmul(o_cat, w_out_f, "nn", F32, "mm_out")
    x_mid = _resid(x2d, y1, post_mix_norm, gt_m, "resid_mix")

    h2 = _norm_mod(x_mid, pre_ffn_norm, sc_f, sh_f, "norm_mod_ffn")
    relu_a1, r1 = _matmul(h2, g_ff1, "nn", BF16, "mm_ff1", relu2=True, b_split=True)
    y2 = _matmul(r1, w_ff2_f, "nn", F32, "mm_ff2")
    d_out, loss_row = _loss_head(x_mid, y2, post_ffn_norm, gt_f, tgt, "loss_head")

    dy2, d_gt_f, d_post_ffn = _resid_bwd(d_out, y2, post_ffn_norm, gt_f, "resid_ffn_bwd")
    gw_ff2 = _matmul(r1, dy2, "tn", F32, "mm_ff2_dw")
    da1 = _matmul(dy2, w_ff2_f, "nt", BF16, "mm_ff2_dx", times=relu_a1)
    gw_ff1 = _matmul(h2, da1, "tn", F32, "mm_ff1_dw", out_split=True)
    dh2 = _matmul(da1, g_ff1, "nt", BF16, "mm_ff1_dx", b_split=True)
    d_mid, d_pre_ffn, d_sc_f, d_sh_f = _norm_mod_bwd(x_mid, pre_ffn_norm, sc_f, dh2, d_out, "norm_mod_ffn_bwd")

    dy1, d_gt_m, d_post_mix = _resid_bwd(d_mid, y1, post_mix_norm, gt_m, "resid_mix_bwd")
    gw_out = _matmul(o_cat, dy1, "tn", F32, "mm_out_dw")
    d_ocat = _matmul(dy1, w_out_f, "nt", F32, "mm_out_dx")
    dp_hg, dl0, dl1, d_hg_norm = _hgrn2_bwd(proj, l0, l1, hg_norm, hg_saved, d_ocat, n_heads, "hgrn2_bwd")
    dqkv, dg_g, dab, d_alog, d_dtb, d_gdn_norm = _gdn_bwd(
        qkv, proj, alog_row, dtb_row, gdn_norm, gdn_saved, d_ocat, n_heads, "gdn_bwd")
    du, d_conv_w = _conv_bwd(proj, conv_w, dqkv, 4 * n_heads, "conv_bwd")
    dproj = jnp.concatenate([dp_hg, du, dg_g, dab.astype(BF16)], axis=1)
    gw_in = _matmul(h1, dproj, "tn", F32, "mm_in_dw", tn=640)
    dh1 = _matmul(dproj, w_in_f, "nt", BF16, "mm_in_dx", tk=1664)
    grad_x, d_pre_mix, d_sc_m, d_sh_m = _norm_mod_bwd(x2d, pre_mix_norm, sc_m, dh1, d_mid, "norm_mod_mix_bwd")

    d_mod = jnp.concatenate([d_sh_m, d_sc_m, d_gt_m, d_sh_f, d_sc_f, d_gt_f], axis=1)
    d_lb_logits = jnp.stack([dl0.reshape(n_heads, HEAD), dl1.reshape(n_heads, HEAD)])
    pack3, offs3 = _pack_rows([loss_row[0, :1], d_pre_mix, d_post_mix, d_pre_ffn, d_post_ffn, d_lb_logits, d_hg_norm,
                               d_conv_w, d_alog[0, :n_heads], d_dtb[0, :n_heads], d_gdn_norm, d_mod])
    parts = _unpack(_gather8(pack3, "gather_vec_grads"), offs3)
    sums = [_sum_devices(p) for p in parts[:-1]]
    loss = sums[0][0]
    dmod_all = parts[-1]
    g_b_ada = _sum_devices(dmod_all).reshape(1, N_MOD * d)
    g_conv_full = sums[7].reshape(CONV_K, N_CHIP * conv_sh)
    g_conv = lax.dynamic_slice(g_conv_full, (0, chip * conv_sh), (CONV_K, conv_sh))
    gw_ada = _wada_grad(c_all, lax.dynamic_slice(dmod_all, (0, chip * na), (N_DEV, na)), "wada_grad")

    in_sh = in_cols // N_CHIP
    ff_sh = d_ff // N_CHIP
    by_chip = [jnp.transpose(gw_in[:, :in_cols].reshape(d, N_CHIP, in_sh), (1, 0, 2)),
               gw_out.reshape(N_CHIP, d // N_CHIP, d),
               gw_ff1,
               gw_ff2.reshape(N_CHIP, ff_sh, d)]

    my_half = jnp.reshape(ac, (1,)).astype(jnp.int32)
    to_sib = [_row_half_to_bf16(a, 1 - my_half, None, f"sibling_half_{i}") for i, a in enumerate(by_chip)]
    from_sib = _sibling_exchange(to_sib, "sibling_partials")
    chip_part = [_row_half_to_bf16(a, my_half, s, f"add_halves_{i}") for i, (a, s) in enumerate(zip(by_chip, from_sib))]
    recv = _chip_exchange(chip_part, False, "scatter_grads")
    mine = [_sum_chips(rv, f"sum_chips_{i}") for i, rv in enumerate(recv)]
    theirs = _sibling_exchange(mine, "sibling_grads")

    big = {}
    for i, (nm, w_, m_, v_) in enumerate([("w_in", w_in, m_w_in, v_w_in), ("w_out", w_out, m_w_out, v_w_out),
                                          ("w_ff1", w_ff1, m_w_ff1, v_w_ff1), ("w_ff2", w_ff2, m_w_ff2, v_w_ff2)]):
        big[nm] = [o[None] for o in _adamw(w_[0], [mine[i], theirs[i]], m_[0], v_[0], f"adamw_{nm}", by_core=True)]
    big["w_ada"] = [o[None] for o in _adamw(w_ada[0], [gw_ada], m_w_ada[0], v_w_ada[0], "adamw_w_ada")]

    small_names = ["b_ada", "pre_mix_norm", "post_mix_norm", "pre_ffn_norm", "post_ffn_norm", "hg_lb_logits", "hg_norm",
                   "gdn_conv_w", "gdn_a_log", "gdn_dt_bias", "gdn_norm"]
    small_w = [b_ada, pre_mix_norm, post_mix_norm, pre_ffn_norm, post_ffn_norm, hg_lb_logits, hg_norm, gdn_conv_w,
               gdn_a_log, gdn_dt_bias, gdn_norm]
    small_m = [m_b_ada, m_pre_mix_norm, m_post_mix_norm, m_pre_ffn_norm, m_post_ffn_norm, m_hg_lb_logits, m_hg_norm,
               m_gdn_conv_w, m_gdn_a_log, m_gdn_dt_bias, m_gdn_norm]
    small_v = [v_b_ada, v_pre_mix_norm, v_post_mix_norm, v_pre_ffn_norm, v_post_ffn_norm, v_hg_lb_logits, v_hg_norm,
               v_gdn_conv_w, v_gdn_a_log, v_gdn_dt_bias, v_gdn_norm]
    small_g = [g_b_ada, sums[1], sums[2], sums[3], sums[4], sums[5], sums[6], g_conv, sums[8], sums[9], sums[10]]
    pw, offs_s = _pack_rows(small_w)
    pg, _ = _pack_rows(small_g)
    pm, _ = _pack_rows(small_m)
    pv, _ = _pack_rows(small_v)
    packed = _adamw(pw, [pg], pm, pv, "adamw_vectors")
    small = {}
    for nm, w_, (o, n) in zip(small_names, small_w, offs_s):
        small[nm] = [p.reshape(-1)[o:o + n].reshape(w_.shape) for p in packed]

    order = ["w_ada", "b_ada", "pre_mix_norm", "post_mix_norm", "pre_ffn_norm", "post_ffn_norm", "w_in", "hg_lb_logits",
             "hg_norm", "gdn_conv_w", "gdn_a_log", "gdn_dt_bias", "gdn_norm", "w_out", "w_ff1", "w_ff2"]
    res = {**big, **small}
    outs = [loss, grad_x[None]]
    for k in range(4):
        outs += [res[nm][k] for nm in order]
    return tuple(outs)
```

```python
import functools
import math

import jax
import jax.numpy as jnp
from jax import lax
from jax.experimental import pallas as pl
from jax.experimental.pallas import tpu as pltpu
from jax.experimental.pallas import tpu_sc as plsc

F32 = jnp.float32
BF16 = jnp.bfloat16
HI = lax.Precision.HIGHEST
MESH = pl.DeviceIdType.MESH

LANES = 128
SUBLANES = 8
VMEM_LIMIT = 48 * 1024 * 1024
EPS = 1e-6
HEAD = 128
CONV_K = 4
GDN_CHUNK = 64
GDN_INV_BLOCK = 16
HG_SUB = 16
HG_BLOCK = 128
HEAD_GROUP = 8
N_MOD = 6
N_DEV = 8
N_CHIP = 4

ADAM_LR = 0.001
ADAM_B1 = 0.9
ADAM_B2 = 0.999
ADAM_EPS = 1e-08
ADAM_WD = 0.01
ADAM_STEP = 10

NT_DIMS = (((1,), (1,)), ((), ()))
TN_DIMS = (((0,), (0,)), ((), ()))


def _tile(dim, target, align):
    if dim <= target:
        return dim
    best = dim
    t = align
    while t <= target:
        if dim % t == 0:
            best = t
        t += align
    return best


def _elementwise_tiles(r, c):
    tc = _tile(c, 1024, LANES)
    tr = _tile(r, max(16, (256 * 1024) // tc // 16 * 16), 16)
    return tr, tc


def _params(sem):
    return pltpu.CompilerParams(dimension_semantics=sem, vmem_limit_bytes=VMEM_LIMIT)


def _silu(x):
    return x * jax.nn.sigmoid(x)


def _softplus(x):
    pos = x > 0
    return jnp.where(pos, x, 0.0) + jnp.log(1.0 + jnp.exp(jnp.where(pos, -x, x)))


def _rms_scale(x):
    return lax.rsqrt(jnp.mean(x * x, axis=-1, keepdims=True) + EPS)


def _gather8(x_shard, name):
    m_per, n = x_shard.shape
    assert m_per % SUBLANES == 0 and n % LANES == 0

    def body(x_ref, out_ref, send_sems, recv_sems, local_sem):
        x, y, c = lax.axis_index("x"), lax.axis_index("y"), lax.axis_index("c")
        me, sibling = (x, y, c), (x, y, 1 - c)
        chips = [(1 - x, y), (x, 1 - y), (1 - x, 1 - y)]

        def rows(px, py, pc):
            return out_ref.at[pl.ds((4 * px + 2 * py + pc) * m_per, m_per), :]

        def copy(k, block, to, src=None):
            return pltpu.make_async_remote_copy(
                src_ref=rows(*block) if src is None else src, dst_ref=rows(*block),
                send_sem=send_sems.at[k], recv_sem=recv_sems.at[k], device_id=to, device_id_type=MESH)

        mine = pltpu.make_async_copy(x_ref, rows(*me), local_sem)
        mine.start()
        first = [copy(0, me, sibling, src=x_ref)]
        first += [copy(1 + j, me, (*chip, c), src=x_ref) for j, chip in enumerate(chips)]
        for cp in first:
            cp.start()
        passed = [copy(4 + j, (*chip, c), sibling) for j, chip in enumerate(chips)]
        for j, chip in enumerate(chips):
            copy(1 + j, (*chip, c), me).wait_recv()
            passed[j].start()
        copy(0, sibling, me).wait_recv()
        for j, chip in enumerate(chips):
            copy(4 + j, (*chip, 1 - c), me).wait_recv()
        for cp in first + passed:
            cp.wait_send()
        mine.wait()

    return pl.pallas_call(
        body, name=name,
        out_shape=jax.ShapeDtypeStruct((N_DEV * m_per, n), x_shard.dtype),
        in_specs=[pl.BlockSpec(memory_space=pltpu.VMEM)],
        out_specs=pl.BlockSpec(memory_space=pltpu.VMEM),
        scratch_shapes=[pltpu.SemaphoreType.DMA((7,)), pltpu.SemaphoreType.DMA((7,)), pltpu.SemaphoreType.DMA],
        compiler_params=pltpu.CompilerParams(vmem_limit_bytes=VMEM_LIMIT),
    )(x_shard)


def _gather_weights(arrs, name):
    n = len(arrs)
    out_shapes = [jax.ShapeDtypeStruct((N_CHIP,) + a.shape, a.dtype) for a in arrs]

    def body(*refs):
        ins, outs = refs[:n], refs[n:2 * n]
        ici_send, ici_recv, d2d_send, d2d_recv = refs[2 * n:]
        x, y, c = lax.axis_index("x"), lax.axis_index("y"), lax.axis_index("c")
        me = 2 * x + y
        sibling = (x, y, 1 - c)
        peers = [(1 - x, y), (x, 1 - y), (1 - x, 1 - y)]

        def half(a, cc):
            rh = arrs[a].shape[0] // 2
            return pl.ds(pl.multiple_of(cc * rh, 16), rh)

        sent = []
        for a in range(n):
            for k, (px, py) in enumerate(peers):
                cp = pltpu.make_async_remote_copy(
                    src_ref=ins[a].at[half(a, c)], dst_ref=outs[a].at[me, half(a, c)],
                    send_sem=ici_send.at[3 * a + k], recv_sem=ici_recv.at[3 * a + k],
                    device_id=(px, py, c), device_id_type=MESH)
                cp.start()
                sent.append(cp)
        for a in range(n):
            for k, (px, py) in enumerate(peers):
                landed = outs[a].at[2 * px + py, half(a, c)]
                pltpu.make_async_remote_copy(
                    src_ref=landed, dst_ref=landed, send_sem=ici_send.at[3 * a + k], recv_sem=ici_recv.at[3 * a + k],
                    device_id=(px, py, c), device_id_type=MESH).wait_recv()
                fwd = pltpu.make_async_remote_copy(
                    src_ref=landed, dst_ref=landed, send_sem=d2d_send.at[3 * a + k], recv_sem=d2d_recv.at[3 * a + k],
                    device_id=sibling, device_id_type=MESH)
                fwd.start()
                sent.append(fwd)
        for a in range(n):
            for k, (px, py) in enumerate(peers):
                passed = outs[a].at[2 * px + py, half(a, 1 - c)]
                pltpu.make_async_remote_copy(
                    src_ref=passed, dst_ref=passed, send_sem=d2d_send.at[3 * a + k], recv_sem=d2d_recv.at[3 * a + k],
                    device_id=sibling, device_id_type=MESH).wait_recv()
        for cp in sent:
            cp.wait_send()

    hbm = pl.BlockSpec(memory_space=pltpu.HBM)
    gathered = pl.pallas_call(
        body, name=name, out_shape=out_shapes, in_specs=[hbm] * n, out_specs=[hbm] * n,
        scratch_shapes=[pltpu.SemaphoreType.DMA((3 * n,))] * 4,
    )(*arrs)
    chip = 2 * lax.axis_index("x") + lax.axis_index("y")
    return [lax.dynamic_update_slice(g, a[None], (chip, 0, 0)) for g, a in zip(gathered, arrs)]


def _chip_peers():
    x, y, c = lax.axis_index("x"), lax.axis_index("y"), lax.axis_index("c")
    return [(1 - x, y, c), (x, 1 - y, c), (1 - x, 1 - y, c)], (x, y, 1 - c)


def _handshake(peers):
    barrier = pltpu.get_barrier_semaphore()
    for peer in peers:
        pl.semaphore_signal(barrier, inc=1, device_id=peer, device_id_type=MESH)
    pl.semaphore_wait(barrier, len(peers))


def _sequencer_gather_weights(arrs, name, collective_id):
    n = len(arrs)
    out_types = [jax.ShapeDtypeStruct((N_CHIP,) + a.shape, a.dtype) for a in arrs]

    def body(*refs):
        ins, outs = refs[:n], refs[n:2 * n]
        ici_send, ici_recv, d2d_send, d2d_recv = refs[2 * n:]
        chips, sibling = _chip_peers()
        _handshake(chips + [sibling])
        c = lax.axis_index("c")
        me = 2 * lax.axis_index("x") + lax.axis_index("y")

        def half(a, cc):
            rh = arrs[a].shape[0] // 2
            return pl.ds(pl.multiple_of(cc * rh, 16), rh)

        sent = []
        for a in range(n):
            for k, peer in enumerate(chips):
                cp = pltpu.make_async_remote_copy(
                    src_ref=ins[a].at[half(a, c)], dst_ref=outs[a].at[me, half(a, c)],
                    send_sem=ici_send.at[3 * a + k], recv_sem=ici_recv.at[3 * a + k], device_id=peer, device_id_type=MESH)
                cp.start()
                sent.append(cp)
        for a in range(n):
            for k, peer in enumerate(chips):
                landed = outs[a].at[2 * peer[0] + peer[1], half(a, c)]
                pltpu.make_async_remote_copy(
                    src_ref=landed, dst_ref=landed, send_sem=ici_send.at[3 * a + k], recv_sem=ici_recv.at[3 * a + k],
                    device_id=peer, device_id_type=MESH).wait_recv()
                fwd = pltpu.make_async_remote_copy(
                    src_ref=landed, dst_ref=landed, send_sem=d2d_send.at[3 * a + k], recv_sem=d2d_recv.at[3 * a + k],
                    device_id=sibling, device_id_type=MESH)
                fwd.start()
                sent.append(fwd)
        for a in range(n):
            for k, peer in enumerate(chips):
                passed = outs[a].at[2 * peer[0] + peer[1], half(a, 1 - c)]
                pltpu.make_async_remote_copy(
                    src_ref=passed, dst_ref=passed, send_sem=d2d_send.at[3 * a + k], recv_sem=d2d_recv.at[3 * a + k],
                    device_id=sibling, device_id_type=MESH).wait_recv()
        for cp in sent:
            cp.wait_send()

    gathered = pl.kernel(
        body, out_type=out_types, mesh=plsc.ScalarSubcoreMesh(axis_name="sequencer", num_cores=1), name=name,
        scratch_types=[pltpu.SemaphoreType.DMA((3 * n,))] * 4,
        compiler_params=pltpu.CompilerParams(collective_id=collective_id),
    )(*arrs)
    chip = 2 * lax.axis_index("x") + lax.axis_index("y")
    return [lax.dynamic_update_slice(g, a[None], (chip, 0, 0)) for g, a in zip(gathered, arrs)]


def _sequencer_chip_exchange(arrs, name, collective_id):
    n = len(arrs)
    out_types = [jax.ShapeDtypeStruct(a.shape, a.dtype) for a in arrs]

    def body(*refs):
        ins, outs = refs[:n], refs[n:2 * n]
        send_sems, recv_sems = refs[2 * n:]
        chips, _ = _chip_peers()
        _handshake(chips)
        me = 2 * lax.axis_index("x") + lax.axis_index("y")
        sent = []
        for a in range(n):
            for k, peer in enumerate(chips):
                cp = pltpu.make_async_remote_copy(
                    src_ref=ins[a].at[2 * peer[0] + peer[1]], dst_ref=outs[a].at[me],
                    send_sem=send_sems.at[3 * a + k], recv_sem=recv_sems.at[3 * a + k], device_id=peer, device_id_type=MESH)
                cp.start()
                sent.append(cp)
        for a in range(n):
            for k, peer in enumerate(chips):
                landed = outs[a].at[2 * peer[0] + peer[1]]
                pltpu.make_async_remote_copy(
                    src_ref=landed, dst_ref=landed, send_sem=send_sems.at[3 * a + k], recv_sem=recv_sems.at[3 * a + k],
                    device_id=peer, device_id_type=MESH).wait_recv()
        for cp in sent:
            cp.wait_send()

    received = pl.kernel(
        body, out_type=out_types, mesh=plsc.ScalarSubcoreMesh(axis_name="sequencer", num_cores=1), name=name,
        scratch_types=[pltpu.SemaphoreType.DMA((3 * n,))] * 2,
        compiler_params=pltpu.CompilerParams(collective_id=collective_id),
    )(*arrs)
    chip = 2 * lax.axis_index("x") + lax.axis_index("y")
    return [lax.dynamic_update_slice(r, lax.dynamic_slice(a, (chip, 0, 0), (1,) + a.shape[1:]), (chip, 0, 0))
            for r, a in zip(received, arrs)]


def _sibling_exchange(arrs, name):
    n = len(arrs)

    def body(*refs):
        ins, outs = refs[:n], refs[n:2 * n]
        send_sems, recv_sems = refs[2 * n:]
        sibling = (lax.axis_index("x"), lax.axis_index("y"), 1 - lax.axis_index("c"))
        cps = []
        for a in range(n):
            cp = pltpu.make_async_remote_copy(src_ref=ins[a], dst_ref=outs[a], send_sem=send_sems.at[a],
                                              recv_sem=recv_sems.at[a], device_id=sibling, device_id_type=MESH)
            cp.start()
            cps.append(cp)
        for cp in cps:
            cp.wait_recv()
        for cp in cps:
            cp.wait_send()

    hbm = pl.BlockSpec(memory_space=pltpu.HBM)
    return pl.pallas_call(
        body, name=name, out_shape=[jax.ShapeDtypeStruct(a.shape, a.dtype) for a in arrs],
        in_specs=[hbm] * n, out_specs=[hbm] * n,
        scratch_shapes=[pltpu.SemaphoreType.DMA((n,)), pltpu.SemaphoreType.DMA((n,))],
    )(*arrs)


def _matmul(a, b, mode, out_dtype, name, tm=1024, tn=1024, tk=2048, relu2=False, times=None, b_split=False,
            out_split=False):
    b_shape = (b.shape[1], b.shape[2] * N_CHIP) if b_split else b.shape
    if mode == "nn":
        (m, k), (k2, n) = a.shape, b_shape
    elif mode == "nt":
        (m, k), (n, k2) = a.shape, b_shape
    else:
        (k, m), (k2, n) = a.shape, b_shape
    assert k == k2, (a.shape, b.shape, mode)
    n_cut = n // N_CHIP if (out_split or (b_split and mode != "nt")) else n
    k_cut = k // N_CHIP if (b_split and mode == "nt") else k
    tm, tn, tk = _tile(m, tm, LANES), _tile(n_cut, tn, LANES), _tile(k_cut, tk, LANES)
    assert n_cut % tn == 0 and k_cut % tk == 0 and m % tm == 0, (name, m, n, k, tm, tn, tk)
    nk = k // tk
    nbc, nkc = n_cut // tn, k_cut // tk
    n_in = 2 if times is None else 3
    n_out = 2 if relu2 else 1

    def product(a_ref, b_ref):
        if mode == "nn":
            return jnp.dot(a_ref[...], b_ref[...], preferred_element_type=F32)
        return lax.dot_general(a_ref[...], b_ref[...], NT_DIMS if mode == "nt" else TN_DIMS, preferred_element_type=F32)

    def finish(p, refs, o_refs):
        if relu2:
            p = jnp.maximum(p, 0.0)
            o_refs[0][...] = p.astype(o_refs[0].dtype)
            o_refs[1][...] = (p * p).astype(o_refs[1].dtype)
        elif times is not None:
            o_refs[0][...] = (2.0 * refs[2][...].astype(F32) * p).astype(o_refs[0].dtype)
        else:
            o_refs[0][...] = p.astype(o_refs[0].dtype)

    def body(*refs):
        o_refs = refs[n_in:n_in + n_out]
        if nk == 1:
            finish(product(refs[0], refs[1]), refs, o_refs)
            return
        acc_ref = refs[n_in + n_out]
        kk = pl.program_id(2)

        @pl.when(kk == 0)
        def _():
            acc_ref[...] = product(refs[0], refs[1])

        @pl.when((kk > 0) & (kk < nk - 1))
        def _():
            acc_ref[...] += product(refs[0], refs[1])

        @pl.when(kk == nk - 1)
        def _():
            finish(acc_ref[...] + product(refs[0], refs[1]), refs, o_refs)

    if mode == "tn":
        a_spec = pl.BlockSpec((tk, tm), lambda i, j, kk: (kk, i))
    else:
        a_spec = pl.BlockSpec((tm, tk), lambda i, j, kk: (i, kk))
    if mode == "nt":
        b_spec = (pl.BlockSpec((None, tn, tk), lambda i, j, kk: (kk // nkc, j, kk % nkc)) if b_split
                  else pl.BlockSpec((tn, tk), lambda i, j, kk: (j, kk)))
    else:
        b_spec = (pl.BlockSpec((None, tk, tn), lambda i, j, kk: (j // nbc, kk, j % nbc)) if b_split
                  else pl.BlockSpec((tk, tn), lambda i, j, kk: (kk, j)))
    mn_spec = pl.BlockSpec((tm, tn), lambda i, j, kk: (i, j))
    if out_split:
        o_spec = pl.BlockSpec((None, tm, tn), lambda i, j, kk: (j // nbc, i, j % nbc))
        o_shape = jax.ShapeDtypeStruct((N_CHIP, m, n_cut), out_dtype)
    else:
        o_spec, o_shape = mn_spec, jax.ShapeDtypeStruct((m, n), out_dtype)
    out = pl.pallas_call(
        body, name=name, grid=(m // tm, n // tn, nk), in_specs=[a_spec, b_spec] + [mn_spec] * (n_in - 2),
        out_specs=[o_spec] * n_out, out_shape=[o_shape] * n_out,
        scratch_shapes=[] if nk == 1 else [pltpu.VMEM((tm, tn), F32)],
        compiler_params=_params(("parallel", "parallel", "arbitrary")),
    )(*((a, b) if times is None else (a, b, times)))
    return out if relu2 else out[0]


def _mod_part(c_all, w_s, b_s, name):
    d, na = w_s.shape
    tn = _tile(na, 512, LANES)

    def body(c_ref, w_ref, b_ref, o_ref):
        ca = _silu(c_ref[...]).astype(BF16)
        o_ref[...] = jnp.dot(ca, w_ref[...].astype(BF16), preferred_element_type=F32) + b_ref[...]

    return pl.pallas_call(
        body, name=name, grid=(na // tn,),
        in_specs=[pl.BlockSpec((N_DEV, d), lambda j: (0, 0)), pl.BlockSpec((d, tn), lambda j: (0, j)),
                  pl.BlockSpec((1, tn), lambda j: (0, j))],
        out_specs=pl.BlockSpec((N_DEV, tn), lambda j: (0, j)),
        out_shape=jax.ShapeDtypeStruct((N_DEV, na), F32), compiler_params=_params(("parallel",)),
    )(c_all, w_s, b_s)


def _wada_grad(c_all, dmod_s, name):
    d = c_all.shape[1]
    na = dmod_s.shape[1]
    td, tn = _tile(d, 512, LANES), _tile(na, 512, LANES)

    def body(c_ref, g_ref, o_ref):
        o_ref[...] = lax.dot_general(_silu(c_ref[...]), g_ref[...], TN_DIMS, precision=HI, preferred_element_type=F32)

    return pl.pallas_call(
        body, name=name, grid=(d // td, na // tn),
        in_specs=[pl.BlockSpec((N_DEV, td), lambda i, j: (0, i)), pl.BlockSpec((N_DEV, tn), lambda i, j: (0, j))],
        out_specs=pl.BlockSpec((td, tn), lambda i, j: (i, j)),
        out_shape=jax.ShapeDtypeStruct((d, na), F32), compiler_params=_params(("parallel", "parallel")),
    )(c_all, dmod_s)


def _row_specs(tb, d, n_full, n_vec):
    full = pl.BlockSpec((tb, d), lambda i: (i, 0))
    vec = pl.BlockSpec((1, d), lambda i: (0, 0))
    return [full] * n_full + [vec] * n_vec


def _norm_mod(x, w, sc, sh, name):
    t, d = x.shape
    tb = _tile(t, 256, SUBLANES)

    def body(x_ref, w_ref, sc_ref, sh_ref, o_ref):
        xv = x_ref[...]
        o_ref[...] = (xv * _rms_scale(xv) * w_ref[...] * (1.0 + sc_ref[...]) + sh_ref[...]).astype(o_ref.dtype)

    return pl.pallas_call(
        body, name=name, grid=(t // tb,), in_specs=_row_specs(tb, d, 1, 3),
        out_specs=pl.BlockSpec((tb, d), lambda i: (i, 0)), out_shape=jax.ShapeDtypeStruct((t, d), BF16),
        compiler_params=_params(("parallel",)),
    )(x, w, sc, sh)


def _norm_mod_bwd(x, w, sc, dh, dres, name):
    t, d = x.shape
    tb = _tile(t, 256, SUBLANES)

    def body(x_ref, w_ref, sc_ref, dh_ref, dres_ref, dx_ref, dw_ref, dsc_ref, dsh_ref):
        @pl.when(pl.program_id(0) == 0)
        def _():
            dw_ref[...] = jnp.zeros_like(dw_ref)
            dsc_ref[...] = jnp.zeros_like(dsc_ref)
            dsh_ref[...] = jnp.zeros_like(dsh_ref)

        xv = x_ref[...]
        r = _rms_scale(xv)
        xn = xv * r
        g = dh_ref[...].astype(F32)
        wv, one_sc = w_ref[...], 1.0 + sc_ref[...]
        gxn = g * xn
        dsh_ref[...] += jnp.sum(g, axis=0, keepdims=True)
        dsc_ref[...] += jnp.sum(gxn, axis=0, keepdims=True) * wv
        dw_ref[...] += jnp.sum(gxn, axis=0, keepdims=True) * one_sc
        dxn = g * (wv * one_sc)
        dx_ref[...] = dres_ref[...] + r * (dxn - xn * jnp.mean(dxn * xn, axis=-1, keepdims=True))

    vec_out = pl.BlockSpec((1, d), lambda i: (0, 0))
    return pl.pallas_call(
        body, name=name, grid=(t // tb,),
        in_specs=[pl.BlockSpec((tb, d), lambda i: (i, 0)), pl.BlockSpec((1, d), lambda i: (0, 0)),
                  pl.BlockSpec((1, d), lambda i: (0, 0)), pl.BlockSpec((tb, d), lambda i: (i, 0)),
                  pl.BlockSpec((tb, d), lambda i: (i, 0))],
        out_specs=[pl.BlockSpec((tb, d), lambda i: (i, 0)), vec_out, vec_out, vec_out],
        out_shape=[jax.ShapeDtypeStruct((t, d), F32)] + [jax.ShapeDtypeStruct((1, d), F32)] * 3,
        compiler_params=_params(("arbitrary",)),
    )(x, w, sc, dh, dres)


def _resid(x, y, w, gt, name):
    t, d = x.shape
    tb = _tile(t, 256, SUBLANES)

    def body(x_ref, y_ref, w_ref, gt_ref, o_ref):
        yv = y_ref[...]
        o_ref[...] = x_ref[...] + gt_ref[...] * (yv * _rms_scale(yv) * w_ref[...])

    return pl.pallas_call(
        body, name=name, grid=(t // tb,), in_specs=_row_specs(tb, d, 2, 2),
        out_specs=pl.BlockSpec((tb, d), lambda i: (i, 0)), out_shape=jax.ShapeDtypeStruct((t, d), F32),
        compiler_params=_params(("parallel",)),
    )(x, y, w, gt)


def _loss_head(x2, y2, w, gt, target, name):
    t, d = x2.shape
    tb = _tile(t, 256, SUBLANES)

    def body(x_ref, y_ref, tg_ref, w_ref, gt_ref, do_ref, loss_ref):
        @pl.when(pl.program_id(0) == 0)
        def _():
            loss_ref[...] = jnp.zeros_like(loss_ref)

        yv = y_ref[...]
        out = x_ref[...] + gt_ref[...] * (yv * _rms_scale(yv) * w_ref[...])
        err = out - tg_ref[...]
        do_ref[...] = err * (1.0 / d)
        per_tok = jnp.mean(err * err, axis=-1, keepdims=True)
        loss_ref[...] += 0.5 * jnp.sum(per_tok, axis=0, keepdims=True)

    return pl.pallas_call(
        body, name=name, grid=(t // tb,), in_specs=_row_specs(tb, d, 3, 2),
        out_specs=[pl.BlockSpec((tb, d), lambda i: (i, 0)), pl.BlockSpec((1, LANES), lambda i: (0, 0))],
        out_shape=[jax.ShapeDtypeStruct((t, d), F32), jax.ShapeDtypeStruct((1, LANES), F32)],
        compiler_params=_params(("arbitrary",)),
    )(x2, y2, target, w, gt)


def _resid_bwd(dout, y, w, gt, name):
    t, d = y.shape
    tb = _tile(t, 256, SUBLANES)

    def body(do_ref, y_ref, w_ref, gt_ref, dy_ref, dgt_ref, dw_ref):
        @pl.when(pl.program_id(0) == 0)
        def _():
            dgt_ref[...] = jnp.zeros_like(dgt_ref)
            dw_ref[...] = jnp.zeros_like(dw_ref)

        yv, g = y_ref[...], do_ref[...]
        r = _rms_scale(yv)
        yn = yv * r
        wv, gtv = w_ref[...], gt_ref[...]
        gyn = jnp.sum(g * yn, axis=0, keepdims=True)
        dgt_ref[...] += gyn * wv
        dw_ref[...] += gyn * gtv
        dyn = g * (gtv * wv)
        dy_ref[...] = (r * (dyn - yn * jnp.mean(dyn * yn, axis=-1, keepdims=True))).astype(dy_ref.dtype)

    vec_out = pl.BlockSpec((1, d), lambda i: (0, 0))
    return pl.pallas_call(
        body, name=name, grid=(t // tb,), in_specs=_row_specs(tb, d, 2, 2),
        out_specs=[pl.BlockSpec((tb, d), lambda i: (i, 0)), vec_out, vec_out],
        out_shape=[jax.ShapeDtypeStruct((t, d), BF16)] + [jax.ShapeDtypeStruct((1, d), F32)] * 2,
        compiler_params=_params(("arbitrary",)),
    )(dout, y, w, gt)


def _row_half_to_bf16(full, which, sib, name):
    n, r2, c = full.shape
    r = r2 // 2
    tr, tc = _elementwise_tiles(r, c)
    nbh = r // tr

    def body(which_ref, a_ref, *rest):
        if sib is None:
            rest[0][...] = a_ref[...].astype(BF16)
        else:
            rest[1][...] = (a_ref[...] + rest[0][...].astype(F32)).astype(BF16)

    half_spec = pl.BlockSpec((1, tr, tc), lambda j, i, k, which_ref: (j, which_ref[0] * nbh + i, k))
    spec = pl.BlockSpec((1, tr, tc), lambda j, i, k, which_ref: (j, i, k))
    grid_spec = pltpu.PrefetchScalarGridSpec(
        num_scalar_prefetch=1, grid=(n, nbh, c // tc), in_specs=[half_spec] + ([] if sib is None else [spec]), out_specs=spec)
    return pl.pallas_call(
        body, name=name, grid_spec=grid_spec, out_shape=jax.ShapeDtypeStruct((n, r, c), BF16),
        compiler_params=_params(("parallel", "parallel", "parallel")),
    )(which, full, *([] if sib is None else [sib]))


def _sum_chips(recv, name):
    _, r, c = recv.shape
    tr, tc = _elementwise_tiles(r, c)

    def body(x_ref, o_ref):
        acc = x_ref[0].astype(F32)
        for j in range(1, N_CHIP):
            acc = acc + x_ref[j].astype(F32)
        o_ref[...] = acc

    return pl.pallas_call(
        body, name=name, grid=(r // tr, c // tc), in_specs=[pl.BlockSpec((N_CHIP, tr, tc), lambda i, j: (0, i, j))],
        out_specs=pl.BlockSpec((tr, tc), lambda i, j: (i, j)), out_shape=jax.ShapeDtypeStruct((r, c), F32),
        compiler_params=_params(("parallel", "parallel")),
    )(recv)


def _adamw(w, g_parts, m, v, name, by_core=False):
    r, c = w.shape
    tr, tc = _elementwise_tiles(r // 2 if by_core else r, c)
    n_g = len(g_parts)
    nbh = (r // 2) // tr
    c1 = 1.0 / (1.0 - ADAM_B1 ** ADAM_STEP)
    c2 = 1.0 / (1.0 - ADAM_B2 ** ADAM_STEP)

    def body(*refs):
        w_ref, g_refs, m_ref, v_ref = refs[0], refs[1:1 + n_g], refs[1 + n_g], refs[2 + n_g]
        g_out, d_out, m_out, v_out = refs[3 + n_g:]
        if by_core:
            in_my_half = (pl.program_id(0) // nbh) == lax.axis_index("c")
            g = jnp.where(in_my_half, g_refs[0][...], g_refs[1][...])
        else:
            g = g_refs[0][...]
        mn = ADAM_B1 * m_ref[...] + (1.0 - ADAM_B1) * g
        vn = ADAM_B2 * v_ref[...] + (1.0 - ADAM_B2) * (g * g)
        g_out[...] = g
        m_out[...] = mn
        v_out[...] = vn
        d_out[...] = -ADAM_LR * ((mn * c1) / (jnp.sqrt(vn * c2) + ADAM_EPS) + ADAM_WD * w_ref[...])

    spec = pl.BlockSpec((tr, tc), lambda i, j: (i, j))
    g_spec = pl.BlockSpec((tr, tc), lambda i, j: (i % nbh, j)) if by_core else spec
    return pl.pallas_call(
        body, name=name, grid=(r // tr, c // tc), in_specs=[spec] + [g_spec] * n_g + [spec] * 2, out_specs=[spec] * 4,
        out_shape=[jax.ShapeDtypeStruct((r, c), F32)] * 4, compiler_params=_params(("parallel", "parallel")),
    )(w, *g_parts, m, v)


def _conv_taps(u, t):
    rows = lax.broadcasted_iota(jnp.int32, u.shape, 0)
    return [u] + [jnp.where(rows >= dd, pltpu.roll(u, dd, 0), 0.0) for dd in range(1, CONV_K)]


def _conv_fwd(proj, conv_w, col0, name):
    t = proj.shape[0]
    ch = conv_w.shape[1]

    def body(u_ref, w_ref, o_ref):
        taps = _conv_taps(u_ref[...], t)
        wv = w_ref[...]
        y = taps[0] * wv[CONV_K - 1:CONV_K]
        for dd in range(1, CONV_K):
            y = y + taps[dd] * wv[CONV_K - 1 - dd:CONV_K - dd]
        o_ref[...] = _silu(y)

    return pl.pallas_call(
        body, name=name, grid=(ch // LANES,),
        in_specs=[pl.BlockSpec((t, LANES), lambda j: (0, col0 + j)), pl.BlockSpec((CONV_K, LANES), lambda j: (0, j))],
        out_specs=pl.BlockSpec((t, LANES), lambda j: (0, j)), out_shape=jax.ShapeDtypeStruct((t, ch), F32),
        compiler_params=_params(("parallel",)),
    )(proj, conv_w)


def _conv_bwd(proj, conv_w, ds, col0, name):
    t = proj.shape[0]
    ch = conv_w.shape[1]

    def body(u_ref, w_ref, ds_ref, du_ref, dw_ref):
        u = u_ref[...]
        taps = _conv_taps(u, t)
        wv = w_ref[...]
        y = taps[0] * wv[CONV_K - 1:CONV_K]
        for dd in range(1, CONV_K):
            y = y + taps[dd] * wv[CONV_K - 1 - dd:CONV_K - dd]
        sg = jax.nn.sigmoid(y)
        dy = ds_ref[...] * (sg * (1.0 + y * (1.0 - sg)))
        rows = lax.broadcasted_iota(jnp.int32, u.shape, 0)
        du = dy * wv[CONV_K - 1:CONV_K]
        for dd in range(1, CONV_K):
            ahead = jnp.where(rows < t - dd, pltpu.roll(dy, t - dd, 0), 0.0)
            du = du + ahead * wv[CONV_K - 1 - dd:CONV_K - dd]
        du_ref[...] = du.astype(du_ref.dtype)
        dws = [jnp.sum(dy * taps[CONV_K - 1 - j], axis=0, keepdims=True) for j in range(CONV_K)]
        dw_ref[...] = jnp.concatenate(dws, axis=0)

    return pl.pallas_call(
        body, name=name, grid=(ch // LANES,),
        in_specs=[pl.BlockSpec((t, LANES), lambda j: (0, col0 + j)), pl.BlockSpec((CONV_K, LANES), lambda j: (0, j)),
                  pl.BlockSpec((t, LANES), lambda j: (0, j))],
        out_specs=[pl.BlockSpec((t, LANES), lambda j: (0, j)), pl.BlockSpec((CONV_K, LANES), lambda j: (0, j))],
        out_shape=[jax.ShapeDtypeStruct((t, ch), BF16), jax.ShapeDtypeStruct((CONV_K, ch), F32)],
        compiler_params=_params(("parallel",)),
    )(proj, conv_w, ds)


def _hg_block(st, q, fl, vi, g, l0, l1, nw):
    hs = range(len(st))
    tb = q[0].shape[0]
    ln = HG_SUB
    lb = [jax.nn.sigmoid(l0[h] - l1[h]) for h in hs]
    rows = lax.broadcasted_iota(jnp.int32, (ln, HEAD), 0)
    tri = (lax.broadcasted_iota(jnp.int32, (ln, ln), 0) >= lax.broadcasted_iota(jnp.int32, (ln, ln), 1)).astype(F32)
    st = list(st)
    outs = [[] for _ in hs]
    for i in range(tb // ln):
        sl = slice(i * ln, (i + 1) * ln)
        qs, vs = [q[h][sl] for h in hs], [vi[h][sl] for h in hs]
        f = [lb[h] + (1.0 - lb[h]) * jax.nn.sigmoid(fl[h][sl]) for h in hs]
        k = [1.0 - f[h] for h in hs]
        b = [jnp.dot(tri, jnp.log(f[h]), precision=HI, preferred_element_type=F32) for h in hs]
        o = [lax.dot_general((qs[h] * jnp.exp(b[h])).astype(BF16), st[h].astype(BF16), NT_DIMS, preferred_element_type=F32)
             for h in hs]
        for s in range(ln):
            e = [jnp.exp(jnp.where(rows >= s, b[h] - b[h][s:s + 1], -1e30)) for h in hs]
            a = [jnp.sum(qs[h] * e[h] * k[h][s:s + 1], axis=-1, keepdims=True) for h in hs]
            o = [o[h] + a[h] * vs[h][s:s + 1] for h in hs]
        kt = [k[h] * jnp.exp(b[h][ln - 1:ln] - b[h]) for h in hs]
        upd = [lax.dot_general(vs[h].astype(BF16), kt[h].astype(BF16), TN_DIMS, preferred_element_type=F32) for h in hs]
        st = [st[h] * jnp.exp(b[h][ln - 1:ln]) + upd[h] for h in hs]
        for h in hs:
            outs[h].append(o[h])
    o = [jnp.concatenate(outs[h], axis=0) for h in hs]
    out = [o[h] * _rms_scale(o[h]) * nw * _silu(g[h]) for h in hs]
    return st, out


def _head_cols(h):
    return slice(h * HEAD, (h + 1) * HEAD)


def _head_groups(n_heads):
    g = min(HEAD_GROUP, n_heads)
    return [list(range(i, min(i + g, n_heads))) for i in range(0, n_heads, g)]


def _hg_in_specs(n_heads, tb, time_index):
    hw = n_heads * HEAD
    cols = [pl.BlockSpec((tb, hw), functools.partial(lambda part, j: (time_index(j), part), part)) for part in range(4)]
    head_rows = pl.BlockSpec((n_heads, 1, HEAD), lambda j: (0, 0, 0))
    return cols + [head_rows, head_rows, pl.BlockSpec((1, HEAD), lambda j: (0, 0))]


def _hgrn2_fwd(proj, l0, l1, nw, n_heads, name):
    t = proj.shape[0]
    hw = n_heads * HEAD
    tb = _tile(t, HG_BLOCK, HG_SUB)
    nb = t // tb

    def body(q_ref, f_ref, i_ref, g_ref, l0_ref, l1_ref, nw_ref, o_ref, save_ref, st_ref):
        @pl.when(pl.program_id(0) == 0)
        def _():
            st_ref[...] = jnp.zeros_like(st_ref)

        for hs in _head_groups(n_heads):
            st = [st_ref[h] for h in hs]
            for h, s in zip(hs, st):
                save_ref[h] = s
            st, out = _hg_block(st, *[[r[:, _head_cols(h)] for h in hs] for r in (q_ref, f_ref, i_ref, g_ref)],
                                [l0_ref[h] for h in hs], [l1_ref[h] for h in hs], nw_ref[...])
            for h, s, o in zip(hs, st, out):
                st_ref[h] = s
                o_ref[:, _head_cols(h)] = o.astype(o_ref.dtype)

    return pl.pallas_call(
        body, name=name, grid=(nb,), in_specs=_hg_in_specs(n_heads, tb, lambda j: j),
        out_specs=[pl.BlockSpec((tb, hw), lambda j: (j, 0)),
                   pl.BlockSpec((None, n_heads, HEAD, HEAD), lambda j: (j, 0, 0, 0))],
        out_shape=[jax.ShapeDtypeStruct((t, hw), BF16), jax.ShapeDtypeStruct((nb, n_heads, HEAD, HEAD), F32)],
        scratch_shapes=[pltpu.VMEM((n_heads, HEAD, HEAD), F32)], compiler_params=_params(("arbitrary",)),
    )(proj, proj, proj, proj, l0, l1, nw)


def _hgrn2_bwd(proj, l0, l1, nw, saved, d_ocat, n_heads, name):
    t = proj.shape[0]
    tb = _tile(t, HG_BLOCK, HG_SUB)
    nb = t // tb
    rev = lambda j: nb - 1 - j

    hw = n_heads * HEAD

    def body(q_ref, f_ref, i_ref, g_ref, l0_ref, l1_ref, nw_ref, save_ref, do_ref,
             dp_ref, dl0_ref, dl1_ref, dnw_ref, dst_ref):
        @pl.when(pl.program_id(0) == 0)
        def _():
            dst_ref[...] = jnp.zeros_like(dst_ref)
            dl0_ref[...] = jnp.zeros_like(dl0_ref)
            dl1_ref[...] = jnp.zeros_like(dl1_ref)
            dnw_ref[...] = jnp.zeros_like(dnw_ref)

        dnw_acc = jnp.zeros((1, HEAD), F32)
        for hs in _head_groups(n_heads):
            _, vjp = jax.vjp(_hg_block, [save_ref[h] for h in hs],
                             *[[r[:, _head_cols(h)] for h in hs] for r in (q_ref, f_ref, i_ref, g_ref)],
                             [l0_ref[h] for h in hs], [l1_ref[h] for h in hs], nw_ref[...])
            dst, dq, df, di, dg, dl0, dl1, dnw = vjp(([dst_ref[h] for h in hs], [do_ref[:, _head_cols(h)] for h in hs]))
            for i, h in enumerate(hs):
                dst_ref[h] = dst[i]
                for part, val in enumerate((dq, df, di, dg)):
                    dp_ref[:, part * hw + h * HEAD:part * hw + (h + 1) * HEAD] = val[i].astype(dp_ref.dtype)
                dl0_ref[h] += dl0[i]
                dl1_ref[h] += dl1[i]
            dnw_acc = dnw_acc + dnw
        dnw_ref[...] += dnw_acc

    head_rows = pl.BlockSpec((n_heads, 1, HEAD), lambda j: (0, 0, 0))
    return pl.pallas_call(
        body, name=name, grid=(nb,),
        in_specs=_hg_in_specs(n_heads, tb, rev) + [pl.BlockSpec((None, n_heads, HEAD, HEAD), lambda j: (rev(j), 0, 0, 0)),
                                                   pl.BlockSpec((tb, hw), lambda j: (rev(j), 0))],
        out_specs=[pl.BlockSpec((tb, 4 * hw), lambda j: (rev(j), 0)), head_rows, head_rows,
                   pl.BlockSpec((1, HEAD), lambda j: (0, 0))],
        out_shape=[jax.ShapeDtypeStruct((t, 4 * hw), BF16)] + [jax.ShapeDtypeStruct((n_heads, 1, HEAD), F32)] * 2
        + [jax.ShapeDtypeStruct((1, HEAD), F32)],
        scratch_shapes=[pltpu.VMEM((n_heads, HEAD, HEAD), F32)], compiler_params=_params(("arbitrary",)),
    )(proj, proj, proj, proj, l0, l1, nw, saved, d_ocat)


def _dot_hi(a, b, dims=(((1,), (0,)), ((), ()))):
    return lax.dot_general(a, b, dims, precision=HI, preferred_element_type=F32)


def _dot_bf16(a, b, dims=(((1,), (0,)), ((), ()))):
    return lax.dot_general(a.astype(BF16), b.astype(BF16), dims, preferred_element_type=F32)


def _inv_unit_lower_raw(ms):
    hs = range(len(ms))
    c = ms[0].shape[0]
    r = lax.broadcasted_iota(jnp.int32, (c, c), 0)
    q = lax.broadcasted_iota(jnp.int32, (c, c), 1)
    eye = (r == q).astype(F32)
    md = [jnp.where((r // GDN_INV_BLOCK) == (q // GDN_INV_BLOCK), ms[h], 0.0) for h in hs]
    p = [-md[h] for h in hs]
    t16 = [eye + p[h] for h in hs]
    for _ in range(int(math.log2(GDN_INV_BLOCK)) - 1):
        p = [_dot_hi(p[h], p[h]) for h in hs]
        t16 = [t16[h] + _dot_hi(t16[h], p[h]) for h in hs]
    p = [-_dot_hi(t16[h], ms[h] - md[h]) for h in hs]
    t2 = [eye + p[h] for h in hs]
    for _ in range(int(math.log2(c // GDN_INV_BLOCK)) - 1):
        p = [_dot_hi(p[h], p[h]) for h in hs]
        t2 = [t2[h] + _dot_hi(t2[h], p[h]) for h in hs]
    return [_dot_hi(t2[h], t16[h]) for h in hs]


@jax.custom_vjp
def _inv_unit_lower(ms):
    return _inv_unit_lower_raw(ms)


def _inv_fwd(ms):
    ts = _inv_unit_lower_raw(ms)
    return ts, ts


def _inv_bwd(ts, dts):
    hs = range(len(ts))
    inner = [_dot_hi(ts[h], dts[h], TN_DIMS) for h in hs]
    return ([-_dot_hi(inner[h], ts[h], NT_DIMS) for h in hs],)


_inv_unit_lower.defvjp(_inv_fwd, _inv_bwd)


def _gdn_block(inverse, onehots, st, qc, kc, vc, g, ab, alog_row, dtb_row, nw):
    hs = range(len(st))
    c = qc[0].shape[0]
    lane_sum = lambda v: jnp.sum(v, axis=-1, keepdims=True)
    a = [lane_sum(ab * onehots[h][0]) for h in hs]
    bb = [lane_sum(ab * onehots[h][1]) for h in hs]
    alog = [lane_sum(alog_row * onehots[h][0]) for h in hs]
    dtb = [lane_sum(dtb_row * onehots[h][0]) for h in hs]
    la = [-jnp.exp(alog[h]) * _softplus(a[h] + dtb[h]) for h in hs]
    beta = [jax.nn.sigmoid(bb[h]) for h in hs]
    q = [qc[h] * lax.rsqrt(lane_sum(qc[h] * qc[h]) + EPS) * (HEAD ** -0.5) for h in hs]
    k = [kc[h] * lax.rsqrt(lane_sum(kc[h] * kc[h]) + EPS) for h in hs]
    r = lax.broadcasted_iota(jnp.int32, (c, c), 0)
    s = lax.broadcasted_iota(jnp.int32, (c, c), 1)
    tri = (r >= s).astype(F32)
    g_cc = [_dot_hi(tri, jnp.broadcast_to(la[h], (c, c))) for h in hs]
    g_cl = [_dot_hi(tri, jnp.broadcast_to(la[h], (c, HEAD))) for h in hs]
    gamma = [jnp.exp(jnp.where(r >= s, g_cc[h] - g_cc[h].T, -1e30)) for h in hs]
    kk = [_dot_bf16(k[h], k[h], NT_DIMS) for h in hs]
    m = [jnp.where(r > s, beta[h] * kk[h] * gamma[h], 0.0) for h in hs]
    tm = inverse(m)
    eg = [jnp.exp(g_cl[h]) for h in hs]
    rhs = [jnp.concatenate([vc[h] * beta[h], k[h] * (beta[h] * eg[h])], axis=1) for h in hs]
    sol = [_dot_hi(tm[h], rhs[h]) for h in hs]
    qk = [_dot_bf16(q[h], k[h], NT_DIMS) * gamma[h] for h in hs]
    g_last = [g_cl[h][c - 1:c] for h in hs]
    k_tail = [k[h] * jnp.exp(g_last[h] - g_cl[h]) for h in hs]
    v_new = [sol[h][:, :HEAD] - _dot_bf16(sol[h][:, HEAD:], st[h], NT_DIMS) for h in hs]
    o_st = [_dot_bf16(q[h] * eg[h], st[h], NT_DIMS) for h in hs]
    o = [o_st[h] + _dot_bf16(qk[h], v_new[h]) for h in hs]
    upd = [_dot_bf16(v_new[h], k_tail[h], TN_DIMS) for h in hs]
    st = [st[h] * jnp.exp(g_last[h]) + upd[h] for h in hs]
    out = [o[h] * _rms_scale(o[h]) * nw * _silu(g[h]) for h in hs]
    return st, out


def _head_onehots(n_heads, h):
    lane = lax.broadcasted_iota(jnp.int32, (1, LANES), 1)
    return (lane == h).astype(F32), (lane == n_heads + h).astype(F32)


def _gdn_in_specs(n_heads, c, time_index):
    hw = n_heads * HEAD
    qkv = [pl.BlockSpec((c, hw), functools.partial(lambda part, j: (time_index(j), part), part)) for part in range(3)]
    row = pl.BlockSpec((1, LANES), lambda j: (0, 0))
    return qkv + [pl.BlockSpec((c, hw), lambda j: (time_index(j), 7)),
                  pl.BlockSpec((c, LANES), lambda j: (time_index(j), 8 * n_heads)), row, row, row]


def _gdn_fwd(qkv, proj, alog_row, dtb_row, nw, n_heads, name):
    t = qkv.shape[0]
    hw = n_heads * HEAD
    c = _tile(t, GDN_CHUNK, GDN_CHUNK)
    nb = t // c

    def body(q_ref, k_ref, v_ref, g_ref, ab_ref, al_ref, dt_ref, nw_ref, o_ref, save_ref, st_ref):
        @pl.when(pl.program_id(0) == 0)
        def _():
            st_ref[...] = jnp.zeros_like(st_ref)

        for hs in _head_groups(n_heads):
            st = [st_ref[h] for h in hs]
            for h, s in zip(hs, st):
                save_ref[h] = s
            st, out = _gdn_block(_inv_unit_lower_raw, [_head_onehots(n_heads, h) for h in hs], st,
                                 *[[r[:, _head_cols(h)] for h in hs] for r in (q_ref, k_ref, v_ref, g_ref)],
                                 ab_ref[...], al_ref[...], dt_ref[...], nw_ref[...])
            for h, s, o in zip(hs, st, out):
                st_ref[h] = s
                o_ref[:, _head_cols(h)] = o.astype(o_ref.dtype)

    return pl.pallas_call(
        body, name=name, grid=(nb,), in_specs=_gdn_in_specs(n_heads, c, lambda j: j),
        out_specs=[pl.BlockSpec((c, hw), lambda j: (j, 0)),
                   pl.BlockSpec((None, n_heads, HEAD, HEAD), lambda j: (j, 0, 0, 0))],
        out_shape=[jax.ShapeDtypeStruct((t, hw), BF16), jax.ShapeDtypeStruct((nb, n_heads, HEAD, HEAD), F32)],
        scratch_shapes=[pltpu.VMEM((n_heads, HEAD, HEAD), F32)], compiler_params=_params(("arbitrary",)),
    )(qkv, qkv, qkv, proj, proj, alog_row, dtb_row, nw)


def _gdn_bwd(qkv, proj, alog_row, dtb_row, nw, saved, d_ocat, n_heads, name):
    t = qkv.shape[0]
    c = _tile(t, GDN_CHUNK, GDN_CHUNK)
    nb = t // c
    rev = lambda j: nb - 1 - j

    hw = n_heads * HEAD

    def body(q_ref, k_ref, v_ref, g_ref, ab_ref, al_ref, dt_ref, nw_ref, save_ref, do_ref,
             dqkv_ref, dg_ref, dab_ref, dal_ref, ddt_ref, dnw_ref, dst_ref):
        @pl.when(pl.program_id(0) == 0)
        def _():
            dst_ref[...] = jnp.zeros_like(dst_ref)
            dal_ref[...] = jnp.zeros_like(dal_ref)
            ddt_ref[...] = jnp.zeros_like(ddt_ref)
            dnw_ref[...] = jnp.zeros_like(dnw_ref)

        dab_acc = jnp.zeros((c, LANES), F32)
        row_acc = [jnp.zeros((1, LANES), F32)] * 3
        for hs in _head_groups(n_heads):
            fn = functools.partial(_gdn_block, _inv_unit_lower, [_head_onehots(n_heads, h) for h in hs])
            _, vjp = jax.vjp(fn, [save_ref[h] for h in hs],
                             *[[r[:, _head_cols(h)] for h in hs] for r in (q_ref, k_ref, v_ref, g_ref)],
                             ab_ref[...], al_ref[...], dt_ref[...], nw_ref[...])
            dst, dq, dk, dv, dg, dab, dal, ddt, dnw = vjp(([dst_ref[h] for h in hs], [do_ref[:, _head_cols(h)] for h in hs]))
            for i, h in enumerate(hs):
                dst_ref[h] = dst[i]
                for part, val in enumerate((dq, dk, dv)):
                    dqkv_ref[:, part * hw + h * HEAD:part * hw + (h + 1) * HEAD] = val[i]
                dg_ref[:, _head_cols(h)] = dg[i].astype(dg_ref.dtype)
            dab_acc = dab_acc + dab
            row_acc = [acc + val for acc, val in zip(row_acc, (dal, ddt, dnw))]
        dab_ref[...] = dab_acc
        dal_ref[...] += row_acc[0]
        ddt_ref[...] += row_acc[1]
        dnw_ref[...] += row_acc[2]

    row = pl.BlockSpec((1, LANES), lambda j: (0, 0))
    return pl.pallas_call(
        body, name=name, grid=(nb,),
        in_specs=_gdn_in_specs(n_heads, c, rev) + [pl.BlockSpec((None, n_heads, HEAD, HEAD), lambda j: (rev(j), 0, 0, 0)),
                                                   pl.BlockSpec((c, hw), lambda j: (rev(j), 1))],
        out_specs=[pl.BlockSpec((c, 3 * hw), lambda j: (rev(j), 0)), pl.BlockSpec((c, hw), lambda j: (rev(j), 0)),
                   pl.BlockSpec((c, LANES), lambda j: (rev(j), 0)), row, row, row],
        out_shape=[jax.ShapeDtypeStruct((t, 3 * hw), F32), jax.ShapeDtypeStruct((t, hw), BF16),
                   jax.ShapeDtypeStruct((t, LANES), F32)] + [jax.ShapeDtypeStruct((1, LANES), F32)] * 3,
        scratch_shapes=[pltpu.VMEM((n_heads, HEAD, HEAD), F32)], compiler_params=_params(("arbitrary",)),
    )(qkv, qkv, qkv, proj, proj, alog_row, dtb_row, nw, saved, d_ocat)


def _pad_lanes(v, n):
    v = v.reshape(1, -1)
    return jnp.pad(v, ((0, 0), (0, n - v.shape[1])))


def _pack_rows(vecs):
    flat = jnp.concatenate([v.reshape(-1) for v in vecs])
    offs, o = [], 0
    for v in vecs:
        offs.append((o, v.size))
        o += v.size
    per_row = -(-o // (SUBLANES * LANES)) * LANES
    flat = jnp.pad(flat, (0, SUBLANES * per_row - o))
    return flat.reshape(SUBLANES, per_row), offs


def _unpack(gathered, offs):
    per_dev = gathered.reshape(N_DEV, -1)
    return [per_dev[:, o:o + n] for o, n in offs]


def _sum_devices(part):
    acc = part[0]
    for i in range(1, N_DEV):
        acc = acc + part[i]
    return acc


def kernel(x, c, w_ada, b_ada, pre_mix_norm, post_mix_norm, pre_ffn_norm, post_ffn_norm, w_in, hg_lb_logits, hg_norm, gdn_conv_w, gdn_a_log, gdn_dt_bias, gdn_norm, w_out, w_ff1, w_ff2, loss_target, m_w_ada, m_b_ada, m_pre_mix_norm, m_post_mix_norm, m_pre_ffn_norm, m_post_ffn_norm, m_w_in, m_hg_lb_logits, m_hg_norm, m_gdn_conv_w, m_gdn_a_log, m_gdn_dt_bias, m_gdn_norm, m_w_out, m_w_ff1, m_w_ff2, v_w_ada, v_b_ada, v_pre_mix_norm, v_post_mix_norm, v_pre_ffn_norm, v_post_ffn_norm, v_w_in, v_hg_lb_logits, v_hg_norm, v_gdn_conv_w, v_gdn_a_log, v_gdn_dt_bias, v_gdn_norm, v_w_out, v_w_ff1, v_w_ff2):
    assert x.shape[0] == 1 and w_ada.shape[0] == 1 and hg_lb_logits.shape[0] == 2
    t, d = x.shape[1], x.shape[2]
    n_heads = (d // 2) // HEAD
    hw = n_heads * HEAD
    in_cols = 8 * hw + 2 * n_heads
    np_cols = 8 * hw + LANES
    d_ff = w_ff1.shape[2] * N_CHIP
    na = w_ada.shape[2]
    ax, ay, ac = lax.axis_index("x"), lax.axis_index("y"), lax.axis_index("c")
    chip = 2 * ax + ay
    dev = 4 * ax + 2 * ay + ac

    x2d, tgt = x[0], loss_target[0]

    pack1, offs1 = _pack_rows([c[0], gdn_conv_w[0]])
    c_all, convw_all = _unpack(_gather8(pack1, "gather_cond"), offs1)
    conv_sh = gdn_conv_w.shape[2]
    conv_w = jnp.concatenate([convw_all[2 * j].reshape(CONV_K, conv_sh) for j in range(N_CHIP)], axis=1)

    b_s = lax.dynamic_slice(b_ada, (0, chip * na), (1, na))
    mod_part = _mod_part(c_all, w_ada[0], b_s, "mod_part")
    pack2, offs2 = _pack_rows([mod_part])
    (mod_parts,) = _unpack(_gather8(pack2, "gather_mod"), offs2)
    mod_all = jnp.concatenate([mod_parts[2 * j].reshape(N_DEV, na) for j in range(N_CHIP)], axis=1)
    mod = lax.dynamic_slice(mod_all, (dev, 0), (1, N_MOD * d))
    sh_m, sc_m, gt_m, sh_f, sc_f, gt_f = [mod[:, i * d:(i + 1) * d] for i in range(N_MOD)]

    (g_in,) = _gather_weights([w_in[0].astype(BF16)], "gather_w_in")
    g_out, g_ff1, g_ff2 = _sequencer_gather_weights(
        [w_out[0].astype(BF16), w_ff1[0].astype(BF16), w_ff2[0].astype(BF16)], "gather_weights_late", 1)
    w_in_f = jnp.pad(jnp.transpose(g_in, (1, 0, 2)).reshape(d, in_cols), ((0, 0), (0, np_cols - in_cols)))
    w_out_f = g_out.reshape(d, d)
    w_ff2_f = g_ff2.reshape(d_ff, d)

    h1 = _norm_mod(x2d, pre_mix_norm, sc_m, sh_m, "norm_mod_mix")
    proj = _matmul(h1, w_in_f, "nn", F32, "mm_in", tn=640)
    l0, l1 = hg_lb_logits[0].reshape(n_heads, 1, HEAD), hg_lb_logits[1].reshape(n_heads, 1, HEAD)
    o_hg, hg_saved = _hgrn2_fwd(proj, l0, l1, hg_norm, n_heads, "hgrn2_fwd")
    qkv = _conv_fwd(proj, conv_w, 4 * n_heads, "conv_fwd")
    alog_row, dtb_row = _pad_lanes(gdn_a_log, LANES), _pad_lanes(gdn_dt_bias, LANES)
    o_gdn, gdn_saved = _gdn_fwd(qkv, proj, alog_row, dtb_row, gdn_norm, n_heads, "gdn_fwd")
    o_cat = jnp.concatenate([o_hg, o_gdn], axis=1)
    y1 = _matmul(o_cat, w_out_f, "nn", F32, "mm_out")
    x_mid = _resid(x2d, y1, post_mix_norm, gt_m, "resid_mix")

    h2 = _norm_mod(x_mid, pre_ffn_norm, sc_f, sh_f, "norm_mod_ffn")
    relu_a1, r1 = _matmul(h2, g_ff1, "nn", BF16, "mm_ff1", relu2=True, b_split=True)
    y2 = _matmul(r1, w_ff2_f, "nn", F32, "mm_ff2")
    d_out, loss_row = _loss_head(x_mid, y2, post_ffn_norm, gt_f, tgt, "loss_head")

    in_sh = in_cols // N_CHIP
    ff_sh = d_ff // N_CHIP
    my_half = jnp.reshape(ac, (1,)).astype(jnp.int32)

    def start_reduce(by_chip, tag, collective_id):
        to_sib = [_row_half_to_bf16(a, 1 - my_half, None, f"sibling_half_{tag}{i}") for i, a in enumerate(by_chip)]
        from_sib = _sibling_exchange(to_sib, f"sibling_partials_{tag}")
        chip_part = [_row_half_to_bf16(a, my_half, s, f"add_halves_{tag}{i}") for i, (a, s) in enumerate(zip(by_chip, from_sib))]
        return _sequencer_chip_exchange(chip_part, f"scatter_grads_{tag}", collective_id)

    dy2, d_gt_f, d_post_ffn = _resid_bwd(d_out, y2, post_ffn_norm, gt_f, "resid_ffn_bwd")
    gw_ff2 = _matmul(r1, dy2, "tn", F32, "mm_ff2_dw")
    da1 = _matmul(dy2, w_ff2_f, "nt", BF16, "mm_ff2_dx", times=relu_a1)
    gw_ff1 = _matmul(h2, da1, "tn", F32, "mm_ff1_dw", out_split=True)
    recv_ff2, recv_ff1 = start_reduce([gw_ff2.reshape(N_CHIP, ff_sh, d), gw_ff1], "ff", 2)
    dh2 = _matmul(da1, g_ff1, "nt", BF16, "mm_ff1_dx", b_split=True)
    d_mid, d_pre_ffn, d_sc_f, d_sh_f = _norm_mod_bwd(x_mid, pre_ffn_norm, sc_f, dh2, d_out, "norm_mod_ffn_bwd")

    dy1, d_gt_m, d_post_mix = _resid_bwd(d_mid, y1, post_mix_norm, gt_m, "resid_mix_bwd")
    gw_out = _matmul(o_cat, dy1, "tn", F32, "mm_out_dw")
    (recv_out,) = start_reduce([gw_out.reshape(N_CHIP, d // N_CHIP, d)], "out", 3)
    d_ocat = _matmul(dy1, w_out_f, "nt", F32, "mm_out_dx")
    dp_hg, dl0, dl1, d_hg_norm = _hgrn2_bwd(proj, l0, l1, hg_norm, hg_saved, d_ocat, n_heads, "hgrn2_bwd")
    dqkv, dg_g, dab, d_alog, d_dtb, d_gdn_norm = _gdn_bwd(
        qkv, proj, alog_row, dtb_row, gdn_norm, gdn_saved, d_ocat, n_heads, "gdn_bwd")
    du, d_conv_w = _conv_bwd(proj, conv_w, dqkv, 4 * n_heads, "conv_bwd")
    dproj = jnp.concatenate([dp_hg, du, dg_g, dab.astype(BF16)], axis=1)
    gw_in = _matmul(h1, dproj, "tn", F32, "mm_in_dw", tn=640)
    (recv_in,) = start_reduce([jnp.transpose(gw_in[:, :in_cols].reshape(d, N_CHIP, in_sh), (1, 0, 2))], "in", 4)
    dh1 = _matmul(dproj, w_in_f, "nt", BF16, "mm_in_dx", tk=1664)
    grad_x, d_pre_mix, d_sc_m, d_sh_m = _norm_mod_bwd(x2d, pre_mix_norm, sc_m, dh1, d_mid, "norm_mod_mix_bwd")

    d_mod = jnp.concatenate([d_sh_m, d_sc_m, d_gt_m, d_sh_f, d_sc_f, d_gt_f], axis=1)
    d_lb_logits = jnp.stack([dl0.reshape(n_heads, HEAD), dl1.reshape(n_heads, HEAD)])
    pack3, offs3 = _pack_rows([loss_row[0, :1], d_pre_mix, d_post_mix, d_pre_ffn, d_post_ffn, d_lb_logits, d_hg_norm,
                               d_conv_w, d_alog[0, :n_heads], d_dtb[0, :n_heads], d_gdn_norm, d_mod])
    parts = _unpack(_gather8(pack3, "gather_vec_grads"), offs3)
    sums = [_sum_devices(p) for p in parts[:-1]]
    loss = sums[0][0]
    dmod_all = parts[-1]
    g_b_ada = _sum_devices(dmod_all).reshape(1, N_MOD * d)
    g_conv_full = sums[7].reshape(CONV_K, N_CHIP * conv_sh)
    g_conv = lax.dynamic_slice(g_conv_full, (0, chip * conv_sh), (CONV_K, conv_sh))
    gw_ada = _wada_grad(c_all, lax.dynamic_slice(dmod_all, (0, chip * na), (N_DEV, na)), "wada_grad")

    mine = [_sum_chips(rv, f"sum_chips_{i}") for i, rv in enumerate([recv_in, recv_out, recv_ff1, recv_ff2])]
    theirs = _sibling_exchange(mine, "sibling_grads")

    big = {}
    for i, (nm, w_, m_, v_) in enumerate([("w_in", w_in, m_w_in, v_w_in), ("w_out", w_out, m_w_out, v_w_out),
                                          ("w_ff1", w_ff1, m_w_ff1, v_w_ff1), ("w_ff2", w_ff2, m_w_ff2, v_w_ff2)]):
        big[nm] = [o[None] for o in _adamw(w_[0], [mine[i], theirs[i]], m_[0], v_[0], f"adamw_{nm}", by_core=True)]
    big["w_ada"] = [o[None] for o in _adamw(w_ada[0], [gw_ada], m_w_ada[0], v_w_ada[0], "adamw_w_ada")]

    small_names = ["b_ada", "pre_mix_norm", "post_mix_norm", "pre_ffn_norm", "post_ffn_norm", "hg_lb_logits", "hg_norm",
                   "gdn_conv_w", "gdn_a_log", "gdn_dt_bias", "gdn_norm"]
    small_w = [b_ada, pre_mix_norm, post_mix_norm, pre_ffn_norm, post_ffn_norm, hg_lb_logits, hg_norm, gdn_conv_w,
               gdn_a_log, gdn_dt_bias, gdn_norm]
    small_m = [m_b_ada, m_pre_mix_norm, m_post_mix_norm, m_pre_ffn_norm, m_post_ffn_norm, m_hg_lb_logits, m_hg_norm,
               m_gdn_conv_w, m_gdn_a_log, m_gdn_dt_bias, m_gdn_norm]
    small_v = [v_b_ada, v_pre_mix_norm, v_post_mix_norm, v_pre_ffn_norm, v_post_ffn_norm, v_hg_lb_logits, v_hg_norm,
               v_gdn_conv_w, v_gdn_a_log, v_gdn_dt_bias, v_gdn_norm]
    small_g = [g_b_ada, sums[1], sums[2], sums[3], sums[4], sums[5], sums[6], g_conv, sums[8], sums[9], sums[10]]
    pw, offs_s = _pack_rows(small_w)
    pg, _ = _pack_rows(small_g)
    pm, _ = _pack_rows(small_m)
    pv, _ = _pack_rows(small_v)
    packed = _adamw(pw, [pg], pm, pv, "adamw_vectors")
    small = {}
    for nm, w_, (o, n) in zip(small_names, small_w, offs_s):
        small[nm] = [p.reshape(-1)[o:o + n].reshape(w_.shape) for p in packed]

    order = ["w_ada", "b_ada", "pre_mix_norm", "post_mix_norm", "pre_ffn_norm", "post_ffn_norm", "w_in", "hg_lb_logits",
             "hg_norm", "gdn_conv_w", "gdn_a_log", "gdn_dt_bias", "gdn_norm", "w_out", "w_ff1", "w_ff2"]
    res = {**big, **small}
    outs = [loss, grad_x[None]]
    for k in range(4):
        outs += [res[nm][k] for nm in order]
    return tuple(outs)
```

```python
import functools
import math

import jax
import jax.numpy as jnp
from jax import lax
from jax.experimental import pallas as pl
from jax.experimental.pallas import tpu as pltpu
from jax.experimental.pallas import tpu_sc as plsc

F32 = jnp.float32
BF16 = jnp.bfloat16
HI = lax.Precision.HIGHEST
MESH = pl.DeviceIdType.MESH

LANES = 128
SUBLANES = 8
VMEM_LIMIT = 48 * 1024 * 1024
EPS = 1e-6
HEAD = 128
CONV_K = 4
GDN_CHUNK = 64
GDN_INV_BLOCK = 16
HG_SUB = 16
HG_BLOCK = 128
HG_FWD_GROUP = 8
HG_BWD_GROUP = 4
GDN_FWD_GROUP = 8
GDN_BWD_GROUP = 8
N_MOD = 6
N_DEV = 8
N_CHIP = 4

ADAM_LR = 0.001
ADAM_B1 = 0.9
ADAM_B2 = 0.999
ADAM_EPS = 1e-08
ADAM_WD = 0.01
ADAM_STEP = 10

NT_DIMS = (((1,), (1,)), ((), ()))
TN_DIMS = (((0,), (0,)), ((), ()))


def _tile(dim, target, align):
    if dim <= target:
        return dim
    best = dim
    t = align
    while t <= target:
        if dim % t == 0:
            best = t
        t += align
    return best


def _elementwise_tiles(r, c):
    tc = _tile(c, 1024, LANES)
    tr = _tile(r, max(16, (256 * 1024) // tc // 16 * 16), 16)
    return tr, tc


def _params(sem):
    return pltpu.CompilerParams(dimension_semantics=sem, vmem_limit_bytes=VMEM_LIMIT)


def _silu(x):
    return x * jax.nn.sigmoid(x)


def _softplus(x):
    pos = x > 0
    return jnp.where(pos, x, 0.0) + jnp.log(1.0 + jnp.exp(jnp.where(pos, -x, x)))


def _rms_scale(x):
    return lax.rsqrt(jnp.mean(x * x, axis=-1, keepdims=True) + EPS)


def _gather8(x_shard, name):
    m_per, n = x_shard.shape
    assert m_per % SUBLANES == 0 and n % LANES == 0

    def body(x_ref, out_ref, send_sems, recv_sems, local_sem):
        x, y, c = lax.axis_index("x"), lax.axis_index("y"), lax.axis_index("c")
        me, sibling = (x, y, c), (x, y, 1 - c)
        chips = [(1 - x, y), (x, 1 - y), (1 - x, 1 - y)]

        def rows(px, py, pc):
            return out_ref.at[pl.ds((4 * px + 2 * py + pc) * m_per, m_per), :]

        def copy(k, block, to, src=None):
            return pltpu.make_async_remote_copy(
                src_ref=rows(*block) if src is None else src, dst_ref=rows(*block),
                send_sem=send_sems.at[k], recv_sem=recv_sems.at[k], device_id=to, device_id_type=MESH)

        mine = pltpu.make_async_copy(x_ref, rows(*me), local_sem)
        mine.start()
        first = [copy(0, me, sibling, src=x_ref)]
        first += [copy(1 + j, me, (*chip, c), src=x_ref) for j, chip in enumerate(chips)]
        for cp in first:
            cp.start()
        passed = [copy(4 + j, (*chip, c), sibling) for j, chip in enumerate(chips)]
        for j, chip in enumerate(chips):
            copy(1 + j, (*chip, c), me).wait_recv()
            passed[j].start()
        copy(0, sibling, me).wait_recv()
        for j, chip in enumerate(chips):
            copy(4 + j, (*chip, 1 - c), me).wait_recv()
        for cp in first + passed:
            cp.wait_send()
        mine.wait()

    return pl.pallas_call(
        body, name=name,
        out_shape=jax.ShapeDtypeStruct((N_DEV * m_per, n), x_shard.dtype),
        in_specs=[pl.BlockSpec(memory_space=pltpu.VMEM)],
        out_specs=pl.BlockSpec(memory_space=pltpu.VMEM),
        scratch_shapes=[pltpu.SemaphoreType.DMA((7,)), pltpu.SemaphoreType.DMA((7,)), pltpu.SemaphoreType.DMA],
        compiler_params=pltpu.CompilerParams(vmem_limit_bytes=VMEM_LIMIT),
    )(x_shard)


def _gather_weights(arrs, name):
    n = len(arrs)
    out_shapes = [jax.ShapeDtypeStruct((N_CHIP,) + a.shape, a.dtype) for a in arrs]

    def body(*refs):
        ins, outs = refs[:n], refs[n:2 * n]
        ici_send, ici_recv, d2d_send, d2d_recv = refs[2 * n:]
        x, y, c = lax.axis_index("x"), lax.axis_index("y"), lax.axis_index("c")
        me = 2 * x + y
        sibling = (x, y, 1 - c)
        peers = [(1 - x, y), (x, 1 - y), (1 - x, 1 - y)]

        def half(a, cc):
            rh = arrs[a].shape[0] // 2
            return pl.ds(pl.multiple_of(cc * rh, 16), rh)

        sent = []
        for a in range(n):
            for k, (px, py) in enumerate(peers):
                cp = pltpu.make_async_remote_copy(
                    src_ref=ins[a].at[half(a, c)], dst_ref=outs[a].at[me, half(a, c)],
                    send_sem=ici_send.at[3 * a + k], recv_sem=ici_recv.at[3 * a + k],
                    device_id=(px, py, c), device_id_type=MESH)
                cp.start()
                sent.append(cp)
        for a in range(n):
            for k, (px, py) in enumerate(peers):
                landed = outs[a].at[2 * px + py, half(a, c)]
                pltpu.make_async_remote_copy(
                    src_ref=landed, dst_ref=landed, send_sem=ici_send.at[3 * a + k], recv_sem=ici_recv.at[3 * a + k],
                    device_id=(px, py, c), device_id_type=MESH).wait_recv()
                fwd = pltpu.make_async_remote_copy(
                    src_ref=landed, dst_ref=landed, send_sem=d2d_send.at[3 * a + k], recv_sem=d2d_recv.at[3 * a + k],
                    device_id=sibling, device_id_type=MESH)
                fwd.start()
                sent.append(fwd)
        for a in range(n):
            for k, (px, py) in enumerate(peers):
                passed = outs[a].at[2 * px + py, half(a, 1 - c)]
                pltpu.make_async_remote_copy(
                    src_ref=passed, dst_ref=passed, send_sem=d2d_send.at[3 * a + k], recv_sem=d2d_recv.at[3 * a + k],
                    device_id=sibling, device_id_type=MESH).wait_recv()
        for cp in sent:
            cp.wait_send()

    hbm = pl.BlockSpec(memory_space=pltpu.HBM)
    gathered = pl.pallas_call(
        body, name=name, out_shape=out_shapes, in_specs=[hbm] * n, out_specs=[hbm] * n,
        scratch_shapes=[pltpu.SemaphoreType.DMA((3 * n,))] * 4,
    )(*arrs)
    chip = 2 * lax.axis_index("x") + lax.axis_index("y")
    return [lax.dynamic_update_slice(g, a[None], (chip, 0, 0)) for g, a in zip(gathered, arrs)]


def _chip_peers():
    x, y, c = lax.axis_index("x"), lax.axis_index("y"), lax.axis_index("c")
    return [(1 - x, y, c), (x, 1 - y, c), (1 - x, 1 - y, c)], (x, y, 1 - c)


def _handshake(peers):
    barrier = pltpu.get_barrier_semaphore()
    for peer in peers:
        pl.semaphore_signal(barrier, inc=1, device_id=peer, device_id_type=MESH)
    pl.semaphore_wait(barrier, len(peers))


def _sequencer_gather_weights(arrs, name, collective_id):
    n = len(arrs)
    out_types = [jax.ShapeDtypeStruct((N_CHIP,) + a.shape, a.dtype) for a in arrs]

    def body(*refs):
        ins, outs = refs[:n], refs[n:2 * n]
        ici_send, ici_recv, d2d_send, d2d_recv = refs[2 * n:]
        chips, sibling = _chip_peers()
        _handshake(chips + [sibling])
        c = lax.axis_index("c")
        me = 2 * lax.axis_index("x") + lax.axis_index("y")

        def half(a, cc):
            rh = arrs[a].shape[0] // 2
            return pl.ds(pl.multiple_of(cc * rh, 16), rh)

        sent = []
        for a in range(n):
            for k, peer in enumerate(chips):
                cp = pltpu.make_async_remote_copy(
                    src_ref=ins[a].at[half(a, c)], dst_ref=outs[a].at[me, half(a, c)],
                    send_sem=ici_send.at[3 * a + k], recv_sem=ici_recv.at[3 * a + k], device_id=peer, device_id_type=MESH)
                cp.start()
                sent.append(cp)
        for a in range(n):
            for k, peer in enumerate(chips):
                landed = outs[a].at[2 * peer[0] + peer[1], half(a, c)]
                pltpu.make_async_remote_copy(
                    src_ref=landed, dst_ref=landed, send_sem=ici_send.at[3 * a + k], recv_sem=ici_recv.at[3 * a + k],
                    device_id=peer, device_id_type=MESH).wait_recv()
                fwd = pltpu.make_async_remote_copy(
                    src_ref=landed, dst_ref=landed, send_sem=d2d_send.at[3 * a + k], recv_sem=d2d_recv.at[3 * a + k],
                    device_id=sibling, device_id_type=MESH)
                fwd.start()
                sent.append(fwd)
        for a in range(n):
            for k, peer in enumerate(chips):
                passed = outs[a].at[2 * peer[0] + peer[1], half(a, 1 - c)]
                pltpu.make_async_remote_copy(
                    src_ref=passed, dst_ref=passed, send_sem=d2d_send.at[3 * a + k], recv_sem=d2d_recv.at[3 * a + k],
                    device_id=sibling, device_id_type=MESH).wait_recv()
        for cp in sent:
            cp.wait_send()

    gathered = pl.kernel(
        body, out_type=out_types, mesh=plsc.ScalarSubcoreMesh(axis_name="sequencer", num_cores=1), name=name,
        scratch_types=[pltpu.SemaphoreType.DMA((3 * n,))] * 4,
        compiler_params=pltpu.CompilerParams(collective_id=collective_id),
    )(*arrs)
    chip = 2 * lax.axis_index("x") + lax.axis_index("y")
    return [lax.dynamic_update_slice(g, a[None], (chip, 0, 0)) for g, a in zip(gathered, arrs)]


def _sequencer_chip_exchange(arrs, name, collective_id):
    n = len(arrs)
    out_types = [jax.ShapeDtypeStruct(a.shape, a.dtype) for a in arrs]

    def body(*refs):
        ins, outs = refs[:n], refs[n:2 * n]
        send_sems, recv_sems = refs[2 * n:]
        chips, _ = _chip_peers()
        _handshake(chips)
        me = 2 * lax.axis_index("x") + lax.axis_index("y")
        sent = []
        for a in range(n):
            for k, peer in enumerate(chips):
                cp = pltpu.make_async_remote_copy(
                    src_ref=ins[a].at[2 * peer[0] + peer[1]], dst_ref=outs[a].at[me],
                    send_sem=send_sems.at[3 * a + k], recv_sem=recv_sems.at[3 * a + k], device_id=peer, device_id_type=MESH)
                cp.start()
                sent.append(cp)
        for a in range(n):
            for k, peer in enumerate(chips):
                landed = outs[a].at[2 * peer[0] + peer[1]]
                pltpu.make_async_remote_copy(
                    src_ref=landed, dst_ref=landed, send_sem=send_sems.at[3 * a + k], recv_sem=recv_sems.at[3 * a + k],
                    device_id=peer, device_id_type=MESH).wait_recv()
        for cp in sent:
            cp.wait_send()

    received = pl.kernel(
        body, out_type=out_types, mesh=plsc.ScalarSubcoreMesh(axis_name="sequencer", num_cores=1), name=name,
        scratch_types=[pltpu.SemaphoreType.DMA((3 * n,))] * 2,
        compiler_params=pltpu.CompilerParams(collective_id=collective_id),
    )(*arrs)
    chip = 2 * lax.axis_index("x") + lax.axis_index("y")
    return [lax.dynamic_update_slice(r, lax.dynamic_slice(a, (chip, 0, 0), (1,) + a.shape[1:]), (chip, 0, 0))
            for r, a in zip(received, arrs)]


def _sibling_exchange(arrs, name):
    n = len(arrs)

    def body(*refs):
        ins, outs = refs[:n], refs[n:2 * n]
        send_sems, recv_sems = refs[2 * n:]
        sibling = (lax.axis_index("x"), lax.axis_index("y"), 1 - lax.axis_index("c"))
        cps = []
        for a in range(n):
            cp = pltpu.make_async_remote_copy(src_ref=ins[a], dst_ref=outs[a], send_sem=send_sems.at[a],
                                              recv_sem=recv_sems.at[a], device_id=sibling, device_id_type=MESH)
            cp.start()
            cps.append(cp)
        for cp in cps:
            cp.wait_recv()
        for cp in cps:
            cp.wait_send()

    hbm = pl.BlockSpec(memory_space=pltpu.HBM)
    return pl.pallas_call(
        body, name=name, out_shape=[jax.ShapeDtypeStruct(a.shape, a.dtype) for a in arrs],
        in_specs=[hbm] * n, out_specs=[hbm] * n,
        scratch_shapes=[pltpu.SemaphoreType.DMA((n,)), pltpu.SemaphoreType.DMA((n,))],
    )(*arrs)


def _matmul(a, b, mode, out_dtype, name, tm=1024, tn=1024, tk=2048, relu2=False, times=None, b_split=False,
            out_split=False):
    b_shape = (b.shape[1], b.shape[2] * N_CHIP) if b_split else b.shape
    if mode == "nn":
        (m, k), (k2, n) = a.shape, b_shape
    elif mode == "nt":
        (m, k), (n, k2) = a.shape, b_shape
    else:
        (k, m), (k2, n) = a.shape, b_shape
    assert k == k2, (a.shape, b.shape, mode)
    n_cut = n // N_CHIP if (out_split or (b_split and mode != "nt")) else n
    k_cut = k // N_CHIP if (b_split and mode == "nt") else k
    tm, tn, tk = _tile(m, tm, LANES), _tile(n_cut, tn, LANES), _tile(k_cut, tk, LANES)
    assert n_cut % tn == 0 and k_cut % tk == 0 and m % tm == 0, (name, m, n, k, tm, tn, tk)
    nk = k // tk
    nbc, nkc = n_cut // tn, k_cut // tk
    n_in = 2 if times is None else 3
    n_out = 2 if relu2 else 1

    def product(a_ref, b_ref):
        if mode == "nn":
            return jnp.dot(a_ref[...], b_ref[...], preferred_element_type=F32)
        return lax.dot_general(a_ref[...], b_ref[...], NT_DIMS if mode == "nt" else TN_DIMS, preferred_element_type=F32)

    def finish(p, refs, o_refs):
        if relu2:
            p = jnp.maximum(p, 0.0)
            o_refs[0][...] = p.astype(o_refs[0].dtype)
            o_refs[1][...] = (p * p).astype(o_refs[1].dtype)
        elif times is not None:
            o_refs[0][...] = (2.0 * refs[2][...].astype(F32) * p).astype(o_refs[0].dtype)
        else:
            o_refs[0][...] = p.astype(o_refs[0].dtype)

    def body(*refs):
        o_refs = refs[n_in:n_in + n_out]
        if nk == 1:
            finish(product(refs[0], refs[1]), refs, o_refs)
            return
        acc_ref = refs[n_in + n_out]
        kk = pl.program_id(2)

        @pl.when(kk == 0)
        def _():
            acc_ref[...] = product(refs[0], refs[1])

        @pl.when((kk > 0) & (kk < nk - 1))
        def _():
            acc_ref[...] += product(refs[0], refs[1])

        @pl.when(kk == nk - 1)
        def _():
            finish(acc_ref[...] + product(refs[0], refs[1]), refs, o_refs)

    if mode == "tn":
        a_spec = pl.BlockSpec((tk, tm), lambda i, j, kk: (kk, i))
    else:
        a_spec = pl.BlockSpec((tm, tk), lambda i, j, kk: (i, kk))
    if mode == "nt":
        b_spec = (pl.BlockSpec((None, tn, tk), lambda i, j, kk: (kk // nkc, j, kk % nkc)) if b_split
                  else pl.BlockSpec((tn, tk), lambda i, j, kk: (j, kk)))
    else:
        b_spec = (pl.BlockSpec((None, tk, tn), lambda i, j, kk: (j // nbc, kk, j % nbc)) if b_split
                  else pl.BlockSpec((tk, tn), lambda i, j, kk: (kk, j)))
    mn_spec = pl.BlockSpec((tm, tn), lambda i, j, kk: (i, j))
    if out_split:
        o_spec = pl.BlockSpec((None, tm, tn), lambda i, j, kk: (j // nbc, i, j % nbc))
        o_shape = jax.ShapeDtypeStruct((N_CHIP, m, n_cut), out_dtype)
    else:
        o_spec, o_shape = mn_spec, jax.ShapeDtypeStruct((m, n), out_dtype)
    out = pl.pallas_call(
        body, name=name, grid=(m // tm, n // tn, nk), in_specs=[a_spec, b_spec] + [mn_spec] * (n_in - 2),
        out_specs=[o_spec] * n_out, out_shape=[o_shape] * n_out,
        scratch_shapes=[] if nk == 1 else [pltpu.VMEM((tm, tn), F32)],
        compiler_params=_params(("parallel", "parallel", "arbitrary")),
    )(*((a, b) if times is None else (a, b, times)))
    return out if relu2 else out[0]


def _mod_part(c_all, w_s, b_s, name):
    d, na = w_s.shape
    tn = _tile(na, 512, LANES)

    def body(c_ref, w_ref, b_ref, o_ref):
        ca = _silu(c_ref[...]).astype(BF16)
        o_ref[...] = jnp.dot(ca, w_ref[...].astype(BF16), preferred_element_type=F32) + b_ref[...]

    return pl.pallas_call(
        body, name=name, grid=(na // tn,),
        in_specs=[pl.BlockSpec((N_DEV, d), lambda j: (0, 0)), pl.BlockSpec((d, tn), lambda j: (0, j)),
                  pl.BlockSpec((1, tn), lambda j: (0, j))],
        out_specs=pl.BlockSpec((N_DEV, tn), lambda j: (0, j)),
        out_shape=jax.ShapeDtypeStruct((N_DEV, na), F32), compiler_params=_params(("parallel",)),
    )(c_all, w_s, b_s)


def _wada_grad(c_all, dmod_s, name):
    d = c_all.shape[1]
    na = dmod_s.shape[1]
    td, tn = _tile(d, 512, LANES), _tile(na, 512, LANES)

    def body(c_ref, g_ref, o_ref):
        o_ref[...] = lax.dot_general(_silu(c_ref[...]), g_ref[...], TN_DIMS, precision=HI, preferred_element_type=F32)

    return pl.pallas_call(
        body, name=name, grid=(d // td, na // tn),
        in_specs=[pl.BlockSpec((N_DEV, td), lambda i, j: (0, i)), pl.BlockSpec((N_DEV, tn), lambda i, j: (0, j))],
        out_specs=pl.BlockSpec((td, tn), lambda i, j: (i, j)),
        out_shape=jax.ShapeDtypeStruct((d, na), F32), compiler_params=_params(("parallel", "parallel")),
    )(c_all, dmod_s)


def _row_specs(tb, d, n_full, n_vec):
    full = pl.BlockSpec((tb, d), lambda i: (i, 0))
    vec = pl.BlockSpec((1, d), lambda i: (0, 0))
    return [full] * n_full + [vec] * n_vec


def _norm_mod(x, w, sc, sh, name):
    t, d = x.shape
    tb = _tile(t, 256, SUBLANES)

    def body(x_ref, w_ref, sc_ref, sh_ref, o_ref):
        xv = x_ref[...]
        o_ref[...] = (xv * _rms_scale(xv) * w_ref[...] * (1.0 + sc_ref[...]) + sh_ref[...]).astype(o_ref.dtype)

    return pl.pallas_call(
        body, name=name, grid=(t // tb,), in_specs=_row_specs(tb, d, 1, 3),
        out_specs=pl.BlockSpec((tb, d), lambda i: (i, 0)), out_shape=jax.ShapeDtypeStruct((t, d), BF16),
        compiler_params=_params(("parallel",)),
    )(x, w, sc, sh)


def _norm_mod_bwd(x, w, sc, dh, dres, name):
    t, d = x.shape
    tb = _tile(t, 256, SUBLANES)

    def body(x_ref, w_ref, sc_ref, dh_ref, dres_ref, dx_ref, dw_ref, dsc_ref, dsh_ref):
        @pl.when(pl.program_id(0) == 0)
        def _():
            dw_ref[...] = jnp.zeros_like(dw_ref)
            dsc_ref[...] = jnp.zeros_like(dsc_ref)
            dsh_ref[...] = jnp.zeros_like(dsh_ref)

        xv = x_ref[...]
        r = _rms_scale(xv)
        xn = xv * r
        g = dh_ref[...].astype(F32)
        wv, one_sc = w_ref[...], 1.0 + sc_ref[...]
        gxn = g * xn
        dsh_ref[...] += jnp.sum(g, axis=0, keepdims=True)
        dsc_ref[...] += jnp.sum(gxn, axis=0, keepdims=True) * wv
        dw_ref[...] += jnp.sum(gxn, axis=0, keepdims=True) * one_sc
        dxn = g * (wv * one_sc)
        dx_ref[...] = dres_ref[...] + r * (dxn - xn * jnp.mean(dxn * xn, axis=-1, keepdims=True))

    vec_out = pl.BlockSpec((1, d), lambda i: (0, 0))
    return pl.pallas_call(
        body, name=name, grid=(t // tb,),
        in_specs=[pl.BlockSpec((tb, d), lambda i: (i, 0)), pl.BlockSpec((1, d), lambda i: (0, 0)),
                  pl.BlockSpec((1, d), lambda i: (0, 0)), pl.BlockSpec((tb, d), lambda i: (i, 0)),
                  pl.BlockSpec((tb, d), lambda i: (i, 0))],
        out_specs=[pl.BlockSpec((tb, d), lambda i: (i, 0)), vec_out, vec_out, vec_out],
        out_shape=[jax.ShapeDtypeStruct((t, d), F32)] + [jax.ShapeDtypeStruct((1, d), F32)] * 3,
        compiler_params=_params(("arbitrary",)),
    )(x, w, sc, dh, dres)


def _resid(x, y, w, gt, name):
    t, d = x.shape
    tb = _tile(t, 256, SUBLANES)

    def body(x_ref, y_ref, w_ref, gt_ref, o_ref):
        yv = y_ref[...]
        o_ref[...] = x_ref[...] + gt_ref[...] * (yv * _rms_scale(yv) * w_ref[...])

    return pl.pallas_call(
        body, name=name, grid=(t // tb,), in_specs=_row_specs(tb, d, 2, 2),
        out_specs=pl.BlockSpec((tb, d), lambda i: (i, 0)), out_shape=jax.ShapeDtypeStruct((t, d), F32),
        compiler_params=_params(("parallel",)),
    )(x, y, w, gt)


def _loss_head(x2, y2, w, gt, target, name):
    t, d = x2.shape
    tb = _tile(t, 256, SUBLANES)

    def body(x_ref, y_ref, tg_ref, w_ref, gt_ref, do_ref, loss_ref):
        @pl.when(pl.program_id(0) == 0)
        def _():
            loss_ref[...] = jnp.zeros_like(loss_ref)

        yv = y_ref[...]
        out = x_ref[...] + gt_ref[...] * (yv * _rms_scale(yv) * w_ref[...])
        err = out - tg_ref[...]
        do_ref[...] = err * (1.0 / d)
        per_tok = jnp.mean(err * err, axis=-1, keepdims=True)
        loss_ref[...] += 0.5 * jnp.sum(per_tok, axis=0, keepdims=True)

    return pl.pallas_call(
        body, name=name, grid=(t // tb,), in_specs=_row_specs(tb, d, 3, 2),
        out_specs=[pl.BlockSpec((tb, d), lambda i: (i, 0)), pl.BlockSpec((1, LANES), lambda i: (0, 0))],
        out_shape=[jax.ShapeDtypeStruct((t, d), F32), jax.ShapeDtypeStruct((1, LANES), F32)],
        compiler_params=_params(("arbitrary",)),
    )(x2, y2, target, w, gt)


def _resid_bwd(dout, y, w, gt, name):
    t, d = y.shape
    tb = _tile(t, 256, SUBLANES)

    def body(do_ref, y_ref, w_ref, gt_ref, dy_ref, dgt_ref, dw_ref):
        @pl.when(pl.program_id(0) == 0)
        def _():
            dgt_ref[...] = jnp.zeros_like(dgt_ref)
            dw_ref[...] = jnp.zeros_like(dw_ref)

        yv, g = y_ref[...], do_ref[...]
        r = _rms_scale(yv)
        yn = yv * r
        wv, gtv = w_ref[...], gt_ref[...]
        gyn = jnp.sum(g * yn, axis=0, keepdims=True)
        dgt_ref[...] += gyn * wv
        dw_ref[...] += gyn * gtv
        dyn = g * (gtv * wv)
        dy_ref[...] = (r * (dyn - yn * jnp.mean(dyn * yn, axis=-1, keepdims=True))).astype(dy_ref.dtype)

    vec_out = pl.BlockSpec((1, d), lambda i: (0, 0))
    return pl.pallas_call(
        body, name=name, grid=(t // tb,), in_specs=_row_specs(tb, d, 2, 2),
        out_specs=[pl.BlockSpec((tb, d), lambda i: (i, 0)), vec_out, vec_out],
        out_shape=[jax.ShapeDtypeStruct((t, d), BF16)] + [jax.ShapeDtypeStruct((1, d), F32)] * 2,
        compiler_params=_params(("arbitrary",)),
    )(dout, y, w, gt)


def _row_half_to_bf16(full, which, sib, name):
    n, r2, c = full.shape
    r = r2 // 2
    tr, tc = _elementwise_tiles(r, c)
    nbh = r // tr

    def body(which_ref, a_ref, *rest):
        if sib is None:
            rest[0][...] = a_ref[...].astype(BF16)
        else:
            rest[1][...] = (a_ref[...] + rest[0][...].astype(F32)).astype(BF16)

    half_spec = pl.BlockSpec((1, tr, tc), lambda j, i, k, which_ref: (j, which_ref[0] * nbh + i, k))
    spec = pl.BlockSpec((1, tr, tc), lambda j, i, k, which_ref: (j, i, k))
    grid_spec = pltpu.PrefetchScalarGridSpec(
        num_scalar_prefetch=1, grid=(n, nbh, c // tc), in_specs=[half_spec] + ([] if sib is None else [spec]), out_specs=spec)
    return pl.pallas_call(
        body, name=name, grid_spec=grid_spec, out_shape=jax.ShapeDtypeStruct((n, r, c), BF16),
        compiler_params=_params(("parallel", "parallel", "parallel")),
    )(which, full, *([] if sib is None else [sib]))


def _sum_chips(recv, name):
    _, r, c = recv.shape
    tr, tc = _elementwise_tiles(r, c)

    def body(x_ref, o_ref):
        acc = x_ref[0].astype(F32)
        for j in range(1, N_CHIP):
            acc = acc + x_ref[j].astype(F32)
        o_ref[...] = acc

    return pl.pallas_call(
        body, name=name, grid=(r // tr, c // tc), in_specs=[pl.BlockSpec((N_CHIP, tr, tc), lambda i, j: (0, i, j))],
        out_specs=pl.BlockSpec((tr, tc), lambda i, j: (i, j)), out_shape=jax.ShapeDtypeStruct((r, c), F32),
        compiler_params=_params(("parallel", "parallel")),
    )(recv)


def _adamw(w, g_parts, m, v, name, by_core=False):
    r, c = w.shape
    tr, tc = _elementwise_tiles(r // 2 if by_core else r, c)
    n_g = len(g_parts)
    nbh = (r // 2) // tr
    c1 = 1.0 / (1.0 - ADAM_B1 ** ADAM_STEP)
    c2 = 1.0 / (1.0 - ADAM_B2 ** ADAM_STEP)

    def body(*refs):
        w_ref, g_refs, m_ref, v_ref = refs[0], refs[1:1 + n_g], refs[1 + n_g], refs[2 + n_g]
        g_out, d_out, m_out, v_out = refs[3 + n_g:]
        if by_core:
            in_my_half = (pl.program_id(0) // nbh) == lax.axis_index("c")
            g = jnp.where(in_my_half, g_refs[0][...], g_refs[1][...])
        else:
            g = g_refs[0][...]
        mn = ADAM_B1 * m_ref[...] + (1.0 - ADAM_B1) * g
        vn = ADAM_B2 * v_ref[...] + (1.0 - ADAM_B2) * (g * g)
        g_out[...] = g
        m_out[...] = mn
        v_out[...] = vn
        d_out[...] = -ADAM_LR * ((mn * c1) / (jnp.sqrt(vn * c2) + ADAM_EPS) + ADAM_WD * w_ref[...])

    spec = pl.BlockSpec((tr, tc), lambda i, j: (i, j))
    g_spec = pl.BlockSpec((tr, tc), lambda i, j: (i % nbh, j)) if by_core else spec
    return pl.pallas_call(
        body, name=name, grid=(r // tr, c // tc), in_specs=[spec] + [g_spec] * n_g + [spec] * 2, out_specs=[spec] * 4,
        out_shape=[jax.ShapeDtypeStruct((r, c), F32)] * 4, compiler_params=_params(("parallel", "parallel")),
    )(w, *g_parts, m, v)


def _conv_taps(u, t):
    rows = lax.broadcasted_iota(jnp.int32, u.shape, 0)
    return [u] + [jnp.where(rows >= dd, pltpu.roll(u, dd, 0), 0.0) for dd in range(1, CONV_K)]


def _conv_fwd(proj, conv_w, col0, name):
    t = proj.shape[0]
    ch = conv_w.shape[1]

    def body(u_ref, w_ref, o_ref):
        taps = _conv_taps(u_ref[...], t)
        wv = w_ref[...]
        y = taps[0] * wv[CONV_K - 1:CONV_K]
        for dd in range(1, CONV_K):
            y = y + taps[dd] * wv[CONV_K - 1 - dd:CONV_K - dd]
        o_ref[...] = _silu(y)

    return pl.pallas_call(
        body, name=name, grid=(ch // LANES,),
        in_specs=[pl.BlockSpec((t, LANES), lambda j: (0, col0 + j)), pl.BlockSpec((CONV_K, LANES), lambda j: (0, j))],
        out_specs=pl.BlockSpec((t, LANES), lambda j: (0, j)), out_shape=jax.ShapeDtypeStruct((t, ch), F32),
        compiler_params=_params(("parallel",)),
    )(proj, conv_w)


def _conv_bwd(proj, conv_w, ds, col0, name):
    t = proj.shape[0]
    ch = conv_w.shape[1]

    def body(u_ref, w_ref, ds_ref, du_ref, dw_ref):
        u = u_ref[...]
        taps = _conv_taps(u, t)
        wv = w_ref[...]
        y = taps[0] * wv[CONV_K - 1:CONV_K]
        for dd in range(1, CONV_K):
            y = y + taps[dd] * wv[CONV_K - 1 - dd:CONV_K - dd]
        sg = jax.nn.sigmoid(y)
        dy = ds_ref[...] * (sg * (1.0 + y * (1.0 - sg)))
        rows = lax.broadcasted_iota(jnp.int32, u.shape, 0)
        du = dy * wv[CONV_K - 1:CONV_K]
        for dd in range(1, CONV_K):
            ahead = jnp.where(rows < t - dd, pltpu.roll(dy, t - dd, 0), 0.0)
            du = du + ahead * wv[CONV_K - 1 - dd:CONV_K - dd]
        du_ref[...] = du.astype(du_ref.dtype)
        dws = [jnp.sum(dy * taps[CONV_K - 1 - j], axis=0, keepdims=True) for j in range(CONV_K)]
        dw_ref[...] = jnp.concatenate(dws, axis=0)

    return pl.pallas_call(
        body, name=name, grid=(ch // LANES,),
        in_specs=[pl.BlockSpec((t, LANES), lambda j: (0, col0 + j)), pl.BlockSpec((CONV_K, LANES), lambda j: (0, j)),
                  pl.BlockSpec((t, LANES), lambda j: (0, j))],
        out_specs=[pl.BlockSpec((t, LANES), lambda j: (0, j)), pl.BlockSpec((CONV_K, LANES), lambda j: (0, j))],
        out_shape=[jax.ShapeDtypeStruct((t, ch), BF16), jax.ShapeDtypeStruct((CONV_K, ch), F32)],
        compiler_params=_params(("parallel",)),
    )(proj, conv_w, ds)


def _hg_block(st, q, fl, vi, g, l0, l1, nw):
    hs = range(len(st))
    tb = q[0].shape[0]
    ln = HG_SUB
    lb = [jax.nn.sigmoid(l0[h] - l1[h]) for h in hs]
    rows = lax.broadcasted_iota(jnp.int32, (ln, HEAD), 0)
    tri = (lax.broadcasted_iota(jnp.int32, (ln, ln), 0) >= lax.broadcasted_iota(jnp.int32, (ln, ln), 1)).astype(F32)
    st = list(st)
    outs = [[] for _ in hs]
    for i in range(tb // ln):
        sl = slice(i * ln, (i + 1) * ln)
        qs, vs = [q[h][sl] for h in hs], [vi[h][sl] for h in hs]
        f = [lb[h] + (1.0 - lb[h]) * jax.nn.sigmoid(fl[h][sl]) for h in hs]
        k = [1.0 - f[h] for h in hs]
        b = [jnp.dot(tri, jnp.log(f[h]), precision=HI, preferred_element_type=F32) for h in hs]
        o = [lax.dot_general((qs[h] * jnp.exp(b[h])).astype(BF16), st[h].astype(BF16), NT_DIMS, preferred_element_type=F32)
             for h in hs]
        for s in range(ln):
            e = [jnp.exp(jnp.where(rows >= s, b[h] - b[h][s:s + 1], -1e30)) for h in hs]
            a = [jnp.sum(qs[h] * e[h] * k[h][s:s + 1], axis=-1, keepdims=True) for h in hs]
            o = [o[h] + a[h] * vs[h][s:s + 1] for h in hs]
        kt = [k[h] * jnp.exp(b[h][ln - 1:ln] - b[h]) for h in hs]
        upd = [lax.dot_general(vs[h].astype(BF16), kt[h].astype(BF16), TN_DIMS, preferred_element_type=F32) for h in hs]
        st = [st[h] * jnp.exp(b[h][ln - 1:ln]) + upd[h] for h in hs]
        for h in hs:
            outs[h].append(o[h])
    o = [jnp.concatenate(outs[h], axis=0) for h in hs]
    out = [o[h] * _rms_scale(o[h]) * nw * _silu(g[h]) for h in hs]
    return st, out


def _head_cols(h):
    return slice(h * HEAD, (h + 1) * HEAD)


def _head_groups(n_heads, group):
    g = min(group, n_heads)
    return [list(range(i, min(i + g, n_heads))) for i in range(0, n_heads, g)]


def _hg_in_specs(n_heads, tb, time_index):
    hw = n_heads * HEAD
    cols = [pl.BlockSpec((tb, hw), functools.partial(lambda part, j: (time_index(j), part), part)) for part in range(4)]
    head_rows = pl.BlockSpec((n_heads, 1, HEAD), lambda j: (0, 0, 0))
    return cols + [head_rows, head_rows, pl.BlockSpec((1, HEAD), lambda j: (0, 0))]


def _hgrn2_fwd(proj, l0, l1, nw, n_heads, name):
    t = proj.shape[0]
    hw = n_heads * HEAD
    tb = _tile(t, HG_BLOCK, HG_SUB)
    nb = t // tb

    def body(q_ref, f_ref, i_ref, g_ref, l0_ref, l1_ref, nw_ref, o_ref, save_ref, st_ref):
        @pl.when(pl.program_id(0) == 0)
        def _():
            st_ref[...] = jnp.zeros_like(st_ref)

        for hs in _head_groups(n_heads, HG_FWD_GROUP):
            st = [st_ref[h] for h in hs]
            for h, s in zip(hs, st):
                save_ref[h] = s
            st, out = _hg_block(st, *[[r[:, _head_cols(h)] for h in hs] for r in (q_ref, f_ref, i_ref, g_ref)],
                                [l0_ref[h] for h in hs], [l1_ref[h] for h in hs], nw_ref[...])
            for h, s, o in zip(hs, st, out):
                st_ref[h] = s
                o_ref[:, _head_cols(h)] = o.astype(o_ref.dtype)

    return pl.pallas_call(
        body, name=name, grid=(nb,), in_specs=_hg_in_specs(n_heads, tb, lambda j: j),
        out_specs=[pl.BlockSpec((tb, hw), lambda j: (j, 0)),
                   pl.BlockSpec((None, n_heads, HEAD, HEAD), lambda j: (j, 0, 0, 0))],
        out_shape=[jax.ShapeDtypeStruct((t, hw), BF16), jax.ShapeDtypeStruct((nb, n_heads, HEAD, HEAD), F32)],
        scratch_shapes=[pltpu.VMEM((n_heads, HEAD, HEAD), F32)], compiler_params=_params(("arbitrary",)),
    )(proj, proj, proj, proj, l0, l1, nw)


def _hgrn2_bwd(proj, l0, l1, nw, saved, d_ocat, n_heads, name):
    t = proj.shape[0]
    tb = _tile(t, HG_BLOCK, HG_SUB)
    nb = t // tb
    rev = lambda j: nb - 1 - j

    hw = n_heads * HEAD

    def body(q_ref, f_ref, i_ref, g_ref, l0_ref, l1_ref, nw_ref, save_ref, do_ref,
             dp_ref, dl0_ref, dl1_ref, dnw_ref, dst_ref):
        @pl.when(pl.program_id(0) == 0)
        def _():
            dst_ref[...] = jnp.zeros_like(dst_ref)
            dl0_ref[...] = jnp.zeros_like(dl0_ref)
            dl1_ref[...] = jnp.zeros_like(dl1_ref)
            dnw_ref[...] = jnp.zeros_like(dnw_ref)

        dnw_acc = jnp.zeros((1, HEAD), F32)
        for hs in _head_groups(n_heads, HG_BWD_GROUP):
            _, vjp = jax.vjp(_hg_block, [save_ref[h] for h in hs],
                             *[[r[:, _head_cols(h)] for h in hs] for r in (q_ref, f_ref, i_ref, g_ref)],
                             [l0_ref[h] for h in hs], [l1_ref[h] for h in hs], nw_ref[...])
            dst, dq, df, di, dg, dl0, dl1, dnw = vjp(([dst_ref[h] for h in hs], [do_ref[:, _head_cols(h)] for h in hs]))
            for i, h in enumerate(hs):
                dst_ref[h] = dst[i]
                for part, val in enumerate((dq, df, di, dg)):
                    dp_ref[:, part * hw + h * HEAD:part * hw + (h + 1) * HEAD] = val[i].astype(dp_ref.dtype)
                dl0_ref[h] += dl0[i]
                dl1_ref[h] += dl1[i]
            dnw_acc = dnw_acc + dnw
        dnw_ref[...] += dnw_acc

    head_rows = pl.BlockSpec((n_heads, 1, HEAD), lambda j: (0, 0, 0))
    return pl.pallas_call(
        body, name=name, grid=(nb,),
        in_specs=_hg_in_specs(n_heads, tb, rev) + [pl.BlockSpec((None, n_heads, HEAD, HEAD), lambda j: (rev(j), 0, 0, 0)),
                                                   pl.BlockSpec((tb, hw), lambda j: (rev(j), 0))],
        out_specs=[pl.BlockSpec((tb, 4 * hw), lambda j: (rev(j), 0)), head_rows, head_rows,
                   pl.BlockSpec((1, HEAD), lambda j: (0, 0))],
        out_shape=[jax.ShapeDtypeStruct((t, 4 * hw), BF16)] + [jax.ShapeDtypeStruct((n_heads, 1, HEAD), F32)] * 2
        + [jax.ShapeDtypeStruct((1, HEAD), F32)],
        scratch_shapes=[pltpu.VMEM((n_heads, HEAD, HEAD), F32)], compiler_params=_params(("arbitrary",)),
    )(proj, proj, proj, proj, l0, l1, nw, saved, d_ocat)


NN_DIMS = (((1,), (0,)), ((), ()))


def _split_bf16(x):
    hi = x.astype(BF16)
    return hi, (x - hi.astype(F32)).astype(BF16)


def _dot3(a, b, dims=NN_DIMS):
    (ah, al), (bh, bl) = _split_bf16(a), _split_bf16(b)
    dot = functools.partial(lax.dot_general, dimension_numbers=dims, preferred_element_type=F32)
    return dot(ah, bh) + dot(ah, bl) + dot(al, bh)


@jax.custom_vjp
def _mm3(a, b):
    return _dot3(a, b)


def _mm3_fwd(a, b):
    return _dot3(a, b), (a, b)


def _mm3_bwd(res, g):
    a, b = res
    return _dot3(g, b, NT_DIMS), _dot3(a, g, TN_DIMS)


_mm3.defvjp(_mm3_fwd, _mm3_bwd)


def _dot_bf16(a, b, dims=(((1,), (0,)), ((), ()))):
    return lax.dot_general(a.astype(BF16), b.astype(BF16), dims, preferred_element_type=F32)


def _inv_unit_lower_raw(ms):
    hs = range(len(ms))
    c = ms[0].shape[0]
    r = lax.broadcasted_iota(jnp.int32, (c, c), 0)
    q = lax.broadcasted_iota(jnp.int32, (c, c), 1)
    eye = (r == q).astype(F32)
    md = [jnp.where((r // GDN_INV_BLOCK) == (q // GDN_INV_BLOCK), ms[h], 0.0) for h in hs]
    p = [-md[h] for h in hs]
    t16 = [eye + p[h] for h in hs]
    for _ in range(int(math.log2(GDN_INV_BLOCK)) - 1):
        p = [_dot3(p[h], p[h]) for h in hs]
        t16 = [t16[h] + _dot3(t16[h], p[h]) for h in hs]
    p = [-_dot3(t16[h], ms[h] - md[h]) for h in hs]
    t2 = [eye + p[h] for h in hs]
    for _ in range(int(math.log2(c // GDN_INV_BLOCK)) - 1):
        p = [_dot3(p[h], p[h]) for h in hs]
        t2 = [t2[h] + _dot3(t2[h], p[h]) for h in hs]
    return [_dot3(t2[h], t16[h]) for h in hs]


@jax.custom_vjp
def _inv_unit_lower(ms):
    return _inv_unit_lower_raw(ms)


def _inv_fwd(ms):
    ts = _inv_unit_lower_raw(ms)
    return ts, ts


def _inv_bwd(ts, dts):
    hs = range(len(ts))
    inner = [_dot3(ts[h], dts[h], TN_DIMS) for h in hs]
    return ([-_dot3(inner[h], ts[h], NT_DIMS) for h in hs],)


_inv_unit_lower.defvjp(_inv_fwd, _inv_bwd)


def _gdn_block(precise, onehots, st, qc, kc, vc, g, ab, alog_row, dtb_row, nw):
    inverse, dot3 = precise
    hs = range(len(st))
    c = qc[0].shape[0]
    lane_sum = lambda v: jnp.sum(v, axis=-1, keepdims=True)
    a = [lane_sum(ab * onehots[h][0]) for h in hs]
    bb = [lane_sum(ab * onehots[h][1]) for h in hs]
    alog = [lane_sum(alog_row * onehots[h][0]) for h in hs]
    dtb = [lane_sum(dtb_row * onehots[h][0]) for h in hs]
    la = [-jnp.exp(alog[h]) * _softplus(a[h] + dtb[h]) for h in hs]
    beta = [jax.nn.sigmoid(bb[h]) for h in hs]
    q = [qc[h] * lax.rsqrt(lane_sum(qc[h] * qc[h]) + EPS) * (HEAD ** -0.5) for h in hs]
    k = [kc[h] * lax.rsqrt(lane_sum(kc[h] * kc[h]) + EPS) for h in hs]
    r = lax.broadcasted_iota(jnp.int32, (c, c), 0)
    s = lax.broadcasted_iota(jnp.int32, (c, c), 1)
    tri = (r >= s).astype(F32)
    g_cc = [dot3(tri, jnp.broadcast_to(la[h], (c, c))) for h in hs]
    g_cl = [dot3(tri, jnp.broadcast_to(la[h], (c, HEAD))) for h in hs]
    gamma = [jnp.exp(jnp.where(r >= s, g_cc[h] - g_cc[h].T, -1e30)) for h in hs]
    kk = [_dot_bf16(k[h], k[h], NT_DIMS) for h in hs]
    m = [jnp.where(r > s, beta[h] * kk[h] * gamma[h], 0.0) for h in hs]
    tm = inverse(m)
    eg = [jnp.exp(g_cl[h]) for h in hs]
    rhs = [jnp.concatenate([vc[h] * beta[h], k[h] * (beta[h] * eg[h])], axis=1) for h in hs]
    sol = [dot3(tm[h], rhs[h]) for h in hs]
    qk = [_dot_bf16(q[h], k[h], NT_DIMS) * gamma[h] for h in hs]
    g_last = [g_cl[h][c - 1:c] for h in hs]
    k_tail = [k[h] * jnp.exp(g_last[h] - g_cl[h]) for h in hs]
    v_new = [sol[h][:, :HEAD] - _dot_bf16(sol[h][:, HEAD:], st[h], NT_DIMS) for h in hs]
    o_st = [_dot_bf16(q[h] * eg[h], st[h], NT_DIMS) for h in hs]
    o = [o_st[h] + _dot_bf16(qk[h], v_new[h]) for h in hs]
    upd = [_dot_bf16(v_new[h], k_tail[h], TN_DIMS) for h in hs]
    st = [st[h] * jnp.exp(g_last[h]) + upd[h] for h in hs]
    out = [o[h] * _rms_scale(o[h]) * nw * _silu(g[h]) for h in hs]
    return st, out


def _head_onehots(n_heads, h):
    lane = lax.broadcasted_iota(jnp.int32, (1, LANES), 1)
    return (lane == h).astype(F32), (lane == n_heads + h).astype(F32)


def _gdn_in_specs(n_heads, c, time_index):
    hw = n_heads * HEAD
    qkv = [pl.BlockSpec((c, hw), functools.partial(lambda part, j: (time_index(j), part), part)) for part in range(3)]
    row = pl.BlockSpec((1, LANES), lambda j: (0, 0))
    return qkv + [pl.BlockSpec((c, hw), lambda j: (time_index(j), 7)),
                  pl.BlockSpec((c, LANES), lambda j: (time_index(j), 8 * n_heads)), row, row, row]


def _gdn_fwd(qkv, proj, alog_row, dtb_row, nw, n_heads, name):
    t = qkv.shape[0]
    hw = n_heads * HEAD
    c = _tile(t, GDN_CHUNK, GDN_CHUNK)
    nb = t // c

    def body(q_ref, k_ref, v_ref, g_ref, ab_ref, al_ref, dt_ref, nw_ref, o_ref, save_ref, st_ref):
        @pl.when(pl.program_id(0) == 0)
        def _():
            st_ref[...] = jnp.zeros_like(st_ref)

        for hs in _head_groups(n_heads, GDN_FWD_GROUP):
            st = [st_ref[h] for h in hs]
            for h, s in zip(hs, st):
                save_ref[h] = s
            st, out = _gdn_block((_inv_unit_lower_raw, _dot3), [_head_onehots(n_heads, h) for h in hs], st,
                                 *[[r[:, _head_cols(h)] for h in hs] for r in (q_ref, k_ref, v_ref, g_ref)],
                                 ab_ref[...], al_ref[...], dt_ref[...], nw_ref[...])
            for h, s, o in zip(hs, st, out):
                st_ref[h] = s
                o_ref[:, _head_cols(h)] = o.astype(o_ref.dtype)

    return pl.pallas_call(
        body, name=name, grid=(nb,), in_specs=_gdn_in_specs(n_heads, c, lambda j: j),
        out_specs=[pl.BlockSpec((c, hw), lambda j: (j, 0)),
                   pl.BlockSpec((None, n_heads, HEAD, HEAD), lambda j: (j, 0, 0, 0))],
        out_shape=[jax.ShapeDtypeStruct((t, hw), BF16), jax.ShapeDtypeStruct((nb, n_heads, HEAD, HEAD), F32)],
        scratch_shapes=[pltpu.VMEM((n_heads, HEAD, HEAD), F32)], compiler_params=_params(("arbitrary",)),
    )(qkv, qkv, qkv, proj, proj, alog_row, dtb_row, nw)


def _gdn_bwd(qkv, proj, alog_row, dtb_row, nw, saved, d_ocat, n_heads, name):
    t = qkv.shape[0]
    c = _tile(t, GDN_CHUNK, GDN_CHUNK)
    nb = t // c
    rev = lambda j: nb - 1 - j

    hw = n_heads * HEAD

    def body(q_ref, k_ref, v_ref, g_ref, ab_ref, al_ref, dt_ref, nw_ref, save_ref, do_ref,
             dqkv_ref, dg_ref, dab_ref, dal_ref, ddt_ref, dnw_ref, dst_ref):
        @pl.when(pl.program_id(0) == 0)
        def _():
            dst_ref[...] = jnp.zeros_like(dst_ref)
            dal_ref[...] = jnp.zeros_like(dal_ref)
            ddt_ref[...] = jnp.zeros_like(ddt_ref)
            dnw_ref[...] = jnp.zeros_like(dnw_ref)

        dab_acc = jnp.zeros((c, LANES), F32)
        row_acc = [jnp.zeros((1, LANES), F32)] * 3
        for hs in _head_groups(n_heads, GDN_BWD_GROUP):
            fn = functools.partial(_gdn_block, (_inv_unit_lower, _mm3), [_head_onehots(n_heads, h) for h in hs])
            _, vjp = jax.vjp(fn, [save_ref[h] for h in hs],
                             *[[r[:, _head_cols(h)] for h in hs] for r in (q_ref, k_ref, v_ref, g_ref)],
                             ab_ref[...], al_ref[...], dt_ref[...], nw_ref[...])
            dst, dq, dk, dv, dg, dab, dal, ddt, dnw = vjp(([dst_ref[h] for h in hs], [do_ref[:, _head_cols(h)] for h in hs]))
            for i, h in enumerate(hs):
                dst_ref[h] = dst[i]
                for part, val in enumerate((dq, dk, dv)):
                    dqkv_ref[:, part * hw + h * HEAD:part * hw + (h + 1) * HEAD] = val[i]
                dg_ref[:, _head_cols(h)] = dg[i].astype(dg_ref.dtype)
            dab_acc = dab_acc + dab
            row_acc = [acc + val for acc, val in zip(row_acc, (dal, ddt, dnw))]
        dab_ref[...] = dab_acc
        dal_ref[...] += row_acc[0]
        ddt_ref[...] += row_acc[1]
        dnw_ref[...] += row_acc[2]

    row = pl.BlockSpec((1, LANES), lambda j: (0, 0))
    return pl.pallas_call(
        body, name=name, grid=(nb,),
        in_specs=_gdn_in_specs(n_heads, c, rev) + [pl.BlockSpec((None, n_heads, HEAD, HEAD), lambda j: (rev(j), 0, 0, 0)),
                                                   pl.BlockSpec((c, hw), lambda j: (rev(j), 1))],
        out_specs=[pl.BlockSpec((c, 3 * hw), lambda j: (rev(j), 0)), pl.BlockSpec((c, hw), lambda j: (rev(j), 0)),
                   pl.BlockSpec((c, LANES), lambda j: (rev(j), 0)), row, row, row],
        out_shape=[jax.ShapeDtypeStruct((t, 3 * hw), F32), jax.ShapeDtypeStruct((t, hw), BF16),
                   jax.ShapeDtypeStruct((t, LANES), F32)] + [jax.ShapeDtypeStruct((1, LANES), F32)] * 3,
        scratch_shapes=[pltpu.VMEM((n_heads, HEAD, HEAD), F32)], compiler_params=_params(("arbitrary",)),
    )(qkv, qkv, qkv, proj, proj, alog_row, dtb_row, nw, saved, d_ocat)


def _pad_lanes(v, n):
    v = v.reshape(1, -1)
    return jnp.pad(v, ((0, 0), (0, n - v.shape[1])))


def _pack_rows(vecs):
    flat = jnp.concatenate([v.reshape(-1) for v in vecs])
    offs, o = [], 0
    for v in vecs:
        offs.append((o, v.size))
        o += v.size
    per_row = -(-o // (SUBLANES * LANES)) * LANES
    flat = jnp.pad(flat, (0, SUBLANES * per_row - o))
    return flat.reshape(SUBLANES, per_row), offs


def _unpack(gathered, offs):
    per_dev = gathered.reshape(N_DEV, -1)
    return [per_dev[:, o:o + n] for o, n in offs]


def _sum_devices(part):
    acc = part[0]
    for i in range(1, N_DEV):
        acc = acc + part[i]
    return acc


def kernel(x, c, w_ada, b_ada, pre_mix_norm, post_mix_norm, pre_ffn_norm, post_ffn_norm, w_in, hg_lb_logits, hg_norm, gdn_conv_w, gdn_a_log, gdn_dt_bias, gdn_norm, w_out, w_ff1, w_ff2, loss_target, m_w_ada, m_b_ada, m_pre_mix_norm, m_post_mix_norm, m_pre_ffn_norm, m_post_ffn_norm, m_w_in, m_hg_lb_logits, m_hg_norm, m_gdn_conv_w, m_gdn_a_log, m_gdn_dt_bias, m_gdn_norm, m_w_out, m_w_ff1, m_w_ff2, v_w_ada, v_b_ada, v_pre_mix_norm, v_post_mix_norm, v_pre_ffn_norm, v_post_ffn_norm, v_w_in, v_hg_lb_logits, v_hg_norm, v_gdn_conv_w, v_gdn_a_log, v_gdn_dt_bias, v_gdn_norm, v_w_out, v_w_ff1, v_w_ff2):
    assert x.shape[0] == 1 and w_ada.shape[0] == 1 and hg_lb_logits.shape[0] == 2
    t, d = x.shape[1], x.shape[2]
    n_heads = (d // 2) // HEAD
    hw = n_heads * HEAD
    in_cols = 8 * hw + 2 * n_heads
    np_cols = 8 * hw + 2 * LANES
    d_ff = w_ff1.shape[2] * N_CHIP
    na = w_ada.shape[2]
    ax, ay, ac = lax.axis_index("x"), lax.axis_index("y"), lax.axis_index("c")
    chip = 2 * ax + ay
    dev = 4 * ax + 2 * ay + ac

    x2d, tgt = x[0], loss_target[0]

    pack1, offs1 = _pack_rows([c[0], gdn_conv_w[0]])
    c_all, convw_all = _unpack(_gather8(pack1, "gather_cond"), offs1)
    conv_sh = gdn_conv_w.shape[2]
    conv_w = jnp.concatenate([convw_all[2 * j].reshape(CONV_K, conv_sh) for j in range(N_CHIP)], axis=1)

    b_s = lax.dynamic_slice(b_ada, (0, chip * na), (1, na))
    mod_part = _mod_part(c_all, w_ada[0], b_s, "mod_part")
    pack2, offs2 = _pack_rows([mod_part])
    (mod_parts,) = _unpack(_gather8(pack2, "gather_mod"), offs2)
    mod_all = jnp.concatenate([mod_parts[2 * j].reshape(N_DEV, na) for j in range(N_CHIP)], axis=1)
    mod = lax.dynamic_slice(mod_all, (dev, 0), (1, N_MOD * d))
    sh_m, sc_m, gt_m, sh_f, sc_f, gt_f = [mod[:, i * d:(i + 1) * d] for i in range(N_MOD)]

    (g_in,) = _gather_weights([w_in[0].astype(BF16)], "gather_w_in")
    g_out, g_ff1, g_ff2 = _sequencer_gather_weights(
        [w_out[0].astype(BF16), w_ff1[0].astype(BF16), w_ff2[0].astype(BF16)], "gather_weights_late", 1)
    w_in_f = jnp.pad(jnp.transpose(g_in, (1, 0, 2)).reshape(d, in_cols), ((0, 0), (0, np_cols - in_cols)))
    w_out_f = g_out.reshape(d, d)
    w_ff2_f = g_ff2.reshape(d_ff, d)

    h1 = _norm_mod(x2d, pre_mix_norm, sc_m, sh_m, "norm_mod_mix")
    proj = _matmul(h1, w_in_f, "nn", F32, "mm_in", tn=768)
    l0, l1 = hg_lb_logits[0].reshape(n_heads, 1, HEAD), hg_lb_logits[1].reshape(n_heads, 1, HEAD)
    o_hg, hg_saved = _hgrn2_fwd(proj, l0, l1, hg_norm, n_heads, "hgrn2_fwd")
    qkv = _conv_fwd(proj, conv_w, 4 * n_heads, "conv_fwd")
    alog_row, dtb_row = _pad_lanes(gdn_a_log, LANES), _pad_lanes(gdn_dt_bias, LANES)
    o_gdn, gdn_saved = _gdn_fwd(qkv, proj, alog_row, dtb_row, gdn_norm, n_heads, "gdn_fwd")
    o_cat = jnp.concatenate([o_hg, o_gdn], axis=1)
    y1 = _matmul(o_cat, w_out_f, "nn", F32, "mm_out")
    x_mid = _resid(x2d, y1, post_mix_norm, gt_m, "resid_mix")

    h2 = _norm_mod(x_mid, pre_ffn_norm, sc_f, sh_f, "norm_mod_ffn")
    relu_a1, r1 = _matmul(h2, g_ff1, "nn", BF16, "mm_ff1", relu2=True, b_split=True)
    y2 = _matmul(r1, w_ff2_f, "nn", F32, "mm_ff2")
    d_out, loss_row = _loss_head(x_mid, y2, post_ffn_norm, gt_f, tgt, "loss_head")

    in_sh = in_cols // N_CHIP
    ff_sh = d_ff // N_CHIP
    my_half = jnp.reshape(ac, (1,)).astype(jnp.int32)

    def start_reduce(by_chip, tag, collective_id):
        to_sib = [_row_half_to_bf16(a, 1 - my_half, None, f"sibling_half_{tag}{i}") for i, a in enumerate(by_chip)]
        from_sib = _sibling_exchange(to_sib, f"sibling_partials_{tag}")
        chip_part = [_row_half_to_bf16(a, my_half, s, f"add_halves_{tag}{i}") for i, (a, s) in enumerate(zip(by_chip, from_sib))]
        return _sequencer_chip_exchange(chip_part, f"scatter_grads_{tag}", collective_id)

    dy2, d_gt_f, d_post_ffn = _resid_bwd(d_out, y2, post_ffn_norm, gt_f, "resid_ffn_bwd")
    gw_ff2 = _matmul(r1, dy2, "tn", F32, "mm_ff2_dw")
    da1 = _matmul(dy2, w_ff2_f, "nt", BF16, "mm_ff2_dx", times=relu_a1)
    gw_ff1 = _matmul(h2, da1, "tn", F32, "mm_ff1_dw", out_split=True)
    recv_ff2, recv_ff1 = start_reduce([gw_ff2.reshape(N_CHIP, ff_sh, d), gw_ff1], "ff", 2)
    dh2 = _matmul(da1, g_ff1, "nt", BF16, "mm_ff1_dx", b_split=True)
    d_mid, d_pre_ffn, d_sc_f, d_sh_f = _norm_mod_bwd(x_mid, pre_ffn_norm, sc_f, dh2, d_out, "norm_mod_ffn_bwd")

    dy1, d_gt_m, d_post_mix = _resid_bwd(d_mid, y1, post_mix_norm, gt_m, "resid_mix_bwd")
    gw_out = _matmul(o_cat, dy1, "tn", F32, "mm_out_dw")
    (recv_out,) = start_reduce([gw_out.reshape(N_CHIP, d // N_CHIP, d)], "out", 3)
    d_ocat = _matmul(dy1, w_out_f, "nt", F32, "mm_out_dx")
    dp_hg, dl0, dl1, d_hg_norm = _hgrn2_bwd(proj, l0, l1, hg_norm, hg_saved, d_ocat, n_heads, "hgrn2_bwd")
    dqkv, dg_g, dab, d_alog, d_dtb, d_gdn_norm = _gdn_bwd(
        qkv, proj, alog_row, dtb_row, gdn_norm, gdn_saved, d_ocat, n_heads, "gdn_bwd")
    du, d_conv_w = _conv_bwd(proj, conv_w, dqkv, 4 * n_heads, "conv_bwd")
    dproj = jnp.concatenate([dp_hg, du, dg_g, dab.astype(BF16), jnp.zeros((t, LANES), BF16)], axis=1)
    gw_in = _matmul(h1, dproj, "tn", F32, "mm_in_dw", tn=768)
    (recv_in,) = start_reduce([jnp.transpose(gw_in[:, :in_cols].reshape(d, N_CHIP, in_sh), (1, 0, 2))], "in", 4)
    dh1 = _matmul(dproj, w_in_f, "nt", BF16, "mm_in_dx", tk=2816)
    grad_x, d_pre_mix, d_sc_m, d_sh_m = _norm_mod_bwd(x2d, pre_mix_norm, sc_m, dh1, d_mid, "norm_mod_mix_bwd")

    d_mod = jnp.concatenate([d_sh_m, d_sc_m, d_gt_m, d_sh_f, d_sc_f, d_gt_f], axis=1)
    d_lb_logits = jnp.stack([dl0.reshape(n_heads, HEAD), dl1.reshape(n_heads, HEAD)])
    pack3, offs3 = _pack_rows([loss_row[0, :1], d_pre_mix, d_post_mix, d_pre_ffn, d_post_ffn, d_lb_logits, d_hg_norm,
                               d_conv_w, d_alog[0, :n_heads], d_dtb[0, :n_heads], d_gdn_norm, d_mod])
    parts = _unpack(_gather8(pack3, "gather_vec_grads"), offs3)
    sums = [_sum_devices(p) for p in parts[:-1]]
    loss = sums[0][0]
    dmod_all = parts[-1]
    g_b_ada = _sum_devices(dmod_all).reshape(1, N_MOD * d)
    g_conv_full = sums[7].reshape(CONV_K, N_CHIP * conv_sh)
    g_conv = lax.dynamic_slice(g_conv_full, (0, chip * conv_sh), (CONV_K, conv_sh))
    gw_ada = _wada_grad(c_all, lax.dynamic_slice(dmod_all, (0, chip * na), (N_DEV, na)), "wada_grad")

    mine = [_sum_chips(rv, f"sum_chips_{i}") for i, rv in enumerate([recv_in, recv_out, recv_ff1, recv_ff2])]
    theirs = _sibling_exchange(mine, "sibling_grads")

    big = {}
    for i, (nm, w_, m_, v_) in enumerate([("w_in", w_in, m_w_in, v_w_in), ("w_out", w_out, m_w_out, v_w_out),
                                          ("w_ff1", w_ff1, m_w_ff1, v_w_ff1), ("w_ff2", w_ff2, m_w_ff2, v_w_ff2)]):
        big[nm] = [o[None] for o in _adamw(w_[0], [mine[i], theirs[i]], m_[0], v_[0], f"adamw_{nm}", by_core=True)]
    big["w_ada"] = [o[None] for o in _adamw(w_ada[0], [gw_ada], m_w_ada[0], v_w_ada[0], "adamw_w_ada")]

    small_names = ["b_ada", "pre_mix_norm", "post_mix_norm", "pre_ffn_norm", "post_ffn_norm", "hg_lb_logits", "hg_norm",
                   "gdn_conv_w", "gdn_a_log", "gdn_dt_bias", "gdn_norm"]
    small_w = [b_ada, pre_mix_norm, post_mix_norm, pre_ffn_norm, post_ffn_norm, hg_lb_logits, hg_norm, gdn_conv_w,
               gdn_a_log, gdn_dt_bias, gdn_norm]
    small_m = [m_b_ada, m_pre_mix_norm, m_post_mix_norm, m_pre_ffn_norm, m_post_ffn_norm, m_hg_lb_logits, m_hg_norm,
               m_gdn_conv_w, m_gdn_a_log, m_gdn_dt_bias, m_gdn_norm]
    small_v = [v_b_ada, v_pre_mix_norm, v_post_mix_norm, v_pre_ffn_norm, v_post_ffn_norm, v_hg_lb_logits, v_hg_norm,
               v_gdn_conv_w, v_gdn_a_log, v_gdn_dt_bias, v_gdn_norm]
    small_g = [g_b_ada, sums[1], sums[2], sums[3], sums[4], sums[5], sums[6], g_conv, sums[8], sums[9], sums[10]]
    pw, offs_s = _pack_rows(small_w)
    pg, _ = _pack_rows(small_g)
    pm, _ = _pack_rows(small_m)
    pv, _ = _pack_rows(small_v)
    packed = _adamw(pw, [pg], pm, pv, "adamw_vectors")
    small = {}
    for nm, w_, (o, n) in zip(small_names, small_w, offs_s):
        small[nm] = [p.reshape(-1)[o:o + n].reshape(w_.shape) for p in packed]

    order = ["w_ada", "b_ada", "pre_mix_norm", "post_mix_norm", "pre_ffn_norm", "post_ffn_norm", "w_in", "hg_lb_logits",
             "hg_norm", "gdn_conv_w", "gdn_a_log", "gdn_dt_bias", "gdn_norm", "w_out", "w_ff1", "w_ff2"]
    res = {**big, **small}
    outs = [loss, grad_x[None]]
    for k in range(4):
        outs += [res[nm][k] for nm in order]
    return tuple(outs)
```

```python
import functools
import math

import jax
import jax.numpy as jnp
from jax import lax
from jax.experimental import pallas as pl
from jax.experimental.pallas import tpu as pltpu
from jax.experimental.pallas import tpu_sc as plsc

F32 = jnp.float32
BF16 = jnp.bfloat16
HI = lax.Precision.HIGHEST
MESH = pl.DeviceIdType.MESH

LANES = 128
SUBLANES = 8
VMEM_LIMIT = 48 * 1024 * 1024
EPS = 1e-6
HEAD = 128
CONV_K = 4
GDN_CHUNK = 64
GDN_INV_BLOCK = 16
HG_SUB = 16
HG_BLOCK = 128
HG_FWD_GROUP = 8
HG_BWD_GROUP = 4
GDN_FWD_GROUP = 8
GDN_BWD_GROUP = 8
N_MOD = 6
N_DEV = 8
N_CHIP = 4

ADAM_LR = 0.001
ADAM_B1 = 0.9
ADAM_B2 = 0.999
ADAM_EPS = 1e-08
ADAM_WD = 0.01
ADAM_STEP = 10

NT_DIMS = (((1,), (1,)), ((), ()))
TN_DIMS = (((0,), (0,)), ((), ()))


def _tile(dim, target, align):
    if dim <= target:
        return dim
    best = dim
    t = align
    while t <= target:
        if dim % t == 0:
            best = t
        t += align
    return best


def _elementwise_tiles(r, c):
    tc = _tile(c, 1024, LANES)
    tr = _tile(r, max(16, (256 * 1024) // tc // 16 * 16), 16)
    return tr, tc


def _params(sem):
    return pltpu.CompilerParams(dimension_semantics=sem, vmem_limit_bytes=VMEM_LIMIT)


def _silu(x):
    return x * jax.nn.sigmoid(x)


def _softplus(x):
    pos = x > 0
    return jnp.where(pos, x, 0.0) + jnp.log(1.0 + jnp.exp(jnp.where(pos, -x, x)))


def _rms_scale(x):
    return lax.rsqrt(jnp.mean(x * x, axis=-1, keepdims=True) + EPS)


def _gather8(x_shard, name):
    m_per, n = x_shard.shape
    assert m_per % SUBLANES == 0 and n % LANES == 0

    def body(x_ref, out_ref, send_sems, recv_sems, local_sem):
        x, y, c = lax.axis_index("x"), lax.axis_index("y"), lax.axis_index("c")
        me, sibling = (x, y, c), (x, y, 1 - c)
        chips = [(1 - x, y), (x, 1 - y), (1 - x, 1 - y)]

        def rows(px, py, pc):
            return out_ref.at[pl.ds((4 * px + 2 * py + pc) * m_per, m_per), :]

        def copy(k, block, to, src=None):
            return pltpu.make_async_remote_copy(
                src_ref=rows(*block) if src is None else src, dst_ref=rows(*block),
                send_sem=send_sems.at[k], recv_sem=recv_sems.at[k], device_id=to, device_id_type=MESH)

        mine = pltpu.make_async_copy(x_ref, rows(*me), local_sem)
        mine.start()
        first = [copy(0, me, sibling, src=x_ref)]
        first += [copy(1 + j, me, (*chip, c), src=x_ref) for j, chip in enumerate(chips)]
        for cp in first:
            cp.start()
        passed = [copy(4 + j, (*chip, c), sibling) for j, chip in enumerate(chips)]
        for j, chip in enumerate(chips):
            copy(1 + j, (*chip, c), me).wait_recv()
            passed[j].start()
        copy(0, sibling, me).wait_recv()
        for j, chip in enumerate(chips):
            copy(4 + j, (*chip, 1 - c), me).wait_recv()
        for cp in first + passed:
            cp.wait_send()
        mine.wait()

    return pl.pallas_call(
        body, name=name,
        out_shape=jax.ShapeDtypeStruct((N_DEV * m_per, n), x_shard.dtype),
        in_specs=[pl.BlockSpec(memory_space=pltpu.VMEM)],
        out_specs=pl.BlockSpec(memory_space=pltpu.VMEM),
        scratch_shapes=[pltpu.SemaphoreType.DMA((7,)), pltpu.SemaphoreType.DMA((7,)), pltpu.SemaphoreType.DMA],
        compiler_params=pltpu.CompilerParams(vmem_limit_bytes=VMEM_LIMIT),
    )(x_shard)


def _gather_weights(arrs, name):
    n = len(arrs)
    out_shapes = [jax.ShapeDtypeStruct((N_CHIP,) + a.shape, a.dtype) for a in arrs]

    def body(*refs):
        ins, outs = refs[:n], refs[n:2 * n]
        ici_send, ici_recv, d2d_send, d2d_recv = refs[2 * n:]
        x, y, c = lax.axis_index("x"), lax.axis_index("y"), lax.axis_index("c")
        me = 2 * x + y
        sibling = (x, y, 1 - c)
        peers = [(1 - x, y), (x, 1 - y), (1 - x, 1 - y)]

        def half(a, cc):
            rh = arrs[a].shape[0] // 2
            return pl.ds(pl.multiple_of(cc * rh, 16), rh)

        sent = []
        for a in range(n):
            for k, (px, py) in enumerate(peers):
                cp = pltpu.make_async_remote_copy(
                    src_ref=ins[a].at[half(a, c)], dst_ref=outs[a].at[me, half(a, c)],
                    send_sem=ici_send.at[3 * a + k], recv_sem=ici_recv.at[3 * a + k],
                    device_id=(px, py, c), device_id_type=MESH)
                cp.start()
                sent.append(cp)
        for a in range(n):
            for k, (px, py) in enumerate(peers):
                landed = outs[a].at[2 * px + py, half(a, c)]
                pltpu.make_async_remote_copy(
                    src_ref=landed, dst_ref=landed, send_sem=ici_send.at[3 * a + k], recv_sem=ici_recv.at[3 * a + k],
                    device_id=(px, py, c), device_id_type=MESH).wait_recv()
                fwd = pltpu.make_async_remote_copy(
                    src_ref=landed, dst_ref=landed, send_sem=d2d_send.at[3 * a + k], recv_sem=d2d_recv.at[3 * a + k],
                    device_id=sibling, device_id_type=MESH)
                fwd.start()
                sent.append(fwd)
        for a in range(n):
            for k, (px, py) in enumerate(peers):
                passed = outs[a].at[2 * px + py, half(a, 1 - c)]
                pltpu.make_async_remote_copy(
                    src_ref=passed, dst_ref=passed, send_sem=d2d_send.at[3 * a + k], recv_sem=d2d_recv.at[3 * a + k],
                    device_id=sibling, device_id_type=MESH).wait_recv()
        for cp in sent:
            cp.wait_send()

    hbm = pl.BlockSpec(memory_space=pltpu.HBM)
    gathered = pl.pallas_call(
        body, name=name, out_shape=out_shapes, in_specs=[hbm] * n, out_specs=[hbm] * n,
        scratch_shapes=[pltpu.SemaphoreType.DMA((3 * n,))] * 4,
    )(*arrs)
    chip = 2 * lax.axis_index("x") + lax.axis_index("y")
    return [lax.dynamic_update_slice(g, a[None], (chip, 0, 0)) for g, a in zip(gathered, arrs)]


def _chip_peers():
    x, y, c = lax.axis_index("x"), lax.axis_index("y"), lax.axis_index("c")
    return [(1 - x, y, c), (x, 1 - y, c), (1 - x, 1 - y, c)], (x, y, 1 - c)


def _handshake(peers):
    barrier = pltpu.get_barrier_semaphore()
    for peer in peers:
        pl.semaphore_signal(barrier, inc=1, device_id=peer, device_id_type=MESH)
    pl.semaphore_wait(barrier, len(peers))


def _sequencer_gather_weights(arrs, name, collective_id):
    n = len(arrs)
    out_types = [jax.ShapeDtypeStruct((N_CHIP,) + a.shape, a.dtype) for a in arrs]

    def body(*refs):
        ins, outs = refs[:n], refs[n:2 * n]
        ici_send, ici_recv, d2d_send, d2d_recv = refs[2 * n:]
        chips, sibling = _chip_peers()
        _handshake(chips + [sibling])
        c = lax.axis_index("c")
        me = 2 * lax.axis_index("x") + lax.axis_index("y")

        def half(a, cc):
            rh = arrs[a].shape[0] // 2
            return pl.ds(pl.multiple_of(cc * rh, 16), rh)

        sent = []
        for a in range(n):
            for k, peer in enumerate(chips):
                cp = pltpu.make_async_remote_copy(
                    src_ref=ins[a].at[half(a, c)], dst_ref=outs[a].at[me, half(a, c)],
                    send_sem=ici_send.at[3 * a + k], recv_sem=ici_recv.at[3 * a + k], device_id=peer, device_id_type=MESH)
                cp.start()
                sent.append(cp)
        for a in range(n):
            for k, peer in enumerate(chips):
                landed = outs[a].at[2 * peer[0] + peer[1], half(a, c)]
                pltpu.make_async_remote_copy(
                    src_ref=landed, dst_ref=landed, send_sem=ici_send.at[3 * a + k], recv_sem=ici_recv.at[3 * a + k],
                    device_id=peer, device_id_type=MESH).wait_recv()
                fwd = pltpu.make_async_remote_copy(
                    src_ref=landed, dst_ref=landed, send_sem=d2d_send.at[3 * a + k], recv_sem=d2d_recv.at[3 * a + k],
                    device_id=sibling, device_id_type=MESH)
                fwd.start()
                sent.append(fwd)
        for a in range(n):
            for k, peer in enumerate(chips):
                passed = outs[a].at[2 * peer[0] + peer[1], half(a, 1 - c)]
                pltpu.make_async_remote_copy(
                    src_ref=passed, dst_ref=passed, send_sem=d2d_send.at[3 * a + k], recv_sem=d2d_recv.at[3 * a + k],
                    device_id=sibling, device_id_type=MESH).wait_recv()
        for cp in sent:
            cp.wait_send()

    gathered = pl.kernel(
        body, out_type=out_types, mesh=plsc.ScalarSubcoreMesh(axis_name="sequencer", num_cores=1), name=name,
        scratch_types=[pltpu.SemaphoreType.DMA((3 * n,))] * 4,
        compiler_params=pltpu.CompilerParams(collective_id=collective_id),
    )(*arrs)
    chip = 2 * lax.axis_index("x") + lax.axis_index("y")
    return [lax.dynamic_update_slice(g, a[None], (chip, 0, 0)) for g, a in zip(gathered, arrs)]


def _sequencer_chip_exchange(arrs, name, collective_id):
    n = len(arrs)
    out_types = [jax.ShapeDtypeStruct(a.shape, a.dtype) for a in arrs]

    def body(*refs):
        ins, outs = refs[:n], refs[n:2 * n]
        send_sems, recv_sems = refs[2 * n:]
        chips, _ = _chip_peers()
        _handshake(chips)
        me = 2 * lax.axis_index("x") + lax.axis_index("y")
        sent = []
        for a in range(n):
            for k, peer in enumerate(chips):
                cp = pltpu.make_async_remote_copy(
                    src_ref=ins[a].at[2 * peer[0] + peer[1]], dst_ref=outs[a].at[me],
                    send_sem=send_sems.at[3 * a + k], recv_sem=recv_sems.at[3 * a + k], device_id=peer, device_id_type=MESH)
                cp.start()
                sent.append(cp)
        for a in range(n):
            for k, peer in enumerate(chips):
                landed = outs[a].at[2 * peer[0] + peer[1]]
                pltpu.make_async_remote_copy(
                    src_ref=landed, dst_ref=landed, send_sem=send_sems.at[3 * a + k], recv_sem=recv_sems.at[3 * a + k],
                    device_id=peer, device_id_type=MESH).wait_recv()
        for cp in sent:
            cp.wait_send()

    received = pl.kernel(
        body, out_type=out_types, mesh=plsc.ScalarSubcoreMesh(axis_name="sequencer", num_cores=1), name=name,
        scratch_types=[pltpu.SemaphoreType.DMA((3 * n,))] * 2,
        compiler_params=pltpu.CompilerParams(collective_id=collective_id),
    )(*arrs)
    chip = 2 * lax.axis_index("x") + lax.axis_index("y")
    return [lax.dynamic_update_slice(r, lax.dynamic_slice(a, (chip, 0, 0), (1,) + a.shape[1:]), (chip, 0, 0))
            for r, a in zip(received, arrs)]


def _sibling_exchange(arrs, name):
    n = len(arrs)

    def body(*refs):
        ins, outs = refs[:n], refs[n:2 * n]
        send_sems, recv_sems = refs[2 * n:]
        sibling = (lax.axis_index("x"), lax.axis_index("y"), 1 - lax.axis_index("c"))
        cps = []
        for a in range(n):
            cp = pltpu.make_async_remote_copy(src_ref=ins[a], dst_ref=outs[a], send_sem=send_sems.at[a],
                                              recv_sem=recv_sems.at[a], device_id=sibling, device_id_type=MESH)
            cp.start()
            cps.append(cp)
        for cp in cps:
            cp.wait_recv()
        for cp in cps:
            cp.wait_send()

    hbm = pl.BlockSpec(memory_space=pltpu.HBM)
    return pl.pallas_call(
        body, name=name, out_shape=[jax.ShapeDtypeStruct(a.shape, a.dtype) for a in arrs],
        in_specs=[hbm] * n, out_specs=[hbm] * n,
        scratch_shapes=[pltpu.SemaphoreType.DMA((n,)), pltpu.SemaphoreType.DMA((n,))],
    )(*arrs)


def _matmul(a, b, mode, out_dtype, name, tm=1024, tn=1024, tk=2048, relu2=False, times=None, b_split=False,
            out_split=False):
    b_shape = (b.shape[1], b.shape[2] * N_CHIP) if b_split else b.shape
    if mode == "nn":
        (m, k), (k2, n) = a.shape, b_shape
    elif mode == "nt":
        (m, k), (n, k2) = a.shape, b_shape
    else:
        (k, m), (k2, n) = a.shape, b_shape
    assert k == k2, (a.shape, b.shape, mode)
    n_cut = n // N_CHIP if (out_split or (b_split and mode != "nt")) else n
    k_cut = k // N_CHIP if (b_split and mode == "nt") else k
    tm, tn, tk = _tile(m, tm, LANES), _tile(n_cut, tn, LANES), _tile(k_cut, tk, LANES)
    assert n_cut % tn == 0 and k_cut % tk == 0 and m % tm == 0, (name, m, n, k, tm, tn, tk)
    nk = k // tk
    nbc, nkc = n_cut // tn, k_cut // tk
    n_in = 2 if times is None else 3
    n_out = 2 if relu2 else 1

    def product(a_ref, b_ref):
        if mode == "nn":
            return jnp.dot(a_ref[...], b_ref[...], preferred_element_type=F32)
        return lax.dot_general(a_ref[...], b_ref[...], NT_DIMS if mode == "nt" else TN_DIMS, preferred_element_type=F32)

    def finish(p, refs, o_refs):
        if relu2:
            p = jnp.maximum(p, 0.0)
            o_refs[0][...] = p.astype(o_refs[0].dtype)
            o_refs[1][...] = (p * p).astype(o_refs[1].dtype)
        elif times is not None:
            o_refs[0][...] = (2.0 * refs[2][...].astype(F32) * p).astype(o_refs[0].dtype)
        else:
            o_refs[0][...] = p.astype(o_refs[0].dtype)

    def body(*refs):
        o_refs = refs[n_in:n_in + n_out]
        if nk == 1:
            finish(product(refs[0], refs[1]), refs, o_refs)
            return
        acc_ref = refs[n_in + n_out]
        kk = pl.program_id(2)

        @pl.when(kk == 0)
        def _():
            acc_ref[...] = product(refs[0], refs[1])

        @pl.when((kk > 0) & (kk < nk - 1))
        def _():
            acc_ref[...] += product(refs[0], refs[1])

        @pl.when(kk == nk - 1)
        def _():
            finish(acc_ref[...] + product(refs[0], refs[1]), refs, o_refs)

    if mode == "tn":
        a_spec = pl.BlockSpec((tk, tm), lambda i, j, kk: (kk, i))
    else:
        a_spec = pl.BlockSpec((tm, tk), lambda i, j, kk: (i, kk))
    if mode == "nt":
        b_spec = (pl.BlockSpec((None, tn, tk), lambda i, j, kk: (kk // nkc, j, kk % nkc)) if b_split
                  else pl.BlockSpec((tn, tk), lambda i, j, kk: (j, kk)))
    else:
        b_spec = (pl.BlockSpec((None, tk, tn), lambda i, j, kk: (j // nbc, kk, j % nbc)) if b_split
                  else pl.BlockSpec((tk, tn), lambda i, j, kk: (kk, j)))
    mn_spec = pl.BlockSpec((tm, tn), lambda i, j, kk: (i, j))
    if out_split:
        o_spec = pl.BlockSpec((None, tm, tn), lambda i, j, kk: (j // nbc, i, j % nbc))
        o_shape = jax.ShapeDtypeStruct((N_CHIP, m, n_cut), out_dtype)
    else:
        o_spec, o_shape = mn_spec, jax.ShapeDtypeStruct((m, n), out_dtype)
    out = pl.pallas_call(
        body, name=name, grid=(m // tm, n // tn, nk), in_specs=[a_spec, b_spec] + [mn_spec] * (n_in - 2),
        out_specs=[o_spec] * n_out, out_shape=[o_shape] * n_out,
        scratch_shapes=[] if nk == 1 else [pltpu.VMEM((tm, tn), F32)],
        compiler_params=_params(("parallel", "parallel", "arbitrary")),
    )(*((a, b) if times is None else (a, b, times)))
    return out if relu2 else out[0]


def _mod_part(c_all, w_s, b_s, name):
    d, na = w_s.shape
    tn = _tile(na, 512, LANES)

    def body(c_ref, w_ref, b_ref, o_ref):
        ca = _silu(c_ref[...]).astype(BF16)
        o_ref[...] = jnp.dot(ca, w_ref[...].astype(BF16), preferred_element_type=F32) + b_ref[...]

    return pl.pallas_call(
        body, name=name, grid=(na // tn,),
        in_specs=[pl.BlockSpec((N_DEV, d), lambda j: (0, 0)), pl.BlockSpec((d, tn), lambda j: (0, j)),
                  pl.BlockSpec((1, tn), lambda j: (0, j))],
        out_specs=pl.BlockSpec((N_DEV, tn), lambda j: (0, j)),
        out_shape=jax.ShapeDtypeStruct((N_DEV, na), F32), compiler_params=_params(("parallel",)),
    )(c_all, w_s, b_s)


def _wada_grad(c_all, dmod_s, name):
    d = c_all.shape[1]
    na = dmod_s.shape[1]
    td, tn = _tile(d, 512, LANES), _tile(na, 512, LANES)

    def body(c_ref, g_ref, o_ref):
        o_ref[...] = lax.dot_general(_silu(c_ref[...]), g_ref[...], TN_DIMS, precision=HI, preferred_element_type=F32)

    return pl.pallas_call(
        body, name=name, grid=(d // td, na // tn),
        in_specs=[pl.BlockSpec((N_DEV, td), lambda i, j: (0, i)), pl.BlockSpec((N_DEV, tn), lambda i, j: (0, j))],
        out_specs=pl.BlockSpec((td, tn), lambda i, j: (i, j)),
        out_shape=jax.ShapeDtypeStruct((d, na), F32), compiler_params=_params(("parallel", "parallel")),
    )(c_all, dmod_s)


def _row_specs(tb, d, n_full, n_vec):
    full = pl.BlockSpec((tb, d), lambda i: (i, 0))
    vec = pl.BlockSpec((1, d), lambda i: (0, 0))
    return [full] * n_full + [vec] * n_vec


def _norm_mod(x, w, sc, sh, name):
    t, d = x.shape
    tb = _tile(t, 256, SUBLANES)

    def body(x_ref, w_ref, sc_ref, sh_ref, o_ref):
        xv = x_ref[...]
        o_ref[...] = (xv * _rms_scale(xv) * w_ref[...] * (1.0 + sc_ref[...]) + sh_ref[...]).astype(o_ref.dtype)

    return pl.pallas_call(
        body, name=name, grid=(t // tb,), in_specs=_row_specs(tb, d, 1, 3),
        out_specs=pl.BlockSpec((tb, d), lambda i: (i, 0)), out_shape=jax.ShapeDtypeStruct((t, d), BF16),
        compiler_params=_params(("parallel",)),
    )(x, w, sc, sh)


def _norm_mod_bwd(x, w, sc, dh, dres, name):
    t, d = x.shape
    tb = _tile(t, 256, SUBLANES)

    def body(x_ref, w_ref, sc_ref, dh_ref, dres_ref, dx_ref, dw_ref, dsc_ref, dsh_ref):
        @pl.when(pl.program_id(0) == 0)
        def _():
            dw_ref[...] = jnp.zeros_like(dw_ref)
            dsc_ref[...] = jnp.zeros_like(dsc_ref)
            dsh_ref[...] = jnp.zeros_like(dsh_ref)

        xv = x_ref[...]
        r = _rms_scale(xv)
        xn = xv * r
        g = dh_ref[...].astype(F32)
        wv, one_sc = w_ref[...], 1.0 + sc_ref[...]
        gxn = g * xn
        dsh_ref[...] += jnp.sum(g, axis=0, keepdims=True)
        dsc_ref[...] += jnp.sum(gxn, axis=0, keepdims=True) * wv
        dw_ref[...] += jnp.sum(gxn, axis=0, keepdims=True) * one_sc
        dxn = g * (wv * one_sc)
        dx_ref[...] = dres_ref[...] + r * (dxn - xn * jnp.mean(dxn * xn, axis=-1, keepdims=True))

    vec_out = pl.BlockSpec((1, d), lambda i: (0, 0))
    return pl.pallas_call(
        body, name=name, grid=(t // tb,),
        in_specs=[pl.BlockSpec((tb, d), lambda i: (i, 0)), pl.BlockSpec((1, d), lambda i: (0, 0)),
                  pl.BlockSpec((1, d), lambda i: (0, 0)), pl.BlockSpec((tb, d), lambda i: (i, 0)),
                  pl.BlockSpec((tb, d), lambda i: (i, 0))],
        out_specs=[pl.BlockSpec((tb, d), lambda i: (i, 0)), vec_out, vec_out, vec_out],
        out_shape=[jax.ShapeDtypeStruct((t, d), F32)] + [jax.ShapeDtypeStruct((1, d), F32)] * 3,
        compiler_params=_params(("arbitrary",)),
    )(x, w, sc, dh, dres)


def _resid(x, y, w, gt, name):
    t, d = x.shape
    tb = _tile(t, 256, SUBLANES)

    def body(x_ref, y_ref, w_ref, gt_ref, o_ref):
        yv = y_ref[...]
        o_ref[...] = x_ref[...] + gt_ref[...] * (yv * _rms_scale(yv) * w_ref[...])

    return pl.pallas_call(
        body, name=name, grid=(t // tb,), in_specs=_row_specs(tb, d, 2, 2),
        out_specs=pl.BlockSpec((tb, d), lambda i: (i, 0)), out_shape=jax.ShapeDtypeStruct((t, d), F32),
        compiler_params=_params(("parallel",)),
    )(x, y, w, gt)


def _loss_head(x2, y2, w, gt, target, name):
    t, d = x2.shape
    tb = _tile(t, 256, SUBLANES)

    def body(x_ref, y_ref, tg_ref, w_ref, gt_ref, do_ref, loss_ref):
        @pl.when(pl.program_id(0) == 0)
        def _():
            loss_ref[...] = jnp.zeros_like(loss_ref)

        yv = y_ref[...]
        out = x_ref[...] + gt_ref[...] * (yv * _rms_scale(yv) * w_ref[...])
        err = out - tg_ref[...]
        do_ref[...] = err * (1.0 / d)
        per_tok = jnp.mean(err * err, axis=-1, keepdims=True)
        loss_ref[...] += 0.5 * jnp.sum(per_tok, axis=0, keepdims=True)

    return pl.pallas_call(
        body, name=name, grid=(t // tb,), in_specs=_row_specs(tb, d, 3, 2),
        out_specs=[pl.BlockSpec((tb, d), lambda i: (i, 0)), pl.BlockSpec((1, LANES), lambda i: (0, 0))],
        out_shape=[jax.ShapeDtypeStruct((t, d), F32), jax.ShapeDtypeStruct((1, LANES), F32)],
        compiler_params=_params(("arbitrary",)),
    )(x2, y2, target, w, gt)


def _resid_bwd(dout, y, w, gt, name):
    t, d = y.shape
    tb = _tile(t, 256, SUBLANES)

    def body(do_ref, y_ref, w_ref, gt_ref, dy_ref, dgt_ref, dw_ref):
        @pl.when(pl.program_id(0) == 0)
        def _():
            dgt_ref[...] = jnp.zeros_like(dgt_ref)
            dw_ref[...] = jnp.zeros_like(dw_ref)

        yv, g = y_ref[...], do_ref[...]
        r = _rms_scale(yv)
        yn = yv * r
        wv, gtv = w_ref[...], gt_ref[...]
        gyn = jnp.sum(g * yn, axis=0, keepdims=True)
        dgt_ref[...] += gyn * wv
        dw_ref[...] += gyn * gtv
        dyn = g * (gtv * wv)
        dy_ref[...] = (r * (dyn - yn * jnp.mean(dyn * yn, axis=-1, keepdims=True))).astype(dy_ref.dtype)

    vec_out = pl.BlockSpec((1, d), lambda i: (0, 0))
    return pl.pallas_call(
        body, name=name, grid=(t // tb,), in_specs=_row_specs(tb, d, 2, 2),
        out_specs=[pl.BlockSpec((tb, d), lambda i: (i, 0)), vec_out, vec_out],
        out_shape=[jax.ShapeDtypeStruct((t, d), BF16)] + [jax.ShapeDtypeStruct((1, d), F32)] * 2,
        compiler_params=_params(("arbitrary",)),
    )(dout, y, w, gt)


def _row_half_to_bf16(full, which, sib, name):
    n, r2, c = full.shape
    r = r2 // 2
    tr, tc = _elementwise_tiles(r, c)
    nbh = r // tr

    def body(which_ref, a_ref, *rest):
        if sib is None:
            rest[0][...] = a_ref[...].astype(BF16)
        else:
            rest[1][...] = (a_ref[...] + rest[0][...].astype(F32)).astype(BF16)

    half_spec = pl.BlockSpec((1, tr, tc), lambda j, i, k, which_ref: (j, which_ref[0] * nbh + i, k))
    spec = pl.BlockSpec((1, tr, tc), lambda j, i, k, which_ref: (j, i, k))
    grid_spec = pltpu.PrefetchScalarGridSpec(
        num_scalar_prefetch=1, grid=(n, nbh, c // tc), in_specs=[half_spec] + ([] if sib is None else [spec]), out_specs=spec)
    return pl.pallas_call(
        body, name=name, grid_spec=grid_spec, out_shape=jax.ShapeDtypeStruct((n, r, c), BF16),
        compiler_params=_params(("parallel", "parallel", "parallel")),
    )(which, full, *([] if sib is None else [sib]))


def _sum_chips(recv, name):
    _, r, c = recv.shape
    tr, tc = _elementwise_tiles(r, c)

    def body(x_ref, o_ref):
        acc = x_ref[0].astype(F32)
        for j in range(1, N_CHIP):
            acc = acc + x_ref[j].astype(F32)
        o_ref[...] = acc

    return pl.pallas_call(
        body, name=name, grid=(r // tr, c // tc), in_specs=[pl.BlockSpec((N_CHIP, tr, tc), lambda i, j: (0, i, j))],
        out_specs=pl.BlockSpec((tr, tc), lambda i, j: (i, j)), out_shape=jax.ShapeDtypeStruct((r, c), F32),
        compiler_params=_params(("parallel", "parallel")),
    )(recv)


def _adamw(w, g_parts, m, v, name, by_core=False):
    r, c = w.shape
    tr, tc = _elementwise_tiles(r // 2 if by_core else r, c)
    n_g = len(g_parts)
    nbh = (r // 2) // tr
    c1 = 1.0 / (1.0 - ADAM_B1 ** ADAM_STEP)
    c2 = 1.0 / (1.0 - ADAM_B2 ** ADAM_STEP)

    def body(*refs):
        w_ref, g_refs, m_ref, v_ref = refs[0], refs[1:1 + n_g], refs[1 + n_g], refs[2 + n_g]
        g_out, d_out, m_out, v_out = refs[3 + n_g:]
        if by_core:
            in_my_half = (pl.program_id(0) // nbh) == lax.axis_index("c")
            g = jnp.where(in_my_half, g_refs[0][...], g_refs[1][...])
        else:
            g = g_refs[0][...]
            for extra in g_refs[1:]:
                g = g + extra[...]
        mn = ADAM_B1 * m_ref[...] + (1.0 - ADAM_B1) * g
        vn = ADAM_B2 * v_ref[...] + (1.0 - ADAM_B2) * (g * g)
        g_out[...] = g
        m_out[...] = mn
        v_out[...] = vn
        d_out[...] = -ADAM_LR * ((mn * c1) / (jnp.sqrt(vn * c2) + ADAM_EPS) + ADAM_WD * w_ref[...])

    spec = pl.BlockSpec((tr, tc), lambda i, j: (i, j))
    g_spec = pl.BlockSpec((tr, tc), lambda i, j: (i % nbh, j)) if by_core else spec
    return pl.pallas_call(
        body, name=name, grid=(r // tr, c // tc), in_specs=[spec] + [g_spec] * n_g + [spec] * 2, out_specs=[spec] * 4,
        out_shape=[jax.ShapeDtypeStruct((r, c), F32)] * 4, compiler_params=_params(("parallel", "parallel")),
    )(w, *g_parts, m, v)


def _conv_taps(u, t):
    rows = lax.broadcasted_iota(jnp.int32, u.shape, 0)
    return [u] + [jnp.where(rows >= dd, pltpu.roll(u, dd, 0), 0.0) for dd in range(1, CONV_K)]


def _conv_fwd(proj, conv_w, col0, name):
    t = proj.shape[0]
    ch = conv_w.shape[1]

    def body(u_ref, w_ref, o_ref):
        taps = _conv_taps(u_ref[...], t)
        wv = w_ref[...]
        y = taps[0] * wv[CONV_K - 1:CONV_K]
        for dd in range(1, CONV_K):
            y = y + taps[dd] * wv[CONV_K - 1 - dd:CONV_K - dd]
        o_ref[...] = _silu(y)

    return pl.pallas_call(
        body, name=name, grid=(ch // LANES,),
        in_specs=[pl.BlockSpec((t, LANES), lambda j: (0, col0 + j)), pl.BlockSpec((CONV_K, LANES), lambda j: (0, j))],
        out_specs=pl.BlockSpec((t, LANES), lambda j: (0, j)), out_shape=jax.ShapeDtypeStruct((t, ch), F32),
        compiler_params=_params(("parallel",)),
    )(proj, conv_w)


def _conv_bwd(proj, conv_w, ds, col0, name):
    t = proj.shape[0]
    ch = conv_w.shape[1]

    def body(u_ref, w_ref, ds_ref, du_ref, dw_ref):
        u = u_ref[...]
        taps = _conv_taps(u, t)
        wv = w_ref[...]
        y = taps[0] * wv[CONV_K - 1:CONV_K]
        for dd in range(1, CONV_K):
            y = y + taps[dd] * wv[CONV_K - 1 - dd:CONV_K - dd]
        sg = jax.nn.sigmoid(y)
        dy = ds_ref[...] * (sg * (1.0 + y * (1.0 - sg)))
        rows = lax.broadcasted_iota(jnp.int32, u.shape, 0)
        du = dy * wv[CONV_K - 1:CONV_K]
        for dd in range(1, CONV_K):
            ahead = jnp.where(rows < t - dd, pltpu.roll(dy, t - dd, 0), 0.0)
            du = du + ahead * wv[CONV_K - 1 - dd:CONV_K - dd]
        du_ref[...] = du.astype(du_ref.dtype)
        dws = [jnp.sum(dy * taps[CONV_K - 1 - j], axis=0, keepdims=True) for j in range(CONV_K)]
        dw_ref[...] = jnp.concatenate(dws, axis=0)

    return pl.pallas_call(
        body, name=name, grid=(ch // LANES,),
        in_specs=[pl.BlockSpec((t, LANES), lambda j: (0, col0 + j)), pl.BlockSpec((CONV_K, LANES), lambda j: (0, j)),
                  pl.BlockSpec((t, LANES), lambda j: (0, j))],
        out_specs=[pl.BlockSpec((t, LANES), lambda j: (0, j)), pl.BlockSpec((CONV_K, LANES), lambda j: (0, j))],
        out_shape=[jax.ShapeDtypeStruct((t, ch), BF16), jax.ShapeDtypeStruct((CONV_K, ch), F32)],
        compiler_params=_params(("parallel",)),
    )(proj, conv_w, ds)


def _hg_block(st, q, fl, vi, g, l0, l1, nw):
    hs = range(len(st))
    tb = q[0].shape[0]
    ln = HG_SUB
    lb = [jax.nn.sigmoid(l0[h] - l1[h]) for h in hs]
    rows = lax.broadcasted_iota(jnp.int32, (ln, HEAD), 0)
    tri = (lax.broadcasted_iota(jnp.int32, (ln, ln), 0) >= lax.broadcasted_iota(jnp.int32, (ln, ln), 1)).astype(F32)
    st = list(st)
    outs = [[] for _ in hs]
    for i in range(tb // ln):
        sl = slice(i * ln, (i + 1) * ln)
        qs, vs = [q[h][sl] for h in hs], [vi[h][sl] for h in hs]
        f = [lb[h] + (1.0 - lb[h]) * jax.nn.sigmoid(fl[h][sl]) for h in hs]
        k = [1.0 - f[h] for h in hs]
        b = [jnp.dot(tri, jnp.log(f[h]), precision=HI, preferred_element_type=F32) for h in hs]
        o = [lax.dot_general((qs[h] * jnp.exp(b[h])).astype(BF16), st[h].astype(BF16), NT_DIMS, preferred_element_type=F32)
             for h in hs]
        for s in range(ln):
            e = [jnp.exp(jnp.where(rows >= s, b[h] - b[h][s:s + 1], -1e30)) for h in hs]
            a = [jnp.sum(qs[h] * e[h] * k[h][s:s + 1], axis=-1, keepdims=True) for h in hs]
            o = [o[h] + a[h] * vs[h][s:s + 1] for h in hs]
        kt = [k[h] * jnp.exp(b[h][ln - 1:ln] - b[h]) for h in hs]
        upd = [lax.dot_general(vs[h].astype(BF16), kt[h].astype(BF16), TN_DIMS, preferred_element_type=F32) for h in hs]
        st = [st[h] * jnp.exp(b[h][ln - 1:ln]) + upd[h] for h in hs]
        for h in hs:
            outs[h].append(o[h])
    o = [jnp.concatenate(outs[h], axis=0) for h in hs]
    out = [o[h] * _rms_scale(o[h]) * nw * _silu(g[h]) for h in hs]
    return st, out


def _head_cols(h):
    return slice(h * HEAD, (h + 1) * HEAD)


def _head_groups(n_heads, group):
    g = min(group, n_heads)
    return [list(range(i, min(i + g, n_heads))) for i in range(0, n_heads, g)]


def _hg_in_specs(n_heads, tb, time_index):
    hw = n_heads * HEAD
    cols = [pl.BlockSpec((tb, hw), functools.partial(lambda part, j: (time_index(j), part), part)) for part in range(4)]
    head_rows = pl.BlockSpec((n_heads, 1, HEAD), lambda j: (0, 0, 0))
    return cols + [head_rows, head_rows, pl.BlockSpec((1, HEAD), lambda j: (0, 0))]


def _hgrn2_fwd(proj, l0, l1, nw, n_heads, name):
    t = proj.shape[0]
    hw = n_heads * HEAD
    tb = _tile(t, HG_BLOCK, HG_SUB)
    nb = t // tb

    def body(q_ref, f_ref, i_ref, g_ref, l0_ref, l1_ref, nw_ref, o_ref, save_ref, st_ref):
        @pl.when(pl.program_id(0) == 0)
        def _():
            st_ref[...] = jnp.zeros_like(st_ref)

        for hs in _head_groups(n_heads, HG_FWD_GROUP):
            st = [st_ref[h] for h in hs]
            for h, s in zip(hs, st):
                save_ref[h] = s
            st, out = _hg_block(st, *[[r[:, _head_cols(h)] for h in hs] for r in (q_ref, f_ref, i_ref, g_ref)],
                                [l0_ref[h] for h in hs], [l1_ref[h] for h in hs], nw_ref[...])
            for h, s, o in zip(hs, st, out):
                st_ref[h] = s
                o_ref[:, _head_cols(h)] = o.astype(o_ref.dtype)

    return pl.pallas_call(
        body, name=name, grid=(nb,), in_specs=_hg_in_specs(n_heads, tb, lambda j: j),
        out_specs=[pl.BlockSpec((tb, hw), lambda j: (j, 0)),
                   pl.BlockSpec((None, n_heads, HEAD, HEAD), lambda j: (j, 0, 0, 0))],
        out_shape=[jax.ShapeDtypeStruct((t, hw), BF16), jax.ShapeDtypeStruct((nb, n_heads, HEAD, HEAD), F32)],
        scratch_shapes=[pltpu.VMEM((n_heads, HEAD, HEAD), F32)], compiler_params=_params(("arbitrary",)),
    )(proj, proj, proj, proj, l0, l1, nw)


def _hgrn2_bwd(proj, l0, l1, nw, saved, d_ocat, n_heads, name):
    t = proj.shape[0]
    tb = _tile(t, HG_BLOCK, HG_SUB)
    nb = t // tb
    rev = lambda j: nb - 1 - j

    hw = n_heads * HEAD

    def body(q_ref, f_ref, i_ref, g_ref, l0_ref, l1_ref, nw_ref, save_ref, do_ref,
             dp_ref, dl0_ref, dl1_ref, dnw_ref, dst_ref):
        @pl.when(pl.program_id(0) == 0)
        def _():
            dst_ref[...] = jnp.zeros_like(dst_ref)
            dl0_ref[...] = jnp.zeros_like(dl0_ref)
            dl1_ref[...] = jnp.zeros_like(dl1_ref)
            dnw_ref[...] = jnp.zeros_like(dnw_ref)

        dnw_acc = jnp.zeros((1, HEAD), F32)
        for hs in _head_groups(n_heads, HG_BWD_GROUP):
            _, vjp = jax.vjp(_hg_block, [save_ref[h] for h in hs],
                             *[[r[:, _head_cols(h)] for h in hs] for r in (q_ref, f_ref, i_ref, g_ref)],
                             [l0_ref[h] for h in hs], [l1_ref[h] for h in hs], nw_ref[...])
            dst, dq, df, di, dg, dl0, dl1, dnw = vjp(([dst_ref[h] for h in hs], [do_ref[:, _head_cols(h)] for h in hs]))
            for i, h in enumerate(hs):
                dst_ref[h] = dst[i]
                for part, val in enumerate((dq, df, di, dg)):
                    dp_ref[:, part * hw + h * HEAD:part * hw + (h + 1) * HEAD] = val[i].astype(dp_ref.dtype)
                dl0_ref[h] += dl0[i]
                dl1_ref[h] += dl1[i]
            dnw_acc = dnw_acc + dnw
        dnw_ref[...] += dnw_acc

    head_rows = pl.BlockSpec((n_heads, 1, HEAD), lambda j: (0, 0, 0))
    return pl.pallas_call(
        body, name=name, grid=(nb,),
        in_specs=_hg_in_specs(n_heads, tb, rev) + [pl.BlockSpec((None, n_heads, HEAD, HEAD), lambda j: (rev(j), 0, 0, 0)),
                                                   pl.BlockSpec((tb, hw), lambda j: (rev(j), 0))],
        out_specs=[pl.BlockSpec((tb, 4 * hw), lambda j: (rev(j), 0)), head_rows, head_rows,
                   pl.BlockSpec((1, HEAD), lambda j: (0, 0))],
        out_shape=[jax.ShapeDtypeStruct((t, 4 * hw), BF16)] + [jax.ShapeDtypeStruct((n_heads, 1, HEAD), F32)] * 2
        + [jax.ShapeDtypeStruct((1, HEAD), F32)],
        scratch_shapes=[pltpu.VMEM((n_heads, HEAD, HEAD), F32)], compiler_params=_params(("arbitrary",)),
    )(proj, proj, proj, proj, l0, l1, nw, saved, d_ocat)


NN_DIMS = (((1,), (0,)), ((), ()))


def _split_bf16(x):
    hi = x.astype(BF16)
    return hi, (x - hi.astype(F32)).astype(BF16)


def _dot3(a, b, dims=NN_DIMS):
    (ah, al), (bh, bl) = _split_bf16(a), _split_bf16(b)
    dot = functools.partial(lax.dot_general, dimension_numbers=dims, preferred_element_type=F32)
    return dot(ah, bh) + dot(ah, bl) + dot(al, bh)


@jax.custom_vjp
def _mm3(a, b):
    return _dot3(a, b)


def _mm3_fwd(a, b):
    return _dot3(a, b), (a, b)


def _mm3_bwd(res, g):
    a, b = res
    return _dot3(g, b, NT_DIMS), _dot3(a, g, TN_DIMS)


_mm3.defvjp(_mm3_fwd, _mm3_bwd)


def _dot_bf16(a, b, dims=(((1,), (0,)), ((), ()))):
    return lax.dot_general(a.astype(BF16), b.astype(BF16), dims, preferred_element_type=F32)


def _inv_unit_lower_raw(ms):
    hs = range(len(ms))
    c = ms[0].shape[0]
    r = lax.broadcasted_iota(jnp.int32, (c, c), 0)
    q = lax.broadcasted_iota(jnp.int32, (c, c), 1)
    eye = (r == q).astype(F32)
    md = [jnp.where((r // GDN_INV_BLOCK) == (q // GDN_INV_BLOCK), ms[h], 0.0) for h in hs]
    p = [-md[h] for h in hs]
    t16 = [eye + p[h] for h in hs]
    for _ in range(int(math.log2(GDN_INV_BLOCK)) - 1):
        p = [_dot3(p[h], p[h]) for h in hs]
        t16 = [t16[h] + _dot3(t16[h], p[h]) for h in hs]
    p = [-_dot3(t16[h], ms[h] - md[h]) for h in hs]
    t2 = [eye + p[h] for h in hs]
    for _ in range(int(math.log2(c // GDN_INV_BLOCK)) - 1):
        p = [_dot3(p[h], p[h]) for h in hs]
        t2 = [t2[h] + _dot3(t2[h], p[h]) for h in hs]
    return [_dot3(t2[h], t16[h]) for h in hs]


@jax.custom_vjp
def _inv_unit_lower(ms):
    return _inv_unit_lower_raw(ms)


def _inv_fwd(ms):
    ts = _inv_unit_lower_raw(ms)
    return ts, ts


def _inv_bwd(ts, dts):
    hs = range(len(ts))
    inner = [_dot3(ts[h], dts[h], TN_DIMS) for h in hs]
    return ([-_dot3(inner[h], ts[h], NT_DIMS) for h in hs],)


_inv_unit_lower.defvjp(_inv_fwd, _inv_bwd)


def _gdn_block(precise, onehots, st, qc, kc, vc, g, ab, alog_row, dtb_row, nw):
    inverse, dot3 = precise
    hs = range(len(st))
    c = qc[0].shape[0]
    lane_sum = lambda v: jnp.sum(v, axis=-1, keepdims=True)
    a = [lane_sum(ab * onehots[h][0]) for h in hs]
    bb = [lane_sum(ab * onehots[h][1]) for h in hs]
    alog = [lane_sum(alog_row * onehots[h][0]) for h in hs]
    dtb = [lane_sum(dtb_row * onehots[h][0]) for h in hs]
    la = [-jnp.exp(alog[h]) * _softplus(a[h] + dtb[h]) for h in hs]
    beta = [jax.nn.sigmoid(bb[h]) for h in hs]
    q = [qc[h] * lax.rsqrt(lane_sum(qc[h] * qc[h]) + EPS) * (HEAD ** -0.5) for h in hs]
    k = [kc[h] * lax.rsqrt(lane_sum(kc[h] * kc[h]) + EPS) for h in hs]
    r = lax.broadcasted_iota(jnp.int32, (c, c), 0)
    s = lax.broadcasted_iota(jnp.int32, (c, c), 1)
    tri = (r >= s).astype(F32)
    g_cc = [dot3(tri, jnp.broadcast_to(la[h], (c, c))) for h in hs]
    g_cl = [dot3(tri, jnp.broadcast_to(la[h], (c, HEAD))) for h in hs]
    gamma = [jnp.exp(jnp.where(r >= s, g_cc[h] - g_cc[h].T, -1e30)) for h in hs]
    kk = [_dot_bf16(k[h], k[h], NT_DIMS) for h in hs]
    m = [jnp.where(r > s, beta[h] * kk[h] * gamma[h], 0.0) for h in hs]
    tm = inverse(m)
    eg = [jnp.exp(g_cl[h]) for h in hs]
    rhs = [jnp.concatenate([vc[h] * beta[h], k[h] * (beta[h] * eg[h])], axis=1) for h in hs]
    sol = [dot3(tm[h], rhs[h]) for h in hs]
    qk = [_dot_bf16(q[h], k[h], NT_DIMS) * gamma[h] for h in hs]
    g_last = [g_cl[h][c - 1:c] for h in hs]
    k_tail = [k[h] * jnp.exp(g_last[h] - g_cl[h]) for h in hs]
    v_new = [sol[h][:, :HEAD] - _dot_bf16(sol[h][:, HEAD:], st[h], NT_DIMS) for h in hs]
    o_st = [_dot_bf16(q[h] * eg[h], st[h], NT_DIMS) for h in hs]
    o = [o_st[h] + _dot_bf16(qk[h], v_new[h]) for h in hs]
    upd = [_dot_bf16(v_new[h], k_tail[h], TN_DIMS) for h in hs]
    st = [st[h] * jnp.exp(g_last[h]) + upd[h] for h in hs]
    out = [o[h] * _rms_scale(o[h]) * nw * _silu(g[h]) for h in hs]
    return st, out


def _head_onehots(n_heads, h):
    lane = lax.broadcasted_iota(jnp.int32, (1, LANES), 1)
    return (lane == h).astype(F32), (lane == n_heads + h).astype(F32)


def _gdn_in_specs(n_heads, c, time_index):
    hw = n_heads * HEAD
    qkv = [pl.BlockSpec((c, hw), functools.partial(lambda part, j: (time_index(j), part), part)) for part in range(3)]
    row = pl.BlockSpec((1, LANES), lambda j: (0, 0))
    return qkv + [pl.BlockSpec((c, hw), lambda j: (time_index(j), 7)),
                  pl.BlockSpec((c, LANES), lambda j: (time_index(j), 8 * n_heads)), row, row, row]


def _gdn_fwd(qkv, proj, alog_row, dtb_row, nw, n_heads, name):
    t = qkv.shape[0]
    hw = n_heads * HEAD
    c = _tile(t, GDN_CHUNK, GDN_CHUNK)
    nb = t // c

    def body(q_ref, k_ref, v_ref, g_ref, ab_ref, al_ref, dt_ref, nw_ref, o_ref, save_ref, st_ref):
        @pl.when(pl.program_id(0) == 0)
        def _():
            st_ref[...] = jnp.zeros_like(st_ref)

        for hs in _head_groups(n_heads, GDN_FWD_GROUP):
            st = [st_ref[h] for h in hs]
            for h, s in zip(hs, st):
                save_ref[h] = s
            st, out = _gdn_block((_inv_unit_lower_raw, _dot3), [_head_onehots(n_heads, h) for h in hs], st,
                                 *[[r[:, _head_cols(h)] for h in hs] for r in (q_ref, k_ref, v_ref, g_ref)],
                                 ab_ref[...], al_ref[...], dt_ref[...], nw_ref[...])
            for h, s, o in zip(hs, st, out):
                st_ref[h] = s
                o_ref[:, _head_cols(h)] = o.astype(o_ref.dtype)

    return pl.pallas_call(
        body, name=name, grid=(nb,), in_specs=_gdn_in_specs(n_heads, c, lambda j: j),
        out_specs=[pl.BlockSpec((c, hw), lambda j: (j, 0)),
                   pl.BlockSpec((None, n_heads, HEAD, HEAD), lambda j: (j, 0, 0, 0))],
        out_shape=[jax.ShapeDtypeStruct((t, hw), BF16), jax.ShapeDtypeStruct((nb, n_heads, HEAD, HEAD), F32)],
        scratch_shapes=[pltpu.VMEM((n_heads, HEAD, HEAD), F32)], compiler_params=_params(("arbitrary",)),
    )(qkv, qkv, qkv, proj, proj, alog_row, dtb_row, nw)


def _gdn_bwd(qkv, proj, alog_row, dtb_row, nw, saved, d_ocat, n_heads, name):
    t = qkv.shape[0]
    c = _tile(t, GDN_CHUNK, GDN_CHUNK)
    nb = t // c
    rev = lambda j: nb - 1 - j

    hw = n_heads * HEAD

    def body(q_ref, k_ref, v_ref, g_ref, ab_ref, al_ref, dt_ref, nw_ref, save_ref, do_ref,
             dqkv_ref, dg_ref, dab_ref, dal_ref, ddt_ref, dnw_ref, dst_ref):
        @pl.when(pl.program_id(0) == 0)
        def _():
            dst_ref[...] = jnp.zeros_like(dst_ref)
            dal_ref[...] = jnp.zeros_like(dal_ref)
            ddt_ref[...] = jnp.zeros_like(ddt_ref)
            dnw_ref[...] = jnp.zeros_like(dnw_ref)

        dab_acc = jnp.zeros((c, LANES), F32)
        row_acc = [jnp.zeros((1, LANES), F32)] * 3
        for hs in _head_groups(n_heads, GDN_BWD_GROUP):
            fn = functools.partial(_gdn_block, (_inv_unit_lower, _mm3), [_head_onehots(n_heads, h) for h in hs])
            _, vjp = jax.vjp(fn, [save_ref[h] for h in hs],
                             *[[r[:, _head_cols(h)] for h in hs] for r in (q_ref, k_ref, v_ref, g_ref)],
                             ab_ref[...], al_ref[...], dt_ref[...], nw_ref[...])
            dst, dq, dk, dv, dg, dab, dal, ddt, dnw = vjp(([dst_ref[h] for h in hs], [do_ref[:, _head_cols(h)] for h in hs]))
            for i, h in enumerate(hs):
                dst_ref[h] = dst[i]
                for part, val in enumerate((dq, dk, dv)):
                    dqkv_ref[:, part * hw + h * HEAD:part * hw + (h + 1) * HEAD] = val[i]
                dg_ref[:, _head_cols(h)] = dg[i].astype(dg_ref.dtype)
            dab_acc = dab_acc + dab
            row_acc = [acc + val for acc, val in zip(row_acc, (dal, ddt, dnw))]
        dab_ref[...] = dab_acc
        dal_ref[...] += row_acc[0]
        ddt_ref[...] += row_acc[1]
        dnw_ref[...] += row_acc[2]

    row = pl.BlockSpec((1, LANES), lambda j: (0, 0))
    return pl.pallas_call(
        body, name=name, grid=(nb,),
        in_specs=_gdn_in_specs(n_heads, c, rev) + [pl.BlockSpec((None, n_heads, HEAD, HEAD), lambda j: (rev(j), 0, 0, 0)),
                                                   pl.BlockSpec((c, hw), lambda j: (rev(j), 1))],
        out_specs=[pl.BlockSpec((c, 3 * hw), lambda j: (rev(j), 0)), pl.BlockSpec((c, hw), lambda j: (rev(j), 0)),
                   pl.BlockSpec((c, LANES), lambda j: (rev(j), 0)), row, row, row],
        out_shape=[jax.ShapeDtypeStruct((t, 3 * hw), F32), jax.ShapeDtypeStruct((t, hw), BF16),
                   jax.ShapeDtypeStruct((t, LANES), F32)] + [jax.ShapeDtypeStruct((1, LANES), F32)] * 3,
        scratch_shapes=[pltpu.VMEM((n_heads, HEAD, HEAD), F32)], compiler_params=_params(("arbitrary",)),
    )(qkv, qkv, qkv, proj, proj, alog_row, dtb_row, nw, saved, d_ocat)


def _pad_lanes(v, n):
    v = v.reshape(1, -1)
    return jnp.pad(v, ((0, 0), (0, n - v.shape[1])))


def _pack_rows(vecs):
    flat = jnp.concatenate([v.reshape(-1) for v in vecs])
    offs, o = [], 0
    for v in vecs:
        offs.append((o, v.size))
        o += v.size
    per_row = -(-o // (SUBLANES * LANES)) * LANES
    flat = jnp.pad(flat, (0, SUBLANES * per_row - o))
    return flat.reshape(SUBLANES, per_row), offs


def _unpack(gathered, offs):
    per_dev = gathered.reshape(N_DEV, -1)
    return [per_dev[:, o:o + n] for o, n in offs]


def _sum_devices(part):
    acc = part[0]
    for i in range(1, N_DEV):
        acc = acc + part[i]
    return acc


def kernel(x, c, w_ada, b_ada, pre_mix_norm, post_mix_norm, pre_ffn_norm, post_ffn_norm, w_in, hg_lb_logits, hg_norm, gdn_conv_w, gdn_a_log, gdn_dt_bias, gdn_norm, w_out, w_ff1, w_ff2, loss_target, m_w_ada, m_b_ada, m_pre_mix_norm, m_post_mix_norm, m_pre_ffn_norm, m_post_ffn_norm, m_w_in, m_hg_lb_logits, m_hg_norm, m_gdn_conv_w, m_gdn_a_log, m_gdn_dt_bias, m_gdn_norm, m_w_out, m_w_ff1, m_w_ff2, v_w_ada, v_b_ada, v_pre_mix_norm, v_post_mix_norm, v_pre_ffn_norm, v_post_ffn_norm, v_w_in, v_hg_lb_logits, v_hg_norm, v_gdn_conv_w, v_gdn_a_log, v_gdn_dt_bias, v_gdn_norm, v_w_out, v_w_ff1, v_w_ff2):
    assert x.shape[0] == 1 and w_ada.shape[0] == 1 and hg_lb_logits.shape[0] == 2
    t, d = x.shape[1], x.shape[2]
    n_heads = (d // 2) // HEAD
    hw = n_heads * HEAD
    in_cols = 8 * hw + 2 * n_heads
    np_cols = 8 * hw + 2 * LANES
    d_ff = w_ff1.shape[2] * N_CHIP
    na = w_ada.shape[2]
    ax, ay, ac = lax.axis_index("x"), lax.axis_index("y"), lax.axis_index("c")
    chip = 2 * ax + ay
    dev = 4 * ax + 2 * ay + ac

    x2d, tgt = x[0], loss_target[0]

    pack1, offs1 = _pack_rows([c[0], gdn_conv_w[0]])
    c_all, convw_all = _unpack(_gather8(pack1, "gather_cond"), offs1)
    conv_sh = gdn_conv_w.shape[2]
    conv_w = jnp.concatenate([convw_all[2 * j].reshape(CONV_K, conv_sh) for j in range(N_CHIP)], axis=1)

    b_s = lax.dynamic_slice(b_ada, (0, chip * na), (1, na))
    mod_part = _mod_part(c_all, w_ada[0], b_s, "mod_part")
    pack2, offs2 = _pack_rows([mod_part])
    (mod_parts,) = _unpack(_gather8(pack2, "gather_mod"), offs2)
    mod_all = jnp.concatenate([mod_parts[2 * j].reshape(N_DEV, na) for j in range(N_CHIP)], axis=1)
    mod = lax.dynamic_slice(mod_all, (dev, 0), (1, N_MOD * d))
    sh_m, sc_m, gt_m, sh_f, sc_f, gt_f = [mod[:, i * d:(i + 1) * d] for i in range(N_MOD)]

    (g_in,) = _gather_weights([w_in[0].astype(BF16)], "gather_w_in")
    g_out, g_ff1, g_ff2 = _sequencer_gather_weights(
        [w_out[0].astype(BF16), w_ff1[0].astype(BF16), w_ff2[0].astype(BF16)], "gather_weights_late", 1)
    w_in_f = jnp.pad(jnp.transpose(g_in, (1, 0, 2)).reshape(d, in_cols), ((0, 0), (0, np_cols - in_cols)))
    w_out_f = g_out.reshape(d, d)
    w_ff2_f = g_ff2.reshape(d_ff, d)

    h1 = _norm_mod(x2d, pre_mix_norm, sc_m, sh_m, "norm_mod_mix")
    proj = _matmul(h1, w_in_f, "nn", F32, "mm_in", tn=768)
    l0, l1 = hg_lb_logits[0].reshape(n_heads, 1, HEAD), hg_lb_logits[1].reshape(n_heads, 1, HEAD)
    o_hg, hg_saved = _hgrn2_fwd(proj, l0, l1, hg_norm, n_heads, "hgrn2_fwd")
    qkv = _conv_fwd(proj, conv_w, 4 * n_heads, "conv_fwd")
    alog_row, dtb_row = _pad_lanes(gdn_a_log, LANES), _pad_lanes(gdn_dt_bias, LANES)
    o_gdn, gdn_saved = _gdn_fwd(qkv, proj, alog_row, dtb_row, gdn_norm, n_heads, "gdn_fwd")
    o_cat = jnp.concatenate([o_hg, o_gdn], axis=1)
    y1 = _matmul(o_cat, w_out_f, "nn", F32, "mm_out")
    x_mid = _resid(x2d, y1, post_mix_norm, gt_m, "resid_mix")

    h2 = _norm_mod(x_mid, pre_ffn_norm, sc_f, sh_f, "norm_mod_ffn")
    relu_a1, r1 = _matmul(h2, g_ff1, "nn", BF16, "mm_ff1", relu2=True, b_split=True)
    y2 = _matmul(r1, w_ff2_f, "nn", F32, "mm_ff2")
    d_out, loss_row = _loss_head(x_mid, y2, post_ffn_norm, gt_f, tgt, "loss_head")

    in_sh = in_cols // N_CHIP
    ff_sh = d_ff // N_CHIP
    my_half = jnp.reshape(ac, (1,)).astype(jnp.int32)

    def start_reduce(by_chip, tag, collective_id):
        to_sib = [_row_half_to_bf16(a, 1 - my_half, None, f"sibling_half_{tag}{i}") for i, a in enumerate(by_chip)]
        from_sib = _sibling_exchange(to_sib, f"sibling_partials_{tag}")
        chip_part = [_row_half_to_bf16(a, my_half, s, f"add_halves_{tag}{i}") for i, (a, s) in enumerate(zip(by_chip, from_sib))]
        return _sequencer_chip_exchange(chip_part, f"scatter_grads_{tag}", collective_id)

    dy2, d_gt_f, d_post_ffn = _resid_bwd(d_out, y2, post_ffn_norm, gt_f, "resid_ffn_bwd")
    gw_ff2 = _matmul(r1, dy2, "tn", BF16, "mm_ff2_dw")
    da1 = _matmul(dy2, w_ff2_f, "nt", BF16, "mm_ff2_dx", times=relu_a1)
    gw_ff1 = _matmul(h2, da1, "tn", BF16, "mm_ff1_dw", out_split=True)
    recv_ff2, recv_ff1 = _sequencer_chip_exchange([gw_ff2.reshape(N_CHIP, ff_sh, d), gw_ff1], "scatter_grads_ff", 2)
    dh2 = _matmul(da1, g_ff1, "nt", BF16, "mm_ff1_dx", b_split=True)
    d_mid, d_pre_ffn, d_sc_f, d_sh_f = _norm_mod_bwd(x_mid, pre_ffn_norm, sc_f, dh2, d_out, "norm_mod_ffn_bwd")

    dy1, d_gt_m, d_post_mix = _resid_bwd(d_mid, y1, post_mix_norm, gt_m, "resid_mix_bwd")
    gw_out = _matmul(o_cat, dy1, "tn", BF16, "mm_out_dw")
    (recv_out,) = _sequencer_chip_exchange([gw_out.reshape(N_CHIP, d // N_CHIP, d)], "scatter_grads_out", 3)
    d_ocat = _matmul(dy1, w_out_f, "nt", F32, "mm_out_dx")
    dp_hg, dl0, dl1, d_hg_norm = _hgrn2_bwd(proj, l0, l1, hg_norm, hg_saved, d_ocat, n_heads, "hgrn2_bwd")
    dqkv, dg_g, dab, d_alog, d_dtb, d_gdn_norm = _gdn_bwd(
        qkv, proj, alog_row, dtb_row, gdn_norm, gdn_saved, d_ocat, n_heads, "gdn_bwd")
    du, d_conv_w = _conv_bwd(proj, conv_w, dqkv, 4 * n_heads, "conv_bwd")
    dproj = jnp.concatenate([dp_hg, du, dg_g, dab.astype(BF16), jnp.zeros((t, LANES), BF16)], axis=1)
    gw_in = _matmul(h1, dproj, "tn", F32, "mm_in_dw", tn=768)
    (recv_in,) = start_reduce([jnp.transpose(gw_in[:, :in_cols].reshape(d, N_CHIP, in_sh), (1, 0, 2))], "in", 4)
    dh1 = _matmul(dproj, w_in_f, "nt", BF16, "mm_in_dx", tk=2816)
    grad_x, d_pre_mix, d_sc_m, d_sh_m = _norm_mod_bwd(x2d, pre_mix_norm, sc_m, dh1, d_mid, "norm_mod_mix_bwd")

    d_mod = jnp.concatenate([d_sh_m, d_sc_m, d_gt_m, d_sh_f, d_sc_f, d_gt_f], axis=1)
    d_lb_logits = jnp.stack([dl0.reshape(n_heads, HEAD), dl1.reshape(n_heads, HEAD)])
    pack3, offs3 = _pack_rows([loss_row[0, :1], d_pre_mix, d_post_mix, d_pre_ffn, d_post_ffn, d_lb_logits, d_hg_norm,
                               d_conv_w, d_alog[0, :n_heads], d_dtb[0, :n_heads], d_gdn_norm, d_mod])
    parts = _unpack(_gather8(pack3, "gather_vec_grads"), offs3)
    sums = [_sum_devices(p) for p in parts[:-1]]
    loss = sums[0][0]
    dmod_all = parts[-1]
    g_b_ada = _sum_devices(dmod_all).reshape(1, N_MOD * d)
    g_conv_full = sums[7].reshape(CONV_K, N_CHIP * conv_sh)
    g_conv = lax.dynamic_slice(g_conv_full, (0, chip * conv_sh), (CONV_K, conv_sh))
    gw_ada = _wada_grad(c_all, lax.dynamic_slice(dmod_all, (0, chip * na), (N_DEV, na)), "wada_grad")

    mine = [_sum_chips(rv, f"sum_chips_{i}") for i, rv in enumerate([recv_in, recv_out, recv_ff1, recv_ff2])]
    theirs = _sibling_exchange(mine, "sibling_grads")

    big = {}
    for i, (nm, w_, m_, v_) in enumerate([("w_in", w_in, m_w_in, v_w_in), ("w_out", w_out, m_w_out, v_w_out),
                                          ("w_ff1", w_ff1, m_w_ff1, v_w_ff1), ("w_ff2", w_ff2, m_w_ff2, v_w_ff2)]):
        big[nm] = [o[None] for o in _adamw(w_[0], [mine[i], theirs[i]], m_[0], v_[0], f"adamw_{nm}", by_core=(nm == "w_in"))]
    big["w_ada"] = [o[None] for o in _adamw(w_ada[0], [gw_ada], m_w_ada[0], v_w_ada[0], "adamw_w_ada")]

    small_names = ["b_ada", "pre_mix_norm", "post_mix_norm", "pre_ffn_norm", "post_ffn_norm", "hg_lb_logits", "hg_norm",
                   "gdn_conv_w", "gdn_a_log", "gdn_dt_bias", "gdn_norm"]
    small_w = [b_ada, pre_mix_norm, post_mix_norm, pre_ffn_norm, post_ffn_norm, hg_lb_logits, hg_norm, gdn_conv_w,
               gdn_a_log, gdn_dt_bias, gdn_norm]
    small_m = [m_b_ada, m_pre_mix_norm, m_post_mix_norm, m_pre_ffn_norm, m_post_ffn_norm, m_hg_lb_logits, m_hg_norm,
               m_gdn_conv_w, m_gdn_a_log, m_gdn_dt_bias, m_gdn_norm]
    small_v = [v_b_ada, v_pre_mix_norm, v_post_mix_norm, v_pre_ffn_norm, v_post_ffn_norm, v_hg_lb_logits, v_hg_norm,
               v_gdn_conv_w, v_gdn_a_log, v_gdn_dt_bias, v_gdn_norm]
    small_g = [g_b_ada, sums[1], sums[2], sums[3], sums[4], sums[5], sums[6], g_conv, sums[8], sums[9], sums[10]]
    pw, offs_s = _pack_rows(small_w)
    pg, _ = _pack_rows(small_g)
    pm, _ = _pack_rows(small_m)
    pv, _ = _pack_rows(small_v)
    packed = _adamw(pw, [pg], pm, pv, "adamw_vectors")
    small = {}
    for nm, w_, (o, n) in zip(small_names, small_w, offs_s):
        small[nm] = [p.reshape(-1)[o:o + n].reshape(w_.shape) for p in packed]

    order = ["w_ada", "b_ada", "pre_mix_norm", "post_mix_norm", "pre_ffn_norm", "post_ffn_norm", "w_in", "hg_lb_logits",
             "hg_norm", "gdn_conv_w", "gdn_a_log", "gdn_dt_bias", "gdn_norm", "w_out", "w_ff1", "w_ff2"]
    res = {**big, **small}
    outs = [loss, grad_x[None]]
    for k in range(4):
        outs += [res[nm][k] for nm in order]
    return tuple(outs)
```

```python
import functools
import math

import jax
import jax.numpy as jnp
from jax import lax
from jax.experimental import pallas as pl
from jax.experimental.pallas import tpu as pltpu
from jax.experimental.pallas import tpu_sc as plsc

F32 = jnp.float32
BF16 = jnp.bfloat16
HI = lax.Precision.HIGHEST
MESH = pl.DeviceIdType.MESH

LANES = 128
SUBLANES = 8
VMEM_LIMIT = 48 * 1024 * 1024
EPS = 1e-6
HEAD = 128
CONV_K = 4
GDN_CHUNK = 64
GDN_INV_BLOCK = 16
HG_SUB = 16
HG_BLOCK = 128
HG_FWD_GROUP = 8
HG_BWD_GROUP = 4
GDN_FWD_GROUP = 8
GDN_BWD_GROUP = 8
N_MOD = 6
N_DEV = 8
N_CHIP = 4

ADAM_LR = 0.001
ADAM_B1 = 0.9
ADAM_B2 = 0.999
ADAM_EPS = 1e-08
ADAM_WD = 0.01
ADAM_STEP = 10

NT_DIMS = (((1,), (1,)), ((), ()))
TN_DIMS = (((0,), (0,)), ((), ()))


def _tile(dim, target, align):
    if dim <= target:
        return dim
    best = dim
    t = align
    while t <= target:
        if dim % t == 0:
            best = t
        t += align
    return best


def _elementwise_tiles(r, c):
    tc = _tile(c, 1024, LANES)
    tr = _tile(r, max(16, (256 * 1024) // tc // 16 * 16), 16)
    if tr == r and r * tc > 512 * 1024:
        tc = _tile(c, max(LANES, (256 * 1024) // r // LANES * LANES), LANES)
    return tr, tc


def _params(sem):
    return pltpu.CompilerParams(dimension_semantics=sem, vmem_limit_bytes=VMEM_LIMIT)


def _silu(x):
    return x * jax.nn.sigmoid(x)


def _softplus(x):
    pos = x > 0
    return jnp.where(pos, x, 0.0) + jnp.log(1.0 + jnp.exp(jnp.where(pos, -x, x)))


def _rms_scale(x):
    return lax.rsqrt(jnp.mean(x * x, axis=-1, keepdims=True) + EPS)


def _gather8(x_shard, name):
    m_per, n = x_shard.shape
    assert m_per % SUBLANES == 0 and n % LANES == 0

    def body(x_ref, out_ref, send_sems, recv_sems, local_sem):
        x, y, c = lax.axis_index("x"), lax.axis_index("y"), lax.axis_index("c")
        me, sibling = (x, y, c), (x, y, 1 - c)
        chips = [(1 - x, y), (x, 1 - y), (1 - x, 1 - y)]

        def rows(px, py, pc):
            return out_ref.at[pl.ds((4 * px + 2 * py + pc) * m_per, m_per), :]

        def copy(k, block, to, src=None):
            return pltpu.make_async_remote_copy(
                src_ref=rows(*block) if src is None else src, dst_ref=rows(*block),
                send_sem=send_sems.at[k], recv_sem=recv_sems.at[k], device_id=to, device_id_type=MESH)

        mine = pltpu.make_async_copy(x_ref, rows(*me), local_sem)
        mine.start()
        first = [copy(0, me, sibling, src=x_ref)]
        first += [copy(1 + j, me, (*chip, c), src=x_ref) for j, chip in enumerate(chips)]
        for cp in first:
            cp.start()
        passed = [copy(4 + j, (*chip, c), sibling) for j, chip in enumerate(chips)]
        for j, chip in enumerate(chips):
            copy(1 + j, (*chip, c), me).wait_recv()
            passed[j].start()
        copy(0, sibling, me).wait_recv()
        for j, chip in enumerate(chips):
            copy(4 + j, (*chip, 1 - c), me).wait_recv()
        for cp in first + passed:
            cp.wait_send()
        mine.wait()

    return pl.pallas_call(
        body, name=name,
        out_shape=jax.ShapeDtypeStruct((N_DEV * m_per, n), x_shard.dtype),
        in_specs=[pl.BlockSpec(memory_space=pltpu.VMEM)],
        out_specs=pl.BlockSpec(memory_space=pltpu.VMEM),
        scratch_shapes=[pltpu.SemaphoreType.DMA((7,)), pltpu.SemaphoreType.DMA((7,)), pltpu.SemaphoreType.DMA],
        compiler_params=pltpu.CompilerParams(vmem_limit_bytes=VMEM_LIMIT),
    )(x_shard)


def _core_half(shape, cc):
    r, c = shape
    if r % 32 == 0:
        return (pl.ds(pl.multiple_of(cc * (r // 2), 16), r // 2), slice(None))
    assert c % (2 * LANES) == 0, shape
    return (slice(None), pl.ds(pl.multiple_of(cc * (c // 2), LANES), c // 2))


def _gather_weights(arrs, name):
    n = len(arrs)
    out_shapes = [jax.ShapeDtypeStruct((N_CHIP,) + a.shape, a.dtype) for a in arrs]

    def body(*refs):
        ins, outs = refs[:n], refs[n:2 * n]
        ici_send, ici_recv, d2d_send, d2d_recv = refs[2 * n:]
        x, y, c = lax.axis_index("x"), lax.axis_index("y"), lax.axis_index("c")
        me = 2 * x + y
        peers = [(1 - x, y), (x, 1 - y), (1 - x, 1 - y)]

        def half(a, cc):
            r = arrs[a].shape[0]
            cut = r // 32 * 16
            return pl.ds(0, cut) if cc == 0 else pl.ds(cut, r - cut)

        def exchange(mine):
            sibling = (x, y, 1 - mine)
            sent = []
            for a in range(n):
                for k, (px, py) in enumerate(peers):
                    cp = pltpu.make_async_remote_copy(
                        src_ref=ins[a].at[half(a, mine)], dst_ref=outs[a].at[me, half(a, mine)],
                        send_sem=ici_send.at[3 * a + k], recv_sem=ici_recv.at[3 * a + k],
                        device_id=(px, py, mine), device_id_type=MESH)
                    cp.start()
                    sent.append(cp)
            for a in range(n):
                for k, (px, py) in enumerate(peers):
                    landed = outs[a].at[2 * px + py, half(a, mine)]
                    pltpu.make_async_remote_copy(
                        src_ref=landed, dst_ref=landed, send_sem=ici_send.at[3 * a + k], recv_sem=ici_recv.at[3 * a + k],
                        device_id=(px, py, mine), device_id_type=MESH).wait_recv()
                    fwd = pltpu.make_async_remote_copy(
                        src_ref=landed, dst_ref=landed, send_sem=d2d_send.at[3 * a + k], recv_sem=d2d_recv.at[3 * a + k],
                        device_id=sibling, device_id_type=MESH)
                    fwd.start()
                    sent.append(fwd)
            for a in range(n):
                for k, (px, py) in enumerate(peers):
                    passed = outs[a].at[2 * px + py, half(a, 1 - mine)]
                    pltpu.make_async_remote_copy(
                        src_ref=passed, dst_ref=passed, send_sem=d2d_send.at[3 * a + k], recv_sem=d2d_recv.at[3 * a + k],
                        device_id=sibling, device_id_type=MESH).wait_recv()
            for cp in sent:
                cp.wait_send()

        for core in (0, 1):
            pl.when(c == core)(functools.partial(exchange, core))

    hbm = pl.BlockSpec(memory_space=pltpu.HBM)
    gathered = pl.pallas_call(
        body, name=name, out_shape=out_shapes, in_specs=[hbm] * n, out_specs=[hbm] * n,
        scratch_shapes=[pltpu.SemaphoreType.DMA((3 * n,))] * 4,
    )(*arrs)
    chip = 2 * lax.axis_index("x") + lax.axis_index("y")
    return [lax.dynamic_update_slice(g, a[None], (chip, 0, 0)) for g, a in zip(gathered, arrs)]


def _chip_peers():
    x, y, c = lax.axis_index("x"), lax.axis_index("y"), lax.axis_index("c")
    return [(1 - x, y, c), (x, 1 - y, c), (1 - x, 1 - y, c)], (x, y, 1 - c)


def _handshake(peers):
    barrier = pltpu.get_barrier_semaphore()
    for peer in peers:
        pl.semaphore_signal(barrier, inc=1, device_id=peer, device_id_type=MESH)
    pl.semaphore_wait(barrier, len(peers))


def _sequencer_gather_weights(arrs, name, collective_id):
    n = len(arrs)
    out_types = [jax.ShapeDtypeStruct((N_CHIP,) + a.shape, a.dtype) for a in arrs]

    def body(*refs):
        ins, outs = refs[:n], refs[n:2 * n]
        ici_send, ici_recv, d2d_send, d2d_recv = refs[2 * n:]
        chips, sibling = _chip_peers()
        _handshake(chips + [sibling])
        c = lax.axis_index("c")
        me = 2 * lax.axis_index("x") + lax.axis_index("y")

        def half(a, cc):
            return _core_half(arrs[a].shape, cc)

        sent = []
        for a in range(n):
            for k, peer in enumerate(chips):
                cp = pltpu.make_async_remote_copy(
                    src_ref=ins[a].at[half(a, c)], dst_ref=outs[a].at[(me,) + half(a, c)],
                    send_sem=ici_send.at[3 * a + k], recv_sem=ici_recv.at[3 * a + k], device_id=peer, device_id_type=MESH)
                cp.start()
                sent.append(cp)
        for a in range(n):
            for k, peer in enumerate(chips):
                landed = outs[a].at[(2 * peer[0] + peer[1],) + half(a, c)]
                pltpu.make_async_remote_copy(
                    src_ref=landed, dst_ref=landed, send_sem=ici_send.at[3 * a + k], recv_sem=ici_recv.at[3 * a + k],
                    device_id=peer, device_id_type=MESH).wait_recv()
                fwd = pltpu.make_async_remote_copy(
                    src_ref=landed, dst_ref=landed, send_sem=d2d_send.at[3 * a + k], recv_sem=d2d_recv.at[3 * a + k],
                    device_id=sibling, device_id_type=MESH)
                fwd.start()
                sent.append(fwd)
        for a in range(n):
            for k, peer in enumerate(chips):
                passed = outs[a].at[(2 * peer[0] + peer[1],) + half(a, 1 - c)]
                pltpu.make_async_remote_copy(
                    src_ref=passed, dst_ref=passed, send_sem=d2d_send.at[3 * a + k], recv_sem=d2d_recv.at[3 * a + k],
                    device_id=sibling, device_id_type=MESH).wait_recv()
        for cp in sent:
            cp.wait_send()

    gathered = pl.kernel(
        body, out_type=out_types, mesh=plsc.ScalarSubcoreMesh(axis_name="sequencer", num_cores=1), name=name,
        scratch_types=[pltpu.SemaphoreType.DMA((3 * n,))] * 4,
        compiler_params=pltpu.CompilerParams(collective_id=collective_id),
    )(*arrs)
    chip = 2 * lax.axis_index("x") + lax.axis_index("y")
    return [lax.dynamic_update_slice(g, a[None], (chip, 0, 0)) for g, a in zip(gathered, arrs)]


def _sequencer_chip_exchange(arrs, name, collective_id):
    n = len(arrs)
    out_types = [jax.ShapeDtypeStruct(a.shape, a.dtype) for a in arrs]

    def body(*refs):
        ins, outs = refs[:n], refs[n:2 * n]
        send_sems, recv_sems = refs[2 * n:]
        chips, _ = _chip_peers()
        _handshake(chips)
        me = 2 * lax.axis_index("x") + lax.axis_index("y")
        sent = []
        for a in range(n):
            for k, peer in enumerate(chips):
                cp = pltpu.make_async_remote_copy(
                    src_ref=ins[a].at[2 * peer[0] + peer[1]], dst_ref=outs[a].at[me],
                    send_sem=send_sems.at[3 * a + k], recv_sem=recv_sems.at[3 * a + k], device_id=peer, device_id_type=MESH)
                cp.start()
                sent.append(cp)
        for a in range(n):
            for k, peer in enumerate(chips):
                landed = outs[a].at[2 * peer[0] + peer[1]]
                pltpu.make_async_remote_copy(
                    src_ref=landed, dst_ref=landed, send_sem=send_sems.at[3 * a + k], recv_sem=recv_sems.at[3 * a + k],
                    device_id=peer, device_id_type=MESH).wait_recv()
        for cp in sent:
            cp.wait_send()

    received = pl.kernel(
        body, out_type=out_types, mesh=plsc.ScalarSubcoreMesh(axis_name="sequencer", num_cores=1), name=name,
        scratch_types=[pltpu.SemaphoreType.DMA((3 * n,))] * 2,
        compiler_params=pltpu.CompilerParams(collective_id=collective_id),
    )(*arrs)
    chip = 2 * lax.axis_index("x") + lax.axis_index("y")
    return [lax.dynamic_update_slice(r, lax.dynamic_slice(a, (chip, 0, 0), (1,) + a.shape[1:]), (chip, 0, 0))
            for r, a in zip(received, arrs)]


def _sibling_exchange(arrs, name):
    n = len(arrs)

    def body(*refs):
        ins, outs = refs[:n], refs[n:2 * n]
        send_sems, recv_sems = refs[2 * n:]
        sibling = (lax.axis_index("x"), lax.axis_index("y"), 1 - lax.axis_index("c"))
        cps = []
        for a in range(n):
            cp = pltpu.make_async_remote_copy(src_ref=ins[a], dst_ref=outs[a], send_sem=send_sems.at[a],
                                              recv_sem=recv_sems.at[a], device_id=sibling, device_id_type=MESH)
            cp.start()
            cps.append(cp)
        for cp in cps:
            cp.wait_recv()
        for cp in cps:
            cp.wait_send()

    hbm = pl.BlockSpec(memory_space=pltpu.HBM)
    return pl.pallas_call(
        body, name=name, out_shape=[jax.ShapeDtypeStruct(a.shape, a.dtype) for a in arrs],
        in_specs=[hbm] * n, out_specs=[hbm] * n,
        scratch_shapes=[pltpu.SemaphoreType.DMA((n,)), pltpu.SemaphoreType.DMA((n,))],
    )(*arrs)


def _matmul(a, b, mode, out_dtype, name, tm=1024, tn=1024, tk=2048, relu2=False, times=None, b_split=False,
            out_split=False):
    b_shape = (b.shape[1], b.shape[2] * N_CHIP) if b_split else b.shape
    if mode == "nn":
        (m, k), (k2, n) = a.shape, b_shape
    elif mode == "nt":
        (m, k), (n, k2) = a.shape, b_shape
    else:
        (k, m), (k2, n) = a.shape, b_shape
    assert k == k2, (a.shape, b.shape, mode)
    n_cut = n // N_CHIP if (out_split or (b_split and mode != "nt")) else n
    k_cut = k // N_CHIP if (b_split and mode == "nt") else k
    tm, tn, tk = _tile(m, tm, LANES), _tile(n_cut, tn, LANES), _tile(k_cut, tk, LANES)
    assert n_cut % tn == 0 and k_cut % tk == 0 and m % tm == 0, (name, m, n, k, tm, tn, tk)
    nk = k // tk
    nbc, nkc = n_cut // tn, k_cut // tk
    n_in = 2 if times is None else 3
    n_out = 2 if relu2 else 1

    def product(a_ref, b_ref):
        if mode == "nn":
            return jnp.dot(a_ref[...], b_ref[...], preferred_element_type=F32)
        return lax.dot_general(a_ref[...], b_ref[...], NT_DIMS if mode == "nt" else TN_DIMS, preferred_element_type=F32)

    def finish(p, refs, o_refs):
        if relu2:
            p = jnp.maximum(p, 0.0)
            o_refs[0][...] = p.astype(o_refs[0].dtype)
            o_refs[1][...] = (p * p).astype(o_refs[1].dtype)
        elif times is not None:
            o_refs[0][...] = (2.0 * refs[2][...].astype(F32) * p).astype(o_refs[0].dtype)
        else:
            o_refs[0][...] = p.astype(o_refs[0].dtype)

    def body(*refs):
        o_refs = refs[n_in:n_in + n_out]
        if nk == 1:
            finish(product(refs[0], refs[1]), refs, o_refs)
            return
        acc_ref = refs[n_in + n_out]
        kk = pl.program_id(2)

        @pl.when(kk == 0)
        def _():
            acc_ref[...] = product(refs[0], refs[1])

        @pl.when((kk > 0) & (kk < nk - 1))
        def _():
            acc_ref[...] += product(refs[0], refs[1])

        @pl.when(kk == nk - 1)
        def _():
            finish(acc_ref[...] + product(refs[0], refs[1]), refs, o_refs)

    if mode == "tn":
        a_spec = pl.BlockSpec((tk, tm), lambda i, j, kk: (kk, i))
    else:
        a_spec = pl.BlockSpec((tm, tk), lambda i, j, kk: (i, kk))
    if mode == "nt":
        b_spec = (pl.BlockSpec((None, tn, tk), lambda i, j, kk: (kk // nkc, j, kk % nkc)) if b_split
                  else pl.BlockSpec((tn, tk), lambda i, j, kk: (j, kk)))
    else:
        b_spec = (pl.BlockSpec((None, tk, tn), lambda i, j, kk: (j // nbc, kk, j % nbc)) if b_split
                  else pl.BlockSpec((tk, tn), lambda i, j, kk: (kk, j)))
    mn_spec = pl.BlockSpec((tm, tn), lambda i, j, kk: (i, j))
    if out_split:
        o_spec = pl.BlockSpec((None, tm, tn), lambda i, j, kk: (j // nbc, i, j % nbc))
        o_shape = jax.ShapeDtypeStruct((N_CHIP, m, n_cut), out_dtype)
    else:
        o_spec, o_shape = mn_spec, jax.ShapeDtypeStruct((m, n), out_dtype)
    out = pl.pallas_call(
        body, name=name, grid=(m // tm, n // tn, nk), in_specs=[a_spec, b_spec] + [mn_spec] * (n_in - 2),
        out_specs=[o_spec] * n_out, out_shape=[o_shape] * n_out,
        scratch_shapes=[] if nk == 1 else [pltpu.VMEM((tm, tn), F32)],
        compiler_params=_params(("parallel", "parallel", "arbitrary")),
    )(*((a, b) if times is None else (a, b, times)))
    return out if relu2 else out[0]


def _mod_part(c_all, w_s, b_s, name):
    d, na = w_s.shape
    tn = _tile(na, 512, LANES)

    def body(c_ref, w_ref, b_ref, o_ref):
        ca = _silu(c_ref[...]).astype(BF16)
        o_ref[...] = jnp.dot(ca, w_ref[...].astype(BF16), preferred_element_type=F32) + b_ref[...]

    return pl.pallas_call(
        body, name=name, grid=(na // tn,),
        in_specs=[pl.BlockSpec((N_DEV, d), lambda j: (0, 0)), pl.BlockSpec((d, tn), lambda j: (0, j)),
                  pl.BlockSpec((1, tn), lambda j: (0, j))],
        out_specs=pl.BlockSpec((N_DEV, tn), lambda j: (0, j)),
        out_shape=jax.ShapeDtypeStruct((N_DEV, na), F32), compiler_params=_params(("parallel",)),
    )(c_all, w_s, b_s)


def _wada_grad(c_all, dmod_s, name):
    d = c_all.shape[1]
    na = dmod_s.shape[1]
    td, tn = _tile(d, 512, LANES), _tile(na, 512, LANES)

    def body(c_ref, g_ref, o_ref):
        o_ref[...] = lax.dot_general(_silu(c_ref[...]), g_ref[...], TN_DIMS, precision=HI, preferred_element_type=F32)

    return pl.pallas_call(
        body, name=name, grid=(d // td, na // tn),
        in_specs=[pl.BlockSpec((N_DEV, td), lambda i, j: (0, i)), pl.BlockSpec((N_DEV, tn), lambda i, j: (0, j))],
        out_specs=pl.BlockSpec((td, tn), lambda i, j: (i, j)),
        out_shape=jax.ShapeDtypeStruct((d, na), F32), compiler_params=_params(("parallel", "parallel")),
    )(c_all, dmod_s)


def _row_specs(tb, d, n_full, n_vec):
    full = pl.BlockSpec((tb, d), lambda i: (i, 0))
    vec = pl.BlockSpec((1, d), lambda i: (0, 0))
    return [full] * n_full + [vec] * n_vec


def _norm_mod(x, w, sc, sh, name):
    t, d = x.shape
    tb = _tile(t, 256, SUBLANES)

    def body(x_ref, w_ref, sc_ref, sh_ref, o_ref):
        xv = x_ref[...]
        o_ref[...] = (xv * _rms_scale(xv) * w_ref[...] * (1.0 + sc_ref[...]) + sh_ref[...]).astype(o_ref.dtype)

    return pl.pallas_call(
        body, name=name, grid=(t // tb,), in_specs=_row_specs(tb, d, 1, 3),
        out_specs=pl.BlockSpec((tb, d), lambda i: (i, 0)), out_shape=jax.ShapeDtypeStruct((t, d), BF16),
        compiler_params=_params(("parallel",)),
    )(x, w, sc, sh)


def _norm_mod_bwd(x, w, sc, dh, dres, name):
    t, d = x.shape
    tb = _tile(t, 256, SUBLANES)

    def body(x_ref, w_ref, sc_ref, dh_ref, dres_ref, dx_ref, dw_ref, dsc_ref, dsh_ref):
        @pl.when(pl.program_id(0) == 0)
        def _():
            dw_ref[...] = jnp.zeros_like(dw_ref)
            dsc_ref[...] = jnp.zeros_like(dsc_ref)
            dsh_ref[...] = jnp.zeros_like(dsh_ref)

        xv = x_ref[...]
        r = _rms_scale(xv)
        xn = xv * r
        g = dh_ref[...].astype(F32)
        wv, one_sc = w_ref[...], 1.0 + sc_ref[...]
        gxn = g * xn
        dsh_ref[...] += jnp.sum(g, axis=0, keepdims=True)
        dsc_ref[...] += jnp.sum(gxn, axis=0, keepdims=True) * wv
        dw_ref[...] += jnp.sum(gxn, axis=0, keepdims=True) * one_sc
        dxn = g * (wv * one_sc)
        dx_ref[...] = dres_ref[...] + r * (dxn - xn * jnp.mean(dxn * xn, axis=-1, keepdims=True))

    vec_out = pl.BlockSpec((1, d), lambda i: (0, 0))
    return pl.pallas_call(
        body, name=name, grid=(t // tb,),
        in_specs=[pl.BlockSpec((tb, d), lambda i: (i, 0)), pl.BlockSpec((1, d), lambda i: (0, 0)),
                  pl.BlockSpec((1, d), lambda i: (0, 0)), pl.BlockSpec((tb, d), lambda i: (i, 0)),
                  pl.BlockSpec((tb, d), lambda i: (i, 0))],
        out_specs=[pl.BlockSpec((tb, d), lambda i: (i, 0)), vec_out, vec_out, vec_out],
        out_shape=[jax.ShapeDtypeStruct((t, d), F32)] + [jax.ShapeDtypeStruct((1, d), F32)] * 3,
        compiler_params=_params(("arbitrary",)),
    )(x, w, sc, dh, dres)


def _resid(x, y, w, gt, name):
    t, d = x.shape
    tb = _tile(t, 256, SUBLANES)

    def body(x_ref, y_ref, w_ref, gt_ref, o_ref):
        yv = y_ref[...]
        o_ref[...] = x_ref[...] + gt_ref[...] * (yv * _rms_scale(yv) * w_ref[...])

    return pl.pallas_call(
        body, name=name, grid=(t // tb,), in_specs=_row_specs(tb, d, 2, 2),
        out_specs=pl.BlockSpec((tb, d), lambda i: (i, 0)), out_shape=jax.ShapeDtypeStruct((t, d), F32),
        compiler_params=_params(("parallel",)),
    )(x, y, w, gt)


def _loss_head(x2, y2, w, gt, target, name):
    t, d = x2.shape
    tb = _tile(t, 256, SUBLANES)

    def body(x_ref, y_ref, tg_ref, w_ref, gt_ref, do_ref, loss_ref):
        @pl.when(pl.program_id(0) == 0)
        def _():
            loss_ref[...] = jnp.zeros_like(loss_ref)

        yv = y_ref[...]
        out = x_ref[...] + gt_ref[...] * (yv * _rms_scale(yv) * w_ref[...])
        err = out - tg_ref[...]
        do_ref[...] = err * (1.0 / d)
        per_tok = jnp.mean(err * err, axis=-1, keepdims=True)
        loss_ref[...] += 0.5 * jnp.sum(per_tok, axis=0, keepdims=True)

    return pl.pallas_call(
        body, name=name, grid=(t // tb,), in_specs=_row_specs(tb, d, 3, 2),
        out_specs=[pl.BlockSpec((tb, d), lambda i: (i, 0)), pl.BlockSpec((1, LANES), lambda i: (0, 0))],
        out_shape=[jax.ShapeDtypeStruct((t, d), F32), jax.ShapeDtypeStruct((1, LANES), F32)],
        compiler_params=_params(("arbitrary",)),
    )(x2, y2, target, w, gt)


def _resid_bwd(dout, y, w, gt, name):
    t, d = y.shape
    tb = _tile(t, 256, SUBLANES)

    def body(do_ref, y_ref, w_ref, gt_ref, dy_ref, dgt_ref, dw_ref):
        @pl.when(pl.program_id(0) == 0)
        def _():
            dgt_ref[...] = jnp.zeros_like(dgt_ref)
            dw_ref[...] = jnp.zeros_like(dw_ref)

        yv, g = y_ref[...], do_ref[...]
        r = _rms_scale(yv)
        yn = yv * r
        wv, gtv = w_ref[...], gt_ref[...]
        gyn = jnp.sum(g * yn, axis=0, keepdims=True)
        dgt_ref[...] += gyn * wv
        dw_ref[...] += gyn * gtv
        dyn = g * (gtv * wv)
        dy_ref[...] = (r * (dyn - yn * jnp.mean(dyn * yn, axis=-1, keepdims=True))).astype(dy_ref.dtype)

    vec_out = pl.BlockSpec((1, d), lambda i: (0, 0))
    return pl.pallas_call(
        body, name=name, grid=(t // tb,), in_specs=_row_specs(tb, d, 2, 2),
        out_specs=[pl.BlockSpec((tb, d), lambda i: (i, 0)), vec_out, vec_out],
        out_shape=[jax.ShapeDtypeStruct((t, d), BF16)] + [jax.ShapeDtypeStruct((1, d), F32)] * 2,
        compiler_params=_params(("arbitrary",)),
    )(dout, y, w, gt)


def _row_half_to_bf16(full, which, sib, name):
    n, r, c = full.shape
    by_rows = r % 32 == 0
    r, c = (r // 2, c) if by_rows else (r, c // 2)
    tr, tc = _elementwise_tiles(r, c)
    nbh = (r // tr) if by_rows else (c // tc)

    def body(which_ref, a_ref, *rest):
        if sib is None:
            rest[0][...] = a_ref[...].astype(BF16)
        else:
            rest[1][...] = (a_ref[...] + rest[0][...].astype(F32)).astype(BF16)

    if by_rows:
        half_spec = pl.BlockSpec((1, tr, tc), lambda j, i, k, which_ref: (j, which_ref[0] * nbh + i, k))
    else:
        half_spec = pl.BlockSpec((1, tr, tc), lambda j, i, k, which_ref: (j, i, which_ref[0] * nbh + k))
    spec = pl.BlockSpec((1, tr, tc), lambda j, i, k, which_ref: (j, i, k))
    grid_spec = pltpu.PrefetchScalarGridSpec(
        num_scalar_prefetch=1, grid=(n, r // tr, c // tc), in_specs=[half_spec] + ([] if sib is None else [spec]), out_specs=spec)
    return pl.pallas_call(
        body, name=name, grid_spec=grid_spec, out_shape=jax.ShapeDtypeStruct((n, r, c), BF16),
        compiler_params=_params(("parallel", "parallel", "parallel")),
    )(which, full, *([] if sib is None else [sib]))


def _sum_chips(recv, name):
    _, r, c = recv.shape
    tr, tc = _elementwise_tiles(r, c)

    def body(x_ref, o_ref):
        acc = x_ref[0].astype(F32)
        for j in range(1, N_CHIP):
            acc = acc + x_ref[j].astype(F32)
        o_ref[...] = acc

    return pl.pallas_call(
        body, name=name, grid=(r // tr, c // tc), in_specs=[pl.BlockSpec((N_CHIP, tr, tc), lambda i, j: (0, i, j))],
        out_specs=pl.BlockSpec((tr, tc), lambda i, j: (i, j)), out_shape=jax.ShapeDtypeStruct((r, c), F32),
        compiler_params=_params(("parallel", "parallel")),
    )(recv)


def _adamw(w, g_parts, m, v, name, by_core=False):
    r, c = w.shape
    by_rows = r % 32 == 0
    if by_core:
        tr, tc = _elementwise_tiles(*((r // 2, c) if by_rows else (r, c // 2)))
        nbh = (r // 2) // tr if by_rows else (c // 2) // tc
    else:
        tr, tc = _elementwise_tiles(r, c)
    n_g = len(g_parts)
    c1 = 1.0 / (1.0 - ADAM_B1 ** ADAM_STEP)
    c2 = 1.0 / (1.0 - ADAM_B2 ** ADAM_STEP)

    def body(*refs):
        w_ref, g_refs, m_ref, v_ref = refs[0], refs[1:1 + n_g], refs[1 + n_g], refs[2 + n_g]
        g_out, d_out, m_out, v_out = refs[3 + n_g:]
        if by_core:
            in_my_half = (pl.program_id(0 if by_rows else 1) // nbh) == lax.axis_index("c")
            g = jnp.where(in_my_half, g_refs[0][...], g_refs[1][...])
        else:
            g = g_refs[0][...]
            for extra in g_refs[1:]:
                g = g + extra[...]
        mn = ADAM_B1 * m_ref[...] + (1.0 - ADAM_B1) * g
        vn = ADAM_B2 * v_ref[...] + (1.0 - ADAM_B2) * (g * g)
        g_out[...] = g
        m_out[...] = mn
        v_out[...] = vn
        d_out[...] = -ADAM_LR * ((mn * c1) / (jnp.sqrt(vn * c2) + ADAM_EPS) + ADAM_WD * w_ref[...])

    spec = pl.BlockSpec((tr, tc), lambda i, j: (i, j))
    if by_core:
        g_spec = pl.BlockSpec((tr, tc), (lambda i, j: (i % nbh, j)) if by_rows else (lambda i, j: (i, j % nbh)))
    else:
        g_spec = spec
    return pl.pallas_call(
        body, name=name, grid=(r // tr, c // tc), in_specs=[spec] + [g_spec] * n_g + [spec] * 2, out_specs=[spec] * 4,
        out_shape=[jax.ShapeDtypeStruct((r, c), F32)] * 4, compiler_params=_params(("parallel", "parallel")),
    )(w, *g_parts, m, v)


def _conv_taps(u, t):
    rows = lax.broadcasted_iota(jnp.int32, u.shape, 0)
    return [u] + [jnp.where(rows >= dd, pltpu.roll(u, dd, 0), 0.0) for dd in range(1, CONV_K)]


def _conv_fwd(proj, conv_w, col0, name):
    t = proj.shape[0]
    ch = conv_w.shape[1]

    def body(u_ref, w_ref, o_ref):
        taps = _conv_taps(u_ref[...], t)
        wv = w_ref[...]
        y = taps[0] * wv[CONV_K - 1:CONV_K]
        for dd in range(1, CONV_K):
            y = y + taps[dd] * wv[CONV_K - 1 - dd:CONV_K - dd]
        o_ref[...] = _silu(y)

    return pl.pallas_call(
        body, name=name, grid=(ch // LANES,),
        in_specs=[pl.BlockSpec((t, LANES), lambda j: (0, col0 + j)), pl.BlockSpec((CONV_K, LANES), lambda j: (0, j))],
        out_specs=pl.BlockSpec((t, LANES), lambda j: (0, j)), out_shape=jax.ShapeDtypeStruct((t, ch), F32),
        compiler_params=_params(("parallel",)),
    )(proj, conv_w)


def _conv_bwd(proj, conv_w, ds, col0, name):
    t = proj.shape[0]
    ch = conv_w.shape[1]

    def body(u_ref, w_ref, ds_ref, du_ref, dw_ref):
        u = u_ref[...]
        taps = _conv_taps(u, t)
        wv = w_ref[...]
        y = taps[0] * wv[CONV_K - 1:CONV_K]
        for dd in range(1, CONV_K):
            y = y + taps[dd] * wv[CONV_K - 1 - dd:CONV_K - dd]
        sg = jax.nn.sigmoid(y)
        dy = ds_ref[...] * (sg * (1.0 + y * (1.0 - sg)))
        rows = lax.broadcasted_iota(jnp.int32, u.shape, 0)
        du = dy * wv[CONV_K - 1:CONV_K]
        for dd in range(1, CONV_K):
            ahead = jnp.where(rows < t - dd, pltpu.roll(dy, t - dd, 0), 0.0)
            du = du + ahead * wv[CONV_K - 1 - dd:CONV_K - dd]
        du_ref[...] = du.astype(du_ref.dtype)
        dws = [jnp.sum(dy * taps[CONV_K - 1 - j], axis=0, keepdims=True) for j in range(CONV_K)]
        dw_ref[...] = jnp.concatenate(dws, axis=0)

    return pl.pallas_call(
        body, name=name, grid=(ch // LANES,),
        in_specs=[pl.BlockSpec((t, LANES), lambda j: (0, col0 + j)), pl.BlockSpec((CONV_K, LANES), lambda j: (0, j)),
                  pl.BlockSpec((t, LANES), lambda j: (0, j))],
        out_specs=[pl.BlockSpec((t, LANES), lambda j: (0, j)), pl.BlockSpec((CONV_K, LANES), lambda j: (0, j))],
        out_shape=[jax.ShapeDtypeStruct((t, ch), BF16), jax.ShapeDtypeStruct((CONV_K, ch), F32)],
        compiler_params=_params(("parallel",)),
    )(proj, conv_w, ds)


def _hg_block(st, q, fl, vi, g, l0, l1, nw):
    hs = range(len(st))
    tb = q[0].shape[0]
    ln = HG_SUB
    lb = [jax.nn.sigmoid(l0[h] - l1[h]) for h in hs]
    rows = lax.broadcasted_iota(jnp.int32, (ln, HEAD), 0)
    tri = (lax.broadcasted_iota(jnp.int32, (ln, ln), 0) >= lax.broadcasted_iota(jnp.int32, (ln, ln), 1)).astype(F32)
    st = list(st)
    outs = [[] for _ in hs]
    for i in range(tb // ln):
        sl = slice(i * ln, (i + 1) * ln)
        qs, vs = [q[h][sl] for h in hs], [vi[h][sl] for h in hs]
        f = [lb[h] + (1.0 - lb[h]) * jax.nn.sigmoid(fl[h][sl]) for h in hs]
        k = [1.0 - f[h] for h in hs]
        b = [jnp.dot(tri, jnp.log(f[h]), precision=HI, preferred_element_type=F32) for h in hs]
        o = [lax.dot_general((qs[h] * jnp.exp(b[h])).astype(BF16), st[h].astype(BF16), NT_DIMS, preferred_element_type=F32)
             for h in hs]
        for s in range(ln):
            e = [jnp.exp(jnp.where(rows >= s, b[h] - b[h][s:s + 1], -1e30)) for h in hs]
            a = [jnp.sum(qs[h] * e[h] * k[h][s:s + 1], axis=-1, keepdims=True) for h in hs]
            o = [o[h] + a[h] * vs[h][s:s + 1] for h in hs]
        kt = [k[h] * jnp.exp(b[h][ln - 1:ln] - b[h]) for h in hs]
        upd = [lax.dot_general(vs[h].astype(BF16), kt[h].astype(BF16), TN_DIMS, preferred_element_type=F32) for h in hs]
        st = [st[h] * jnp.exp(b[h][ln - 1:ln]) + upd[h] for h in hs]
        for h in hs:
            outs[h].append(o[h])
    o = [jnp.concatenate(outs[h], axis=0) for h in hs]
    out = [o[h] * _rms_scale(o[h]) * nw * _silu(g[h]) for h in hs]
    return st, out


def _head_cols(h):
    return slice(h * HEAD, (h + 1) * HEAD)


def _head_groups(n_heads, group):
    g = min(group, n_heads)
    return [list(range(i, min(i + g, n_heads))) for i in range(0, n_heads, g)]


def _hg_in_specs(n_heads, tb, time_index):
    hw = n_heads * HEAD
    cols = [pl.BlockSpec((tb, hw), functools.partial(lambda part, j: (time_index(j), part), part)) for part in range(4)]
    head_rows = pl.BlockSpec((n_heads, 1, HEAD), lambda j: (0, 0, 0))
    return cols + [head_rows, head_rows, pl.BlockSpec((1, HEAD), lambda j: (0, 0))]


def _hgrn2_fwd(proj, l0, l1, nw, n_heads, name):
    t = proj.shape[0]
    hw = n_heads * HEAD
    tb = _tile(t, HG_BLOCK, HG_SUB)
    nb = t // tb

    def body(q_ref, f_ref, i_ref, g_ref, l0_ref, l1_ref, nw_ref, o_ref, save_ref, st_ref):
        @pl.when(pl.program_id(0) == 0)
        def _():
            st_ref[...] = jnp.zeros_like(st_ref)

        for hs in _head_groups(n_heads, HG_FWD_GROUP):
            st = [st_ref[h] for h in hs]
            for h, s in zip(hs, st):
                save_ref[h] = s
            st, out = _hg_block(st, *[[r[:, _head_cols(h)] for h in hs] for r in (q_ref, f_ref, i_ref, g_ref)],
                                [l0_ref[h] for h in hs], [l1_ref[h] for h in hs], nw_ref[...])
            for h, s, o in zip(hs, st, out):
                st_ref[h] = s
                o_ref[:, _head_cols(h)] = o.astype(o_ref.dtype)

    return pl.pallas_call(
        body, name=name, grid=(nb,), in_specs=_hg_in_specs(n_heads, tb, lambda j: j),
        out_specs=[pl.BlockSpec((tb, hw), lambda j: (j, 0)),
                   pl.BlockSpec((None, n_heads, HEAD, HEAD), lambda j: (j, 0, 0, 0))],
        out_shape=[jax.ShapeDtypeStruct((t, hw), BF16), jax.ShapeDtypeStruct((nb, n_heads, HEAD, HEAD), F32)],
        scratch_shapes=[pltpu.VMEM((n_heads, HEAD, HEAD), F32)], compiler_params=_params(("arbitrary",)),
    )(proj, proj, proj, proj, l0, l1, nw)


def _hgrn2_bwd(proj, l0, l1, nw, saved, d_ocat, n_heads, name):
    t = proj.shape[0]
    tb = _tile(t, HG_BLOCK, HG_SUB)
    nb = t // tb
    rev = lambda j: nb - 1 - j

    hw = n_heads * HEAD

    def body(q_ref, f_ref, i_ref, g_ref, l0_ref, l1_ref, nw_ref, save_ref, do_ref,
             dp_ref, dl0_ref, dl1_ref, dnw_ref, dst_ref):
        @pl.when(pl.program_id(0) == 0)
        def _():
            dst_ref[...] = jnp.zeros_like(dst_ref)
            dl0_ref[...] = jnp.zeros_like(dl0_ref)
            dl1_ref[...] = jnp.zeros_like(dl1_ref)
            dnw_ref[...] = jnp.zeros_like(dnw_ref)

        dnw_acc = jnp.zeros((1, HEAD), F32)
        for hs in _head_groups(n_heads, HG_BWD_GROUP):
            _, vjp = jax.vjp(_hg_block, [save_ref[h] for h in hs],
                             *[[r[:, _head_cols(h)] for h in hs] for r in (q_ref, f_ref, i_ref, g_ref)],
                             [l0_ref[h] for h in hs], [l1_ref[h] for h in hs], nw_ref[...])
            dst, dq, df, di, dg, dl0, dl1, dnw = vjp(([dst_ref[h] for h in hs], [do_ref[:, _head_cols(h)] for h in hs]))
            for i, h in enumerate(hs):
                dst_ref[h] = dst[i]
                for part, val in enumerate((dq, df, di, dg)):
                    dp_ref[:, part * hw + h * HEAD:part * hw + (h + 1) * HEAD] = val[i].astype(dp_ref.dtype)
                dl0_ref[h] += dl0[i]
                dl1_ref[h] += dl1[i]
            dnw_acc = dnw_acc + dnw
        dnw_ref[...] += dnw_acc

    head_rows = pl.BlockSpec((n_heads, 1, HEAD), lambda j: (0, 0, 0))
    return pl.pallas_call(
        body, name=name, grid=(nb,),
        in_specs=_hg_in_specs(n_heads, tb, rev) + [pl.BlockSpec((None, n_heads, HEAD, HEAD), lambda j: (rev(j), 0, 0, 0)),
                                                   pl.BlockSpec((tb, hw), lambda j: (rev(j), 0))],
        out_specs=[pl.BlockSpec((tb, 4 * hw), lambda j: (rev(j), 0)), head_rows, head_rows,
                   pl.BlockSpec((1, HEAD), lambda j: (0, 0))],
        out_shape=[jax.ShapeDtypeStruct((t, 4 * hw), BF16)] + [jax.ShapeDtypeStruct((n_heads, 1, HEAD), F32)] * 2
        + [jax.ShapeDtypeStruct((1, HEAD), F32)],
        scratch_shapes=[pltpu.VMEM((n_heads, HEAD, HEAD), F32)], compiler_params=_params(("arbitrary",)),
    )(proj, proj, proj, proj, l0, l1, nw, saved, d_ocat)


NN_DIMS = (((1,), (0,)), ((), ()))


def _split_bf16(x):
    hi = x.astype(BF16)
    return hi, (x - hi.astype(F32)).astype(BF16)


def _dot3(a, b, dims=NN_DIMS):
    (ah, al), (bh, bl) = _split_bf16(a), _split_bf16(b)
    dot = functools.partial(lax.dot_general, dimension_numbers=dims, preferred_element_type=F32)
    return dot(ah, bh) + dot(ah, bl) + dot(al, bh)


@jax.custom_vjp
def _mm3(a, b):
    return _dot3(a, b)


def _mm3_fwd(a, b):
    return _dot3(a, b), (a, b)


def _mm3_bwd(res, g):
    a, b = res
    return _dot3(g, b, NT_DIMS), _dot3(a, g, TN_DIMS)


_mm3.defvjp(_mm3_fwd, _mm3_bwd)


def _dot_bf16(a, b, dims=(((1,), (0,)), ((), ()))):
    return lax.dot_general(a.astype(BF16), b.astype(BF16), dims, preferred_element_type=F32)


def _inv_unit_lower_raw(ms):
    hs = range(len(ms))
    c = ms[0].shape[0]
    r = lax.broadcasted_iota(jnp.int32, (c, c), 0)
    q = lax.broadcasted_iota(jnp.int32, (c, c), 1)
    eye = (r == q).astype(F32)
    md = [jnp.where((r // GDN_INV_BLOCK) == (q // GDN_INV_BLOCK), ms[h], 0.0) for h in hs]
    p = [-md[h] for h in hs]
    t16 = [eye + p[h] for h in hs]
    for _ in range(int(math.log2(GDN_INV_BLOCK)) - 1):
        p = [_dot3(p[h], p[h]) for h in hs]
        t16 = [t16[h] + _dot3(t16[h], p[h]) for h in hs]
    p = [-_dot3(t16[h], ms[h] - md[h]) for h in hs]
    t2 = [eye + p[h] for h in hs]
    for _ in range(int(math.log2(c // GDN_INV_BLOCK)) - 1):
        p = [_dot3(p[h], p[h]) for h in hs]
        t2 = [t2[h] + _dot3(t2[h], p[h]) for h in hs]
    return [_dot3(t2[h], t16[h]) for h in hs]


@jax.custom_vjp
def _inv_unit_lower(ms):
    return _inv_unit_lower_raw(ms)


def _inv_fwd(ms):
    ts = _inv_unit_lower_raw(ms)
    return ts, ts


def _inv_bwd(ts, dts):
    hs = range(len(ts))
    inner = [_dot3(ts[h], dts[h], TN_DIMS) for h in hs]
    return ([-_dot3(inner[h], ts[h], NT_DIMS) for h in hs],)


_inv_unit_lower.defvjp(_inv_fwd, _inv_bwd)


def _gdn_block(precise, onehots, st, qc, kc, vc, g, ab, alog_row, dtb_row, nw):
    inverse, dot3 = precise
    hs = range(len(st))
    c = qc[0].shape[0]
    lane_sum = lambda v: jnp.sum(v, axis=-1, keepdims=True)
    a = [lane_sum(ab * onehots[h][0]) for h in hs]
    bb = [lane_sum(ab * onehots[h][1]) for h in hs]
    alog = [lane_sum(alog_row * onehots[h][0]) for h in hs]
    dtb = [lane_sum(dtb_row * onehots[h][0]) for h in hs]
    la = [-jnp.exp(alog[h]) * _softplus(a[h] + dtb[h]) for h in hs]
    beta = [jax.nn.sigmoid(bb[h]) for h in hs]
    q = [qc[h] * lax.rsqrt(lane_sum(qc[h] * qc[h]) + EPS) * (HEAD ** -0.5) for h in hs]
    k = [kc[h] * lax.rsqrt(lane_sum(kc[h] * kc[h]) + EPS) for h in hs]
    r = lax.broadcasted_iota(jnp.int32, (c, c), 0)
    s = lax.broadcasted_iota(jnp.int32, (c, c), 1)
    tri = (r >= s).astype(F32)
    g_cc = [dot3(tri, jnp.broadcast_to(la[h], (c, c))) for h in hs]
    g_cl = [dot3(tri, jnp.broadcast_to(la[h], (c, HEAD))) for h in hs]
    gamma = [jnp.exp(jnp.where(r >= s, g_cc[h] - g_cc[h].T, -1e30)) for h in hs]
    kk = [_dot_bf16(k[h], k[h], NT_DIMS) for h in hs]
    m = [jnp.where(r > s, beta[h] * kk[h] * gamma[h], 0.0) for h in hs]
    tm = inverse(m)
    eg = [jnp.exp(g_cl[h]) for h in hs]
    rhs = [jnp.concatenate([vc[h] * beta[h], k[h] * (beta[h] * eg[h])], axis=1) for h in hs]
    sol = [dot3(tm[h], rhs[h]) for h in hs]
    qk = [_dot_bf16(q[h], k[h], NT_DIMS) * gamma[h] for h in hs]
    g_last = [g_cl[h][c - 1:c] for h in hs]
    k_tail = [k[h] * jnp.exp(g_last[h] - g_cl[h]) for h in hs]
    v_new = [sol[h][:, :HEAD] - _dot_bf16(sol[h][:, HEAD:], st[h], NT_DIMS) for h in hs]
    o_st = [_dot_bf16(q[h] * eg[h], st[h], NT_DIMS) for h in hs]
    o = [o_st[h] + _dot_bf16(qk[h], v_new[h]) for h in hs]
    upd = [_dot_bf16(v_new[h], k_tail[h], TN_DIMS) for h in hs]
    st = [st[h] * jnp.exp(g_last[h]) + upd[h] for h in hs]
    out = [o[h] * _rms_scale(o[h]) * nw * _silu(g[h]) for h in hs]
    return st, out


def _head_onehots(n_heads, h):
    lane = lax.broadcasted_iota(jnp.int32, (1, LANES), 1)
    return (lane == h).astype(F32), (lane == n_heads + h).astype(F32)


def _gdn_in_specs(n_heads, c, time_index):
    hw = n_heads * HEAD
    qkv = [pl.BlockSpec((c, hw), functools.partial(lambda part, j: (time_index(j), part), part)) for part in range(3)]
    row = pl.BlockSpec((1, LANES), lambda j: (0, 0))
    return qkv + [pl.BlockSpec((c, hw), lambda j: (time_index(j), 7)),
                  pl.BlockSpec((c, LANES), lambda j: (time_index(j), 8 * n_heads)), row, row, row]


def _gdn_fwd(qkv, proj, alog_row, dtb_row, nw, n_heads, name):
    t = qkv.shape[0]
    hw = n_heads * HEAD
    c = _tile(t, GDN_CHUNK, GDN_CHUNK)
    nb = t // c

    def body(q_ref, k_ref, v_ref, g_ref, ab_ref, al_ref, dt_ref, nw_ref, o_ref, save_ref, st_ref):
        @pl.when(pl.program_id(0) == 0)
        def _():
            st_ref[...] = jnp.zeros_like(st_ref)

        for hs in _head_groups(n_heads, GDN_FWD_GROUP):
            st = [st_ref[h] for h in hs]
            for h, s in zip(hs, st):
                save_ref[h] = s
            st, out = _gdn_block((_inv_unit_lower_raw, _dot3), [_head_onehots(n_heads, h) for h in hs], st,
                                 *[[r[:, _head_cols(h)] for h in hs] for r in (q_ref, k_ref, v_ref, g_ref)],
                                 ab_ref[...], al_ref[...], dt_ref[...], nw_ref[...])
            for h, s, o in zip(hs, st, out):
                st_ref[h] = s
                o_ref[:, _head_cols(h)] = o.astype(o_ref.dtype)

    return pl.pallas_call(
        body, name=name, grid=(nb,), in_specs=_gdn_in_specs(n_heads, c, lambda j: j),
        out_specs=[pl.BlockSpec((c, hw), lambda j: (j, 0)),
                   pl.BlockSpec((None, n_heads, HEAD, HEAD), lambda j: (j, 0, 0, 0))],
        out_shape=[jax.ShapeDtypeStruct((t, hw), BF16), jax.ShapeDtypeStruct((nb, n_heads, HEAD, HEAD), F32)],
        scratch_shapes=[pltpu.VMEM((n_heads, HEAD, HEAD), F32)], compiler_params=_params(("arbitrary",)),
    )(qkv, qkv, qkv, proj, proj, alog_row, dtb_row, nw)


def _gdn_bwd(qkv, proj, alog_row, dtb_row, nw, saved, d_ocat, n_heads, name):
    t = qkv.shape[0]
    c = _tile(t, GDN_CHUNK, GDN_CHUNK)
    nb = t // c
    rev = lambda j: nb - 1 - j

    hw = n_heads * HEAD

    def body(q_ref, k_ref, v_ref, g_ref, ab_ref, al_ref, dt_ref, nw_ref, save_ref, do_ref,
             dqkv_ref, dg_ref, dab_ref, dal_ref, ddt_ref, dnw_ref, dst_ref):
        @pl.when(pl.program_id(0) == 0)
        def _():
            dst_ref[...] = jnp.zeros_like(dst_ref)
            dal_ref[...] = jnp.zeros_like(dal_ref)
            ddt_ref[...] = jnp.zeros_like(ddt_ref)
            dnw_ref[...] = jnp.zeros_like(dnw_ref)

        dab_acc = jnp.zeros((c, LANES), F32)
        row_acc = [jnp.zeros((1, LANES), F32)] * 3
        for hs in _head_groups(n_heads, GDN_BWD_GROUP):
            fn = functools.partial(_gdn_block, (_inv_unit_lower, _mm3), [_head_onehots(n_heads, h) for h in hs])
            _, vjp = jax.vjp(fn, [save_ref[h] for h in hs],
                             *[[r[:, _head_cols(h)] for h in hs] for r in (q_ref, k_ref, v_ref, g_ref)],
                             ab_ref[...], al_ref[...], dt_ref[...], nw_ref[...])
            dst, dq, dk, dv, dg, dab, dal, ddt, dnw = vjp(([dst_ref[h] for h in hs], [do_ref[:, _head_cols(h)] for h in hs]))
            for i, h in enumerate(hs):
                dst_ref[h] = dst[i]
                for part, val in enumerate((dq, dk, dv)):
                    dqkv_ref[:, part * hw + h * HEAD:part * hw + (h + 1) * HEAD] = val[i]
                dg_ref[:, _head_cols(h)] = dg[i].astype(dg_ref.dtype)
            dab_acc = dab_acc + dab
            row_acc = [acc + val for acc, val in zip(row_acc, (dal, ddt, dnw))]
        dab_ref[...] = dab_acc
        dal_ref[...] += row_acc[0]
        ddt_ref[...] += row_acc[1]
        dnw_ref[...] += row_acc[2]

    row = pl.BlockSpec((1, LANES), lambda j: (0, 0))
    return pl.pallas_call(
        body, name=name, grid=(nb,),
        in_specs=_gdn_in_specs(n_heads, c, rev) + [pl.BlockSpec((None, n_heads, HEAD, HEAD), lambda j: (rev(j), 0, 0, 0)),
                                                   pl.BlockSpec((c, hw), lambda j: (rev(j), 1))],
        out_specs=[pl.BlockSpec((c, 3 * hw), lambda j: (rev(j), 0)), pl.BlockSpec((c, hw), lambda j: (rev(j), 0)),
                   pl.BlockSpec((c, LANES), lambda j: (rev(j), 0)), row, row, row],
        out_shape=[jax.ShapeDtypeStruct((t, 3 * hw), F32), jax.ShapeDtypeStruct((t, hw), BF16),
                   jax.ShapeDtypeStruct((t, LANES), F32)] + [jax.ShapeDtypeStruct((1, LANES), F32)] * 3,
        scratch_shapes=[pltpu.VMEM((n_heads, HEAD, HEAD), F32)], compiler_params=_params(("arbitrary",)),
    )(qkv, qkv, qkv, proj, proj, alog_row, dtb_row, nw, saved, d_ocat)


def _pad_lanes(v, n):
    v = v.reshape(1, -1)
    return jnp.pad(v, ((0, 0), (0, n - v.shape[1])))


def _pack_rows(vecs):
    flat = jnp.concatenate([v.reshape(-1) for v in vecs])
    offs, o = [], 0
    for v in vecs:
        offs.append((o, v.size))
        o += v.size
    per_row = -(-o // (SUBLANES * LANES)) * LANES
    flat = jnp.pad(flat, (0, SUBLANES * per_row - o))
    return flat.reshape(SUBLANES, per_row), offs


def _unpack(gathered, offs):
    per_dev = gathered.reshape(N_DEV, -1)
    return [per_dev[:, o:o + n] for o, n in offs]


def _sum_devices(part):
    acc = part[0]
    for i in range(1, N_DEV):
        acc = acc + part[i]
    return acc


def kernel(x, c, w_ada, b_ada, pre_mix_norm, post_mix_norm, pre_ffn_norm, post_ffn_norm, w_in, hg_lb_logits, hg_norm, gdn_conv_w, gdn_a_log, gdn_dt_bias, gdn_norm, w_out, w_ff1, w_ff2, loss_target, m_w_ada, m_b_ada, m_pre_mix_norm, m_post_mix_norm, m_pre_ffn_norm, m_post_ffn_norm, m_w_in, m_hg_lb_logits, m_hg_norm, m_gdn_conv_w, m_gdn_a_log, m_gdn_dt_bias, m_gdn_norm, m_w_out, m_w_ff1, m_w_ff2, v_w_ada, v_b_ada, v_pre_mix_norm, v_post_mix_norm, v_pre_ffn_norm, v_post_ffn_norm, v_w_in, v_hg_lb_logits, v_hg_norm, v_gdn_conv_w, v_gdn_a_log, v_gdn_dt_bias, v_gdn_norm, v_w_out, v_w_ff1, v_w_ff2):
    assert x.shape[0] == 1 and w_ada.shape[0] == 1 and hg_lb_logits.shape[0] == 2
    t, d = x.shape[1], x.shape[2]
    n_heads = (d // 2) // HEAD
    hw = n_heads * HEAD
    in_cols = 8 * hw + 2 * n_heads
    np_cols = 8 * hw + 2 * LANES
    d_ff = w_ff1.shape[2] * N_CHIP
    na = w_ada.shape[2]
    ax, ay, ac = lax.axis_index("x"), lax.axis_index("y"), lax.axis_index("c")
    chip = 2 * ax + ay
    dev = 4 * ax + 2 * ay + ac

    x2d, tgt = x[0], loss_target[0]

    pack1, offs1 = _pack_rows([c[0], gdn_conv_w[0]])
    c_all, convw_all = _unpack(_gather8(pack1, "gather_cond"), offs1)
    conv_sh = gdn_conv_w.shape[2]
    conv_w = jnp.concatenate([convw_all[2 * j].reshape(CONV_K, conv_sh) for j in range(N_CHIP)], axis=1)

    b_s = lax.dynamic_slice(b_ada, (0, chip * na), (1, na))
    mod_part = _mod_part(c_all, w_ada[0], b_s, "mod_part")
    pack2, offs2 = _pack_rows([mod_part])
    (mod_parts,) = _unpack(_gather8(pack2, "gather_mod"), offs2)
    mod_all = jnp.concatenate([mod_parts[2 * j].reshape(N_DEV, na) for j in range(N_CHIP)], axis=1)
    mod = lax.dynamic_slice(mod_all, (dev, 0), (1, N_MOD * d))
    sh_m, sc_m, gt_m, sh_f, sc_f, gt_f = [mod[:, i * d:(i + 1) * d] for i in range(N_MOD)]

    (g_in,) = _gather_weights([jnp.transpose(w_in[0]).astype(BF16)], "gather_w_in")
    late, g_in = lax.optimization_barrier(([w_out[0].astype(BF16), w_ff1[0].astype(BF16), w_ff2[0].astype(BF16)], g_in))
    g_out, g_ff1, g_ff2 = _sequencer_gather_weights(late, "gather_weights_late", 1)
    w_in_f = jnp.pad(g_in.reshape(in_cols, d), ((0, np_cols - in_cols), (0, 0)))
    w_out_f = g_out.reshape(d, d)
    w_ff2_f = g_ff2.reshape(d_ff, d)

    h1 = _norm_mod(x2d, pre_mix_norm, sc_m, sh_m, "norm_mod_mix")
    proj = _matmul(h1, w_in_f, "nt", F32, "mm_in", tn=768)
    l0, l1 = hg_lb_logits[0].reshape(n_heads, 1, HEAD), hg_lb_logits[1].reshape(n_heads, 1, HEAD)
    o_hg, hg_saved = _hgrn2_fwd(proj, l0, l1, hg_norm, n_heads, "hgrn2_fwd")
    qkv = _conv_fwd(proj, conv_w, 4 * n_heads, "conv_fwd")
    alog_row, dtb_row = _pad_lanes(gdn_a_log, LANES), _pad_lanes(gdn_dt_bias, LANES)
    o_gdn, gdn_saved = _gdn_fwd(qkv, proj, alog_row, dtb_row, gdn_norm, n_heads, "gdn_fwd")
    o_cat = jnp.concatenate([o_hg, o_gdn], axis=1)
    y1 = _matmul(o_cat, w_out_f, "nn", F32, "mm_out")
    x_mid = _resid(x2d, y1, post_mix_norm, gt_m, "resid_mix")

    h2 = _norm_mod(x_mid, pre_ffn_norm, sc_f, sh_f, "norm_mod_ffn")
    relu_a1, r1 = _matmul(h2, g_ff1, "nn", BF16, "mm_ff1", relu2=True, b_split=True)
    y2 = _matmul(r1, w_ff2_f, "nn", F32, "mm_ff2")
    d_out, loss_row = _loss_head(x_mid, y2, post_ffn_norm, gt_f, tgt, "loss_head")

    in_sh = in_cols // N_CHIP
    ff_sh = d_ff // N_CHIP
    my_half = jnp.reshape(ac, (1,)).astype(jnp.int32)

    def start_reduce(by_chip, tag, collective_id):
        to_sib = [_row_half_to_bf16(a, 1 - my_half, None, f"sibling_half_{tag}{i}") for i, a in enumerate(by_chip)]
        from_sib = _sibling_exchange(to_sib, f"sibling_partials_{tag}")
        chip_part = [_row_half_to_bf16(a, my_half, s, f"add_halves_{tag}{i}") for i, (a, s) in enumerate(zip(by_chip, from_sib))]
        return _sequencer_chip_exchange(chip_part, f"scatter_grads_{tag}", collective_id)

    dy2, d_gt_f, d_post_ffn = _resid_bwd(d_out, y2, post_ffn_norm, gt_f, "resid_ffn_bwd")
    gw_ff2 = _matmul(r1, dy2, "tn", BF16, "mm_ff2_dw")
    gw_ff2, dy2 = lax.optimization_barrier((gw_ff2, dy2))
    da1 = _matmul(dy2, w_ff2_f, "nt", BF16, "mm_ff2_dx", times=relu_a1)
    gw_ff1 = _matmul(h2, da1, "tn", BF16, "mm_ff1_dw", out_split=True)
    gw_ff1, da1 = lax.optimization_barrier((gw_ff1, da1))
    recv_ff2, recv_ff1 = _sequencer_chip_exchange([gw_ff2.reshape(N_CHIP, ff_sh, d), gw_ff1], "scatter_grads_ff", 2)
    dh2 = _matmul(da1, g_ff1, "nt", BF16, "mm_ff1_dx", b_split=True)
    d_mid, d_pre_ffn, d_sc_f, d_sh_f = _norm_mod_bwd(x_mid, pre_ffn_norm, sc_f, dh2, d_out, "norm_mod_ffn_bwd")

    dy1, d_gt_m, d_post_mix = _resid_bwd(d_mid, y1, post_mix_norm, gt_m, "resid_mix_bwd")
    gw_out = _matmul(o_cat, dy1, "tn", BF16, "mm_out_dw")
    gw_out, dy1 = lax.optimization_barrier((gw_out, dy1))
    (recv_out,) = _sequencer_chip_exchange([gw_out.reshape(N_CHIP, d // N_CHIP, d)], "scatter_grads_out", 3)
    d_ocat = _matmul(dy1, w_out_f, "nt", F32, "mm_out_dx")
    dp_hg, dl0, dl1, d_hg_norm = _hgrn2_bwd(proj, l0, l1, hg_norm, hg_saved, d_ocat, n_heads, "hgrn2_bwd")
    dqkv, dg_g, dab, d_alog, d_dtb, d_gdn_norm = _gdn_bwd(
        qkv, proj, alog_row, dtb_row, gdn_norm, gdn_saved, d_ocat, n_heads, "gdn_bwd")
    du, d_conv_w = _conv_bwd(proj, conv_w, dqkv, 4 * n_heads, "conv_bwd")
    dproj = jnp.concatenate([dp_hg, du, dg_g, dab.astype(BF16), jnp.zeros((t, LANES), BF16)], axis=1)
    gw_in = _matmul(dproj, h1, "tn", F32, "mm_in_dw", tm=768)
    (recv_in,) = start_reduce([gw_in[:in_cols].reshape(N_CHIP, in_sh, d)], "in", 4)
    dh1 = _matmul(dproj, w_in_f, "nn", BF16, "mm_in_dx", tk=2816)
    grad_x, d_pre_mix, d_sc_m, d_sh_m = _norm_mod_bwd(x2d, pre_mix_norm, sc_m, dh1, d_mid, "norm_mod_mix_bwd")

    d_mod = jnp.concatenate([d_sh_m, d_sc_m, d_gt_m, d_sh_f, d_sc_f, d_gt_f], axis=1)
    d_lb_logits = jnp.stack([dl0.reshape(n_heads, HEAD), dl1.reshape(n_heads, HEAD)])
    pack3, offs3 = _pack_rows([loss_row[0, :1], d_pre_mix, d_post_mix, d_pre_ffn, d_post_ffn, d_lb_logits, d_hg_norm,
                               d_conv_w, d_alog[0, :n_heads], d_dtb[0, :n_heads], d_gdn_norm, d_mod])
    parts = _unpack(_gather8(pack3, "gather_vec_grads"), offs3)
    sums = [_sum_devices(p) for p in parts[:-1]]
    loss = sums[0][0]
    dmod_all = parts[-1]
    g_b_ada = _sum_devices(dmod_all).reshape(1, N_MOD * d)
    g_conv_full = sums[7].reshape(CONV_K, N_CHIP * conv_sh)
    g_conv = lax.dynamic_slice(g_conv_full, (0, chip * conv_sh), (CONV_K, conv_sh))
    gw_ada = _wada_grad(c_all, lax.dynamic_slice(dmod_all, (0, chip * na), (N_DEV, na)), "wada_grad")

    sum_ff2 = _sum_chips(recv_ff2, "sum_chips_ff2")
    recv_ff1, sum_ff2 = lax.optimization_barrier((recv_ff1, sum_ff2))
    sum_ff1 = _sum_chips(recv_ff1, "sum_chips_ff1")
    recv_out, sum_ff1 = lax.optimization_barrier((recv_out, sum_ff1))
    sum_out = _sum_chips(recv_out, "sum_chips_out")
    recv_in, sum_out = lax.optimization_barrier((recv_in, sum_out))
    mine = [_sum_chips(recv_in, "sum_chips_in"), sum_out, sum_ff1, sum_ff2]
    theirs = _sibling_exchange(mine, "sibling_grads")

    big = {}
    for i, (nm, w_, m_, v_) in enumerate([("w_in", w_in, m_w_in, v_w_in), ("w_out", w_out, m_w_out, v_w_out),
                                          ("w_ff1", w_ff1, m_w_ff1, v_w_ff1), ("w_ff2", w_ff2, m_w_ff2, v_w_ff2)]):
        if nm == "w_in":
            res_t = _adamw(jnp.transpose(w_[0]), [mine[i], theirs[i]], jnp.transpose(m_[0]), jnp.transpose(v_[0]),
                           f"adamw_{nm}", by_core=True)
            big[nm] = [jnp.transpose(o)[None] for o in res_t]
        else:
            big[nm] = [o[None] for o in _adamw(w_[0], [mine[i], theirs[i]], m_[0], v_[0], f"adamw_{nm}")]
    big["w_ada"] = [o[None] for o in _adamw(w_ada[0], [gw_ada], m_w_ada[0], v_w_ada[0], "adamw_w_ada")]

    small_names = ["b_ada", "pre_mix_norm", "post_mix_norm", "pre_ffn_norm", "post_ffn_norm", "hg_lb_logits", "hg_norm",
                   "gdn_conv_w", "gdn_a_log", "gdn_dt_bias", "gdn_norm"]
    small_w = [b_ada, pre_mix_norm, post_mix_norm, pre_ffn_norm, post_ffn_norm, hg_lb_logits, hg_norm, gdn_conv_w,
               gdn_a_log, gdn_dt_bias, gdn_norm]
    small_m = [m_b_ada, m_pre_mix_norm, m_post_mix_norm, m_pre_ffn_norm, m_post_ffn_norm, m_hg_lb_logits, m_hg_norm,
               m_gdn_conv_w, m_gdn_a_log, m_gdn_dt_bias, m_gdn_norm]
    small_v = [v_b_ada, v_pre_mix_norm, v_post_mix_norm, v_pre_ffn_norm, v_post_ffn_norm, v_hg_lb_logits, v_hg_norm,
               v_gdn_conv_w, v_gdn_a_log, v_gdn_dt_bias, v_gdn_norm]
    small_g = [g_b_ada, sums[1], sums[2], sums[3], sums[4], sums[5], sums[6], g_conv, sums[8], sums[9], sums[10]]
    pw, offs_s = _pack_rows(small_w)
    pg, _ = _pack_rows(small_g)
    pm, _ = _pack_rows(small_m)
    pv, _ = _pack_rows(small_v)
    packed = _adamw(pw, [pg], pm, pv, "adamw_vectors")
    small = {}
    for nm, w_, (o, n) in zip(small_names, small_w, offs_s):
        small[nm] = [p.reshape(-1)[o:o + n].reshape(w_.shape) for p in packed]

    order = ["w_ada", "b_ada", "pre_mix_norm", "post_mix_norm", "pre_ffn_norm", "post_ffn_norm", "w_in", "hg_lb_logits",
             "hg_norm", "gdn_conv_w", "gdn_a_log", "gdn_dt_bias", "gdn_norm", "w_out", "w_ff1", "w_ff2"]
    res = {**big, **small}
    outs = [loss, grad_x[None]]
    for k in range(4):
        outs += [res[nm][k] for nm in order]
    return tuple(outs)
```

```python
import functools
import math

import jax
import jax.numpy as jnp
from jax import lax
from jax.experimental import pallas as pl
from jax.experimental.pallas import tpu as pltpu
from jax.experimental.pallas import tpu_sc as plsc

F32 = jnp.float32
BF16 = jnp.bfloat16
HI = lax.Precision.HIGHEST
MESH = pl.DeviceIdType.MESH

LANES = 128
SUBLANES = 8
VMEM_LIMIT = 48 * 1024 * 1024
EPS = 1e-6
HEAD = 128
CONV_K = 4
GDN_CHUNK = 64
GDN_INV_BLOCK = 16
HG_SUB = 16
HG_BLOCK = 128
HG_FWD_GROUP = 8
HG_BWD_GROUP = 4
GDN_FWD_GROUP = 8
GDN_BWD_GROUP = 8
N_MOD = 6
N_DEV = 8
N_CHIP = 4

ADAM_LR = 0.001
ADAM_B1 = 0.9
ADAM_B2 = 0.999
ADAM_EPS = 1e-08
ADAM_WD = 0.01
ADAM_STEP = 10

NT_DIMS = (((1,), (1,)), ((), ()))
TN_DIMS = (((0,), (0,)), ((), ()))


def _tile(dim, target, align):
    if dim <= target:
        return dim
    best = dim
    t = align
    while t <= target:
        if dim % t == 0:
            best = t
        t += align
    return best


def _elementwise_tiles(r, c):
    tc = _tile(c, 1024, LANES)
    tr = _tile(r, max(16, (256 * 1024) // tc // 16 * 16), 16)
    if tr == r and r * tc > 512 * 1024:
        tc = _tile(c, max(LANES, (256 * 1024) // r // LANES * LANES), LANES)
    return tr, tc


def _params(sem):
    return pltpu.CompilerParams(dimension_semantics=sem, vmem_limit_bytes=VMEM_LIMIT)


def _silu(x):
    return x * jax.nn.sigmoid(x)


def _softplus(x):
    pos = x > 0
    return jnp.where(pos, x, 0.0) + jnp.log(1.0 + jnp.exp(jnp.where(pos, -x, x)))


def _rms_scale(x):
    return lax.rsqrt(jnp.mean(x * x, axis=-1, keepdims=True) + EPS)


def _gather8(x_shard, name):
    m_per, n = x_shard.shape
    assert m_per % SUBLANES == 0 and n % LANES == 0

    def body(x_ref, out_ref, send_sems, recv_sems, local_sem):
        x, y, c = lax.axis_index("x"), lax.axis_index("y"), lax.axis_index("c")
        me, sibling = (x, y, c), (x, y, 1 - c)
        chips = [(1 - x, y), (x, 1 - y), (1 - x, 1 - y)]

        def rows(px, py, pc):
            return out_ref.at[pl.ds((4 * px + 2 * py + pc) * m_per, m_per), :]

        def copy(k, block, to, src=None):
            return pltpu.make_async_remote_copy(
                src_ref=rows(*block) if src is None else src, dst_ref=rows(*block),
                send_sem=send_sems.at[k], recv_sem=recv_sems.at[k], device_id=to, device_id_type=MESH)

        mine = pltpu.make_async_copy(x_ref, rows(*me), local_sem)
        mine.start()
        first = [copy(0, me, sibling, src=x_ref)]
        first += [copy(1 + j, me, (*chip, c), src=x_ref) for j, chip in enumerate(chips)]
        for cp in first:
            cp.start()
        passed = [copy(4 + j, (*chip, c), sibling) for j, chip in enumerate(chips)]
        for j, chip in enumerate(chips):
            copy(1 + j, (*chip, c), me).wait_recv()
            passed[j].start()
        copy(0, sibling, me).wait_recv()
        for j, chip in enumerate(chips):
            copy(4 + j, (*chip, 1 - c), me).wait_recv()
        for cp in first + passed:
            cp.wait_send()
        mine.wait()

    return pl.pallas_call(
        body, name=name,
        out_shape=jax.ShapeDtypeStruct((N_DEV * m_per, n), x_shard.dtype),
        in_specs=[pl.BlockSpec(memory_space=pltpu.VMEM)],
        out_specs=pl.BlockSpec(memory_space=pltpu.VMEM),
        scratch_shapes=[pltpu.SemaphoreType.DMA((7,)), pltpu.SemaphoreType.DMA((7,)), pltpu.SemaphoreType.DMA],
        compiler_params=pltpu.CompilerParams(vmem_limit_bytes=VMEM_LIMIT),
    )(x_shard)


def _gather_weights(arrs, name, sequencer_id=None, after=None):
    n = len(arrs)
    out_shapes = [jax.ShapeDtypeStruct((N_CHIP,) + a.shape, a.dtype) for a in arrs]

    def body(*refs):
        ins, outs = refs[:n], refs[n:2 * n]
        ici_send, ici_recv, d2d_send, d2d_recv = refs[2 * n:]
        if sequencer_id is not None:
            chips, sib = _chip_peers()
            _handshake(chips + [sib])
        x, y, c = lax.axis_index("x"), lax.axis_index("y"), lax.axis_index("c")
        me = 2 * x + y
        peers = [(1 - x, y), (x, 1 - y), (1 - x, 1 - y)]

        def half(a, cc):
            r = arrs[a].shape[0]
            cut = r // 32 * 16
            return pl.ds(0, cut) if cc == 0 else pl.ds(cut, r - cut)

        def exchange(mine):
            sibling = (x, y, 1 - mine)
            sent = []
            for a in range(n):
                for k, (px, py) in enumerate(peers):
                    cp = pltpu.make_async_remote_copy(
                        src_ref=ins[a].at[half(a, mine)], dst_ref=outs[a].at[me, half(a, mine)],
                        send_sem=ici_send.at[3 * a + k], recv_sem=ici_recv.at[3 * a + k],
                        device_id=(px, py, mine), device_id_type=MESH)
                    cp.start()
                    sent.append(cp)
            for a in range(n):
                for k, (px, py) in enumerate(peers):
                    landed = outs[a].at[2 * px + py, half(a, mine)]
                    pltpu.make_async_remote_copy(
                        src_ref=landed, dst_ref=landed, send_sem=ici_send.at[3 * a + k], recv_sem=ici_recv.at[3 * a + k],
                        device_id=(px, py, mine), device_id_type=MESH).wait_recv()
                    fwd = pltpu.make_async_remote_copy(
                        src_ref=landed, dst_ref=landed, send_sem=d2d_send.at[3 * a + k], recv_sem=d2d_recv.at[3 * a + k],
                        device_id=sibling, device_id_type=MESH)
                    fwd.start()
                    sent.append(fwd)
            for a in range(n):
                for k, (px, py) in enumerate(peers):
                    passed = outs[a].at[2 * px + py, half(a, 1 - mine)]
                    pltpu.make_async_remote_copy(
                        src_ref=passed, dst_ref=passed, send_sem=d2d_send.at[3 * a + k], recv_sem=d2d_recv.at[3 * a + k],
                        device_id=sibling, device_id_type=MESH).wait_recv()
            for cp in sent:
                cp.wait_send()

        for core in (0, 1):
            pl.when(c == core)(functools.partial(exchange, core))

    sems = [pltpu.SemaphoreType.DMA((3 * n,))] * 4
    if sequencer_id is None:
        hbm = pl.BlockSpec(memory_space=pltpu.HBM)
        gathered = pl.pallas_call(body, name=name, out_shape=out_shapes, in_specs=[hbm] * n, out_specs=[hbm] * n,
                                  scratch_shapes=sems)(*arrs)
    else:
        gathered = pl.kernel(body, out_type=out_shapes, mesh=plsc.ScalarSubcoreMesh(axis_name="sequencer", num_cores=1),
                             name=name, scratch_types=sems,
                             compiler_params=pltpu.CompilerParams(collective_id=sequencer_id))(*arrs)
    if after is not None:
        gathered, after = lax.optimization_barrier((gathered, after))
    chip = 2 * lax.axis_index("x") + lax.axis_index("y")
    filled = [lax.dynamic_update_slice(g, a[None], (chip, 0, 0)) for g, a in zip(gathered, arrs)]
    return filled if after is None else (filled, after)


def _chip_peers():
    x, y, c = lax.axis_index("x"), lax.axis_index("y"), lax.axis_index("c")
    return [(1 - x, y, c), (x, 1 - y, c), (1 - x, 1 - y, c)], (x, y, 1 - c)


def _handshake(peers):
    barrier = pltpu.get_barrier_semaphore()
    for peer in peers:
        pl.semaphore_signal(barrier, inc=1, device_id=peer, device_id_type=MESH)
    pl.semaphore_wait(barrier, len(peers))


def _sequencer_chip_exchange(arrs, name, collective_id):
    n = len(arrs)
    out_types = [jax.ShapeDtypeStruct(a.shape, a.dtype) for a in arrs]

    def body(*refs):
        ins, outs = refs[:n], refs[n:2 * n]
        send_sems, recv_sems = refs[2 * n:]
        chips, _ = _chip_peers()
        _handshake(chips)
        me = 2 * lax.axis_index("x") + lax.axis_index("y")
        sent = []
        for a in range(n):
            for k, peer in enumerate(chips):
                cp = pltpu.make_async_remote_copy(
                    src_ref=ins[a].at[2 * peer[0] + peer[1]], dst_ref=outs[a].at[me],
                    send_sem=send_sems.at[3 * a + k], recv_sem=recv_sems.at[3 * a + k], device_id=peer, device_id_type=MESH)
                cp.start()
                sent.append(cp)
        for a in range(n):
            for k, peer in enumerate(chips):
                landed = outs[a].at[2 * peer[0] + peer[1]]
                pltpu.make_async_remote_copy(
                    src_ref=landed, dst_ref=landed, send_sem=send_sems.at[3 * a + k], recv_sem=recv_sems.at[3 * a + k],
                    device_id=peer, device_id_type=MESH).wait_recv()
        for cp in sent:
            cp.wait_send()

    received = pl.kernel(
        body, out_type=out_types, mesh=plsc.ScalarSubcoreMesh(axis_name="sequencer", num_cores=1), name=name,
        scratch_types=[pltpu.SemaphoreType.DMA((3 * n,))] * 2,
        compiler_params=pltpu.CompilerParams(collective_id=collective_id),
    )(*arrs)
    chip = 2 * lax.axis_index("x") + lax.axis_index("y")
    return [lax.dynamic_update_slice(r, lax.dynamic_slice(a, (chip, 0, 0), (1,) + a.shape[1:]), (chip, 0, 0))
            for r, a in zip(received, arrs)]


def _sibling_exchange(arrs, name):
    n = len(arrs)

    def body(*refs):
        ins, outs = refs[:n], refs[n:2 * n]
        send_sems, recv_sems = refs[2 * n:]
        sibling = (lax.axis_index("x"), lax.axis_index("y"), 1 - lax.axis_index("c"))
        cps = []
        for a in range(n):
            cp = pltpu.make_async_remote_copy(src_ref=ins[a], dst_ref=outs[a], send_sem=send_sems.at[a],
                                              recv_sem=recv_sems.at[a], device_id=sibling, device_id_type=MESH)
            cp.start()
            cps.append(cp)
        for cp in cps:
            cp.wait_recv()
        for cp in cps:
            cp.wait_send()

    hbm = pl.BlockSpec(memory_space=pltpu.HBM)
    return pl.pallas_call(
        body, name=name, out_shape=[jax.ShapeDtypeStruct(a.shape, a.dtype) for a in arrs],
        in_specs=[hbm] * n, out_specs=[hbm] * n,
        scratch_shapes=[pltpu.SemaphoreType.DMA((n,)), pltpu.SemaphoreType.DMA((n,))],
    )(*arrs)


def _matmul(a, b, mode, out_dtype, name, tm=1024, tn=1024, tk=2048, relu2=False, times=None, b_split=False,
            out_split=False):
    b_shape = (b.shape[1], b.shape[2] * N_CHIP) if b_split else b.shape
    if mode == "nn":
        (m, k), (k2, n) = a.shape, b_shape
    elif mode == "nt":
        (m, k), (n, k2) = a.shape, b_shape
    else:
        (k, m), (k2, n) = a.shape, b_shape
    assert k == k2, (a.shape, b.shape, mode)
    n_cut = n // N_CHIP if (out_split or (b_split and mode != "nt")) else n
    k_cut = k // N_CHIP if (b_split and mode == "nt") else k
    tm, tn, tk = _tile(m, tm, LANES), _tile(n_cut, tn, LANES), _tile(k_cut, tk, LANES)
    assert n_cut % tn == 0 and k_cut % tk == 0 and m % tm == 0, (name, m, n, k, tm, tn, tk)
    nk = k // tk
    nbc, nkc = n_cut // tn, k_cut // tk
    n_in = 2 if times is None else 3
    n_out = 2 if relu2 else 1

    def product(a_ref, b_ref):
        if mode == "nn":
            return jnp.dot(a_ref[...], b_ref[...], preferred_element_type=F32)
        return lax.dot_general(a_ref[...], b_ref[...], NT_DIMS if mode == "nt" else TN_DIMS, preferred_element_type=F32)

    def finish(p, refs, o_refs):
        if relu2:
            p = jnp.maximum(p, 0.0)
            o_refs[0][...] = p.astype(o_refs[0].dtype)
            o_refs[1][...] = (p * p).astype(o_refs[1].dtype)
        elif times is not None:
            o_refs[0][...] = (2.0 * refs[2][...].astype(F32) * p).astype(o_refs[0].dtype)
        else:
            o_refs[0][...] = p.astype(o_refs[0].dtype)

    def body(*refs):
        o_refs = refs[n_in:n_in + n_out]
        if nk == 1:
            finish(product(refs[0], refs[1]), refs, o_refs)
            return
        acc_ref = refs[n_in + n_out]
        kk = pl.program_id(2)

        @pl.when(kk == 0)
        def _():
            acc_ref[...] = product(refs[0], refs[1])

        @pl.when((kk > 0) & (kk < nk - 1))
        def _():
            acc_ref[...] += product(refs[0], refs[1])

        @pl.when(kk == nk - 1)
        def _():
            finish(acc_ref[...] + product(refs[0], refs[1]), refs, o_refs)

    if mode == "tn":
        a_spec = pl.BlockSpec((tk, tm), lambda i, j, kk: (kk, i))
    else:
        a_spec = pl.BlockSpec((tm, tk), lambda i, j, kk: (i, kk))
    if mode == "nt":
        b_spec = (pl.BlockSpec((None, tn, tk), lambda i, j, kk: (kk // nkc, j, kk % nkc)) if b_split
                  else pl.BlockSpec((tn, tk), lambda i, j, kk: (j, kk)))
    else:
        b_spec = (pl.BlockSpec((None, tk, tn), lambda i, j, kk: (j // nbc, kk, j % nbc)) if b_split
                  else pl.BlockSpec((tk, tn), lambda i, j, kk: (kk, j)))
    mn_spec = pl.BlockSpec((tm, tn), lambda i, j, kk: (i, j))
    if out_split:
        o_spec = pl.BlockSpec((None, tm, tn), lambda i, j, kk: (j // nbc, i, j % nbc))
        o_shape = jax.ShapeDtypeStruct((N_CHIP, m, n_cut), out_dtype)
    else:
        o_spec, o_shape = mn_spec, jax.ShapeDtypeStruct((m, n), out_dtype)
    out = pl.pallas_call(
        body, name=name, grid=(m // tm, n // tn, nk), in_specs=[a_spec, b_spec] + [mn_spec] * (n_in - 2),
        out_specs=[o_spec] * n_out, out_shape=[o_shape] * n_out,
        scratch_shapes=[] if nk == 1 else [pltpu.VMEM((tm, tn), F32)],
        compiler_params=_params(("parallel", "parallel", "arbitrary")),
    )(*((a, b) if times is None else (a, b, times)))
    return out if relu2 else out[0]


def _mod_part(c_all, w_s, b_s, name):
    d, na = w_s.shape
    tn = _tile(na, 512, LANES)

    def body(c_ref, w_ref, b_ref, o_ref):
        ca = _silu(c_ref[...]).astype(BF16)
        o_ref[...] = jnp.dot(ca, w_ref[...].astype(BF16), preferred_element_type=F32) + b_ref[...]

    return pl.pallas_call(
        body, name=name, grid=(na // tn,),
        in_specs=[pl.BlockSpec((N_DEV, d), lambda j: (0, 0)), pl.BlockSpec((d, tn), lambda j: (0, j)),
                  pl.BlockSpec((1, tn), lambda j: (0, j))],
        out_specs=pl.BlockSpec((N_DEV, tn), lambda j: (0, j)),
        out_shape=jax.ShapeDtypeStruct((N_DEV, na), F32), compiler_params=_params(("parallel",)),
    )(c_all, w_s, b_s)


def _wada_grad(c_all, dmod_s, name):
    d = c_all.shape[1]
    na = dmod_s.shape[1]
    td, tn = _tile(d, 512, LANES), _tile(na, 512, LANES)

    def body(c_ref, g_ref, o_ref):
        o_ref[...] = lax.dot_general(_silu(c_ref[...]), g_ref[...], TN_DIMS, precision=HI, preferred_element_type=F32)

    return pl.pallas_call(
        body, name=name, grid=(d // td, na // tn),
        in_specs=[pl.BlockSpec((N_DEV, td), lambda i, j: (0, i)), pl.BlockSpec((N_DEV, tn), lambda i, j: (0, j))],
        out_specs=pl.BlockSpec((td, tn), lambda i, j: (i, j)),
        out_shape=jax.ShapeDtypeStruct((d, na), F32), compiler_params=_params(("parallel", "parallel")),
    )(c_all, dmod_s)


def _row_specs(tb, d, n_full, n_vec):
    full = pl.BlockSpec((tb, d), lambda i: (i, 0))
    vec = pl.BlockSpec((1, d), lambda i: (0, 0))
    return [full] * n_full + [vec] * n_vec


def _norm_mod(x, w, sc, sh, name):
    t, d = x.shape
    tb = _tile(t, 256, SUBLANES)

    def body(x_ref, w_ref, sc_ref, sh_ref, o_ref):
        xv = x_ref[...]
        o_ref[...] = (xv * _rms_scale(xv) * w_ref[...] * (1.0 + sc_ref[...]) + sh_ref[...]).astype(o_ref.dtype)

    return pl.pallas_call(
        body, name=name, grid=(t // tb,), in_specs=_row_specs(tb, d, 1, 3),
        out_specs=pl.BlockSpec((tb, d), lambda i: (i, 0)), out_shape=jax.ShapeDtypeStruct((t, d), BF16),
        compiler_params=_params(("parallel",)),
    )(x, w, sc, sh)


def _norm_mod_bwd(x, w, sc, dh, dres, name):
    t, d = x.shape
    tb = _tile(t, 256, SUBLANES)

    def body(x_ref, w_ref, sc_ref, dh_ref, dres_ref, dx_ref, dw_ref, dsc_ref, dsh_ref):
        @pl.when(pl.program_id(0) == 0)
        def _():
            dw_ref[...] = jnp.zeros_like(dw_ref)
            dsc_ref[...] = jnp.zeros_like(dsc_ref)
            dsh_ref[...] = jnp.zeros_like(dsh_ref)

        xv = x_ref[...]
        r = _rms_scale(xv)
        xn = xv * r
        g = dh_ref[...].astype(F32)
        wv, one_sc = w_ref[...], 1.0 + sc_ref[...]
        gxn = g * xn
        dsh_ref[...] += jnp.sum(g, axis=0, keepdims=True)
        dsc_ref[...] += jnp.sum(gxn, axis=0, keepdims=True) * wv
        dw_ref[...] += jnp.sum(gxn, axis=0, keepdims=True) * one_sc
        dxn = g * (wv * one_sc)
        dx_ref[...] = dres_ref[...] + r * (dxn - xn * jnp.mean(dxn * xn, axis=-1, keepdims=True))

    vec_out = pl.BlockSpec((1, d), lambda i: (0, 0))
    return pl.pallas_call(
        body, name=name, grid=(t // tb,),
        in_specs=[pl.BlockSpec((tb, d), lambda i: (i, 0)), pl.BlockSpec((1, d), lambda i: (0, 0)),
                  pl.BlockSpec((1, d), lambda i: (0, 0)), pl.BlockSpec((tb, d), lambda i: (i, 0)),
                  pl.BlockSpec((tb, d), lambda i: (i, 0))],
        out_specs=[pl.BlockSpec((tb, d), lambda i: (i, 0)), vec_out, vec_out, vec_out],
        out_shape=[jax.ShapeDtypeStruct((t, d), F32)] + [jax.ShapeDtypeStruct((1, d), F32)] * 3,
        compiler_params=_params(("arbitrary",)),
    )(x, w, sc, dh, dres)


def _resid(x, y, w, gt, name):
    t, d = x.shape
    tb = _tile(t, 256, SUBLANES)

    def body(x_ref, y_ref, w_ref, gt_ref, o_ref):
        yv = y_ref[...]
        o_ref[...] = x_ref[...] + gt_ref[...] * (yv * _rms_scale(yv) * w_ref[...])

    return pl.pallas_call(
        body, name=name, grid=(t // tb,), in_specs=_row_specs(tb, d, 2, 2),
        out_specs=pl.BlockSpec((tb, d), lambda i: (i, 0)), out_shape=jax.ShapeDtypeStruct((t, d), F32),
        compiler_params=_params(("parallel",)),
    )(x, y, w, gt)


def _loss_head(x2, y2, w, gt, target, name):
    t, d = x2.shape
    tb = _tile(t, 256, SUBLANES)

    def body(x_ref, y_ref, tg_ref, w_ref, gt_ref, do_ref, loss_ref):
        @pl.when(pl.program_id(0) == 0)
        def _():
            loss_ref[...] = jnp.zeros_like(loss_ref)

        yv = y_ref[...]
        out = x_ref[...] + gt_ref[...] * (yv * _rms_scale(yv) * w_ref[...])
        err = out - tg_ref[...]
        do_ref[...] = err * (1.0 / d)
        per_tok = jnp.mean(err * err, axis=-1, keepdims=True)
        loss_ref[...] += 0.5 * jnp.sum(per_tok, axis=0, keepdims=True)

    return pl.pallas_call(
        body, name=name, grid=(t // tb,), in_specs=_row_specs(tb, d, 3, 2),
        out_specs=[pl.BlockSpec((tb, d), lambda i: (i, 0)), pl.BlockSpec((1, LANES), lambda i: (0, 0))],
        out_shape=[jax.ShapeDtypeStruct((t, d), F32), jax.ShapeDtypeStruct((1, LANES), F32)],
        compiler_params=_params(("arbitrary",)),
    )(x2, y2, target, w, gt)


def _resid_bwd(dout, y, w, gt, name):
    t, d = y.shape
    tb = _tile(t, 256, SUBLANES)

    def body(do_ref, y_ref, w_ref, gt_ref, dy_ref, dgt_ref, dw_ref):
        @pl.when(pl.program_id(0) == 0)
        def _():
            dgt_ref[...] = jnp.zeros_like(dgt_ref)
            dw_ref[...] = jnp.zeros_like(dw_ref)

        yv, g = y_ref[...], do_ref[...]
        r = _rms_scale(yv)
        yn = yv * r
        wv, gtv = w_ref[...], gt_ref[...]
        gyn = jnp.sum(g * yn, axis=0, keepdims=True)
        dgt_ref[...] += gyn * wv
        dw_ref[...] += gyn * gtv
        dyn = g * (gtv * wv)
        dy_ref[...] = (r * (dyn - yn * jnp.mean(dyn * yn, axis=-1, keepdims=True))).astype(dy_ref.dtype)

    vec_out = pl.BlockSpec((1, d), lambda i: (0, 0))
    return pl.pallas_call(
        body, name=name, grid=(t // tb,), in_specs=_row_specs(tb, d, 2, 2),
        out_specs=[pl.BlockSpec((tb, d), lambda i: (i, 0)), vec_out, vec_out],
        out_shape=[jax.ShapeDtypeStruct((t, d), BF16)] + [jax.ShapeDtypeStruct((1, d), F32)] * 2,
        compiler_params=_params(("arbitrary",)),
    )(dout, y, w, gt)


def _row_half_to_bf16(full, which, sib, name):
    n, r, c = full.shape
    by_rows = r % 32 == 0
    r, c = (r // 2, c) if by_rows else (r, c // 2)
    tr, tc = _elementwise_tiles(r, c)
    nbh = (r // tr) if by_rows else (c // tc)

    def body(which_ref, a_ref, *rest):
        if sib is None:
            rest[0][...] = a_ref[...].astype(BF16)
        else:
            rest[1][...] = (a_ref[...] + rest[0][...].astype(F32)).astype(BF16)

    if by_rows:
        half_spec = pl.BlockSpec((1, tr, tc), lambda j, i, k, which_ref: (j, which_ref[0] * nbh + i, k))
    else:
        half_spec = pl.BlockSpec((1, tr, tc), lambda j, i, k, which_ref: (j, i, which_ref[0] * nbh + k))
    spec = pl.BlockSpec((1, tr, tc), lambda j, i, k, which_ref: (j, i, k))
    grid_spec = pltpu.PrefetchScalarGridSpec(
        num_scalar_prefetch=1, grid=(n, r // tr, c // tc), in_specs=[half_spec] + ([] if sib is None else [spec]), out_specs=spec)
    return pl.pallas_call(
        body, name=name, grid_spec=grid_spec, out_shape=jax.ShapeDtypeStruct((n, r, c), BF16),
        compiler_params=_params(("parallel", "parallel", "parallel")),
    )(which, full, *([] if sib is None else [sib]))


def _sum_chips(recv, name):
    _, r, c = recv.shape
    tr, tc = _elementwise_tiles(r, c)

    def body(x_ref, o_ref):
        acc = x_ref[0].astype(F32)
        for j in range(1, N_CHIP):
            acc = acc + x_ref[j].astype(F32)
        o_ref[...] = acc

    return pl.pallas_call(
        body, name=name, grid=(r // tr, c // tc), in_specs=[pl.BlockSpec((N_CHIP, tr, tc), lambda i, j: (0, i, j))],
        out_specs=pl.BlockSpec((tr, tc), lambda i, j: (i, j)), out_shape=jax.ShapeDtypeStruct((r, c), F32),
        compiler_params=_params(("parallel", "parallel")),
    )(recv)


def _adamw(w, g_parts, m, v, name, by_core=False):
    r, c = w.shape
    by_rows = r % 32 == 0
    if by_core:
        tr, tc = _elementwise_tiles(*((r // 2, c) if by_rows else (r, c // 2)))
        nbh = (r // 2) // tr if by_rows else (c // 2) // tc
    else:
        tr, tc = _elementwise_tiles(r, c)
    n_g = len(g_parts)
    c1 = 1.0 / (1.0 - ADAM_B1 ** ADAM_STEP)
    c2 = 1.0 / (1.0 - ADAM_B2 ** ADAM_STEP)

    def body(*refs):
        w_ref, g_refs, m_ref, v_ref = refs[0], refs[1:1 + n_g], refs[1 + n_g], refs[2 + n_g]
        g_out, d_out, m_out, v_out = refs[3 + n_g:]
        if by_core:
            in_my_half = (pl.program_id(0 if by_rows else 1) // nbh) == lax.axis_index("c")
            g = jnp.where(in_my_half, g_refs[0][...], g_refs[1][...])
        else:
            g = g_refs[0][...]
            for extra in g_refs[1:]:
                g = g + extra[...]
        mn = ADAM_B1 * m_ref[...] + (1.0 - ADAM_B1) * g
        vn = ADAM_B2 * v_ref[...] + (1.0 - ADAM_B2) * (g * g)
        g_out[...] = g
        m_out[...] = mn
        v_out[...] = vn
        d_out[...] = -ADAM_LR * ((mn * c1) / (jnp.sqrt(vn * c2) + ADAM_EPS) + ADAM_WD * w_ref[...])

    spec = pl.BlockSpec((tr, tc), lambda i, j: (i, j))
    if by_core:
        g_spec = pl.BlockSpec((tr, tc), (lambda i, j: (i % nbh, j)) if by_rows else (lambda i, j: (i, j % nbh)))
    else:
        g_spec = spec
    return pl.pallas_call(
        body, name=name, grid=(r // tr, c // tc), in_specs=[spec] + [g_spec] * n_g + [spec] * 2, out_specs=[spec] * 4,
        out_shape=[jax.ShapeDtypeStruct((r, c), F32)] * 4, compiler_params=_params(("parallel", "parallel")),
    )(w, *g_parts, m, v)


def _conv_taps(u, t):
    rows = lax.broadcasted_iota(jnp.int32, u.shape, 0)
    return [u] + [jnp.where(rows >= dd, pltpu.roll(u, dd, 0), 0.0) for dd in range(1, CONV_K)]


def _conv_fwd(proj, conv_w, col0, name):
    t = proj.shape[0]
    ch = conv_w.shape[1]

    def body(u_ref, w_ref, o_ref):
        taps = _conv_taps(u_ref[...], t)
        wv = w_ref[...]
        y = taps[0] * wv[CONV_K - 1:CONV_K]
        for dd in range(1, CONV_K):
            y = y + taps[dd] * wv[CONV_K - 1 - dd:CONV_K - dd]
        o_ref[...] = _silu(y)

    return pl.pallas_call(
        body, name=name, grid=(ch // LANES,),
        in_specs=[pl.BlockSpec((t, LANES), lambda j: (0, col0 + j)), pl.BlockSpec((CONV_K, LANES), lambda j: (0, j))],
        out_specs=pl.BlockSpec((t, LANES), lambda j: (0, j)), out_shape=jax.ShapeDtypeStruct((t, ch), F32),
        compiler_params=_params(("parallel",)),
    )(proj, conv_w)


def _conv_bwd(proj, conv_w, ds, col0, name):
    t = proj.shape[0]
    ch = conv_w.shape[1]

    def body(u_ref, w_ref, ds_ref, du_ref, dw_ref):
        u = u_ref[...]
        taps = _conv_taps(u, t)
        wv = w_ref[...]
        y = taps[0] * wv[CONV_K - 1:CONV_K]
        for dd in range(1, CONV_K):
            y = y + taps[dd] * wv[CONV_K - 1 - dd:CONV_K - dd]
        sg = jax.nn.sigmoid(y)
        dy = ds_ref[...] * (sg * (1.0 + y * (1.0 - sg)))
        rows = lax.broadcasted_iota(jnp.int32, u.shape, 0)
        du = dy * wv[CONV_K - 1:CONV_K]
        for dd in range(1, CONV_K):
            ahead = jnp.where(rows < t - dd, pltpu.roll(dy, t - dd, 0), 0.0)
            du = du + ahead * wv[CONV_K - 1 - dd:CONV_K - dd]
        du_ref[...] = du.astype(du_ref.dtype)
        dws = [jnp.sum(dy * taps[CONV_K - 1 - j], axis=0, keepdims=True) for j in range(CONV_K)]
        dw_ref[...] = jnp.concatenate(dws, axis=0)

    return pl.pallas_call(
        body, name=name, grid=(ch // LANES,),
        in_specs=[pl.BlockSpec((t, LANES), lambda j: (0, col0 + j)), pl.BlockSpec((CONV_K, LANES), lambda j: (0, j)),
                  pl.BlockSpec((t, LANES), lambda j: (0, j))],
        out_specs=[pl.BlockSpec((t, LANES), lambda j: (0, j)), pl.BlockSpec((CONV_K, LANES), lambda j: (0, j))],
        out_shape=[jax.ShapeDtypeStruct((t, ch), BF16), jax.ShapeDtypeStruct((CONV_K, ch), F32)],
        compiler_params=_params(("parallel",)),
    )(proj, conv_w, ds)


def _hg_block(st, q, fl, vi, g, l0, l1, nw):
    hs = range(len(st))
    tb = q[0].shape[0]
    ln = HG_SUB
    lb = [jax.nn.sigmoid(l0[h] - l1[h]) for h in hs]
    rows = lax.broadcasted_iota(jnp.int32, (ln, HEAD), 0)
    tri = (lax.broadcasted_iota(jnp.int32, (ln, ln), 0) >= lax.broadcasted_iota(jnp.int32, (ln, ln), 1)).astype(F32)
    st = list(st)
    outs = [[] for _ in hs]
    for i in range(tb // ln):
        sl = slice(i * ln, (i + 1) * ln)
        qs, vs = [q[h][sl] for h in hs], [vi[h][sl] for h in hs]
        f = [lb[h] + (1.0 - lb[h]) * jax.nn.sigmoid(fl[h][sl]) for h in hs]
        k = [1.0 - f[h] for h in hs]
        b = [jnp.dot(tri, jnp.log(f[h]), precision=HI, preferred_element_type=F32) for h in hs]
        o = [lax.dot_general((qs[h] * jnp.exp(b[h])).astype(BF16), st[h].astype(BF16), NT_DIMS, preferred_element_type=F32)
             for h in hs]
        for s in range(ln):
            e = [jnp.exp(jnp.where(rows >= s, b[h] - b[h][s:s + 1], -1e30)) for h in hs]
            a = [jnp.sum(qs[h] * e[h] * k[h][s:s + 1], axis=-1, keepdims=True) for h in hs]
            o = [o[h] + a[h] * vs[h][s:s + 1] for h in hs]
        kt = [k[h] * jnp.exp(b[h][ln - 1:ln] - b[h]) for h in hs]
        upd = [lax.dot_general(vs[h].astype(BF16), kt[h].astype(BF16), TN_DIMS, preferred_element_type=F32) for h in hs]
        st = [st[h] * jnp.exp(b[h][ln - 1:ln]) + upd[h] for h in hs]
        for h in hs:
            outs[h].append(o[h])
    o = [jnp.concatenate(outs[h], axis=0) for h in hs]
    out = [o[h] * _rms_scale(o[h]) * nw * _silu(g[h]) for h in hs]
    return st, out


def _head_cols(h):
    return slice(h * HEAD, (h + 1) * HEAD)


def _head_groups(n_heads, group):
    g = min(group, n_heads)
    return [list(range(i, min(i + g, n_heads))) for i in range(0, n_heads, g)]


def _hg_in_specs(n_heads, tb, time_index):
    hw = n_heads * HEAD
    cols = [pl.BlockSpec((tb, hw), functools.partial(lambda part, j: (time_index(j), part), part)) for part in range(4)]
    head_rows = pl.BlockSpec((n_heads, 1, HEAD), lambda j: (0, 0, 0))
    return cols + [head_rows, head_rows, pl.BlockSpec((1, HEAD), lambda j: (0, 0))]


def _hgrn2_fwd(proj, l0, l1, nw, n_heads, name):
    t = proj.shape[0]
    hw = n_heads * HEAD
    tb = _tile(t, HG_BLOCK, HG_SUB)
    nb = t // tb

    def body(q_ref, f_ref, i_ref, g_ref, l0_ref, l1_ref, nw_ref, o_ref, save_ref, st_ref):
        @pl.when(pl.program_id(0) == 0)
        def _():
            st_ref[...] = jnp.zeros_like(st_ref)

        for hs in _head_groups(n_heads, HG_FWD_GROUP):
            st = [st_ref[h] for h in hs]
            for h, s in zip(hs, st):
                save_ref[h] = s
            st, out = _hg_block(st, *[[r[:, _head_cols(h)] for h in hs] for r in (q_ref, f_ref, i_ref, g_ref)],
                                [l0_ref[h] for h in hs], [l1_ref[h] for h in hs], nw_ref[...])
            for h, s, o in zip(hs, st, out):
                st_ref[h] = s
                o_ref[:, _head_cols(h)] = o.astype(o_ref.dtype)

    return pl.pallas_call(
        body, name=name, grid=(nb,), in_specs=_hg_in_specs(n_heads, tb, lambda j: j),
        out_specs=[pl.BlockSpec((tb, hw), lambda j: (j, 0)),
                   pl.BlockSpec((None, n_heads, HEAD, HEAD), lambda j: (j, 0, 0, 0))],
        out_shape=[jax.ShapeDtypeStruct((t, hw), BF16), jax.ShapeDtypeStruct((nb, n_heads, HEAD, HEAD), F32)],
        scratch_shapes=[pltpu.VMEM((n_heads, HEAD, HEAD), F32)], compiler_params=_params(("arbitrary",)),
    )(proj, proj, proj, proj, l0, l1, nw)


def _hgrn2_bwd(proj, l0, l1, nw, saved, d_ocat, n_heads, name):
    t = proj.shape[0]
    tb = _tile(t, HG_BLOCK, HG_SUB)
    nb = t // tb
    rev = lambda j: nb - 1 - j

    hw = n_heads * HEAD

    def body(q_ref, f_ref, i_ref, g_ref, l0_ref, l1_ref, nw_ref, save_ref, do_ref,
             dp_ref, dl0_ref, dl1_ref, dnw_ref, dst_ref):
        @pl.when(pl.program_id(0) == 0)
        def _():
            dst_ref[...] = jnp.zeros_like(dst_ref)
            dl0_ref[...] = jnp.zeros_like(dl0_ref)
            dl1_ref[...] = jnp.zeros_like(dl1_ref)
            dnw_ref[...] = jnp.zeros_like(dnw_ref)

        dnw_acc = jnp.zeros((1, HEAD), F32)
        for hs in _head_groups(n_heads, HG_BWD_GROUP):
            _, vjp = jax.vjp(_hg_block, [save_ref[h] for h in hs],
                             *[[r[:, _head_cols(h)] for h in hs] for r in (q_ref, f_ref, i_ref, g_ref)],
                             [l0_ref[h] for h in hs], [l1_ref[h] for h in hs], nw_ref[...])
            dst, dq, df, di, dg, dl0, dl1, dnw = vjp(([dst_ref[h] for h in hs], [do_ref[:, _head_cols(h)] for h in hs]))
            for i, h in enumerate(hs):
                dst_ref[h] = dst[i]
                for part, val in enumerate((dq, df, di, dg)):
                    dp_ref[:, part * hw + h * HEAD:part * hw + (h + 1) * HEAD] = val[i].astype(dp_ref.dtype)
                dl0_ref[h] += dl0[i]
                dl1_ref[h] += dl1[i]
            dnw_acc = dnw_acc + dnw
        dnw_ref[...] += dnw_acc

    head_rows = pl.BlockSpec((n_heads, 1, HEAD), lambda j: (0, 0, 0))
    return pl.pallas_call(
        body, name=name, grid=(nb,),
        in_specs=_hg_in_specs(n_heads, tb, rev) + [pl.BlockSpec((None, n_heads, HEAD, HEAD), lambda j: (rev(j), 0, 0, 0)),
                                                   pl.BlockSpec((tb, hw), lambda j: (rev(j), 0))],
        out_specs=[pl.BlockSpec((tb, 4 * hw), lambda j: (rev(j), 0)), head_rows, head_rows,
                   pl.BlockSpec((1, HEAD), lambda j: (0, 0))],
        out_shape=[jax.ShapeDtypeStruct((t, 4 * hw), BF16)] + [jax.ShapeDtypeStruct((n_heads, 1, HEAD), F32)] * 2
        + [jax.ShapeDtypeStruct((1, HEAD), F32)],
        scratch_shapes=[pltpu.VMEM((n_heads, HEAD, HEAD), F32)], compiler_params=_params(("arbitrary",)),
    )(proj, proj, proj, proj, l0, l1, nw, saved, d_ocat)


NN_DIMS = (((1,), (0,)), ((), ()))


def _split_bf16(x):
    hi = x.astype(BF16)
    return hi, (x - hi.astype(F32)).astype(BF16)


def _dot3(a, b, dims=NN_DIMS):
    (ah, al), (bh, bl) = _split_bf16(a), _split_bf16(b)
    dot = functools.partial(lax.dot_general, dimension_numbers=dims, preferred_element_type=F32)
    return dot(ah, bh) + dot(ah, bl) + dot(al, bh)


@jax.custom_vjp
def _mm3(a, b):
    return _dot3(a, b)


def _mm3_fwd(a, b):
    return _dot3(a, b), (a, b)


def _mm3_bwd(res, g):
    a, b = res
    return _dot3(g, b, NT_DIMS), _dot3(a, g, TN_DIMS)


_mm3.defvjp(_mm3_fwd, _mm3_bwd)


def _dot_bf16(a, b, dims=(((1,), (0,)), ((), ()))):
    return lax.dot_general(a.astype(BF16), b.astype(BF16), dims, preferred_element_type=F32)


def _inv_unit_lower_raw(ms):
    hs = range(len(ms))
    c = ms[0].shape[0]
    r = lax.broadcasted_iota(jnp.int32, (c, c), 0)
    q = lax.broadcasted_iota(jnp.int32, (c, c), 1)
    eye = (r == q).astype(F32)
    md = [jnp.where((r // GDN_INV_BLOCK) == (q // GDN_INV_BLOCK), ms[h], 0.0) for h in hs]
    p = [-md[h] for h in hs]
    t16 = [eye + p[h] for h in hs]
    for _ in range(int(math.log2(GDN_INV_BLOCK)) - 1):
        p = [_dot3(p[h], p[h]) for h in hs]
        t16 = [t16[h] + _dot3(t16[h], p[h]) for h in hs]
    p = [-_dot3(t16[h], ms[h] - md[h]) for h in hs]
    t2 = [eye + p[h] for h in hs]
    for _ in range(int(math.log2(c // GDN_INV_BLOCK)) - 1):
        p = [_dot3(p[h], p[h]) for h in hs]
        t2 = [t2[h] + _dot3(t2[h], p[h]) for h in hs]
    return [_dot3(t2[h], t16[h]) for h in hs]


@jax.custom_vjp
def _inv_unit_lower(ms):
    return _inv_unit_lower_raw(ms)


def _inv_fwd(ms):
    ts = _inv_unit_lower_raw(ms)
    return ts, ts


def _inv_bwd(ts, dts):
    hs = range(len(ts))
    inner = [_dot3(ts[h], dts[h], TN_DIMS) for h in hs]
    return ([-_dot3(inner[h], ts[h], NT_DIMS) for h in hs],)


_inv_unit_lower.defvjp(_inv_fwd, _inv_bwd)


def _gdn_block(precise, onehots, st, qc, kc, vc, g, ab, alog_row, dtb_row, nw):
    inverse, dot3 = precise
    hs = range(len(st))
    c = qc[0].shape[0]
    lane_sum = lambda v: jnp.sum(v, axis=-1, keepdims=True)
    a = [lane_sum(ab * onehots[h][0]) for h in hs]
    bb = [lane_sum(ab * onehots[h][1]) for h in hs]
    alog = [lane_sum(alog_row * onehots[h][0]) for h in hs]
    dtb = [lane_sum(dtb_row * onehots[h][0]) for h in hs]
    la = [-jnp.exp(alog[h]) * _softplus(a[h] + dtb[h]) for h in hs]
    beta = [jax.nn.sigmoid(bb[h]) for h in hs]
    q = [qc[h] * lax.rsqrt(lane_sum(qc[h] * qc[h]) + EPS) * (HEAD ** -0.5) for h in hs]
    k = [kc[h] * lax.rsqrt(lane_sum(kc[h] * kc[h]) + EPS) for h in hs]
    r = lax.broadcasted_iota(jnp.int32, (c, c), 0)
    s = lax.broadcasted_iota(jnp.int32, (c, c), 1)
    tri = (r >= s).astype(F32)
    g_cc = [dot3(tri, jnp.broadcast_to(la[h], (c, c))) for h in hs]
    g_cl = [dot3(tri, jnp.broadcast_to(la[h], (c, HEAD))) for h in hs]
    gamma = [jnp.exp(jnp.where(r >= s, g_cc[h] - g_cc[h].T, -1e30)) for h in hs]
    kk = [_dot_bf16(k[h], k[h], NT_DIMS) for h in hs]
    m = [jnp.where(r > s, beta[h] * kk[h] * gamma[h], 0.0) for h in hs]
    tm = inverse(m)
    eg = [jnp.exp(g_cl[h]) for h in hs]
    rhs = [jnp.concatenate([vc[h] * beta[h], k[h] * (beta[h] * eg[h])], axis=1) for h in hs]
    sol = [dot3(tm[h], rhs[h]) for h in hs]
    qk = [_dot_bf16(q[h], k[h], NT_DIMS) * gamma[h] for h in hs]
    g_last = [g_cl[h][c - 1:c] for h in hs]
    k_tail = [k[h] * jnp.exp(g_last[h] - g_cl[h]) for h in hs]
    v_new = [sol[h][:, :HEAD] - _dot_bf16(sol[h][:, HEAD:], st[h], NT_DIMS) for h in hs]
    o_st = [_dot_bf16(q[h] * eg[h], st[h], NT_DIMS) for h in hs]
    o = [o_st[h] + _dot_bf16(qk[h], v_new[h]) for h in hs]
    upd = [_dot_bf16(v_new[h], k_tail[h], TN_DIMS) for h in hs]
    st = [st[h] * jnp.exp(g_last[h]) + upd[h] for h in hs]
    out = [o[h] * _rms_scale(o[h]) * nw * _silu(g[h]) for h in hs]
    return st, out


def _head_onehots(n_heads, h):
    lane = lax.broadcasted_iota(jnp.int32, (1, LANES), 1)
    return (lane == h).astype(F32), (lane == n_heads + h).astype(F32)


def _gdn_in_specs(n_heads, c, time_index):
    hw = n_heads * HEAD
    qkv = [pl.BlockSpec((c, hw), functools.partial(lambda part, j: (time_index(j), part), part)) for part in range(3)]
    row = pl.BlockSpec((1, LANES), lambda j: (0, 0))
    return qkv + [pl.BlockSpec((c, hw), lambda j: (time_index(j), 7)),
                  pl.BlockSpec((c, LANES), lambda j: (time_index(j), 8 * n_heads)), row, row, row]


def _gdn_fwd(qkv, proj, alog_row, dtb_row, nw, n_heads, name):
    t = qkv.shape[0]
    hw = n_heads * HEAD
    c = _tile(t, GDN_CHUNK, GDN_CHUNK)
    nb = t // c

    def body(q_ref, k_ref, v_ref, g_ref, ab_ref, al_ref, dt_ref, nw_ref, o_ref, save_ref, st_ref):
        @pl.when(pl.program_id(0) == 0)
        def _():
            st_ref[...] = jnp.zeros_like(st_ref)

        for hs in _head_groups(n_heads, GDN_FWD_GROUP):
            st = [st_ref[h] for h in hs]
            for h, s in zip(hs, st):
                save_ref[h] = s
            st, out = _gdn_block((_inv_unit_lower_raw, _dot3), [_head_onehots(n_heads, h) for h in hs], st,
                                 *[[r[:, _head_cols(h)] for h in hs] for r in (q_ref, k_ref, v_ref, g_ref)],
                                 ab_ref[...], al_ref[...], dt_ref[...], nw_ref[...])
            for h, s, o in zip(hs, st, out):
                st_ref[h] = s
                o_ref[:, _head_cols(h)] = o.astype(o_ref.dtype)

    return pl.pallas_call(
        body, name=name, grid=(nb,), in_specs=_gdn_in_specs(n_heads, c, lambda j: j),
        out_specs=[pl.BlockSpec((c, hw), lambda j: (j, 0)),
                   pl.BlockSpec((None, n_heads, HEAD, HEAD), lambda j: (j, 0, 0, 0))],
        out_shape=[jax.ShapeDtypeStruct((t, hw), BF16), jax.ShapeDtypeStruct((nb, n_heads, HEAD, HEAD), F32)],
        scratch_shapes=[pltpu.VMEM((n_heads, HEAD, HEAD), F32)], compiler_params=_params(("arbitrary",)),
    )(qkv, qkv, qkv, proj, proj, alog_row, dtb_row, nw)


def _gdn_bwd(qkv, proj, alog_row, dtb_row, nw, saved, d_ocat, n_heads, name):
    t = qkv.shape[0]
    c = _tile(t, GDN_CHUNK, GDN_CHUNK)
    nb = t // c
    rev = lambda j: nb - 1 - j

    hw = n_heads * HEAD

    def body(q_ref, k_ref, v_ref, g_ref, ab_ref, al_ref, dt_ref, nw_ref, save_ref, do_ref,
             dqkv_ref, dg_ref, dab_ref, dal_ref, ddt_ref, dnw_ref, dst_ref):
        @pl.when(pl.program_id(0) == 0)
        def _():
            dst_ref[...] = jnp.zeros_like(dst_ref)
            dal_ref[...] = jnp.zeros_like(dal_ref)
            ddt_ref[...] = jnp.zeros_like(ddt_ref)
            dnw_ref[...] = jnp.zeros_like(dnw_ref)

        dab_acc = jnp.zeros((c, LANES), F32)
        row_acc = [jnp.zeros((1, LANES), F32)] * 3
        for hs in _head_groups(n_heads, GDN_BWD_GROUP):
            fn = functools.partial(_gdn_block, (_inv_unit_lower, _mm3), [_head_onehots(n_heads, h) for h in hs])
            _, vjp = jax.vjp(fn, [save_ref[h] for h in hs],
                             *[[r[:, _head_cols(h)] for h in hs] for r in (q_ref, k_ref, v_ref, g_ref)],
                             ab_ref[...], al_ref[...], dt_ref[...], nw_ref[...])
            dst, dq, dk, dv, dg, dab, dal, ddt, dnw = vjp(([dst_ref[h] for h in hs], [do_ref[:, _head_cols(h)] for h in hs]))
            for i, h in enumerate(hs):
                dst_ref[h] = dst[i]
                for part, val in enumerate((dq, dk, dv)):
                    dqkv_ref[:, part * hw + h * HEAD:part * hw + (h + 1) * HEAD] = val[i]
                dg_ref[:, _head_cols(h)] = dg[i].astype(dg_ref.dtype)
            dab_acc = dab_acc + dab
            row_acc = [acc + val for acc, val in zip(row_acc, (dal, ddt, dnw))]
        dab_ref[...] = dab_acc
        dal_ref[...] += row_acc[0]
        ddt_ref[...] += row_acc[1]
        dnw_ref[...] += row_acc[2]

    row = pl.BlockSpec((1, LANES), lambda j: (0, 0))
    return pl.pallas_call(
        body, name=name, grid=(nb,),
        in_specs=_gdn_in_specs(n_heads, c, rev) + [pl.BlockSpec((None, n_heads, HEAD, HEAD), lambda j: (rev(j), 0, 0, 0)),
                                                   pl.BlockSpec((c, hw), lambda j: (rev(j), 1))],
        out_specs=[pl.BlockSpec((c, 3 * hw), lambda j: (rev(j), 0)), pl.BlockSpec((c, hw), lambda j: (rev(j), 0)),
                   pl.BlockSpec((c, LANES), lambda j: (rev(j), 0)), row, row, row],
        out_shape=[jax.ShapeDtypeStruct((t, 3 * hw), F32), jax.ShapeDtypeStruct((t, hw), BF16),
                   jax.ShapeDtypeStruct((t, LANES), F32)] + [jax.ShapeDtypeStruct((1, LANES), F32)] * 3,
        scratch_shapes=[pltpu.VMEM((n_heads, HEAD, HEAD), F32)], compiler_params=_params(("arbitrary",)),
    )(qkv, qkv, qkv, proj, proj, alog_row, dtb_row, nw, saved, d_ocat)


def _pad_lanes(v, n):
    v = v.reshape(1, -1)
    return jnp.pad(v, ((0, 0), (0, n - v.shape[1])))


def _pack_rows(vecs):
    flat = jnp.concatenate([v.reshape(-1) for v in vecs])
    offs, o = [], 0
    for v in vecs:
        offs.append((o, v.size))
        o += v.size
    per_row = -(-o // (SUBLANES * LANES)) * LANES
    flat = jnp.pad(flat, (0, SUBLANES * per_row - o))
    return flat.reshape(SUBLANES, per_row), offs


def _unpack(gathered, offs):
    per_dev = gathered.reshape(N_DEV, -1)
    return [per_dev[:, o:o + n] for o, n in offs]


def _sum_devices(part):
    acc = part[0]
    for i in range(1, N_DEV):
        acc = acc + part[i]
    return acc


def kernel(x, c, w_ada, b_ada, pre_mix_norm, post_mix_norm, pre_ffn_norm, post_ffn_norm, w_in, hg_lb_logits, hg_norm, gdn_conv_w, gdn_a_log, gdn_dt_bias, gdn_norm, w_out, w_ff1, w_ff2, loss_target, m_w_ada, m_b_ada, m_pre_mix_norm, m_post_mix_norm, m_pre_ffn_norm, m_post_ffn_norm, m_w_in, m_hg_lb_logits, m_hg_norm, m_gdn_conv_w, m_gdn_a_log, m_gdn_dt_bias, m_gdn_norm, m_w_out, m_w_ff1, m_w_ff2, v_w_ada, v_b_ada, v_pre_mix_norm, v_post_mix_norm, v_pre_ffn_norm, v_post_ffn_norm, v_w_in, v_hg_lb_logits, v_hg_norm, v_gdn_conv_w, v_gdn_a_log, v_gdn_dt_bias, v_gdn_norm, v_w_out, v_w_ff1, v_w_ff2):
    assert x.shape[0] == 1 and w_ada.shape[0] == 1 and hg_lb_logits.shape[0] == 2
    t, d = x.shape[1], x.shape[2]
    n_heads = (d // 2) // HEAD
    hw = n_heads * HEAD
    in_cols = 8 * hw + 2 * n_heads
    np_cols = 8 * hw + 2 * LANES
    d_ff = w_ff1.shape[2] * N_CHIP
    na = w_ada.shape[2]
    ax, ay, ac = lax.axis_index("x"), lax.axis_index("y"), lax.axis_index("c")
    chip = 2 * ax + ay
    dev = 4 * ax + 2 * ay + ac

    x2d, tgt = x[0], loss_target[0]

    pack1, offs1 = _pack_rows([c[0], gdn_conv_w[0]])
    c_all, convw_all = _unpack(_gather8(pack1, "gather_cond"), offs1)
    conv_sh = gdn_conv_w.shape[2]
    conv_w = jnp.concatenate([convw_all[2 * j].reshape(CONV_K, conv_sh) for j in range(N_CHIP)], axis=1)

    b_s = lax.dynamic_slice(b_ada, (0, chip * na), (1, na))
    mod_part = _mod_part(c_all, w_ada[0], b_s, "mod_part")
    pack2, offs2 = _pack_rows([mod_part])
    (mod_parts,) = _unpack(_gather8(pack2, "gather_mod"), offs2)
    mod_all = jnp.concatenate([mod_parts[2 * j].reshape(N_DEV, na) for j in range(N_CHIP)], axis=1)
    mod = lax.dynamic_slice(mod_all, (dev, 0), (1, N_MOD * d))
    sh_m, sc_m, gt_m, sh_f, sc_f, gt_f = [mod[:, i * d:(i + 1) * d] for i in range(N_MOD)]

    h1 = _norm_mod(x2d, pre_mix_norm, sc_m, sh_m, "norm_mod_mix")
    (g_in,), h1 = _gather_weights([jnp.transpose(w_in[0]).astype(BF16)], "gather_w_in", sequencer_id=5, after=h1)
    late, g_in = lax.optimization_barrier(([w_out[0].astype(BF16), w_ff1[0].astype(BF16), w_ff2[0].astype(BF16)], g_in))
    g_out, g_ff1, g_ff2 = _gather_weights(late, "gather_weights_late", sequencer_id=1)
    w_in_f = jnp.pad(g_in.reshape(in_cols, d), ((0, np_cols - in_cols), (0, 0)))
    w_out_f = g_out.reshape(d, d)
    w_ff2_f = g_ff2.reshape(d_ff, d)

    proj = _matmul(h1, w_in_f, "nt", F32, "mm_in", tn=768)
    l0, l1 = hg_lb_logits[0].reshape(n_heads, 1, HEAD), hg_lb_logits[1].reshape(n_heads, 1, HEAD)
    o_hg, hg_saved = _hgrn2_fwd(proj, l0, l1, hg_norm, n_heads, "hgrn2_fwd")
    qkv = _conv_fwd(proj, conv_w, 4 * n_heads, "conv_fwd")
    alog_row, dtb_row = _pad_lanes(gdn_a_log, LANES), _pad_lanes(gdn_dt_bias, LANES)
    o_gdn, gdn_saved = _gdn_fwd(qkv, proj, alog_row, dtb_row, gdn_norm, n_heads, "gdn_fwd")
    o_cat = jnp.concatenate([o_hg, o_gdn], axis=1)
    y1 = _matmul(o_cat, w_out_f, "nn", F32, "mm_out")
    x_mid = _resid(x2d, y1, post_mix_norm, gt_m, "resid_mix")

    h2 = _norm_mod(x_mid, pre_ffn_norm, sc_f, sh_f, "norm_mod_ffn")
    relu_a1, r1 = _matmul(h2, g_ff1, "nn", BF16, "mm_ff1", relu2=True, b_split=True)
    y2 = _matmul(r1, w_ff2_f, "nn", F32, "mm_ff2")
    d_out, loss_row = _loss_head(x_mid, y2, post_ffn_norm, gt_f, tgt, "loss_head")

    in_sh = in_cols // N_CHIP
    ff_sh = d_ff // N_CHIP
    my_half = jnp.reshape(ac, (1,)).astype(jnp.int32)

    def start_reduce(by_chip, tag, collective_id):
        to_sib = [_row_half_to_bf16(a, 1 - my_half, None, f"sibling_half_{tag}{i}") for i, a in enumerate(by_chip)]
        from_sib = _sibling_exchange(to_sib, f"sibling_partials_{tag}")
        chip_part = [_row_half_to_bf16(a, my_half, s, f"add_halves_{tag}{i}") for i, (a, s) in enumerate(zip(by_chip, from_sib))]
        return _sequencer_chip_exchange(chip_part, f"scatter_grads_{tag}", collective_id)

    dy2, d_gt_f, d_post_ffn = _resid_bwd(d_out, y2, post_ffn_norm, gt_f, "resid_ffn_bwd")
    gw_ff2 = _matmul(r1, dy2, "tn", BF16, "mm_ff2_dw")
    gw_ff2, dy2 = lax.optimization_barrier((gw_ff2, dy2))
    da1 = _matmul(dy2, w_ff2_f, "nt", BF16, "mm_ff2_dx", times=relu_a1)
    gw_ff1 = _matmul(h2, da1, "tn", BF16, "mm_ff1_dw", out_split=True)
    gw_ff1, da1 = lax.optimization_barrier((gw_ff1, da1))
    recv_ff2, recv_ff1 = _sequencer_chip_exchange([gw_ff2.reshape(N_CHIP, ff_sh, d), gw_ff1], "scatter_grads_ff", 2)
    dh2 = _matmul(da1, g_ff1, "nt", BF16, "mm_ff1_dx", b_split=True)
    d_mid, d_pre_ffn, d_sc_f, d_sh_f = _norm_mod_bwd(x_mid, pre_ffn_norm, sc_f, dh2, d_out, "norm_mod_ffn_bwd")

    dy1, d_gt_m, d_post_mix = _resid_bwd(d_mid, y1, post_mix_norm, gt_m, "resid_mix_bwd")
    gw_out = _matmul(o_cat, dy1, "tn", BF16, "mm_out_dw")
    gw_out, dy1 = lax.optimization_barrier((gw_out, dy1))
    (recv_out,) = _sequencer_chip_exchange([gw_out.reshape(N_CHIP, d // N_CHIP, d)], "scatter_grads_out", 3)
    d_ocat = _matmul(dy1, w_out_f, "nt", F32, "mm_out_dx")
    dp_hg, dl0, dl1, d_hg_norm = _hgrn2_bwd(proj, l0, l1, hg_norm, hg_saved, d_ocat, n_heads, "hgrn2_bwd")
    dqkv, dg_g, dab, d_alog, d_dtb, d_gdn_norm = _gdn_bwd(
        qkv, proj, alog_row, dtb_row, gdn_norm, gdn_saved, d_ocat, n_heads, "gdn_bwd")
    du, d_conv_w = _conv_bwd(proj, conv_w, dqkv, 4 * n_heads, "conv_bwd")
    dproj = jnp.concatenate([dp_hg, du, dg_g, dab.astype(BF16), jnp.zeros((t, LANES), BF16)], axis=1)
    gw_in = _matmul(dproj, h1, "tn", F32, "mm_in_dw", tm=768)
    (recv_in,) = start_reduce([gw_in[:in_cols].reshape(N_CHIP, in_sh, d)], "in", 4)
    dh1 = _matmul(dproj, w_in_f, "nn", BF16, "mm_in_dx", tk=2816)
    grad_x, d_pre_mix, d_sc_m, d_sh_m = _norm_mod_bwd(x2d, pre_mix_norm, sc_m, dh1, d_mid, "norm_mod_mix_bwd")

    d_mod = jnp.concatenate([d_sh_m, d_sc_m, d_gt_m, d_sh_f, d_sc_f, d_gt_f], axis=1)
    d_lb_logits = jnp.stack([dl0.reshape(n_heads, HEAD), dl1.reshape(n_heads, HEAD)])
    pack3, offs3 = _pack_rows([loss_row[0, :1], d_pre_mix, d_post_mix, d_pre_ffn, d_post_ffn, d_lb_logits, d_hg_norm,
                               d_conv_w, d_alog[0, :n_heads], d_dtb[0, :n_heads], d_gdn_norm, d_mod])
    parts = _unpack(_gather8(pack3, "gather_vec_grads"), offs3)
    sums = [_sum_devices(p) for p in parts[:-1]]
    loss = sums[0][0]
    dmod_all = parts[-1]
    g_b_ada = _sum_devices(dmod_all).reshape(1, N_MOD * d)
    g_conv_full = sums[7].reshape(CONV_K, N_CHIP * conv_sh)
    g_conv = lax.dynamic_slice(g_conv_full, (0, chip * conv_sh), (CONV_K, conv_sh))
    gw_ada = _wada_grad(c_all, lax.dynamic_slice(dmod_all, (0, chip * na), (N_DEV, na)), "wada_grad")

    sum_ff2 = _sum_chips(recv_ff2, "sum_chips_ff2")
    recv_ff1, sum_ff2 = lax.optimization_barrier((recv_ff1, sum_ff2))
    sum_ff1 = _sum_chips(recv_ff1, "sum_chips_ff1")
    recv_out, sum_ff1 = lax.optimization_barrier((recv_out, sum_ff1))
    sum_out = _sum_chips(recv_out, "sum_chips_out")
    recv_in, sum_out = lax.optimization_barrier((recv_in, sum_out))
    mine = [_sum_chips(recv_in, "sum_chips_in"), sum_out, sum_ff1, sum_ff2]
    theirs = _sibling_exchange(mine, "sibling_grads")

    big = {}
    for i, (nm, w_, m_, v_) in enumerate([("w_in", w_in, m_w_in, v_w_in), ("w_out", w_out, m_w_out, v_w_out),
                                          ("w_ff1", w_ff1, m_w_ff1, v_w_ff1), ("w_ff2", w_ff2, m_w_ff2, v_w_ff2)]):
        if nm == "w_in":
            res_t = _adamw(jnp.transpose(w_[0]), [mine[i], theirs[i]], jnp.transpose(m_[0]), jnp.transpose(v_[0]),
                           f"adamw_{nm}", by_core=True)
            big[nm] = [jnp.transpose(o)[None] for o in res_t]
        else:
            big[nm] = [o[None] for o in _adamw(w_[0], [mine[i], theirs[i]], m_[0], v_[0], f"adamw_{nm}")]
    big["w_ada"] = [o[None] for o in _adamw(w_ada[0], [gw_ada], m_w_ada[0], v_w_ada[0], "adamw_w_ada")]

    small_names = ["b_ada", "pre_mix_norm", "post_mix_norm", "pre_ffn_norm", "post_ffn_norm", "hg_lb_logits", "hg_norm",
                   "gdn_conv_w", "gdn_a_log", "gdn_dt_bias", "gdn_norm"]
    small_w = [b_ada, pre_mix_norm, post_mix_norm, pre_ffn_norm, post_ffn_norm, hg_lb_logits, hg_norm, gdn_conv_w,
               gdn_a_log, gdn_dt_bias, gdn_norm]
    small_m = [m_b_ada, m_pre_mix_norm, m_post_mix_norm, m_pre_ffn_norm, m_post_ffn_norm, m_hg_lb_logits, m_hg_norm,
               m_gdn_conv_w, m_gdn_a_log, m_gdn_dt_bias, m_gdn_norm]
    small_v = [v_b_ada, v_pre_mix_norm, v_post_mix_norm, v_pre_ffn_norm, v_post_ffn_norm, v_hg_lb_logits, v_hg_norm,
               v_gdn_conv_w, v_gdn_a_log, v_gdn_dt_bias, v_gdn_norm]
    small_g = [g_b_ada, sums[1], sums[2], sums[3], sums[4], sums[5], sums[6], g_conv, sums[8], sums[9], sums[10]]
    pw, offs_s = _pack_rows(small_w)
    pg, _ = _pack_rows(small_g)
    pm, _ = _pack_rows(small_m)
    pv, _ = _pack_rows(small_v)
    packed = _adamw(pw, [pg], pm, pv, "adamw_vectors")
    small = {}
    for nm, w_, (o, n) in zip(small_names, small_w, offs_s):
        small[nm] = [p.reshape(-1)[o:o + n].reshape(w_.shape) for p in packed]

    order = ["w_ada", "b_ada", "pre_mix_norm", "post_mix_norm", "pre_ffn_norm", "post_ffn_norm", "w_in", "hg_lb_logits",
             "hg_norm", "gdn_conv_w", "gdn_a_log", "gdn_dt_bias", "gdn_norm", "w_out", "w_ff1", "w_ff2"]
    res = {**big, **small}
    outs = [loss, grad_x[None]]
    for k in range(4):
        outs += [res[nm][k] for nm in order]
    return tuple(outs)
```

```python
import functools
import math

import jax
import jax.numpy as jnp
from jax import lax
from jax.experimental import pallas as pl
from jax.experimental.pallas import tpu as pltpu
from jax.experimental.pallas import tpu_sc as plsc

F32 = jnp.float32
BF16 = jnp.bfloat16
HI = lax.Precision.HIGHEST
MESH = pl.DeviceIdType.MESH

LANES = 128
SUBLANES = 8
VMEM_LIMIT = 48 * 1024 * 1024
EPS = 1e-6
HEAD = 128
CONV_K = 4
GDN_CHUNK = 64
GDN_INV_BLOCK = 16
HG_SUB = 16
HG_BLOCK = 128
HG_FWD_GROUP = 8
HG_BWD_GROUP = 4
GDN_FWD_GROUP = 8
GDN_BWD_GROUP = 8
N_MOD = 6
N_DEV = 8
N_CHIP = 4

ADAM_LR = 0.001
ADAM_B1 = 0.9
ADAM_B2 = 0.999
ADAM_EPS = 1e-08
ADAM_WD = 0.01
ADAM_STEP = 10

NT_DIMS = (((1,), (1,)), ((), ()))
TN_DIMS = (((0,), (0,)), ((), ()))


def _tile(dim, target, align):
    if dim <= target:
        return dim
    best = dim
    t = align
    while t <= target:
        if dim % t == 0:
            best = t
        t += align
    return best


def _elementwise_tiles(r, c):
    tc = _tile(c, 1024, LANES)
    tr = _tile(r, max(16, (256 * 1024) // tc // 16 * 16), 16)
    if tr == r and r * tc > 512 * 1024:
        tc = _tile(c, max(LANES, (256 * 1024) // r // LANES * LANES), LANES)
    return tr, tc


def _params(sem):
    return pltpu.CompilerParams(dimension_semantics=sem, vmem_limit_bytes=VMEM_LIMIT)


def _silu(x):
    return x * jax.nn.sigmoid(x)


def _softplus(x):
    pos = x > 0
    return jnp.where(pos, x, 0.0) + jnp.log(1.0 + jnp.exp(jnp.where(pos, -x, x)))


def _rms_scale(x):
    return lax.rsqrt(jnp.mean(x * x, axis=-1, keepdims=True) + EPS)


def _gather8(x_shard, name):
    m_per, n = x_shard.shape
    assert m_per % SUBLANES == 0 and n % LANES == 0

    def body(x_ref, out_ref, send_sems, recv_sems, local_sem):
        x, y, c = lax.axis_index("x"), lax.axis_index("y"), lax.axis_index("c")
        me, sibling = (x, y, c), (x, y, 1 - c)
        chips = [(1 - x, y), (x, 1 - y), (1 - x, 1 - y)]

        def rows(px, py, pc):
            return out_ref.at[pl.ds((4 * px + 2 * py + pc) * m_per, m_per), :]

        def copy(k, block, to, src=None):
            return pltpu.make_async_remote_copy(
                src_ref=rows(*block) if src is None else src, dst_ref=rows(*block),
                send_sem=send_sems.at[k], recv_sem=recv_sems.at[k], device_id=to, device_id_type=MESH)

        mine = pltpu.make_async_copy(x_ref, rows(*me), local_sem)
        mine.start()
        first = [copy(0, me, sibling, src=x_ref)]
        first += [copy(1 + j, me, (*chip, c), src=x_ref) for j, chip in enumerate(chips)]
        for cp in first:
            cp.start()
        passed = [copy(4 + j, (*chip, c), sibling) for j, chip in enumerate(chips)]
        for j, chip in enumerate(chips):
            copy(1 + j, (*chip, c), me).wait_recv()
            passed[j].start()
        copy(0, sibling, me).wait_recv()
        for j, chip in enumerate(chips):
            copy(4 + j, (*chip, 1 - c), me).wait_recv()
        for cp in first + passed:
            cp.wait_send()
        mine.wait()

    return pl.pallas_call(
        body, name=name,
        out_shape=jax.ShapeDtypeStruct((N_DEV * m_per, n), x_shard.dtype),
        in_specs=[pl.BlockSpec(memory_space=pltpu.VMEM)],
        out_specs=pl.BlockSpec(memory_space=pltpu.VMEM),
        scratch_shapes=[pltpu.SemaphoreType.DMA((7,)), pltpu.SemaphoreType.DMA((7,)), pltpu.SemaphoreType.DMA],
        compiler_params=pltpu.CompilerParams(vmem_limit_bytes=VMEM_LIMIT),
    )(x_shard)


def _gather_weights(arrs, name, sequencer_id=None, after=None):
    n = len(arrs)
    out_shapes = [jax.ShapeDtypeStruct((N_CHIP,) + a.shape, a.dtype) for a in arrs]

    def body(*refs):
        ins, outs = refs[:n], refs[n:2 * n]
        ici_send, ici_recv, d2d_send, d2d_recv = refs[2 * n:]
        if sequencer_id is not None:
            chips, sib = _chip_peers()
            _handshake(chips + [sib])
        x, y, c = lax.axis_index("x"), lax.axis_index("y"), lax.axis_index("c")
        me = 2 * x + y
        peers = [(1 - x, y), (x, 1 - y), (1 - x, 1 - y)]

        def half(a, cc):
            r = arrs[a].shape[0]
            cut = r // 32 * 16
            return pl.ds(0, cut) if cc == 0 else pl.ds(cut, r - cut)

        def exchange(mine):
            sibling = (x, y, 1 - mine)
            sent = []
            for a in range(n):
                for k, (px, py) in enumerate(peers):
                    cp = pltpu.make_async_remote_copy(
                        src_ref=ins[a].at[half(a, mine)], dst_ref=outs[a].at[me, half(a, mine)],
                        send_sem=ici_send.at[3 * a + k], recv_sem=ici_recv.at[3 * a + k],
                        device_id=(px, py, mine), device_id_type=MESH)
                    cp.start()
                    sent.append(cp)
            for a in range(n):
                for k, (px, py) in enumerate(peers):
                    landed = outs[a].at[2 * px + py, half(a, mine)]
                    pltpu.make_async_remote_copy(
                        src_ref=landed, dst_ref=landed, send_sem=ici_send.at[3 * a + k], recv_sem=ici_recv.at[3 * a + k],
                        device_id=(px, py, mine), device_id_type=MESH).wait_recv()
                    fwd = pltpu.make_async_remote_copy(
                        src_ref=landed, dst_ref=landed, send_sem=d2d_send.at[3 * a + k], recv_sem=d2d_recv.at[3 * a + k],
                        device_id=sibling, device_id_type=MESH)
                    fwd.start()
                    sent.append(fwd)
            for a in range(n):
                for k, (px, py) in enumerate(peers):
                    passed = outs[a].at[2 * px + py, half(a, 1 - mine)]
                    pltpu.make_async_remote_copy(
                        src_ref=passed, dst_ref=passed, send_sem=d2d_send.at[3 * a + k], recv_sem=d2d_recv.at[3 * a + k],
                        device_id=sibling, device_id_type=MESH).wait_recv()
            for cp in sent:
                cp.wait_send()

        for core in (0, 1):
            pl.when(c == core)(functools.partial(exchange, core))

    sems = [pltpu.SemaphoreType.DMA((3 * n,))] * 4
    if sequencer_id is None:
        hbm = pl.BlockSpec(memory_space=pltpu.HBM)
        gathered = pl.pallas_call(body, name=name, out_shape=out_shapes, in_specs=[hbm] * n, out_specs=[hbm] * n,
                                  scratch_shapes=sems)(*arrs)
    else:
        gathered = pl.kernel(body, out_type=out_shapes, mesh=plsc.ScalarSubcoreMesh(axis_name="sequencer", num_cores=1),
                             name=name, scratch_types=sems,
                             compiler_params=pltpu.CompilerParams(collective_id=sequencer_id))(*arrs)
    if after is not None:
        gathered, after = lax.optimization_barrier((gathered, after))
    chip = 2 * lax.axis_index("x") + lax.axis_index("y")
    filled = [lax.dynamic_update_slice(g, a[None], (chip, 0, 0)) for g, a in zip(gathered, arrs)]
    return filled if after is None else (filled, after)


def _chip_peers():
    x, y, c = lax.axis_index("x"), lax.axis_index("y"), lax.axis_index("c")
    return [(1 - x, y, c), (x, 1 - y, c), (1 - x, 1 - y, c)], (x, y, 1 - c)


def _handshake(peers):
    barrier = pltpu.get_barrier_semaphore()
    for peer in peers:
        pl.semaphore_signal(barrier, inc=1, device_id=peer, device_id_type=MESH)
    pl.semaphore_wait(barrier, len(peers))


def _sequencer_chip_exchange(arrs, name, collective_id):
    n = len(arrs)
    out_types = [jax.ShapeDtypeStruct(a.shape, a.dtype) for a in arrs]

    def body(*refs):
        ins, outs = refs[:n], refs[n:2 * n]
        send_sems, recv_sems = refs[2 * n:]
        chips, _ = _chip_peers()
        _handshake(chips)
        me = 2 * lax.axis_index("x") + lax.axis_index("y")
        sent = []
        for a in range(n):
            for k, peer in enumerate(chips):
                cp = pltpu.make_async_remote_copy(
                    src_ref=ins[a].at[2 * peer[0] + peer[1]], dst_ref=outs[a].at[me],
                    send_sem=send_sems.at[3 * a + k], recv_sem=recv_sems.at[3 * a + k], device_id=peer, device_id_type=MESH)
                cp.start()
                sent.append(cp)
        for a in range(n):
            for k, peer in enumerate(chips):
                landed = outs[a].at[2 * peer[0] + peer[1]]
                pltpu.make_async_remote_copy(
                    src_ref=landed, dst_ref=landed, send_sem=send_sems.at[3 * a + k], recv_sem=recv_sems.at[3 * a + k],
                    device_id=peer, device_id_type=MESH).wait_recv()
        for cp in sent:
            cp.wait_send()

    received = pl.kernel(
        body, out_type=out_types, mesh=plsc.ScalarSubcoreMesh(axis_name="sequencer", num_cores=1), name=name,
        scratch_types=[pltpu.SemaphoreType.DMA((3 * n,))] * 2,
        compiler_params=pltpu.CompilerParams(collective_id=collective_id),
    )(*arrs)
    chip = 2 * lax.axis_index("x") + lax.axis_index("y")
    return [lax.dynamic_update_slice(r, lax.dynamic_slice(a, (chip, 0, 0), (1,) + a.shape[1:]), (chip, 0, 0))
            for r, a in zip(received, arrs)]


def _sibling_exchange(arrs, name):
    n = len(arrs)

    def body(*refs):
        ins, outs = refs[:n], refs[n:2 * n]
        send_sems, recv_sems = refs[2 * n:]
        sibling = (lax.axis_index("x"), lax.axis_index("y"), 1 - lax.axis_index("c"))
        cps = []
        for a in range(n):
            cp = pltpu.make_async_remote_copy(src_ref=ins[a], dst_ref=outs[a], send_sem=send_sems.at[a],
                                              recv_sem=recv_sems.at[a], device_id=sibling, device_id_type=MESH)
            cp.start()
            cps.append(cp)
        for cp in cps:
            cp.wait_recv()
        for cp in cps:
            cp.wait_send()

    hbm = pl.BlockSpec(memory_space=pltpu.HBM)
    return pl.pallas_call(
        body, name=name, out_shape=[jax.ShapeDtypeStruct(a.shape, a.dtype) for a in arrs],
        in_specs=[hbm] * n, out_specs=[hbm] * n,
        scratch_shapes=[pltpu.SemaphoreType.DMA((n,)), pltpu.SemaphoreType.DMA((n,))],
    )(*arrs)


def _matmul(a, b, mode, out_dtype, name, tm=1024, tn=1024, tk=2048, relu2=False, times=None, b_split=False,
            out_split=False):
    b_shape = (b.shape[1], b.shape[2] * N_CHIP) if b_split else b.shape
    if mode == "nn":
        (m, k), (k2, n) = a.shape, b_shape
    elif mode == "nt":
        (m, k), (n, k2) = a.shape, b_shape
    else:
        (k, m), (k2, n) = a.shape, b_shape
    assert k == k2, (a.shape, b.shape, mode)
    n_cut = n // N_CHIP if (out_split or (b_split and mode != "nt")) else n
    k_cut = k // N_CHIP if (b_split and mode == "nt") else k
    tm, tn, tk = _tile(m, tm, LANES), _tile(n_cut, tn, LANES), _tile(k_cut, tk, LANES)
    assert n_cut % tn == 0 and k_cut % tk == 0 and m % tm == 0, (name, m, n, k, tm, tn, tk)
    nk = k // tk
    nbc, nkc = n_cut // tn, k_cut // tk
    n_in = 2 if times is None else 3
    n_out = 2 if relu2 else 1

    def product(a_ref, b_ref):
        if mode == "nn":
            return jnp.dot(a_ref[...], b_ref[...], preferred_element_type=F32)
        return lax.dot_general(a_ref[...], b_ref[...], NT_DIMS if mode == "nt" else TN_DIMS, preferred_element_type=F32)

    def finish(p, refs, o_refs):
        if relu2:
            p = jnp.maximum(p, 0.0)
            o_refs[0][...] = p.astype(o_refs[0].dtype)
            o_refs[1][...] = (p * p).astype(o_refs[1].dtype)
        elif times is not None:
            o_refs[0][...] = (2.0 * refs[2][...].astype(F32) * p).astype(o_refs[0].dtype)
        else:
            o_refs[0][...] = p.astype(o_refs[0].dtype)

    def body(*refs):
        o_refs = refs[n_in:n_in + n_out]
        if nk == 1:
            finish(product(refs[0], refs[1]), refs, o_refs)
            return
        acc_ref = refs[n_in + n_out]
        kk = pl.program_id(2)

        @pl.when(kk == 0)
        def _():
            acc_ref[...] = product(refs[0], refs[1])

        @pl.when((kk > 0) & (kk < nk - 1))
        def _():
            acc_ref[...] += product(refs[0], refs[1])

        @pl.when(kk == nk - 1)
        def _():
            finish(acc_ref[...] + product(refs[0], refs[1]), refs, o_refs)

    if mode == "tn":
        a_spec = pl.BlockSpec((tk, tm), lambda i, j, kk: (kk, i))
    else:
        a_spec = pl.BlockSpec((tm, tk), lambda i, j, kk: (i, kk))
    if mode == "nt":
        b_spec = (pl.BlockSpec((None, tn, tk), lambda i, j, kk: (kk // nkc, j, kk % nkc)) if b_split
                  else pl.BlockSpec((tn, tk), lambda i, j, kk: (j, kk)))
    else:
        b_spec = (pl.BlockSpec((None, tk, tn), lambda i, j, kk: (j // nbc, kk, j % nbc)) if b_split
                  else pl.BlockSpec((tk, tn), lambda i, j, kk: (kk, j)))
    mn_spec = pl.BlockSpec((tm, tn), lambda i, j, kk: (i, j))
    if out_split:
        o_spec = pl.BlockSpec((None, tm, tn), lambda i, j, kk: (j // nbc, i, j % nbc))
        o_shape = jax.ShapeDtypeStruct((N_CHIP, m, n_cut), out_dtype)
    else:
        o_spec, o_shape = mn_spec, jax.ShapeDtypeStruct((m, n), out_dtype)
    out = pl.pallas_call(
        body, name=name, grid=(m // tm, n // tn, nk), in_specs=[a_spec, b_spec] + [mn_spec] * (n_in - 2),
        out_specs=[o_spec] * n_out, out_shape=[o_shape] * n_out,
        scratch_shapes=[] if nk == 1 else [pltpu.VMEM((tm, tn), F32)],
        compiler_params=_params(("parallel", "parallel", "arbitrary")),
    )(*((a, b) if times is None else (a, b, times)))
    return out if relu2 else out[0]


def _mod_part(c_all, w_s, b_s, name):
    d, na = w_s.shape
    tn = _tile(na, 512, LANES)

    def body(c_ref, w_ref, b_ref, o_ref):
        ca = _silu(c_ref[...]).astype(BF16)
        o_ref[...] = jnp.dot(ca, w_ref[...].astype(BF16), preferred_element_type=F32) + b_ref[...]

    return pl.pallas_call(
        body, name=name, grid=(na // tn,),
        in_specs=[pl.BlockSpec((N_DEV, d), lambda j: (0, 0)), pl.BlockSpec((d, tn), lambda j: (0, j)),
                  pl.BlockSpec((1, tn), lambda j: (0, j))],
        out_specs=pl.BlockSpec((N_DEV, tn), lambda j: (0, j)),
        out_shape=jax.ShapeDtypeStruct((N_DEV, na), F32), compiler_params=_params(("parallel",)),
    )(c_all, w_s, b_s)


def _wada_grad(c_all, dmod_s, name):
    d = c_all.shape[1]
    na = dmod_s.shape[1]
    td, tn = _tile(d, 512, LANES), _tile(na, 512, LANES)

    def body(c_ref, g_ref, o_ref):
        o_ref[...] = lax.dot_general(_silu(c_ref[...]), g_ref[...], TN_DIMS, precision=HI, preferred_element_type=F32)

    return pl.pallas_call(
        body, name=name, grid=(d // td, na // tn),
        in_specs=[pl.BlockSpec((N_DEV, td), lambda i, j: (0, i)), pl.BlockSpec((N_DEV, tn), lambda i, j: (0, j))],
        out_specs=pl.BlockSpec((td, tn), lambda i, j: (i, j)),
        out_shape=jax.ShapeDtypeStruct((d, na), F32), compiler_params=_params(("parallel", "parallel")),
    )(c_all, dmod_s)


def _row_specs(tb, d, n_full, n_vec):
    full = pl.BlockSpec((tb, d), lambda i: (i, 0))
    vec = pl.BlockSpec((1, d), lambda i: (0, 0))
    return [full] * n_full + [vec] * n_vec


def _norm_mod(x, w, sc, sh, name):
    t, d = x.shape
    tb = _tile(t, 256, SUBLANES)

    def body(x_ref, w_ref, sc_ref, sh_ref, o_ref):
        xv = x_ref[...]
        o_ref[...] = (xv * _rms_scale(xv) * w_ref[...] * (1.0 + sc_ref[...]) + sh_ref[...]).astype(o_ref.dtype)

    return pl.pallas_call(
        body, name=name, grid=(t // tb,), in_specs=_row_specs(tb, d, 1, 3),
        out_specs=pl.BlockSpec((tb, d), lambda i: (i, 0)), out_shape=jax.ShapeDtypeStruct((t, d), BF16),
        compiler_params=_params(("parallel",)),
    )(x, w, sc, sh)


def _norm_mod_bwd(x, w, sc, dh, dres, name):
    t, d = x.shape
    tb = _tile(t, 256, SUBLANES)

    def body(x_ref, w_ref, sc_ref, dh_ref, dres_ref, dx_ref, dw_ref, dsc_ref, dsh_ref):
        @pl.when(pl.program_id(0) == 0)
        def _():
            dw_ref[...] = jnp.zeros_like(dw_ref)
            dsc_ref[...] = jnp.zeros_like(dsc_ref)
            dsh_ref[...] = jnp.zeros_like(dsh_ref)

        xv = x_ref[...]
        r = _rms_scale(xv)
        xn = xv * r
        g = dh_ref[...].astype(F32)
        wv, one_sc = w_ref[...], 1.0 + sc_ref[...]
        gxn = g * xn
        dsh_ref[...] += jnp.sum(g, axis=0, keepdims=True)
        dsc_ref[...] += jnp.sum(gxn, axis=0, keepdims=True) * wv
        dw_ref[...] += jnp.sum(gxn, axis=0, keepdims=True) * one_sc
        dxn = g * (wv * one_sc)
        dx_ref[...] = dres_ref[...] + r * (dxn - xn * jnp.mean(dxn * xn, axis=-1, keepdims=True))

    vec_out = pl.BlockSpec((1, d), lambda i: (0, 0))
    return pl.pallas_call(
        body, name=name, grid=(t // tb,),
        in_specs=[pl.BlockSpec((tb, d), lambda i: (i, 0)), pl.BlockSpec((1, d), lambda i: (0, 0)),
                  pl.BlockSpec((1, d), lambda i: (0, 0)), pl.BlockSpec((tb, d), lambda i: (i, 0)),
                  pl.BlockSpec((tb, d), lambda i: (i, 0))],
        out_specs=[pl.BlockSpec((tb, d), lambda i: (i, 0)), vec_out, vec_out, vec_out],
        out_shape=[jax.ShapeDtypeStruct((t, d), F32)] + [jax.ShapeDtypeStruct((1, d), F32)] * 3,
        compiler_params=_params(("arbitrary",)),
    )(x, w, sc, dh, dres)


def _resid_norm_mod(x, y, w, gt, w2, sc, sh, name):
    t, d = x.shape
    tb = _tile(t, 256, SUBLANES)

    def body(x_ref, y_ref, w_ref, gt_ref, w2_ref, sc_ref, sh_ref, o_ref, h_ref):
        yv = y_ref[...]
        x2 = x_ref[...] + gt_ref[...] * (yv * _rms_scale(yv) * w_ref[...])
        o_ref[...] = x2
        h_ref[...] = (x2 * _rms_scale(x2) * w2_ref[...] * (1.0 + sc_ref[...]) + sh_ref[...]).astype(h_ref.dtype)

    full = pl.BlockSpec((tb, d), lambda i: (i, 0))
    return pl.pallas_call(
        body, name=name, grid=(t // tb,), in_specs=_row_specs(tb, d, 2, 5), out_specs=[full, full],
        out_shape=[jax.ShapeDtypeStruct((t, d), F32), jax.ShapeDtypeStruct((t, d), BF16)],
        compiler_params=_params(("parallel",)),
    )(x, y, w, gt, w2, sc, sh)


def _loss_head(x2, y2, w, gt, target, name):
    t, d = x2.shape
    tb = _tile(t, 256, SUBLANES)

    def body(x_ref, y_ref, tg_ref, w_ref, gt_ref, do_ref, loss_ref, dy_ref, dgt_ref, dw_ref):
        @pl.when(pl.program_id(0) == 0)
        def _():
            loss_ref[...] = jnp.zeros_like(loss_ref)
            dgt_ref[...] = jnp.zeros_like(dgt_ref)
            dw_ref[...] = jnp.zeros_like(dw_ref)

        yv = y_ref[...]
        r = _rms_scale(yv)
        yn = yv * r
        wv, gtv = w_ref[...], gt_ref[...]
        err = x_ref[...] + gtv * (yn * wv) - tg_ref[...]
        g = err * (1.0 / d)
        do_ref[...] = g
        per_tok = jnp.mean(err * err, axis=-1, keepdims=True)
        loss_ref[...] += 0.5 * jnp.sum(per_tok, axis=0, keepdims=True)
        gyn = jnp.sum(g * yn, axis=0, keepdims=True)
        dgt_ref[...] += gyn * wv
        dw_ref[...] += gyn * gtv
        dyn = g * (gtv * wv)
        dy_ref[...] = (r * (dyn - yn * jnp.mean(dyn * yn, axis=-1, keepdims=True))).astype(dy_ref.dtype)

    full = pl.BlockSpec((tb, d), lambda i: (i, 0))
    vec_out = pl.BlockSpec((1, d), lambda i: (0, 0))
    return pl.pallas_call(
        body, name=name, grid=(t // tb,), in_specs=_row_specs(tb, d, 3, 2),
        out_specs=[full, pl.BlockSpec((1, LANES), lambda i: (0, 0)), full, vec_out, vec_out],
        out_shape=[jax.ShapeDtypeStruct((t, d), F32), jax.ShapeDtypeStruct((1, LANES), F32), jax.ShapeDtypeStruct((t, d), BF16),
                   jax.ShapeDtypeStruct((1, d), F32), jax.ShapeDtypeStruct((1, d), F32)],
        compiler_params=_params(("arbitrary",)),
    )(x2, y2, target, w, gt)


def _resid_bwd(dout, y, w, gt, name):
    t, d = y.shape
    tb = _tile(t, 256, SUBLANES)

    def body(do_ref, y_ref, w_ref, gt_ref, dy_ref, dgt_ref, dw_ref):
        @pl.when(pl.program_id(0) == 0)
        def _():
            dgt_ref[...] = jnp.zeros_like(dgt_ref)
            dw_ref[...] = jnp.zeros_like(dw_ref)

        yv, g = y_ref[...], do_ref[...]
        r = _rms_scale(yv)
        yn = yv * r
        wv, gtv = w_ref[...], gt_ref[...]
        gyn = jnp.sum(g * yn, axis=0, keepdims=True)
        dgt_ref[...] += gyn * wv
        dw_ref[...] += gyn * gtv
        dyn = g * (gtv * wv)
        dy_ref[...] = (r * (dyn - yn * jnp.mean(dyn * yn, axis=-1, keepdims=True))).astype(dy_ref.dtype)

    vec_out = pl.BlockSpec((1, d), lambda i: (0, 0))
    return pl.pallas_call(
        body, name=name, grid=(t // tb,), in_specs=_row_specs(tb, d, 2, 2),
        out_specs=[pl.BlockSpec((tb, d), lambda i: (i, 0)), vec_out, vec_out],
        out_shape=[jax.ShapeDtypeStruct((t, d), BF16)] + [jax.ShapeDtypeStruct((1, d), F32)] * 2,
        compiler_params=_params(("arbitrary",)),
    )(dout, y, w, gt)


def _row_half_to_bf16(full, which, sib, name):
    n, r, c = full.shape
    by_rows = r % 32 == 0
    r, c = (r // 2, c) if by_rows else (r, c // 2)
    tr, tc = _elementwise_tiles(r, c)
    nbh = (r // tr) if by_rows else (c // tc)

    def body(which_ref, a_ref, *rest):
        if sib is None:
            rest[0][...] = a_ref[...].astype(BF16)
        else:
            rest[1][...] = (a_ref[...] + rest[0][...].astype(F32)).astype(BF16)

    if by_rows:
        half_spec = pl.BlockSpec((1, tr, tc), lambda j, i, k, which_ref: (j, which_ref[0] * nbh + i, k))
    else:
        half_spec = pl.BlockSpec((1, tr, tc), lambda j, i, k, which_ref: (j, i, which_ref[0] * nbh + k))
    spec = pl.BlockSpec((1, tr, tc), lambda j, i, k, which_ref: (j, i, k))
    grid_spec = pltpu.PrefetchScalarGridSpec(
        num_scalar_prefetch=1, grid=(n, r // tr, c // tc), in_specs=[half_spec] + ([] if sib is None else [spec]), out_specs=spec)
    return pl.pallas_call(
        body, name=name, grid_spec=grid_spec, out_shape=jax.ShapeDtypeStruct((n, r, c), BF16),
        compiler_params=_params(("parallel", "parallel", "parallel")),
    )(which, full, *([] if sib is None else [sib]))


def _sum_chips(recv, name):
    _, r, c = recv.shape
    tr, tc = _elementwise_tiles(r, c)

    def body(x_ref, o_ref):
        acc = x_ref[0].astype(F32)
        for j in range(1, N_CHIP):
            acc = acc + x_ref[j].astype(F32)
        o_ref[...] = acc

    return pl.pallas_call(
        body, name=name, grid=(r // tr, c // tc), in_specs=[pl.BlockSpec((N_CHIP, tr, tc), lambda i, j: (0, i, j))],
        out_specs=pl.BlockSpec((tr, tc), lambda i, j: (i, j)), out_shape=jax.ShapeDtypeStruct((r, c), F32),
        compiler_params=_params(("parallel", "parallel")),
    )(recv)


def _adamw(w, g_parts, m, v, name, by_core=False):
    r, c = w.shape
    by_rows = r % 32 == 0
    if by_core:
        tr, tc = _elementwise_tiles(*((r // 2, c) if by_rows else (r, c // 2)))
        nbh = (r // 2) // tr if by_rows else (c // 2) // tc
    else:
        tr, tc = _elementwise_tiles(r, c)
    n_g = len(g_parts)
    c1 = 1.0 / (1.0 - ADAM_B1 ** ADAM_STEP)
    c2 = 1.0 / (1.0 - ADAM_B2 ** ADAM_STEP)

    def body(*refs):
        w_ref, g_refs, m_ref, v_ref = refs[0], refs[1:1 + n_g], refs[1 + n_g], refs[2 + n_g]
        g_out, d_out, m_out, v_out = refs[3 + n_g:]
        if by_core:
            in_my_half = (pl.program_id(0 if by_rows else 1) // nbh) == lax.axis_index("c")
            g = jnp.where(in_my_half, g_refs[0][...], g_refs[1][...])
        else:
            g = g_refs[0][...]
            for extra in g_refs[1:]:
                g = g + extra[...]
        mn = ADAM_B1 * m_ref[...] + (1.0 - ADAM_B1) * g
        vn = ADAM_B2 * v_ref[...] + (1.0 - ADAM_B2) * (g * g)
        g_out[...] = g
        m_out[...] = mn
        v_out[...] = vn
        d_out[...] = -ADAM_LR * ((mn * c1) / (jnp.sqrt(vn * c2) + ADAM_EPS) + ADAM_WD * w_ref[...])

    spec = pl.BlockSpec((tr, tc), lambda i, j: (i, j))
    if by_core:
        g_spec = pl.BlockSpec((tr, tc), (lambda i, j: (i % nbh, j)) if by_rows else (lambda i, j: (i, j % nbh)))
    else:
        g_spec = spec
    return pl.pallas_call(
        body, name=name, grid=(r // tr, c // tc), in_specs=[spec] + [g_spec] * n_g + [spec] * 2, out_specs=[spec] * 4,
        out_shape=[jax.ShapeDtypeStruct((r, c), F32)] * 4, compiler_params=_params(("parallel", "parallel")),
    )(w, *g_parts, m, v)


def _conv_taps(u, t):
    rows = lax.broadcasted_iota(jnp.int32, u.shape, 0)
    return [u] + [jnp.where(rows >= dd, pltpu.roll(u, dd, 0), 0.0) for dd in range(1, CONV_K)]


def _conv_fwd(proj, conv_w, col0, name):
    t = proj.shape[0]
    ch = conv_w.shape[1]

    def body(u_ref, w_ref, o_ref):
        taps = _conv_taps(u_ref[...], t)
        wv = w_ref[...]
        y = taps[0] * wv[CONV_K - 1:CONV_K]
        for dd in range(1, CONV_K):
            y = y + taps[dd] * wv[CONV_K - 1 - dd:CONV_K - dd]
        o_ref[...] = _silu(y)

    return pl.pallas_call(
        body, name=name, grid=(ch // LANES,),
        in_specs=[pl.BlockSpec((t, LANES), lambda j: (0, col0 + j)), pl.BlockSpec((CONV_K, LANES), lambda j: (0, j))],
        out_specs=pl.BlockSpec((t, LANES), lambda j: (0, j)), out_shape=jax.ShapeDtypeStruct((t, ch), F32),
        compiler_params=_params(("parallel",)),
    )(proj, conv_w)


def _conv_bwd(proj, conv_w, ds, col0, name):
    t = proj.shape[0]
    ch = conv_w.shape[1]

    def body(u_ref, w_ref, ds_ref, du_ref, dw_ref):
        u = u_ref[...]
        taps = _conv_taps(u, t)
        wv = w_ref[...]
        y = taps[0] * wv[CONV_K - 1:CONV_K]
        for dd in range(1, CONV_K):
            y = y + taps[dd] * wv[CONV_K - 1 - dd:CONV_K - dd]
        sg = jax.nn.sigmoid(y)
        dy = ds_ref[...] * (sg * (1.0 + y * (1.0 - sg)))
        rows = lax.broadcasted_iota(jnp.int32, u.shape, 0)
        du = dy * wv[CONV_K - 1:CONV_K]
        for dd in range(1, CONV_K):
            ahead = jnp.where(rows < t - dd, pltpu.roll(dy, t - dd, 0), 0.0)
            du = du + ahead * wv[CONV_K - 1 - dd:CONV_K - dd]
        du_ref[...] = du.astype(du_ref.dtype)
        dws = [jnp.sum(dy * taps[CONV_K - 1 - j], axis=0, keepdims=True) for j in range(CONV_K)]
        dw_ref[...] = jnp.concatenate(dws, axis=0)

    return pl.pallas_call(
        body, name=name, grid=(ch // LANES,),
        in_specs=[pl.BlockSpec((t, LANES), lambda j: (0, col0 + j)), pl.BlockSpec((CONV_K, LANES), lambda j: (0, j)),
                  pl.BlockSpec((t, LANES), lambda j: (0, j))],
        out_specs=[pl.BlockSpec((t, LANES), lambda j: (0, j)), pl.BlockSpec((CONV_K, LANES), lambda j: (0, j))],
        out_shape=[jax.ShapeDtypeStruct((t, ch), BF16), jax.ShapeDtypeStruct((CONV_K, ch), F32)],
        compiler_params=_params(("parallel",)),
    )(proj, conv_w, ds)


def _hg_block(st, q, fl, vi, g, l0, l1, nw):
    hs = range(len(st))
    tb = q[0].shape[0]
    ln = HG_SUB
    lb = [jax.nn.sigmoid(l0[h] - l1[h]) for h in hs]
    rows = lax.broadcasted_iota(jnp.int32, (ln, HEAD), 0)
    tri = (lax.broadcasted_iota(jnp.int32, (ln, ln), 0) >= lax.broadcasted_iota(jnp.int32, (ln, ln), 1)).astype(F32)
    st = list(st)
    outs = [[] for _ in hs]
    for i in range(tb // ln):
        sl = slice(i * ln, (i + 1) * ln)
        qs, vs = [q[h][sl] for h in hs], [vi[h][sl] for h in hs]
        f = [lb[h] + (1.0 - lb[h]) * jax.nn.sigmoid(fl[h][sl]) for h in hs]
        k = [1.0 - f[h] for h in hs]
        b = [jnp.dot(tri, jnp.log(f[h]), precision=HI, preferred_element_type=F32) for h in hs]
        o = [lax.dot_general((qs[h] * jnp.exp(b[h])).astype(BF16), st[h].astype(BF16), NT_DIMS, preferred_element_type=F32)
             for h in hs]
        late = [jnp.zeros((ln - SUBLANES, HEAD), F32) for _ in hs]
        for s in range(ln):
            r0 = 0 if s < SUBLANES else SUBLANES
            e = [jnp.exp(jnp.where(rows[r0:] >= s, b[h][r0:] - b[h][s:s + 1], -1e30)) for h in hs]
            a = [jnp.sum(qs[h][r0:] * e[h] * k[h][s:s + 1], axis=-1, keepdims=True) for h in hs]
            if r0 == 0:
                o = [o[h] + a[h] * vs[h][s:s + 1] for h in hs]
            else:
                late = [late[h] + a[h] * vs[h][s:s + 1] for h in hs]
        o = [jnp.concatenate([o[h][:SUBLANES], o[h][SUBLANES:] + late[h]], axis=0) for h in hs]
        kt = [k[h] * jnp.exp(b[h][ln - 1:ln] - b[h]) for h in hs]
        upd = [lax.dot_general(vs[h].astype(BF16), kt[h].astype(BF16), TN_DIMS, preferred_element_type=F32) for h in hs]
        st = [st[h] * jnp.exp(b[h][ln - 1:ln]) + upd[h] for h in hs]
        for h in hs:
            outs[h].append(o[h])
    o = [jnp.concatenate(outs[h], axis=0) for h in hs]
    out = [o[h] * _rms_scale(o[h]) * nw * _silu(g[h]) for h in hs]
    return st, out


def _head_cols(h):
    return slice(h * HEAD, (h + 1) * HEAD)


def _head_groups(n_heads, group):
    g = min(group, n_heads)
    return [list(range(i, min(i + g, n_heads))) for i in range(0, n_heads, g)]


def _hg_in_specs(n_heads, tb, time_index):
    hw = n_heads * HEAD
    cols = [pl.BlockSpec((tb, hw), functools.partial(lambda part, j: (time_index(j), part), part)) for part in range(4)]
    head_rows = pl.BlockSpec((n_heads, 1, HEAD), lambda j: (0, 0, 0))
    return cols + [head_rows, head_rows, pl.BlockSpec((1, HEAD), lambda j: (0, 0))]


def _hgrn2_fwd(proj, l0, l1, nw, n_heads, name):
    t = proj.shape[0]
    hw = n_heads * HEAD
    tb = _tile(t, HG_BLOCK, HG_SUB)
    nb = t // tb

    def body(q_ref, f_ref, i_ref, g_ref, l0_ref, l1_ref, nw_ref, o_ref, save_ref, st_ref):
        @pl.when(pl.program_id(0) == 0)
        def _():
            st_ref[...] = jnp.zeros_like(st_ref)

        for hs in _head_groups(n_heads, HG_FWD_GROUP):
            st = [st_ref[h] for h in hs]
            for h, s in zip(hs, st):
                save_ref[h] = s
            st, out = _hg_block(st, *[[r[:, _head_cols(h)] for h in hs] for r in (q_ref, f_ref, i_ref, g_ref)],
                                [l0_ref[h] for h in hs], [l1_ref[h] for h in hs], nw_ref[...])
            for h, s, o in zip(hs, st, out):
                st_ref[h] = s
                o_ref[:, _head_cols(h)] = o.astype(o_ref.dtype)

    return pl.pallas_call(
        body, name=name, grid=(nb,), in_specs=_hg_in_specs(n_heads, tb, lambda j: j),
        out_specs=[pl.BlockSpec((tb, hw), lambda j: (j, 0)),
                   pl.BlockSpec((None, n_heads, HEAD, HEAD), lambda j: (j, 0, 0, 0))],
        out_shape=[jax.ShapeDtypeStruct((t, hw), BF16), jax.ShapeDtypeStruct((nb, n_heads, HEAD, HEAD), F32)],
        scratch_shapes=[pltpu.VMEM((n_heads, HEAD, HEAD), F32)], compiler_params=_params(("arbitrary",)),
    )(proj, proj, proj, proj, l0, l1, nw)


def _hgrn2_bwd(proj, l0, l1, nw, saved, d_ocat, n_heads, name):
    t = proj.shape[0]
    tb = _tile(t, HG_BLOCK, HG_SUB)
    nb = t // tb
    rev = lambda j: nb - 1 - j

    hw = n_heads * HEAD

    def body(q_ref, f_ref, i_ref, g_ref, l0_ref, l1_ref, nw_ref, save_ref, do_ref,
             dp_ref, dl0_ref, dl1_ref, dnw_ref, dst_ref):
        @pl.when(pl.program_id(0) == 0)
        def _():
            dst_ref[...] = jnp.zeros_like(dst_ref)
            dl0_ref[...] = jnp.zeros_like(dl0_ref)
            dl1_ref[...] = jnp.zeros_like(dl1_ref)
            dnw_ref[...] = jnp.zeros_like(dnw_ref)

        dnw_acc = jnp.zeros((1, HEAD), F32)
        for hs in _head_groups(n_heads, HG_BWD_GROUP):
            _, vjp = jax.vjp(_hg_block, [save_ref[h] for h in hs],
                             *[[r[:, _head_cols(h)] for h in hs] for r in (q_ref, f_ref, i_ref, g_ref)],
                             [l0_ref[h] for h in hs], [l1_ref[h] for h in hs], nw_ref[...])
            dst, dq, df, di, dg, dl0, dl1, dnw = vjp(([dst_ref[h] for h in hs], [do_ref[:, _head_cols(h)] for h in hs]))
            for i, h in enumerate(hs):
                dst_ref[h] = dst[i]
                for part, val in enumerate((dq, df, di, dg)):
                    dp_ref[:, part * hw + h * HEAD:part * hw + (h + 1) * HEAD] = val[i].astype(dp_ref.dtype)
                dl0_ref[h] += dl0[i]
                dl1_ref[h] += dl1[i]
            dnw_acc = dnw_acc + dnw
        dnw_ref[...] += dnw_acc

    head_rows = pl.BlockSpec((n_heads, 1, HEAD), lambda j: (0, 0, 0))
    return pl.pallas_call(
        body, name=name, grid=(nb,),
        in_specs=_hg_in_specs(n_heads, tb, rev) + [pl.BlockSpec((None, n_heads, HEAD, HEAD), lambda j: (rev(j), 0, 0, 0)),
                                                   pl.BlockSpec((tb, hw), lambda j: (rev(j), 0))],
        out_specs=[pl.BlockSpec((tb, 4 * hw), lambda j: (rev(j), 0)), head_rows, head_rows,
                   pl.BlockSpec((1, HEAD), lambda j: (0, 0))],
        out_shape=[jax.ShapeDtypeStruct((t, 4 * hw), BF16)] + [jax.ShapeDtypeStruct((n_heads, 1, HEAD), F32)] * 2
        + [jax.ShapeDtypeStruct((1, HEAD), F32)],
        scratch_shapes=[pltpu.VMEM((n_heads, HEAD, HEAD), F32)], compiler_params=_params(("arbitrary",)),
    )(proj, proj, proj, proj, l0, l1, nw, saved, d_ocat)


NN_DIMS = (((1,), (0,)), ((), ()))


def _split_bf16(x):
    hi = x.astype(BF16)
    return hi, (x - hi.astype(F32)).astype(BF16)


def _dot3(a, b, dims=NN_DIMS):
    (ah, al), (bh, bl) = _split_bf16(a), _split_bf16(b)
    dot = functools.partial(lax.dot_general, dimension_numbers=dims, preferred_element_type=F32)
    return dot(ah, bh) + dot(ah, bl) + dot(al, bh)


@jax.custom_vjp
def _mm3(a, b):
    return _dot3(a, b)


def _mm3_fwd(a, b):
    return _dot3(a, b), (a, b)


def _mm3_bwd(res, g):
    a, b = res
    return _dot3(g, b, NT_DIMS), _dot3(a, g, TN_DIMS)


_mm3.defvjp(_mm3_fwd, _mm3_bwd)


def _dot_bf16(a, b, dims=(((1,), (0,)), ((), ()))):
    return lax.dot_general(a.astype(BF16), b.astype(BF16), dims, preferred_element_type=F32)


def _inv_unit_lower_raw(ms):
    hs = range(len(ms))
    c = ms[0].shape[0]
    r = lax.broadcasted_iota(jnp.int32, (c, c), 0)
    q = lax.broadcasted_iota(jnp.int32, (c, c), 1)
    eye = (r == q).astype(F32)
    md = [jnp.where((r // GDN_INV_BLOCK) == (q // GDN_INV_BLOCK), ms[h], 0.0) for h in hs]
    p = [-md[h] for h in hs]
    t16 = [eye + p[h] for h in hs]
    for _ in range(int(math.log2(GDN_INV_BLOCK)) - 1):
        p = [_dot3(p[h], p[h]) for h in hs]
        t16 = [t16[h] + _dot3(t16[h], p[h]) for h in hs]
    p = [-_dot3(t16[h], ms[h] - md[h]) for h in hs]
    t2 = [eye + p[h] for h in hs]
    for _ in range(int(math.log2(c // GDN_INV_BLOCK)) - 1):
        p = [_dot3(p[h], p[h]) for h in hs]
        t2 = [t2[h] + _dot3(t2[h], p[h]) for h in hs]
    return [_dot3(t2[h], t16[h]) for h in hs]


@jax.custom_vjp
def _inv_unit_lower(ms):
    return _inv_unit_lower_raw(ms)


def _inv_fwd(ms):
    ts = _inv_unit_lower_raw(ms)
    return ts, ts


def _inv_bwd(ts, dts):
    hs = range(len(ts))
    inner = [_dot3(ts[h], dts[h], TN_DIMS) for h in hs]
    return ([-_dot3(inner[h], ts[h], NT_DIMS) for h in hs],)


_inv_unit_lower.defvjp(_inv_fwd, _inv_bwd)


def _gdn_block(precise, onehots, st, qc, kc, vc, g, ab, alog_row, dtb_row, nw):
    inverse, dot3 = precise
    hs = range(len(st))
    c = qc[0].shape[0]
    lane_sum = lambda v: jnp.sum(v, axis=-1, keepdims=True)
    a = [lane_sum(ab * onehots[h][0]) for h in hs]
    bb = [lane_sum(ab * onehots[h][1]) for h in hs]
    alog = [lane_sum(alog_row * onehots[h][0]) for h in hs]
    dtb = [lane_sum(dtb_row * onehots[h][0]) for h in hs]
    la = [-jnp.exp(alog[h]) * _softplus(a[h] + dtb[h]) for h in hs]
    beta = [jax.nn.sigmoid(bb[h]) for h in hs]
    q = [qc[h] * lax.rsqrt(lane_sum(qc[h] * qc[h]) + EPS) * (HEAD ** -0.5) for h in hs]
    k = [kc[h] * lax.rsqrt(lane_sum(kc[h] * kc[h]) + EPS) for h in hs]
    r = lax.broadcasted_iota(jnp.int32, (c, c), 0)
    s = lax.broadcasted_iota(jnp.int32, (c, c), 1)
    tri = (r >= s).astype(F32)
    g_cc = [dot3(tri, jnp.broadcast_to(la[h], (c, c))) for h in hs]
    g_cl = [dot3(tri, jnp.broadcast_to(la[h], (c, HEAD))) for h in hs]
    gamma = [jnp.exp(jnp.where(r >= s, g_cc[h] - g_cc[h].T, -1e30)) for h in hs]
    kk = [_dot_bf16(k[h], k[h], NT_DIMS) for h in hs]
    m = [jnp.where(r > s, beta[h] * kk[h] * gamma[h], 0.0) for h in hs]
    tm = inverse(m)
    eg = [jnp.exp(g_cl[h]) for h in hs]
    rhs = [jnp.concatenate([vc[h] * beta[h], k[h] * (beta[h] * eg[h])], axis=1) for h in hs]
    sol = [dot3(tm[h], rhs[h]) for h in hs]
    qk = [_dot_bf16(q[h], k[h], NT_DIMS) * gamma[h] for h in hs]
    g_last = [g_cl[h][c - 1:c] for h in hs]
    k_tail = [k[h] * jnp.exp(g_last[h] - g_cl[h]) for h in hs]
    v_new = [sol[h][:, :HEAD] - _dot_bf16(sol[h][:, HEAD:], st[h], NT_DIMS) for h in hs]
    o_st = [_dot_bf16(q[h] * eg[h], st[h], NT_DIMS) for h in hs]
    o = [o_st[h] + _dot_bf16(qk[h], v_new[h]) for h in hs]
    upd = [_dot_bf16(v_new[h], k_tail[h], TN_DIMS) for h in hs]
    st = [st[h] * jnp.exp(g_last[h]) + upd[h] for h in hs]
    out = [o[h] * _rms_scale(o[h]) * nw * _silu(g[h]) for h in hs]
    return st, out


def _head_onehots(n_heads, h):
    lane = lax.broadcasted_iota(jnp.int32, (1, LANES), 1)
    return (lane == h).astype(F32), (lane == n_heads + h).astype(F32)


def _gdn_in_specs(n_heads, c, time_index):
    hw = n_heads * HEAD
    qkv = [pl.BlockSpec((c, hw), functools.partial(lambda part, j: (time_index(j), part), part)) for part in range(3)]
    row = pl.BlockSpec((1, LANES), lambda j: (0, 0))
    return qkv + [pl.BlockSpec((c, hw), lambda j: (time_index(j), 7)),
                  pl.BlockSpec((c, LANES), lambda j: (time_index(j), 8 * n_heads)), row, row, row]


def _gdn_fwd(qkv, proj, alog_row, dtb_row, nw, n_heads, name):
    t = qkv.shape[0]
    hw = n_heads * HEAD
    c = _tile(t, GDN_CHUNK, GDN_CHUNK)
    nb = t // c

    def body(q_ref, k_ref, v_ref, g_ref, ab_ref, al_ref, dt_ref, nw_ref, o_ref, save_ref, st_ref):
        @pl.when(pl.program_id(0) == 0)
        def _():
            st_ref[...] = jnp.zeros_like(st_ref)

        for hs in _head_groups(n_heads, GDN_FWD_GROUP):
            st = [st_ref[h] for h in hs]
            for h, s in zip(hs, st):
                save_ref[h] = s
            st, out = _gdn_block((_inv_unit_lower_raw, _dot3), [_head_onehots(n_heads, h) for h in hs], st,
                                 *[[r[:, _head_cols(h)] for h in hs] for r in (q_ref, k_ref, v_ref, g_ref)],
                                 ab_ref[...], al_ref[...], dt_ref[...], nw_ref[...])
            for h, s, o in zip(hs, st, out):
                st_ref[h] = s
                o_ref[:, _head_cols(h)] = o.astype(o_ref.dtype)

    return pl.pallas_call(
        body, name=name, grid=(nb,), in_specs=_gdn_in_specs(n_heads, c, lambda j: j),
        out_specs=[pl.BlockSpec((c, hw), lambda j: (j, 0)),
                   pl.BlockSpec((None, n_heads, HEAD, HEAD), lambda j: (j, 0, 0, 0))],
        out_shape=[jax.ShapeDtypeStruct((t, hw), BF16), jax.ShapeDtypeStruct((nb, n_heads, HEAD, HEAD), F32)],
        scratch_shapes=[pltpu.VMEM((n_heads, HEAD, HEAD), F32)], compiler_params=_params(("arbitrary",)),
    )(qkv, qkv, qkv, proj, proj, alog_row, dtb_row, nw)


def _gdn_bwd(qkv, proj, alog_row, dtb_row, nw, saved, d_ocat, n_heads, name):
    t = qkv.shape[0]
    c = _tile(t, GDN_CHUNK, GDN_CHUNK)
    nb = t // c
    rev = lambda j: nb - 1 - j

    hw = n_heads * HEAD

    def body(q_ref, k_ref, v_ref, g_ref, ab_ref, al_ref, dt_ref, nw_ref, save_ref, do_ref,
             dqkv_ref, dg_ref, dab_ref, dal_ref, ddt_ref, dnw_ref, dst_ref):
        @pl.when(pl.program_id(0) == 0)
        def _():
            dst_ref[...] = jnp.zeros_like(dst_ref)
            dal_ref[...] = jnp.zeros_like(dal_ref)
            ddt_ref[...] = jnp.zeros_like(ddt_ref)
            dnw_ref[...] = jnp.zeros_like(dnw_ref)

        dab_acc = jnp.zeros((c, LANES), F32)
        row_acc = [jnp.zeros((1, LANES), F32)] * 3
        for hs in _head_groups(n_heads, GDN_BWD_GROUP):
            fn = functools.partial(_gdn_block, (_inv_unit_lower, _mm3), [_head_onehots(n_heads, h) for h in hs])
            _, vjp = jax.vjp(fn, [save_ref[h] for h in hs],
                             *[[r[:, _head_cols(h)] for h in hs] for r in (q_ref, k_ref, v_ref, g_ref)],
                             ab_ref[...], al_ref[...], dt_ref[...], nw_ref[...])
            dst, dq, dk, dv, dg, dab, dal, ddt, dnw = vjp(([dst_ref[h] for h in hs], [do_ref[:, _head_cols(h)] for h in hs]))
            for i, h in enumerate(hs):
                dst_ref[h] = dst[i]
                for part, val in enumerate((dq, dk, dv)):
                    dqkv_ref[:, part * hw + h * HEAD:part * hw + (h + 1) * HEAD] = val[i]
                dg_ref[:, _head_cols(h)] = dg[i].astype(dg_ref.dtype)
            dab_acc = dab_acc + dab
            row_acc = [acc + val for acc, val in zip(row_acc, (dal, ddt, dnw))]
        dab_ref[...] = dab_acc
        dal_ref[...] += row_acc[0]
        ddt_ref[...] += row_acc[1]
        dnw_ref[...] += row_acc[2]

    row = pl.BlockSpec((1, LANES), lambda j: (0, 0))
    return pl.pallas_call(
        body, name=name, grid=(nb,),
        in_specs=_gdn_in_specs(n_heads, c, rev) + [pl.BlockSpec((None, n_heads, HEAD, HEAD), lambda j: (rev(j), 0, 0, 0)),
                                                   pl.BlockSpec((c, hw), lambda j: (rev(j), 1))],
        out_specs=[pl.BlockSpec((c, 3 * hw), lambda j: (rev(j), 0)), pl.BlockSpec((c, hw), lambda j: (rev(j), 0)),
                   pl.BlockSpec((c, LANES), lambda j: (rev(j), 0)), row, row, row],
        out_shape=[jax.ShapeDtypeStruct((t, 3 * hw), F32), jax.ShapeDtypeStruct((t, hw), BF16),
                   jax.ShapeDtypeStruct((t, LANES), F32)] + [jax.ShapeDtypeStruct((1, LANES), F32)] * 3,
        scratch_shapes=[pltpu.VMEM((n_heads, HEAD, HEAD), F32)], compiler_params=_params(("arbitrary",)),
    )(qkv, qkv, qkv, proj, proj, alog_row, dtb_row, nw, saved, d_ocat)


def _pad_lanes(v, n):
    v = v.reshape(1, -1)
    return jnp.pad(v, ((0, 0), (0, n - v.shape[1])))


def _pack_rows(vecs):
    flat = jnp.concatenate([v.reshape(-1) for v in vecs])
    offs, o = [], 0
    for v in vecs:
        offs.append((o, v.size))
        o += v.size
    per_row = -(-o // (SUBLANES * LANES)) * LANES
    flat = jnp.pad(flat, (0, SUBLANES * per_row - o))
    return flat.reshape(SUBLANES, per_row), offs


def _unpack(gathered, offs):
    per_dev = gathered.reshape(N_DEV, -1)
    return [per_dev[:, o:o + n] for o, n in offs]


def _sum_devices(part):
    acc = part[0]
    for i in range(1, N_DEV):
        acc = acc + part[i]
    return acc


def kernel(x, c, w_ada, b_ada, pre_mix_norm, post_mix_norm, pre_ffn_norm, post_ffn_norm, w_in, hg_lb_logits, hg_norm, gdn_conv_w, gdn_a_log, gdn_dt_bias, gdn_norm, w_out, w_ff1, w_ff2, loss_target, m_w_ada, m_b_ada, m_pre_mix_norm, m_post_mix_norm, m_pre_ffn_norm, m_post_ffn_norm, m_w_in, m_hg_lb_logits, m_hg_norm, m_gdn_conv_w, m_gdn_a_log, m_gdn_dt_bias, m_gdn_norm, m_w_out, m_w_ff1, m_w_ff2, v_w_ada, v_b_ada, v_pre_mix_norm, v_post_mix_norm, v_pre_ffn_norm, v_post_ffn_norm, v_w_in, v_hg_lb_logits, v_hg_norm, v_gdn_conv_w, v_gdn_a_log, v_gdn_dt_bias, v_gdn_norm, v_w_out, v_w_ff1, v_w_ff2):
    assert x.shape[0] == 1 and w_ada.shape[0] == 1 and hg_lb_logits.shape[0] == 2
    t, d = x.shape[1], x.shape[2]
    n_heads = (d // 2) // HEAD
    hw = n_heads * HEAD
    in_cols = 8 * hw + 2 * n_heads
    np_cols = 8 * hw + 2 * LANES
    d_ff = w_ff1.shape[2] * N_CHIP
    na = w_ada.shape[2]
    ax, ay, ac = lax.axis_index("x"), lax.axis_index("y"), lax.axis_index("c")
    chip = 2 * ax + ay
    dev = 4 * ax + 2 * ay + ac

    x2d, tgt = x[0], loss_target[0]

    pack1, offs1 = _pack_rows([c[0], gdn_conv_w[0]])
    c_all, convw_all = _unpack(_gather8(pack1, "gather_cond"), offs1)
    conv_sh = gdn_conv_w.shape[2]
    conv_w = jnp.concatenate([convw_all[2 * j].reshape(CONV_K, conv_sh) for j in range(N_CHIP)], axis=1)

    b_s = lax.dynamic_slice(b_ada, (0, chip * na), (1, na))
    mod_part = _mod_part(c_all, w_ada[0], b_s, "mod_part")
    pack2, offs2 = _pack_rows([mod_part])
    (mod_parts,) = _unpack(_gather8(pack2, "gather_mod"), offs2)
    mod_all = jnp.concatenate([mod_parts[2 * j].reshape(N_DEV, na) for j in range(N_CHIP)], axis=1)
    mod = lax.dynamic_slice(mod_all, (dev, 0), (1, N_MOD * d))
    sh_m, sc_m, gt_m, sh_f, sc_f, gt_f = [mod[:, i * d:(i + 1) * d] for i in range(N_MOD)]

    h1 = _norm_mod(x2d, pre_mix_norm, sc_m, sh_m, "norm_mod_mix")
    (g_in,), h1 = _gather_weights([jnp.transpose(w_in[0]).astype(BF16)], "gather_w_in", sequencer_id=5, after=h1)
    late, g_in = lax.optimization_barrier(([w_out[0].astype(BF16), w_ff1[0].astype(BF16), w_ff2[0].astype(BF16)], g_in))
    g_out, g_ff1, g_ff2 = _gather_weights(late, "gather_weights_late", sequencer_id=1)
    w_in_f = jnp.pad(g_in.reshape(in_cols, d), ((0, np_cols - in_cols), (0, 0)))
    w_out_f = g_out.reshape(d, d)
    w_ff2_f = g_ff2.reshape(d_ff, d)

    proj = _matmul(h1, w_in_f, "nt", F32, "mm_in", tn=768)
    l0, l1 = hg_lb_logits[0].reshape(n_heads, 1, HEAD), hg_lb_logits[1].reshape(n_heads, 1, HEAD)
    o_hg, hg_saved = _hgrn2_fwd(proj, l0, l1, hg_norm, n_heads, "hgrn2_fwd")
    qkv = _conv_fwd(proj, conv_w, 4 * n_heads, "conv_fwd")
    alog_row, dtb_row = _pad_lanes(gdn_a_log, LANES), _pad_lanes(gdn_dt_bias, LANES)
    o_gdn, gdn_saved = _gdn_fwd(qkv, proj, alog_row, dtb_row, gdn_norm, n_heads, "gdn_fwd")
    o_cat = jnp.concatenate([o_hg, o_gdn], axis=1)
    y1 = _matmul(o_cat, w_out_f, "nn", F32, "mm_out")
    x_mid, h2 = _resid_norm_mod(x2d, y1, post_mix_norm, gt_m, pre_ffn_norm, sc_f, sh_f, "resid_mix_norm_mod_ffn")

    relu_a1, r1 = _matmul(h2, g_ff1, "nn", BF16, "mm_ff1", relu2=True, b_split=True)
    y2 = _matmul(r1, w_ff2_f, "nn", F32, "mm_ff2")
    d_out, loss_row, dy2, d_gt_f, d_post_ffn = _loss_head(x_mid, y2, post_ffn_norm, gt_f, tgt, "loss_head")

    in_sh = in_cols // N_CHIP
    ff_sh = d_ff // N_CHIP
    my_half = jnp.reshape(ac, (1,)).astype(jnp.int32)

    def start_reduce(by_chip, tag, collective_id):
        to_sib = [_row_half_to_bf16(a, 1 - my_half, None, f"sibling_half_{tag}{i}") for i, a in enumerate(by_chip)]
        from_sib = _sibling_exchange(to_sib, f"sibling_partials_{tag}")
        chip_part = [_row_half_to_bf16(a, my_half, s, f"add_halves_{tag}{i}") for i, (a, s) in enumerate(zip(by_chip, from_sib))]
        return _sequencer_chip_exchange(chip_part, f"scatter_grads_{tag}", collective_id)

    gw_ff2 = _matmul(r1, dy2, "tn", BF16, "mm_ff2_dw")
    gw_ff2, dy2 = lax.optimization_barrier((gw_ff2, dy2))
    da1 = _matmul(dy2, w_ff2_f, "nt", BF16, "mm_ff2_dx", times=relu_a1)
    gw_ff1 = _matmul(h2, da1, "tn", BF16, "mm_ff1_dw", out_split=True)
    gw_ff1, da1 = lax.optimization_barrier((gw_ff1, da1))
    recv_ff2, recv_ff1 = _sequencer_chip_exchange([gw_ff2.reshape(N_CHIP, ff_sh, d), gw_ff1], "scatter_grads_ff", 2)
    dh2 = _matmul(da1, g_ff1, "nt", BF16, "mm_ff1_dx", b_split=True)
    d_mid, d_pre_ffn, d_sc_f, d_sh_f = _norm_mod_bwd(x_mid, pre_ffn_norm, sc_f, dh2, d_out, "norm_mod_ffn_bwd")

    dy1, d_gt_m, d_post_mix = _resid_bwd(d_mid, y1, post_mix_norm, gt_m, "resid_mix_bwd")
    gw_out = _matmul(o_cat, dy1, "tn", BF16, "mm_out_dw")
    gw_out, dy1 = lax.optimization_barrier((gw_out, dy1))
    (recv_out,) = _sequencer_chip_exchange([gw_out.reshape(N_CHIP, d // N_CHIP, d)], "scatter_grads_out", 3)
    d_ocat = _matmul(dy1, w_out_f, "nt", F32, "mm_out_dx")
    dp_hg, dl0, dl1, d_hg_norm = _hgrn2_bwd(proj, l0, l1, hg_norm, hg_saved, d_ocat, n_heads, "hgrn2_bwd")
    dqkv, dg_g, dab, d_alog, d_dtb, d_gdn_norm = _gdn_bwd(
        qkv, proj, alog_row, dtb_row, gdn_norm, gdn_saved, d_ocat, n_heads, "gdn_bwd")
    du, d_conv_w = _conv_bwd(proj, conv_w, dqkv, 4 * n_heads, "conv_bwd")
    dproj = jnp.concatenate([dp_hg, du, dg_g, dab.astype(BF16), jnp.zeros((t, LANES), BF16)], axis=1)
    gw_in = _matmul(dproj, h1, "tn", F32, "mm_in_dw", tm=768)
    (recv_in,) = start_reduce([gw_in[:in_cols].reshape(N_CHIP, in_sh, d)], "in", 4)
    dh1 = _matmul(dproj, w_in_f, "nn", BF16, "mm_in_dx", tk=2816)
    grad_x, d_pre_mix, d_sc_m, d_sh_m = _norm_mod_bwd(x2d, pre_mix_norm, sc_m, dh1, d_mid, "norm_mod_mix_bwd")

    d_mod = jnp.concatenate([d_sh_m, d_sc_m, d_gt_m, d_sh_f, d_sc_f, d_gt_f], axis=1)
    d_lb_logits = jnp.stack([dl0.reshape(n_heads, HEAD), dl1.reshape(n_heads, HEAD)])
    pack3, offs3 = _pack_rows([loss_row[0, :1], d_pre_mix, d_post_mix, d_pre_ffn, d_post_ffn, d_lb_logits, d_hg_norm,
                               d_conv_w, d_alog[0, :n_heads], d_dtb[0, :n_heads], d_gdn_norm, d_mod])
    parts = _unpack(_gather8(pack3, "gather_vec_grads"), offs3)
    sums = [_sum_devices(p) for p in parts[:-1]]
    loss = sums[0][0]
    dmod_all = parts[-1]
    g_b_ada = _sum_devices(dmod_all).reshape(1, N_MOD * d)
    g_conv_full = sums[7].reshape(CONV_K, N_CHIP * conv_sh)
    g_conv = lax.dynamic_slice(g_conv_full, (0, chip * conv_sh), (CONV_K, conv_sh))
    gw_ada = _wada_grad(c_all, lax.dynamic_slice(dmod_all, (0, chip * na), (N_DEV, na)), "wada_grad")

    sum_ff2 = _sum_chips(recv_ff2, "sum_chips_ff2")
    recv_ff1, sum_ff2 = lax.optimization_barrier((recv_ff1, sum_ff2))
    sum_ff1 = _sum_chips(recv_ff1, "sum_chips_ff1")
    recv_out, sum_ff1 = lax.optimization_barrier((recv_out, sum_ff1))
    sum_out = _sum_chips(recv_out, "sum_chips_out")
    recv_in, sum_out = lax.optimization_barrier((recv_in, sum_out))
    mine = [_sum_chips(recv_in, "sum_chips_in"), sum_out, sum_ff1, sum_ff2]
    theirs = _sibling_exchange(mine, "sibling_grads")

    big = {}
    for i, (nm, w_, m_, v_) in enumerate([("w_in", w_in, m_w_in, v_w_in), ("w_out", w_out, m_w_out, v_w_out),
                                          ("w_ff1", w_ff1, m_w_ff1, v_w_ff1), ("w_ff2", w_ff2, m_w_ff2, v_w_ff2)]):
        if nm == "w_in":
            res_t = _adamw(jnp.transpose(w_[0]), [mine[i], theirs[i]], jnp.transpose(m_[0]), jnp.transpose(v_[0]),
                           f"adamw_{nm}", by_core=True)
            big[nm] = [jnp.transpose(o)[None] for o in res_t]
        else:
            big[nm] = [o[None] for o in _adamw(w_[0], [mine[i], theirs[i]], m_[0], v_[0], f"adamw_{nm}")]
    big["w_ada"] = [o[None] for o in _adamw(w_ada[0], [gw_ada], m_w_ada[0], v_w_ada[0], "adamw_w_ada")]

    small_names = ["b_ada", "pre_mix_norm", "post_mix_norm", "pre_ffn_norm", "post_ffn_norm", "hg_lb_logits", "hg_norm",
                   "gdn_conv_w", "gdn_a_log", "gdn_dt_bias", "gdn_norm"]
    small_w = [b_ada, pre_mix_norm, post_mix_norm, pre_ffn_norm, post_ffn_norm, hg_lb_logits, hg_norm, gdn_conv_w,
               gdn_a_log, gdn_dt_bias, gdn_norm]
    small_m = [m_b_ada, m_pre_mix_norm, m_post_mix_norm, m_pre_ffn_norm, m_post_ffn_norm, m_hg_lb_logits, m_hg_norm,
               m_gdn_conv_w, m_gdn_a_log, m_gdn_dt_bias, m_gdn_norm]
    small_v = [v_b_ada, v_pre_mix_norm, v_post_mix_norm, v_pre_ffn_norm, v_post_ffn_norm, v_hg_lb_logits, v_hg_norm,
               v_gdn_conv_w, v_gdn_a_log, v_gdn_dt_bias, v_gdn_norm]
    small_g = [g_b_ada, sums[1], sums[2], sums[3], sums[4], sums[5], sums[6], g_conv, sums[8], sums[9], sums[10]]
    pw, offs_s = _pack_rows(small_w)
    pg, _ = _pack_rows(small_g)
    pm, _ = _pack_rows(small_m)
    pv, _ = _pack_rows(small_v)
    packed = _adamw(pw, [pg], pm, pv, "adamw_vectors")
    small = {}
    for nm, w_, (o, n) in zip(small_names, small_w, offs_s):
        small[nm] = [p.reshape(-1)[o:o + n].reshape(w_.shape) for p in packed]

    order = ["w_ada", "b_ada", "pre_mix_norm", "post_mix_norm", "pre_ffn_norm", "post_ffn_norm", "w_in", "hg_lb_logits",
             "hg_norm", "gdn_conv_w", "gdn_a_log", "gdn_dt_bias", "gdn_norm", "w_out", "w_ff1", "w_ff2"]
    res = {**big, **small}
    outs = [loss, grad_x[None]]
    for k in range(4):
        outs += [res[nm][k] for nm in order]
    return tuple(outs)
```

```python
import functools
import math

import jax
import jax.numpy as jnp
from jax import lax
from jax.experimental import pallas as pl
from jax.experimental.pallas import tpu as pltpu
from jax.experimental.pallas import tpu_sc as plsc

F32 = jnp.float32
BF16 = jnp.bfloat16
HI = lax.Precision.HIGHEST
MESH = pl.DeviceIdType.MESH

LANES = 128
SUBLANES = 8
VMEM_LIMIT = 48 * 1024 * 1024
EPS = 1e-6
HEAD = 128
CONV_K = 4
GDN_CHUNK = 64
GDN_INV_BLOCK = 16
HG_SUB = 16
HG_BLOCK = 128
HG_FWD_GROUP = 8
HG_BWD_GROUP = 4
GDN_FWD_GROUP = 8
GDN_BWD_GROUP = 8
N_MOD = 6
N_DEV = 8
N_CHIP = 4

ADAM_LR = 0.001
ADAM_B1 = 0.9
ADAM_B2 = 0.999
ADAM_EPS = 1e-08
ADAM_WD = 0.01
ADAM_STEP = 10

NT_DIMS = (((1,), (1,)), ((), ()))
TN_DIMS = (((0,), (0,)), ((), ()))


def _tile(dim, target, align):
    if dim <= target:
        return dim
    best = dim
    t = align
    while t <= target:
        if dim % t == 0:
            best = t
        t += align
    return best


def _elementwise_tiles(r, c):
    tc = _tile(c, 1024, LANES)
    tr = _tile(r, max(16, (256 * 1024) // tc // 16 * 16), 16)
    if tr == r and r * tc > 512 * 1024:
        tc = _tile(c, max(LANES, (256 * 1024) // r // LANES * LANES), LANES)
    return tr, tc


def _params(sem):
    return pltpu.CompilerParams(dimension_semantics=sem, vmem_limit_bytes=VMEM_LIMIT)


def _silu(x):
    return x * jax.nn.sigmoid(x)


def _softplus(x):
    pos = x > 0
    return jnp.where(pos, x, 0.0) + jnp.log(1.0 + jnp.exp(jnp.where(pos, -x, x)))


def _rms_scale(x):
    return lax.rsqrt(jnp.mean(x * x, axis=-1, keepdims=True) + EPS)


def _gather8(x_shard, name):
    m_per, n = x_shard.shape
    assert m_per % SUBLANES == 0 and n % LANES == 0

    def body(x_ref, out_ref, send_sems, recv_sems, local_sem):
        x, y, c = lax.axis_index("x"), lax.axis_index("y"), lax.axis_index("c")
        me, sibling = (x, y, c), (x, y, 1 - c)
        chips = [(1 - x, y), (x, 1 - y), (1 - x, 1 - y)]

        def rows(px, py, pc):
            return out_ref.at[pl.ds((4 * px + 2 * py + pc) * m_per, m_per), :]

        def copy(k, block, to, src=None):
            return pltpu.make_async_remote_copy(
                src_ref=rows(*block) if src is None else src, dst_ref=rows(*block),
                send_sem=send_sems.at[k], recv_sem=recv_sems.at[k], device_id=to, device_id_type=MESH)

        mine = pltpu.make_async_copy(x_ref, rows(*me), local_sem)
        mine.start()
        first = [copy(0, me, sibling, src=x_ref)]
        first += [copy(1 + j, me, (*chip, c), src=x_ref) for j, chip in enumerate(chips)]
        for cp in first:
            cp.start()
        passed = [copy(4 + j, (*chip, c), sibling) for j, chip in enumerate(chips)]
        for j, chip in enumerate(chips):
            copy(1 + j, (*chip, c), me).wait_recv()
            passed[j].start()
        copy(0, sibling, me).wait_recv()
        for j, chip in enumerate(chips):
            copy(4 + j, (*chip, 1 - c), me).wait_recv()
        for cp in first + passed:
            cp.wait_send()
        mine.wait()

    return pl.pallas_call(
        body, name=name,
        out_shape=jax.ShapeDtypeStruct((N_DEV * m_per, n), x_shard.dtype),
        in_specs=[pl.BlockSpec(memory_space=pltpu.VMEM)],
        out_specs=pl.BlockSpec(memory_space=pltpu.VMEM),
        scratch_shapes=[pltpu.SemaphoreType.DMA((7,)), pltpu.SemaphoreType.DMA((7,)), pltpu.SemaphoreType.DMA],
        compiler_params=pltpu.CompilerParams(vmem_limit_bytes=VMEM_LIMIT),
    )(x_shard)


def _gather_weights(arrs, name, sequencer_id=None, after=None):
    n = len(arrs)
    out_shapes = [jax.ShapeDtypeStruct((N_CHIP,) + a.shape, a.dtype) for a in arrs]

    def body(*refs):
        ins, outs = refs[:n], refs[n:2 * n]
        ici_send, ici_recv, d2d_send, d2d_recv = refs[2 * n:]
        if sequencer_id is not None:
            chips, sib = _chip_peers()
            _handshake(chips + [sib])
        x, y, c = lax.axis_index("x"), lax.axis_index("y"), lax.axis_index("c")
        me = 2 * x + y
        peers = [(1 - x, y), (x, 1 - y), (1 - x, 1 - y)]

        def half(a, cc):
            r = arrs[a].shape[0]
            cut = r // 32 * 16
            return pl.ds(0, cut) if cc == 0 else pl.ds(cut, r - cut)

        def exchange(mine):
            sibling = (x, y, 1 - mine)
            sent = []
            for a in range(n):
                for k, (px, py) in enumerate(peers):
                    cp = pltpu.make_async_remote_copy(
                        src_ref=ins[a].at[half(a, mine)], dst_ref=outs[a].at[me, half(a, mine)],
                        send_sem=ici_send.at[3 * a + k], recv_sem=ici_recv.at[3 * a + k],
                        device_id=(px, py, mine), device_id_type=MESH)
                    cp.start()
                    sent.append(cp)
            for a in range(n):
                for k, (px, py) in enumerate(peers):
                    landed = outs[a].at[2 * px + py, half(a, mine)]
                    pltpu.make_async_remote_copy(
                        src_ref=landed, dst_ref=landed, send_sem=ici_send.at[3 * a + k], recv_sem=ici_recv.at[3 * a + k],
                        device_id=(px, py, mine), device_id_type=MESH).wait_recv()
                    fwd = pltpu.make_async_remote_copy(
                        src_ref=landed, dst_ref=landed, send_sem=d2d_send.at[3 * a + k], recv_sem=d2d_recv.at[3 * a + k],
                        device_id=sibling, device_id_type=MESH)
                    fwd.start()
                    sent.append(fwd)
            for a in range(n):
                for k, (px, py) in enumerate(peers):
                    passed = outs[a].at[2 * px + py, half(a, 1 - mine)]
                    pltpu.make_async_remote_copy(
                        src_ref=passed, dst_ref=passed, send_sem=d2d_send.at[3 * a + k], recv_sem=d2d_recv.at[3 * a + k],
                        device_id=sibling, device_id_type=MESH).wait_recv()
            for cp in sent:
                cp.wait_send()

        for core in (0, 1):
            pl.when(c == core)(functools.partial(exchange, core))

    sems = [pltpu.SemaphoreType.DMA((3 * n,))] * 4
    if sequencer_id is None:
        hbm = pl.BlockSpec(memory_space=pltpu.HBM)
        gathered = pl.pallas_call(body, name=name, out_shape=out_shapes, in_specs=[hbm] * n, out_specs=[hbm] * n,
                                  scratch_shapes=sems)(*arrs)
    else:
        gathered = pl.kernel(body, out_type=out_shapes, mesh=plsc.ScalarSubcoreMesh(axis_name="sequencer", num_cores=1),
                             name=name, scratch_types=sems,
                             compiler_params=pltpu.CompilerParams(collective_id=sequencer_id))(*arrs)
    if after is not None:
        gathered, after = lax.optimization_barrier((gathered, after))
    chip = 2 * lax.axis_index("x") + lax.axis_index("y")
    filled = [lax.dynamic_update_slice(g, a[None], (chip, 0, 0)) for g, a in zip(gathered, arrs)]
    return filled if after is None else (filled, after)


def _chip_peers():
    x, y, c = lax.axis_index("x"), lax.axis_index("y"), lax.axis_index("c")
    return [(1 - x, y, c), (x, 1 - y, c), (1 - x, 1 - y, c)], (x, y, 1 - c)


def _handshake(peers):
    barrier = pltpu.get_barrier_semaphore()
    for peer in peers:
        pl.semaphore_signal(barrier, inc=1, device_id=peer, device_id_type=MESH)
    pl.semaphore_wait(barrier, len(peers))


def _sequencer_chip_exchange(arrs, name, collective_id):
    n = len(arrs)
    out_types = [jax.ShapeDtypeStruct(a.shape, a.dtype) for a in arrs]

    def body(*refs):
        ins, outs = refs[:n], refs[n:2 * n]
        send_sems, recv_sems = refs[2 * n:]
        chips, _ = _chip_peers()
        _handshake(chips)
        me = 2 * lax.axis_index("x") + lax.axis_index("y")
        sent = []
        for a in range(n):
            for k, peer in enumerate(chips):
                cp = pltpu.make_async_remote_copy(
                    src_ref=ins[a].at[2 * peer[0] + peer[1]], dst_ref=outs[a].at[me],
                    send_sem=send_sems.at[3 * a + k], recv_sem=recv_sems.at[3 * a + k], device_id=peer, device_id_type=MESH)
                cp.start()
                sent.append(cp)
        for a in range(n):
            for k, peer in enumerate(chips):
                landed = outs[a].at[2 * peer[0] + peer[1]]
                pltpu.make_async_remote_copy(
                    src_ref=landed, dst_ref=landed, send_sem=send_sems.at[3 * a + k], recv_sem=recv_sems.at[3 * a + k],
                    device_id=peer, device_id_type=MESH).wait_recv()
        for cp in sent:
            cp.wait_send()

    received = pl.kernel(
        body, out_type=out_types, mesh=plsc.ScalarSubcoreMesh(axis_name="sequencer", num_cores=1), name=name,
        scratch_types=[pltpu.SemaphoreType.DMA((3 * n,))] * 2,
        compiler_params=pltpu.CompilerParams(collective_id=collective_id),
    )(*arrs)
    chip = 2 * lax.axis_index("x") + lax.axis_index("y")
    return [lax.dynamic_update_slice(r, lax.dynamic_slice(a, (chip, 0, 0), (1,) + a.shape[1:]), (chip, 0, 0))
            for r, a in zip(received, arrs)]


def _sibling_exchange(arrs, name):
    n = len(arrs)

    def body(*refs):
        ins, outs = refs[:n], refs[n:2 * n]
        send_sems, recv_sems = refs[2 * n:]
        sibling = (lax.axis_index("x"), lax.axis_index("y"), 1 - lax.axis_index("c"))
        cps = []
        for a in range(n):
            cp = pltpu.make_async_remote_copy(src_ref=ins[a], dst_ref=outs[a], send_sem=send_sems.at[a],
                                              recv_sem=recv_sems.at[a], device_id=sibling, device_id_type=MESH)
            cp.start()
            cps.append(cp)
        for cp in cps:
            cp.wait_recv()
        for cp in cps:
            cp.wait_send()

    hbm = pl.BlockSpec(memory_space=pltpu.HBM)
    return pl.pallas_call(
        body, name=name, out_shape=[jax.ShapeDtypeStruct(a.shape, a.dtype) for a in arrs],
        in_specs=[hbm] * n, out_specs=[hbm] * n,
        scratch_shapes=[pltpu.SemaphoreType.DMA((n,)), pltpu.SemaphoreType.DMA((n,))],
    )(*arrs)


def _matmul(a, b, mode, out_dtype, name, tm=1024, tn=1024, tk=2048, relu2=False, times=None, b_split=False,
            out_split=False):
    b_shape = (b.shape[1], b.shape[2] * N_CHIP) if b_split else b.shape
    if mode == "nn":
        (m, k), (k2, n) = a.shape, b_shape
    elif mode == "nt":
        (m, k), (n, k2) = a.shape, b_shape
    else:
        (k, m), (k2, n) = a.shape, b_shape
    assert k == k2, (a.shape, b.shape, mode)
    n_cut = n // N_CHIP if (out_split or (b_split and mode != "nt")) else n
    k_cut = k // N_CHIP if (b_split and mode == "nt") else k
    tm, tn, tk = _tile(m, tm, LANES), _tile(n_cut, tn, LANES), _tile(k_cut, tk, LANES)
    assert n_cut % tn == 0 and k_cut % tk == 0 and m % tm == 0, (name, m, n, k, tm, tn, tk)
    nk = k // tk
    nbc, nkc = n_cut // tn, k_cut // tk
    n_in = 2 if times is None else 3
    n_out = 2 if relu2 else 1

    def product(a_ref, b_ref):
        if mode == "nn":
            return jnp.dot(a_ref[...], b_ref[...], preferred_element_type=F32)
        return lax.dot_general(a_ref[...], b_ref[...], NT_DIMS if mode == "nt" else TN_DIMS, preferred_element_type=F32)

    def finish(p, refs, o_refs):
        if relu2:
            p = jnp.maximum(p, 0.0)
            o_refs[0][...] = p.astype(o_refs[0].dtype)
            o_refs[1][...] = (p * p).astype(o_refs[1].dtype)
        elif times is not None:
            o_refs[0][...] = (2.0 * refs[2][...].astype(F32) * p).astype(o_refs[0].dtype)
        else:
            o_refs[0][...] = p.astype(o_refs[0].dtype)

    def body(*refs):
        o_refs = refs[n_in:n_in + n_out]
        if nk == 1:
            finish(product(refs[0], refs[1]), refs, o_refs)
            return
        acc_ref = refs[n_in + n_out]
        kk = pl.program_id(2)

        @pl.when(kk == 0)
        def _():
            acc_ref[...] = product(refs[0], refs[1])

        @pl.when((kk > 0) & (kk < nk - 1))
        def _():
            acc_ref[...] += product(refs[0], refs[1])

        @pl.when(kk == nk - 1)
        def _():
            finish(acc_ref[...] + product(refs[0], refs[1]), refs, o_refs)

    if mode == "tn":
        a_spec = pl.BlockSpec((tk, tm), lambda i, j, kk: (kk, i))
    else:
        a_spec = pl.BlockSpec((tm, tk), lambda i, j, kk: (i, kk))
    if mode == "nt":
        b_spec = (pl.BlockSpec((None, tn, tk), lambda i, j, kk: (kk // nkc, j, kk % nkc)) if b_split
                  else pl.BlockSpec((tn, tk), lambda i, j, kk: (j, kk)))
    else:
        b_spec = (pl.BlockSpec((None, tk, tn), lambda i, j, kk: (j // nbc, kk, j % nbc)) if b_split
                  else pl.BlockSpec((tk, tn), lambda i, j, kk: (kk, j)))
    mn_spec = pl.BlockSpec((tm, tn), lambda i, j, kk: (i, j))
    if out_split:
        o_spec = pl.BlockSpec((None, tm, tn), lambda i, j, kk: (j // nbc, i, j % nbc))
        o_shape = jax.ShapeDtypeStruct((N_CHIP, m, n_cut), out_dtype)
    else:
        o_spec, o_shape = mn_spec, jax.ShapeDtypeStruct((m, n), out_dtype)
    out = pl.pallas_call(
        body, name=name, grid=(m // tm, n // tn, nk), in_specs=[a_spec, b_spec] + [mn_spec] * (n_in - 2),
        out_specs=[o_spec] * n_out, out_shape=[o_shape] * n_out,
        scratch_shapes=[] if nk == 1 else [pltpu.VMEM((tm, tn), F32)],
        compiler_params=_params(("parallel", "parallel", "arbitrary")),
    )(*((a, b) if times is None else (a, b, times)))
    return out if relu2 else out[0]


def _mod_part(c_all, w_s, b_s, name):
    d, na = w_s.shape
    tn = _tile(na, 512, LANES)

    def body(c_ref, w_ref, b_ref, o_ref):
        ca = _silu(c_ref[...]).astype(BF16)
        o_ref[...] = jnp.dot(ca, w_ref[...].astype(BF16), preferred_element_type=F32) + b_ref[...]

    return pl.pallas_call(
        body, name=name, grid=(na // tn,),
        in_specs=[pl.BlockSpec((N_DEV, d), lambda j: (0, 0)), pl.BlockSpec((d, tn), lambda j: (0, j)),
                  pl.BlockSpec((1, tn), lambda j: (0, j))],
        out_specs=pl.BlockSpec((N_DEV, tn), lambda j: (0, j)),
        out_shape=jax.ShapeDtypeStruct((N_DEV, na), F32), compiler_params=_params(("parallel",)),
    )(c_all, w_s, b_s)


def _row_specs(tb, d, n_full, n_vec):
    full = pl.BlockSpec((tb, d), lambda i: (i, 0))
    vec = pl.BlockSpec((1, d), lambda i: (0, 0))
    return [full] * n_full + [vec] * n_vec


def _norm_mod(x, w, sc, sh, name):
    t, d = x.shape
    tb = _tile(t, 256, SUBLANES)

    def body(x_ref, w_ref, sc_ref, sh_ref, o_ref):
        xv = x_ref[...]
        o_ref[...] = (xv * _rms_scale(xv) * w_ref[...] * (1.0 + sc_ref[...]) + sh_ref[...]).astype(o_ref.dtype)

    return pl.pallas_call(
        body, name=name, grid=(t // tb,), in_specs=_row_specs(tb, d, 1, 3),
        out_specs=pl.BlockSpec((tb, d), lambda i: (i, 0)), out_shape=jax.ShapeDtypeStruct((t, d), BF16),
        compiler_params=_params(("parallel",)),
    )(x, w, sc, sh)


def _norm_mod_bwd(x, w, sc, dh, dres, name):
    t, d = x.shape
    tb = _tile(t, 256, SUBLANES)

    def body(x_ref, w_ref, sc_ref, dh_ref, dres_ref, dx_ref, dw_ref, dsc_ref, dsh_ref):
        @pl.when(pl.program_id(0) == 0)
        def _():
            dw_ref[...] = jnp.zeros_like(dw_ref)
            dsc_ref[...] = jnp.zeros_like(dsc_ref)
            dsh_ref[...] = jnp.zeros_like(dsh_ref)

        xv = x_ref[...]
        r = _rms_scale(xv)
        xn = xv * r
        g = dh_ref[...].astype(F32)
        wv, one_sc = w_ref[...], 1.0 + sc_ref[...]
        gxn = g * xn
        dsh_ref[...] += jnp.sum(g, axis=0, keepdims=True)
        dsc_ref[...] += jnp.sum(gxn, axis=0, keepdims=True) * wv
        dw_ref[...] += jnp.sum(gxn, axis=0, keepdims=True) * one_sc
        dxn = g * (wv * one_sc)
        dx_ref[...] = dres_ref[...] + r * (dxn - xn * jnp.mean(dxn * xn, axis=-1, keepdims=True))

    vec_out = pl.BlockSpec((1, d), lambda i: (0, 0))
    return pl.pallas_call(
        body, name=name, grid=(t // tb,),
        in_specs=[pl.BlockSpec((tb, d), lambda i: (i, 0)), pl.BlockSpec((1, d), lambda i: (0, 0)),
                  pl.BlockSpec((1, d), lambda i: (0, 0)), pl.BlockSpec((tb, d), lambda i: (i, 0)),
                  pl.BlockSpec((tb, d), lambda i: (i, 0))],
        out_specs=[pl.BlockSpec((tb, d), lambda i: (i, 0)), vec_out, vec_out, vec_out],
        out_shape=[jax.ShapeDtypeStruct((t, d), F32)] + [jax.ShapeDtypeStruct((1, d), F32)] * 3,
        compiler_params=_params(("arbitrary",)),
    )(x, w, sc, dh, dres)


def _resid_norm_mod(x, y, w, gt, w2, sc, sh, name):
    t, d = x.shape
    tb = _tile(t, 256, SUBLANES)

    def body(x_ref, y_ref, w_ref, gt_ref, w2_ref, sc_ref, sh_ref, o_ref, h_ref):
        yv = y_ref[...]
        x2 = x_ref[...] + gt_ref[...] * (yv * _rms_scale(yv) * w_ref[...])
        o_ref[...] = x2
        h_ref[...] = (x2 * _rms_scale(x2) * w2_ref[...] * (1.0 + sc_ref[...]) + sh_ref[...]).astype(h_ref.dtype)

    full = pl.BlockSpec((tb, d), lambda i: (i, 0))
    return pl.pallas_call(
        body, name=name, grid=(t // tb,), in_specs=_row_specs(tb, d, 2, 5), out_specs=[full, full],
        out_shape=[jax.ShapeDtypeStruct((t, d), F32), jax.ShapeDtypeStruct((t, d), BF16)],
        compiler_params=_params(("parallel",)),
    )(x, y, w, gt, w2, sc, sh)


def _loss_head(x2, y2, w, gt, target, name):
    t, d = x2.shape
    tb = _tile(t, 256, SUBLANES)

    def body(x_ref, y_ref, tg_ref, w_ref, gt_ref, do_ref, loss_ref, dy_ref, dgt_ref, dw_ref):
        @pl.when(pl.program_id(0) == 0)
        def _():
            loss_ref[...] = jnp.zeros_like(loss_ref)
            dgt_ref[...] = jnp.zeros_like(dgt_ref)
            dw_ref[...] = jnp.zeros_like(dw_ref)

        yv = y_ref[...]
        r = _rms_scale(yv)
        yn = yv * r
        wv, gtv = w_ref[...], gt_ref[...]
        err = x_ref[...] + gtv * (yn * wv) - tg_ref[...]
        g = err * (1.0 / d)
        do_ref[...] = g
        per_tok = jnp.mean(err * err, axis=-1, keepdims=True)
        loss_ref[...] += 0.5 * jnp.sum(per_tok, axis=0, keepdims=True)
        gyn = jnp.sum(g * yn, axis=0, keepdims=True)
        dgt_ref[...] += gyn * wv
        dw_ref[...] += gyn * gtv
        dyn = g * (gtv * wv)
        dy_ref[...] = (r * (dyn - yn * jnp.mean(dyn * yn, axis=-1, keepdims=True))).astype(dy_ref.dtype)

    full = pl.BlockSpec((tb, d), lambda i: (i, 0))
    vec_out = pl.BlockSpec((1, d), lambda i: (0, 0))
    return pl.pallas_call(
        body, name=name, grid=(t // tb,), in_specs=_row_specs(tb, d, 3, 2),
        out_specs=[full, pl.BlockSpec((1, LANES), lambda i: (0, 0)), full, vec_out, vec_out],
        out_shape=[jax.ShapeDtypeStruct((t, d), F32), jax.ShapeDtypeStruct((1, LANES), F32), jax.ShapeDtypeStruct((t, d), BF16),
                   jax.ShapeDtypeStruct((1, d), F32), jax.ShapeDtypeStruct((1, d), F32)],
        compiler_params=_params(("arbitrary",)),
    )(x2, y2, target, w, gt)


def _resid_bwd(dout, y, w, gt, name):
    t, d = y.shape
    tb = _tile(t, 256, SUBLANES)

    def body(do_ref, y_ref, w_ref, gt_ref, dy_ref, dgt_ref, dw_ref):
        @pl.when(pl.program_id(0) == 0)
        def _():
            dgt_ref[...] = jnp.zeros_like(dgt_ref)
            dw_ref[...] = jnp.zeros_like(dw_ref)

        yv, g = y_ref[...], do_ref[...]
        r = _rms_scale(yv)
        yn = yv * r
        wv, gtv = w_ref[...], gt_ref[...]
        gyn = jnp.sum(g * yn, axis=0, keepdims=True)
        dgt_ref[...] += gyn * wv
        dw_ref[...] += gyn * gtv
        dyn = g * (gtv * wv)
        dy_ref[...] = (r * (dyn - yn * jnp.mean(dyn * yn, axis=-1, keepdims=True))).astype(dy_ref.dtype)

    vec_out = pl.BlockSpec((1, d), lambda i: (0, 0))
    return pl.pallas_call(
        body, name=name, grid=(t // tb,), in_specs=_row_specs(tb, d, 2, 2),
        out_specs=[pl.BlockSpec((tb, d), lambda i: (i, 0)), vec_out, vec_out],
        out_shape=[jax.ShapeDtypeStruct((t, d), BF16)] + [jax.ShapeDtypeStruct((1, d), F32)] * 2,
        compiler_params=_params(("arbitrary",)),
    )(dout, y, w, gt)


def _row_half_to_bf16(full, which, sib, name):
    n, r, c = full.shape
    by_rows = r % 32 == 0
    r, c = (r // 2, c) if by_rows else (r, c // 2)
    tr, tc = _elementwise_tiles(r, c)
    nbh = (r // tr) if by_rows else (c // tc)

    def body(which_ref, a_ref, *rest):
        if sib is None:
            rest[0][...] = a_ref[...].astype(BF16)
        else:
            rest[1][...] = (a_ref[...] + rest[0][...].astype(F32)).astype(BF16)

    if by_rows:
        half_spec = pl.BlockSpec((1, tr, tc), lambda j, i, k, which_ref: (j, which_ref[0] * nbh + i, k))
    else:
        half_spec = pl.BlockSpec((1, tr, tc), lambda j, i, k, which_ref: (j, i, which_ref[0] * nbh + k))
    spec = pl.BlockSpec((1, tr, tc), lambda j, i, k, which_ref: (j, i, k))
    grid_spec = pltpu.PrefetchScalarGridSpec(
        num_scalar_prefetch=1, grid=(n, r // tr, c // tc), in_specs=[half_spec] + ([] if sib is None else [spec]), out_specs=spec)
    return pl.pallas_call(
        body, name=name, grid_spec=grid_spec, out_shape=jax.ShapeDtypeStruct((n, r, c), BF16),
        compiler_params=_params(("parallel", "parallel", "parallel")),
    )(which, full, *([] if sib is None else [sib]))


def _sum_chips(recv, name):
    _, r, c = recv.shape
    tr, tc = _elementwise_tiles(r, c)

    def body(x_ref, o_ref):
        acc = x_ref[0].astype(F32)
        for j in range(1, N_CHIP):
            acc = acc + x_ref[j].astype(F32)
        o_ref[...] = acc

    return pl.pallas_call(
        body, name=name, grid=(r // tr, c // tc), in_specs=[pl.BlockSpec((N_CHIP, tr, tc), lambda i, j: (0, i, j))],
        out_specs=pl.BlockSpec((tr, tc), lambda i, j: (i, j)), out_shape=jax.ShapeDtypeStruct((r, c), F32),
        compiler_params=_params(("parallel", "parallel")),
    )(recv)


def _adamw(w, g_parts, m, v, name, by_core=False, outer=False):
    r, c = w.shape
    by_rows = r % 32 == 0
    if by_core:
        tr, tc = _elementwise_tiles(*((r // 2, c) if by_rows else (r, c // 2)))
        nbh = (r // 2) // tr if by_rows else (c // 2) // tc
    else:
        tr, tc = _elementwise_tiles(r, c)
    n_g = len(g_parts)
    c1 = 1.0 / (1.0 - ADAM_B1 ** ADAM_STEP)
    c2 = 1.0 / (1.0 - ADAM_B2 ** ADAM_STEP)

    def body(*refs):
        w_ref, g_refs, m_ref, v_ref = refs[0], refs[1:1 + n_g], refs[1 + n_g], refs[2 + n_g]
        g_out, d_out, m_out, v_out = refs[3 + n_g:]
        if by_core:
            in_my_half = (pl.program_id(0 if by_rows else 1) // nbh) == lax.axis_index("c")
            g = jnp.where(in_my_half, g_refs[0][...], g_refs[1][...])
        elif outer:
            g = lax.dot_general(_silu(g_refs[0][...]), g_refs[1][...], TN_DIMS, precision=HI, preferred_element_type=F32)
        else:
            g = g_refs[0][...]
            for extra in g_refs[1:]:
                g = g + extra[...]
        mn = ADAM_B1 * m_ref[...] + (1.0 - ADAM_B1) * g
        vn = ADAM_B2 * v_ref[...] + (1.0 - ADAM_B2) * (g * g)
        g_out[...] = g
        m_out[...] = mn
        v_out[...] = vn
        d_out[...] = -ADAM_LR * ((mn * c1) / (jnp.sqrt(vn * c2) + ADAM_EPS) + ADAM_WD * w_ref[...])

    spec = pl.BlockSpec((tr, tc), lambda i, j: (i, j))
    if by_core:
        g_spec = pl.BlockSpec((tr, tc), (lambda i, j: (i % nbh, j)) if by_rows else (lambda i, j: (i, j % nbh)))
    else:
        g_spec = spec
    g_specs = [g_spec] * n_g
    if outer:
        g_specs = [pl.BlockSpec((N_DEV, tr), lambda i, j: (0, i)), pl.BlockSpec((N_DEV, tc), lambda i, j: (0, j))]
    return pl.pallas_call(
        body, name=name, grid=(r // tr, c // tc), in_specs=[spec] + g_specs + [spec] * 2, out_specs=[spec] * 4,
        out_shape=[jax.ShapeDtypeStruct((r, c), F32)] * 4, compiler_params=_params(("parallel", "parallel")),
    )(w, *g_parts, m, v)


def _conv_taps(u, t):
    rows = lax.broadcasted_iota(jnp.int32, u.shape, 0)
    return [u] + [jnp.where(rows >= dd, pltpu.roll(u, dd, 0), 0.0) for dd in range(1, CONV_K)]


def _conv_fwd(proj, conv_w, col0, name):
    t = proj.shape[0]
    ch = conv_w.shape[1]

    def body(u_ref, w_ref, o_ref):
        taps = _conv_taps(u_ref[...], t)
        wv = w_ref[...]
        y = taps[0] * wv[CONV_K - 1:CONV_K]
        for dd in range(1, CONV_K):
            y = y + taps[dd] * wv[CONV_K - 1 - dd:CONV_K - dd]
        o_ref[...] = _silu(y)

    return pl.pallas_call(
        body, name=name, grid=(ch // LANES,),
        in_specs=[pl.BlockSpec((t, LANES), lambda j: (0, col0 + j)), pl.BlockSpec((CONV_K, LANES), lambda j: (0, j))],
        out_specs=pl.BlockSpec((t, LANES), lambda j: (0, j)), out_shape=jax.ShapeDtypeStruct((t, ch), F32),
        compiler_params=_params(("parallel",)),
    )(proj, conv_w)


def _conv_bwd(proj, conv_w, ds, col0, name):
    t = proj.shape[0]
    ch = conv_w.shape[1]

    def body(u_ref, w_ref, ds_ref, du_ref, dw_ref):
        u = u_ref[...]
        taps = _conv_taps(u, t)
        wv = w_ref[...]
        y = taps[0] * wv[CONV_K - 1:CONV_K]
        for dd in range(1, CONV_K):
            y = y + taps[dd] * wv[CONV_K - 1 - dd:CONV_K - dd]
        sg = jax.nn.sigmoid(y)
        dy = ds_ref[...] * (sg * (1.0 + y * (1.0 - sg)))
        rows = lax.broadcasted_iota(jnp.int32, u.shape, 0)
        du = dy * wv[CONV_K - 1:CONV_K]
        for dd in range(1, CONV_K):
            ahead = jnp.where(rows < t - dd, pltpu.roll(dy, t - dd, 0), 0.0)
            du = du + ahead * wv[CONV_K - 1 - dd:CONV_K - dd]
        du_ref[...] = du.astype(du_ref.dtype)
        dws = [jnp.sum(dy * taps[CONV_K - 1 - j], axis=0, keepdims=True) for j in range(CONV_K)]
        dw_ref[...] = jnp.concatenate(dws, axis=0)

    return pl.pallas_call(
        body, name=name, grid=(ch // LANES,),
        in_specs=[pl.BlockSpec((t, LANES), lambda j: (0, col0 + j)), pl.BlockSpec((CONV_K, LANES), lambda j: (0, j)),
                  pl.BlockSpec((t, LANES), lambda j: (0, j))],
        out_specs=[pl.BlockSpec((t, LANES), lambda j: (0, j)), pl.BlockSpec((CONV_K, LANES), lambda j: (0, j))],
        out_shape=[jax.ShapeDtypeStruct((t, ch), BF16), jax.ShapeDtypeStruct((CONV_K, ch), F32)],
        compiler_params=_params(("parallel",)),
    )(proj, conv_w, ds)


def _hg_block(st, q, fl, vi, g, l0, l1, nw):
    hs = range(len(st))
    tb = q[0].shape[0]
    ln = HG_SUB
    lb = [jax.nn.sigmoid(l0[h] - l1[h]) for h in hs]
    rows = lax.broadcasted_iota(jnp.int32, (ln, HEAD), 0)
    tri = (lax.broadcasted_iota(jnp.int32, (ln, ln), 0) >= lax.broadcasted_iota(jnp.int32, (ln, ln), 1)).astype(F32)
    st = list(st)
    outs = [[] for _ in hs]
    for i in range(tb // ln):
        sl = slice(i * ln, (i + 1) * ln)
        qs, vs = [q[h][sl] for h in hs], [vi[h][sl] for h in hs]
        f = [lb[h] + (1.0 - lb[h]) * jax.nn.sigmoid(fl[h][sl]) for h in hs]
        k = [1.0 - f[h] for h in hs]
        b = [jnp.dot(tri, jnp.log(f[h]), precision=HI, preferred_element_type=F32) for h in hs]
        o = [lax.dot_general((qs[h] * jnp.exp(b[h])).astype(BF16), st[h].astype(BF16), NT_DIMS, preferred_element_type=F32)
             for h in hs]
        n_tiles = ln // SUBLANES
        acc = [o] + [[jnp.zeros((ln - SUBLANES * ti, HEAD), F32) for _ in hs] for ti in range(1, n_tiles)]
        for s in range(ln):
            ti = s // SUBLANES
            r0 = ti * SUBLANES
            e = [jnp.exp(jnp.where(rows[r0:] >= s, b[h][r0:] - b[h][s:s + 1], -1e30)) for h in hs]
            a = [jnp.sum(qs[h][r0:] * e[h] * k[h][s:s + 1], axis=-1, keepdims=True) for h in hs]
            acc[ti] = [acc[ti][h] + a[h] * vs[h][s:s + 1] for h in hs]
        o = [jnp.concatenate([sum(acc[ti][h][(j - ti) * SUBLANES:(j - ti + 1) * SUBLANES] for ti in range(j + 1))
                              for j in range(n_tiles)], axis=0) for h in hs]
        kt = [k[h] * jnp.exp(b[h][ln - 1:ln] - b[h]) for h in hs]
        upd = [lax.dot_general(vs[h].astype(BF16), kt[h].astype(BF16), TN_DIMS, preferred_element_type=F32) for h in hs]
        st = [st[h] * jnp.exp(b[h][ln - 1:ln]) + upd[h] for h in hs]
        for h in hs:
            outs[h].append(o[h])
    o = [jnp.concatenate(outs[h], axis=0) for h in hs]
    out = [o[h] * _rms_scale(o[h]) * nw * _silu(g[h]) for h in hs]
    return st, out


def _head_cols(h):
    return slice(h * HEAD, (h + 1) * HEAD)


def _head_groups(n_heads, group):
    g = min(group, n_heads)
    return [list(range(i, min(i + g, n_heads))) for i in range(0, n_heads, g)]


def _hg_in_specs(n_heads, tb, time_index):
    hw = n_heads * HEAD
    cols = [pl.BlockSpec((tb, hw), functools.partial(lambda part, j: (time_index(j), part), part)) for part in range(4)]
    head_rows = pl.BlockSpec((n_heads, 1, HEAD), lambda j: (0, 0, 0))
    return cols + [head_rows, head_rows, pl.BlockSpec((1, HEAD), lambda j: (0, 0))]


def _hgrn2_fwd(proj, l0, l1, nw, n_heads, name):
    t = proj.shape[0]
    hw = n_heads * HEAD
    tb = _tile(t, HG_BLOCK, HG_SUB)
    nb = t // tb

    def body(q_ref, f_ref, i_ref, g_ref, l0_ref, l1_ref, nw_ref, o_ref, save_ref, st_ref):
        @pl.when(pl.program_id(0) == 0)
        def _():
            st_ref[...] = jnp.zeros_like(st_ref)

        for hs in _head_groups(n_heads, HG_FWD_GROUP):
            st = [st_ref[h] for h in hs]
            for h, s in zip(hs, st):
                save_ref[h] = s
            st, out = _hg_block(st, *[[r[:, _head_cols(h)] for h in hs] for r in (q_ref, f_ref, i_ref, g_ref)],
                                [l0_ref[h] for h in hs], [l1_ref[h] for h in hs], nw_ref[...])
            for h, s, o in zip(hs, st, out):
                st_ref[h] = s
                o_ref[:, _head_cols(h)] = o.astype(o_ref.dtype)

    return pl.pallas_call(
        body, name=name, grid=(nb,), in_specs=_hg_in_specs(n_heads, tb, lambda j: j),
        out_specs=[pl.BlockSpec((tb, hw), lambda j: (j, 0)),
                   pl.BlockSpec((None, n_heads, HEAD, HEAD), lambda j: (j, 0, 0, 0))],
        out_shape=[jax.ShapeDtypeStruct((t, hw), BF16), jax.ShapeDtypeStruct((nb, n_heads, HEAD, HEAD), F32)],
        scratch_shapes=[pltpu.VMEM((n_heads, HEAD, HEAD), F32)], compiler_params=_params(("arbitrary",)),
    )(proj, proj, proj, proj, l0, l1, nw)


def _hgrn2_bwd(proj, l0, l1, nw, saved, d_ocat, n_heads, name):
    t = proj.shape[0]
    tb = _tile(t, HG_BLOCK, HG_SUB)
    nb = t // tb
    rev = lambda j: nb - 1 - j

    hw = n_heads * HEAD

    def body(q_ref, f_ref, i_ref, g_ref, l0_ref, l1_ref, nw_ref, save_ref, do_ref,
             dp_ref, dl0_ref, dl1_ref, dnw_ref, dst_ref):
        @pl.when(pl.program_id(0) == 0)
        def _():
            dst_ref[...] = jnp.zeros_like(dst_ref)
            dl0_ref[...] = jnp.zeros_like(dl0_ref)
            dl1_ref[...] = jnp.zeros_like(dl1_ref)
            dnw_ref[...] = jnp.zeros_like(dnw_ref)

        dnw_acc = jnp.zeros((1, HEAD), F32)
        for hs in _head_groups(n_heads, HG_BWD_GROUP):
            _, vjp = jax.vjp(_hg_block, [save_ref[h] for h in hs],
                             *[[r[:, _head_cols(h)] for h in hs] for r in (q_ref, f_ref, i_ref, g_ref)],
                             [l0_ref[h] for h in hs], [l1_ref[h] for h in hs], nw_ref[...])
            dst, dq, df, di, dg, dl0, dl1, dnw = vjp(([dst_ref[h] for h in hs], [do_ref[:, _head_cols(h)] for h in hs]))
            for i, h in enumerate(hs):
                dst_ref[h] = dst[i]
                for part, val in enumerate((dq, df, di, dg)):
                    dp_ref[:, part * hw + h * HEAD:part * hw + (h + 1) * HEAD] = val[i].astype(dp_ref.dtype)
                dl0_ref[h] += dl0[i]
                dl1_ref[h] += dl1[i]
            dnw_acc = dnw_acc + dnw
        dnw_ref[...] += dnw_acc

    head_rows = pl.BlockSpec((n_heads, 1, HEAD), lambda j: (0, 0, 0))
    return pl.pallas_call(
        body, name=name, grid=(nb,),
        in_specs=_hg_in_specs(n_heads, tb, rev) + [pl.BlockSpec((None, n_heads, HEAD, HEAD), lambda j: (rev(j), 0, 0, 0)),
                                                   pl.BlockSpec((tb, hw), lambda j: (rev(j), 0))],
        out_specs=[pl.BlockSpec((tb, 4 * hw), lambda j: (rev(j), 0)), head_rows, head_rows,
                   pl.BlockSpec((1, HEAD), lambda j: (0, 0))],
        out_shape=[jax.ShapeDtypeStruct((t, 4 * hw), BF16)] + [jax.ShapeDtypeStruct((n_heads, 1, HEAD), F32)] * 2
        + [jax.ShapeDtypeStruct((1, HEAD), F32)],
        scratch_shapes=[pltpu.VMEM((n_heads, HEAD, HEAD), F32)], compiler_params=_params(("arbitrary",)),
    )(proj, proj, proj, proj, l0, l1, nw, saved, d_ocat)


NN_DIMS = (((1,), (0,)), ((), ()))


def _split_bf16(x):
    hi = x.astype(BF16)
    return hi, (x - hi.astype(F32)).astype(BF16)


def _dot3(a, b, dims=NN_DIMS):
    (ah, al), (bh, bl) = _split_bf16(a), _split_bf16(b)
    dot = functools.partial(lax.dot_general, dimension_numbers=dims, preferred_element_type=F32)
    return dot(ah, bh) + dot(ah, bl) + dot(al, bh)


@jax.custom_vjp
def _mm3(a, b):
    return _dot3(a, b)


def _mm3_fwd(a, b):
    return _dot3(a, b), (a, b)


def _mm3_bwd(res, g):
    a, b = res
    return _dot3(g, b, NT_DIMS), _dot3(a, g, TN_DIMS)


_mm3.defvjp(_mm3_fwd, _mm3_bwd)


def _dot_bf16(a, b, dims=(((1,), (0,)), ((), ()))):
    return lax.dot_general(a.astype(BF16), b.astype(BF16), dims, preferred_element_type=F32)


def _inv_unit_lower_raw(ms):
    hs = range(len(ms))
    c = ms[0].shape[0]
    r = lax.broadcasted_iota(jnp.int32, (c, c), 0)
    q = lax.broadcasted_iota(jnp.int32, (c, c), 1)
    eye = (r == q).astype(F32)
    md = [jnp.where((r // GDN_INV_BLOCK) == (q // GDN_INV_BLOCK), ms[h], 0.0) for h in hs]
    p = [-md[h] for h in hs]
    t16 = [eye + p[h] for h in hs]
    for _ in range(int(math.log2(GDN_INV_BLOCK)) - 1):
        p = [_dot3(p[h], p[h]) for h in hs]
        t16 = [t16[h] + _dot3(t16[h], p[h]) for h in hs]
    p = [-_dot3(t16[h], ms[h] - md[h]) for h in hs]
    t2 = [eye + p[h] for h in hs]
    for _ in range(int(math.log2(c // GDN_INV_BLOCK)) - 1):
        p = [_dot3(p[h], p[h]) for h in hs]
        t2 = [t2[h] + _dot3(t2[h], p[h]) for h in hs]
    return [_dot3(t2[h], t16[h]) for h in hs]


@jax.custom_vjp
def _inv_unit_lower(ms):
    return _inv_unit_lower_raw(ms)


def _inv_fwd(ms):
    ts = _inv_unit_lower_raw(ms)
    return ts, ts


def _inv_bwd(ts, dts):
    hs = range(len(ts))
    inner = [_dot3(ts[h], dts[h], TN_DIMS) for h in hs]
    return ([-_dot3(inner[h], ts[h], NT_DIMS) for h in hs],)


_inv_unit_lower.defvjp(_inv_fwd, _inv_bwd)


def _gdn_block(precise, onehots, st, qc, kc, vc, g, ab, alog_row, dtb_row, nw):
    inverse, dot3 = precise
    hs = range(len(st))
    c = qc[0].shape[0]
    lane_sum = lambda v: jnp.sum(v, axis=-1, keepdims=True)
    a = [lane_sum(ab * onehots[h][0]) for h in hs]
    bb = [lane_sum(ab * onehots[h][1]) for h in hs]
    alog = [lane_sum(alog_row * onehots[h][0]) for h in hs]
    dtb = [lane_sum(dtb_row * onehots[h][0]) for h in hs]
    la = [-jnp.exp(alog[h]) * _softplus(a[h] + dtb[h]) for h in hs]
    beta = [jax.nn.sigmoid(bb[h]) for h in hs]
    q = [qc[h] * lax.rsqrt(lane_sum(qc[h] * qc[h]) + EPS) * (HEAD ** -0.5) for h in hs]
    k = [kc[h] * lax.rsqrt(lane_sum(kc[h] * kc[h]) + EPS) for h in hs]
    r = lax.broadcasted_iota(jnp.int32, (c, c), 0)
    s = lax.broadcasted_iota(jnp.int32, (c, c), 1)
    tri = (r >= s).astype(F32)
    g_cc = [dot3(tri, jnp.broadcast_to(la[h], (c, c))) for h in hs]
    g_cl = [dot3(tri, jnp.broadcast_to(la[h], (c, HEAD))) for h in hs]
    gamma = [jnp.exp(jnp.where(r >= s, g_cc[h] - g_cc[h].T, -1e30)) for h in hs]
    kk = [_dot_bf16(k[h], k[h], NT_DIMS) for h in hs]
    m = [jnp.where(r > s, beta[h] * kk[h] * gamma[h], 0.0) for h in hs]
    tm = inverse(m)
    eg = [jnp.exp(g_cl[h]) for h in hs]
    rhs = [jnp.concatenate([vc[h] * beta[h], k[h] * (beta[h] * eg[h])], axis=1) for h in hs]
    sol = [dot3(tm[h], rhs[h]) for h in hs]
    qk = [_dot_bf16(q[h], k[h], NT_DIMS) * gamma[h] for h in hs]
    g_last = [g_cl[h][c - 1:c] for h in hs]
    k_tail = [k[h] * jnp.exp(g_last[h] - g_cl[h]) for h in hs]
    v_new = [sol[h][:, :HEAD] - _dot_bf16(sol[h][:, HEAD:], st[h], NT_DIMS) for h in hs]
    o_st = [_dot_bf16(q[h] * eg[h], st[h], NT_DIMS) for h in hs]
    o = [o_st[h] + _dot_bf16(qk[h], v_new[h]) for h in hs]
    upd = [_dot_bf16(v_new[h], k_tail[h], TN_DIMS) for h in hs]
    st = [st[h] * jnp.exp(g_last[h]) + upd[h] for h in hs]
    out = [o[h] * _rms_scale(o[h]) * nw * _silu(g[h]) for h in hs]
    return st, out


def _head_onehots(n_heads, h):
    lane = lax.broadcasted_iota(jnp.int32, (1, LANES), 1)
    return (lane == h).astype(F32), (lane == n_heads + h).astype(F32)


def _gdn_in_specs(n_heads, c, time_index):
    hw = n_heads * HEAD
    qkv = [pl.BlockSpec((c, hw), functools.partial(lambda part, j: (time_index(j), part), part)) for part in range(3)]
    row = pl.BlockSpec((1, LANES), lambda j: (0, 0))
    return qkv + [pl.BlockSpec((c, hw), lambda j: (time_index(j), 7)),
                  pl.BlockSpec((c, LANES), lambda j: (time_index(j), 8 * n_heads)), row, row, row]


def _gdn_fwd(qkv, proj, alog_row, dtb_row, nw, n_heads, name):
    t = qkv.shape[0]
    hw = n_heads * HEAD
    c = _tile(t, GDN_CHUNK, GDN_CHUNK)
    nb = t // c

    def body(q_ref, k_ref, v_ref, g_ref, ab_ref, al_ref, dt_ref, nw_ref, o_ref, save_ref, st_ref):
        @pl.when(pl.program_id(0) == 0)
        def _():
            st_ref[...] = jnp.zeros_like(st_ref)

        for hs in _head_groups(n_heads, GDN_FWD_GROUP):
            st = [st_ref[h] for h in hs]
            for h, s in zip(hs, st):
                save_ref[h] = s
            st, out = _gdn_block((_inv_unit_lower_raw, _dot3), [_head_onehots(n_heads, h) for h in hs], st,
                                 *[[r[:, _head_cols(h)] for h in hs] for r in (q_ref, k_ref, v_ref, g_ref)],
                                 ab_ref[...], al_ref[...], dt_ref[...], nw_ref[...])
            for h, s, o in zip(hs, st, out):
                st_ref[h] = s
                o_ref[:, _head_cols(h)] = o.astype(o_ref.dtype)

    return pl.pallas_call(
        body, name=name, grid=(nb,), in_specs=_gdn_in_specs(n_heads, c, lambda j: j),
        out_specs=[pl.BlockSpec((c, hw), lambda j: (j, 0)),
                   pl.BlockSpec((None, n_heads, HEAD, HEAD), lambda j: (j, 0, 0, 0))],
        out_shape=[jax.ShapeDtypeStruct((t, hw), BF16), jax.ShapeDtypeStruct((nb, n_heads, HEAD, HEAD), F32)],
        scratch_shapes=[pltpu.VMEM((n_heads, HEAD, HEAD), F32)], compiler_params=_params(("arbitrary",)),
    )(qkv, qkv, qkv, proj, proj, alog_row, dtb_row, nw)


def _gdn_bwd(qkv, proj, alog_row, dtb_row, nw, saved, d_ocat, n_heads, name):
    t = qkv.shape[0]
    c = _tile(t, GDN_CHUNK, GDN_CHUNK)
    nb = t // c
    rev = lambda j: nb - 1 - j

    hw = n_heads * HEAD

    def body(q_ref, k_ref, v_ref, g_ref, ab_ref, al_ref, dt_ref, nw_ref, save_ref, do_ref,
             dqkv_ref, dg_ref, dab_ref, dal_ref, ddt_ref, dnw_ref, dst_ref):
        @pl.when(pl.program_id(0) == 0)
        def _():
            dst_ref[...] = jnp.zeros_like(dst_ref)
            dal_ref[...] = jnp.zeros_like(dal_ref)
            ddt_ref[...] = jnp.zeros_like(ddt_ref)
            dnw_ref[...] = jnp.zeros_like(dnw_ref)

        dab_acc = jnp.zeros((c, LANES), F32)
        row_acc = [jnp.zeros((1, LANES), F32)] * 3
        for hs in _head_groups(n_heads, GDN_BWD_GROUP):
            fn = functools.partial(_gdn_block, (_inv_unit_lower, _mm3), [_head_onehots(n_heads, h) for h in hs])
            _, vjp = jax.vjp(fn, [save_ref[h] for h in hs],
                             *[[r[:, _head_cols(h)] for h in hs] for r in (q_ref, k_ref, v_ref, g_ref)],
                             ab_ref[...], al_ref[...], dt_ref[...], nw_ref[...])
            dst, dq, dk, dv, dg, dab, dal, ddt, dnw = vjp(([dst_ref[h] for h in hs], [do_ref[:, _head_cols(h)] for h in hs]))
            for i, h in enumerate(hs):
                dst_ref[h] = dst[i]
                for part, val in enumerate((dq, dk, dv)):
                    dqkv_ref[:, part * hw + h * HEAD:part * hw + (h + 1) * HEAD] = val[i]
                dg_ref[:, _head_cols(h)] = dg[i].astype(dg_ref.dtype)
            dab_acc = dab_acc + dab
            row_acc = [acc + val for acc, val in zip(row_acc, (dal, ddt, dnw))]
        dab_ref[...] = dab_acc
        dal_ref[...] += row_acc[0]
        ddt_ref[...] += row_acc[1]
        dnw_ref[...] += row_acc[2]

    row = pl.BlockSpec((1, LANES), lambda j: (0, 0))
    return pl.pallas_call(
        body, name=name, grid=(nb,),
        in_specs=_gdn_in_specs(n_heads, c, rev) + [pl.BlockSpec((None, n_heads, HEAD, HEAD), lambda j: (rev(j), 0, 0, 0)),
                                                   pl.BlockSpec((c, hw), lambda j: (rev(j), 1))],
        out_specs=[pl.BlockSpec((c, 3 * hw), lambda j: (rev(j), 0)), pl.BlockSpec((c, hw), lambda j: (rev(j), 0)),
                   pl.BlockSpec((c, LANES), lambda j: (rev(j), 0)), row, row, row],
        out_shape=[jax.ShapeDtypeStruct((t, 3 * hw), F32), jax.ShapeDtypeStruct((t, hw), BF16),
                   jax.ShapeDtypeStruct((t, LANES), F32)] + [jax.ShapeDtypeStruct((1, LANES), F32)] * 3,
        scratch_shapes=[pltpu.VMEM((n_heads, HEAD, HEAD), F32)], compiler_params=_params(("arbitrary",)),
    )(qkv, qkv, qkv, proj, proj, alog_row, dtb_row, nw, saved, d_ocat)


def _pad_lanes(v, n):
    v = v.reshape(1, -1)
    return jnp.pad(v, ((0, 0), (0, n - v.shape[1])))


def _pack_rows(vecs):
    flat = jnp.concatenate([v.reshape(-1) for v in vecs])
    offs, o = [], 0
    for v in vecs:
        offs.append((o, v.size))
        o += v.size
    per_row = -(-o // (SUBLANES * LANES)) * LANES
    flat = jnp.pad(flat, (0, SUBLANES * per_row - o))
    return flat.reshape(SUBLANES, per_row), offs


def _unpack(gathered, offs):
    per_dev = gathered.reshape(N_DEV, -1)
    return [per_dev[:, o:o + n] for o, n in offs]


def _sum_devices(part):
    acc = part[0]
    for i in range(1, N_DEV):
        acc = acc + part[i]
    return acc


def kernel(x, c, w_ada, b_ada, pre_mix_norm, post_mix_norm, pre_ffn_norm, post_ffn_norm, w_in, hg_lb_logits, hg_norm, gdn_conv_w, gdn_a_log, gdn_dt_bias, gdn_norm, w_out, w_ff1, w_ff2, loss_target, m_w_ada, m_b_ada, m_pre_mix_norm, m_post_mix_norm, m_pre_ffn_norm, m_post_ffn_norm, m_w_in, m_hg_lb_logits, m_hg_norm, m_gdn_conv_w, m_gdn_a_log, m_gdn_dt_bias, m_gdn_norm, m_w_out, m_w_ff1, m_w_ff2, v_w_ada, v_b_ada, v_pre_mix_norm, v_post_mix_norm, v_pre_ffn_norm, v_post_ffn_norm, v_w_in, v_hg_lb_logits, v_hg_norm, v_gdn_conv_w, v_gdn_a_log, v_gdn_dt_bias, v_gdn_norm, v_w_out, v_w_ff1, v_w_ff2):
    assert x.shape[0] == 1 and w_ada.shape[0] == 1 and hg_lb_logits.shape[0] == 2
    t, d = x.shape[1], x.shape[2]
    n_heads = (d // 2) // HEAD
    hw = n_heads * HEAD
    in_cols = 8 * hw + 2 * n_heads
    np_cols = 8 * hw + 2 * LANES
    d_ff = w_ff1.shape[2] * N_CHIP
    na = w_ada.shape[2]
    ax, ay, ac = lax.axis_index("x"), lax.axis_index("y"), lax.axis_index("c")
    chip = 2 * ax + ay
    dev = 4 * ax + 2 * ay + ac

    x2d, tgt = x[0], loss_target[0]

    pack1, offs1 = _pack_rows([c[0], gdn_conv_w[0]])
    c_all, convw_all = _unpack(_gather8(pack1, "gather_cond"), offs1)
    conv_sh = gdn_conv_w.shape[2]
    conv_w = jnp.concatenate([convw_all[2 * j].reshape(CONV_K, conv_sh) for j in range(N_CHIP)], axis=1)

    b_s = lax.dynamic_slice(b_ada, (0, chip * na), (1, na))
    mod_part = _mod_part(c_all, w_ada[0], b_s, "mod_part")
    pack2, offs2 = _pack_rows([mod_part])
    (mod_parts,) = _unpack(_gather8(pack2, "gather_mod"), offs2)
    mod_all = jnp.concatenate([mod_parts[2 * j].reshape(N_DEV, na) for j in range(N_CHIP)], axis=1)
    mod = lax.dynamic_slice(mod_all, (dev, 0), (1, N_MOD * d))
    sh_m, sc_m, gt_m, sh_f, sc_f, gt_f = [mod[:, i * d:(i + 1) * d] for i in range(N_MOD)]

    h1 = _norm_mod(x2d, pre_mix_norm, sc_m, sh_m, "norm_mod_mix")
    (g_in,), h1 = _gather_weights([jnp.transpose(w_in[0]).astype(BF16)], "gather_w_in", sequencer_id=5, after=h1)
    late, g_in = lax.optimization_barrier(([w_out[0].astype(BF16), w_ff1[0].astype(BF16), w_ff2[0].astype(BF16)], g_in))
    g_out, g_ff1, g_ff2 = _gather_weights(late, "gather_weights_late", sequencer_id=1)
    w_in_f = jnp.pad(g_in.reshape(in_cols, d), ((0, np_cols - in_cols), (0, 0)))
    w_out_f = g_out.reshape(d, d)
    w_ff2_f = g_ff2.reshape(d_ff, d)

    proj = _matmul(h1, w_in_f, "nt", F32, "mm_in", tn=768)
    l0, l1 = hg_lb_logits[0].reshape(n_heads, 1, HEAD), hg_lb_logits[1].reshape(n_heads, 1, HEAD)
    o_hg, hg_saved = _hgrn2_fwd(proj, l0, l1, hg_norm, n_heads, "hgrn2_fwd")
    qkv = _conv_fwd(proj, conv_w, 4 * n_heads, "conv_fwd")
    alog_row, dtb_row = _pad_lanes(gdn_a_log, LANES), _pad_lanes(gdn_dt_bias, LANES)
    o_gdn, gdn_saved = _gdn_fwd(qkv, proj, alog_row, dtb_row, gdn_norm, n_heads, "gdn_fwd")
    o_cat = jnp.concatenate([o_hg, o_gdn], axis=1)
    y1 = _matmul(o_cat, w_out_f, "nn", F32, "mm_out")
    x_mid, h2 = _resid_norm_mod(x2d, y1, post_mix_norm, gt_m, pre_ffn_norm, sc_f, sh_f, "resid_mix_norm_mod_ffn")

    relu_a1, r1 = _matmul(h2, g_ff1, "nn", BF16, "mm_ff1", relu2=True, b_split=True)
    y2 = _matmul(r1, w_ff2_f, "nn", F32, "mm_ff2")
    d_out, loss_row, dy2, d_gt_f, d_post_ffn = _loss_head(x_mid, y2, post_ffn_norm, gt_f, tgt, "loss_head")

    in_sh = in_cols // N_CHIP
    ff_sh = d_ff // N_CHIP
    my_half = jnp.reshape(ac, (1,)).astype(jnp.int32)

    def start_reduce(by_chip, tag, collective_id):
        to_sib = [_row_half_to_bf16(a, 1 - my_half, None, f"sibling_half_{tag}{i}") for i, a in enumerate(by_chip)]
        from_sib = _sibling_exchange(to_sib, f"sibling_partials_{tag}")
        chip_part = [_row_half_to_bf16(a, my_half, s, f"add_halves_{tag}{i}") for i, (a, s) in enumerate(zip(by_chip, from_sib))]
        return _sequencer_chip_exchange(chip_part, f"scatter_grads_{tag}", collective_id)

    gw_ff2 = _matmul(r1, dy2, "tn", BF16, "mm_ff2_dw")
    gw_ff2, dy2 = lax.optimization_barrier((gw_ff2, dy2))
    da1 = _matmul(dy2, w_ff2_f, "nt", BF16, "mm_ff2_dx", times=relu_a1)
    gw_ff1 = _matmul(h2, da1, "tn", BF16, "mm_ff1_dw", out_split=True)
    gw_ff1, da1 = lax.optimization_barrier((gw_ff1, da1))
    recv_ff2, recv_ff1 = _sequencer_chip_exchange([gw_ff2.reshape(N_CHIP, ff_sh, d), gw_ff1], "scatter_grads_ff", 2)
    dh2 = _matmul(da1, g_ff1, "nt", BF16, "mm_ff1_dx", b_split=True)
    d_mid, d_pre_ffn, d_sc_f, d_sh_f = _norm_mod_bwd(x_mid, pre_ffn_norm, sc_f, dh2, d_out, "norm_mod_ffn_bwd")

    dy1, d_gt_m, d_post_mix = _resid_bwd(d_mid, y1, post_mix_norm, gt_m, "resid_mix_bwd")
    gw_out = _matmul(o_cat, dy1, "tn", BF16, "mm_out_dw")
    gw_out, dy1 = lax.optimization_barrier((gw_out, dy1))
    (recv_out,) = _sequencer_chip_exchange([gw_out.reshape(N_CHIP, d // N_CHIP, d)], "scatter_grads_out", 3)
    d_ocat = _matmul(dy1, w_out_f, "nt", F32, "mm_out_dx")
    dp_hg, dl0, dl1, d_hg_norm = _hgrn2_bwd(proj, l0, l1, hg_norm, hg_saved, d_ocat, n_heads, "hgrn2_bwd")
    dqkv, dg_g, dab, d_alog, d_dtb, d_gdn_norm = _gdn_bwd(
        qkv, proj, alog_row, dtb_row, gdn_norm, gdn_saved, d_ocat, n_heads, "gdn_bwd")
    du, d_conv_w = _conv_bwd(proj, conv_w, dqkv, 4 * n_heads, "conv_bwd")
    dproj = jnp.concatenate([dp_hg, du, dg_g, dab.astype(BF16), jnp.zeros((t, LANES), BF16)], axis=1)
    gw_in = _matmul(dproj, h1, "tn", F32, "mm_in_dw", tm=768)
    (recv_in,) = start_reduce([gw_in[:in_cols].reshape(N_CHIP, in_sh, d)], "in", 4)
    dh1 = _matmul(dproj, w_in_f, "nn", BF16, "mm_in_dx", tk=2816)
    grad_x, d_pre_mix, d_sc_m, d_sh_m = _norm_mod_bwd(x2d, pre_mix_norm, sc_m, dh1, d_mid, "norm_mod_mix_bwd")

    d_mod = jnp.concatenate([d_sh_m, d_sc_m, d_gt_m, d_sh_f, d_sc_f, d_gt_f], axis=1)
    d_lb_logits = jnp.stack([dl0.reshape(n_heads, HEAD), dl1.reshape(n_heads, HEAD)])
    pack3, offs3 = _pack_rows([loss_row[0, :1], d_pre_mix, d_post_mix, d_pre_ffn, d_post_ffn, d_lb_logits, d_hg_norm,
                               d_conv_w, d_alog[0, :n_heads], d_dtb[0, :n_heads], d_gdn_norm, d_mod])
    parts = _unpack(_gather8(pack3, "gather_vec_grads"), offs3)
    sums = [_sum_devices(p) for p in parts[:-1]]
    loss = sums[0][0]
    dmod_all = parts[-1]
    g_b_ada = _sum_devices(dmod_all).reshape(1, N_MOD * d)
    g_conv_full = sums[7].reshape(CONV_K, N_CHIP * conv_sh)
    g_conv = lax.dynamic_slice(g_conv_full, (0, chip * conv_sh), (CONV_K, conv_sh))
    dmod_chip = lax.dynamic_slice(dmod_all, (0, chip * na), (N_DEV, na))

    sum_ff2 = _sum_chips(recv_ff2, "sum_chips_ff2")
    recv_ff1, sum_ff2 = lax.optimization_barrier((recv_ff1, sum_ff2))
    sum_ff1 = _sum_chips(recv_ff1, "sum_chips_ff1")
    recv_out, sum_ff1 = lax.optimization_barrier((recv_out, sum_ff1))
    sum_out = _sum_chips(recv_out, "sum_chips_out")
    recv_in, sum_out = lax.optimization_barrier((recv_in, sum_out))
    mine = [_sum_chips(recv_in, "sum_chips_in"), sum_out, sum_ff1, sum_ff2]
    theirs = _sibling_exchange(mine, "sibling_grads")

    big = {}
    for i, (nm, w_, m_, v_) in enumerate([("w_in", w_in, m_w_in, v_w_in), ("w_out", w_out, m_w_out, v_w_out),
                                          ("w_ff1", w_ff1, m_w_ff1, v_w_ff1), ("w_ff2", w_ff2, m_w_ff2, v_w_ff2)]):
        if nm == "w_in":
            res_t = _adamw(jnp.transpose(w_[0]), [mine[i], theirs[i]], jnp.transpose(m_[0]), jnp.transpose(v_[0]),
                           f"adamw_{nm}", by_core=True)
            big[nm] = [jnp.transpose(o)[None] for o in res_t]
        else:
            big[nm] = [o[None] for o in _adamw(w_[0], [mine[i], theirs[i]], m_[0], v_[0], f"adamw_{nm}")]
    big["w_ada"] = [o[None] for o in _adamw(w_ada[0], [c_all, dmod_chip], m_w_ada[0], v_w_ada[0], "adamw_w_ada", outer=True)]

    small_names = ["b_ada", "pre_mix_norm", "post_mix_norm", "pre_ffn_norm", "post_ffn_norm", "hg_lb_logits", "hg_norm",
                   "gdn_conv_w", "gdn_a_log", "gdn_dt_bias", "gdn_norm"]
    small_w = [b_ada, pre_mix_norm, post_mix_norm, pre_ffn_norm, post_ffn_norm, hg_lb_logits, hg_norm, gdn_conv_w,
               gdn_a_log, gdn_dt_bias, gdn_norm]
    small_m = [m_b_ada, m_pre_mix_norm, m_post_mix_norm, m_pre_ffn_norm, m_post_ffn_norm, m_hg_lb_logits, m_hg_norm,
               m_gdn_conv_w, m_gdn_a_log, m_gdn_dt_bias, m_gdn_norm]
    small_v = [v_b_ada, v_pre_mix_norm, v_post_mix_norm, v_pre_ffn_norm, v_post_ffn_norm, v_hg_lb_logits, v_hg_norm,
               v_gdn_conv_w, v_gdn_a_log, v_gdn_dt_bias, v_gdn_norm]
    small_g = [g_b_ada, sums[1], sums[2], sums[3], sums[4], sums[5], sums[6], g_conv, sums[8], sums[9], sums[10]]
    pw, offs_s = _pack_rows(small_w)
    pg, _ = _pack_rows(small_g)
    pm, _ = _pack_rows(small_m)
    pv, _ = _pack_rows(small_v)
    packed = _adamw(pw, [pg], pm, pv, "adamw_vectors")
    small = {}
    for nm, w_, (o, n) in zip(small_names, small_w, offs_s):
        small[nm] = [p.reshape(-1)[o:o + n].reshape(w_.shape) for p in packed]

    order = ["w_ada", "b_ada", "pre_mix_norm", "post_mix_norm", "pre_ffn_norm", "post_ffn_norm", "w_in", "hg_lb_logits",
             "hg_norm", "gdn_conv_w", "gdn_a_log", "gdn_dt_bias", "gdn_norm", "w_out", "w_ff1", "w_ff2"]
    res = {**big, **small}
    outs = [loss, grad_x[None]]
    for k in range(4):
        outs += [res[nm][k] for nm in order]
    return tuple(outs)
```

```python
import functools
import math

import jax
import jax.numpy as jnp
from jax import lax
from jax.experimental import pallas as pl
from jax.experimental.pallas import tpu as pltpu
from jax.experimental.pallas import tpu_sc as plsc

F32 = jnp.float32
BF16 = jnp.bfloat16
HI = lax.Precision.HIGHEST
MESH = pl.DeviceIdType.MESH

LANES = 128
SUBLANES = 8
VMEM_LIMIT = 48 * 1024 * 1024
EPS = 1e-6
HEAD = 128
CONV_K = 4
GDN_CHUNK = 64
GDN_INV_BLOCK = 16
HG_SUB = 16
HG_BLOCK = 128
HG_FWD_GROUP = 8
HG_BWD_GROUP = 4
GDN_FWD_GROUP = 8
GDN_BWD_GROUP = 8
N_MOD = 6
N_DEV = 8
N_CHIP = 4

ADAM_LR = 0.001
ADAM_B1 = 0.9
ADAM_B2 = 0.999
ADAM_EPS = 1e-08
ADAM_WD = 0.01
ADAM_STEP = 10

NT_DIMS = (((1,), (1,)), ((), ()))
TN_DIMS = (((0,), (0,)), ((), ()))


def _tile(dim, target, align):
    if dim <= target:
        return dim
    best = dim
    t = align
    while t <= target:
        if dim % t == 0:
            best = t
        t += align
    return best


def _elementwise_tiles(r, c):
    tc = _tile(c, 1024, LANES)
    tr = _tile(r, max(16, (256 * 1024) // tc // 16 * 16), 16)
    if tr == r and r * tc > 512 * 1024:
        tc = _tile(c, max(LANES, (256 * 1024) // r // LANES * LANES), LANES)
    return tr, tc


def _params(sem):
    return pltpu.CompilerParams(dimension_semantics=sem, vmem_limit_bytes=VMEM_LIMIT)


def _silu(x):
    return x * jax.nn.sigmoid(x)


def _softplus(x):
    pos = x > 0
    return jnp.where(pos, x, 0.0) + jnp.log(1.0 + jnp.exp(jnp.where(pos, -x, x)))


def _rms_scale(x):
    return lax.rsqrt(jnp.mean(x * x, axis=-1, keepdims=True) + EPS)


def _gather8(x_shard, name):
    m_per, n = x_shard.shape
    assert m_per % SUBLANES == 0 and n % LANES == 0

    def body(x_ref, out_ref, send_sems, recv_sems, local_sem):
        x, y, c = lax.axis_index("x"), lax.axis_index("y"), lax.axis_index("c")
        me, sibling = (x, y, c), (x, y, 1 - c)
        chips = [(1 - x, y), (x, 1 - y), (1 - x, 1 - y)]

        def rows(px, py, pc):
            return out_ref.at[pl.ds((4 * px + 2 * py + pc) * m_per, m_per), :]

        def copy(k, block, to, src=None):
            return pltpu.make_async_remote_copy(
                src_ref=rows(*block) if src is None else src, dst_ref=rows(*block),
                send_sem=send_sems.at[k], recv_sem=recv_sems.at[k], device_id=to, device_id_type=MESH)

        mine = pltpu.make_async_copy(x_ref, rows(*me), local_sem)
        mine.start()
        first = [copy(0, me, sibling, src=x_ref)]
        first += [copy(1 + j, me, (*chip, c), src=x_ref) for j, chip in enumerate(chips)]
        for cp in first:
            cp.start()
        passed = [copy(4 + j, (*chip, c), sibling) for j, chip in enumerate(chips)]
        for j, chip in enumerate(chips):
            copy(1 + j, (*chip, c), me).wait_recv()
            passed[j].start()
        copy(0, sibling, me).wait_recv()
        for j, chip in enumerate(chips):
            copy(4 + j, (*chip, 1 - c), me).wait_recv()
        for cp in first + passed:
            cp.wait_send()
        mine.wait()

    return pl.pallas_call(
        body, name=name,
        out_shape=jax.ShapeDtypeStruct((N_DEV * m_per, n), x_shard.dtype),
        in_specs=[pl.BlockSpec(memory_space=pltpu.VMEM)],
        out_specs=pl.BlockSpec(memory_space=pltpu.VMEM),
        scratch_shapes=[pltpu.SemaphoreType.DMA((7,)), pltpu.SemaphoreType.DMA((7,)), pltpu.SemaphoreType.DMA],
        compiler_params=pltpu.CompilerParams(vmem_limit_bytes=VMEM_LIMIT),
    )(x_shard)


def _gather_weights(arrs, name, sequencer_id=None, after=None):
    n = len(arrs)
    out_shapes = [jax.ShapeDtypeStruct((N_CHIP,) + a.shape, a.dtype) for a in arrs]

    def body(*refs):
        ins, outs = refs[:n], refs[n:2 * n]
        ici_send, ici_recv, d2d_send, d2d_recv = refs[2 * n:]
        if sequencer_id is not None:
            chips, sib = _chip_peers()
            _handshake(chips + [sib])
        x, y, c = lax.axis_index("x"), lax.axis_index("y"), lax.axis_index("c")
        me = 2 * x + y
        peers = [(1 - x, y), (x, 1 - y), (1 - x, 1 - y)]

        def half(a, cc):
            r = arrs[a].shape[0]
            cut = r // 32 * 16
            return pl.ds(0, cut) if cc == 0 else pl.ds(cut, r - cut)

        def exchange(mine):
            sibling = (x, y, 1 - mine)
            sent = []
            for a in range(n):
                for k, (px, py) in enumerate(peers):
                    cp = pltpu.make_async_remote_copy(
                        src_ref=ins[a].at[half(a, mine)], dst_ref=outs[a].at[me, half(a, mine)],
                        send_sem=ici_send.at[3 * a + k], recv_sem=ici_recv.at[3 * a + k],
                        device_id=(px, py, mine), device_id_type=MESH)
                    cp.start()
                    sent.append(cp)
            for a in range(n):
                for k, (px, py) in enumerate(peers):
                    landed = outs[a].at[2 * px + py, half(a, mine)]
                    pltpu.make_async_remote_copy(
                        src_ref=landed, dst_ref=landed, send_sem=ici_send.at[3 * a + k], recv_sem=ici_recv.at[3 * a + k],
                        device_id=(px, py, mine), device_id_type=MESH).wait_recv()
                    fwd = pltpu.make_async_remote_copy(
                        src_ref=landed, dst_ref=landed, send_sem=d2d_send.at[3 * a + k], recv_sem=d2d_recv.at[3 * a + k],
                        device_id=sibling, device_id_type=MESH)
                    fwd.start()
                    sent.append(fwd)
            for a in range(n):
                for k, (px, py) in enumerate(peers):
                    passed = outs[a].at[2 * px + py, half(a, 1 - mine)]
                    pltpu.make_async_remote_copy(
                        src_ref=passed, dst_ref=passed, send_sem=d2d_send.at[3 * a + k], recv_sem=d2d_recv.at[3 * a + k],
                        device_id=sibling, device_id_type=MESH).wait_recv()
            for cp in sent:
                cp.wait_send()

        for core in (0, 1):
            pl.when(c == core)(functools.partial(exchange, core))

    sems = [pltpu.SemaphoreType.DMA((3 * n,))] * 4
    if sequencer_id is None:
        hbm = pl.BlockSpec(memory_space=pltpu.HBM)
        gathered = pl.pallas_call(body, name=name, out_shape=out_shapes, in_specs=[hbm] * n, out_specs=[hbm] * n,
                                  scratch_shapes=sems)(*arrs)
    else:
        gathered = pl.kernel(body, out_type=out_shapes, mesh=plsc.ScalarSubcoreMesh(axis_name="sequencer", num_cores=1),
                             name=name, scratch_types=sems,
                             compiler_params=pltpu.CompilerParams(collective_id=sequencer_id))(*arrs)
    if after is not None:
        gathered, after = lax.optimization_barrier((gathered, after))
    chip = 2 * lax.axis_index("x") + lax.axis_index("y")
    filled = [lax.dynamic_update_slice(g, a[None], (chip, 0, 0)) for g, a in zip(gathered, arrs)]
    return filled if after is None else (filled, after)


def _chip_peers():
    x, y, c = lax.axis_index("x"), lax.axis_index("y"), lax.axis_index("c")
    return [(1 - x, y, c), (x, 1 - y, c), (1 - x, 1 - y, c)], (x, y, 1 - c)


def _handshake(peers):
    barrier = pltpu.get_barrier_semaphore()
    for peer in peers:
        pl.semaphore_signal(barrier, inc=1, device_id=peer, device_id_type=MESH)
    pl.semaphore_wait(barrier, len(peers))


def _sequencer_chip_exchange(arrs, name, collective_id):
    n = len(arrs)
    out_types = [jax.ShapeDtypeStruct(a.shape, a.dtype) for a in arrs]

    def body(*refs):
        ins, outs = refs[:n], refs[n:2 * n]
        send_sems, recv_sems = refs[2 * n:]
        chips, _ = _chip_peers()
        _handshake(chips)
        me = 2 * lax.axis_index("x") + lax.axis_index("y")
        sent = []
        for a in range(n):
            for k, peer in enumerate(chips):
                cp = pltpu.make_async_remote_copy(
                    src_ref=ins[a].at[2 * peer[0] + peer[1]], dst_ref=outs[a].at[me],
                    send_sem=send_sems.at[3 * a + k], recv_sem=recv_sems.at[3 * a + k], device_id=peer, device_id_type=MESH)
                cp.start()
                sent.append(cp)
        for a in range(n):
            for k, peer in enumerate(chips):
                landed = outs[a].at[2 * peer[0] + peer[1]]
                pltpu.make_async_remote_copy(
                    src_ref=landed, dst_ref=landed, send_sem=send_sems.at[3 * a + k], recv_sem=recv_sems.at[3 * a + k],
                    device_id=peer, device_id_type=MESH).wait_recv()
        for cp in sent:
            cp.wait_send()

    received = pl.kernel(
        body, out_type=out_types, mesh=plsc.ScalarSubcoreMesh(axis_name="sequencer", num_cores=1), name=name,
        scratch_types=[pltpu.SemaphoreType.DMA((3 * n,))] * 2,
        compiler_params=pltpu.CompilerParams(collective_id=collective_id),
    )(*arrs)
    chip = 2 * lax.axis_index("x") + lax.axis_index("y")
    return [lax.dynamic_update_slice(r, lax.dynamic_slice(a, (chip, 0, 0), (1,) + a.shape[1:]), (chip, 0, 0))
            for r, a in zip(received, arrs)]


def _sibling_exchange(arrs, name, sequencer_id=None):
    n = len(arrs)

    def body(*refs):
        ins, outs = refs[:n], refs[n:2 * n]
        send_sems, recv_sems = refs[2 * n:]
        sibling = (lax.axis_index("x"), lax.axis_index("y"), 1 - lax.axis_index("c"))
        if sequencer_id is not None:
            _handshake([sibling])
        cps = []
        for a in range(n):
            cp = pltpu.make_async_remote_copy(src_ref=ins[a], dst_ref=outs[a], send_sem=send_sems.at[a],
                                              recv_sem=recv_sems.at[a], device_id=sibling, device_id_type=MESH)
            cp.start()
            cps.append(cp)
        for cp in cps:
            cp.wait_recv()
        for cp in cps:
            cp.wait_send()

    out_shapes = [jax.ShapeDtypeStruct(a.shape, a.dtype) for a in arrs]
    sems = [pltpu.SemaphoreType.DMA((n,)), pltpu.SemaphoreType.DMA((n,))]
    if sequencer_id is not None:
        return pl.kernel(body, out_type=out_shapes, mesh=plsc.ScalarSubcoreMesh(axis_name="sequencer", num_cores=1), name=name,
                         scratch_types=sems, compiler_params=pltpu.CompilerParams(collective_id=sequencer_id))(*arrs)
    hbm = pl.BlockSpec(memory_space=pltpu.HBM)
    return pl.pallas_call(body, name=name, out_shape=out_shapes, in_specs=[hbm] * n, out_specs=[hbm] * n,
                          scratch_shapes=sems)(*arrs)


def _matmul(a, b, mode, out_dtype, name, tm=1024, tn=1024, tk=2048, relu2=False, times=None, b_split=False,
            out_split=False):
    b_shape = (b.shape[1], b.shape[2] * N_CHIP) if b_split else b.shape
    if mode == "nn":
        (m, k), (k2, n) = a.shape, b_shape
    elif mode == "nt":
        (m, k), (n, k2) = a.shape, b_shape
    else:
        (k, m), (k2, n) = a.shape, b_shape
    assert k == k2, (a.shape, b.shape, mode)
    n_cut = n // N_CHIP if (out_split or (b_split and mode != "nt")) else n
    k_cut = k // N_CHIP if (b_split and mode == "nt") else k
    tm, tn, tk = _tile(m, tm, LANES), _tile(n_cut, tn, LANES), _tile(k_cut, tk, LANES)
    assert n_cut % tn == 0 and k_cut % tk == 0 and m % tm == 0, (name, m, n, k, tm, tn, tk)
    nk = k // tk
    nbc, nkc = n_cut // tn, k_cut // tk
    n_in = 2 if times is None else 3
    n_out = 2 if relu2 else 1

    def product(a_ref, b_ref):
        if mode == "nn":
            return jnp.dot(a_ref[...], b_ref[...], preferred_element_type=F32)
        return lax.dot_general(a_ref[...], b_ref[...], NT_DIMS if mode == "nt" else TN_DIMS, preferred_element_type=F32)

    def finish(p, refs, o_refs):
        if relu2:
            p = jnp.maximum(p, 0.0)
            o_refs[0][...] = p.astype(o_refs[0].dtype)
            o_refs[1][...] = (p * p).astype(o_refs[1].dtype)
        elif times is not None:
            o_refs[0][...] = (2.0 * refs[2][...].astype(F32) * p).astype(o_refs[0].dtype)
        else:
            o_refs[0][...] = p.astype(o_refs[0].dtype)

    def body(*refs):
        o_refs = refs[n_in:n_in + n_out]
        if nk == 1:
            finish(product(refs[0], refs[1]), refs, o_refs)
            return
        acc_ref = refs[n_in + n_out]
        kk = pl.program_id(2)

        @pl.when(kk == 0)
        def _():
            acc_ref[...] = product(refs[0], refs[1])

        @pl.when((kk > 0) & (kk < nk - 1))
        def _():
            acc_ref[...] += product(refs[0], refs[1])

        @pl.when(kk == nk - 1)
        def _():
            finish(acc_ref[...] + product(refs[0], refs[1]), refs, o_refs)

    if mode == "tn":
        a_spec = pl.BlockSpec((tk, tm), lambda i, j, kk: (kk, i))
    else:
        a_spec = pl.BlockSpec((tm, tk), lambda i, j, kk: (i, kk))
    if mode == "nt":
        b_spec = (pl.BlockSpec((None, tn, tk), lambda i, j, kk: (kk // nkc, j, kk % nkc)) if b_split
                  else pl.BlockSpec((tn, tk), lambda i, j, kk: (j, kk)))
    else:
        b_spec = (pl.BlockSpec((None, tk, tn), lambda i, j, kk: (j // nbc, kk, j % nbc)) if b_split
                  else pl.BlockSpec((tk, tn), lambda i, j, kk: (kk, j)))
    mn_spec = pl.BlockSpec((tm, tn), lambda i, j, kk: (i, j))
    if out_split:
        o_spec = pl.BlockSpec((None, tm, tn), lambda i, j, kk: (j // nbc, i, j % nbc))
        o_shape = jax.ShapeDtypeStruct((N_CHIP, m, n_cut), out_dtype)
    else:
        o_spec, o_shape = mn_spec, jax.ShapeDtypeStruct((m, n), out_dtype)
    out = pl.pallas_call(
        body, name=name, grid=(m // tm, n // tn, nk), in_specs=[a_spec, b_spec] + [mn_spec] * (n_in - 2),
        out_specs=[o_spec] * n_out, out_shape=[o_shape] * n_out,
        scratch_shapes=[] if nk == 1 else [pltpu.VMEM((tm, tn), F32)],
        compiler_params=_params(("parallel", "parallel", "arbitrary")),
    )(*((a, b) if times is None else (a, b, times)))
    return out if relu2 else out[0]


def _mod_part(c_all, w_s, b_s, name):
    d, na = w_s.shape
    tn = _tile(na, 512, LANES)

    def body(c_ref, w_ref, b_ref, o_ref):
        ca = _silu(c_ref[...]).astype(BF16)
        o_ref[...] = jnp.dot(ca, w_ref[...].astype(BF16), preferred_element_type=F32) + b_ref[...]

    return pl.pallas_call(
        body, name=name, grid=(na // tn,),
        in_specs=[pl.BlockSpec((N_DEV, d), lambda j: (0, 0)), pl.BlockSpec((d, tn), lambda j: (0, j)),
                  pl.BlockSpec((1, tn), lambda j: (0, j))],
        out_specs=pl.BlockSpec((N_DEV, tn), lambda j: (0, j)),
        out_shape=jax.ShapeDtypeStruct((N_DEV, na), F32), compiler_params=_params(("parallel",)),
    )(c_all, w_s, b_s)


def _row_specs(tb, d, n_full, n_vec):
    full = pl.BlockSpec((tb, d), lambda i: (i, 0))
    vec = pl.BlockSpec((1, d), lambda i: (0, 0))
    return [full] * n_full + [vec] * n_vec


def _norm_mod(x, w, sc, sh, name):
    t, d = x.shape
    tb = _tile(t, 256, SUBLANES)

    def body(x_ref, w_ref, sc_ref, sh_ref, o_ref):
        xv = x_ref[...]
        o_ref[...] = (xv * _rms_scale(xv) * w_ref[...] * (1.0 + sc_ref[...]) + sh_ref[...]).astype(o_ref.dtype)

    return pl.pallas_call(
        body, name=name, grid=(t // tb,), in_specs=_row_specs(tb, d, 1, 3),
        out_specs=pl.BlockSpec((tb, d), lambda i: (i, 0)), out_shape=jax.ShapeDtypeStruct((t, d), BF16),
        compiler_params=_params(("parallel",)),
    )(x, w, sc, sh)


def _norm_mod_bwd(x, w, sc, dh, dres, name):
    t, d = x.shape
    tb = _tile(t, 256, SUBLANES)

    def body(x_ref, w_ref, sc_ref, dh_ref, dres_ref, dx_ref, dw_ref, dsc_ref, dsh_ref):
        @pl.when(pl.program_id(0) == 0)
        def _():
            dw_ref[...] = jnp.zeros_like(dw_ref)
            dsc_ref[...] = jnp.zeros_like(dsc_ref)
            dsh_ref[...] = jnp.zeros_like(dsh_ref)

        xv = x_ref[...]
        r = _rms_scale(xv)
        xn = xv * r
        g = dh_ref[...].astype(F32)
        wv, one_sc = w_ref[...], 1.0 + sc_ref[...]
        gxn = g * xn
        dsh_ref[...] += jnp.sum(g, axis=0, keepdims=True)
        dsc_ref[...] += jnp.sum(gxn, axis=0, keepdims=True) * wv
        dw_ref[...] += jnp.sum(gxn, axis=0, keepdims=True) * one_sc
        dxn = g * (wv * one_sc)
        dx_ref[...] = dres_ref[...] + r * (dxn - xn * jnp.mean(dxn * xn, axis=-1, keepdims=True))

    vec_out = pl.BlockSpec((1, d), lambda i: (0, 0))
    return pl.pallas_call(
        body, name=name, grid=(t // tb,),
        in_specs=[pl.BlockSpec((tb, d), lambda i: (i, 0)), pl.BlockSpec((1, d), lambda i: (0, 0)),
                  pl.BlockSpec((1, d), lambda i: (0, 0)), pl.BlockSpec((tb, d), lambda i: (i, 0)),
                  pl.BlockSpec((tb, d), lambda i: (i, 0))],
        out_specs=[pl.BlockSpec((tb, d), lambda i: (i, 0)), vec_out, vec_out, vec_out],
        out_shape=[jax.ShapeDtypeStruct((t, d), F32)] + [jax.ShapeDtypeStruct((1, d), F32)] * 3,
        compiler_params=_params(("arbitrary",)),
    )(x, w, sc, dh, dres)


def _resid_norm_mod(x, y, w, gt, w2, sc, sh, name):
    t, d = x.shape
    tb = _tile(t, 256, SUBLANES)

    def body(x_ref, y_ref, w_ref, gt_ref, w2_ref, sc_ref, sh_ref, o_ref, h_ref):
        yv = y_ref[...]
        x2 = x_ref[...] + gt_ref[...] * (yv * _rms_scale(yv) * w_ref[...])
        o_ref[...] = x2
        h_ref[...] = (x2 * _rms_scale(x2) * w2_ref[...] * (1.0 + sc_ref[...]) + sh_ref[...]).astype(h_ref.dtype)

    full = pl.BlockSpec((tb, d), lambda i: (i, 0))
    return pl.pallas_call(
        body, name=name, grid=(t // tb,), in_specs=_row_specs(tb, d, 2, 5), out_specs=[full, full],
        out_shape=[jax.ShapeDtypeStruct((t, d), F32), jax.ShapeDtypeStruct((t, d), BF16)],
        compiler_params=_params(("parallel",)),
    )(x, y, w, gt, w2, sc, sh)


def _loss_head(x2, y2, w, gt, target, name):
    t, d = x2.shape
    tb = _tile(t, 256, SUBLANES)

    def body(x_ref, y_ref, tg_ref, w_ref, gt_ref, do_ref, loss_ref, dy_ref, dgt_ref, dw_ref):
        @pl.when(pl.program_id(0) == 0)
        def _():
            loss_ref[...] = jnp.zeros_like(loss_ref)
            dgt_ref[...] = jnp.zeros_like(dgt_ref)
            dw_ref[...] = jnp.zeros_like(dw_ref)

        yv = y_ref[...]
        r = _rms_scale(yv)
        yn = yv * r
        wv, gtv = w_ref[...], gt_ref[...]
        err = x_ref[...] + gtv * (yn * wv) - tg_ref[...]
        g = err * (1.0 / d)
        do_ref[...] = g
        per_tok = jnp.mean(err * err, axis=-1, keepdims=True)
        loss_ref[...] += 0.5 * jnp.sum(per_tok, axis=0, keepdims=True)
        gyn = jnp.sum(g * yn, axis=0, keepdims=True)
        dgt_ref[...] += gyn * wv
        dw_ref[...] += gyn * gtv
        dyn = g * (gtv * wv)
        dy_ref[...] = (r * (dyn - yn * jnp.mean(dyn * yn, axis=-1, keepdims=True))).astype(dy_ref.dtype)

    full = pl.BlockSpec((tb, d), lambda i: (i, 0))
    vec_out = pl.BlockSpec((1, d), lambda i: (0, 0))
    return pl.pallas_call(
        body, name=name, grid=(t // tb,), in_specs=_row_specs(tb, d, 3, 2),
        out_specs=[full, pl.BlockSpec((1, LANES), lambda i: (0, 0)), full, vec_out, vec_out],
        out_shape=[jax.ShapeDtypeStruct((t, d), F32), jax.ShapeDtypeStruct((1, LANES), F32), jax.ShapeDtypeStruct((t, d), BF16),
                   jax.ShapeDtypeStruct((1, d), F32), jax.ShapeDtypeStruct((1, d), F32)],
        compiler_params=_params(("arbitrary",)),
    )(x2, y2, target, w, gt)


def _resid_bwd(dout, y, w, gt, name):
    t, d = y.shape
    tb = _tile(t, 256, SUBLANES)

    def body(do_ref, y_ref, w_ref, gt_ref, dy_ref, dgt_ref, dw_ref):
        @pl.when(pl.program_id(0) == 0)
        def _():
            dgt_ref[...] = jnp.zeros_like(dgt_ref)
            dw_ref[...] = jnp.zeros_like(dw_ref)

        yv, g = y_ref[...], do_ref[...]
        r = _rms_scale(yv)
        yn = yv * r
        wv, gtv = w_ref[...], gt_ref[...]
        gyn = jnp.sum(g * yn, axis=0, keepdims=True)
        dgt_ref[...] += gyn * wv
        dw_ref[...] += gyn * gtv
        dyn = g * (gtv * wv)
        dy_ref[...] = (r * (dyn - yn * jnp.mean(dyn * yn, axis=-1, keepdims=True))).astype(dy_ref.dtype)

    vec_out = pl.BlockSpec((1, d), lambda i: (0, 0))
    return pl.pallas_call(
        body, name=name, grid=(t // tb,), in_specs=_row_specs(tb, d, 2, 2),
        out_specs=[pl.BlockSpec((tb, d), lambda i: (i, 0)), vec_out, vec_out],
        out_shape=[jax.ShapeDtypeStruct((t, d), BF16)] + [jax.ShapeDtypeStruct((1, d), F32)] * 2,
        compiler_params=_params(("arbitrary",)),
    )(dout, y, w, gt)


def _row_half_to_bf16(full, which, sib, name):
    n, r, c = full.shape
    by_rows = r % 32 == 0
    r, c = (r // 2, c) if by_rows else (r, c // 2)
    tr, tc = _elementwise_tiles(r, c)
    nbh = (r // tr) if by_rows else (c // tc)

    def body(which_ref, a_ref, *rest):
        if sib is None:
            rest[0][...] = a_ref[...].astype(BF16)
        else:
            rest[1][...] = (a_ref[...] + rest[0][...].astype(F32)).astype(BF16)

    if by_rows:
        half_spec = pl.BlockSpec((1, tr, tc), lambda j, i, k, which_ref: (j, which_ref[0] * nbh + i, k))
    else:
        half_spec = pl.BlockSpec((1, tr, tc), lambda j, i, k, which_ref: (j, i, which_ref[0] * nbh + k))
    spec = pl.BlockSpec((1, tr, tc), lambda j, i, k, which_ref: (j, i, k))
    grid_spec = pltpu.PrefetchScalarGridSpec(
        num_scalar_prefetch=1, grid=(n, r // tr, c // tc), in_specs=[half_spec] + ([] if sib is None else [spec]), out_specs=spec)
    return pl.pallas_call(
        body, name=name, grid_spec=grid_spec, out_shape=jax.ShapeDtypeStruct((n, r, c), BF16),
        compiler_params=_params(("parallel", "parallel", "parallel")),
    )(which, full, *([] if sib is None else [sib]))


def _sum_chips(recv, name):
    _, r, c = recv.shape
    tr, tc = _elementwise_tiles(r, c)

    def body(x_ref, o_ref):
        acc = x_ref[0].astype(F32)
        for j in range(1, N_CHIP):
            acc = acc + x_ref[j].astype(F32)
        o_ref[...] = acc

    return pl.pallas_call(
        body, name=name, grid=(r // tr, c // tc), in_specs=[pl.BlockSpec((N_CHIP, tr, tc), lambda i, j: (0, i, j))],
        out_specs=pl.BlockSpec((tr, tc), lambda i, j: (i, j)), out_shape=jax.ShapeDtypeStruct((r, c), F32),
        compiler_params=_params(("parallel", "parallel")),
    )(recv)


def _adamw(w, g_parts, m, v, name, by_core=False, outer=False):
    r, c = w.shape
    by_rows = r % 32 == 0
    if by_core:
        tr, tc = _elementwise_tiles(*((r // 2, c) if by_rows else (r, c // 2)))
        nbh = (r // 2) // tr if by_rows else (c // 2) // tc
    else:
        tr, tc = _elementwise_tiles(r, c)
    n_g = len(g_parts)
    c1 = 1.0 / (1.0 - ADAM_B1 ** ADAM_STEP)
    c2 = 1.0 / (1.0 - ADAM_B2 ** ADAM_STEP)

    def body(*refs):
        w_ref, g_refs, m_ref, v_ref = refs[0], refs[1:1 + n_g], refs[1 + n_g], refs[2 + n_g]
        g_out, d_out, m_out, v_out = refs[3 + n_g:]
        if by_core:
            in_my_half = (pl.program_id(0 if by_rows else 1) // nbh) == lax.axis_index("c")
            g = jnp.where(in_my_half, g_refs[0][...], g_refs[1][...])
        elif outer:
            g = lax.dot_general(_silu(g_refs[0][...]), g_refs[1][...], TN_DIMS, precision=HI, preferred_element_type=F32)
        else:
            g = g_refs[0][...]
            for extra in g_refs[1:]:
                g = g + extra[...]
        mn = ADAM_B1 * m_ref[...] + (1.0 - ADAM_B1) * g
        vn = ADAM_B2 * v_ref[...] + (1.0 - ADAM_B2) * (g * g)
        g_out[...] = g
        m_out[...] = mn
        v_out[...] = vn
        d_out[...] = -ADAM_LR * ((mn * c1) / (jnp.sqrt(vn * c2) + ADAM_EPS) + ADAM_WD * w_ref[...])

    spec = pl.BlockSpec((tr, tc), lambda i, j: (i, j))
    if by_core:
        g_spec = pl.BlockSpec((tr, tc), (lambda i, j: (i % nbh, j)) if by_rows else (lambda i, j: (i, j % nbh)))
    else:
        g_spec = spec
    g_specs = [g_spec] * n_g
    if outer:
        g_specs = [pl.BlockSpec((N_DEV, tr), lambda i, j: (0, i)), pl.BlockSpec((N_DEV, tc), lambda i, j: (0, j))]
    return pl.pallas_call(
        body, name=name, grid=(r // tr, c // tc), in_specs=[spec] + g_specs + [spec] * 2, out_specs=[spec] * 4,
        out_shape=[jax.ShapeDtypeStruct((r, c), F32)] * 4, compiler_params=_params(("parallel", "parallel")),
    )(w, *g_parts, m, v)


def _conv_taps(u, t):
    rows = lax.broadcasted_iota(jnp.int32, u.shape, 0)
    return [u] + [jnp.where(rows >= dd, pltpu.roll(u, dd, 0), 0.0) for dd in range(1, CONV_K)]


def _conv_fwd(proj, conv_w, col0, name):
    t = proj.shape[0]
    ch = conv_w.shape[1]

    def body(u_ref, w_ref, o_ref):
        taps = _conv_taps(u_ref[...], t)
        wv = w_ref[...]
        y = taps[0] * wv[CONV_K - 1:CONV_K]
        for dd in range(1, CONV_K):
            y = y + taps[dd] * wv[CONV_K - 1 - dd:CONV_K - dd]
        o_ref[...] = _silu(y)

    return pl.pallas_call(
        body, name=name, grid=(ch // LANES,),
        in_specs=[pl.BlockSpec((t, LANES), lambda j: (0, col0 + j)), pl.BlockSpec((CONV_K, LANES), lambda j: (0, j))],
        out_specs=pl.BlockSpec((t, LANES), lambda j: (0, j)), out_shape=jax.ShapeDtypeStruct((t, ch), F32),
        compiler_params=_params(("parallel",)),
    )(proj, conv_w)


def _conv_bwd(proj, conv_w, ds, col0, name):
    t = proj.shape[0]
    ch = conv_w.shape[1]

    def body(u_ref, w_ref, ds_ref, du_ref, dw_ref):
        u = u_ref[...]
        taps = _conv_taps(u, t)
        wv = w_ref[...]
        y = taps[0] * wv[CONV_K - 1:CONV_K]
        for dd in range(1, CONV_K):
            y = y + taps[dd] * wv[CONV_K - 1 - dd:CONV_K - dd]
        sg = jax.nn.sigmoid(y)
        dy = ds_ref[...] * (sg * (1.0 + y * (1.0 - sg)))
        rows = lax.broadcasted_iota(jnp.int32, u.shape, 0)
        du = dy * wv[CONV_K - 1:CONV_K]
        for dd in range(1, CONV_K):
            ahead = jnp.where(rows < t - dd, pltpu.roll(dy, t - dd, 0), 0.0)
            du = du + ahead * wv[CONV_K - 1 - dd:CONV_K - dd]
        du_ref[...] = du.astype(du_ref.dtype)
        dws = [jnp.sum(dy * taps[CONV_K - 1 - j], axis=0, keepdims=True) for j in range(CONV_K)]
        dw_ref[...] = jnp.concatenate(dws, axis=0)

    return pl.pallas_call(
        body, name=name, grid=(ch // LANES,),
        in_specs=[pl.BlockSpec((t, LANES), lambda j: (0, col0 + j)), pl.BlockSpec((CONV_K, LANES), lambda j: (0, j)),
                  pl.BlockSpec((t, LANES), lambda j: (0, j))],
        out_specs=[pl.BlockSpec((t, LANES), lambda j: (0, j)), pl.BlockSpec((CONV_K, LANES), lambda j: (0, j))],
        out_shape=[jax.ShapeDtypeStruct((t, ch), BF16), jax.ShapeDtypeStruct((CONV_K, ch), F32)],
        compiler_params=_params(("parallel",)),
    )(proj, conv_w, ds)


def _hg_block(st, q, fl, vi, g, l0, l1, nw):
    hs = range(len(st))
    tb = q[0].shape[0]
    ln = HG_SUB
    lb = [jax.nn.sigmoid(l0[h] - l1[h]) for h in hs]
    rows = lax.broadcasted_iota(jnp.int32, (ln, HEAD), 0)
    tri = (lax.broadcasted_iota(jnp.int32, (ln, ln), 0) >= lax.broadcasted_iota(jnp.int32, (ln, ln), 1)).astype(F32)
    st = list(st)
    outs = [[] for _ in hs]
    for i in range(tb // ln):
        sl = slice(i * ln, (i + 1) * ln)
        qs, vs = [q[h][sl] for h in hs], [vi[h][sl] for h in hs]
        f = [lb[h] + (1.0 - lb[h]) * jax.nn.sigmoid(fl[h][sl]) for h in hs]
        k = [1.0 - f[h] for h in hs]
        b = [jnp.dot(tri, jnp.log(f[h]), precision=HI, preferred_element_type=F32) for h in hs]
        o = [lax.dot_general((qs[h] * jnp.exp(b[h])).astype(BF16), st[h].astype(BF16), NT_DIMS, preferred_element_type=F32)
             for h in hs]
        n_tiles = ln // SUBLANES
        acc = [o] + [[jnp.zeros((ln - SUBLANES * ti, HEAD), F32) for _ in hs] for ti in range(1, n_tiles)]
        for s in range(ln):
            ti = s // SUBLANES
            r0 = ti * SUBLANES
            e = [jnp.exp(jnp.where(rows[r0:] >= s, b[h][r0:] - b[h][s:s + 1], -1e30)) for h in hs]
            a = [jnp.sum(qs[h][r0:] * e[h] * k[h][s:s + 1], axis=-1, keepdims=True) for h in hs]
            acc[ti] = [acc[ti][h] + a[h] * vs[h][s:s + 1] for h in hs]
        o = [jnp.concatenate([sum(acc[ti][h][(j - ti) * SUBLANES:(j - ti + 1) * SUBLANES] for ti in range(j + 1))
                              for j in range(n_tiles)], axis=0) for h in hs]
        kt = [k[h] * jnp.exp(b[h][ln - 1:ln] - b[h]) for h in hs]
        upd = [lax.dot_general(vs[h].astype(BF16), kt[h].astype(BF16), TN_DIMS, preferred_element_type=F32) for h in hs]
        st = [st[h] * jnp.exp(b[h][ln - 1:ln]) + upd[h] for h in hs]
        for h in hs:
            outs[h].append(o[h])
    o = [jnp.concatenate(outs[h], axis=0) for h in hs]
    out = [o[h] * _rms_scale(o[h]) * nw * _silu(g[h]) for h in hs]
    return st, out


def _head_cols(h):
    return slice(h * HEAD, (h + 1) * HEAD)


def _head_groups(n_heads, group):
    g = min(group, n_heads)
    return [list(range(i, min(i + g, n_heads))) for i in range(0, n_heads, g)]


def _hg_in_specs(n_heads, tb, time_index):
    hw = n_heads * HEAD
    cols = [pl.BlockSpec((tb, hw), functools.partial(lambda part, j: (time_index(j), part), part)) for part in range(4)]
    head_rows = pl.BlockSpec((n_heads, 1, HEAD), lambda j: (0, 0, 0))
    return cols + [head_rows, head_rows, pl.BlockSpec((1, HEAD), lambda j: (0, 0))]


def _hgrn2_fwd(proj, l0, l1, nw, n_heads, name):
    t = proj.shape[0]
    hw = n_heads * HEAD
    tb = _tile(t, HG_BLOCK, HG_SUB)
    nb = t // tb

    def body(q_ref, f_ref, i_ref, g_ref, l0_ref, l1_ref, nw_ref, o_ref, save_ref, st_ref):
        @pl.when(pl.program_id(0) == 0)
        def _():
            st_ref[...] = jnp.zeros_like(st_ref)

        for hs in _head_groups(n_heads, HG_FWD_GROUP):
            st = [st_ref[h] for h in hs]
            for h, s in zip(hs, st):
                save_ref[h] = s
            st, out = _hg_block(st, *[[r[:, _head_cols(h)] for h in hs] for r in (q_ref, f_ref, i_ref, g_ref)],
                                [l0_ref[h] for h in hs], [l1_ref[h] for h in hs], nw_ref[...])
            for h, s, o in zip(hs, st, out):
                st_ref[h] = s
                o_ref[:, _head_cols(h)] = o.astype(o_ref.dtype)

    return pl.pallas_call(
        body, name=name, grid=(nb,), in_specs=_hg_in_specs(n_heads, tb, lambda j: j),
        out_specs=[pl.BlockSpec((tb, hw), lambda j: (j, 0)),
                   pl.BlockSpec((None, n_heads, HEAD, HEAD), lambda j: (j, 0, 0, 0))],
        out_shape=[jax.ShapeDtypeStruct((t, hw), BF16), jax.ShapeDtypeStruct((nb, n_heads, HEAD, HEAD), F32)],
        scratch_shapes=[pltpu.VMEM((n_heads, HEAD, HEAD), F32)], compiler_params=_params(("arbitrary",)),
    )(proj, proj, proj, proj, l0, l1, nw)


def _hgrn2_bwd(proj, l0, l1, nw, saved, d_ocat, n_heads, name):
    t = proj.shape[0]
    tb = _tile(t, HG_BLOCK, HG_SUB)
    nb = t // tb
    rev = lambda j: nb - 1 - j

    hw = n_heads * HEAD

    def body(q_ref, f_ref, i_ref, g_ref, l0_ref, l1_ref, nw_ref, save_ref, do_ref,
             dp_ref, dl0_ref, dl1_ref, dnw_ref, dst_ref):
        @pl.when(pl.program_id(0) == 0)
        def _():
            dst_ref[...] = jnp.zeros_like(dst_ref)
            dl0_ref[...] = jnp.zeros_like(dl0_ref)
            dl1_ref[...] = jnp.zeros_like(dl1_ref)
            dnw_ref[...] = jnp.zeros_like(dnw_ref)

        dnw_acc = jnp.zeros((1, HEAD), F32)
        for hs in _head_groups(n_heads, HG_BWD_GROUP):
            _, vjp = jax.vjp(_hg_block, [save_ref[h] for h in hs],
                             *[[r[:, _head_cols(h)] for h in hs] for r in (q_ref, f_ref, i_ref, g_ref)],
                             [l0_ref[h] for h in hs], [l1_ref[h] for h in hs], nw_ref[...])
            dst, dq, df, di, dg, dl0, dl1, dnw = vjp(([dst_ref[h] for h in hs], [do_ref[:, _head_cols(h)] for h in hs]))
            for i, h in enumerate(hs):
                dst_ref[h] = dst[i]
                for part, val in enumerate((dq, df, di, dg)):
                    dp_ref[:, part * hw + h * HEAD:part * hw + (h + 1) * HEAD] = val[i].astype(dp_ref.dtype)
                dl0_ref[h] += dl0[i]
                dl1_ref[h] += dl1[i]
            dnw_acc = dnw_acc + dnw
        dnw_ref[...] += dnw_acc

    head_rows = pl.BlockSpec((n_heads, 1, HEAD), lambda j: (0, 0, 0))
    return pl.pallas_call(
        body, name=name, grid=(nb,),
        in_specs=_hg_in_specs(n_heads, tb, rev) + [pl.BlockSpec((None, n_heads, HEAD, HEAD), lambda j: (rev(j), 0, 0, 0)),
                                                   pl.BlockSpec((tb, hw), lambda j: (rev(j), 0))],
        out_specs=[pl.BlockSpec((tb, 4 * hw), lambda j: (rev(j), 0)), head_rows, head_rows,
                   pl.BlockSpec((1, HEAD), lambda j: (0, 0))],
        out_shape=[jax.ShapeDtypeStruct((t, 4 * hw), BF16)] + [jax.ShapeDtypeStruct((n_heads, 1, HEAD), F32)] * 2
        + [jax.ShapeDtypeStruct((1, HEAD), F32)],
        scratch_shapes=[pltpu.VMEM((n_heads, HEAD, HEAD), F32)], compiler_params=_params(("arbitrary",)),
    )(proj, proj, proj, proj, l0, l1, nw, saved, d_ocat)


NN_DIMS = (((1,), (0,)), ((), ()))


def _split_bf16(x):
    hi = x.astype(BF16)
    return hi, (x - hi.astype(F32)).astype(BF16)


def _dot3(a, b, dims=NN_DIMS):
    (ah, al), (bh, bl) = _split_bf16(a), _split_bf16(b)
    dot = functools.partial(lax.dot_general, dimension_numbers=dims, preferred_element_type=F32)
    return dot(ah, bh) + dot(ah, bl) + dot(al, bh)


@jax.custom_vjp
def _mm3(a, b):
    return _dot3(a, b)


def _mm3_fwd(a, b):
    return _dot3(a, b), (a, b)


def _mm3_bwd(res, g):
    a, b = res
    return _dot3(g, b, NT_DIMS), _dot3(a, g, TN_DIMS)


_mm3.defvjp(_mm3_fwd, _mm3_bwd)


def _dot_bf16(a, b, dims=(((1,), (0,)), ((), ()))):
    return lax.dot_general(a.astype(BF16), b.astype(BF16), dims, preferred_element_type=F32)


def _inv_unit_lower_raw(ms):
    hs = range(len(ms))
    c = ms[0].shape[0]
    r = lax.broadcasted_iota(jnp.int32, (c, c), 0)
    q = lax.broadcasted_iota(jnp.int32, (c, c), 1)
    eye = (r == q).astype(F32)
    md = [jnp.where((r // GDN_INV_BLOCK) == (q // GDN_INV_BLOCK), ms[h], 0.0) for h in hs]
    p = [-md[h] for h in hs]
    t16 = [eye + p[h] for h in hs]
    for _ in range(int(math.log2(GDN_INV_BLOCK)) - 1):
        p = [_dot3(p[h], p[h]) for h in hs]
        t16 = [t16[h] + _dot3(t16[h], p[h]) for h in hs]
    p = [-_dot3(t16[h], ms[h] - md[h]) for h in hs]
    t2 = [eye + p[h] for h in hs]
    for _ in range(int(math.log2(c // GDN_INV_BLOCK)) - 1):
        p = [_dot3(p[h], p[h]) for h in hs]
        t2 = [t2[h] + _dot3(t2[h], p[h]) for h in hs]
    return [_dot3(t2[h], t16[h]) for h in hs]


@jax.custom_vjp
def _inv_unit_lower(ms):
    return _inv_unit_lower_raw(ms)


def _inv_fwd(ms):
    ts = _inv_unit_lower_raw(ms)
    return ts, ts


def _inv_bwd(ts, dts):
    hs = range(len(ts))
    inner = [_dot3(ts[h], dts[h], TN_DIMS) for h in hs]
    return ([-_dot3(inner[h], ts[h], NT_DIMS) for h in hs],)


_inv_unit_lower.defvjp(_inv_fwd, _inv_bwd)


def _gdn_block(precise, onehots, st, qc, kc, vc, g, ab, alog_row, dtb_row, nw):
    inverse, dot3 = precise
    hs = range(len(st))
    c = qc[0].shape[0]
    lane_sum = lambda v: jnp.sum(v, axis=-1, keepdims=True)
    a = [lane_sum(ab * onehots[h][0]) for h in hs]
    bb = [lane_sum(ab * onehots[h][1]) for h in hs]
    alog = [lane_sum(alog_row * onehots[h][0]) for h in hs]
    dtb = [lane_sum(dtb_row * onehots[h][0]) for h in hs]
    la = [-jnp.exp(alog[h]) * _softplus(a[h] + dtb[h]) for h in hs]
    beta = [jax.nn.sigmoid(bb[h]) for h in hs]
    q = [qc[h] * lax.rsqrt(lane_sum(qc[h] * qc[h]) + EPS) * (HEAD ** -0.5) for h in hs]
    k = [kc[h] * lax.rsqrt(lane_sum(kc[h] * kc[h]) + EPS) for h in hs]
    r = lax.broadcasted_iota(jnp.int32, (c, c), 0)
    s = lax.broadcasted_iota(jnp.int32, (c, c), 1)
    tri = (r >= s).astype(F32)
    g_cc = [dot3(tri, jnp.broadcast_to(la[h], (c, c))) for h in hs]
    g_cl = [dot3(tri, jnp.broadcast_to(la[h], (c, HEAD))) for h in hs]
    gamma = [jnp.exp(jnp.where(r >= s, g_cc[h] - g_cc[h].T, -1e30)) for h in hs]
    kk = [_dot_bf16(k[h], k[h], NT_DIMS) for h in hs]
    m = [jnp.where(r > s, beta[h] * kk[h] * gamma[h], 0.0) for h in hs]
    tm = inverse(m)
    eg = [jnp.exp(g_cl[h]) for h in hs]
    rhs = [jnp.concatenate([vc[h] * beta[h], k[h] * (beta[h] * eg[h])], axis=1) for h in hs]
    sol = [dot3(tm[h], rhs[h]) for h in hs]
    qk = [_dot_bf16(q[h], k[h], NT_DIMS) * gamma[h] for h in hs]
    g_last = [g_cl[h][c - 1:c] for h in hs]
    k_tail = [k[h] * jnp.exp(g_last[h] - g_cl[h]) for h in hs]
    v_new = [sol[h][:, :HEAD] - _dot_bf16(sol[h][:, HEAD:], st[h], NT_DIMS) for h in hs]
    o_st = [_dot_bf16(q[h] * eg[h], st[h], NT_DIMS) for h in hs]
    o = [o_st[h] + _dot_bf16(qk[h], v_new[h]) for h in hs]
    upd = [_dot_bf16(v_new[h], k_tail[h], TN_DIMS) for h in hs]
    st = [st[h] * jnp.exp(g_last[h]) + upd[h] for h in hs]
    out = [o[h] * _rms_scale(o[h]) * nw * _silu(g[h]) for h in hs]
    return st, out


def _head_onehots(n_heads, h):
    lane = lax.broadcasted_iota(jnp.int32, (1, LANES), 1)
    return (lane == h).astype(F32), (lane == n_heads + h).astype(F32)


def _gdn_in_specs(n_heads, c, time_index):
    hw = n_heads * HEAD
    qkv = [pl.BlockSpec((c, hw), functools.partial(lambda part, j: (time_index(j), part), part)) for part in range(3)]
    row = pl.BlockSpec((1, LANES), lambda j: (0, 0))
    return qkv + [pl.BlockSpec((c, hw), lambda j: (time_index(j), 7)),
                  pl.BlockSpec((c, LANES), lambda j: (time_index(j), 8 * n_heads)), row, row, row]


def _gdn_fwd(qkv, proj, alog_row, dtb_row, nw, n_heads, name):
    t = qkv.shape[0]
    hw = n_heads * HEAD
    c = _tile(t, GDN_CHUNK, GDN_CHUNK)
    nb = t // c

    def body(q_ref, k_ref, v_ref, g_ref, ab_ref, al_ref, dt_ref, nw_ref, o_ref, save_ref, st_ref):
        @pl.when(pl.program_id(0) == 0)
        def _():
            st_ref[...] = jnp.zeros_like(st_ref)

        for hs in _head_groups(n_heads, GDN_FWD_GROUP):
            st = [st_ref[h] for h in hs]
            for h, s in zip(hs, st):
                save_ref[h] = s
            st, out = _gdn_block((_inv_unit_lower_raw, _dot3), [_head_onehots(n_heads, h) for h in hs], st,
                                 *[[r[:, _head_cols(h)] for h in hs] for r in (q_ref, k_ref, v_ref, g_ref)],
                                 ab_ref[...], al_ref[...], dt_ref[...], nw_ref[...])
            for h, s, o in zip(hs, st, out):
                st_ref[h] = s
                o_ref[:, _head_cols(h)] = o.astype(o_ref.dtype)

    return pl.pallas_call(
        body, name=name, grid=(nb,), in_specs=_gdn_in_specs(n_heads, c, lambda j: j),
        out_specs=[pl.BlockSpec((c, hw), lambda j: (j, 0)),
                   pl.BlockSpec((None, n_heads, HEAD, HEAD), lambda j: (j, 0, 0, 0))],
        out_shape=[jax.ShapeDtypeStruct((t, hw), BF16), jax.ShapeDtypeStruct((nb, n_heads, HEAD, HEAD), F32)],
        scratch_shapes=[pltpu.VMEM((n_heads, HEAD, HEAD), F32)], compiler_params=_params(("arbitrary",)),
    )(qkv, qkv, qkv, proj, proj, alog_row, dtb_row, nw)


def _gdn_bwd(qkv, proj, alog_row, dtb_row, nw, saved, d_ocat, n_heads, name):
    t = qkv.shape[0]
    c = _tile(t, GDN_CHUNK, GDN_CHUNK)
    nb = t // c
    rev = lambda j: nb - 1 - j

    hw = n_heads * HEAD

    def body(q_ref, k_ref, v_ref, g_ref, ab_ref, al_ref, dt_ref, nw_ref, save_ref, do_ref,
             dqkv_ref, dg_ref, dab_ref, dal_ref, ddt_ref, dnw_ref, dst_ref):
        @pl.when(pl.program_id(0) == 0)
        def _():
            dst_ref[...] = jnp.zeros_like(dst_ref)
            dal_ref[...] = jnp.zeros_like(dal_ref)
            ddt_ref[...] = jnp.zeros_like(ddt_ref)
            dnw_ref[...] = jnp.zeros_like(dnw_ref)

        dab_acc = jnp.zeros((c, LANES), F32)
        row_acc = [jnp.zeros((1, LANES), F32)] * 3
        for hs in _head_groups(n_heads, GDN_BWD_GROUP):
            fn = functools.partial(_gdn_block, (_inv_unit_lower, _mm3), [_head_onehots(n_heads, h) for h in hs])
            _, vjp = jax.vjp(fn, [save_ref[h] for h in hs],
                             *[[r[:, _head_cols(h)] for h in hs] for r in (q_ref, k_ref, v_ref, g_ref)],
                             ab_ref[...], al_ref[...], dt_ref[...], nw_ref[...])
            dst, dq, dk, dv, dg, dab, dal, ddt, dnw = vjp(([dst_ref[h] for h in hs], [do_ref[:, _head_cols(h)] for h in hs]))
            for i, h in enumerate(hs):
                dst_ref[h] = dst[i]
                for part, val in enumerate((dq, dk, dv)):
                    dqkv_ref[:, part * hw + h * HEAD:part * hw + (h + 1) * HEAD] = val[i]
                dg_ref[:, _head_cols(h)] = dg[i].astype(dg_ref.dtype)
            dab_acc = dab_acc + dab
            row_acc = [acc + val for acc, val in zip(row_acc, (dal, ddt, dnw))]
        dab_ref[...] = dab_acc
        dal_ref[...] += row_acc[0]
        ddt_ref[...] += row_acc[1]
        dnw_ref[...] += row_acc[2]

    row = pl.BlockSpec((1, LANES), lambda j: (0, 0))
    return pl.pallas_call(
        body, name=name, grid=(nb,),
        in_specs=_gdn_in_specs(n_heads, c, rev) + [pl.BlockSpec((None, n_heads, HEAD, HEAD), lambda j: (rev(j), 0, 0, 0)),
                                                   pl.BlockSpec((c, hw), lambda j: (rev(j), 1))],
        out_specs=[pl.BlockSpec((c, 3 * hw), lambda j: (rev(j), 0)), pl.BlockSpec((c, hw), lambda j: (rev(j), 0)),
                   pl.BlockSpec((c, LANES), lambda j: (rev(j), 0)), row, row, row],
        out_shape=[jax.ShapeDtypeStruct((t, 3 * hw), F32), jax.ShapeDtypeStruct((t, hw), BF16),
                   jax.ShapeDtypeStruct((t, LANES), F32)] + [jax.ShapeDtypeStruct((1, LANES), F32)] * 3,
        scratch_shapes=[pltpu.VMEM((n_heads, HEAD, HEAD), F32)], compiler_params=_params(("arbitrary",)),
    )(qkv, qkv, qkv, proj, proj, alog_row, dtb_row, nw, saved, d_ocat)


def _pad_lanes(v, n):
    v = v.reshape(1, -1)
    return jnp.pad(v, ((0, 0), (0, n - v.shape[1])))


def _pack_rows(vecs):
    flat = jnp.concatenate([v.reshape(-1) for v in vecs])
    offs, o = [], 0
    for v in vecs:
        offs.append((o, v.size))
        o += v.size
    per_row = -(-o // (SUBLANES * LANES)) * LANES
    flat = jnp.pad(flat, (0, SUBLANES * per_row - o))
    return flat.reshape(SUBLANES, per_row), offs


def _unpack(gathered, offs):
    per_dev = gathered.reshape(N_DEV, -1)
    return [per_dev[:, o:o + n] for o, n in offs]


def _sum_devices(part):
    acc = part[0]
    for i in range(1, N_DEV):
        acc = acc + part[i]
    return acc


def kernel(x, c, w_ada, b_ada, pre_mix_norm, post_mix_norm, pre_ffn_norm, post_ffn_norm, w_in, hg_lb_logits, hg_norm, gdn_conv_w, gdn_a_log, gdn_dt_bias, gdn_norm, w_out, w_ff1, w_ff2, loss_target, m_w_ada, m_b_ada, m_pre_mix_norm, m_post_mix_norm, m_pre_ffn_norm, m_post_ffn_norm, m_w_in, m_hg_lb_logits, m_hg_norm, m_gdn_conv_w, m_gdn_a_log, m_gdn_dt_bias, m_gdn_norm, m_w_out, m_w_ff1, m_w_ff2, v_w_ada, v_b_ada, v_pre_mix_norm, v_post_mix_norm, v_pre_ffn_norm, v_post_ffn_norm, v_w_in, v_hg_lb_logits, v_hg_norm, v_gdn_conv_w, v_gdn_a_log, v_gdn_dt_bias, v_gdn_norm, v_w_out, v_w_ff1, v_w_ff2):
    assert x.shape[0] == 1 and w_ada.shape[0] == 1 and hg_lb_logits.shape[0] == 2
    t, d = x.shape[1], x.shape[2]
    n_heads = (d // 2) // HEAD
    hw = n_heads * HEAD
    in_cols = 8 * hw + 2 * n_heads
    np_cols = 8 * hw + 2 * LANES
    d_ff = w_ff1.shape[2] * N_CHIP
    na = w_ada.shape[2]
    ax, ay, ac = lax.axis_index("x"), lax.axis_index("y"), lax.axis_index("c")
    chip = 2 * ax + ay
    dev = 4 * ax + 2 * ay + ac

    x2d, tgt = x[0], loss_target[0]

    pack1, offs1 = _pack_rows([c[0], gdn_conv_w[0]])
    c_all, convw_all = _unpack(_gather8(pack1, "gather_cond"), offs1)
    conv_sh = gdn_conv_w.shape[2]
    conv_w = jnp.concatenate([convw_all[2 * j].reshape(CONV_K, conv_sh) for j in range(N_CHIP)], axis=1)

    b_s = lax.dynamic_slice(b_ada, (0, chip * na), (1, na))
    mod_part = _mod_part(c_all, w_ada[0], b_s, "mod_part")
    pack2, offs2 = _pack_rows([mod_part])
    (mod_parts,) = _unpack(_gather8(pack2, "gather_mod"), offs2)
    mod_all = jnp.concatenate([mod_parts[2 * j].reshape(N_DEV, na) for j in range(N_CHIP)], axis=1)
    mod = lax.dynamic_slice(mod_all, (dev, 0), (1, N_MOD * d))
    sh_m, sc_m, gt_m, sh_f, sc_f, gt_f = [mod[:, i * d:(i + 1) * d] for i in range(N_MOD)]

    h1 = _norm_mod(x2d, pre_mix_norm, sc_m, sh_m, "norm_mod_mix")
    (g_in,), h1 = _gather_weights([jnp.transpose(w_in[0]).astype(BF16)], "gather_w_in", sequencer_id=5, after=h1)
    late, g_in = lax.optimization_barrier(([w_out[0].astype(BF16), w_ff1[0].astype(BF16), w_ff2[0].astype(BF16)], g_in))
    g_out, g_ff1, g_ff2 = _gather_weights(late, "gather_weights_late", sequencer_id=1)
    w_in_f = jnp.pad(g_in.reshape(in_cols, d), ((0, np_cols - in_cols), (0, 0)))
    w_out_f = g_out.reshape(d, d)
    w_ff2_f = g_ff2.reshape(d_ff, d)

    proj = _matmul(h1, w_in_f, "nt", F32, "mm_in", tn=768)
    l0, l1 = hg_lb_logits[0].reshape(n_heads, 1, HEAD), hg_lb_logits[1].reshape(n_heads, 1, HEAD)
    o_hg, hg_saved = _hgrn2_fwd(proj, l0, l1, hg_norm, n_heads, "hgrn2_fwd")
    qkv = _conv_fwd(proj, conv_w, 4 * n_heads, "conv_fwd")
    alog_row, dtb_row = _pad_lanes(gdn_a_log, LANES), _pad_lanes(gdn_dt_bias, LANES)
    o_gdn, gdn_saved = _gdn_fwd(qkv, proj, alog_row, dtb_row, gdn_norm, n_heads, "gdn_fwd")
    o_cat = jnp.concatenate([o_hg, o_gdn], axis=1)
    y1 = _matmul(o_cat, w_out_f, "nn", F32, "mm_out")
    x_mid, h2 = _resid_norm_mod(x2d, y1, post_mix_norm, gt_m, pre_ffn_norm, sc_f, sh_f, "resid_mix_norm_mod_ffn")

    relu_a1, r1 = _matmul(h2, g_ff1, "nn", BF16, "mm_ff1", relu2=True, b_split=True)
    y2 = _matmul(r1, w_ff2_f, "nn", F32, "mm_ff2")
    d_out, loss_row, dy2, d_gt_f, d_post_ffn = _loss_head(x_mid, y2, post_ffn_norm, gt_f, tgt, "loss_head")

    in_sh = in_cols // N_CHIP
    ff_sh = d_ff // N_CHIP
    my_half = jnp.reshape(ac, (1,)).astype(jnp.int32)

    def start_reduce(by_chip, tag, collective_id):
        to_sib = [_row_half_to_bf16(a, 1 - my_half, None, f"sibling_half_{tag}{i}") for i, a in enumerate(by_chip)]
        from_sib = _sibling_exchange(to_sib, f"sibling_partials_{tag}")
        chip_part = [_row_half_to_bf16(a, my_half, s, f"add_halves_{tag}{i}") for i, (a, s) in enumerate(zip(by_chip, from_sib))]
        return _sequencer_chip_exchange(chip_part, f"scatter_grads_{tag}", collective_id)

    gw_ff2 = _matmul(r1, dy2, "tn", BF16, "mm_ff2_dw")
    gw_ff2, dy2 = lax.optimization_barrier((gw_ff2, dy2))
    da1 = _matmul(dy2, w_ff2_f, "nt", BF16, "mm_ff2_dx", times=relu_a1)
    gw_ff1 = _matmul(h2, da1, "tn", BF16, "mm_ff1_dw", out_split=True)
    gw_ff1, da1 = lax.optimization_barrier((gw_ff1, da1))
    recv_ff2, recv_ff1 = _sequencer_chip_exchange([gw_ff2.reshape(N_CHIP, ff_sh, d), gw_ff1], "scatter_grads_ff", 2)
    dh2 = _matmul(da1, g_ff1, "nt", BF16, "mm_ff1_dx", b_split=True)
    d_mid, d_pre_ffn, d_sc_f, d_sh_f = _norm_mod_bwd(x_mid, pre_ffn_norm, sc_f, dh2, d_out, "norm_mod_ffn_bwd")

    dy1, d_gt_m, d_post_mix = _resid_bwd(d_mid, y1, post_mix_norm, gt_m, "resid_mix_bwd")
    gw_out = _matmul(o_cat, dy1, "tn", BF16, "mm_out_dw")
    gw_out, dy1 = lax.optimization_barrier((gw_out, dy1))
    (recv_out,) = _sequencer_chip_exchange([gw_out.reshape(N_CHIP, d // N_CHIP, d)], "scatter_grads_out", 3)
    d_ocat = _matmul(dy1, w_out_f, "nt", F32, "mm_out_dx")
    dp_hg, dl0, dl1, d_hg_norm = _hgrn2_bwd(proj, l0, l1, hg_norm, hg_saved, d_ocat, n_heads, "hgrn2_bwd")
    dqkv, dg_g, dab, d_alog, d_dtb, d_gdn_norm = _gdn_bwd(
        qkv, proj, alog_row, dtb_row, gdn_norm, gdn_saved, d_ocat, n_heads, "gdn_bwd")
    du, d_conv_w = _conv_bwd(proj, conv_w, dqkv, 4 * n_heads, "conv_bwd")
    dproj = jnp.concatenate([dp_hg, du, dg_g, dab.astype(BF16), jnp.zeros((t, LANES), BF16)], axis=1)
    gw_in = _matmul(dproj, h1, "tn", F32, "mm_in_dw", tm=768)
    (recv_in,) = start_reduce([gw_in[:in_cols].reshape(N_CHIP, in_sh, d)], "in", 4)
    dh1 = _matmul(dproj, w_in_f, "nn", BF16, "mm_in_dx", tk=2816)
    grad_x, d_pre_mix, d_sc_m, d_sh_m = _norm_mod_bwd(x2d, pre_mix_norm, sc_m, dh1, d_mid, "norm_mod_mix_bwd")

    d_mod = jnp.concatenate([d_sh_m, d_sc_m, d_gt_m, d_sh_f, d_sc_f, d_gt_f], axis=1)
    d_lb_logits = jnp.stack([dl0.reshape(n_heads, HEAD), dl1.reshape(n_heads, HEAD)])
    pack3, offs3 = _pack_rows([loss_row[0, :1], d_pre_mix, d_post_mix, d_pre_ffn, d_post_ffn, d_lb_logits, d_hg_norm,
                               d_conv_w, d_alog[0, :n_heads], d_dtb[0, :n_heads], d_gdn_norm, d_mod])
    parts = _unpack(_gather8(pack3, "gather_vec_grads"), offs3)
    sums = [_sum_devices(p) for p in parts[:-1]]
    loss = sums[0][0]
    dmod_all = parts[-1]
    g_b_ada = _sum_devices(dmod_all).reshape(1, N_MOD * d)
    g_conv_full = sums[7].reshape(CONV_K, N_CHIP * conv_sh)
    g_conv = lax.dynamic_slice(g_conv_full, (0, chip * conv_sh), (CONV_K, conv_sh))
    dmod_chip = lax.dynamic_slice(dmod_all, (0, chip * na), (N_DEV, na))

    sum_ff2 = _sum_chips(recv_ff2, "sum_chips_ff2")
    recv_ff1, sum_ff2 = lax.optimization_barrier((recv_ff1, sum_ff2))
    sum_ff1 = _sum_chips(recv_ff1, "sum_chips_ff1")
    recv_out, sum_ff1 = lax.optimization_barrier((recv_out, sum_ff1))
    sum_out = _sum_chips(recv_out, "sum_chips_out")
    recv_in, sum_out = lax.optimization_barrier((recv_in, sum_out))
    mine = [_sum_chips(recv_in, "sum_chips_in"), sum_out, sum_ff1, sum_ff2]
    theirs = _sibling_exchange(mine, "sibling_grads", sequencer_id=6)
    ada = _adamw(w_ada[0], [c_all, dmod_chip], m_w_ada[0], v_w_ada[0], "adamw_w_ada", outer=True)
    theirs, ada = lax.optimization_barrier((theirs, ada))

    big = {"w_ada": [o[None] for o in ada]}
    for i, (nm, w_, m_, v_) in enumerate([("w_in", w_in, m_w_in, v_w_in), ("w_out", w_out, m_w_out, v_w_out),
                                          ("w_ff1", w_ff1, m_w_ff1, v_w_ff1), ("w_ff2", w_ff2, m_w_ff2, v_w_ff2)]):
        if nm == "w_in":
            res_t = _adamw(jnp.transpose(w_[0]), [mine[i], theirs[i]], jnp.transpose(m_[0]), jnp.transpose(v_[0]),
                           f"adamw_{nm}", by_core=True)
            big[nm] = [jnp.transpose(o)[None] for o in res_t]
        else:
            big[nm] = [o[None] for o in _adamw(w_[0], [mine[i], theirs[i]], m_[0], v_[0], f"adamw_{nm}")]

    small_names = ["b_ada", "pre_mix_norm", "post_mix_norm", "pre_ffn_norm", "post_ffn_norm", "hg_lb_logits", "hg_norm",
                   "gdn_conv_w", "gdn_a_log", "gdn_dt_bias", "gdn_norm"]
    small_w = [b_ada, pre_mix_norm, post_mix_norm, pre_ffn_norm, post_ffn_norm, hg_lb_logits, hg_norm, gdn_conv_w,
               gdn_a_log, gdn_dt_bias, gdn_norm]
    small_m = [m_b_ada, m_pre_mix_norm, m_post_mix_norm, m_pre_ffn_norm, m_post_ffn_norm, m_hg_lb_logits, m_hg_norm,
               m_gdn_conv_w, m_gdn_a_log, m_gdn_dt_bias, m_gdn_norm]
    small_v = [v_b_ada, v_pre_mix_norm, v_post_mix_norm, v_pre_ffn_norm, v_post_ffn_norm, v_hg_lb_logits, v_hg_norm,
               v_gdn_conv_w, v_gdn_a_log, v_gdn_dt_bias, v_gdn_norm]
    small_g = [g_b_ada, sums[1], sums[2], sums[3], sums[4], sums[5], sums[6], g_conv, sums[8], sums[9], sums[10]]
    pw, offs_s = _pack_rows(small_w)
    pg, _ = _pack_rows(small_g)
    pm, _ = _pack_rows(small_m)
    pv, _ = _pack_rows(small_v)
    packed = _adamw(pw, [pg], pm, pv, "adamw_vectors")
    small = {}
    for nm, w_, (o, n) in zip(small_names, small_w, offs_s):
        small[nm] = [p.reshape(-1)[o:o + n].reshape(w_.shape) for p in packed]

    order = ["w_ada", "b_ada", "pre_mix_norm", "post_mix_norm", "pre_ffn_norm", "post_ffn_norm", "w_in", "hg_lb_logits",
             "hg_norm", "gdn_conv_w", "gdn_a_log", "gdn_dt_bias", "gdn_norm", "w_out", "w_ff1", "w_ff2"]
    res = {**big, **small}
    outs = [loss, grad_x[None]]
    for k in range(4):
        outs += [res[nm][k] for nm in order]
    return tuple(outs)
```

```python
import functools
import math

import jax
import jax.numpy as jnp
from jax import lax
from jax.experimental import pallas as pl
from jax.experimental.pallas import tpu as pltpu
from jax.experimental.pallas import tpu_sc as plsc

F32 = jnp.float32
BF16 = jnp.bfloat16
HI = lax.Precision.HIGHEST
MESH = pl.DeviceIdType.MESH

LANES = 128
SUBLANES = 8
VMEM_LIMIT = 48 * 1024 * 1024
EPS = 1e-6
HEAD = 128
CONV_K = 4
GDN_CHUNK = 64
GDN_INV_BLOCK = 16
HG_SUB = 16
HG_BLOCK = 128
HG_FWD_GROUP = 8
HG_BWD_GROUP = 4
GDN_FWD_GROUP = 8
GDN_BWD_GROUP = 8
N_MOD = 6
N_DEV = 8
N_CHIP = 4

ADAM_LR = 0.001
ADAM_B1 = 0.9
ADAM_B2 = 0.999
ADAM_EPS = 1e-08
ADAM_WD = 0.01
ADAM_STEP = 10

NT_DIMS = (((1,), (1,)), ((), ()))
TN_DIMS = (((0,), (0,)), ((), ()))


def _tile(dim, target, align):
    if dim <= target:
        return dim
    best = dim
    t = align
    while t <= target:
        if dim % t == 0:
            best = t
        t += align
    return best


def _elementwise_tiles(r, c):
    tc = _tile(c, 1024, LANES)
    tr = _tile(r, max(16, (256 * 1024) // tc // 16 * 16), 16)
    if tr == r and r * tc > 512 * 1024:
        tc = _tile(c, max(LANES, (256 * 1024) // r // LANES * LANES), LANES)
    return tr, tc


def _params(sem):
    return pltpu.CompilerParams(dimension_semantics=sem, vmem_limit_bytes=VMEM_LIMIT)


def _silu(x):
    return x * jax.nn.sigmoid(x)


def _softplus(x):
    pos = x > 0
    return jnp.where(pos, x, 0.0) + jnp.log(1.0 + jnp.exp(jnp.where(pos, -x, x)))


def _rms_scale(x):
    return lax.rsqrt(jnp.mean(x * x, axis=-1, keepdims=True) + EPS)


def _gather8(x_shard, name):
    m_per, n = x_shard.shape
    assert m_per % SUBLANES == 0 and n % LANES == 0

    def body(x_ref, out_ref, send_sems, recv_sems, local_sem):
        x, y, c = lax.axis_index("x"), lax.axis_index("y"), lax.axis_index("c")
        me, sibling = (x, y, c), (x, y, 1 - c)
        chips = [(1 - x, y), (x, 1 - y), (1 - x, 1 - y)]

        def rows(px, py, pc):
            return out_ref.at[pl.ds((4 * px + 2 * py + pc) * m_per, m_per), :]

        def copy(k, block, to, src=None):
            return pltpu.make_async_remote_copy(
                src_ref=rows(*block) if src is None else src, dst_ref=rows(*block),
                send_sem=send_sems.at[k], recv_sem=recv_sems.at[k], device_id=to, device_id_type=MESH)

        mine = pltpu.make_async_copy(x_ref, rows(*me), local_sem)
        mine.start()
        first = [copy(0, me, sibling, src=x_ref)]
        first += [copy(1 + j, me, (*chip, c), src=x_ref) for j, chip in enumerate(chips)]
        for cp in first:
            cp.start()
        passed = [copy(4 + j, (*chip, c), sibling) for j, chip in enumerate(chips)]
        for j, chip in enumerate(chips):
            copy(1 + j, (*chip, c), me).wait_recv()
            passed[j].start()
        copy(0, sibling, me).wait_recv()
        for j, chip in enumerate(chips):
            copy(4 + j, (*chip, 1 - c), me).wait_recv()
        for cp in first + passed:
            cp.wait_send()
        mine.wait()

    return pl.pallas_call(
        body, name=name,
        out_shape=jax.ShapeDtypeStruct((N_DEV * m_per, n), x_shard.dtype),
        in_specs=[pl.BlockSpec(memory_space=pltpu.VMEM)],
        out_specs=pl.BlockSpec(memory_space=pltpu.VMEM),
        scratch_shapes=[pltpu.SemaphoreType.DMA((7,)), pltpu.SemaphoreType.DMA((7,)), pltpu.SemaphoreType.DMA],
        compiler_params=pltpu.CompilerParams(vmem_limit_bytes=VMEM_LIMIT),
    )(x_shard)


def _gather_weights(arrs, name, sequencer_id=None, after=None):
    n = len(arrs)
    out_shapes = [jax.ShapeDtypeStruct((N_CHIP,) + a.shape, a.dtype) for a in arrs]

    def body(*refs):
        ins, outs = refs[:n], refs[n:2 * n]
        ici_send, ici_recv, d2d_send, d2d_recv = refs[2 * n:]
        if sequencer_id is not None:
            chips, sib = _chip_peers()
            _handshake(chips + [sib])
        x, y, c = lax.axis_index("x"), lax.axis_index("y"), lax.axis_index("c")
        me = 2 * x + y
        peers = [(1 - x, y), (x, 1 - y), (1 - x, 1 - y)]

        def half(a, cc):
            r = arrs[a].shape[0]
            cut = r // 32 * 16
            return pl.ds(0, cut) if cc == 0 else pl.ds(cut, r - cut)

        def exchange(mine):
            sibling = (x, y, 1 - mine)
            sent = []
            for a in range(n):
                for k, (px, py) in enumerate(peers):
                    cp = pltpu.make_async_remote_copy(
                        src_ref=ins[a].at[half(a, mine)], dst_ref=outs[a].at[me, half(a, mine)],
                        send_sem=ici_send.at[3 * a + k], recv_sem=ici_recv.at[3 * a + k],
                        device_id=(px, py, mine), device_id_type=MESH)
                    cp.start()
                    sent.append(cp)
            for a in range(n):
                for k, (px, py) in enumerate(peers):
                    landed = outs[a].at[2 * px + py, half(a, mine)]
                    pltpu.make_async_remote_copy(
                        src_ref=landed, dst_ref=landed, send_sem=ici_send.at[3 * a + k], recv_sem=ici_recv.at[3 * a + k],
                        device_id=(px, py, mine), device_id_type=MESH).wait_recv()
                    fwd = pltpu.make_async_remote_copy(
                        src_ref=landed, dst_ref=landed, send_sem=d2d_send.at[3 * a + k], recv_sem=d2d_recv.at[3 * a + k],
                        device_id=sibling, device_id_type=MESH)
                    fwd.start()
                    sent.append(fwd)
            for a in range(n):
                for k, (px, py) in enumerate(peers):
                    passed = outs[a].at[2 * px + py, half(a, 1 - mine)]
                    pltpu.make_async_remote_copy(
                        src_ref=passed, dst_ref=passed, send_sem=d2d_send.at[3 * a + k], recv_sem=d2d_recv.at[3 * a + k],
                        device_id=sibling, device_id_type=MESH).wait_recv()
            for cp in sent:
                cp.wait_send()

        for core in (0, 1):
            pl.when(c == core)(functools.partial(exchange, core))

    sems = [pltpu.SemaphoreType.DMA((3 * n,))] * 4
    if sequencer_id is None:
        hbm = pl.BlockSpec(memory_space=pltpu.HBM)
        gathered = pl.pallas_call(body, name=name, out_shape=out_shapes, in_specs=[hbm] * n, out_specs=[hbm] * n,
                                  scratch_shapes=sems)(*arrs)
    else:
        gathered = pl.kernel(body, out_type=out_shapes, mesh=plsc.ScalarSubcoreMesh(axis_name="sequencer", num_cores=1),
                             name=name, scratch_types=sems,
                             compiler_params=pltpu.CompilerParams(collective_id=sequencer_id))(*arrs)
    if after is not None:
        gathered, after = lax.optimization_barrier((gathered, after))
    chip = 2 * lax.axis_index("x") + lax.axis_index("y")
    filled = [lax.dynamic_update_slice(g, a[None], (chip, 0, 0)) for g, a in zip(gathered, arrs)]
    return filled if after is None else (filled, after)


def _chip_peers():
    x, y, c = lax.axis_index("x"), lax.axis_index("y"), lax.axis_index("c")
    return [(1 - x, y, c), (x, 1 - y, c), (1 - x, 1 - y, c)], (x, y, 1 - c)


def _handshake(peers):
    barrier = pltpu.get_barrier_semaphore()
    for peer in peers:
        pl.semaphore_signal(barrier, inc=1, device_id=peer, device_id_type=MESH)
    pl.semaphore_wait(barrier, len(peers))


def _sequencer_chip_exchange(arrs, name, collective_id):
    n = len(arrs)
    out_types = [jax.ShapeDtypeStruct(a.shape, a.dtype) for a in arrs]

    def body(*refs):
        ins, outs = refs[:n], refs[n:2 * n]
        send_sems, recv_sems = refs[2 * n:]
        chips, _ = _chip_peers()
        _handshake(chips)
        me = 2 * lax.axis_index("x") + lax.axis_index("y")
        sent = []
        for a in range(n):
            for k, peer in enumerate(chips):
                cp = pltpu.make_async_remote_copy(
                    src_ref=ins[a].at[2 * peer[0] + peer[1]], dst_ref=outs[a].at[me],
                    send_sem=send_sems.at[3 * a + k], recv_sem=recv_sems.at[3 * a + k], device_id=peer, device_id_type=MESH)
                cp.start()
                sent.append(cp)
        for a in range(n):
            for k, peer in enumerate(chips):
                landed = outs[a].at[2 * peer[0] + peer[1]]
                pltpu.make_async_remote_copy(
                    src_ref=landed, dst_ref=landed, send_sem=send_sems.at[3 * a + k], recv_sem=recv_sems.at[3 * a + k],
                    device_id=peer, device_id_type=MESH).wait_recv()
        for cp in sent:
            cp.wait_send()

    received = pl.kernel(
        body, out_type=out_types, mesh=plsc.ScalarSubcoreMesh(axis_name="sequencer", num_cores=1), name=name,
        scratch_types=[pltpu.SemaphoreType.DMA((3 * n,))] * 2,
        compiler_params=pltpu.CompilerParams(collective_id=collective_id),
    )(*arrs)
    chip = 2 * lax.axis_index("x") + lax.axis_index("y")
    return [lax.dynamic_update_slice(r, lax.dynamic_slice(a, (chip, 0, 0), (1,) + a.shape[1:]), (chip, 0, 0))
            for r, a in zip(received, arrs)]


def _sibling_exchange(arrs, name, sequencer_id=None):
    n = len(arrs)

    def body(*refs):
        ins, outs = refs[:n], refs[n:2 * n]
        send_sems, recv_sems = refs[2 * n:]
        sibling = (lax.axis_index("x"), lax.axis_index("y"), 1 - lax.axis_index("c"))
        if sequencer_id is not None:
            _handshake([sibling])
        cps = []
        for a in range(n):
            cp = pltpu.make_async_remote_copy(src_ref=ins[a], dst_ref=outs[a], send_sem=send_sems.at[a],
                                              recv_sem=recv_sems.at[a], device_id=sibling, device_id_type=MESH)
            cp.start()
            cps.append(cp)
        for cp in cps:
            cp.wait_recv()
        for cp in cps:
            cp.wait_send()

    out_shapes = [jax.ShapeDtypeStruct(a.shape, a.dtype) for a in arrs]
    sems = [pltpu.SemaphoreType.DMA((n,)), pltpu.SemaphoreType.DMA((n,))]
    if sequencer_id is not None:
        return pl.kernel(body, out_type=out_shapes, mesh=plsc.ScalarSubcoreMesh(axis_name="sequencer", num_cores=1), name=name,
                         scratch_types=sems, compiler_params=pltpu.CompilerParams(collective_id=sequencer_id))(*arrs)
    hbm = pl.BlockSpec(memory_space=pltpu.HBM)
    return pl.pallas_call(body, name=name, out_shape=out_shapes, in_specs=[hbm] * n, out_specs=[hbm] * n,
                          scratch_shapes=sems)(*arrs)


def _matmul(a, b, mode, out_dtype, name, tm=1024, tn=1024, tk=2048, relu2=False, times=None, b_split=False,
            out_split=False):
    b_shape = (b.shape[1], b.shape[2] * N_CHIP) if b_split else b.shape
    if mode == "nn":
        (m, k), (k2, n) = a.shape, b_shape
    elif mode == "nt":
        (m, k), (n, k2) = a.shape, b_shape
    else:
        (k, m), (k2, n) = a.shape, b_shape
    assert k == k2, (a.shape, b.shape, mode)
    n_cut = n // N_CHIP if (out_split or (b_split and mode != "nt")) else n
    k_cut = k // N_CHIP if (b_split and mode == "nt") else k
    tm, tn, tk = _tile(m, tm, LANES), _tile(n_cut, tn, LANES), _tile(k_cut, tk, LANES)
    assert n_cut % tn == 0 and k_cut % tk == 0 and m % tm == 0, (name, m, n, k, tm, tn, tk)
    nk = k // tk
    nbc, nkc = n_cut // tn, k_cut // tk
    n_in = 2 if times is None else 3
    n_out = 2 if relu2 else 1

    def product(a_ref, b_ref):
        if mode == "nn":
            return jnp.dot(a_ref[...], b_ref[...], preferred_element_type=F32)
        return lax.dot_general(a_ref[...], b_ref[...], NT_DIMS if mode == "nt" else TN_DIMS, preferred_element_type=F32)

    def finish(p, refs, o_refs):
        if relu2:
            p = jnp.maximum(p, 0.0)
            o_refs[0][...] = p.astype(o_refs[0].dtype)
            o_refs[1][...] = (p * p).astype(o_refs[1].dtype)
        elif times is not None:
            o_refs[0][...] = (2.0 * refs[2][...].astype(F32) * p).astype(o_refs[0].dtype)
        else:
            o_refs[0][...] = p.astype(o_refs[0].dtype)

    def body(*refs):
        o_refs = refs[n_in:n_in + n_out]
        if nk == 1:
            finish(product(refs[0], refs[1]), refs, o_refs)
            return
        acc_ref = refs[n_in + n_out]
        kk = pl.program_id(2)

        @pl.when(kk == 0)
        def _():
            acc_ref[...] = product(refs[0], refs[1])

        @pl.when((kk > 0) & (kk < nk - 1))
        def _():
            acc_ref[...] += product(refs[0], refs[1])

        @pl.when(kk == nk - 1)
        def _():
            finish(acc_ref[...] + product(refs[0], refs[1]), refs, o_refs)

    if mode == "tn":
        a_spec = pl.BlockSpec((tk, tm), lambda i, j, kk: (kk, i))
    else:
        a_spec = pl.BlockSpec((tm, tk), lambda i, j, kk: (i, kk))
    if mode == "nt":
        b_spec = (pl.BlockSpec((None, tn, tk), lambda i, j, kk: (kk // nkc, j, kk % nkc)) if b_split
                  else pl.BlockSpec((tn, tk), lambda i, j, kk: (j, kk)))
    else:
        b_spec = (pl.BlockSpec((None, tk, tn), lambda i, j, kk: (j // nbc, kk, j % nbc)) if b_split
                  else pl.BlockSpec((tk, tn), lambda i, j, kk: (kk, j)))
    mn_spec = pl.BlockSpec((tm, tn), lambda i, j, kk: (i, j))
    if out_split:
        o_spec = pl.BlockSpec((None, tm, tn), lambda i, j, kk: (j // nbc, i, j % nbc))
        o_shape = jax.ShapeDtypeStruct((N_CHIP, m, n_cut), out_dtype)
    else:
        o_spec, o_shape = mn_spec, jax.ShapeDtypeStruct((m, n), out_dtype)
    out = pl.pallas_call(
        body, name=name, grid=(m // tm, n // tn, nk), in_specs=[a_spec, b_spec] + [mn_spec] * (n_in - 2),
        out_specs=[o_spec] * n_out, out_shape=[o_shape] * n_out,
        scratch_shapes=[] if nk == 1 else [pltpu.VMEM((tm, tn), F32)],
        compiler_params=_params(("parallel", "parallel", "arbitrary")),
    )(*((a, b) if times is None else (a, b, times)))
    return out if relu2 else out[0]


def _mod_part(c_all, w_s, b_s, name):
    d, na = w_s.shape
    tn = _tile(na, 512, LANES)

    def body(c_ref, w_ref, b_ref, o_ref):
        ca = _silu(c_ref[...]).astype(BF16)
        o_ref[...] = jnp.dot(ca, w_ref[...].astype(BF16), preferred_element_type=F32) + b_ref[...]

    return pl.pallas_call(
        body, name=name, grid=(na // tn,),
        in_specs=[pl.BlockSpec((N_DEV, d), lambda j: (0, 0)), pl.BlockSpec((d, tn), lambda j: (0, j)),
                  pl.BlockSpec((1, tn), lambda j: (0, j))],
        out_specs=pl.BlockSpec((N_DEV, tn), lambda j: (0, j)),
        out_shape=jax.ShapeDtypeStruct((N_DEV, na), F32), compiler_params=_params(("parallel",)),
    )(c_all, w_s, b_s)


def _row_specs(tb, d, n_full, n_vec):
    full = pl.BlockSpec((tb, d), lambda i: (i, 0))
    vec = pl.BlockSpec((1, d), lambda i: (0, 0))
    return [full] * n_full + [vec] * n_vec


def _norm_mod(x, w, sc, sh, name):
    t, d = x.shape
    tb = _tile(t, 256, SUBLANES)

    def body(x_ref, w_ref, sc_ref, sh_ref, o_ref):
        xv = x_ref[...]
        o_ref[...] = (xv * _rms_scale(xv) * w_ref[...] * (1.0 + sc_ref[...]) + sh_ref[...]).astype(o_ref.dtype)

    return pl.pallas_call(
        body, name=name, grid=(t // tb,), in_specs=_row_specs(tb, d, 1, 3),
        out_specs=pl.BlockSpec((tb, d), lambda i: (i, 0)), out_shape=jax.ShapeDtypeStruct((t, d), BF16),
        compiler_params=_params(("parallel",)),
    )(x, w, sc, sh)


def _norm_mod_bwd(x, w, sc, dh, dres, name):
    t, d = x.shape
    tb = _tile(t, 256, SUBLANES)

    def body(x_ref, w_ref, sc_ref, dh_ref, dres_ref, dx_ref, dw_ref, dsc_ref, dsh_ref):
        @pl.when(pl.program_id(0) == 0)
        def _():
            dw_ref[...] = jnp.zeros_like(dw_ref)
            dsc_ref[...] = jnp.zeros_like(dsc_ref)
            dsh_ref[...] = jnp.zeros_like(dsh_ref)

        xv = x_ref[...]
        r = _rms_scale(xv)
        xn = xv * r
        g = dh_ref[...].astype(F32)
        wv, one_sc = w_ref[...], 1.0 + sc_ref[...]
        gxn = g * xn
        dsh_ref[...] += jnp.sum(g, axis=0, keepdims=True)
        dsc_ref[...] += jnp.sum(gxn, axis=0, keepdims=True) * wv
        dw_ref[...] += jnp.sum(gxn, axis=0, keepdims=True) * one_sc
        dxn = g * (wv * one_sc)
        dx_ref[...] = dres_ref[...] + r * (dxn - xn * jnp.mean(dxn * xn, axis=-1, keepdims=True))

    vec_out = pl.BlockSpec((1, d), lambda i: (0, 0))
    return pl.pallas_call(
        body, name=name, grid=(t // tb,),
        in_specs=[pl.BlockSpec((tb, d), lambda i: (i, 0)), pl.BlockSpec((1, d), lambda i: (0, 0)),
                  pl.BlockSpec((1, d), lambda i: (0, 0)), pl.BlockSpec((tb, d), lambda i: (i, 0)),
                  pl.BlockSpec((tb, d), lambda i: (i, 0))],
        out_specs=[pl.BlockSpec((tb, d), lambda i: (i, 0)), vec_out, vec_out, vec_out],
        out_shape=[jax.ShapeDtypeStruct((t, d), F32)] + [jax.ShapeDtypeStruct((1, d), F32)] * 3,
        compiler_params=_params(("arbitrary",)),
    )(x, w, sc, dh, dres)


def _resid_norm_mod(x, y, w, gt, w2, sc, sh, name):
    t, d = x.shape
    tb = _tile(t, 256, SUBLANES)

    def body(x_ref, y_ref, w_ref, gt_ref, w2_ref, sc_ref, sh_ref, o_ref, h_ref):
        yv = y_ref[...]
        x2 = x_ref[...] + gt_ref[...] * (yv * _rms_scale(yv) * w_ref[...])
        o_ref[...] = x2
        h_ref[...] = (x2 * _rms_scale(x2) * w2_ref[...] * (1.0 + sc_ref[...]) + sh_ref[...]).astype(h_ref.dtype)

    full = pl.BlockSpec((tb, d), lambda i: (i, 0))
    return pl.pallas_call(
        body, name=name, grid=(t // tb,), in_specs=_row_specs(tb, d, 2, 5), out_specs=[full, full],
        out_shape=[jax.ShapeDtypeStruct((t, d), F32), jax.ShapeDtypeStruct((t, d), BF16)],
        compiler_params=_params(("parallel",)),
    )(x, y, w, gt, w2, sc, sh)


def _loss_head(x2, y2, w, gt, target, name):
    t, d = x2.shape
    tb = _tile(t, 256, SUBLANES)

    def body(x_ref, y_ref, tg_ref, w_ref, gt_ref, do_ref, loss_ref, dy_ref, dgt_ref, dw_ref):
        @pl.when(pl.program_id(0) == 0)
        def _():
            loss_ref[...] = jnp.zeros_like(loss_ref)
            dgt_ref[...] = jnp.zeros_like(dgt_ref)
            dw_ref[...] = jnp.zeros_like(dw_ref)

        yv = y_ref[...]
        r = _rms_scale(yv)
        yn = yv * r
        wv, gtv = w_ref[...], gt_ref[...]
        err = x_ref[...] + gtv * (yn * wv) - tg_ref[...]
        g = err * (1.0 / d)
        do_ref[...] = g
        per_tok = jnp.mean(err * err, axis=-1, keepdims=True)
        loss_ref[...] += 0.5 * jnp.sum(per_tok, axis=0, keepdims=True)
        gyn = jnp.sum(g * yn, axis=0, keepdims=True)
        dgt_ref[...] += gyn * wv
        dw_ref[...] += gyn * gtv
        dyn = g * (gtv * wv)
        dy_ref[...] = (r * (dyn - yn * jnp.mean(dyn * yn, axis=-1, keepdims=True))).astype(dy_ref.dtype)

    full = pl.BlockSpec((tb, d), lambda i: (i, 0))
    vec_out = pl.BlockSpec((1, d), lambda i: (0, 0))
    return pl.pallas_call(
        body, name=name, grid=(t // tb,), in_specs=_row_specs(tb, d, 3, 2),
        out_specs=[full, pl.BlockSpec((1, LANES), lambda i: (0, 0)), full, vec_out, vec_out],
        out_shape=[jax.ShapeDtypeStruct((t, d), F32), jax.ShapeDtypeStruct((1, LANES), F32), jax.ShapeDtypeStruct((t, d), BF16),
                   jax.ShapeDtypeStruct((1, d), F32), jax.ShapeDtypeStruct((1, d), F32)],
        compiler_params=_params(("arbitrary",)),
    )(x2, y2, target, w, gt)


def _resid_bwd(dout, y, w, gt, name):
    t, d = y.shape
    tb = _tile(t, 256, SUBLANES)

    def body(do_ref, y_ref, w_ref, gt_ref, dy_ref, dgt_ref, dw_ref):
        @pl.when(pl.program_id(0) == 0)
        def _():
            dgt_ref[...] = jnp.zeros_like(dgt_ref)
            dw_ref[...] = jnp.zeros_like(dw_ref)

        yv, g = y_ref[...], do_ref[...]
        r = _rms_scale(yv)
        yn = yv * r
        wv, gtv = w_ref[...], gt_ref[...]
        gyn = jnp.sum(g * yn, axis=0, keepdims=True)
        dgt_ref[...] += gyn * wv
        dw_ref[...] += gyn * gtv
        dyn = g * (gtv * wv)
        dy_ref[...] = (r * (dyn - yn * jnp.mean(dyn * yn, axis=-1, keepdims=True))).astype(dy_ref.dtype)

    vec_out = pl.BlockSpec((1, d), lambda i: (0, 0))
    return pl.pallas_call(
        body, name=name, grid=(t // tb,), in_specs=_row_specs(tb, d, 2, 2),
        out_specs=[pl.BlockSpec((tb, d), lambda i: (i, 0)), vec_out, vec_out],
        out_shape=[jax.ShapeDtypeStruct((t, d), BF16)] + [jax.ShapeDtypeStruct((1, d), F32)] * 2,
        compiler_params=_params(("arbitrary",)),
    )(dout, y, w, gt)


def _row_half_to_bf16(full, which, sib, name):
    n, r, c = full.shape
    by_rows = r % 32 == 0
    r, c = (r // 2, c) if by_rows else (r, c // 2)
    tr, tc = _elementwise_tiles(r, c)
    nbh = (r // tr) if by_rows else (c // tc)

    def body(which_ref, a_ref, *rest):
        if sib is None:
            rest[0][...] = a_ref[...].astype(BF16)
        else:
            rest[1][...] = (a_ref[...] + rest[0][...].astype(F32)).astype(BF16)

    if by_rows:
        half_spec = pl.BlockSpec((1, tr, tc), lambda j, i, k, which_ref: (j, which_ref[0] * nbh + i, k))
    else:
        half_spec = pl.BlockSpec((1, tr, tc), lambda j, i, k, which_ref: (j, i, which_ref[0] * nbh + k))
    spec = pl.BlockSpec((1, tr, tc), lambda j, i, k, which_ref: (j, i, k))
    grid_spec = pltpu.PrefetchScalarGridSpec(
        num_scalar_prefetch=1, grid=(n, r // tr, c // tc), in_specs=[half_spec] + ([] if sib is None else [spec]), out_specs=spec)
    return pl.pallas_call(
        body, name=name, grid_spec=grid_spec, out_shape=jax.ShapeDtypeStruct((n, r, c), BF16),
        compiler_params=_params(("parallel", "parallel", "parallel")),
    )(which, full, *([] if sib is None else [sib]))


def _sum_chips(recv, name):
    _, r, c = recv.shape
    tr, tc = _elementwise_tiles(r, c)

    def body(x_ref, o_ref):
        acc = x_ref[0].astype(F32)
        for j in range(1, N_CHIP):
            acc = acc + x_ref[j].astype(F32)
        o_ref[...] = acc

    return pl.pallas_call(
        body, name=name, grid=(r // tr, c // tc), in_specs=[pl.BlockSpec((N_CHIP, tr, tc), lambda i, j: (0, i, j))],
        out_specs=pl.BlockSpec((tr, tc), lambda i, j: (i, j)), out_shape=jax.ShapeDtypeStruct((r, c), F32),
        compiler_params=_params(("parallel", "parallel")),
    )(recv)


def _adamw(w, g_parts, m, v, name, by_core=False, outer=False):
    r, c = w.shape
    by_rows = r % 32 == 0
    if by_core:
        tr, tc = _elementwise_tiles(*((r // 2, c) if by_rows else (r, c // 2)))
        nbh = (r // 2) // tr if by_rows else (c // 2) // tc
    else:
        tr, tc = _elementwise_tiles(r, c)
    n_g = len(g_parts)
    c1 = 1.0 / (1.0 - ADAM_B1 ** ADAM_STEP)
    c2 = 1.0 / (1.0 - ADAM_B2 ** ADAM_STEP)

    def body(*refs):
        w_ref, g_refs, m_ref, v_ref = refs[0], refs[1:1 + n_g], refs[1 + n_g], refs[2 + n_g]
        g_out, d_out, m_out, v_out = refs[3 + n_g:]
        if by_core:
            in_my_half = (pl.program_id(0 if by_rows else 1) // nbh) == lax.axis_index("c")
            g = jnp.where(in_my_half, g_refs[0][...], g_refs[1][...])
        elif outer:
            g = lax.dot_general(_silu(g_refs[0][...]), g_refs[1][...], TN_DIMS, precision=HI, preferred_element_type=F32)
        else:
            g = g_refs[0][...]
            for extra in g_refs[1:]:
                g = g + extra[...]
        mn = ADAM_B1 * m_ref[...] + (1.0 - ADAM_B1) * g
        vn = ADAM_B2 * v_ref[...] + (1.0 - ADAM_B2) * (g * g)
        g_out[...] = g
        m_out[...] = mn
        v_out[...] = vn
        d_out[...] = -ADAM_LR * ((mn * c1) / (jnp.sqrt(vn * c2) + ADAM_EPS) + ADAM_WD * w_ref[...])

    spec = pl.BlockSpec((tr, tc), lambda i, j: (i, j))
    if by_core:
        g_spec = pl.BlockSpec((tr, tc), (lambda i, j: (i % nbh, j)) if by_rows else (lambda i, j: (i, j % nbh)))
    else:
        g_spec = spec
    g_specs = [g_spec] * n_g
    if outer:
        g_specs = [pl.BlockSpec((N_DEV, tr), lambda i, j: (0, i)), pl.BlockSpec((N_DEV, tc), lambda i, j: (0, j))]
    return pl.pallas_call(
        body, name=name, grid=(r // tr, c // tc), in_specs=[spec] + g_specs + [spec] * 2, out_specs=[spec] * 4,
        out_shape=[jax.ShapeDtypeStruct((r, c), F32)] * 4, compiler_params=_params(("parallel", "parallel")),
    )(w, *g_parts, m, v)


def _conv_taps(u, t):
    rows = lax.broadcasted_iota(jnp.int32, u.shape, 0)
    return [u] + [jnp.where(rows >= dd, pltpu.roll(u, dd, 0), 0.0) for dd in range(1, CONV_K)]


def _conv_fwd(proj, conv_w, col0, name):
    t = proj.shape[0]
    ch = conv_w.shape[1]

    def body(u_ref, w_ref, o_ref):
        taps = _conv_taps(u_ref[...], t)
        wv = w_ref[...]
        y = taps[0] * wv[CONV_K - 1:CONV_K]
        for dd in range(1, CONV_K):
            y = y + taps[dd] * wv[CONV_K - 1 - dd:CONV_K - dd]
        o_ref[...] = _silu(y)

    return pl.pallas_call(
        body, name=name, grid=(ch // LANES,),
        in_specs=[pl.BlockSpec((t, LANES), lambda j: (0, col0 + j)), pl.BlockSpec((CONV_K, LANES), lambda j: (0, j))],
        out_specs=pl.BlockSpec((t, LANES), lambda j: (0, j)), out_shape=jax.ShapeDtypeStruct((t, ch), F32),
        compiler_params=_params(("parallel",)),
    )(proj, conv_w)


def _conv_bwd(proj, conv_w, ds, col0, name):
    t = proj.shape[0]
    ch = conv_w.shape[1]

    def body(u_ref, w_ref, ds_ref, du_ref, dw_ref):
        u = u_ref[...]
        taps = _conv_taps(u, t)
        wv = w_ref[...]
        y = taps[0] * wv[CONV_K - 1:CONV_K]
        for dd in range(1, CONV_K):
            y = y + taps[dd] * wv[CONV_K - 1 - dd:CONV_K - dd]
        sg = jax.nn.sigmoid(y)
        dy = ds_ref[...] * (sg * (1.0 + y * (1.0 - sg)))
        rows = lax.broadcasted_iota(jnp.int32, u.shape, 0)
        du = dy * wv[CONV_K - 1:CONV_K]
        for dd in range(1, CONV_K):
            ahead = jnp.where(rows < t - dd, pltpu.roll(dy, t - dd, 0), 0.0)
            du = du + ahead * wv[CONV_K - 1 - dd:CONV_K - dd]
        du_ref[...] = du.astype(du_ref.dtype)
        dws = [jnp.sum(dy * taps[CONV_K - 1 - j], axis=0, keepdims=True) for j in range(CONV_K)]
        dw_ref[...] = jnp.concatenate(dws, axis=0)

    return pl.pallas_call(
        body, name=name, grid=(ch // LANES,),
        in_specs=[pl.BlockSpec((t, LANES), lambda j: (0, col0 + j)), pl.BlockSpec((CONV_K, LANES), lambda j: (0, j)),
                  pl.BlockSpec((t, LANES), lambda j: (0, j))],
        out_specs=[pl.BlockSpec((t, LANES), lambda j: (0, j)), pl.BlockSpec((CONV_K, LANES), lambda j: (0, j))],
        out_shape=[jax.ShapeDtypeStruct((t, ch), BF16), jax.ShapeDtypeStruct((CONV_K, ch), F32)],
        compiler_params=_params(("parallel",)),
    )(proj, conv_w, ds)


def _hg_block(st, q, fl, vi, g, l0, l1, nw):
    hs = range(len(st))
    tb = q[0].shape[0]
    ln = HG_SUB
    lb = [jax.nn.sigmoid(l0[h] - l1[h]) for h in hs]
    rows = lax.broadcasted_iota(jnp.int32, (ln, HEAD), 0)
    tri = (lax.broadcasted_iota(jnp.int32, (ln, ln), 0) >= lax.broadcasted_iota(jnp.int32, (ln, ln), 1)).astype(F32)
    st = list(st)
    outs = [[] for _ in hs]
    for i in range(tb // ln):
        sl = slice(i * ln, (i + 1) * ln)
        qs, vs = [q[h][sl] for h in hs], [vi[h][sl] for h in hs]
        f = [lb[h] + (1.0 - lb[h]) * jax.nn.sigmoid(fl[h][sl]) for h in hs]
        k = [1.0 - f[h] for h in hs]
        b = [jnp.dot(tri, jnp.log(f[h]), precision=HI, preferred_element_type=F32) for h in hs]
        o = [lax.dot_general((qs[h] * jnp.exp(b[h])).astype(BF16), st[h].astype(BF16), NT_DIMS, preferred_element_type=F32)
             for h in hs]
        n_tiles = ln // SUBLANES
        acc = [o] + [[jnp.zeros((ln - SUBLANES * ti, HEAD), F32) for _ in hs] for ti in range(1, n_tiles)]
        for s in range(ln):
            ti = s // SUBLANES
            r0 = ti * SUBLANES
            e = [jnp.exp(jnp.where(rows[r0:] >= s, b[h][r0:] - b[h][s:s + 1], -1e30)) for h in hs]
            a = [jnp.sum(qs[h][r0:] * e[h] * k[h][s:s + 1], axis=-1, keepdims=True) for h in hs]
            acc[ti] = [acc[ti][h] + a[h] * vs[h][s:s + 1] for h in hs]
        o = [jnp.concatenate([sum(acc[ti][h][(j - ti) * SUBLANES:(j - ti + 1) * SUBLANES] for ti in range(j + 1))
                              for j in range(n_tiles)], axis=0) for h in hs]
        kt = [k[h] * jnp.exp(b[h][ln - 1:ln] - b[h]) for h in hs]
        upd = [lax.dot_general(vs[h].astype(BF16), kt[h].astype(BF16), TN_DIMS, preferred_element_type=F32) for h in hs]
        st = [st[h] * jnp.exp(b[h][ln - 1:ln]) + upd[h] for h in hs]
        for h in hs:
            outs[h].append(o[h])
    o = [jnp.concatenate(outs[h], axis=0) for h in hs]
    out = [o[h] * _rms_scale(o[h]) * nw * _silu(g[h]) for h in hs]
    return st, out


def _head_cols(h):
    return slice(h * HEAD, (h + 1) * HEAD)


def _head_groups(n_heads, group):
    g = min(group, n_heads)
    return [list(range(i, min(i + g, n_heads))) for i in range(0, n_heads, g)]


def _hg_in_specs(n_heads, tb, time_index):
    hw = n_heads * HEAD
    cols = [pl.BlockSpec((tb, hw), functools.partial(lambda part, j: (time_index(j), part), part)) for part in range(4)]
    head_rows = pl.BlockSpec((n_heads, 1, HEAD), lambda j: (0, 0, 0))
    return cols + [head_rows, head_rows, pl.BlockSpec((1, HEAD), lambda j: (0, 0))]


def _hgrn2_fwd(proj, l0, l1, nw, n_heads, name):
    t = proj.shape[0]
    hw = n_heads * HEAD
    tb = _tile(t, HG_BLOCK, HG_SUB)
    nb = t // tb

    def body(q_ref, f_ref, i_ref, g_ref, l0_ref, l1_ref, nw_ref, o_ref, save_ref, st_ref):
        @pl.when(pl.program_id(0) == 0)
        def _():
            st_ref[...] = jnp.zeros_like(st_ref)

        for hs in _head_groups(n_heads, HG_FWD_GROUP):
            st = [st_ref[h] for h in hs]
            for h, s in zip(hs, st):
                save_ref[h] = s
            st, out = _hg_block(st, *[[r[:, _head_cols(h)] for h in hs] for r in (q_ref, f_ref, i_ref, g_ref)],
                                [l0_ref[h] for h in hs], [l1_ref[h] for h in hs], nw_ref[...])
            for h, s, o in zip(hs, st, out):
                st_ref[h] = s
                o_ref[:, _head_cols(h)] = o.astype(o_ref.dtype)

    return pl.pallas_call(
        body, name=name, grid=(nb,), in_specs=_hg_in_specs(n_heads, tb, lambda j: j),
        out_specs=[pl.BlockSpec((tb, hw), lambda j: (j, 0)),
                   pl.BlockSpec((None, n_heads, HEAD, HEAD), lambda j: (j, 0, 0, 0))],
        out_shape=[jax.ShapeDtypeStruct((t, hw), BF16), jax.ShapeDtypeStruct((nb, n_heads, HEAD, HEAD), F32)],
        scratch_shapes=[pltpu.VMEM((n_heads, HEAD, HEAD), F32)], compiler_params=_params(("arbitrary",)),
    )(proj, proj, proj, proj, l0, l1, nw)


def _hgrn2_bwd(proj, l0, l1, nw, saved, d_ocat, n_heads, name):
    t = proj.shape[0]
    tb = _tile(t, HG_BLOCK, HG_SUB)
    nb = t // tb
    rev = lambda j: nb - 1 - j

    hw = n_heads * HEAD

    def body(q_ref, f_ref, i_ref, g_ref, l0_ref, l1_ref, nw_ref, save_ref, do_ref,
             dp_ref, dl0_ref, dl1_ref, dnw_ref, dst_ref):
        @pl.when(pl.program_id(0) == 0)
        def _():
            dst_ref[...] = jnp.zeros_like(dst_ref)
            dl0_ref[...] = jnp.zeros_like(dl0_ref)
            dl1_ref[...] = jnp.zeros_like(dl1_ref)
            dnw_ref[...] = jnp.zeros_like(dnw_ref)

        dnw_acc = jnp.zeros((1, HEAD), F32)
        for hs in _head_groups(n_heads, HG_BWD_GROUP):
            _, vjp = jax.vjp(_hg_block, [save_ref[h] for h in hs],
                             *[[r[:, _head_cols(h)] for h in hs] for r in (q_ref, f_ref, i_ref, g_ref)],
                             [l0_ref[h] for h in hs], [l1_ref[h] for h in hs], nw_ref[...])
            dst, dq, df, di, dg, dl0, dl1, dnw = vjp(([dst_ref[h] for h in hs], [do_ref[:, _head_cols(h)] for h in hs]))
            for i, h in enumerate(hs):
                dst_ref[h] = dst[i]
                for part, val in enumerate((dq, df, di, dg)):
                    dp_ref[:, part * hw + h * HEAD:part * hw + (h + 1) * HEAD] = val[i].astype(dp_ref.dtype)
                dl0_ref[h] += dl0[i]
                dl1_ref[h] += dl1[i]
            dnw_acc = dnw_acc + dnw
        dnw_ref[...] += dnw_acc

    head_rows = pl.BlockSpec((n_heads, 1, HEAD), lambda j: (0, 0, 0))
    return pl.pallas_call(
        body, name=name, grid=(nb,),
        in_specs=_hg_in_specs(n_heads, tb, rev) + [pl.BlockSpec((None, n_heads, HEAD, HEAD), lambda j: (rev(j), 0, 0, 0)),
                                                   pl.BlockSpec((tb, hw), lambda j: (rev(j), 0))],
        out_specs=[pl.BlockSpec((tb, 4 * hw), lambda j: (rev(j), 0)), head_rows, head_rows,
                   pl.BlockSpec((1, HEAD), lambda j: (0, 0))],
        out_shape=[jax.ShapeDtypeStruct((t, 4 * hw), BF16)] + [jax.ShapeDtypeStruct((n_heads, 1, HEAD), F32)] * 2
        + [jax.ShapeDtypeStruct((1, HEAD), F32)],
        scratch_shapes=[pltpu.VMEM((n_heads, HEAD, HEAD), F32)], compiler_params=_params(("arbitrary",)),
    )(proj, proj, proj, proj, l0, l1, nw, saved, d_ocat)


NN_DIMS = (((1,), (0,)), ((), ()))


def _split_bf16(x):
    hi = x.astype(BF16)
    return hi, (x - hi.astype(F32)).astype(BF16)


def _dot3(a, b, dims=NN_DIMS):
    (ah, al), (bh, bl) = _split_bf16(a), _split_bf16(b)
    dot = functools.partial(lax.dot_general, dimension_numbers=dims, preferred_element_type=F32)
    return dot(ah, bh) + dot(ah, bl) + dot(al, bh)


@jax.custom_vjp
def _mm3(a, b):
    return _dot3(a, b)


def _mm3_fwd(a, b):
    return _dot3(a, b), (a, b)


def _mm3_bwd(res, g):
    a, b = res
    return _dot3(g, b, NT_DIMS), _dot3(a, g, TN_DIMS)


_mm3.defvjp(_mm3_fwd, _mm3_bwd)


def _dot_bf16(a, b, dims=(((1,), (0,)), ((), ()))):
    return lax.dot_general(a.astype(BF16), b.astype(BF16), dims, preferred_element_type=F32)


def _inv_unit_lower_raw(ms):
    hs = range(len(ms))
    c = ms[0].shape[0]
    r = lax.broadcasted_iota(jnp.int32, (c, c), 0)
    q = lax.broadcasted_iota(jnp.int32, (c, c), 1)
    eye = (r == q).astype(F32)
    md = [jnp.where((r // GDN_INV_BLOCK) == (q // GDN_INV_BLOCK), ms[h], 0.0) for h in hs]
    p = [-md[h] for h in hs]
    t16 = [eye + p[h] for h in hs]
    for _ in range(int(math.log2(GDN_INV_BLOCK)) - 1):
        p = [_dot3(p[h], p[h]) for h in hs]
        t16 = [t16[h] + _dot3(t16[h], p[h]) for h in hs]
    p = [-_dot3(t16[h], ms[h] - md[h]) for h in hs]
    t2 = [eye + p[h] for h in hs]
    for _ in range(int(math.log2(c // GDN_INV_BLOCK)) - 1):
        p = [_dot3(p[h], p[h]) for h in hs]
        t2 = [t2[h] + _dot3(t2[h], p[h]) for h in hs]
    return [_dot3(t2[h], t16[h]) for h in hs]


@jax.custom_vjp
def _inv_unit_lower(ms):
    return _inv_unit_lower_raw(ms)


def _inv_fwd(ms):
    ts = _inv_unit_lower_raw(ms)
    return ts, ts


def _inv_bwd(ts, dts):
    hs = range(len(ts))
    inner = [_dot3(ts[h], dts[h], TN_DIMS) for h in hs]
    return ([-_dot3(inner[h], ts[h], NT_DIMS) for h in hs],)


_inv_unit_lower.defvjp(_inv_fwd, _inv_bwd)


def _gdn_block(precise, onehots, st, qc, kc, vc, g, ab, alog_row, dtb_row, nw):
    inverse, dot3 = precise
    hs = range(len(st))
    c = qc[0].shape[0]
    lane_sum = lambda v: jnp.sum(v, axis=-1, keepdims=True)
    a = [lane_sum(ab * onehots[h][0]) for h in hs]
    bb = [lane_sum(ab * onehots[h][1]) for h in hs]
    alog = [lane_sum(alog_row * onehots[h][0]) for h in hs]
    dtb = [lane_sum(dtb_row * onehots[h][0]) for h in hs]
    la = [-jnp.exp(alog[h]) * _softplus(a[h] + dtb[h]) for h in hs]
    beta = [jax.nn.sigmoid(bb[h]) for h in hs]
    q = [qc[h] * lax.rsqrt(lane_sum(qc[h] * qc[h]) + EPS) * (HEAD ** -0.5) for h in hs]
    k = [kc[h] * lax.rsqrt(lane_sum(kc[h] * kc[h]) + EPS) for h in hs]
    r = lax.broadcasted_iota(jnp.int32, (c, c), 0)
    s = lax.broadcasted_iota(jnp.int32, (c, c), 1)
    tri = (r >= s).astype(F32)
    g_cc = [dot3(tri, jnp.broadcast_to(la[h], (c, c))) for h in hs]
    g_cl = [dot3(tri, jnp.broadcast_to(la[h], (c, HEAD))) for h in hs]
    gamma = [jnp.exp(jnp.where(r >= s, g_cc[h] - g_cc[h].T, -1e30)) for h in hs]
    kk = [_dot_bf16(k[h], k[h], NT_DIMS) for h in hs]
    m = [jnp.where(r > s, beta[h] * kk[h] * gamma[h], 0.0) for h in hs]
    tm = inverse(m)
    eg = [jnp.exp(g_cl[h]) for h in hs]
    rhs = [jnp.concatenate([vc[h] * beta[h], k[h] * (beta[h] * eg[h])], axis=1) for h in hs]
    sol = [dot3(tm[h], rhs[h]) for h in hs]
    qk = [_dot_bf16(q[h], k[h], NT_DIMS) * gamma[h] for h in hs]
    g_last = [g_cl[h][c - 1:c] for h in hs]
    k_tail = [k[h] * jnp.exp(g_last[h] - g_cl[h]) for h in hs]
    v_new = [sol[h][:, :HEAD] - _dot_bf16(sol[h][:, HEAD:], st[h], NT_DIMS) for h in hs]
    o_st = [_dot_bf16(q[h] * eg[h], st[h], NT_DIMS) for h in hs]
    o = [o_st[h] + _dot_bf16(qk[h], v_new[h]) for h in hs]
    upd = [_dot_bf16(v_new[h], k_tail[h], TN_DIMS) for h in hs]
    st = [st[h] * jnp.exp(g_last[h]) + upd[h] for h in hs]
    out = [o[h] * _rms_scale(o[h]) * nw * _silu(g[h]) for h in hs]
    return st, out


def _head_onehots(n_heads, h):
    lane = lax.broadcasted_iota(jnp.int32, (1, LANES), 1)
    return (lane == h).astype(F32), (lane == n_heads + h).astype(F32)


def _gdn_in_specs(n_heads, c, time_index):
    hw = n_heads * HEAD
    qkv = [pl.BlockSpec((c, hw), functools.partial(lambda part, j: (time_index(j), part), part)) for part in range(3)]
    row = pl.BlockSpec((1, LANES), lambda j: (0, 0))
    return qkv + [pl.BlockSpec((c, hw), lambda j: (time_index(j), 7)),
                  pl.BlockSpec((c, LANES), lambda j: (time_index(j), 8 * n_heads)), row, row, row]


def _gdn_fwd(qkv, proj, alog_row, dtb_row, nw, n_heads, name):
    t = qkv.shape[0]
    hw = n_heads * HEAD
    c = _tile(t, GDN_CHUNK, GDN_CHUNK)
    nb = t // c

    def body(q_ref, k_ref, v_ref, g_ref, ab_ref, al_ref, dt_ref, nw_ref, o_ref, save_ref, st_ref):
        @pl.when(pl.program_id(0) == 0)
        def _():
            st_ref[...] = jnp.zeros_like(st_ref)

        for hs in _head_groups(n_heads, GDN_FWD_GROUP):
            st = [st_ref[h] for h in hs]
            for h, s in zip(hs, st):
                save_ref[h] = s
            st, out = _gdn_block((_inv_unit_lower_raw, _dot3), [_head_onehots(n_heads, h) for h in hs], st,
                                 *[[r[:, _head_cols(h)] for h in hs] for r in (q_ref, k_ref, v_ref, g_ref)],
                                 ab_ref[...], al_ref[...], dt_ref[...], nw_ref[...])
            for h, s, o in zip(hs, st, out):
                st_ref[h] = s
                o_ref[:, _head_cols(h)] = o.astype(o_ref.dtype)

    return pl.pallas_call(
        body, name=name, grid=(nb,), in_specs=_gdn_in_specs(n_heads, c, lambda j: j),
        out_specs=[pl.BlockSpec((c, hw), lambda j: (j, 0)),
                   pl.BlockSpec((None, n_heads, HEAD, HEAD), lambda j: (j, 0, 0, 0))],
        out_shape=[jax.ShapeDtypeStruct((t, hw), BF16), jax.ShapeDtypeStruct((nb, n_heads, HEAD, HEAD), F32)],
        scratch_shapes=[pltpu.VMEM((n_heads, HEAD, HEAD), F32)], compiler_params=_params(("arbitrary",)),
    )(qkv, qkv, qkv, proj, proj, alog_row, dtb_row, nw)


def _gdn_bwd(qkv, proj, alog_row, dtb_row, nw, saved, d_ocat, n_heads, name):
    t = qkv.shape[0]
    c = _tile(t, GDN_CHUNK, GDN_CHUNK)
    nb = t // c
    rev = lambda j: nb - 1 - j

    hw = n_heads * HEAD

    def body(q_ref, k_ref, v_ref, g_ref, ab_ref, al_ref, dt_ref, nw_ref, save_ref, do_ref,
             dqkv_ref, dg_ref, dab_ref, dal_ref, ddt_ref, dnw_ref, dst_ref):
        @pl.when(pl.program_id(0) == 0)
        def _():
            dst_ref[...] = jnp.zeros_like(dst_ref)
            dal_ref[...] = jnp.zeros_like(dal_ref)
            ddt_ref[...] = jnp.zeros_like(ddt_ref)
            dnw_ref[...] = jnp.zeros_like(dnw_ref)

        dab_acc = jnp.zeros((c, LANES), F32)
        row_acc = [jnp.zeros((1, LANES), F32)] * 3
        for hs in _head_groups(n_heads, GDN_BWD_GROUP):
            fn = functools.partial(_gdn_block, (_inv_unit_lower, _mm3), [_head_onehots(n_heads, h) for h in hs])
            _, vjp = jax.vjp(fn, [save_ref[h] for h in hs],
                             *[[r[:, _head_cols(h)] for h in hs] for r in (q_ref, k_ref, v_ref, g_ref)],
                             ab_ref[...], al_ref[...], dt_ref[...], nw_ref[...])
            dst, dq, dk, dv, dg, dab, dal, ddt, dnw = vjp(([dst_ref[h] for h in hs], [do_ref[:, _head_cols(h)] for h in hs]))
            for i, h in enumerate(hs):
                dst_ref[h] = dst[i]
                for part, val in enumerate((dq, dk, dv)):
                    dqkv_ref[:, part * hw + h * HEAD:part * hw + (h + 1) * HEAD] = val[i]
                dg_ref[:, _head_cols(h)] = dg[i].astype(dg_ref.dtype)
            dab_acc = dab_acc + dab
            row_acc = [acc + val for acc, val in zip(row_acc, (dal, ddt, dnw))]
        dab_ref[...] = dab_acc
        dal_ref[...] += row_acc[0]
        ddt_ref[...] += row_acc[1]
        dnw_ref[...] += row_acc[2]

    row = pl.BlockSpec((1, LANES), lambda j: (0, 0))
    return pl.pallas_call(
        body, name=name, grid=(nb,),
        in_specs=_gdn_in_specs(n_heads, c, rev) + [pl.BlockSpec((None, n_heads, HEAD, HEAD), lambda j: (rev(j), 0, 0, 0)),
                                                   pl.BlockSpec((c, hw), lambda j: (rev(j), 1))],
        out_specs=[pl.BlockSpec((c, 3 * hw), lambda j: (rev(j), 0)), pl.BlockSpec((c, hw), lambda j: (rev(j), 0)),
                   pl.BlockSpec((c, LANES), lambda j: (rev(j), 0)), row, row, row],
        out_shape=[jax.ShapeDtypeStruct((t, 3 * hw), F32), jax.ShapeDtypeStruct((t, hw), BF16),
                   jax.ShapeDtypeStruct((t, LANES), F32)] + [jax.ShapeDtypeStruct((1, LANES), F32)] * 3,
        scratch_shapes=[pltpu.VMEM((n_heads, HEAD, HEAD), F32)], compiler_params=_params(("arbitrary",)),
    )(qkv, qkv, qkv, proj, proj, alog_row, dtb_row, nw, saved, d_ocat)


def _pad_lanes(v, n):
    v = v.reshape(1, -1)
    return jnp.pad(v, ((0, 0), (0, n - v.shape[1])))


def _pack_rows(vecs):
    flat = jnp.concatenate([v.reshape(-1) for v in vecs])
    offs, o = [], 0
    for v in vecs:
        offs.append((o, v.size))
        o += v.size
    per_row = -(-o // (SUBLANES * LANES)) * LANES
    flat = jnp.pad(flat, (0, SUBLANES * per_row - o))
    return flat.reshape(SUBLANES, per_row), offs


def _unpack(gathered, offs):
    per_dev = gathered.reshape(N_DEV, -1)
    return [per_dev[:, o:o + n] for o, n in offs]


def _sum_devices(part):
    acc = part[0]
    for i in range(1, N_DEV):
        acc = acc + part[i]
    return acc


def kernel(x, c, w_ada, b_ada, pre_mix_norm, post_mix_norm, pre_ffn_norm, post_ffn_norm, w_in, hg_lb_logits, hg_norm, gdn_conv_w, gdn_a_log, gdn_dt_bias, gdn_norm, w_out, w_ff1, w_ff2, loss_target, m_w_ada, m_b_ada, m_pre_mix_norm, m_post_mix_norm, m_pre_ffn_norm, m_post_ffn_norm, m_w_in, m_hg_lb_logits, m_hg_norm, m_gdn_conv_w, m_gdn_a_log, m_gdn_dt_bias, m_gdn_norm, m_w_out, m_w_ff1, m_w_ff2, v_w_ada, v_b_ada, v_pre_mix_norm, v_post_mix_norm, v_pre_ffn_norm, v_post_ffn_norm, v_w_in, v_hg_lb_logits, v_hg_norm, v_gdn_conv_w, v_gdn_a_log, v_gdn_dt_bias, v_gdn_norm, v_w_out, v_w_ff1, v_w_ff2):
    assert x.shape[0] == 1 and w_ada.shape[0] == 1 and hg_lb_logits.shape[0] == 2
    t, d = x.shape[1], x.shape[2]
    n_heads = (d // 2) // HEAD
    hw = n_heads * HEAD
    in_cols = 8 * hw + 2 * n_heads
    np_cols = 8 * hw + 2 * LANES
    d_ff = w_ff1.shape[2] * N_CHIP
    na = w_ada.shape[2]
    ax, ay, ac = lax.axis_index("x"), lax.axis_index("y"), lax.axis_index("c")
    chip = 2 * ax + ay
    dev = 4 * ax + 2 * ay + ac

    x2d, tgt = x[0], loss_target[0]

    pack1, offs1 = _pack_rows([c[0], gdn_conv_w[0]])
    c_all, convw_all = _unpack(_gather8(pack1, "gather_cond"), offs1)
    conv_sh = gdn_conv_w.shape[2]
    conv_w = jnp.concatenate([convw_all[2 * j].reshape(CONV_K, conv_sh) for j in range(N_CHIP)], axis=1)

    b_s = lax.dynamic_slice(b_ada, (0, chip * na), (1, na))
    mod_part = _mod_part(c_all, w_ada[0], b_s, "mod_part")
    pack2, offs2 = _pack_rows([mod_part])
    (mod_parts,) = _unpack(_gather8(pack2, "gather_mod"), offs2)
    mod_all = jnp.concatenate([mod_parts[2 * j].reshape(N_DEV, na) for j in range(N_CHIP)], axis=1)
    mod = lax.dynamic_slice(mod_all, (dev, 0), (1, N_MOD * d))
    sh_m, sc_m, gt_m, sh_f, sc_f, gt_f = [mod[:, i * d:(i + 1) * d] for i in range(N_MOD)]

    h1 = _norm_mod(x2d, pre_mix_norm, sc_m, sh_m, "norm_mod_mix")
    (g_in,), h1 = _gather_weights([jnp.transpose(w_in[0]).astype(BF16)], "gather_w_in", sequencer_id=5, after=h1)
    late, g_in = lax.optimization_barrier(([w_out[0].astype(BF16), w_ff1[0].astype(BF16), w_ff2[0].astype(BF16)], g_in))
    g_out, g_ff1, g_ff2 = _gather_weights(late, "gather_weights_late", sequencer_id=1)
    w_in_f = jnp.pad(g_in.reshape(in_cols, d), ((0, np_cols - in_cols), (0, 0)))
    w_out_f = g_out.reshape(d, d)
    w_ff2_f = g_ff2.reshape(d_ff, d)

    proj = _matmul(h1, w_in_f, "nt", F32, "mm_in", tn=768)
    l0, l1 = hg_lb_logits[0].reshape(n_heads, 1, HEAD), hg_lb_logits[1].reshape(n_heads, 1, HEAD)
    o_hg, hg_saved = _hgrn2_fwd(proj, l0, l1, hg_norm, n_heads, "hgrn2_fwd")
    qkv = _conv_fwd(proj, conv_w, 4 * n_heads, "conv_fwd")
    alog_row, dtb_row = _pad_lanes(gdn_a_log, LANES), _pad_lanes(gdn_dt_bias, LANES)
    o_gdn, gdn_saved = _gdn_fwd(qkv, proj, alog_row, dtb_row, gdn_norm, n_heads, "gdn_fwd")
    o_cat = jnp.concatenate([o_hg, o_gdn], axis=1)
    y1 = _matmul(o_cat, w_out_f, "nn", F32, "mm_out")
    x_mid, h2 = _resid_norm_mod(x2d, y1, post_mix_norm, gt_m, pre_ffn_norm, sc_f, sh_f, "resid_mix_norm_mod_ffn")

    relu_a1, r1 = _matmul(h2, g_ff1, "nn", BF16, "mm_ff1", relu2=True, b_split=True)
    y2 = _matmul(r1, w_ff2_f, "nn", F32, "mm_ff2")
    d_out, loss_row, dy2, d_gt_f, d_post_ffn = _loss_head(x_mid, y2, post_ffn_norm, gt_f, tgt, "loss_head")

    in_sh = in_cols // N_CHIP
    ff_sh = d_ff // N_CHIP
    my_half = jnp.reshape(ac, (1,)).astype(jnp.int32)

    def start_reduce(by_chip, tag, collective_id):
        to_sib = [_row_half_to_bf16(a, 1 - my_half, None, f"sibling_half_{tag}{i}") for i, a in enumerate(by_chip)]
        from_sib = _sibling_exchange(to_sib, f"sibling_partials_{tag}")
        chip_part = [_row_half_to_bf16(a, my_half, s, f"add_halves_{tag}{i}") for i, (a, s) in enumerate(zip(by_chip, from_sib))]
        return _sequencer_chip_exchange(chip_part, f"scatter_grads_{tag}", collective_id)

    gw_ff2 = _matmul(r1, dy2, "tn", BF16, "mm_ff2_dw")
    gw_ff2, dy2 = lax.optimization_barrier((gw_ff2, dy2))
    da1 = _matmul(dy2, w_ff2_f, "nt", BF16, "mm_ff2_dx", times=relu_a1)
    gw_ff1 = _matmul(h2, da1, "tn", BF16, "mm_ff1_dw", out_split=True)
    gw_ff1, da1 = lax.optimization_barrier((gw_ff1, da1))
    recv_ff2, recv_ff1 = _sequencer_chip_exchange([gw_ff2.reshape(N_CHIP, ff_sh, d), gw_ff1], "scatter_grads_ff", 2)
    dh2 = _matmul(da1, g_ff1, "nt", BF16, "mm_ff1_dx", b_split=True)
    d_mid, d_pre_ffn, d_sc_f, d_sh_f = _norm_mod_bwd(x_mid, pre_ffn_norm, sc_f, dh2, d_out, "norm_mod_ffn_bwd")

    dy1, d_gt_m, d_post_mix = _resid_bwd(d_mid, y1, post_mix_norm, gt_m, "resid_mix_bwd")
    gw_out = _matmul(o_cat, dy1, "tn", BF16, "mm_out_dw")
    gw_out, dy1 = lax.optimization_barrier((gw_out, dy1))
    (recv_out,) = _sequencer_chip_exchange([gw_out.reshape(N_CHIP, d // N_CHIP, d)], "scatter_grads_out", 3)
    d_ocat = _matmul(dy1, w_out_f, "nt", F32, "mm_out_dx")
    dp_hg, dl0, dl1, d_hg_norm = _hgrn2_bwd(proj, l0, l1, hg_norm, hg_saved, d_ocat, n_heads, "hgrn2_bwd")
    dqkv, dg_g, dab, d_alog, d_dtb, d_gdn_norm = _gdn_bwd(
        qkv, proj, alog_row, dtb_row, gdn_norm, gdn_saved, d_ocat, n_heads, "gdn_bwd")
    du, d_conv_w = _conv_bwd(proj, conv_w, dqkv, 4 * n_heads, "conv_bwd")
    dproj = jnp.concatenate([dp_hg, du, dg_g, dab.astype(BF16), jnp.zeros((t, LANES), BF16)], axis=1)
    gw_in = _matmul(dproj, h1, "tn", F32, "mm_in_dw", tm=768)
    (recv_in,) = start_reduce([gw_in[:in_cols].reshape(N_CHIP, in_sh, d)], "in", 4)
    dh1 = _matmul(dproj, w_in_f, "nn", BF16, "mm_in_dx", tk=2816)
    grad_x, d_pre_mix, d_sc_m, d_sh_m = _norm_mod_bwd(x2d, pre_mix_norm, sc_m, dh1, d_mid, "norm_mod_mix_bwd")

    d_mod = jnp.concatenate([d_sh_m, d_sc_m, d_gt_m, d_sh_f, d_sc_f, d_gt_f], axis=1)
    d_lb_logits = jnp.stack([dl0.reshape(n_heads, HEAD), dl1.reshape(n_heads, HEAD)])
    pack3, offs3 = _pack_rows([loss_row[0, :1], d_pre_mix, d_post_mix, d_pre_ffn, d_post_ffn, d_lb_logits, d_hg_norm,
                               d_conv_w, d_alog[0, :n_heads], d_dtb[0, :n_heads], d_gdn_norm, d_mod])
    parts = _unpack(_gather8(pack3, "gather_vec_grads"), offs3)
    sums = [_sum_devices(p) for p in parts[:-1]]
    loss = sums[0][0]
    dmod_all = parts[-1]
    g_b_ada = _sum_devices(dmod_all).reshape(1, N_MOD * d)
    g_conv_full = sums[7].reshape(CONV_K, N_CHIP * conv_sh)
    g_conv = lax.dynamic_slice(g_conv_full, (0, chip * conv_sh), (CONV_K, conv_sh))
    dmod_chip = lax.dynamic_slice(dmod_all, (0, chip * na), (N_DEV, na))

    sum_ff2 = _sum_chips(recv_ff2, "sum_chips_ff2")
    recv_ff1, sum_ff2 = lax.optimization_barrier((recv_ff1, sum_ff2))
    sum_ff1 = _sum_chips(recv_ff1, "sum_chips_ff1")
    recv_out, sum_ff1 = lax.optimization_barrier((recv_out, sum_ff1))
    sum_out = _sum_chips(recv_out, "sum_chips_out")
    recv_in, sum_out, grad_x = lax.optimization_barrier((recv_in, sum_out, grad_x))
    mine = [_sum_chips(recv_in, "sum_chips_in"), sum_out, sum_ff1, sum_ff2]
    theirs = _sibling_exchange(mine, "sibling_grads", sequencer_id=6)
    ada = _adamw(w_ada[0], [c_all, dmod_chip], m_w_ada[0], v_w_ada[0], "adamw_w_ada", outer=True)
    theirs, ada = lax.optimization_barrier((theirs, ada))

    big = {"w_ada": [o[None] for o in ada]}
    for i, (nm, w_, m_, v_) in enumerate([("w_in", w_in, m_w_in, v_w_in), ("w_out", w_out, m_w_out, v_w_out),
                                          ("w_ff1", w_ff1, m_w_ff1, v_w_ff1), ("w_ff2", w_ff2, m_w_ff2, v_w_ff2)]):
        if nm == "w_in":
            res_t = _adamw(jnp.transpose(w_[0]), [mine[i], theirs[i]], jnp.transpose(m_[0]), jnp.transpose(v_[0]),
                           f"adamw_{nm}", by_core=True)
            big[nm] = [jnp.transpose(o)[None] for o in res_t]
        else:
            big[nm] = [o[None] for o in _adamw(w_[0], [mine[i], theirs[i]], m_[0], v_[0], f"adamw_{nm}")]

    small_names = ["b_ada", "pre_mix_norm", "post_mix_norm", "pre_ffn_norm", "post_ffn_norm", "hg_lb_logits", "hg_norm",
                   "gdn_conv_w", "gdn_a_log", "gdn_dt_bias", "gdn_norm"]
    small_w = [b_ada, pre_mix_norm, post_mix_norm, pre_ffn_norm, post_ffn_norm, hg_lb_logits, hg_norm, gdn_conv_w,
               gdn_a_log, gdn_dt_bias, gdn_norm]
    small_m = [m_b_ada, m_pre_mix_norm, m_post_mix_norm, m_pre_ffn_norm, m_post_ffn_norm, m_hg_lb_logits, m_hg_norm,
               m_gdn_conv_w, m_gdn_a_log, m_gdn_dt_bias, m_gdn_norm]
    small_v = [v_b_ada, v_pre_mix_norm, v_post_mix_norm, v_pre_ffn_norm, v_post_ffn_norm, v_hg_lb_logits, v_hg_norm,
               v_gdn_conv_w, v_gdn_a_log, v_gdn_dt_bias, v_gdn_norm]
    small_g = [g_b_ada, sums[1], sums[2], sums[3], sums[4], sums[5], sums[6], g_conv, sums[8], sums[9], sums[10]]
    pw, offs_s = _pack_rows(small_w)
    pg, _ = _pack_rows(small_g)
    pm, _ = _pack_rows(small_m)
    pv, _ = _pack_rows(small_v)
    packed = _adamw(pw, [pg], pm, pv, "adamw_vectors")
    small = {}
    for nm, w_, (o, n) in zip(small_names, small_w, offs_s):
        small[nm] = [p.reshape(-1)[o:o + n].reshape(w_.shape) for p in packed]

    order = ["w_ada", "b_ada", "pre_mix_norm", "post_mix_norm", "pre_ffn_norm", "post_ffn_norm", "w_in", "hg_lb_logits",
             "hg_norm", "gdn_conv_w", "gdn_a_log", "gdn_dt_bias", "gdn_norm", "w_out", "w_ff1", "w_ff2"]
    res = {**big, **small}
    outs = [loss, grad_x[None]]
    for k in range(4):
        outs += [res[nm][k] for nm in order]
    return tuple(outs)
```

```python
import functools
import math

import jax
import jax.numpy as jnp
from jax import lax
from jax.experimental import pallas as pl
from jax.experimental.pallas import tpu as pltpu
from jax.experimental.pallas import tpu_sc as plsc

F32 = jnp.float32
BF16 = jnp.bfloat16
HI = lax.Precision.HIGHEST
MESH = pl.DeviceIdType.MESH

LANES = 128
SUBLANES = 8
VMEM_LIMIT = 48 * 1024 * 1024
EPS = 1e-6
HEAD = 128
CONV_K = 4
GDN_CHUNK = 64
GDN_INV_BLOCK = 16
HG_SUB = 16
HG_BLOCK = 128
HG_FWD_GROUP = 8
HG_BWD_GROUP = 4
GDN_FWD_GROUP = 8
GDN_BWD_GROUP = 8
N_MOD = 6
N_DEV = 8
N_CHIP = 4

ADAM_LR = 0.001
ADAM_B1 = 0.9
ADAM_B2 = 0.999
ADAM_EPS = 1e-08
ADAM_WD = 0.01
ADAM_STEP = 10

NT_DIMS = (((1,), (1,)), ((), ()))
TN_DIMS = (((0,), (0,)), ((), ()))


def _tile(dim, target, align):
    if dim <= target:
        return dim
    best = dim
    t = align
    while t <= target:
        if dim % t == 0:
            best = t
        t += align
    return best


def _elementwise_tiles(r, c):
    tc = _tile(c, 1024, LANES)
    tr = _tile(r, max(16, (256 * 1024) // tc // 16 * 16), 16)
    if tr == r and r * tc > 512 * 1024:
        tc = _tile(c, max(LANES, (256 * 1024) // r // LANES * LANES), LANES)
    return tr, tc


def _params(sem):
    return pltpu.CompilerParams(dimension_semantics=sem, vmem_limit_bytes=VMEM_LIMIT)


def _silu(x):
    return x * jax.nn.sigmoid(x)


def _softplus(x):
    pos = x > 0
    return jnp.where(pos, x, 0.0) + jnp.log(1.0 + jnp.exp(jnp.where(pos, -x, x)))


def _rms_scale(x):
    return lax.rsqrt(jnp.mean(x * x, axis=-1, keepdims=True) + EPS)


def _gather8(x_shard, name):
    m_per, n = x_shard.shape
    assert m_per % SUBLANES == 0 and n % LANES == 0

    def body(x_ref, out_ref, send_sems, recv_sems, local_sem):
        x, y, c = lax.axis_index("x"), lax.axis_index("y"), lax.axis_index("c")
        me, sibling = (x, y, c), (x, y, 1 - c)
        chips = [(1 - x, y), (x, 1 - y), (1 - x, 1 - y)]

        def rows(px, py, pc):
            return out_ref.at[pl.ds((4 * px + 2 * py + pc) * m_per, m_per), :]

        def copy(k, block, to, src=None):
            return pltpu.make_async_remote_copy(
                src_ref=rows(*block) if src is None else src, dst_ref=rows(*block),
                send_sem=send_sems.at[k], recv_sem=recv_sems.at[k], device_id=to, device_id_type=MESH)

        mine = pltpu.make_async_copy(x_ref, rows(*me), local_sem)
        mine.start()
        first = [copy(0, me, sibling, src=x_ref)]
        first += [copy(1 + j, me, (*chip, c), src=x_ref) for j, chip in enumerate(chips)]
        for cp in first:
            cp.start()
        passed = [copy(4 + j, (*chip, c), sibling) for j, chip in enumerate(chips)]
        for j, chip in enumerate(chips):
            copy(1 + j, (*chip, c), me).wait_recv()
            passed[j].start()
        copy(0, sibling, me).wait_recv()
        for j, chip in enumerate(chips):
            copy(4 + j, (*chip, 1 - c), me).wait_recv()
        for cp in first + passed:
            cp.wait_send()
        mine.wait()

    return pl.pallas_call(
        body, name=name,
        out_shape=jax.ShapeDtypeStruct((N_DEV * m_per, n), x_shard.dtype),
        in_specs=[pl.BlockSpec(memory_space=pltpu.VMEM)],
        out_specs=pl.BlockSpec(memory_space=pltpu.VMEM),
        scratch_shapes=[pltpu.SemaphoreType.DMA((7,)), pltpu.SemaphoreType.DMA((7,)), pltpu.SemaphoreType.DMA],
        compiler_params=pltpu.CompilerParams(vmem_limit_bytes=VMEM_LIMIT),
    )(x_shard)


def _gather_weights(arrs, name, sequencer_id=None, after=None):
    n = len(arrs)
    out_shapes = [jax.ShapeDtypeStruct((N_CHIP,) + a.shape, a.dtype) for a in arrs]

    def body(*refs):
        ins, outs = refs[:n], refs[n:2 * n]
        ici_send, ici_recv, d2d_send, d2d_recv = refs[2 * n:]
        if sequencer_id is not None:
            chips, sib = _chip_peers()
            _handshake(chips + [sib])
        x, y, c = lax.axis_index("x"), lax.axis_index("y"), lax.axis_index("c")
        me = 2 * x + y
        peers = [(1 - x, y), (x, 1 - y), (1 - x, 1 - y)]

        def half(a, cc):
            r = arrs[a].shape[0]
            cut = r // 32 * 16
            return pl.ds(0, cut) if cc == 0 else pl.ds(cut, r - cut)

        def exchange(mine):
            sibling = (x, y, 1 - mine)
            sent = []
            for a in range(n):
                for k, (px, py) in enumerate(peers):
                    cp = pltpu.make_async_remote_copy(
                        src_ref=ins[a].at[half(a, mine)], dst_ref=outs[a].at[me, half(a, mine)],
                        send_sem=ici_send.at[3 * a + k], recv_sem=ici_recv.at[3 * a + k],
                        device_id=(px, py, mine), device_id_type=MESH)
                    cp.start()
                    sent.append(cp)
            for a in range(n):
                for k, (px, py) in enumerate(peers):
                    landed = outs[a].at[2 * px + py, half(a, mine)]
                    pltpu.make_async_remote_copy(
                        src_ref=landed, dst_ref=landed, send_sem=ici_send.at[3 * a + k], recv_sem=ici_recv.at[3 * a + k],
                        device_id=(px, py, mine), device_id_type=MESH).wait_recv()
                    fwd = pltpu.make_async_remote_copy(
                        src_ref=landed, dst_ref=landed, send_sem=d2d_send.at[3 * a + k], recv_sem=d2d_recv.at[3 * a + k],
                        device_id=sibling, device_id_type=MESH)
                    fwd.start()
                    sent.append(fwd)
            for a in range(n):
                for k, (px, py) in enumerate(peers):
                    passed = outs[a].at[2 * px + py, half(a, 1 - mine)]
                    pltpu.make_async_remote_copy(
                        src_ref=passed, dst_ref=passed, send_sem=d2d_send.at[3 * a + k], recv_sem=d2d_recv.at[3 * a + k],
                        device_id=sibling, device_id_type=MESH).wait_recv()
            for cp in sent:
                cp.wait_send()

        for core in (0, 1):
            pl.when(c == core)(functools.partial(exchange, core))

    sems = [pltpu.SemaphoreType.DMA((3 * n,))] * 4
    if sequencer_id is None:
        hbm = pl.BlockSpec(memory_space=pltpu.HBM)
        gathered = pl.pallas_call(body, name=name, out_shape=out_shapes, in_specs=[hbm] * n, out_specs=[hbm] * n,
                                  scratch_shapes=sems)(*arrs)
    else:
        gathered = pl.kernel(body, out_type=out_shapes, mesh=plsc.ScalarSubcoreMesh(axis_name="sequencer", num_cores=1),
                             name=name, scratch_types=sems,
                             compiler_params=pltpu.CompilerParams(collective_id=sequencer_id))(*arrs)
    if after is not None:
        gathered, after = lax.optimization_barrier((gathered, after))
    chip = 2 * lax.axis_index("x") + lax.axis_index("y")
    filled = [lax.dynamic_update_slice(g, a[None], (chip, 0, 0)) for g, a in zip(gathered, arrs)]
    return filled if after is None else (filled, after)


def _chip_peers():
    x, y, c = lax.axis_index("x"), lax.axis_index("y"), lax.axis_index("c")
    return [(1 - x, y, c), (x, 1 - y, c), (1 - x, 1 - y, c)], (x, y, 1 - c)


def _handshake(peers):
    barrier = pltpu.get_barrier_semaphore()
    for peer in peers:
        pl.semaphore_signal(barrier, inc=1, device_id=peer, device_id_type=MESH)
    pl.semaphore_wait(barrier, len(peers))


def _sequencer_chip_exchange(arrs, name, collective_id):
    n = len(arrs)
    out_types = [jax.ShapeDtypeStruct(a.shape, a.dtype) for a in arrs]

    def body(*refs):
        ins, outs = refs[:n], refs[n:2 * n]
        send_sems, recv_sems = refs[2 * n:]
        chips, _ = _chip_peers()
        _handshake(chips)
        me = 2 * lax.axis_index("x") + lax.axis_index("y")
        sent = []
        for a in range(n):
            for k, peer in enumerate(chips):
                cp = pltpu.make_async_remote_copy(
                    src_ref=ins[a].at[2 * peer[0] + peer[1]], dst_ref=outs[a].at[me],
                    send_sem=send_sems.at[3 * a + k], recv_sem=recv_sems.at[3 * a + k], device_id=peer, device_id_type=MESH)
                cp.start()
                sent.append(cp)
        for a in range(n):
            for k, peer in enumerate(chips):
                landed = outs[a].at[2 * peer[0] + peer[1]]
                pltpu.make_async_remote_copy(
                    src_ref=landed, dst_ref=landed, send_sem=send_sems.at[3 * a + k], recv_sem=recv_sems.at[3 * a + k],
                    device_id=peer, device_id_type=MESH).wait_recv()
        for cp in sent:
            cp.wait_send()

    received = pl.kernel(
        body, out_type=out_types, mesh=plsc.ScalarSubcoreMesh(axis_name="sequencer", num_cores=1), name=name,
        scratch_types=[pltpu.SemaphoreType.DMA((3 * n,))] * 2,
        compiler_params=pltpu.CompilerParams(collective_id=collective_id),
    )(*arrs)
    chip = 2 * lax.axis_index("x") + lax.axis_index("y")
    return [lax.dynamic_update_slice(r, lax.dynamic_slice(a, (chip, 0, 0), (1,) + a.shape[1:]), (chip, 0, 0))
            for r, a in zip(received, arrs)]


def _sibling_exchange(arrs, name, sequencer_id=None):
    n = len(arrs)

    def body(*refs):
        ins, outs = refs[:n], refs[n:2 * n]
        send_sems, recv_sems = refs[2 * n:]
        sibling = (lax.axis_index("x"), lax.axis_index("y"), 1 - lax.axis_index("c"))
        if sequencer_id is not None:
            _handshake([sibling])
        cps = []
        for a in range(n):
            cp = pltpu.make_async_remote_copy(src_ref=ins[a], dst_ref=outs[a], send_sem=send_sems.at[a],
                                              recv_sem=recv_sems.at[a], device_id=sibling, device_id_type=MESH)
            cp.start()
            cps.append(cp)
        for cp in cps:
            cp.wait_recv()
        for cp in cps:
            cp.wait_send()

    out_shapes = [jax.ShapeDtypeStruct(a.shape, a.dtype) for a in arrs]
    sems = [pltpu.SemaphoreType.DMA((n,)), pltpu.SemaphoreType.DMA((n,))]
    if sequencer_id is not None:
        return pl.kernel(body, out_type=out_shapes, mesh=plsc.ScalarSubcoreMesh(axis_name="sequencer", num_cores=1), name=name,
                         scratch_types=sems, compiler_params=pltpu.CompilerParams(collective_id=sequencer_id))(*arrs)
    hbm = pl.BlockSpec(memory_space=pltpu.HBM)
    return pl.pallas_call(body, name=name, out_shape=out_shapes, in_specs=[hbm] * n, out_specs=[hbm] * n,
                          scratch_shapes=sems)(*arrs)


def _matmul(a, b, mode, out_dtype, name, tm=1024, tn=1024, tk=2048, relu2=False, times=None, b_split=False,
            out_split=False):
    b_shape = (b.shape[1], b.shape[2] * N_CHIP) if b_split else b.shape
    if mode == "nn":
        (m, k), (k2, n) = a.shape, b_shape
    elif mode == "nt":
        (m, k), (n, k2) = a.shape, b_shape
    else:
        (k, m), (k2, n) = a.shape, b_shape
    assert k == k2, (a.shape, b.shape, mode)
    n_cut = n // N_CHIP if (out_split or (b_split and mode != "nt")) else n
    k_cut = k // N_CHIP if (b_split and mode == "nt") else k
    tm, tn, tk = _tile(m, tm, LANES), _tile(n_cut, tn, LANES), _tile(k_cut, tk, LANES)
    assert n_cut % tn == 0 and k_cut % tk == 0 and m % tm == 0, (name, m, n, k, tm, tn, tk)
    nk = k // tk
    nbc, nkc = n_cut // tn, k_cut // tk
    n_in = 2 if times is None else 3
    n_out = 2 if relu2 else 1

    def product(a_ref, b_ref):
        if mode == "nn":
            return jnp.dot(a_ref[...], b_ref[...], preferred_element_type=F32)
        return lax.dot_general(a_ref[...], b_ref[...], NT_DIMS if mode == "nt" else TN_DIMS, preferred_element_type=F32)

    def finish(p, refs, o_refs):
        if relu2:
            p = jnp.maximum(p, 0.0)
            o_refs[0][...] = p.astype(o_refs[0].dtype)
            o_refs[1][...] = (p * p).astype(o_refs[1].dtype)
        elif times is not None:
            o_refs[0][...] = (2.0 * refs[2][...].astype(F32) * p).astype(o_refs[0].dtype)
        else:
            o_refs[0][...] = p.astype(o_refs[0].dtype)

    def body(*refs):
        o_refs = refs[n_in:n_in + n_out]
        if nk == 1:
            finish(product(refs[0], refs[1]), refs, o_refs)
            return
        acc_ref = refs[n_in + n_out]
        kk = pl.program_id(2)

        @pl.when(kk == 0)
        def _():
            acc_ref[...] = product(refs[0], refs[1])

        @pl.when((kk > 0) & (kk < nk - 1))
        def _():
            acc_ref[...] += product(refs[0], refs[1])

        @pl.when(kk == nk - 1)
        def _():
            finish(acc_ref[...] + product(refs[0], refs[1]), refs, o_refs)

    if mode == "tn":
        a_spec = pl.BlockSpec((tk, tm), lambda i, j, kk: (kk, i))
    else:
        a_spec = pl.BlockSpec((tm, tk), lambda i, j, kk: (i, kk))
    if mode == "nt":
        b_spec = (pl.BlockSpec((None, tn, tk), lambda i, j, kk: (kk // nkc, j, kk % nkc)) if b_split
                  else pl.BlockSpec((tn, tk), lambda i, j, kk: (j, kk)))
    else:
        b_spec = (pl.BlockSpec((None, tk, tn), lambda i, j, kk: (j // nbc, kk, j % nbc)) if b_split
                  else pl.BlockSpec((tk, tn), lambda i, j, kk: (kk, j)))
    mn_spec = pl.BlockSpec((tm, tn), lambda i, j, kk: (i, j))
    if out_split:
        o_spec = pl.BlockSpec((None, tm, tn), lambda i, j, kk: (j // nbc, i, j % nbc))
        o_shape = jax.ShapeDtypeStruct((N_CHIP, m, n_cut), out_dtype)
    else:
        o_spec, o_shape = mn_spec, jax.ShapeDtypeStruct((m, n), out_dtype)
    out = pl.pallas_call(
        body, name=name, grid=(m // tm, n // tn, nk), in_specs=[a_spec, b_spec] + [mn_spec] * (n_in - 2),
        out_specs=[o_spec] * n_out, out_shape=[o_shape] * n_out,
        scratch_shapes=[] if nk == 1 else [pltpu.VMEM((tm, tn), F32)],
        compiler_params=_params(("parallel", "parallel", "arbitrary")),
    )(*((a, b) if times is None else (a, b, times)))
    return out if relu2 else out[0]


def _mod_part(c_all, w_s, b_s, name):
    d, na = w_s.shape
    tn = _tile(na, 512, LANES)

    def body(c_ref, w_ref, b_ref, o_ref):
        ca = _silu(c_ref[...]).astype(BF16)
        o_ref[...] = jnp.dot(ca, w_ref[...].astype(BF16), preferred_element_type=F32) + b_ref[...]

    return pl.pallas_call(
        body, name=name, grid=(na // tn,),
        in_specs=[pl.BlockSpec((N_DEV, d), lambda j: (0, 0)), pl.BlockSpec((d, tn), lambda j: (0, j)),
                  pl.BlockSpec((1, tn), lambda j: (0, j))],
        out_specs=pl.BlockSpec((N_DEV, tn), lambda j: (0, j)),
        out_shape=jax.ShapeDtypeStruct((N_DEV, na), F32), compiler_params=_params(("parallel",)),
    )(c_all, w_s, b_s)


def _row_specs(tb, d, n_full, n_vec):
    full = pl.BlockSpec((tb, d), lambda i: (i, 0))
    vec = pl.BlockSpec((1, d), lambda i: (0, 0))
    return [full] * n_full + [vec] * n_vec


def _norm_mod(x, w, sc, sh, name):
    t, d = x.shape
    tb = _tile(t, 256, SUBLANES)

    def body(x_ref, w_ref, sc_ref, sh_ref, o_ref):
        xv = x_ref[...]
        o_ref[...] = (xv * _rms_scale(xv) * w_ref[...] * (1.0 + sc_ref[...]) + sh_ref[...]).astype(o_ref.dtype)

    return pl.pallas_call(
        body, name=name, grid=(t // tb,), in_specs=_row_specs(tb, d, 1, 3),
        out_specs=pl.BlockSpec((tb, d), lambda i: (i, 0)), out_shape=jax.ShapeDtypeStruct((t, d), BF16),
        compiler_params=_params(("parallel",)),
    )(x, w, sc, sh)


def _norm_mod_bwd(x, w, sc, dh, dres, name):
    t, d = x.shape
    tb = _tile(t, 256, SUBLANES)

    def body(x_ref, w_ref, sc_ref, dh_ref, dres_ref, dx_ref, dw_ref, dsc_ref, dsh_ref):
        @pl.when(pl.program_id(0) == 0)
        def _():
            dw_ref[...] = jnp.zeros_like(dw_ref)
            dsc_ref[...] = jnp.zeros_like(dsc_ref)
            dsh_ref[...] = jnp.zeros_like(dsh_ref)

        xv = x_ref[...]
        r = _rms_scale(xv)
        xn = xv * r
        g = dh_ref[...].astype(F32)
        wv, one_sc = w_ref[...], 1.0 + sc_ref[...]
        gxn = g * xn
        dsh_ref[...] += jnp.sum(g, axis=0, keepdims=True)
        dsc_ref[...] += jnp.sum(gxn, axis=0, keepdims=True) * wv
        dw_ref[...] += jnp.sum(gxn, axis=0, keepdims=True) * one_sc
        dxn = g * (wv * one_sc)
        dx_ref[...] = dres_ref[...] + r * (dxn - xn * jnp.mean(dxn * xn, axis=-1, keepdims=True))

    vec_out = pl.BlockSpec((1, d), lambda i: (0, 0))
    return pl.pallas_call(
        body, name=name, grid=(t // tb,),
        in_specs=[pl.BlockSpec((tb, d), lambda i: (i, 0)), pl.BlockSpec((1, d), lambda i: (0, 0)),
                  pl.BlockSpec((1, d), lambda i: (0, 0)), pl.BlockSpec((tb, d), lambda i: (i, 0)),
                  pl.BlockSpec((tb, d), lambda i: (i, 0))],
        out_specs=[pl.BlockSpec((tb, d), lambda i: (i, 0)), vec_out, vec_out, vec_out],
        out_shape=[jax.ShapeDtypeStruct((t, d), F32)] + [jax.ShapeDtypeStruct((1, d), F32)] * 3,
        compiler_params=_params(("arbitrary",)),
    )(x, w, sc, dh, dres)


def _resid_norm_mod(x, y, w, gt, w2, sc, sh, name):
    t, d = x.shape
    tb = _tile(t, 256, SUBLANES)

    def body(x_ref, y_ref, w_ref, gt_ref, w2_ref, sc_ref, sh_ref, o_ref, h_ref):
        yv = y_ref[...]
        x2 = x_ref[...] + gt_ref[...] * (yv * _rms_scale(yv) * w_ref[...])
        o_ref[...] = x2
        h_ref[...] = (x2 * _rms_scale(x2) * w2_ref[...] * (1.0 + sc_ref[...]) + sh_ref[...]).astype(h_ref.dtype)

    full = pl.BlockSpec((tb, d), lambda i: (i, 0))
    return pl.pallas_call(
        body, name=name, grid=(t // tb,), in_specs=_row_specs(tb, d, 2, 5), out_specs=[full, full],
        out_shape=[jax.ShapeDtypeStruct((t, d), F32), jax.ShapeDtypeStruct((t, d), BF16)],
        compiler_params=_params(("parallel",)),
    )(x, y, w, gt, w2, sc, sh)


def _loss_head(x2, y2, w, gt, target, name):
    t, d = x2.shape
    tb = _tile(t, 256, SUBLANES)

    def body(x_ref, y_ref, tg_ref, w_ref, gt_ref, do_ref, loss_ref, dy_ref, dgt_ref, dw_ref):
        @pl.when(pl.program_id(0) == 0)
        def _():
            loss_ref[...] = jnp.zeros_like(loss_ref)
            dgt_ref[...] = jnp.zeros_like(dgt_ref)
            dw_ref[...] = jnp.zeros_like(dw_ref)

        yv = y_ref[...]
        r = _rms_scale(yv)
        yn = yv * r
        wv, gtv = w_ref[...], gt_ref[...]
        err = x_ref[...] + gtv * (yn * wv) - tg_ref[...]
        g = err * (1.0 / d)
        do_ref[...] = g
        per_tok = jnp.mean(err * err, axis=-1, keepdims=True)
        loss_ref[...] += 0.5 * jnp.sum(per_tok, axis=0, keepdims=True)
        gyn = jnp.sum(g * yn, axis=0, keepdims=True)
        dgt_ref[...] += gyn * wv
        dw_ref[...] += gyn * gtv
        dyn = g * (gtv * wv)
        dy_ref[...] = (r * (dyn - yn * jnp.mean(dyn * yn, axis=-1, keepdims=True))).astype(dy_ref.dtype)

    full = pl.BlockSpec((tb, d), lambda i: (i, 0))
    vec_out = pl.BlockSpec((1, d), lambda i: (0, 0))
    return pl.pallas_call(
        body, name=name, grid=(t // tb,), in_specs=_row_specs(tb, d, 3, 2),
        out_specs=[full, pl.BlockSpec((1, LANES), lambda i: (0, 0)), full, vec_out, vec_out],
        out_shape=[jax.ShapeDtypeStruct((t, d), F32), jax.ShapeDtypeStruct((1, LANES), F32), jax.ShapeDtypeStruct((t, d), BF16),
                   jax.ShapeDtypeStruct((1, d), F32), jax.ShapeDtypeStruct((1, d), F32)],
        compiler_params=_params(("arbitrary",)),
    )(x2, y2, target, w, gt)


def _resid_bwd(dout, y, w, gt, name):
    t, d = y.shape
    tb = _tile(t, 256, SUBLANES)

    def body(do_ref, y_ref, w_ref, gt_ref, dy_ref, dgt_ref, dw_ref):
        @pl.when(pl.program_id(0) == 0)
        def _():
            dgt_ref[...] = jnp.zeros_like(dgt_ref)
            dw_ref[...] = jnp.zeros_like(dw_ref)

        yv, g = y_ref[...], do_ref[...]
        r = _rms_scale(yv)
        yn = yv * r
        wv, gtv = w_ref[...], gt_ref[...]
        gyn = jnp.sum(g * yn, axis=0, keepdims=True)
        dgt_ref[...] += gyn * wv
        dw_ref[...] += gyn * gtv
        dyn = g * (gtv * wv)
        dy_ref[...] = (r * (dyn - yn * jnp.mean(dyn * yn, axis=-1, keepdims=True))).astype(dy_ref.dtype)

    vec_out = pl.BlockSpec((1, d), lambda i: (0, 0))
    return pl.pallas_call(
        body, name=name, grid=(t // tb,), in_specs=_row_specs(tb, d, 2, 2),
        out_specs=[pl.BlockSpec((tb, d), lambda i: (i, 0)), vec_out, vec_out],
        out_shape=[jax.ShapeDtypeStruct((t, d), BF16)] + [jax.ShapeDtypeStruct((1, d), F32)] * 2,
        compiler_params=_params(("arbitrary",)),
    )(dout, y, w, gt)


def _row_half_to_bf16(full, which, sib, name):
    n, r, c = full.shape
    by_rows = r % 32 == 0
    r, c = (r // 2, c) if by_rows else (r, c // 2)
    tr, tc = _elementwise_tiles(r, c)
    nbh = (r // tr) if by_rows else (c // tc)

    def body(which_ref, a_ref, *rest):
        if sib is None:
            rest[0][...] = a_ref[...].astype(BF16)
        else:
            rest[1][...] = (a_ref[...] + rest[0][...].astype(F32)).astype(BF16)

    if by_rows:
        half_spec = pl.BlockSpec((1, tr, tc), lambda j, i, k, which_ref: (j, which_ref[0] * nbh + i, k))
    else:
        half_spec = pl.BlockSpec((1, tr, tc), lambda j, i, k, which_ref: (j, i, which_ref[0] * nbh + k))
    spec = pl.BlockSpec((1, tr, tc), lambda j, i, k, which_ref: (j, i, k))
    grid_spec = pltpu.PrefetchScalarGridSpec(
        num_scalar_prefetch=1, grid=(n, r // tr, c // tc), in_specs=[half_spec] + ([] if sib is None else [spec]), out_specs=spec)
    return pl.pallas_call(
        body, name=name, grid_spec=grid_spec, out_shape=jax.ShapeDtypeStruct((n, r, c), BF16),
        compiler_params=_params(("parallel", "parallel", "parallel")),
    )(which, full, *([] if sib is None else [sib]))


def _sum_chips(recv, name):
    _, r, c = recv.shape
    tr, tc = _elementwise_tiles(r, c)

    def body(x_ref, o_ref):
        acc = x_ref[0].astype(F32)
        for j in range(1, N_CHIP):
            acc = acc + x_ref[j].astype(F32)
        o_ref[...] = acc

    return pl.pallas_call(
        body, name=name, grid=(r // tr, c // tc), in_specs=[pl.BlockSpec((N_CHIP, tr, tc), lambda i, j: (0, i, j))],
        out_specs=pl.BlockSpec((tr, tc), lambda i, j: (i, j)), out_shape=jax.ShapeDtypeStruct((r, c), F32),
        compiler_params=_params(("parallel", "parallel")),
    )(recv)


def _adamw(w, g_parts, m, v, name, by_core=False, outer=False):
    r, c = w.shape
    by_rows = r % 32 == 0
    if by_core:
        tr, tc = _elementwise_tiles(*((r // 2, c) if by_rows else (r, c // 2)))
        nbh = (r // 2) // tr if by_rows else (c // 2) // tc
    else:
        tr, tc = _elementwise_tiles(r, c)
    n_g = len(g_parts)
    c1 = 1.0 / (1.0 - ADAM_B1 ** ADAM_STEP)
    c2 = 1.0 / (1.0 - ADAM_B2 ** ADAM_STEP)

    def body(*refs):
        w_ref, g_refs, m_ref, v_ref = refs[0], refs[1:1 + n_g], refs[1 + n_g], refs[2 + n_g]
        g_out, d_out, m_out, v_out = refs[3 + n_g:]
        if by_core:
            in_my_half = (pl.program_id(0 if by_rows else 1) // nbh) == lax.axis_index("c")
            g = jnp.where(in_my_half, g_refs[0][...], g_refs[1][...])
        elif outer:
            g = lax.dot_general(_silu(g_refs[0][...]), g_refs[1][...], TN_DIMS, precision=HI, preferred_element_type=F32)
        else:
            g = g_refs[0][...]
            for extra in g_refs[1:]:
                g = g + extra[...]
        mn = ADAM_B1 * m_ref[...] + (1.0 - ADAM_B1) * g
        vn = ADAM_B2 * v_ref[...] + (1.0 - ADAM_B2) * (g * g)
        g_out[...] = g
        m_out[...] = mn
        v_out[...] = vn
        d_out[...] = -ADAM_LR * ((mn * c1) / (jnp.sqrt(vn * c2) + ADAM_EPS) + ADAM_WD * w_ref[...])

    spec = pl.BlockSpec((tr, tc), lambda i, j: (i, j))
    if by_core:
        g_spec = pl.BlockSpec((tr, tc), (lambda i, j: (i % nbh, j)) if by_rows else (lambda i, j: (i, j % nbh)))
    else:
        g_spec = spec
    g_specs = [g_spec] * n_g
    if outer:
        g_specs = [pl.BlockSpec((N_DEV, tr), lambda i, j: (0, i)), pl.BlockSpec((N_DEV, tc), lambda i, j: (0, j))]
    return pl.pallas_call(
        body, name=name, grid=(r // tr, c // tc), in_specs=[spec] + g_specs + [spec] * 2, out_specs=[spec] * 4,
        out_shape=[jax.ShapeDtypeStruct((r, c), F32)] * 4, compiler_params=_params(("parallel", "parallel")),
    )(w, *g_parts, m, v)


def _conv_taps(u, t):
    rows = lax.broadcasted_iota(jnp.int32, u.shape, 0)
    return [u] + [jnp.where(rows >= dd, pltpu.roll(u, dd, 0), 0.0) for dd in range(1, CONV_K)]


def _conv_fwd(proj, conv_w, col0, name):
    t = proj.shape[0]
    ch = conv_w.shape[1]

    def body(u_ref, w_ref, o_ref):
        taps = _conv_taps(u_ref[...], t)
        wv = w_ref[...]
        y = taps[0] * wv[CONV_K - 1:CONV_K]
        for dd in range(1, CONV_K):
            y = y + taps[dd] * wv[CONV_K - 1 - dd:CONV_K - dd]
        o_ref[...] = _silu(y)

    return pl.pallas_call(
        body, name=name, grid=(ch // LANES,),
        in_specs=[pl.BlockSpec((t, LANES), lambda j: (0, col0 + j)), pl.BlockSpec((CONV_K, LANES), lambda j: (0, j))],
        out_specs=pl.BlockSpec((t, LANES), lambda j: (0, j)), out_shape=jax.ShapeDtypeStruct((t, ch), F32),
        compiler_params=_params(("parallel",)),
    )(proj, conv_w)


def _conv_bwd(proj, conv_w, ds, col0, name):
    t = proj.shape[0]
    ch = conv_w.shape[1]

    def body(u_ref, w_ref, ds_ref, du_ref, dw_ref):
        u = u_ref[...]
        taps = _conv_taps(u, t)
        wv = w_ref[...]
        y = taps[0] * wv[CONV_K - 1:CONV_K]
        for dd in range(1, CONV_K):
            y = y + taps[dd] * wv[CONV_K - 1 - dd:CONV_K - dd]
        sg = jax.nn.sigmoid(y)
        dy = ds_ref[...] * (sg * (1.0 + y * (1.0 - sg)))
        rows = lax.broadcasted_iota(jnp.int32, u.shape, 0)
        du = dy * wv[CONV_K - 1:CONV_K]
        for dd in range(1, CONV_K):
            ahead = jnp.where(rows < t - dd, pltpu.roll(dy, t - dd, 0), 0.0)
            du = du + ahead * wv[CONV_K - 1 - dd:CONV_K - dd]
        du_ref[...] = du.astype(du_ref.dtype)
        dws = [jnp.sum(dy * taps[CONV_K - 1 - j], axis=0, keepdims=True) for j in range(CONV_K)]
        dw_ref[...] = jnp.concatenate(dws, axis=0)

    return pl.pallas_call(
        body, name=name, grid=(ch // LANES,),
        in_specs=[pl.BlockSpec((t, LANES), lambda j: (0, col0 + j)), pl.BlockSpec((CONV_K, LANES), lambda j: (0, j)),
                  pl.BlockSpec((t, LANES), lambda j: (0, j))],
        out_specs=[pl.BlockSpec((t, LANES), lambda j: (0, j)), pl.BlockSpec((CONV_K, LANES), lambda j: (0, j))],
        out_shape=[jax.ShapeDtypeStruct((t, ch), BF16), jax.ShapeDtypeStruct((CONV_K, ch), F32)],
        compiler_params=_params(("parallel",)),
    )(proj, conv_w, ds)


def _hg_block(st, q, fl, vi, g, l0, l1, nw):
    hs = range(len(st))
    tb = q[0].shape[0]
    ln = HG_SUB
    lb = [jax.nn.sigmoid(l0[h] - l1[h]) for h in hs]
    rows = lax.broadcasted_iota(jnp.int32, (ln, HEAD), 0)
    tri = (lax.broadcasted_iota(jnp.int32, (ln, ln), 0) >= lax.broadcasted_iota(jnp.int32, (ln, ln), 1)).astype(F32)
    st = list(st)
    outs = [[] for _ in hs]
    for i in range(tb // ln):
        sl = slice(i * ln, (i + 1) * ln)
        qs, vs = [q[h][sl] for h in hs], [vi[h][sl] for h in hs]
        f = [lb[h] + (1.0 - lb[h]) * jax.nn.sigmoid(fl[h][sl]) for h in hs]
        k = [1.0 - f[h] for h in hs]
        b = [jnp.dot(tri, jnp.log(f[h]), precision=HI, preferred_element_type=F32) for h in hs]
        o = [lax.dot_general((qs[h] * jnp.exp(b[h])).astype(BF16), st[h].astype(BF16), NT_DIMS, preferred_element_type=F32)
             for h in hs]
        n_tiles = ln // SUBLANES
        acc = [o] + [[jnp.zeros((ln - SUBLANES * ti, HEAD), F32) for _ in hs] for ti in range(1, n_tiles)]
        for s in range(ln):
            ti = s // SUBLANES
            r0 = ti * SUBLANES
            e = [jnp.exp(jnp.where(rows[r0:] >= s, b[h][r0:] - b[h][s:s + 1], -1e30)) for h in hs]
            a = [jnp.sum(qs[h][r0:] * e[h] * k[h][s:s + 1], axis=-1, keepdims=True) for h in hs]
            acc[ti] = [acc[ti][h] + a[h] * vs[h][s:s + 1] for h in hs]
        o = [jnp.concatenate([sum(acc[ti][h][(j - ti) * SUBLANES:(j - ti + 1) * SUBLANES] for ti in range(j + 1))
                              for j in range(n_tiles)], axis=0) for h in hs]
        kt = [k[h] * jnp.exp(b[h][ln - 1:ln] - b[h]) for h in hs]
        upd = [lax.dot_general(vs[h].astype(BF16), kt[h].astype(BF16), TN_DIMS, preferred_element_type=F32) for h in hs]
        st = [st[h] * jnp.exp(b[h][ln - 1:ln]) + upd[h] for h in hs]
        for h in hs:
            outs[h].append(o[h])
    o = [jnp.concatenate(outs[h], axis=0) for h in hs]
    out = [o[h] * _rms_scale(o[h]) * nw * _silu(g[h]) for h in hs]
    return st, out


def _head_cols(h):
    return slice(h * HEAD, (h + 1) * HEAD)


def _head_groups(n_heads, group):
    g = min(group, n_heads)
    return [list(range(i, min(i + g, n_heads))) for i in range(0, n_heads, g)]


def _hg_in_specs(n_heads, tb, time_index):
    hw = n_heads * HEAD
    cols = [pl.BlockSpec((tb, hw), functools.partial(lambda part, j: (time_index(j), part), part)) for part in range(4)]
    head_rows = pl.BlockSpec((n_heads, 1, HEAD), lambda j: (0, 0, 0))
    return cols + [head_rows, head_rows, pl.BlockSpec((1, HEAD), lambda j: (0, 0))]


def _hgrn2_fwd(proj, l0, l1, nw, n_heads, name):
    t = proj.shape[0]
    hw = n_heads * HEAD
    tb = _tile(t, HG_BLOCK, HG_SUB)
    nb = t // tb

    def body(q_ref, f_ref, i_ref, g_ref, l0_ref, l1_ref, nw_ref, o_ref, save_ref, st_ref):
        @pl.when(pl.program_id(0) == 0)
        def _():
            st_ref[...] = jnp.zeros_like(st_ref)

        for hs in _head_groups(n_heads, HG_FWD_GROUP):
            st = [st_ref[h] for h in hs]
            for h, s in zip(hs, st):
                save_ref[h] = s
            st, out = _hg_block(st, *[[r[:, _head_cols(h)] for h in hs] for r in (q_ref, f_ref, i_ref, g_ref)],
                                [l0_ref[h] for h in hs], [l1_ref[h] for h in hs], nw_ref[...])
            for h, s, o in zip(hs, st, out):
                st_ref[h] = s
                o_ref[:, _head_cols(h)] = o.astype(o_ref.dtype)

    return pl.pallas_call(
        body, name=name, grid=(nb,), in_specs=_hg_in_specs(n_heads, tb, lambda j: j),
        out_specs=[pl.BlockSpec((tb, hw), lambda j: (j, 0)),
                   pl.BlockSpec((None, n_heads, HEAD, HEAD), lambda j: (j, 0, 0, 0))],
        out_shape=[jax.ShapeDtypeStruct((t, hw), BF16), jax.ShapeDtypeStruct((nb, n_heads, HEAD, HEAD), F32)],
        scratch_shapes=[pltpu.VMEM((n_heads, HEAD, HEAD), F32)], compiler_params=_params(("arbitrary",)),
    )(proj, proj, proj, proj, l0, l1, nw)


def _hgrn2_bwd(proj, l0, l1, nw, saved, d_ocat, n_heads, name):
    t = proj.shape[0]
    tb = _tile(t, HG_BLOCK, HG_SUB)
    nb = t // tb
    rev = lambda j: nb - 1 - j

    hw = n_heads * HEAD

    def body(q_ref, f_ref, i_ref, g_ref, l0_ref, l1_ref, nw_ref, save_ref, do_ref,
             dp_ref, dl0_ref, dl1_ref, dnw_ref, dst_ref):
        @pl.when(pl.program_id(0) == 0)
        def _():
            dst_ref[...] = jnp.zeros_like(dst_ref)
            dl0_ref[...] = jnp.zeros_like(dl0_ref)
            dl1_ref[...] = jnp.zeros_like(dl1_ref)
            dnw_ref[...] = jnp.zeros_like(dnw_ref)

        dnw_acc = jnp.zeros((1, HEAD), F32)
        for hs in _head_groups(n_heads, HG_BWD_GROUP):
            _, vjp = jax.vjp(_hg_block, [save_ref[h] for h in hs],
                             *[[r[:, _head_cols(h)] for h in hs] for r in (q_ref, f_ref, i_ref, g_ref)],
                             [l0_ref[h] for h in hs], [l1_ref[h] for h in hs], nw_ref[...])
            dst, dq, df, di, dg, dl0, dl1, dnw = vjp(([dst_ref[h] for h in hs], [do_ref[:, _head_cols(h)] for h in hs]))
            for i, h in enumerate(hs):
                dst_ref[h] = dst[i]
                for part, val in enumerate((dq, df, di, dg)):
                    dp_ref[:, part * hw + h * HEAD:part * hw + (h + 1) * HEAD] = val[i].astype(dp_ref.dtype)
                dl0_ref[h] += dl0[i]
                dl1_ref[h] += dl1[i]
            dnw_acc = dnw_acc + dnw
        dnw_ref[...] += dnw_acc

    head_rows = pl.BlockSpec((n_heads, 1, HEAD), lambda j: (0, 0, 0))
    return pl.pallas_call(
        body, name=name, grid=(nb,),
        in_specs=_hg_in_specs(n_heads, tb, rev) + [pl.BlockSpec((None, n_heads, HEAD, HEAD), lambda j: (rev(j), 0, 0, 0)),
                                                   pl.BlockSpec((tb, hw), lambda j: (rev(j), 0))],
        out_specs=[pl.BlockSpec((tb, 4 * hw), lambda j: (rev(j), 0)), head_rows, head_rows,
                   pl.BlockSpec((1, HEAD), lambda j: (0, 0))],
        out_shape=[jax.ShapeDtypeStruct((t, 4 * hw), BF16)] + [jax.ShapeDtypeStruct((n_heads, 1, HEAD), F32)] * 2
        + [jax.ShapeDtypeStruct((1, HEAD), F32)],
        scratch_shapes=[pltpu.VMEM((n_heads, HEAD, HEAD), F32)], compiler_params=_params(("arbitrary",)),
    )(proj, proj, proj, proj, l0, l1, nw, saved, d_ocat)


NN_DIMS = (((1,), (0,)), ((), ()))


def _split_bf16(x):
    hi = x.astype(BF16)
    return hi, (x - hi.astype(F32)).astype(BF16)


def _dot3(a, b, dims=NN_DIMS):
    (ah, al), (bh, bl) = _split_bf16(a), _split_bf16(b)
    dot = functools.partial(lax.dot_general, dimension_numbers=dims, preferred_element_type=F32)
    return dot(ah, bh) + dot(ah, bl) + dot(al, bh)


@jax.custom_vjp
def _mm3(a, b):
    return _dot3(a, b)


def _mm3_fwd(a, b):
    return _dot3(a, b), (a, b)


def _mm3_bwd(res, g):
    a, b = res
    return _dot3(g, b, NT_DIMS), _dot3(a, g, TN_DIMS)


_mm3.defvjp(_mm3_fwd, _mm3_bwd)


def _ones_dot(tri, x, dims=NN_DIMS):
    xh, xl = _split_bf16(x)
    dot = functools.partial(lax.dot_general, dimension_numbers=dims, preferred_element_type=F32)
    return dot(tri.astype(BF16), xh) + dot(tri.astype(BF16), xl)


@jax.custom_vjp
def _ones_mm(tri, x):
    return _ones_dot(tri, x)


def _ones_mm_fwd(tri, x):
    return _ones_dot(tri, x), tri


def _ones_mm_bwd(tri, g):
    return jnp.zeros_like(tri), _ones_dot(tri, g, TN_DIMS)


_ones_mm.defvjp(_ones_mm_fwd, _ones_mm_bwd)


def _dot_bf16(a, b, dims=(((1,), (0,)), ((), ()))):
    return lax.dot_general(a.astype(BF16), b.astype(BF16), dims, preferred_element_type=F32)


def _inv_unit_lower_raw(ms):
    hs = range(len(ms))
    c = ms[0].shape[0]
    r = lax.broadcasted_iota(jnp.int32, (c, c), 0)
    q = lax.broadcasted_iota(jnp.int32, (c, c), 1)
    eye = (r == q).astype(F32)
    md = [jnp.where((r // GDN_INV_BLOCK) == (q // GDN_INV_BLOCK), ms[h], 0.0) for h in hs]
    p = [-md[h] for h in hs]
    t16 = [eye + p[h] for h in hs]
    for _ in range(int(math.log2(GDN_INV_BLOCK)) - 1):
        p = [_dot3(p[h], p[h]) for h in hs]
        t16 = [t16[h] + _dot3(t16[h], p[h]) for h in hs]
    p = [-_dot3(t16[h], ms[h] - md[h]) for h in hs]
    t2 = [eye + p[h] for h in hs]
    for _ in range(int(math.log2(c // GDN_INV_BLOCK)) - 1):
        p = [_dot3(p[h], p[h]) for h in hs]
        t2 = [t2[h] + _dot3(t2[h], p[h]) for h in hs]
    return [_dot3(t2[h], t16[h]) for h in hs]


@jax.custom_vjp
def _inv_unit_lower(ms):
    return _inv_unit_lower_raw(ms)


def _inv_fwd(ms):
    ts = _inv_unit_lower_raw(ms)
    return ts, ts


def _inv_bwd(ts, dts):
    hs = range(len(ts))
    inner = [_dot3(ts[h], dts[h], TN_DIMS) for h in hs]
    return ([-_dot3(inner[h], ts[h], NT_DIMS) for h in hs],)


_inv_unit_lower.defvjp(_inv_fwd, _inv_bwd)


def _gdn_block(precise, onehots, st, qc, kc, vc, g, ab, alog_row, dtb_row, nw):
    inverse, dot3, ones_dot = precise
    hs = range(len(st))
    c = qc[0].shape[0]
    lane_sum = lambda v: jnp.sum(v, axis=-1, keepdims=True)
    a = [lane_sum(ab * onehots[h][0]) for h in hs]
    bb = [lane_sum(ab * onehots[h][1]) for h in hs]
    alog = [lane_sum(alog_row * onehots[h][0]) for h in hs]
    dtb = [lane_sum(dtb_row * onehots[h][0]) for h in hs]
    la = [-jnp.exp(alog[h]) * _softplus(a[h] + dtb[h]) for h in hs]
    beta = [jax.nn.sigmoid(bb[h]) for h in hs]
    q = [qc[h] * lax.rsqrt(lane_sum(qc[h] * qc[h]) + EPS) * (HEAD ** -0.5) for h in hs]
    k = [kc[h] * lax.rsqrt(lane_sum(kc[h] * kc[h]) + EPS) for h in hs]
    r = lax.broadcasted_iota(jnp.int32, (c, c), 0)
    s = lax.broadcasted_iota(jnp.int32, (c, c), 1)
    tri = (r >= s).astype(F32)
    g_cc = [ones_dot(tri, jnp.broadcast_to(la[h], (c, c))) for h in hs]
    g_cl = [ones_dot(tri, jnp.broadcast_to(la[h], (c, HEAD))) for h in hs]
    gamma = [jnp.exp(jnp.where(r >= s, g_cc[h] - g_cc[h].T, -1e30)) for h in hs]
    kk = [_dot_bf16(k[h], k[h], NT_DIMS) for h in hs]
    m = [jnp.where(r > s, beta[h] * kk[h] * gamma[h], 0.0) for h in hs]
    tm = inverse(m)
    eg = [jnp.exp(g_cl[h]) for h in hs]
    rhs = [jnp.concatenate([vc[h] * beta[h], k[h] * (beta[h] * eg[h])], axis=1) for h in hs]
    sol = [dot3(tm[h], rhs[h]) for h in hs]
    qk = [_dot_bf16(q[h], k[h], NT_DIMS) * gamma[h] for h in hs]
    g_last = [g_cl[h][c - 1:c] for h in hs]
    k_tail = [k[h] * jnp.exp(g_last[h] - g_cl[h]) for h in hs]
    v_new = [sol[h][:, :HEAD] - _dot_bf16(sol[h][:, HEAD:], st[h], NT_DIMS) for h in hs]
    o_st = [_dot_bf16(q[h] * eg[h], st[h], NT_DIMS) for h in hs]
    o = [o_st[h] + _dot_bf16(qk[h], v_new[h]) for h in hs]
    upd = [_dot_bf16(v_new[h], k_tail[h], TN_DIMS) for h in hs]
    st = [st[h] * jnp.exp(g_last[h]) + upd[h] for h in hs]
    out = [o[h] * _rms_scale(o[h]) * nw * _silu(g[h]) for h in hs]
    return st, out


def _head_onehots(n_heads, h):
    lane = lax.broadcasted_iota(jnp.int32, (1, LANES), 1)
    return (lane == h).astype(F32), (lane == n_heads + h).astype(F32)


def _gdn_in_specs(n_heads, c, time_index):
    hw = n_heads * HEAD
    qkv = [pl.BlockSpec((c, hw), functools.partial(lambda part, j: (time_index(j), part), part)) for part in range(3)]
    row = pl.BlockSpec((1, LANES), lambda j: (0, 0))
    return qkv + [pl.BlockSpec((c, hw), lambda j: (time_index(j), 7)),
                  pl.BlockSpec((c, LANES), lambda j: (time_index(j), 8 * n_heads)), row, row, row]


def _gdn_fwd(qkv, proj, alog_row, dtb_row, nw, n_heads, name):
    t = qkv.shape[0]
    hw = n_heads * HEAD
    c = _tile(t, GDN_CHUNK, GDN_CHUNK)
    nb = t // c

    def body(q_ref, k_ref, v_ref, g_ref, ab_ref, al_ref, dt_ref, nw_ref, o_ref, save_ref, st_ref):
        @pl.when(pl.program_id(0) == 0)
        def _():
            st_ref[...] = jnp.zeros_like(st_ref)

        for hs in _head_groups(n_heads, GDN_FWD_GROUP):
            st = [st_ref[h] for h in hs]
            for h, s in zip(hs, st):
                save_ref[h] = s
            st, out = _gdn_block((_inv_unit_lower_raw, _dot3, _ones_dot), [_head_onehots(n_heads, h) for h in hs], st,
                                 *[[r[:, _head_cols(h)] for h in hs] for r in (q_ref, k_ref, v_ref, g_ref)],
                                 ab_ref[...], al_ref[...], dt_ref[...], nw_ref[...])
            for h, s, o in zip(hs, st, out):
                st_ref[h] = s
                o_ref[:, _head_cols(h)] = o.astype(o_ref.dtype)

    return pl.pallas_call(
        body, name=name, grid=(nb,), in_specs=_gdn_in_specs(n_heads, c, lambda j: j),
        out_specs=[pl.BlockSpec((c, hw), lambda j: (j, 0)),
                   pl.BlockSpec((None, n_heads, HEAD, HEAD), lambda j: (j, 0, 0, 0))],
        out_shape=[jax.ShapeDtypeStruct((t, hw), BF16), jax.ShapeDtypeStruct((nb, n_heads, HEAD, HEAD), F32)],
        scratch_shapes=[pltpu.VMEM((n_heads, HEAD, HEAD), F32)], compiler_params=_params(("arbitrary",)),
    )(qkv, qkv, qkv, proj, proj, alog_row, dtb_row, nw)


def _gdn_bwd(qkv, proj, alog_row, dtb_row, nw, saved, d_ocat, n_heads, name):
    t = qkv.shape[0]
    c = _tile(t, GDN_CHUNK, GDN_CHUNK)
    nb = t // c
    rev = lambda j: nb - 1 - j

    hw = n_heads * HEAD

    def body(q_ref, k_ref, v_ref, g_ref, ab_ref, al_ref, dt_ref, nw_ref, save_ref, do_ref,
             dqkv_ref, dg_ref, dab_ref, dal_ref, ddt_ref, dnw_ref, dst_ref):
        @pl.when(pl.program_id(0) == 0)
        def _():
            dst_ref[...] = jnp.zeros_like(dst_ref)
            dal_ref[...] = jnp.zeros_like(dal_ref)
            ddt_ref[...] = jnp.zeros_like(ddt_ref)
            dnw_ref[...] = jnp.zeros_like(dnw_ref)

        dab_acc = jnp.zeros((c, LANES), F32)
        row_acc = [jnp.zeros((1, LANES), F32)] * 3
        for hs in _head_groups(n_heads, GDN_BWD_GROUP):
            fn = functools.partial(_gdn_block, (_inv_unit_lower, _mm3, _ones_mm), [_head_onehots(n_heads, h) for h in hs])
            _, vjp = jax.vjp(fn, [save_ref[h] for h in hs],
                             *[[r[:, _head_cols(h)] for h in hs] for r in (q_ref, k_ref, v_ref, g_ref)],
                             ab_ref[...], al_ref[...], dt_ref[...], nw_ref[...])
            dst, dq, dk, dv, dg, dab, dal, ddt, dnw = vjp(([dst_ref[h] for h in hs], [do_ref[:, _head_cols(h)] for h in hs]))
            for i, h in enumerate(hs):
                dst_ref[h] = dst[i]
                for part, val in enumerate((dq, dk, dv)):
                    dqkv_ref[:, part * hw + h * HEAD:part * hw + (h + 1) * HEAD] = val[i]
                dg_ref[:, _head_cols(h)] = dg[i].astype(dg_ref.dtype)
            dab_acc = dab_acc + dab
            row_acc = [acc + val for acc, val in zip(row_acc, (dal, ddt, dnw))]
        dab_ref[...] = dab_acc
        dal_ref[...] += row_acc[0]
        ddt_ref[...] += row_acc[1]
        dnw_ref[...] += row_acc[2]

    row = pl.BlockSpec((1, LANES), lambda j: (0, 0))
    return pl.pallas_call(
        body, name=name, grid=(nb,),
        in_specs=_gdn_in_specs(n_heads, c, rev) + [pl.BlockSpec((None, n_heads, HEAD, HEAD), lambda j: (rev(j), 0, 0, 0)),
                                                   pl.BlockSpec((c, hw), lambda j: (rev(j), 1))],
        out_specs=[pl.BlockSpec((c, 3 * hw), lambda j: (rev(j), 0)), pl.BlockSpec((c, hw), lambda j: (rev(j), 0)),
                   pl.BlockSpec((c, LANES), lambda j: (rev(j), 0)), row, row, row],
        out_shape=[jax.ShapeDtypeStruct((t, 3 * hw), F32), jax.ShapeDtypeStruct((t, hw), BF16),
                   jax.ShapeDtypeStruct((t, LANES), F32)] + [jax.ShapeDtypeStruct((1, LANES), F32)] * 3,
        scratch_shapes=[pltpu.VMEM((n_heads, HEAD, HEAD), F32)], compiler_params=_params(("arbitrary",)),
    )(qkv, qkv, qkv, proj, proj, alog_row, dtb_row, nw, saved, d_ocat)


def _pad_lanes(v, n):
    v = v.reshape(1, -1)
    return jnp.pad(v, ((0, 0), (0, n - v.shape[1])))


def _pack_rows(vecs):
    flat = jnp.concatenate([v.reshape(-1) for v in vecs])
    offs, o = [], 0
    for v in vecs:
        offs.append((o, v.size))
        o += v.size
    per_row = -(-o // (SUBLANES * LANES)) * LANES
    flat = jnp.pad(flat, (0, SUBLANES * per_row - o))
    return flat.reshape(SUBLANES, per_row), offs


def _unpack(gathered, offs):
    per_dev = gathered.reshape(N_DEV, -1)
    return [per_dev[:, o:o + n] for o, n in offs]


def _sum_devices(part):
    acc = part[0]
    for i in range(1, N_DEV):
        acc = acc + part[i]
    return acc


def kernel(x, c, w_ada, b_ada, pre_mix_norm, post_mix_norm, pre_ffn_norm, post_ffn_norm, w_in, hg_lb_logits, hg_norm, gdn_conv_w, gdn_a_log, gdn_dt_bias, gdn_norm, w_out, w_ff1, w_ff2, loss_target, m_w_ada, m_b_ada, m_pre_mix_norm, m_post_mix_norm, m_pre_ffn_norm, m_post_ffn_norm, m_w_in, m_hg_lb_logits, m_hg_norm, m_gdn_conv_w, m_gdn_a_log, m_gdn_dt_bias, m_gdn_norm, m_w_out, m_w_ff1, m_w_ff2, v_w_ada, v_b_ada, v_pre_mix_norm, v_post_mix_norm, v_pre_ffn_norm, v_post_ffn_norm, v_w_in, v_hg_lb_logits, v_hg_norm, v_gdn_conv_w, v_gdn_a_log, v_gdn_dt_bias, v_gdn_norm, v_w_out, v_w_ff1, v_w_ff2):
    assert x.shape[0] == 1 and w_ada.shape[0] == 1 and hg_lb_logits.shape[0] == 2
    t, d = x.shape[1], x.shape[2]
    n_heads = (d // 2) // HEAD
    hw = n_heads * HEAD
    in_cols = 8 * hw + 2 * n_heads
    np_cols = 8 * hw + 2 * LANES
    d_ff = w_ff1.shape[2] * N_CHIP
    na = w_ada.shape[2]
    ax, ay, ac = lax.axis_index("x"), lax.axis_index("y"), lax.axis_index("c")
    chip = 2 * ax + ay
    dev = 4 * ax + 2 * ay + ac

    x2d, tgt = x[0], loss_target[0]

    pack1, offs1 = _pack_rows([c[0], gdn_conv_w[0]])
    c_all, convw_all = _unpack(_gather8(pack1, "gather_cond"), offs1)
    conv_sh = gdn_conv_w.shape[2]
    conv_w = jnp.concatenate([convw_all[2 * j].reshape(CONV_K, conv_sh) for j in range(N_CHIP)], axis=1)

    b_s = lax.dynamic_slice(b_ada, (0, chip * na), (1, na))
    mod_part = _mod_part(c_all, w_ada[0], b_s, "mod_part")
    pack2, offs2 = _pack_rows([mod_part])
    (mod_parts,) = _unpack(_gather8(pack2, "gather_mod"), offs2)
    mod_all = jnp.concatenate([mod_parts[2 * j].reshape(N_DEV, na) for j in range(N_CHIP)], axis=1)
    mod = lax.dynamic_slice(mod_all, (dev, 0), (1, N_MOD * d))
    sh_m, sc_m, gt_m, sh_f, sc_f, gt_f = [mod[:, i * d:(i + 1) * d] for i in range(N_MOD)]

    h1 = _norm_mod(x2d, pre_mix_norm, sc_m, sh_m, "norm_mod_mix")
    (g_in,), h1 = _gather_weights([jnp.transpose(w_in[0]).astype(BF16)], "gather_w_in", sequencer_id=5, after=h1)
    late, g_in = lax.optimization_barrier(([w_out[0].astype(BF16), w_ff1[0].astype(BF16), w_ff2[0].astype(BF16)], g_in))
    g_out, g_ff1, g_ff2 = _gather_weights(late, "gather_weights_late", sequencer_id=1)
    w_in_f = jnp.pad(g_in.reshape(in_cols, d), ((0, np_cols - in_cols), (0, 0)))
    w_out_f = g_out.reshape(d, d)
    w_ff2_f = g_ff2.reshape(d_ff, d)

    proj = _matmul(h1, w_in_f, "nt", F32, "mm_in", tn=768)
    l0, l1 = hg_lb_logits[0].reshape(n_heads, 1, HEAD), hg_lb_logits[1].reshape(n_heads, 1, HEAD)
    o_hg, hg_saved = _hgrn2_fwd(proj, l0, l1, hg_norm, n_heads, "hgrn2_fwd")
    qkv = _conv_fwd(proj, conv_w, 4 * n_heads, "conv_fwd")
    alog_row, dtb_row = _pad_lanes(gdn_a_log, LANES), _pad_lanes(gdn_dt_bias, LANES)
    o_gdn, gdn_saved = _gdn_fwd(qkv, proj, alog_row, dtb_row, gdn_norm, n_heads, "gdn_fwd")
    o_cat = jnp.concatenate([o_hg, o_gdn], axis=1)
    y1 = _matmul(o_cat, w_out_f, "nn", F32, "mm_out")
    x_mid, h2 = _resid_norm_mod(x2d, y1, post_mix_norm, gt_m, pre_ffn_norm, sc_f, sh_f, "resid_mix_norm_mod_ffn")

    relu_a1, r1 = _matmul(h2, g_ff1, "nn", BF16, "mm_ff1", relu2=True, b_split=True)
    y2 = _matmul(r1, w_ff2_f, "nn", F32, "mm_ff2")
    d_out, loss_row, dy2, d_gt_f, d_post_ffn = _loss_head(x_mid, y2, post_ffn_norm, gt_f, tgt, "loss_head")

    in_sh = in_cols // N_CHIP
    ff_sh = d_ff // N_CHIP
    my_half = jnp.reshape(ac, (1,)).astype(jnp.int32)

    def start_reduce(by_chip, tag, collective_id):
        to_sib = [_row_half_to_bf16(a, 1 - my_half, None, f"sibling_half_{tag}{i}") for i, a in enumerate(by_chip)]
        from_sib = _sibling_exchange(to_sib, f"sibling_partials_{tag}")
        chip_part = [_row_half_to_bf16(a, my_half, s, f"add_halves_{tag}{i}") for i, (a, s) in enumerate(zip(by_chip, from_sib))]
        return _sequencer_chip_exchange(chip_part, f"scatter_grads_{tag}", collective_id)

    gw_ff2 = _matmul(r1, dy2, "tn", BF16, "mm_ff2_dw")
    gw_ff2, dy2 = lax.optimization_barrier((gw_ff2, dy2))
    da1 = _matmul(dy2, w_ff2_f, "nt", BF16, "mm_ff2_dx", times=relu_a1)
    gw_ff1 = _matmul(h2, da1, "tn", BF16, "mm_ff1_dw", out_split=True)
    gw_ff1, da1 = lax.optimization_barrier((gw_ff1, da1))
    recv_ff2, recv_ff1 = _sequencer_chip_exchange([gw_ff2.reshape(N_CHIP, ff_sh, d), gw_ff1], "scatter_grads_ff", 2)
    dh2 = _matmul(da1, g_ff1, "nt", BF16, "mm_ff1_dx", b_split=True)
    d_mid, d_pre_ffn, d_sc_f, d_sh_f = _norm_mod_bwd(x_mid, pre_ffn_norm, sc_f, dh2, d_out, "norm_mod_ffn_bwd")

    dy1, d_gt_m, d_post_mix = _resid_bwd(d_mid, y1, post_mix_norm, gt_m, "resid_mix_bwd")
    gw_out = _matmul(o_cat, dy1, "tn", BF16, "mm_out_dw")
    gw_out, dy1 = lax.optimization_barrier((gw_out, dy1))
    (recv_out,) = _sequencer_chip_exchange([gw_out.reshape(N_CHIP, d // N_CHIP, d)], "scatter_grads_out", 3)
    d_ocat = _matmul(dy1, w_out_f, "nt", F32, "mm_out_dx")
    dp_hg, dl0, dl1, d_hg_norm = _hgrn2_bwd(proj, l0, l1, hg_norm, hg_saved, d_ocat, n_heads, "hgrn2_bwd")
    dqkv, dg_g, dab, d_alog, d_dtb, d_gdn_norm = _gdn_bwd(
        qkv, proj, alog_row, dtb_row, gdn_norm, gdn_saved, d_ocat, n_heads, "gdn_bwd")
    du, d_conv_w = _conv_bwd(proj, conv_w, dqkv, 4 * n_heads, "conv_bwd")
    dproj = jnp.concatenate([dp_hg, du, dg_g, dab.astype(BF16), jnp.zeros((t, LANES), BF16)], axis=1)
    gw_in = _matmul(dproj, h1, "tn", F32, "mm_in_dw", tm=768)
    (recv_in,) = start_reduce([gw_in[:in_cols].reshape(N_CHIP, in_sh, d)], "in", 4)
    dh1 = _matmul(dproj, w_in_f, "nn", BF16, "mm_in_dx", tk=2816)
    grad_x, d_pre_mix, d_sc_m, d_sh_m = _norm_mod_bwd(x2d, pre_mix_norm, sc_m, dh1, d_mid, "norm_mod_mix_bwd")

    d_mod = jnp.concatenate([d_sh_m, d_sc_m, d_gt_m, d_sh_f, d_sc_f, d_gt_f], axis=1)
    d_lb_logits = jnp.stack([dl0.reshape(n_heads, HEAD), dl1.reshape(n_heads, HEAD)])
    pack3, offs3 = _pack_rows([loss_row[0, :1], d_pre_mix, d_post_mix, d_pre_ffn, d_post_ffn, d_lb_logits, d_hg_norm,
                               d_conv_w, d_alog[0, :n_heads], d_dtb[0, :n_heads], d_gdn_norm, d_mod])
    parts = _unpack(_gather8(pack3, "gather_vec_grads"), offs3)
    sums = [_sum_devices(p) for p in parts[:-1]]
    loss = sums[0][0]
    dmod_all = parts[-1]
    g_b_ada = _sum_devices(dmod_all).reshape(1, N_MOD * d)
    g_conv_full = sums[7].reshape(CONV_K, N_CHIP * conv_sh)
    g_conv = lax.dynamic_slice(g_conv_full, (0, chip * conv_sh), (CONV_K, conv_sh))
    dmod_chip = lax.dynamic_slice(dmod_all, (0, chip * na), (N_DEV, na))

    sum_ff2 = _sum_chips(recv_ff2, "sum_chips_ff2")
    recv_ff1, sum_ff2 = lax.optimization_barrier((recv_ff1, sum_ff2))
    sum_ff1 = _sum_chips(recv_ff1, "sum_chips_ff1")
    recv_out, sum_ff1 = lax.optimization_barrier((recv_out, sum_ff1))
    sum_out = _sum_chips(recv_out, "sum_chips_out")
    recv_in, sum_out, grad_x = lax.optimization_barrier((recv_in, sum_out, grad_x))
    mine = [_sum_chips(recv_in, "sum_chips_in"), sum_out, sum_ff1, sum_ff2]
    theirs = _sibling_exchange(mine, "sibling_grads", sequencer_id=6)
    ada = _adamw(w_ada[0], [c_all, dmod_chip], m_w_ada[0], v_w_ada[0], "adamw_w_ada", outer=True)
    theirs, ada = lax.optimization_barrier((theirs, ada))

    big = {"w_ada": [o[None] for o in ada]}
    for i, (nm, w_, m_, v_) in enumerate([("w_in", w_in, m_w_in, v_w_in), ("w_out", w_out, m_w_out, v_w_out),
                                          ("w_ff1", w_ff1, m_w_ff1, v_w_ff1), ("w_ff2", w_ff2, m_w_ff2, v_w_ff2)]):
        if nm == "w_in":
            res_t = _adamw(jnp.transpose(w_[0]), [mine[i], theirs[i]], jnp.transpose(m_[0]), jnp.transpose(v_[0]),
                           f"adamw_{nm}", by_core=True)
            big[nm] = [jnp.transpose(o)[None] for o in res_t]
        else:
            big[nm] = [o[None] for o in _adamw(w_[0], [mine[i], theirs[i]], m_[0], v_[0], f"adamw_{nm}")]

    small_names = ["b_ada", "pre_mix_norm", "post_mix_norm", "pre_ffn_norm", "post_ffn_norm", "hg_lb_logits", "hg_norm",
                   "gdn_conv_w", "gdn_a_log", "gdn_dt_bias", "gdn_norm"]
    small_w = [b_ada, pre_mix_norm, post_mix_norm, pre_ffn_norm, post_ffn_norm, hg_lb_logits, hg_norm, gdn_conv_w,
               gdn_a_log, gdn_dt_bias, gdn_norm]
    small_m = [m_b_ada, m_pre_mix_norm, m_post_mix_norm, m_pre_ffn_norm, m_post_ffn_norm, m_hg_lb_logits, m_hg_norm,
               m_gdn_conv_w, m_gdn_a_log, m_gdn_dt_bias, m_gdn_norm]
    small_v = [v_b_ada, v_pre_mix_norm, v_post_mix_norm, v_pre_ffn_norm, v_post_ffn_norm, v_hg_lb_logits, v_hg_norm,
               v_gdn_conv_w, v_gdn_a_log, v_gdn_dt_bias, v_gdn_norm]
    small_g = [g_b_ada, sums[1], sums[2], sums[3], sums[4], sums[5], sums[6], g_conv, sums[8], sums[9], sums[10]]
    pw, offs_s = _pack_rows(small_w)
    pg, _ = _pack_rows(small_g)
    pm, _ = _pack_rows(small_m)
    pv, _ = _pack_rows(small_v)
    packed = _adamw(pw, [pg], pm, pv, "adamw_vectors")
    small = {}
    for nm, w_, (o, n) in zip(small_names, small_w, offs_s):
        small[nm] = [p.reshape(-1)[o:o + n].reshape(w_.shape) for p in packed]

    order = ["w_ada", "b_ada", "pre_mix_norm", "post_mix_norm", "pre_ffn_norm", "post_ffn_norm", "w_in", "hg_lb_logits",
             "hg_norm", "gdn_conv_w", "gdn_a_log", "gdn_dt_bias", "gdn_norm", "w_out", "w_ff1", "w_ff2"]
    res = {**big, **small}
    outs = [loss, grad_x[None]]
    for k in range(4):
        outs += [res[nm][k] for nm in order]
    return tuple(outs)
```

```python
import functools
import math

import jax
import jax.numpy as jnp
from jax import lax
from jax.experimental import pallas as pl
from jax.experimental.pallas import tpu as pltpu
from jax.experimental.pallas import tpu_sc as plsc

F32 = jnp.float32
BF16 = jnp.bfloat16
HI = lax.Precision.HIGHEST
MESH = pl.DeviceIdType.MESH

LANES = 128
SUBLANES = 8
VMEM_LIMIT = 48 * 1024 * 1024
EPS = 1e-6
HEAD = 128
CONV_K = 4
GDN_CHUNK = 64
GDN_INV_BLOCK = 16
HG_SUB = 16
HG_BLOCK = 128
HG_FWD_GROUP = 8
HG_BWD_GROUP = 4
GDN_FWD_GROUP = 8
GDN_BWD_GROUP = 8
N_MOD = 6
N_DEV = 8
N_CHIP = 4

ADAM_LR = 0.001
ADAM_B1 = 0.9
ADAM_B2 = 0.999
ADAM_EPS = 1e-08
ADAM_WD = 0.01
ADAM_STEP = 10

NT_DIMS = (((1,), (1,)), ((), ()))
TN_DIMS = (((0,), (0,)), ((), ()))


def _tile(dim, target, align):
    if dim <= target:
        return dim
    best = dim
    t = align
    while t <= target:
        if dim % t == 0:
            best = t
        t += align
    return best


def _elementwise_tiles(r, c):
    tc = _tile(c, 1024, LANES)
    tr = _tile(r, max(16, (256 * 1024) // tc // 16 * 16), 16)
    if tr == r and r * tc > 512 * 1024:
        tc = _tile(c, max(LANES, (256 * 1024) // r // LANES * LANES), LANES)
    return tr, tc


def _params(sem):
    return pltpu.CompilerParams(dimension_semantics=sem, vmem_limit_bytes=VMEM_LIMIT)


def _silu(x):
    return x * jax.nn.sigmoid(x)


def _softplus(x):
    pos = x > 0
    return jnp.where(pos, x, 0.0) + jnp.log(1.0 + jnp.exp(jnp.where(pos, -x, x)))


def _rms_scale(x):
    return lax.rsqrt(jnp.mean(x * x, axis=-1, keepdims=True) + EPS)


def _gather8(x_shard, name):
    m_per, n = x_shard.shape
    assert m_per % SUBLANES == 0 and n % LANES == 0

    def body(x_ref, out_ref, send_sems, recv_sems, local_sem):
        x, y, c = lax.axis_index("x"), lax.axis_index("y"), lax.axis_index("c")
        me, sibling = (x, y, c), (x, y, 1 - c)
        chips = [(1 - x, y), (x, 1 - y), (1 - x, 1 - y)]

        def rows(px, py, pc):
            return out_ref.at[pl.ds((4 * px + 2 * py + pc) * m_per, m_per), :]

        def copy(k, block, to, src=None):
            return pltpu.make_async_remote_copy(
                src_ref=rows(*block) if src is None else src, dst_ref=rows(*block),
                send_sem=send_sems.at[k], recv_sem=recv_sems.at[k], device_id=to, device_id_type=MESH)

        mine = pltpu.make_async_copy(x_ref, rows(*me), local_sem)
        mine.start()
        first = [copy(0, me, sibling, src=x_ref)]
        first += [copy(1 + j, me, (*chip, c), src=x_ref) for j, chip in enumerate(chips)]
        for cp in first:
            cp.start()
        passed = [copy(4 + j, (*chip, c), sibling) for j, chip in enumerate(chips)]
        for j, chip in enumerate(chips):
            copy(1 + j, (*chip, c), me).wait_recv()
            passed[j].start()
        copy(0, sibling, me).wait_recv()
        for j, chip in enumerate(chips):
            copy(4 + j, (*chip, 1 - c), me).wait_recv()
        for cp in first + passed:
            cp.wait_send()
        mine.wait()

    return pl.pallas_call(
        body, name=name,
        out_shape=jax.ShapeDtypeStruct((N_DEV * m_per, n), x_shard.dtype),
        in_specs=[pl.BlockSpec(memory_space=pltpu.VMEM)],
        out_specs=pl.BlockSpec(memory_space=pltpu.VMEM),
        scratch_shapes=[pltpu.SemaphoreType.DMA((7,)), pltpu.SemaphoreType.DMA((7,)), pltpu.SemaphoreType.DMA],
        compiler_params=pltpu.CompilerParams(vmem_limit_bytes=VMEM_LIMIT),
    )(x_shard)


def _gather_weights(arrs, name, sequencer_id=None, after=None):
    n = len(arrs)
    out_shapes = [jax.ShapeDtypeStruct((N_CHIP,) + a.shape, a.dtype) for a in arrs]

    def body(*refs):
        ins, outs = refs[:n], refs[n:2 * n]
        ici_send, ici_recv, d2d_send, d2d_recv = refs[2 * n:]
        if sequencer_id is not None:
            chips, sib = _chip_peers()
            _handshake(chips + [sib])
        x, y, c = lax.axis_index("x"), lax.axis_index("y"), lax.axis_index("c")
        me = 2 * x + y
        peers = [(1 - x, y), (x, 1 - y), (1 - x, 1 - y)]

        def half(a, cc):
            r = arrs[a].shape[0]
            cut = r // 32 * 16
            return pl.ds(0, cut) if cc == 0 else pl.ds(cut, r - cut)

        def exchange(mine):
            sibling = (x, y, 1 - mine)
            sent = []
            for a in range(n):
                for k, (px, py) in enumerate(peers):
                    cp = pltpu.make_async_remote_copy(
                        src_ref=ins[a].at[half(a, mine)], dst_ref=outs[a].at[me, half(a, mine)],
                        send_sem=ici_send.at[3 * a + k], recv_sem=ici_recv.at[3 * a + k],
                        device_id=(px, py, mine), device_id_type=MESH)
                    cp.start()
                    sent.append(cp)
            for a in range(n):
                for k, (px, py) in enumerate(peers):
                    landed = outs[a].at[2 * px + py, half(a, mine)]
                    pltpu.make_async_remote_copy(
                        src_ref=landed, dst_ref=landed, send_sem=ici_send.at[3 * a + k], recv_sem=ici_recv.at[3 * a + k],
                        device_id=(px, py, mine), device_id_type=MESH).wait_recv()
                    fwd = pltpu.make_async_remote_copy(
                        src_ref=landed, dst_ref=landed, send_sem=d2d_send.at[3 * a + k], recv_sem=d2d_recv.at[3 * a + k],
                        device_id=sibling, device_id_type=MESH)
                    fwd.start()
                    sent.append(fwd)
            for a in range(n):
                for k, (px, py) in enumerate(peers):
                    passed = outs[a].at[2 * px + py, half(a, 1 - mine)]
                    pltpu.make_async_remote_copy(
                        src_ref=passed, dst_ref=passed, send_sem=d2d_send.at[3 * a + k], recv_sem=d2d_recv.at[3 * a + k],
                        device_id=sibling, device_id_type=MESH).wait_recv()
            for cp in sent:
                cp.wait_send()

        for core in (0, 1):
            pl.when(c == core)(functools.partial(exchange, core))

    sems = [pltpu.SemaphoreType.DMA((3 * n,))] * 4
    if sequencer_id is None:
        hbm = pl.BlockSpec(memory_space=pltpu.HBM)
        gathered = pl.pallas_call(body, name=name, out_shape=out_shapes, in_specs=[hbm] * n, out_specs=[hbm] * n,
                                  scratch_shapes=sems)(*arrs)
    else:
        gathered = pl.kernel(body, out_type=out_shapes, mesh=plsc.ScalarSubcoreMesh(axis_name="sequencer", num_cores=1),
                             name=name, scratch_types=sems,
                             compiler_params=pltpu.CompilerParams(collective_id=sequencer_id))(*arrs)
    if after is not None:
        gathered, after = lax.optimization_barrier((gathered, after))
    chip = 2 * lax.axis_index("x") + lax.axis_index("y")
    filled = [lax.dynamic_update_slice(g, a[None], (chip, 0, 0)) for g, a in zip(gathered, arrs)]
    return filled if after is None else (filled, after)


def _chip_peers():
    x, y, c = lax.axis_index("x"), lax.axis_index("y"), lax.axis_index("c")
    return [(1 - x, y, c), (x, 1 - y, c), (1 - x, 1 - y, c)], (x, y, 1 - c)


def _handshake(peers):
    barrier = pltpu.get_barrier_semaphore()
    for peer in peers:
        pl.semaphore_signal(barrier, inc=1, device_id=peer, device_id_type=MESH)
    pl.semaphore_wait(barrier, len(peers))


def _sequencer_chip_exchange(arrs, name, collective_id):
    n = len(arrs)
    out_types = [jax.ShapeDtypeStruct(a.shape, a.dtype) for a in arrs]

    def body(*refs):
        ins, outs = refs[:n], refs[n:2 * n]
        send_sems, recv_sems = refs[2 * n:]
        chips, _ = _chip_peers()
        _handshake(chips)
        me = 2 * lax.axis_index("x") + lax.axis_index("y")
        sent = []
        for a in range(n):
            for k, peer in enumerate(chips):
                cp = pltpu.make_async_remote_copy(
                    src_ref=ins[a].at[2 * peer[0] + peer[1]], dst_ref=outs[a].at[me],
                    send_sem=send_sems.at[3 * a + k], recv_sem=recv_sems.at[3 * a + k], device_id=peer, device_id_type=MESH)
                cp.start()
                sent.append(cp)
        for a in range(n):
            for k, peer in enumerate(chips):
                landed = outs[a].at[2 * peer[0] + peer[1]]
                pltpu.make_async_remote_copy(
                    src_ref=landed, dst_ref=landed, send_sem=send_sems.at[3 * a + k], recv_sem=recv_sems.at[3 * a + k],
                    device_id=peer, device_id_type=MESH).wait_recv()
        for cp in sent:
            cp.wait_send()

    received = pl.kernel(
        body, out_type=out_types, mesh=plsc.ScalarSubcoreMesh(axis_name="sequencer", num_cores=1), name=name,
        scratch_types=[pltpu.SemaphoreType.DMA((3 * n,))] * 2,
        compiler_params=pltpu.CompilerParams(collective_id=collective_id),
    )(*arrs)
    chip = 2 * lax.axis_index("x") + lax.axis_index("y")
    return [lax.dynamic_update_slice(r, lax.dynamic_slice(a, (chip, 0, 0), (1,) + a.shape[1:]), (chip, 0, 0))
            for r, a in zip(received, arrs)]


def _sibling_exchange(arrs, name, sequencer_id=None):
    n = len(arrs)

    def body(*refs):
        ins, outs = refs[:n], refs[n:2 * n]
        send_sems, recv_sems = refs[2 * n:]
        sibling = (lax.axis_index("x"), lax.axis_index("y"), 1 - lax.axis_index("c"))
        if sequencer_id is not None:
            _handshake([sibling])
        cps = []
        for a in range(n):
            cp = pltpu.make_async_remote_copy(src_ref=ins[a], dst_ref=outs[a], send_sem=send_sems.at[a],
                                              recv_sem=recv_sems.at[a], device_id=sibling, device_id_type=MESH)
            cp.start()
            cps.append(cp)
        for cp in cps:
            cp.wait_recv()
        for cp in cps:
            cp.wait_send()

    out_shapes = [jax.ShapeDtypeStruct(a.shape, a.dtype) for a in arrs]
    sems = [pltpu.SemaphoreType.DMA((n,)), pltpu.SemaphoreType.DMA((n,))]
    if sequencer_id is not None:
        return pl.kernel(body, out_type=out_shapes, mesh=plsc.ScalarSubcoreMesh(axis_name="sequencer", num_cores=1), name=name,
                         scratch_types=sems, compiler_params=pltpu.CompilerParams(collective_id=sequencer_id))(*arrs)
    hbm = pl.BlockSpec(memory_space=pltpu.HBM)
    return pl.pallas_call(body, name=name, out_shape=out_shapes, in_specs=[hbm] * n, out_specs=[hbm] * n,
                          scratch_shapes=sems)(*arrs)


def _matmul(a, b, mode, out_dtype, name, tm=1024, tn=1024, tk=2048, relu2=False, times=None, b_split=False,
            out_split=False):
    b_shape = (b.shape[1], b.shape[2] * N_CHIP) if b_split else b.shape
    if mode == "nn":
        (m, k), (k2, n) = a.shape, b_shape
    elif mode == "nt":
        (m, k), (n, k2) = a.shape, b_shape
    else:
        (k, m), (k2, n) = a.shape, b_shape
    assert k == k2, (a.shape, b.shape, mode)
    n_cut = n // N_CHIP if (out_split or (b_split and mode != "nt")) else n
    k_cut = k // N_CHIP if (b_split and mode == "nt") else k
    tm, tn, tk = _tile(m, tm, LANES), _tile(n_cut, tn, LANES), _tile(k_cut, tk, LANES)
    assert n_cut % tn == 0 and k_cut % tk == 0 and m % tm == 0, (name, m, n, k, tm, tn, tk)
    nk = k // tk
    nbc, nkc = n_cut // tn, k_cut // tk
    n_in = 2 if times is None else 3
    n_out = 2 if relu2 else 1

    def product(a_ref, b_ref):
        if mode == "nn":
            return jnp.dot(a_ref[...], b_ref[...], preferred_element_type=F32)
        return lax.dot_general(a_ref[...], b_ref[...], NT_DIMS if mode == "nt" else TN_DIMS, preferred_element_type=F32)

    def finish(p, refs, o_refs):
        if relu2:
            p = jnp.maximum(p, 0.0)
            o_refs[0][...] = p.astype(o_refs[0].dtype)
            o_refs[1][...] = (p * p).astype(o_refs[1].dtype)
        elif times is not None:
            o_refs[0][...] = (2.0 * refs[2][...].astype(F32) * p).astype(o_refs[0].dtype)
        else:
            o_refs[0][...] = p.astype(o_refs[0].dtype)

    def body(*refs):
        o_refs = refs[n_in:n_in + n_out]
        if nk == 1:
            finish(product(refs[0], refs[1]), refs, o_refs)
            return
        acc_ref = refs[n_in + n_out]
        kk = pl.program_id(2)

        @pl.when(kk == 0)
        def _():
            acc_ref[...] = product(refs[0], refs[1])

        @pl.when((kk > 0) & (kk < nk - 1))
        def _():
            acc_ref[...] += product(refs[0], refs[1])

        @pl.when(kk == nk - 1)
        def _():
            finish(acc_ref[...] + product(refs[0], refs[1]), refs, o_refs)

    if mode == "tn":
        a_spec = pl.BlockSpec((tk, tm), lambda i, j, kk: (kk, i))
    else:
        a_spec = pl.BlockSpec((tm, tk), lambda i, j, kk: (i, kk))
    if mode == "nt":
        b_spec = (pl.BlockSpec((None, tn, tk), lambda i, j, kk: (kk // nkc, j, kk % nkc)) if b_split
                  else pl.BlockSpec((tn, tk), lambda i, j, kk: (j, kk)))
    else:
        b_spec = (pl.BlockSpec((None, tk, tn), lambda i, j, kk: (j // nbc, kk, j % nbc)) if b_split
                  else pl.BlockSpec((tk, tn), lambda i, j, kk: (kk, j)))
    mn_spec = pl.BlockSpec((tm, tn), lambda i, j, kk: (i, j))
    if out_split:
        o_spec = pl.BlockSpec((None, tm, tn), lambda i, j, kk: (j // nbc, i, j % nbc))
        o_shape = jax.ShapeDtypeStruct((N_CHIP, m, n_cut), out_dtype)
    else:
        o_spec, o_shape = mn_spec, jax.ShapeDtypeStruct((m, n), out_dtype)
    out = pl.pallas_call(
        body, name=name, grid=(m // tm, n // tn, nk), in_specs=[a_spec, b_spec] + [mn_spec] * (n_in - 2),
        out_specs=[o_spec] * n_out, out_shape=[o_shape] * n_out,
        scratch_shapes=[] if nk == 1 else [pltpu.VMEM((tm, tn), F32)],
        compiler_params=_params(("parallel", "parallel", "arbitrary")),
    )(*((a, b) if times is None else (a, b, times)))
    return out if relu2 else out[0]


def _mod_part(c_all, w_s, b_s, name):
    d, na = w_s.shape
    tn = _tile(na, 512, LANES)

    def body(c_ref, w_ref, b_ref, o_ref):
        ca = _silu(c_ref[...]).astype(BF16)
        o_ref[...] = jnp.dot(ca, w_ref[...].astype(BF16), preferred_element_type=F32) + b_ref[...]

    return pl.pallas_call(
        body, name=name, grid=(na // tn,),
        in_specs=[pl.BlockSpec((N_DEV, d), lambda j: (0, 0)), pl.BlockSpec((d, tn), lambda j: (0, j)),
                  pl.BlockSpec((1, tn), lambda j: (0, j))],
        out_specs=pl.BlockSpec((N_DEV, tn), lambda j: (0, j)),
        out_shape=jax.ShapeDtypeStruct((N_DEV, na), F32), compiler_params=_params(("parallel",)),
    )(c_all, w_s, b_s)


def _row_specs(tb, d, n_full, n_vec):
    full = pl.BlockSpec((tb, d), lambda i: (i, 0))
    vec = pl.BlockSpec((1, d), lambda i: (0, 0))
    return [full] * n_full + [vec] * n_vec


def _norm_mod(x, w, sc, sh, name):
    t, d = x.shape
    tb = _tile(t, 256, SUBLANES)

    def body(x_ref, w_ref, sc_ref, sh_ref, o_ref):
        xv = x_ref[...]
        o_ref[...] = (xv * _rms_scale(xv) * w_ref[...] * (1.0 + sc_ref[...]) + sh_ref[...]).astype(o_ref.dtype)

    return pl.pallas_call(
        body, name=name, grid=(t // tb,), in_specs=_row_specs(tb, d, 1, 3),
        out_specs=pl.BlockSpec((tb, d), lambda i: (i, 0)), out_shape=jax.ShapeDtypeStruct((t, d), BF16),
        compiler_params=_params(("parallel",)),
    )(x, w, sc, sh)


def _norm_mod_bwd(x, w, sc, dh, dres, name):
    t, d = x.shape
    tb = _tile(t, 256, SUBLANES)

    def body(x_ref, w_ref, sc_ref, dh_ref, dres_ref, dx_ref, dw_ref, dsc_ref, dsh_ref):
        @pl.when(pl.program_id(0) == 0)
        def _():
            dw_ref[...] = jnp.zeros_like(dw_ref)
            dsc_ref[...] = jnp.zeros_like(dsc_ref)
            dsh_ref[...] = jnp.zeros_like(dsh_ref)

        xv = x_ref[...]
        r = _rms_scale(xv)
        xn = xv * r
        g = dh_ref[...].astype(F32)
        wv, one_sc = w_ref[...], 1.0 + sc_ref[...]
        gxn = g * xn
        dsh_ref[...] += jnp.sum(g, axis=0, keepdims=True)
        dsc_ref[...] += jnp.sum(gxn, axis=0, keepdims=True) * wv
        dw_ref[...] += jnp.sum(gxn, axis=0, keepdims=True) * one_sc
        dxn = g * (wv * one_sc)
        dx_ref[...] = dres_ref[...] + r * (dxn - xn * jnp.mean(dxn * xn, axis=-1, keepdims=True))

    vec_out = pl.BlockSpec((1, d), lambda i: (0, 0))
    return pl.pallas_call(
        body, name=name, grid=(t // tb,),
        in_specs=[pl.BlockSpec((tb, d), lambda i: (i, 0)), pl.BlockSpec((1, d), lambda i: (0, 0)),
                  pl.BlockSpec((1, d), lambda i: (0, 0)), pl.BlockSpec((tb, d), lambda i: (i, 0)),
                  pl.BlockSpec((tb, d), lambda i: (i, 0))],
        out_specs=[pl.BlockSpec((tb, d), lambda i: (i, 0)), vec_out, vec_out, vec_out],
        out_shape=[jax.ShapeDtypeStruct((t, d), F32)] + [jax.ShapeDtypeStruct((1, d), F32)] * 3,
        compiler_params=_params(("arbitrary",)),
    )(x, w, sc, dh, dres)


def _resid_norm_mod(x, y, w, gt, w2, sc, sh, name):
    t, d = x.shape
    tb = _tile(t, 256, SUBLANES)

    def body(x_ref, y_ref, w_ref, gt_ref, w2_ref, sc_ref, sh_ref, o_ref, h_ref):
        yv = y_ref[...]
        x2 = x_ref[...] + gt_ref[...] * (yv * _rms_scale(yv) * w_ref[...])
        o_ref[...] = x2
        h_ref[...] = (x2 * _rms_scale(x2) * w2_ref[...] * (1.0 + sc_ref[...]) + sh_ref[...]).astype(h_ref.dtype)

    full = pl.BlockSpec((tb, d), lambda i: (i, 0))
    return pl.pallas_call(
        body, name=name, grid=(t // tb,), in_specs=_row_specs(tb, d, 2, 5), out_specs=[full, full],
        out_shape=[jax.ShapeDtypeStruct((t, d), F32), jax.ShapeDtypeStruct((t, d), BF16)],
        compiler_params=_params(("parallel",)),
    )(x, y, w, gt, w2, sc, sh)


def _loss_head(x2, y2, w, gt, target, name):
    t, d = x2.shape
    tb = _tile(t, 256, SUBLANES)

    def body(x_ref, y_ref, tg_ref, w_ref, gt_ref, do_ref, loss_ref, dy_ref, dgt_ref, dw_ref):
        @pl.when(pl.program_id(0) == 0)
        def _():
            loss_ref[...] = jnp.zeros_like(loss_ref)
            dgt_ref[...] = jnp.zeros_like(dgt_ref)
            dw_ref[...] = jnp.zeros_like(dw_ref)

        yv = y_ref[...]
        r = _rms_scale(yv)
        yn = yv * r
        wv, gtv = w_ref[...], gt_ref[...]
        err = x_ref[...] + gtv * (yn * wv) - tg_ref[...]
        g = err * (1.0 / d)
        do_ref[...] = g
        per_tok = jnp.mean(err * err, axis=-1, keepdims=True)
        loss_ref[...] += 0.5 * jnp.sum(per_tok, axis=0, keepdims=True)
        gyn = jnp.sum(g * yn, axis=0, keepdims=True)
        dgt_ref[...] += gyn * wv
        dw_ref[...] += gyn * gtv
        dyn = g * (gtv * wv)
        dy_ref[...] = (r * (dyn - yn * jnp.mean(dyn * yn, axis=-1, keepdims=True))).astype(dy_ref.dtype)

    full = pl.BlockSpec((tb, d), lambda i: (i, 0))
    vec_out = pl.BlockSpec((1, d), lambda i: (0, 0))
    return pl.pallas_call(
        body, name=name, grid=(t // tb,), in_specs=_row_specs(tb, d, 3, 2),
        out_specs=[full, pl.BlockSpec((1, LANES), lambda i: (0, 0)), full, vec_out, vec_out],
        out_shape=[jax.ShapeDtypeStruct((t, d), F32), jax.ShapeDtypeStruct((1, LANES), F32), jax.ShapeDtypeStruct((t, d), BF16),
                   jax.ShapeDtypeStruct((1, d), F32), jax.ShapeDtypeStruct((1, d), F32)],
        compiler_params=_params(("arbitrary",)),
    )(x2, y2, target, w, gt)


def _resid_bwd(dout, y, w, gt, name):
    t, d = y.shape
    tb = _tile(t, 256, SUBLANES)

    def body(do_ref, y_ref, w_ref, gt_ref, dy_ref, dgt_ref, dw_ref):
        @pl.when(pl.program_id(0) == 0)
        def _():
            dgt_ref[...] = jnp.zeros_like(dgt_ref)
            dw_ref[...] = jnp.zeros_like(dw_ref)

        yv, g = y_ref[...], do_ref[...]
        r = _rms_scale(yv)
        yn = yv * r
        wv, gtv = w_ref[...], gt_ref[...]
        gyn = jnp.sum(g * yn, axis=0, keepdims=True)
        dgt_ref[...] += gyn * wv
        dw_ref[...] += gyn * gtv
        dyn = g * (gtv * wv)
        dy_ref[...] = (r * (dyn - yn * jnp.mean(dyn * yn, axis=-1, keepdims=True))).astype(dy_ref.dtype)

    vec_out = pl.BlockSpec((1, d), lambda i: (0, 0))
    return pl.pallas_call(
        body, name=name, grid=(t // tb,), in_specs=_row_specs(tb, d, 2, 2),
        out_specs=[pl.BlockSpec((tb, d), lambda i: (i, 0)), vec_out, vec_out],
        out_shape=[jax.ShapeDtypeStruct((t, d), BF16)] + [jax.ShapeDtypeStruct((1, d), F32)] * 2,
        compiler_params=_params(("arbitrary",)),
    )(dout, y, w, gt)


def _row_half_to_bf16(full, which, sib, name):
    n, r, c = full.shape
    by_rows = r % 32 == 0
    r, c = (r // 2, c) if by_rows else (r, c // 2)
    tr, tc = _elementwise_tiles(r, c)
    nbh = (r // tr) if by_rows else (c // tc)

    def body(which_ref, a_ref, *rest):
        if sib is None:
            rest[0][...] = a_ref[...].astype(BF16)
        else:
            rest[1][...] = (a_ref[...].astype(F32) + rest[0][...].astype(F32)).astype(BF16)

    if by_rows:
        half_spec = pl.BlockSpec((1, tr, tc), lambda j, i, k, which_ref: (j, which_ref[0] * nbh + i, k))
    else:
        half_spec = pl.BlockSpec((1, tr, tc), lambda j, i, k, which_ref: (j, i, which_ref[0] * nbh + k))
    spec = pl.BlockSpec((1, tr, tc), lambda j, i, k, which_ref: (j, i, k))
    grid_spec = pltpu.PrefetchScalarGridSpec(
        num_scalar_prefetch=1, grid=(n, r // tr, c // tc), in_specs=[half_spec] + ([] if sib is None else [spec]), out_specs=spec)
    return pl.pallas_call(
        body, name=name, grid_spec=grid_spec, out_shape=jax.ShapeDtypeStruct((n, r, c), BF16),
        compiler_params=_params(("parallel", "parallel", "parallel")),
    )(which, full, *([] if sib is None else [sib]))


def _sum_chips(recv, name):
    _, r, c = recv.shape
    tr, tc = _elementwise_tiles(r, c)

    def body(x_ref, o_ref):
        acc = x_ref[0].astype(F32)
        for j in range(1, N_CHIP):
            acc = acc + x_ref[j].astype(F32)
        o_ref[...] = acc

    return pl.pallas_call(
        body, name=name, grid=(r // tr, c // tc), in_specs=[pl.BlockSpec((N_CHIP, tr, tc), lambda i, j: (0, i, j))],
        out_specs=pl.BlockSpec((tr, tc), lambda i, j: (i, j)), out_shape=jax.ShapeDtypeStruct((r, c), F32),
        compiler_params=_params(("parallel", "parallel")),
    )(recv)


def _adamw(w, g_parts, m, v, name, by_core=False, outer=False):
    r, c = w.shape
    by_rows = r % 32 == 0
    if by_core:
        tr, tc = _elementwise_tiles(*((r // 2, c) if by_rows else (r, c // 2)))
        nbh = (r // 2) // tr if by_rows else (c // 2) // tc
    else:
        tr, tc = _elementwise_tiles(r, c)
    n_g = len(g_parts)
    c1 = 1.0 / (1.0 - ADAM_B1 ** ADAM_STEP)
    c2 = 1.0 / (1.0 - ADAM_B2 ** ADAM_STEP)

    def body(*refs):
        w_ref, g_refs, m_ref, v_ref = refs[0], refs[1:1 + n_g], refs[1 + n_g], refs[2 + n_g]
        g_out, d_out, m_out, v_out = refs[3 + n_g:]
        if by_core:
            in_my_half = (pl.program_id(0 if by_rows else 1) // nbh) == lax.axis_index("c")
            g = jnp.where(in_my_half, g_refs[0][...], g_refs[1][...])
        elif outer:
            g = lax.dot_general(_silu(g_refs[0][...]), g_refs[1][...], TN_DIMS, precision=HI, preferred_element_type=F32)
        else:
            g = g_refs[0][...]
            for extra in g_refs[1:]:
                g = g + extra[...]
        mn = ADAM_B1 * m_ref[...] + (1.0 - ADAM_B1) * g
        vn = ADAM_B2 * v_ref[...] + (1.0 - ADAM_B2) * (g * g)
        g_out[...] = g
        m_out[...] = mn
        v_out[...] = vn
        d_out[...] = -ADAM_LR * ((mn * c1) / (jnp.sqrt(vn * c2) + ADAM_EPS) + ADAM_WD * w_ref[...])

    spec = pl.BlockSpec((tr, tc), lambda i, j: (i, j))
    if by_core:
        g_spec = pl.BlockSpec((tr, tc), (lambda i, j: (i % nbh, j)) if by_rows else (lambda i, j: (i, j % nbh)))
    else:
        g_spec = spec
    g_specs = [g_spec] * n_g
    if outer:
        g_specs = [pl.BlockSpec((N_DEV, tr), lambda i, j: (0, i)), pl.BlockSpec((N_DEV, tc), lambda i, j: (0, j))]
    return pl.pallas_call(
        body, name=name, grid=(r // tr, c // tc), in_specs=[spec] + g_specs + [spec] * 2, out_specs=[spec] * 4,
        out_shape=[jax.ShapeDtypeStruct((r, c), F32)] * 4, compiler_params=_params(("parallel", "parallel")),
    )(w, *g_parts, m, v)


def _conv_taps(u, t):
    rows = lax.broadcasted_iota(jnp.int32, u.shape, 0)
    return [u] + [jnp.where(rows >= dd, pltpu.roll(u, dd, 0), 0.0) for dd in range(1, CONV_K)]


def _conv_fwd(proj, conv_w, col0, name):
    t = proj.shape[0]
    ch = conv_w.shape[1]

    def body(u_ref, w_ref, o_ref):
        taps = _conv_taps(u_ref[...], t)
        wv = w_ref[...]
        y = taps[0] * wv[CONV_K - 1:CONV_K]
        for dd in range(1, CONV_K):
            y = y + taps[dd] * wv[CONV_K - 1 - dd:CONV_K - dd]
        o_ref[...] = _silu(y)

    return pl.pallas_call(
        body, name=name, grid=(ch // LANES,),
        in_specs=[pl.BlockSpec((t, LANES), lambda j: (0, col0 + j)), pl.BlockSpec((CONV_K, LANES), lambda j: (0, j))],
        out_specs=pl.BlockSpec((t, LANES), lambda j: (0, j)), out_shape=jax.ShapeDtypeStruct((t, ch), F32),
        compiler_params=_params(("parallel",)),
    )(proj, conv_w)


def _conv_bwd(proj, conv_w, ds, col0, name):
    t = proj.shape[0]
    ch = conv_w.shape[1]

    def body(u_ref, w_ref, ds_ref, du_ref, dw_ref):
        u = u_ref[...]
        taps = _conv_taps(u, t)
        wv = w_ref[...]
        y = taps[0] * wv[CONV_K - 1:CONV_K]
        for dd in range(1, CONV_K):
            y = y + taps[dd] * wv[CONV_K - 1 - dd:CONV_K - dd]
        sg = jax.nn.sigmoid(y)
        dy = ds_ref[...] * (sg * (1.0 + y * (1.0 - sg)))
        rows = lax.broadcasted_iota(jnp.int32, u.shape, 0)
        du = dy * wv[CONV_K - 1:CONV_K]
        for dd in range(1, CONV_K):
            ahead = jnp.where(rows < t - dd, pltpu.roll(dy, t - dd, 0), 0.0)
            du = du + ahead * wv[CONV_K - 1 - dd:CONV_K - dd]
        du_ref[...] = du.astype(du_ref.dtype)
        dws = [jnp.sum(dy * taps[CONV_K - 1 - j], axis=0, keepdims=True) for j in range(CONV_K)]
        dw_ref[...] = jnp.concatenate(dws, axis=0)

    return pl.pallas_call(
        body, name=name, grid=(ch // LANES,),
        in_specs=[pl.BlockSpec((t, LANES), lambda j: (0, col0 + j)), pl.BlockSpec((CONV_K, LANES), lambda j: (0, j)),
                  pl.BlockSpec((t, LANES), lambda j: (0, j))],
        out_specs=[pl.BlockSpec((t, LANES), lambda j: (0, j)), pl.BlockSpec((CONV_K, LANES), lambda j: (0, j))],
        out_shape=[jax.ShapeDtypeStruct((t, ch), BF16), jax.ShapeDtypeStruct((CONV_K, ch), F32)],
        compiler_params=_params(("parallel",)),
    )(proj, conv_w, ds)


def _hg_block(st, q, fl, vi, g, l0, l1, nw):
    hs = range(len(st))
    tb = q[0].shape[0]
    ln = HG_SUB
    lb = [jax.nn.sigmoid(l0[h] - l1[h]) for h in hs]
    rows = lax.broadcasted_iota(jnp.int32, (ln, HEAD), 0)
    tri = (lax.broadcasted_iota(jnp.int32, (ln, ln), 0) >= lax.broadcasted_iota(jnp.int32, (ln, ln), 1)).astype(F32)
    st = list(st)
    outs = [[] for _ in hs]
    for i in range(tb // ln):
        sl = slice(i * ln, (i + 1) * ln)
        qs, vs = [q[h][sl] for h in hs], [vi[h][sl] for h in hs]
        f = [lb[h] + (1.0 - lb[h]) * jax.nn.sigmoid(fl[h][sl]) for h in hs]
        k = [1.0 - f[h] for h in hs]
        b = [jnp.dot(tri, jnp.log(f[h]), precision=HI, preferred_element_type=F32) for h in hs]
        o = [lax.dot_general((qs[h] * jnp.exp(b[h])).astype(BF16), st[h].astype(BF16), NT_DIMS, preferred_element_type=F32)
             for h in hs]
        n_tiles = ln // SUBLANES
        acc = [o] + [[jnp.zeros((ln - SUBLANES * ti, HEAD), F32) for _ in hs] for ti in range(1, n_tiles)]
        for s in range(ln):
            ti = s // SUBLANES
            r0 = ti * SUBLANES
            e = [jnp.exp(jnp.where(rows[r0:] >= s, b[h][r0:] - b[h][s:s + 1], -1e30)) for h in hs]
            a = [jnp.sum(qs[h][r0:] * e[h] * k[h][s:s + 1], axis=-1, keepdims=True) for h in hs]
            acc[ti] = [acc[ti][h] + a[h] * vs[h][s:s + 1] for h in hs]
        o = [jnp.concatenate([sum(acc[ti][h][(j - ti) * SUBLANES:(j - ti + 1) * SUBLANES] for ti in range(j + 1))
                              for j in range(n_tiles)], axis=0) for h in hs]
        kt = [k[h] * jnp.exp(b[h][ln - 1:ln] - b[h]) for h in hs]
        upd = [lax.dot_general(vs[h].astype(BF16), kt[h].astype(BF16), TN_DIMS, preferred_element_type=F32) for h in hs]
        st = [st[h] * jnp.exp(b[h][ln - 1:ln]) + upd[h] for h in hs]
        for h in hs:
            outs[h].append(o[h])
    o = [jnp.concatenate(outs[h], axis=0) for h in hs]
    out = [o[h] * _rms_scale(o[h]) * nw * _silu(g[h]) for h in hs]
    return st, out


def _head_cols(h):
    return slice(h * HEAD, (h + 1) * HEAD)


def _head_groups(n_heads, group):
    g = min(group, n_heads)
    return [list(range(i, min(i + g, n_heads))) for i in range(0, n_heads, g)]


def _hg_in_specs(n_heads, tb, time_index):
    hw = n_heads * HEAD
    cols = [pl.BlockSpec((tb, hw), functools.partial(lambda part, j: (time_index(j), part), part)) for part in range(4)]
    head_rows = pl.BlockSpec((n_heads, 1, HEAD), lambda j: (0, 0, 0))
    return cols + [head_rows, head_rows, pl.BlockSpec((1, HEAD), lambda j: (0, 0))]


def _hgrn2_fwd(proj, l0, l1, nw, n_heads, name):
    t = proj.shape[0]
    hw = n_heads * HEAD
    tb = _tile(t, HG_BLOCK, HG_SUB)
    nb = t // tb

    def body(q_ref, f_ref, i_ref, g_ref, l0_ref, l1_ref, nw_ref, o_ref, save_ref, st_ref):
        @pl.when(pl.program_id(0) == 0)
        def _():
            st_ref[...] = jnp.zeros_like(st_ref)

        for hs in _head_groups(n_heads, HG_FWD_GROUP):
            st = [st_ref[h] for h in hs]
            for h, s in zip(hs, st):
                save_ref[h] = s
            st, out = _hg_block(st, *[[r[:, _head_cols(h)] for h in hs] for r in (q_ref, f_ref, i_ref, g_ref)],
                                [l0_ref[h] for h in hs], [l1_ref[h] for h in hs], nw_ref[...])
            for h, s, o in zip(hs, st, out):
                st_ref[h] = s
                o_ref[:, _head_cols(h)] = o.astype(o_ref.dtype)

    return pl.pallas_call(
        body, name=name, grid=(nb,), in_specs=_hg_in_specs(n_heads, tb, lambda j: j),
        out_specs=[pl.BlockSpec((tb, hw), lambda j: (j, 0)),
                   pl.BlockSpec((None, n_heads, HEAD, HEAD), lambda j: (j, 0, 0, 0))],
        out_shape=[jax.ShapeDtypeStruct((t, hw), BF16), jax.ShapeDtypeStruct((nb, n_heads, HEAD, HEAD), F32)],
        scratch_shapes=[pltpu.VMEM((n_heads, HEAD, HEAD), F32)], compiler_params=_params(("arbitrary",)),
    )(proj, proj, proj, proj, l0, l1, nw)


def _hgrn2_bwd(proj, l0, l1, nw, saved, d_ocat, n_heads, name):
    t = proj.shape[0]
    tb = _tile(t, HG_BLOCK, HG_SUB)
    nb = t // tb
    rev = lambda j: nb - 1 - j

    hw = n_heads * HEAD

    def body(q_ref, f_ref, i_ref, g_ref, l0_ref, l1_ref, nw_ref, save_ref, do_ref,
             dp_ref, dl0_ref, dl1_ref, dnw_ref, dst_ref):
        @pl.when(pl.program_id(0) == 0)
        def _():
            dst_ref[...] = jnp.zeros_like(dst_ref)
            dl0_ref[...] = jnp.zeros_like(dl0_ref)
            dl1_ref[...] = jnp.zeros_like(dl1_ref)
            dnw_ref[...] = jnp.zeros_like(dnw_ref)

        dnw_acc = jnp.zeros((1, HEAD), F32)
        for hs in _head_groups(n_heads, HG_BWD_GROUP):
            _, vjp = jax.vjp(_hg_block, [save_ref[h] for h in hs],
                             *[[r[:, _head_cols(h)] for h in hs] for r in (q_ref, f_ref, i_ref, g_ref)],
                             [l0_ref[h] for h in hs], [l1_ref[h] for h in hs], nw_ref[...])
            dst, dq, df, di, dg, dl0, dl1, dnw = vjp(([dst_ref[h] for h in hs], [do_ref[:, _head_cols(h)] for h in hs]))
            for i, h in enumerate(hs):
                dst_ref[h] = dst[i]
                for part, val in enumerate((dq, df, di, dg)):
                    dp_ref[:, part * hw + h * HEAD:part * hw + (h + 1) * HEAD] = val[i].astype(dp_ref.dtype)
                dl0_ref[h] += dl0[i]
                dl1_ref[h] += dl1[i]
            dnw_acc = dnw_acc + dnw
        dnw_ref[...] += dnw_acc

    head_rows = pl.BlockSpec((n_heads, 1, HEAD), lambda j: (0, 0, 0))
    return pl.pallas_call(
        body, name=name, grid=(nb,),
        in_specs=_hg_in_specs(n_heads, tb, rev) + [pl.BlockSpec((None, n_heads, HEAD, HEAD), lambda j: (rev(j), 0, 0, 0)),
                                                   pl.BlockSpec((tb, hw), lambda j: (rev(j), 0))],
        out_specs=[pl.BlockSpec((tb, 4 * hw), lambda j: (rev(j), 0)), head_rows, head_rows,
                   pl.BlockSpec((1, HEAD), lambda j: (0, 0))],
        out_shape=[jax.ShapeDtypeStruct((t, 4 * hw), BF16)] + [jax.ShapeDtypeStruct((n_heads, 1, HEAD), F32)] * 2
        + [jax.ShapeDtypeStruct((1, HEAD), F32)],
        scratch_shapes=[pltpu.VMEM((n_heads, HEAD, HEAD), F32)], compiler_params=_params(("arbitrary",)),
    )(proj, proj, proj, proj, l0, l1, nw, saved, d_ocat)


NN_DIMS = (((1,), (0,)), ((), ()))


def _split_bf16(x):
    hi = x.astype(BF16)
    return hi, (x - hi.astype(F32)).astype(BF16)


def _dot3(a, b, dims=NN_DIMS):
    (ah, al), (bh, bl) = _split_bf16(a), _split_bf16(b)
    dot = functools.partial(lax.dot_general, dimension_numbers=dims, preferred_element_type=F32)
    return dot(ah, bh) + dot(ah, bl) + dot(al, bh)


@jax.custom_vjp
def _mm3(a, b):
    return _dot3(a, b)


def _mm3_fwd(a, b):
    return _dot3(a, b), (a, b)


def _mm3_bwd(res, g):
    a, b = res
    return _dot3(g, b, NT_DIMS), _dot3(a, g, TN_DIMS)


_mm3.defvjp(_mm3_fwd, _mm3_bwd)


def _ones_dot(tri, x, dims=NN_DIMS):
    xh, xl = _split_bf16(x)
    dot = functools.partial(lax.dot_general, dimension_numbers=dims, preferred_element_type=F32)
    return dot(tri.astype(BF16), xh) + dot(tri.astype(BF16), xl)


@jax.custom_vjp
def _ones_mm(tri, x):
    return _ones_dot(tri, x)


def _ones_mm_fwd(tri, x):
    return _ones_dot(tri, x), tri


def _ones_mm_bwd(tri, g):
    return jnp.zeros_like(tri), _ones_dot(tri, g, TN_DIMS)


_ones_mm.defvjp(_ones_mm_fwd, _ones_mm_bwd)


def _dot_bf16(a, b, dims=(((1,), (0,)), ((), ()))):
    return lax.dot_general(a.astype(BF16), b.astype(BF16), dims, preferred_element_type=F32)


def _inv_unit_lower_raw(ms):
    hs = range(len(ms))
    c = ms[0].shape[0]
    r = lax.broadcasted_iota(jnp.int32, (c, c), 0)
    q = lax.broadcasted_iota(jnp.int32, (c, c), 1)
    eye = (r == q).astype(F32)
    md = [jnp.where((r // GDN_INV_BLOCK) == (q // GDN_INV_BLOCK), ms[h], 0.0) for h in hs]
    p = [-md[h] for h in hs]
    t16 = [eye + p[h] for h in hs]
    for _ in range(int(math.log2(GDN_INV_BLOCK)) - 1):
        p = [_dot3(p[h], p[h]) for h in hs]
        t16 = [t16[h] + _dot3(t16[h], p[h]) for h in hs]
    p = [-_dot3(t16[h], ms[h] - md[h]) for h in hs]
    t2 = [eye + p[h] for h in hs]
    for _ in range(int(math.log2(c // GDN_INV_BLOCK)) - 1):
        p = [_dot3(p[h], p[h]) for h in hs]
        t2 = [t2[h] + _dot3(t2[h], p[h]) for h in hs]
    return [_dot3(t2[h], t16[h]) for h in hs]


@jax.custom_vjp
def _inv_unit_lower(ms):
    return _inv_unit_lower_raw(ms)


def _inv_fwd(ms):
    ts = _inv_unit_lower_raw(ms)
    return ts, ts


def _inv_bwd(ts, dts):
    hs = range(len(ts))
    inner = [_dot3(ts[h], dts[h], TN_DIMS) for h in hs]
    return ([-_dot3(inner[h], ts[h], NT_DIMS) for h in hs],)


_inv_unit_lower.defvjp(_inv_fwd, _inv_bwd)


def _gdn_block(precise, onehots, st, qc, kc, vc, g, ab, alog_row, dtb_row, nw):
    inverse, dot3, ones_dot = precise
    hs = range(len(st))
    c = qc[0].shape[0]
    lane_sum = lambda v: jnp.sum(v, axis=-1, keepdims=True)
    a = [lane_sum(ab * onehots[h][0]) for h in hs]
    bb = [lane_sum(ab * onehots[h][1]) for h in hs]
    alog = [lane_sum(alog_row * onehots[h][0]) for h in hs]
    dtb = [lane_sum(dtb_row * onehots[h][0]) for h in hs]
    la = [-jnp.exp(alog[h]) * _softplus(a[h] + dtb[h]) for h in hs]
    beta = [jax.nn.sigmoid(bb[h]) for h in hs]
    q = [qc[h] * lax.rsqrt(lane_sum(qc[h] * qc[h]) + EPS) * (HEAD ** -0.5) for h in hs]
    k = [kc[h] * lax.rsqrt(lane_sum(kc[h] * kc[h]) + EPS) for h in hs]
    r = lax.broadcasted_iota(jnp.int32, (c, c), 0)
    s = lax.broadcasted_iota(jnp.int32, (c, c), 1)
    tri = (r >= s).astype(F32)
    g_cc = [ones_dot(tri, jnp.broadcast_to(la[h], (c, c))) for h in hs]
    g_cl = [ones_dot(tri, jnp.broadcast_to(la[h], (c, HEAD))) for h in hs]
    gamma = [jnp.exp(jnp.where(r >= s, g_cc[h] - g_cc[h].T, -1e30)) for h in hs]
    kk = [_dot_bf16(k[h], k[h], NT_DIMS) for h in hs]
    m = [jnp.where(r > s, beta[h] * kk[h] * gamma[h], 0.0) for h in hs]
    tm = inverse(m)
    eg = [jnp.exp(g_cl[h]) for h in hs]
    rhs = [jnp.concatenate([vc[h] * beta[h], k[h] * (beta[h] * eg[h])], axis=1) for h in hs]
    sol = [dot3(tm[h], rhs[h]) for h in hs]
    qk = [_dot_bf16(q[h], k[h], NT_DIMS) * gamma[h] for h in hs]
    g_last = [g_cl[h][c - 1:c] for h in hs]
    k_tail = [k[h] * jnp.exp(g_last[h] - g_cl[h]) for h in hs]
    v_new = [sol[h][:, :HEAD] - _dot_bf16(sol[h][:, HEAD:], st[h], NT_DIMS) for h in hs]
    o_st = [_dot_bf16(q[h] * eg[h], st[h], NT_DIMS) for h in hs]
    o = [o_st[h] + _dot_bf16(qk[h], v_new[h]) for h in hs]
    upd = [_dot_bf16(v_new[h], k_tail[h], TN_DIMS) for h in hs]
    st = [st[h] * jnp.exp(g_last[h]) + upd[h] for h in hs]
    out = [o[h] * _rms_scale(o[h]) * nw * _silu(g[h]) for h in hs]
    return st, out


def _head_onehots(n_heads, h):
    lane = lax.broadcasted_iota(jnp.int32, (1, LANES), 1)
    return (lane == h).astype(F32), (lane == n_heads + h).astype(F32)


def _gdn_in_specs(n_heads, c, time_index):
    hw = n_heads * HEAD
    qkv = [pl.BlockSpec((c, hw), functools.partial(lambda part, j: (time_index(j), part), part)) for part in range(3)]
    row = pl.BlockSpec((1, LANES), lambda j: (0, 0))
    return qkv + [pl.BlockSpec((c, hw), lambda j: (time_index(j), 7)),
                  pl.BlockSpec((c, LANES), lambda j: (time_index(j), 8 * n_heads)), row, row, row]


def _gdn_fwd(qkv, proj, alog_row, dtb_row, nw, n_heads, name):
    t = qkv.shape[0]
    hw = n_heads * HEAD
    c = _tile(t, GDN_CHUNK, GDN_CHUNK)
    nb = t // c

    def body(q_ref, k_ref, v_ref, g_ref, ab_ref, al_ref, dt_ref, nw_ref, o_ref, save_ref, st_ref):
        @pl.when(pl.program_id(0) == 0)
        def _():
            st_ref[...] = jnp.zeros_like(st_ref)

        for hs in _head_groups(n_heads, GDN_FWD_GROUP):
            st = [st_ref[h] for h in hs]
            for h, s in zip(hs, st):
                save_ref[h] = s
            st, out = _gdn_block((_inv_unit_lower_raw, _dot3, _ones_dot), [_head_onehots(n_heads, h) for h in hs], st,
                                 *[[r[:, _head_cols(h)] for h in hs] for r in (q_ref, k_ref, v_ref, g_ref)],
                                 ab_ref[...], al_ref[...], dt_ref[...], nw_ref[...])
            for h, s, o in zip(hs, st, out):
                st_ref[h] = s
                o_ref[:, _head_cols(h)] = o.astype(o_ref.dtype)

    return pl.pallas_call(
        body, name=name, grid=(nb,), in_specs=_gdn_in_specs(n_heads, c, lambda j: j),
        out_specs=[pl.BlockSpec((c, hw), lambda j: (j, 0)),
                   pl.BlockSpec((None, n_heads, HEAD, HEAD), lambda j: (j, 0, 0, 0))],
        out_shape=[jax.ShapeDtypeStruct((t, hw), BF16), jax.ShapeDtypeStruct((nb, n_heads, HEAD, HEAD), F32)],
        scratch_shapes=[pltpu.VMEM((n_heads, HEAD, HEAD), F32)], compiler_params=_params(("arbitrary",)),
    )(qkv, qkv, qkv, proj, proj, alog_row, dtb_row, nw)


def _gdn_bwd(qkv, proj, alog_row, dtb_row, nw, saved, d_ocat, n_heads, name):
    t = qkv.shape[0]
    c = _tile(t, GDN_CHUNK, GDN_CHUNK)
    nb = t // c
    rev = lambda j: nb - 1 - j

    hw = n_heads * HEAD

    def body(q_ref, k_ref, v_ref, g_ref, ab_ref, al_ref, dt_ref, nw_ref, save_ref, do_ref,
             dqkv_ref, dg_ref, dab_ref, dal_ref, ddt_ref, dnw_ref, dst_ref):
        @pl.when(pl.program_id(0) == 0)
        def _():
            dst_ref[...] = jnp.zeros_like(dst_ref)
            dal_ref[...] = jnp.zeros_like(dal_ref)
            ddt_ref[...] = jnp.zeros_like(ddt_ref)
            dnw_ref[...] = jnp.zeros_like(dnw_ref)

        dab_acc = jnp.zeros((c, LANES), F32)
        row_acc = [jnp.zeros((1, LANES), F32)] * 3
        for hs in _head_groups(n_heads, GDN_BWD_GROUP):
            fn = functools.partial(_gdn_block, (_inv_unit_lower, _mm3, _ones_mm), [_head_onehots(n_heads, h) for h in hs])
            _, vjp = jax.vjp(fn, [save_ref[h] for h in hs],
                             *[[r[:, _head_cols(h)] for h in hs] for r in (q_ref, k_ref, v_ref, g_ref)],
                             ab_ref[...], al_ref[...], dt_ref[...], nw_ref[...])
            dst, dq, dk, dv, dg, dab, dal, ddt, dnw = vjp(([dst_ref[h] for h in hs], [do_ref[:, _head_cols(h)] for h in hs]))
            for i, h in enumerate(hs):
                dst_ref[h] = dst[i]
                for part, val in enumerate((dq, dk, dv)):
                    dqkv_ref[:, part * hw + h * HEAD:part * hw + (h + 1) * HEAD] = val[i]
                dg_ref[:, _head_cols(h)] = dg[i].astype(dg_ref.dtype)
            dab_acc = dab_acc + dab
            row_acc = [acc + val for acc, val in zip(row_acc, (dal, ddt, dnw))]
        dab_ref[...] = dab_acc
        dal_ref[...] += row_acc[0]
        ddt_ref[...] += row_acc[1]
        dnw_ref[...] += row_acc[2]

    row = pl.BlockSpec((1, LANES), lambda j: (0, 0))
    return pl.pallas_call(
        body, name=name, grid=(nb,),
        in_specs=_gdn_in_specs(n_heads, c, rev) + [pl.BlockSpec((None, n_heads, HEAD, HEAD), lambda j: (rev(j), 0, 0, 0)),
                                                   pl.BlockSpec((c, hw), lambda j: (rev(j), 1))],
        out_specs=[pl.BlockSpec((c, 3 * hw), lambda j: (rev(j), 0)), pl.BlockSpec((c, hw), lambda j: (rev(j), 0)),
                   pl.BlockSpec((c, LANES), lambda j: (rev(j), 0)), row, row, row],
        out_shape=[jax.ShapeDtypeStruct((t, 3 * hw), F32), jax.ShapeDtypeStruct((t, hw), BF16),
                   jax.ShapeDtypeStruct((t, LANES), F32)] + [jax.ShapeDtypeStruct((1, LANES), F32)] * 3,
        scratch_shapes=[pltpu.VMEM((n_heads, HEAD, HEAD), F32)], compiler_params=_params(("arbitrary",)),
    )(qkv, qkv, qkv, proj, proj, alog_row, dtb_row, nw, saved, d_ocat)


def _pad_lanes(v, n):
    v = v.reshape(1, -1)
    return jnp.pad(v, ((0, 0), (0, n - v.shape[1])))


def _pack_rows(vecs):
    flat = jnp.concatenate([v.reshape(-1) for v in vecs])
    offs, o = [], 0
    for v in vecs:
        offs.append((o, v.size))
        o += v.size
    per_row = -(-o // (SUBLANES * LANES)) * LANES
    flat = jnp.pad(flat, (0, SUBLANES * per_row - o))
    return flat.reshape(SUBLANES, per_row), offs


def _unpack(gathered, offs):
    per_dev = gathered.reshape(N_DEV, -1)
    return [per_dev[:, o:o + n] for o, n in offs]


def _sum_devices(part):
    acc = part[0]
    for i in range(1, N_DEV):
        acc = acc + part[i]
    return acc


def kernel(x, c, w_ada, b_ada, pre_mix_norm, post_mix_norm, pre_ffn_norm, post_ffn_norm, w_in, hg_lb_logits, hg_norm, gdn_conv_w, gdn_a_log, gdn_dt_bias, gdn_norm, w_out, w_ff1, w_ff2, loss_target, m_w_ada, m_b_ada, m_pre_mix_norm, m_post_mix_norm, m_pre_ffn_norm, m_post_ffn_norm, m_w_in, m_hg_lb_logits, m_hg_norm, m_gdn_conv_w, m_gdn_a_log, m_gdn_dt_bias, m_gdn_norm, m_w_out, m_w_ff1, m_w_ff2, v_w_ada, v_b_ada, v_pre_mix_norm, v_post_mix_norm, v_pre_ffn_norm, v_post_ffn_norm, v_w_in, v_hg_lb_logits, v_hg_norm, v_gdn_conv_w, v_gdn_a_log, v_gdn_dt_bias, v_gdn_norm, v_w_out, v_w_ff1, v_w_ff2):
    assert x.shape[0] == 1 and w_ada.shape[0] == 1 and hg_lb_logits.shape[0] == 2
    t, d = x.shape[1], x.shape[2]
    n_heads = (d // 2) // HEAD
    hw = n_heads * HEAD
    in_cols = 8 * hw + 2 * n_heads
    np_cols = 8 * hw + 2 * LANES
    d_ff = w_ff1.shape[2] * N_CHIP
    na = w_ada.shape[2]
    ax, ay, ac = lax.axis_index("x"), lax.axis_index("y"), lax.axis_index("c")
    chip = 2 * ax + ay
    dev = 4 * ax + 2 * ay + ac

    x2d, tgt = x[0], loss_target[0]

    pack1, offs1 = _pack_rows([c[0], gdn_conv_w[0]])
    c_all, convw_all = _unpack(_gather8(pack1, "gather_cond"), offs1)
    conv_sh = gdn_conv_w.shape[2]
    conv_w = jnp.concatenate([convw_all[2 * j].reshape(CONV_K, conv_sh) for j in range(N_CHIP)], axis=1)

    b_s = lax.dynamic_slice(b_ada, (0, chip * na), (1, na))
    mod_part = _mod_part(c_all, w_ada[0], b_s, "mod_part")
    pack2, offs2 = _pack_rows([mod_part])
    (mod_parts,) = _unpack(_gather8(pack2, "gather_mod"), offs2)
    mod_all = jnp.concatenate([mod_parts[2 * j].reshape(N_DEV, na) for j in range(N_CHIP)], axis=1)
    mod = lax.dynamic_slice(mod_all, (dev, 0), (1, N_MOD * d))
    sh_m, sc_m, gt_m, sh_f, sc_f, gt_f = [mod[:, i * d:(i + 1) * d] for i in range(N_MOD)]

    h1 = _norm_mod(x2d, pre_mix_norm, sc_m, sh_m, "norm_mod_mix")
    (g_in,), h1 = _gather_weights([jnp.transpose(w_in[0]).astype(BF16)], "gather_w_in", sequencer_id=5, after=h1)
    late, g_in = lax.optimization_barrier(([w_out[0].astype(BF16), w_ff1[0].astype(BF16), w_ff2[0].astype(BF16)], g_in))
    g_out, g_ff1, g_ff2 = _gather_weights(late, "gather_weights_late", sequencer_id=1)
    w_in_f = jnp.pad(g_in.reshape(in_cols, d), ((0, np_cols - in_cols), (0, 0)))
    w_out_f = g_out.reshape(d, d)
    w_ff2_f = g_ff2.reshape(d_ff, d)

    proj = _matmul(h1, w_in_f, "nt", F32, "mm_in", tn=768)
    l0, l1 = hg_lb_logits[0].reshape(n_heads, 1, HEAD), hg_lb_logits[1].reshape(n_heads, 1, HEAD)
    o_hg, hg_saved = _hgrn2_fwd(proj, l0, l1, hg_norm, n_heads, "hgrn2_fwd")
    qkv = _conv_fwd(proj, conv_w, 4 * n_heads, "conv_fwd")
    alog_row, dtb_row = _pad_lanes(gdn_a_log, LANES), _pad_lanes(gdn_dt_bias, LANES)
    o_gdn, gdn_saved = _gdn_fwd(qkv, proj, alog_row, dtb_row, gdn_norm, n_heads, "gdn_fwd")
    o_cat = jnp.concatenate([o_hg, o_gdn], axis=1)
    y1 = _matmul(o_cat, w_out_f, "nn", F32, "mm_out")
    x_mid, h2 = _resid_norm_mod(x2d, y1, post_mix_norm, gt_m, pre_ffn_norm, sc_f, sh_f, "resid_mix_norm_mod_ffn")

    relu_a1, r1 = _matmul(h2, g_ff1, "nn", BF16, "mm_ff1", relu2=True, b_split=True)
    y2 = _matmul(r1, w_ff2_f, "nn", F32, "mm_ff2")
    d_out, loss_row, dy2, d_gt_f, d_post_ffn = _loss_head(x_mid, y2, post_ffn_norm, gt_f, tgt, "loss_head")

    in_sh = in_cols // N_CHIP
    ff_sh = d_ff // N_CHIP
    my_half = jnp.reshape(ac, (1,)).astype(jnp.int32)

    def start_reduce(by_chip, tag, collective_id):
        to_sib = [_row_half_to_bf16(a, 1 - my_half, None, f"sibling_half_{tag}{i}") for i, a in enumerate(by_chip)]
        from_sib = _sibling_exchange(to_sib, f"sibling_partials_{tag}")
        chip_part = [_row_half_to_bf16(a, my_half, s, f"add_halves_{tag}{i}") for i, (a, s) in enumerate(zip(by_chip, from_sib))]
        return _sequencer_chip_exchange(chip_part, f"scatter_grads_{tag}", collective_id)

    gw_ff2 = _matmul(r1, dy2, "tn", BF16, "mm_ff2_dw")
    gw_ff2, dy2 = lax.optimization_barrier((gw_ff2, dy2))
    da1 = _matmul(dy2, w_ff2_f, "nt", BF16, "mm_ff2_dx", times=relu_a1)
    gw_ff1 = _matmul(h2, da1, "tn", BF16, "mm_ff1_dw", out_split=True)
    gw_ff1, da1 = lax.optimization_barrier((gw_ff1, da1))
    recv_ff2, recv_ff1 = _sequencer_chip_exchange([gw_ff2.reshape(N_CHIP, ff_sh, d), gw_ff1], "scatter_grads_ff", 2)
    dh2 = _matmul(da1, g_ff1, "nt", BF16, "mm_ff1_dx", b_split=True)
    d_mid, d_pre_ffn, d_sc_f, d_sh_f = _norm_mod_bwd(x_mid, pre_ffn_norm, sc_f, dh2, d_out, "norm_mod_ffn_bwd")

    dy1, d_gt_m, d_post_mix = _resid_bwd(d_mid, y1, post_mix_norm, gt_m, "resid_mix_bwd")
    gw_out = _matmul(o_cat, dy1, "tn", BF16, "mm_out_dw")
    gw_out, dy1 = lax.optimization_barrier((gw_out, dy1))
    (recv_out,) = _sequencer_chip_exchange([gw_out.reshape(N_CHIP, d // N_CHIP, d)], "scatter_grads_out", 3)
    d_ocat = _matmul(dy1, w_out_f, "nt", F32, "mm_out_dx")
    dp_hg, dl0, dl1, d_hg_norm = _hgrn2_bwd(proj, l0, l1, hg_norm, hg_saved, d_ocat, n_heads, "hgrn2_bwd")
    dqkv, dg_g, dab, d_alog, d_dtb, d_gdn_norm = _gdn_bwd(
        qkv, proj, alog_row, dtb_row, gdn_norm, gdn_saved, d_ocat, n_heads, "gdn_bwd")
    du, d_conv_w = _conv_bwd(proj, conv_w, dqkv, 4 * n_heads, "conv_bwd")
    dproj = jnp.concatenate([dp_hg, du, dg_g, dab.astype(BF16), jnp.zeros((t, LANES), BF16)], axis=1)
    gw_in = _matmul(dproj, h1, "tn", BF16, "mm_in_dw", tm=768)
    (recv_in,) = start_reduce([gw_in[:in_cols].reshape(N_CHIP, in_sh, d)], "in", 4)
    dh1 = _matmul(dproj, w_in_f, "nn", BF16, "mm_in_dx", tk=2816)
    grad_x, d_pre_mix, d_sc_m, d_sh_m = _norm_mod_bwd(x2d, pre_mix_norm, sc_m, dh1, d_mid, "norm_mod_mix_bwd")

    d_mod = jnp.concatenate([d_sh_m, d_sc_m, d_gt_m, d_sh_f, d_sc_f, d_gt_f], axis=1)
    d_lb_logits = jnp.stack([dl0.reshape(n_heads, HEAD), dl1.reshape(n_heads, HEAD)])
    pack3, offs3 = _pack_rows([loss_row[0, :1], d_pre_mix, d_post_mix, d_pre_ffn, d_post_ffn, d_lb_logits, d_hg_norm,
                               d_conv_w, d_alog[0, :n_heads], d_dtb[0, :n_heads], d_gdn_norm, d_mod])
    parts = _unpack(_gather8(pack3, "gather_vec_grads"), offs3)
    sums = [_sum_devices(p) for p in parts[:-1]]
    loss = sums[0][0]
    dmod_all = parts[-1]
    g_b_ada = _sum_devices(dmod_all).reshape(1, N_MOD * d)
    g_conv_full = sums[7].reshape(CONV_K, N_CHIP * conv_sh)
    g_conv = lax.dynamic_slice(g_conv_full, (0, chip * conv_sh), (CONV_K, conv_sh))
    dmod_chip = lax.dynamic_slice(dmod_all, (0, chip * na), (N_DEV, na))

    sum_ff2 = _sum_chips(recv_ff2, "sum_chips_ff2")
    recv_ff1, sum_ff2 = lax.optimization_barrier((recv_ff1, sum_ff2))
    sum_ff1 = _sum_chips(recv_ff1, "sum_chips_ff1")
    recv_out, sum_ff1 = lax.optimization_barrier((recv_out, sum_ff1))
    sum_out = _sum_chips(recv_out, "sum_chips_out")
    recv_in, sum_out, grad_x = lax.optimization_barrier((recv_in, sum_out, grad_x))
    mine = [_sum_chips(recv_in, "sum_chips_in"), sum_out, sum_ff1, sum_ff2]
    theirs = _sibling_exchange(mine, "sibling_grads", sequencer_id=6)
    ada = _adamw(w_ada[0], [c_all, dmod_chip], m_w_ada[0], v_w_ada[0], "adamw_w_ada", outer=True)
    theirs, ada = lax.optimization_barrier((theirs, ada))

    big = {"w_ada": [o[None] for o in ada]}
    for i, (nm, w_, m_, v_) in enumerate([("w_in", w_in, m_w_in, v_w_in), ("w_out", w_out, m_w_out, v_w_out),
                                          ("w_ff1", w_ff1, m_w_ff1, v_w_ff1), ("w_ff2", w_ff2, m_w_ff2, v_w_ff2)]):
        if nm == "w_in":
            res_t = _adamw(jnp.transpose(w_[0]), [mine[i], theirs[i]], jnp.transpose(m_[0]), jnp.transpose(v_[0]),
                           f"adamw_{nm}", by_core=True)
            big[nm] = [jnp.transpose(o)[None] for o in res_t]
        else:
            big[nm] = [o[None] for o in _adamw(w_[0], [mine[i], theirs[i]], m_[0], v_[0], f"adamw_{nm}")]

    small_names = ["b_ada", "pre_mix_norm", "post_mix_norm", "pre_ffn_norm", "post_ffn_norm", "hg_lb_logits", "hg_norm",
                   "gdn_conv_w", "gdn_a_log", "gdn_dt_bias", "gdn_norm"]
    small_w = [b_ada, pre_mix_norm, post_mix_norm, pre_ffn_norm, post_ffn_norm, hg_lb_logits, hg_norm, gdn_conv_w,
               gdn_a_log, gdn_dt_bias, gdn_norm]
    small_m = [m_b_ada, m_pre_mix_norm, m_post_mix_norm, m_pre_ffn_norm, m_post_ffn_norm, m_hg_lb_logits, m_hg_norm,
               m_gdn_conv_w, m_gdn_a_log, m_gdn_dt_bias, m_gdn_norm]
    small_v = [v_b_ada, v_pre_mix_norm, v_post_mix_norm, v_pre_ffn_norm, v_post_ffn_norm, v_hg_lb_logits, v_hg_norm,
               v_gdn_conv_w, v_gdn_a_log, v_gdn_dt_bias, v_gdn_norm]
    small_g = [g_b_ada, sums[1], sums[2], sums[3], sums[4], sums[5], sums[6], g_conv, sums[8], sums[9], sums[10]]
    pw, offs_s = _pack_rows(small_w)
    pg, _ = _pack_rows(small_g)
    pm, _ = _pack_rows(small_m)
    pv, _ = _pack_rows(small_v)
    packed = _adamw(pw, [pg], pm, pv, "adamw_vectors")
    small = {}
    for nm, w_, (o, n) in zip(small_names, small_w, offs_s):
        small[nm] = [p.reshape(-1)[o:o + n].reshape(w_.shape) for p in packed]

    order = ["w_ada", "b_ada", "pre_mix_norm", "post_mix_norm", "pre_ffn_norm", "post_ffn_norm", "w_in", "hg_lb_logits",
             "hg_norm", "gdn_conv_w", "gdn_a_log", "gdn_dt_bias", "gdn_norm", "w_out", "w_ff1", "w_ff2"]
    res = {**big, **small}
    outs = [loss, grad_x[None]]
    for k in range(4):
        outs += [res[nm][k] for nm in order]
    return tuple(outs)
```
